```python
import jax, jax.numpy as jnp
from jax import lax
import numpy as np

D_MODEL = 1024
BATCH = 8
SEQ = 4096
DEPTH = 1

MIX_WIDTH = D_MODEL
C_CONV = MIX_WIDTH // 2
CONV_HEADS = 8
C_POOL = MIX_WIDTH - C_CONV
POOL_WINDOWS = (2, 4, 8, 16)
N_POOL_GROUPS = len(POOL_WINDOWS)
POOL_GROUP = C_POOL // N_POOL_GROUPS
CONV_K = 31
D_FF = ((8 * D_MODEL // 3 + 255) // 256) * 256
D_PLE = 256
EPS = 1e-6

kernel_name = "hybrid_conformer_conv_multiscale_pool_block"


def rmsnorm(x, g):
    xf = x.astype(jnp.float32)
    y = xf * lax.rsqrt(jnp.mean(xf * xf, axis=-1, keepdims=True) + EPS)
    return (y * g.astype(jnp.float32)).astype(x.dtype)


def layernorm(x, g, b):
    xf = x.astype(jnp.float32)
    mu = jnp.mean(xf, axis=-1, keepdims=True)
    var = jnp.mean(jnp.square(xf - mu), axis=-1, keepdims=True)
    y = (xf - mu) * lax.rsqrt(var + EPS)
    return (y * g.astype(jnp.float32) + b.astype(jnp.float32)).astype(x.dtype)


def causal_depthwise_conv(u, w, b):
    c = u.shape[-1]
    k = w.reshape(CONV_K, 1, c).astype(u.dtype)
    y = lax.conv_general_dilated(
        u, k, window_strides=(1,), padding=[(CONV_K - 1, 0)],
        dimension_numbers=("NWC", "WIO", "NWC"), feature_group_count=c)
    return y + b.astype(u.dtype)


def multiscale_pool(v, pool_w, pool_scale):
    s = v.shape[1]
    vf = v.astype(jnp.float32)
    groups = vf.reshape(vf.shape[0], s, N_POOL_GROUPS, POOL_GROUP)
    pos1 = jnp.arange(1, s + 1, dtype=jnp.float32)[None, :, None]
    outs = []
    for g, w in enumerate(POOL_WINDOWS):
        vg = groups[:, :, g, :]
        c = jnp.cumsum(vg, axis=1)
        c_shift = jnp.pad(c, ((0, 0), (w, 0), (0, 0)))[:, :s]
        cnt = jnp.minimum(pos1, float(w))
        outs.append((c - c_shift) / cnt - vg)
    pooled = jnp.stack(outs, axis=2)
    mixed = jnp.einsum("bsgc,gcd->bsgd", pooled, pool_w.astype(jnp.float32))
    mixed = mixed.reshape(vf.shape) * pool_scale.astype(jnp.float32)
    return mixed.astype(v.dtype)


def _fwd_setup_inputs(seed: int = 0) -> dict:
    key = jax.random.key(seed)
    ks = jax.random.split(key, 20)
    L, D = DEPTH, D_MODEL
    n = lambda k, shape, fan: jax.random.normal(k, shape, jnp.float32) * (fan ** -0.5)
    gain = lambda k, shape: 1.0 + 0.05 * jax.random.normal(k, shape, jnp.float32)
    return {
        "x": jax.random.normal(ks[0], (BATCH, SEQ, D), jnp.float32),
        "p": jax.random.normal(ks[1], (DEPTH, BATCH, SEQ, D_PLE), jnp.float32),
        "g_mix": gain(ks[2], (L, D)),
        "w_in": n(ks[3], (L, D, 2 * C_CONV + C_POOL), D),
        "conv_w": n(ks[4], (L, CONV_K, C_CONV), CONV_K),
        "conv_b": 0.02 * jax.random.normal(ks[5], (L, C_CONV), jnp.float32),
        "ln_g": gain(ks[6], (L, C_CONV)),
        "ln_b": 0.02 * jax.random.normal(ks[7], (L, C_CONV), jnp.float32),
        "pool_w": n(ks[8], (L, N_POOL_GROUPS, POOL_GROUP, POOL_GROUP), POOL_GROUP),
        "pool_scale": gain(ks[9], (L, C_POOL)),
        "w_out": n(ks[10], (L, MIX_WIDTH, D), MIX_WIDTH),
        "g_ffn": gain(ks[11], (L, D)),
        "w_gate_up": n(ks[12], (L, D, 2 * D_FF), D),
        "w_down": n(ks[13], (L, D_FF, D), D_FF),
        "g_ple_gate": gain(ks[14], (L, D)),
        "w_ple_gate": n(ks[15], (L, D, D), D),
        "w_ple_up": n(ks[16], (L, D_PLE, D), D_PLE),
        "g_ple_post": gain(ks[17], (L, D)),
        "g_final": gain(ks[18], (D,)),
    }


def _fwd_reference(x, p, g_mix, w_in, conv_w, conv_b, ln_g, ln_b, pool_w, pool_scale,
              w_out, g_ffn, w_gate_up, w_down, g_ple_gate, w_ple_gate, w_ple_up,
              g_ple_post, g_final):
    for i in range(DEPTH):
        h = rmsnorm(x, g_mix[i])
        z = h @ w_in[i]
        a = z[..., :C_CONV]
        b = z[..., C_CONV:2 * C_CONV]
        v = z[..., 2 * C_CONV:]
        u = a * jax.nn.sigmoid(b)
        u = causal_depthwise_conv(u, conv_w[i], conv_b[i])
        u = jax.nn.silu(layernorm(u, ln_g[i], ln_b[i]))
        q = multiscale_pool(v, pool_w[i], pool_scale[i])
        mix = jnp.concatenate([u, q], axis=-1)
        x = x + mix @ w_out[i]
        h = rmsnorm(x, g_ffn[i])
        gu = h @ w_gate_up[i]
        x = x + (jax.nn.silu(gu[..., :D_FF]) * gu[..., D_FF:]) @ w_down[i]
        gate = jax.nn.sigmoid(rmsnorm(x, g_ple_gate[i]) @ w_ple_gate[i])
        e = rmsnorm(p[i].astype(x.dtype) @ w_ple_up[i], g_ple_post[i])
        x = x + gate * e
    return rmsnorm(x, g_final)


import jax as _jax
import jax.numpy as _jnp

TWIN_FORMAT = 'train_step'
FWD_PARAMS = ['x', 'p', 'g_mix', 'w_in', 'conv_w', 'conv_b', 'ln_g', 'ln_b', 'pool_w', 'pool_scale', 'w_out', 'g_ffn', 'w_gate_up', 'w_down', 'g_ple_gate', 'w_ple_gate', 'w_ple_up', 'g_ple_post', 'g_final']
TWIN_WEIGHTS = ['g_mix', 'w_in', 'conv_w', 'conv_b', 'ln_g', 'ln_b', 'pool_w', 'pool_scale', 'w_out', 'g_ffn', 'w_gate_up', 'w_down', 'g_ple_gate', 'w_ple_gate', 'w_ple_up', 'g_ple_post', 'g_final']
TWIN_DIFF_INPUT = 'x'
TWIN_INPUTS = ['x', 'p', 'g_mix', 'w_in', 'conv_w', 'conv_b', 'ln_g', 'ln_b', 'pool_w', 'pool_scale', 'w_out', 'g_ffn', 'w_gate_up', 'w_down', 'g_ple_gate', 'w_ple_gate', 'w_ple_up', 'g_ple_post', 'g_final', 'loss_target', 'm_g_mix', 'm_w_in', 'm_conv_w', 'm_conv_b', 'm_ln_g', 'm_ln_b', 'm_pool_w', 'm_pool_scale', 'm_w_out', 'm_g_ffn', 'm_w_gate_up', 'm_w_down', 'm_g_ple_gate', 'm_w_ple_gate', 'm_w_ple_up', 'm_g_ple_post', 'm_g_final', 'v_g_mix', 'v_w_in', 'v_conv_w', 'v_conv_b', 'v_ln_g', 'v_ln_b', 'v_pool_w', 'v_pool_scale', 'v_w_out', 'v_g_ffn', 'v_w_gate_up', 'v_w_down', 'v_g_ple_gate', 'v_w_ple_gate', 'v_w_ple_up', 'v_g_ple_post', 'v_g_final']
TWIN_OUTPUTS = ['loss', 'grad_x', 'grad_g_mix', 'grad_w_in', 'grad_conv_w', 'grad_conv_b', 'grad_ln_g', 'grad_ln_b', 'grad_pool_w', 'grad_pool_scale', 'grad_w_out', 'grad_g_ffn', 'grad_w_gate_up', 'grad_w_down', 'grad_g_ple_gate', 'grad_w_ple_gate', 'grad_w_ple_up', 'grad_g_ple_post', 'grad_g_final', 'delta_g_mix', 'delta_w_in', 'delta_conv_w', 'delta_conv_b', 'delta_ln_g', 'delta_ln_b', 'delta_pool_w', 'delta_pool_scale', 'delta_w_out', 'delta_g_ffn', 'delta_w_gate_up', 'delta_w_down', 'delta_g_ple_gate', 'delta_w_ple_gate', 'delta_w_ple_up', 'delta_g_ple_post', 'delta_g_final', 'new_m_g_mix', 'new_m_w_in', 'new_m_conv_w', 'new_m_conv_b', 'new_m_ln_g', 'new_m_ln_b', 'new_m_pool_w', 'new_m_pool_scale', 'new_m_w_out', 'new_m_g_ffn', 'new_m_w_gate_up', 'new_m_w_down', 'new_m_g_ple_gate', 'new_m_w_ple_gate', 'new_m_w_ple_up', 'new_m_g_ple_post', 'new_m_g_final', 'new_v_g_mix', 'new_v_w_in', 'new_v_conv_w', 'new_v_conv_b', 'new_v_ln_g', 'new_v_ln_b', 'new_v_pool_w', 'new_v_pool_scale', 'new_v_w_out', 'new_v_g_ffn', 'new_v_w_gate_up', 'new_v_w_down', 'new_v_g_ple_gate', 'new_v_w_ple_gate', 'new_v_w_ple_up', 'new_v_g_ple_post', 'new_v_g_final']
TWIN_LEAF_KINDS = {'loss': 'loss', 'grad_x': 'grad_x', 'grad_g_mix': 'grad_w', 'grad_w_in': 'grad_w', 'grad_conv_w': 'grad_w', 'grad_conv_b': 'grad_w', 'grad_ln_g': 'grad_w', 'grad_ln_b': 'grad_w', 'grad_pool_w': 'grad_w', 'grad_pool_scale': 'grad_w', 'grad_w_out': 'grad_w', 'grad_g_ffn': 'grad_w', 'grad_w_gate_up': 'grad_w', 'grad_w_down': 'grad_w', 'grad_g_ple_gate': 'grad_w', 'grad_w_ple_gate': 'grad_w', 'grad_w_ple_up': 'grad_w', 'grad_g_ple_post': 'grad_w', 'grad_g_final': 'grad_w', 'delta_g_mix': 'delta_w', 'delta_w_in': 'delta_w', 'delta_conv_w': 'delta_w', 'delta_conv_b': 'delta_w', 'delta_ln_g': 'delta_w', 'delta_ln_b': 'delta_w', 'delta_pool_w': 'delta_w', 'delta_pool_scale': 'delta_w', 'delta_w_out': 'delta_w', 'delta_g_ffn': 'delta_w', 'delta_w_gate_up': 'delta_w', 'delta_w_down': 'delta_w', 'delta_g_ple_gate': 'delta_w', 'delta_w_ple_gate': 'delta_w', 'delta_w_ple_up': 'delta_w', 'delta_g_ple_post': 'delta_w', 'delta_g_final': 'delta_w', 'new_m_g_mix': 'new_m', 'new_m_w_in': 'new_m', 'new_m_conv_w': 'new_m', 'new_m_conv_b': 'new_m', 'new_m_ln_g': 'new_m', 'new_m_ln_b': 'new_m', 'new_m_pool_w': 'new_m', 'new_m_pool_scale': 'new_m', 'new_m_w_out': 'new_m', 'new_m_g_ffn': 'new_m', 'new_m_w_gate_up': 'new_m', 'new_m_w_down': 'new_m', 'new_m_g_ple_gate': 'new_m', 'new_m_w_ple_gate': 'new_m', 'new_m_w_ple_up': 'new_m', 'new_m_g_ple_post': 'new_m', 'new_m_g_final': 'new_m', 'new_v_g_mix': 'new_v', 'new_v_w_in': 'new_v', 'new_v_conv_w': 'new_v', 'new_v_conv_b': 'new_v', 'new_v_ln_g': 'new_v', 'new_v_ln_b': 'new_v', 'new_v_pool_w': 'new_v', 'new_v_pool_scale': 'new_v', 'new_v_w_out': 'new_v', 'new_v_g_ffn': 'new_v', 'new_v_w_gate_up': 'new_v', 'new_v_w_down': 'new_v', 'new_v_g_ple_gate': 'new_v', 'new_v_w_ple_gate': 'new_v', 'new_v_w_ple_up': 'new_v', 'new_v_g_ple_post': 'new_v', 'new_v_g_final': 'new_v'}


def _forward(args):
    return _fwd_reference(*[args[k] for k in FWD_PARAMS])


def _output_shape():
    out = _jax.eval_shape(lambda: _forward(_fwd_setup_inputs(0)))
    return out.shape, out.dtype

N_MICROBATCH = 1
ADAM_LR = 0.001
ADAM_B1 = 0.9
ADAM_B2 = 0.999
ADAM_EPS = 1e-08
ADAM_WD = 0.01
ADAM_STEP = 10
PER_EXAMPLE_BATCH_AXIS = {'x': 0, 'p': 1, 'loss_target': 0}
SHARED_INPUTS = []
_WEIGHT_DTYPES = {'g_mix': _jnp.float32, 'w_in': _jnp.float32, 'conv_w': _jnp.float32, 'conv_b': _jnp.float32, 'ln_g': _jnp.float32, 'ln_b': _jnp.float32, 'pool_w': _jnp.float32, 'pool_scale': _jnp.float32, 'w_out': _jnp.float32, 'g_ffn': _jnp.float32, 'w_gate_up': _jnp.float32, 'w_down': _jnp.float32, 'g_ple_gate': _jnp.float32, 'w_ple_gate': _jnp.float32, 'w_ple_up': _jnp.float32, 'g_ple_post': _jnp.float32, 'g_final': _jnp.float32}
MOMENT_SCALE = {'g_mix': 1.152800e-01, 'w_in': 9.490895e-02, 'conv_w': 9.180420e-02, 'conv_b': 2.080672e-01, 'ln_g': 1.229684e-01, 'ln_b': 1.200703e-01, 'pool_w': 1.313890e-01, 'pool_scale': 1.441996e-01, 'w_out': 1.143077e-01, 'g_ffn': 1.085408e-01, 'w_gate_up': 4.484510e-02, 'w_down': 7.359528e-02, 'g_ple_gate': 2.815435e-02, 'w_ple_gate': 2.668202e-02, 'w_ple_up': 6.577032e-02, 'g_ple_post': 3.259597e-01, 'g_final': 3.208202e+01}


def _to_microbatches(a, axis):
    t = _jnp.moveaxis(a, axis, 0)
    t = t.reshape((N_MICROBATCH, t.shape[0] // N_MICROBATCH) + t.shape[1:])
    return _jnp.moveaxis(t, 1, axis + 1)


def setup_inputs(seed: int = 0) -> dict:
    inp = _fwd_setup_inputs(seed)
    key = _jax.random.fold_in(_jax.random.key(seed), 7919)
    shape, _ = _output_shape()
    out = dict(inp)
    out["loss_target"] = _jax.random.normal(_jax.random.fold_in(key, 0), shape, _jnp.float32)
    for i, name in enumerate(TWIN_WEIGHTS):
        w = inp[name].astype(_jnp.float32)
        if MOMENT_SCALE is None:
            s = _jnp.sqrt(_jnp.mean(_jnp.square(w)) + 1e-30)
        else:
            s = MOMENT_SCALE[name]
        km, kv = _jax.random.split(_jax.random.fold_in(key, i + 1))
        out[name] = w
        out["m_" + name] = s * _jax.random.normal(km, w.shape, _jnp.float32)
        out["v_" + name] = (s * s) * _jax.random.uniform(kv, w.shape, _jnp.float32, 0.5, 1.5)
    if N_MICROBATCH > 1:
        for name, axis in PER_EXAMPLE_BATCH_AXIS.items():
            out[name] = _to_microbatches(out[name], axis)
    return {'x': out['x'], 'p': out['p'], 'g_mix': out['g_mix'], 'w_in': out['w_in'], 'conv_w': out['conv_w'], 'conv_b': out['conv_b'], 'ln_g': out['ln_g'], 'ln_b': out['ln_b'], 'pool_w': out['pool_w'], 'pool_scale': out['pool_scale'], 'w_out': out['w_out'], 'g_ffn': out['g_ffn'], 'w_gate_up': out['w_gate_up'], 'w_down': out['w_down'], 'g_ple_gate': out['g_ple_gate'], 'w_ple_gate': out['w_ple_gate'], 'w_ple_up': out['w_ple_up'], 'g_ple_post': out['g_ple_post'], 'g_final': out['g_final'], 'loss_target': out['loss_target'], 'm_g_mix': out['m_g_mix'], 'm_w_in': out['m_w_in'], 'm_conv_w': out['m_conv_w'], 'm_conv_b': out['m_conv_b'], 'm_ln_g': out['m_ln_g'], 'm_ln_b': out['m_ln_b'], 'm_pool_w': out['m_pool_w'], 'm_pool_scale': out['m_pool_scale'], 'm_w_out': out['m_w_out'], 'm_g_ffn': out['m_g_ffn'], 'm_w_gate_up': out['m_w_gate_up'], 'm_w_down': out['m_w_down'], 'm_g_ple_gate': out['m_g_ple_gate'], 'm_w_ple_gate': out['m_w_ple_gate'], 'm_w_ple_up': out['m_w_ple_up'], 'm_g_ple_post': out['m_g_ple_post'], 'm_g_final': out['m_g_final'], 'v_g_mix': out['v_g_mix'], 'v_w_in': out['v_w_in'], 'v_conv_w': out['v_conv_w'], 'v_conv_b': out['v_conv_b'], 'v_ln_g': out['v_ln_g'], 'v_ln_b': out['v_ln_b'], 'v_pool_w': out['v_pool_w'], 'v_pool_scale': out['v_pool_scale'], 'v_w_out': out['v_w_out'], 'v_g_ffn': out['v_g_ffn'], 'v_w_gate_up': out['v_w_gate_up'], 'v_w_down': out['v_w_down'], 'v_g_ple_gate': out['v_g_ple_gate'], 'v_w_ple_gate': out['v_w_ple_gate'], 'v_w_ple_up': out['v_w_ple_up'], 'v_g_ple_post': out['v_g_ple_post'], 'v_g_final': out['v_g_final']}


def _loss(weights, diff, rest, loss_target):
    with _jax.named_scope("forward"):
        args = {**rest, TWIN_DIFF_INPUT: diff, **{k: w.astype(_WEIGHT_DTYPES[k]) for k, w in weights.items()}}
        y = _forward(args)
    with _jax.named_scope("loss_head"):
        err = _jnp.square(y.astype(_jnp.float32) - loss_target)
        return 0.5 * _jnp.sum(_jnp.mean(err, axis=-1)) if err.ndim else 0.5 * err


def _adamw(w, g, m, v):
    m = ADAM_B1 * m + (1.0 - ADAM_B1) * g
    v = ADAM_B2 * v + (1.0 - ADAM_B2) * _jnp.square(g)
    m_hat = m / (1.0 - ADAM_B1 ** ADAM_STEP)
    v_hat = v / (1.0 - ADAM_B2 ** ADAM_STEP)
    delta = -ADAM_LR * (m_hat / (_jnp.sqrt(v_hat) + ADAM_EPS) + ADAM_WD * w)
    return delta, m, v


def reference(x, p, g_mix, w_in, conv_w, conv_b, ln_g, ln_b, pool_w, pool_scale, w_out, g_ffn, w_gate_up, w_down, g_ple_gate, w_ple_gate, w_ple_up, g_ple_post, g_final, loss_target, m_g_mix, m_w_in, m_conv_w, m_conv_b, m_ln_g, m_ln_b, m_pool_w, m_pool_scale, m_w_out, m_g_ffn, m_w_gate_up, m_w_down, m_g_ple_gate, m_w_ple_gate, m_w_ple_up, m_g_ple_post, m_g_final, v_g_mix, v_w_in, v_conv_w, v_conv_b, v_ln_g, v_ln_b, v_pool_w, v_pool_scale, v_w_out, v_g_ffn, v_w_gate_up, v_w_down, v_g_ple_gate, v_w_ple_gate, v_w_ple_up, v_g_ple_post, v_g_final):
    given = dict(x=x, p=p, g_mix=g_mix, w_in=w_in, conv_w=conv_w, conv_b=conv_b, ln_g=ln_g, ln_b=ln_b, pool_w=pool_w, pool_scale=pool_scale, w_out=w_out, g_ffn=g_ffn, w_gate_up=w_gate_up, w_down=w_down, g_ple_gate=g_ple_gate, w_ple_gate=w_ple_gate, w_ple_up=w_ple_up, g_ple_post=g_ple_post, g_final=g_final, loss_target=loss_target, m_g_mix=m_g_mix, m_w_in=m_w_in, m_conv_w=m_conv_w, m_conv_b=m_conv_b, m_ln_g=m_ln_g, m_ln_b=m_ln_b, m_pool_w=m_pool_w, m_pool_scale=m_pool_scale, m_w_out=m_w_out, m_g_ffn=m_g_ffn, m_w_gate_up=m_w_gate_up, m_w_down=m_w_down, m_g_ple_gate=m_g_ple_gate, m_w_ple_gate=m_w_ple_gate, m_w_ple_up=m_w_ple_up, m_g_ple_post=m_g_ple_post, m_g_final=m_g_final, v_g_mix=v_g_mix, v_w_in=v_w_in, v_conv_w=v_conv_w, v_conv_b=v_conv_b, v_ln_g=v_ln_g, v_ln_b=v_ln_b, v_pool_w=v_pool_w, v_pool_scale=v_pool_scale, v_w_out=v_w_out, v_g_ffn=v_g_ffn, v_w_gate_up=v_w_gate_up, v_w_down=v_w_down, v_g_ple_gate=v_g_ple_gate, v_w_ple_gate=v_w_ple_gate, v_w_ple_up=v_w_ple_up, v_g_ple_post=v_g_ple_post, v_g_final=v_g_final)
    weights = {n: given[n] for n in TWIN_WEIGHTS}
    shared = {n: given[n] for n in SHARED_INPUTS}
    per_example = {n: given[n] for n in ['x', 'p']}
    grad_fn = _jax.value_and_grad(_loss, argnums=(0, 1))

    def one_microbatch(ex, loss_target):
        ex = dict(ex)
        diff = ex.pop(TWIN_DIFF_INPUT)
        return grad_fn(weights, diff, {**shared, **ex}, loss_target)

    if N_MICROBATCH == 1:
        loss, (grad_w, grad_x) = one_microbatch(per_example, given["loss_target"])
    else:
        def body(carry, xs):
            loss_sum, grad_sum = carry
            l_k, (gw_k, gx_k) = one_microbatch(xs[0], xs[1])
            with _jax.named_scope("update"):
                return (loss_sum + l_k, _jax.tree.map(_jnp.add, grad_sum, gw_k)), gx_k

        init = (_jnp.zeros((), _jnp.float32), _jax.tree.map(_jnp.zeros_like, weights))
        (loss, grad_w), grad_x = _jax.lax.scan(body, init, (per_example, given["loss_target"]))
    with _jax.named_scope("update"):
        delta_w, new_m, new_v = {}, {}, {}
        for n in TWIN_WEIGHTS:
            delta_w[n], new_m[n], new_v[n] = _adamw(weights[n], grad_w[n], given["m_" + n], given["v_" + n])
    return (loss, grad_x, *[grad_w[n] for n in TWIN_WEIGHTS], *[delta_w[n] for n in TWIN_WEIGHTS],
            *[new_m[n] for n in TWIN_WEIGHTS], *[new_v[n] for n in TWIN_WEIGHTS])
```

```python
import functools

import jax
import jax.numpy as jnp
from jax import lax
from jax.experimental import pallas as pl
from jax.experimental.pallas import tpu as pltpu

f32, bf16 = jnp.float32, jnp.bfloat16

EPS = 1e-6
D_MODEL = 1024
C_CONV = 512
C_POOL = 512
POOL_WINDOWS = (2, 4, 8, 16)
POOL_GROUP = 128
CONV_K = 31
D_FF = 2816
D_PLE = 256
N_CHIPS = 4
N_DEV = 8
W_IN_COLS = 2 * C_CONV + C_POOL
W_IN_CHUNK = W_IN_COLS // N_CHIPS
FF_CHUNK = 2 * D_FF // N_CHIPS
PLE_CHUNK = D_MODEL // N_CHIPS
HALO = 32
ROW_TILE = 512
CONV_ROWS = 32
FF_SUB = (0, 512, 1024, FF_CHUNK)
VMEM_LIMIT = 56 * 1024 * 1024

ADAM_LR = 0.001
ADAM_B1 = 0.9
ADAM_B2 = 0.999
ADAM_EPS = 1e-08
ADAM_WD = 0.01
ADAM_STEP = 10

MESH = pl.DeviceIdType.MESH
ANY = pl.BlockSpec(memory_space=pl.ANY)
VMEM = pl.BlockSpec(memory_space=pltpu.VMEM)


def _cp(*sem):
    return pltpu.CompilerParams(dimension_semantics=sem, vmem_limit_bytes=VMEM_LIMIT)


def _dot(a, b):
    return jnp.dot(a, b, preferred_element_type=f32)


def _dot_nt(a, b):
    return lax.dot_general(a, b, (((1,), (1,)), ((), ())), preferred_element_type=f32)


def _dot_tn(a, b):
    return lax.dot_general(a, b, (((0,), (0,)), ((), ())), preferred_element_type=f32)


def _sigmoid(v):
    return jax.nn.sigmoid(v)


def _rms_fwd(v, g):
    r = lax.rsqrt(jnp.mean(v * v, axis=-1, keepdims=True) + EPS)
    vh = v * r
    return vh * g, vh, r


def _rms_bwd(dy, vh, r, g):
    dvh = dy * g
    dv = r * (dvh - vh * jnp.mean(dvh * vh, axis=-1, keepdims=True))
    return dv, jnp.sum(dy * vh, axis=0, keepdims=True)


def _silu_grad(v, s):
    return s * (1.0 + v * (1.0 - s))


def _row(i, n):
    return pl.BlockSpec((n[0], n[1]), lambda *a: (a[i], 0))


def _full(shape):
    nd = len(shape)
    return pl.BlockSpec(shape, lambda *a: (0,) * nd)


def _place():
    x, y, c = lax.axis_index("x"), lax.axis_index("y"), lax.axis_index("c")
    others = [(1 - x, y), (x, 1 - y), (1 - x, 1 - y)]
    return x, y, c, 2 * x + y, others


def _mix_in(x, g_mix, w_in_g):
    s = x.shape[0]
    tm = min(ROW_TILE, s)

    def body(x_ref, g_ref, w_ref, z_ref):
        h, _, _ = _rms_fwd(x_ref[...], g_ref[...])
        hb = h.astype(bf16)
        for j in range(N_CHIPS):
            z_ref[:, j * W_IN_CHUNK:(j + 1) * W_IN_CHUNK] = _dot(hb, w_ref[j])

    return pl.pallas_call(
        body, name="mix_in", grid=(s // tm,),
        in_specs=[_row(0, (tm, D_MODEL)), _full((1, D_MODEL)), _full((N_CHIPS, D_MODEL, W_IN_CHUNK))],
        out_specs=_row(0, (tm, W_IN_COLS)),
        out_shape=jax.ShapeDtypeStruct((s, W_IN_COLS), f32),
        compiler_params=_cp("parallel"),
    )(x, g_mix, w_in_g)


def _pool_counts(tm, w, first_row):
    t1 = (lax.broadcasted_iota(jnp.int32, (tm, 1), 0) + first_row + 1).astype(f32)
    return jnp.minimum(t1, float(w))


def _conv_pool_out(z, x, conv_w, conv_b, ln_g, ln_b, pool_w, pool_scale, w_out):
    s = x.shape[0]
    tm = min(ROW_TILE, s)
    hb = tm // HALO

    def body(z_ref, zp_ref, x_ref, cw_ref, cb_ref, lg_ref, lb_ref, pw_ref, ps_ref, wo_ref,
             x1_ref, mix_ref, u1_ref, pooled_ref, ubuf, vbuf):
        i = pl.program_id(0)
        keep = (i > 0).astype(f32)
        zp = zp_ref[...] * keep
        ubuf[0:HALO, :] = zp[:, :C_CONV] * _sigmoid(zp[:, C_CONV:2 * C_CONV])
        vbuf[0:HALO, :] = zp[:, 2 * C_CONV:]
        ubuf[HALO:, :] = z_ref[:, :C_CONV] * _sigmoid(z_ref[:, C_CONV:2 * C_CONV])
        vbuf[HALO:, :] = z_ref[:, 2 * C_CONV:]
        off = HALO - (CONV_K - 1)
        for r0 in range(0, tm, CONV_ROWS):
            acc = jnp.zeros((CONV_ROWS, C_CONV), f32) + cb_ref[...]
            for k in range(CONV_K):
                acc = acc + cw_ref[pl.ds(k, 1), :] * ubuf[pl.ds(r0 + off + k, CONV_ROWS), :]
            u1_ref[r0:r0 + CONV_ROWS, :] = acc
        u1 = u1_ref[...]
        mu = jnp.mean(u1, axis=-1, keepdims=True)
        uc = u1 - mu
        rstd = lax.rsqrt(jnp.mean(uc * uc, axis=-1, keepdims=True) + EPS)
        u2 = uc * rstd * lg_ref[...] + lb_ref[...]
        mix_ref[:, :C_CONV] = (u2 * _sigmoid(u2)).astype(bf16)
        for g, w in enumerate(POOL_WINDOWS):
            cols = slice(g * POOL_GROUP, (g + 1) * POOL_GROUP)
            acc = vbuf[pl.ds(HALO, tm), cols]
            vg = acc
            for d in range(1, w):
                acc = acc + vbuf[pl.ds(HALO - d, tm), cols]
            pooled = (acc / _pool_counts(tm, w, i * tm) - vg).astype(bf16)
            pooled_ref[:, cols] = pooled
            mixed = _dot(pooled, pw_ref[g].astype(bf16))
            mix_ref[:, C_CONV + g * POOL_GROUP:C_CONV + (g + 1) * POOL_GROUP] = (mixed * ps_ref[:, cols]).astype(bf16)
        x1_ref[...] = x_ref[...] + _dot(mix_ref[...], wo_ref[...])

    return pl.pallas_call(
        body, name="conv_pool_out", grid=(s // tm,),
        in_specs=[_row(0, (tm, W_IN_COLS)),
                  pl.BlockSpec((HALO, W_IN_COLS), lambda i: (jnp.maximum(i * hb - 1, 0), 0)),
                  _row(0, (tm, D_MODEL)), _full((HALO, C_CONV)), _full((1, C_CONV)), _full((1, C_CONV)),
                  _full((1, C_CONV)), _full((4, POOL_GROUP, POOL_GROUP)), _full((1, C_POOL)),
                  _full((D_MODEL, D_MODEL))],
        out_specs=[_row(0, (tm, D_MODEL)), _row(0, (tm, D_MODEL)), _row(0, (tm, C_CONV)), _row(0, (tm, C_POOL))],
        out_shape=[jax.ShapeDtypeStruct((s, D_MODEL), f32), jax.ShapeDtypeStruct((s, D_MODEL), bf16),
                   jax.ShapeDtypeStruct((s, C_CONV), f32), jax.ShapeDtypeStruct((s, C_POOL), bf16)],
        scratch_shapes=[pltpu.VMEM((HALO + tm, C_CONV), f32), pltpu.VMEM((HALO + tm, C_POOL), f32)],
        compiler_params=_cp("parallel"),
    )(z, z, x, conv_w, conv_b, ln_g, ln_b, pool_w, pool_scale, w_out)


def _ffn_fwd(x1, g_ffn, w_gu_g, w_down):
    s = x1.shape[0]
    tm = min(ROW_TILE, s)

    def body(x1_ref, g_ref, wg_ref, wu_ref, wd_ref, x2_ref, h2_ref, gu_ref, acc_ref):
        c = pl.program_id(1)

        @pl.when(c == 0)
        def _():
            h, _, _ = _rms_fwd(x1_ref[...], g_ref[...])
            h2_ref[...] = h.astype(bf16)
            acc_ref[...] = jnp.zeros_like(acc_ref)

        h = h2_ref[...]
        for lo, hi in zip(FF_SUB[:-1], FF_SUB[1:]):
            gate = _dot(h, wg_ref[0, :, lo:hi])
            up = _dot(h, wu_ref[0, :, lo:hi])
            gu_ref[0, :, lo:hi] = gate.astype(bf16)
            gu_ref[1, :, lo:hi] = up.astype(bf16)
            f = (gate * _sigmoid(gate) * up).astype(bf16)
            acc_ref[...] += _dot(f, wd_ref[lo:hi, :])

        @pl.when(c == 1)
        def _():
            x2_ref[...] = x1_ref[...] + acc_ref[...]

    return pl.pallas_call(
        body, name="ffn_fwd", grid=(s // tm, 2),
        in_specs=[_row(0, (tm, D_MODEL)), _full((1, D_MODEL)),
                  pl.BlockSpec((1, D_MODEL, FF_CHUNK), lambda i, c: (c, 0, 0)),
                  pl.BlockSpec((1, D_MODEL, FF_CHUNK), lambda i, c: (2 + c, 0, 0)),
                  pl.BlockSpec((FF_CHUNK, D_MODEL), lambda i, c: (c, 0))],
        out_specs=[_row(0, (tm, D_MODEL)), _row(0, (tm, D_MODEL)),
                   pl.BlockSpec((2, tm, FF_CHUNK), lambda i, c: (0, i, c))],
        out_shape=[jax.ShapeDtypeStruct((s, D_MODEL), f32), jax.ShapeDtypeStruct((s, D_MODEL), bf16),
                   jax.ShapeDtypeStruct((2, s, D_FF), bf16)],
        scratch_shapes=[pltpu.VMEM((tm, D_MODEL), f32)],
        compiler_params=_cp("parallel", "arbitrary"),
    )(x1, g_ffn, w_gu_g, w_gu_g, w_down)


def _ple_loss(x2, p, target, g_pg, g_post, g_final, w_pg, w_pu_g):
    s = x2.shape[0]
    tm = min(ROW_TILE, s)
    n = s // tm

    def body(x2_ref, p_ref, t_ref, gpg_ref, gpo_ref, gf_ref, wpg_ref, wpu_ref,
             dx2_ref, dwpg_ref, dwpu_ref, small_ref, apg_ref, apu_ref):
        i = pl.program_id(0)

        @pl.when(i == 0)
        def _():
            apg_ref[...] = jnp.zeros_like(apg_ref)
            apu_ref[...] = jnp.zeros_like(apu_ref)
            small_ref[...] = jnp.zeros_like(small_ref)

        x2 = x2_ref[...]
        h3, x2h, r2 = _rms_fwd(x2, gpg_ref[...])
        h3b = h3.astype(bf16)
        gate = _sigmoid(_dot(h3b, wpg_ref[...]))
        pb = p_ref[...].astype(bf16)
        pe = jnp.concatenate([_dot(pb, wpu_ref[j]) for j in range(N_CHIPS)], axis=-1)
        e, peh, rp = _rms_fwd(pe, gpo_ref[...])
        x3 = x2 + gate * e
        y, x3h, r3 = _rms_fwd(x3, gf_ref[...])
        d = y - t_ref[...]
        loss = 0.5 * jnp.sum(jnp.sum(d * d, axis=-1, keepdims=True) * (1.0 / D_MODEL), axis=0, keepdims=True)
        dx3, dgf = _rms_bwd(d * (1.0 / D_MODEL), x3h, r3, gf_ref[...])
        dpe, dgpo = _rms_bwd(dx3 * gate, peh, rp, gpo_ref[...])
        dgl = (dx3 * e * gate * (1.0 - gate)).astype(bf16)
        apg_ref[...] += _dot_tn(h3b, dgl)
        apu_ref[...] += _dot_tn(pb, dpe.astype(bf16))
        dh3 = _dot_nt(dgl, wpg_ref[...])
        dx2b, dgpg = _rms_bwd(dh3, x2h, r2, gpg_ref[...])
        dx2_ref[...] = dx3 + dx2b
        small_ref[0:1, :] += dgpg
        small_ref[1:2, :] += dgpo
        small_ref[2:3, :] += dgf
        small_ref[3:4, :] += jnp.broadcast_to(loss, (1, D_MODEL))

        @pl.when(i == n - 1)
        def _():
            dwpg_ref[...] = apg_ref[...].astype(bf16)
            for j in range(N_CHIPS):
                dwpu_ref[j] = apu_ref[:, j * PLE_CHUNK:(j + 1) * PLE_CHUNK].astype(bf16)

    return pl.pallas_call(
        body, name="ple_loss", grid=(n,),
        in_specs=[_row(0, (tm, D_MODEL)), _row(0, (tm, D_PLE)), _row(0, (tm, D_MODEL)),
                  _full((1, D_MODEL)), _full((1, D_MODEL)), _full((1, D_MODEL)),
                  _full((D_MODEL, D_MODEL)), _full((N_CHIPS, D_PLE, PLE_CHUNK))],
        out_specs=[_row(0, (tm, D_MODEL)), _full((D_MODEL, D_MODEL)), _full((N_CHIPS, D_PLE, PLE_CHUNK)),
                   _full((8, D_MODEL))],
        out_shape=[jax.ShapeDtypeStruct((s, D_MODEL), f32), jax.ShapeDtypeStruct((D_MODEL, D_MODEL), bf16),
                   jax.ShapeDtypeStruct((N_CHIPS, D_PLE, PLE_CHUNK), bf16), jax.ShapeDtypeStruct((8, D_MODEL), f32)],
        scratch_shapes=[pltpu.VMEM((D_MODEL, D_MODEL), f32), pltpu.VMEM((D_PLE, D_MODEL), f32)],
        compiler_params=_cp("arbitrary"),
    )(x2, p, target, g_pg, g_post, g_final, w_pg, w_pu_g)


def _ffn_bwd_dx(dx2, x1, gu, g_ffn, w_gu_g, w_down):
    s = x1.shape[0]
    tm = min(ROW_TILE, s)

    def body(dx2_ref, x1_ref, gu_ref, g_ref, wg_ref, wu_ref, wd_ref, dx1_ref, dgu_ref, small_ref, acc_ref):
        i, c = pl.program_id(0), pl.program_id(1)

        @pl.when(jnp.logical_and(i == 0, c == 0))
        def _():
            small_ref[...] = jnp.zeros_like(small_ref)

        @pl.when(c == 0)
        def _():
            acc_ref[...] = jnp.zeros_like(acc_ref)

        dyb = dx2_ref[...].astype(bf16)
        for lo, hi in zip(FF_SUB[:-1], FF_SUB[1:]):
            df = _dot_nt(dyb, wd_ref[lo:hi, :])
            gate = gu_ref[0, :, lo:hi].astype(f32)
            up = gu_ref[1, :, lo:hi].astype(f32)
            sg = _sigmoid(gate)
            dgate = (df * up * _silu_grad(gate, sg)).astype(bf16)
            dup = (df * gate * sg).astype(bf16)
            dgu_ref[0, :, lo:hi] = dgate
            dgu_ref[1, :, lo:hi] = dup
            acc_ref[...] += _dot_nt(dgate, wg_ref[0, :, lo:hi]) + _dot_nt(dup, wu_ref[0, :, lo:hi])

        @pl.when(c == 1)
        def _():
            _, x1h, r1 = _rms_fwd(x1_ref[...], g_ref[...])
            dx1b, dg = _rms_bwd(acc_ref[...], x1h, r1, g_ref[...])
            dx1_ref[...] = dx2_ref[...] + dx1b
            small_ref[0:1, :] += dg

    return pl.pallas_call(
        body, name="ffn_bwd_dx", grid=(s // tm, 2),
        in_specs=[_row(0, (tm, D_MODEL)), _row(0, (tm, D_MODEL)),
                  pl.BlockSpec((2, tm, FF_CHUNK), lambda i, c: (0, i, c)), _full((1, D_MODEL)),
                  pl.BlockSpec((1, D_MODEL, FF_CHUNK), lambda i, c: (c, 0, 0)),
                  pl.BlockSpec((1, D_MODEL, FF_CHUNK), lambda i, c: (2 + c, 0, 0)),
                  pl.BlockSpec((FF_CHUNK, D_MODEL), lambda i, c: (c, 0))],
        out_specs=[_row(0, (tm, D_MODEL)), pl.BlockSpec((2, tm, FF_CHUNK), lambda i, c: (0, i, c)),
                   _full((8, D_MODEL))],
        out_shape=[jax.ShapeDtypeStruct((s, D_MODEL), f32), jax.ShapeDtypeStruct((2, s, D_FF), bf16),
                   jax.ShapeDtypeStruct((8, D_MODEL), f32)],
        scratch_shapes=[pltpu.VMEM((tm, D_MODEL), f32)],
        compiler_params=_cp("arbitrary", "arbitrary"),
    )(dx2, x1, gu, g_ffn, w_gu_g, w_gu_g, w_down)


def _ffn_bwd_dw_gu(h2, dgu):
    s = h2.shape[0]
    ts = min(ROW_TILE, s)
    n = s // ts

    def body(h_ref, d_ref, o_ref, acc_ref):
        t = pl.program_id(1)

        @pl.when(t == 0)
        def _():
            acc_ref[...] = jnp.zeros_like(acc_ref)

        acc_ref[...] += _dot_tn(h_ref[...], d_ref[0])

        @pl.when(t == n - 1)
        def _():
            o_ref[0] = acc_ref[...].astype(bf16)

    return pl.pallas_call(
        body, name="ffn_bwd_dw_gu", grid=(N_CHIPS, n),
        in_specs=[pl.BlockSpec((ts, D_MODEL), lambda j, t: (t, 0)),
                  pl.BlockSpec((1, ts, FF_CHUNK), lambda j, t: (j // 2, t, j % 2))],
        out_specs=pl.BlockSpec((1, D_MODEL, FF_CHUNK), lambda j, t: (j, 0, 0)),
        out_shape=jax.ShapeDtypeStruct((N_CHIPS, D_MODEL, FF_CHUNK), bf16),
        scratch_shapes=[pltpu.VMEM((D_MODEL, FF_CHUNK), f32)],
        compiler_params=_cp("parallel", "arbitrary"),
    )(h2, dgu)


def _ffn_bwd_dw_down(gu, dx2):
    s = dx2.shape[0]
    ts = min(ROW_TILE, s)
    n = s // ts

    def body(gu_ref, d_ref, o_ref, acc_ref):
        t = pl.program_id(1)

        @pl.when(t == 0)
        def _():
            acc_ref[...] = jnp.zeros_like(acc_ref)

        gate = gu_ref[0].astype(f32)
        f = (gate * _sigmoid(gate) * gu_ref[1].astype(f32)).astype(bf16)
        acc_ref[...] += _dot_tn(f, d_ref[...].astype(bf16))

        @pl.when(t == n - 1)
        def _():
            o_ref[...] = acc_ref[...].astype(bf16)

    return pl.pallas_call(
        body, name="ffn_bwd_dw_down", grid=(2, n),
        in_specs=[pl.BlockSpec((2, ts, FF_CHUNK), lambda c, t: (0, t, c)),
                  pl.BlockSpec((ts, D_MODEL), lambda c, t: (t, 0))],
        out_specs=pl.BlockSpec((FF_CHUNK, D_MODEL), lambda c, t: (c, 0)),
        out_shape=jax.ShapeDtypeStruct((D_FF, D_MODEL), bf16),
        scratch_shapes=[pltpu.VMEM((FF_CHUNK, D_MODEL), f32)],
        compiler_params=_cp("parallel", "arbitrary"),
    )(gu, dx2)


def _mix_bwd_local(dx1, mix, u1, pooled, w_out, ln_g, ln_b, pool_w, pool_scale):
    s = dx1.shape[0]
    tm = min(ROW_TILE, s)
    n = s // tm

    def body(dx1_ref, mix_ref, u1_ref, po_ref, wo_ref, lg_ref, lb_ref, pw_ref, ps_ref,
             du1_ref, dpo_ref, dwo_ref, dpw_ref, small_ref, awo_ref):
        i = pl.program_id(0)

        @pl.when(i == 0)
        def _():
            awo_ref[...] = jnp.zeros_like(awo_ref)
            dpw_ref[...] = jnp.zeros_like(dpw_ref)
            small_ref[...] = jnp.zeros_like(small_ref)

        dyb = dx1_ref[...].astype(bf16)
        dmix = _dot_nt(dyb, wo_ref[...])
        awo_ref[...] += _dot_tn(mix_ref[...], dyb)
        u1 = u1_ref[...]
        mu = jnp.mean(u1, axis=-1, keepdims=True)
        uc = u1 - mu
        rstd = lax.rsqrt(jnp.mean(uc * uc, axis=-1, keepdims=True) + EPS)
        uh = uc * rstd
        u2 = uh * lg_ref[...] + lb_ref[...]
        du2 = dmix[:, :C_CONV] * _silu_grad(u2, _sigmoid(u2))
        duh = du2 * lg_ref[...]
        du1 = rstd * (duh - jnp.mean(duh, axis=-1, keepdims=True) - uh * jnp.mean(duh * uh, axis=-1, keepdims=True))
        du1_ref[...] = du1
        small_ref[0:1, :] += jnp.sum(du2 * uh, axis=0, keepdims=True)
        small_ref[1:2, :] += jnp.sum(du2, axis=0, keepdims=True)
        small_ref[2:3, :] += jnp.sum(du1, axis=0, keepdims=True)
        for g in range(len(POOL_WINDOWS)):
            cols = slice(g * POOL_GROUP, (g + 1) * POOL_GROUP)
            dq = dmix[:, C_CONV + g * POOL_GROUP:C_CONV + (g + 1) * POOL_GROUP]
            pwb = pw_ref[g].astype(bf16)
            pg = po_ref[:, cols]
            mixed = _dot(pg, pwb)
            small_ref[3:4, cols] += jnp.sum(dq * mixed, axis=0, keepdims=True)
            dmixed = (dq * ps_ref[:, cols]).astype(bf16)
            dpw_ref[g] += _dot_tn(pg, dmixed)
            dpo_ref[:, cols] = _dot_nt(dmixed, pwb)

        @pl.when(i == n - 1)
        def _():
            dwo_ref[...] = awo_ref[...].astype(bf16)

    return pl.pallas_call(
        body, name="mix_bwd_local", grid=(n,),
        in_specs=[_row(0, (tm, D_MODEL)), _row(0, (tm, D_MODEL)), _row(0, (tm, C_CONV)), _row(0, (tm, C_POOL)),
                  _full((D_MODEL, D_MODEL)), _full((1, C_CONV)), _full((1, C_CONV)),
                  _full((4, POOL_GROUP, POOL_GROUP)), _full((1, C_POOL))],
        out_specs=[_row(0, (tm, C_CONV)), _row(0, (tm, C_POOL)), _full((D_MODEL, D_MODEL)),
                   _full((4, POOL_GROUP, POOL_GROUP)), _full((8, C_CONV))],
        out_shape=[jax.ShapeDtypeStruct((s, C_CONV), f32), jax.ShapeDtypeStruct((s, C_POOL), f32),
                   jax.ShapeDtypeStruct((D_MODEL, D_MODEL), bf16),
                   jax.ShapeDtypeStruct((4, POOL_GROUP, POOL_GROUP), f32), jax.ShapeDtypeStruct((8, C_CONV), f32)],
        scratch_shapes=[pltpu.VMEM((D_MODEL, D_MODEL), f32)],
        compiler_params=_cp("arbitrary"),
    )(dx1, mix, u1, pooled, w_out, ln_g, ln_b, pool_w, pool_scale)


def _in_bwd(du1, dpo, z, x, dx1, conv_w, g_mix, w_in_g):
    s = x.shape[0]
    tm = min(ROW_TILE, s)
    n = s // tm
    hb = tm // HALO
    last = s // HALO - 1

    def body(du_ref, dun_ref, dp_ref, dpn_ref, z_ref, zp_ref, x_ref, dx1_ref, cw_ref, g_ref, w_ref,
             gx_ref, dw_ref, dcw_ref, small_ref, dbuf, pbuf, ubuf, dz_ref, acc_ref):
        i = pl.program_id(0)

        @pl.when(i == 0)
        def _():
            acc_ref[...] = jnp.zeros_like(acc_ref)
            dcw_ref[...] = jnp.zeros_like(dcw_ref)
            small_ref[...] = jnp.zeros_like(small_ref)

        keep_prev = (i > 0).astype(f32)
        keep_next = (i < n - 1).astype(f32)
        zp = zp_ref[...] * keep_prev
        ubuf[0:HALO, :] = zp[:, :C_CONV] * _sigmoid(zp[:, C_CONV:2 * C_CONV])
        a = z_ref[:, :C_CONV]
        sb = _sigmoid(z_ref[:, C_CONV:2 * C_CONV])
        ubuf[HALO:, :] = a * sb
        dbuf[0:tm, :] = du_ref[...]
        dbuf[tm:, :] = dun_ref[...] * keep_next
        for g, w in enumerate(POOL_WINDOWS):
            cols = slice(g * POOL_GROUP, (g + 1) * POOL_GROUP)
            pbuf[0:tm, cols] = dp_ref[:, cols] / _pool_counts(tm, w, i * tm)
            pbuf[tm:, cols] = dpn_ref[:, cols] * keep_next / _pool_counts(HALO, w, (i + 1) * tm)
        off = HALO - (CONV_K - 1)
        for r0 in range(0, tm, CONV_ROWS):
            acc = jnp.zeros((CONV_ROWS, C_CONV), f32)
            for k in range(CONV_K):
                acc = acc + cw_ref[pl.ds(k, 1), :] * dbuf[pl.ds(r0 + CONV_K - 1 - k, CONV_ROWS), :]
            rows = pl.ds(r0, CONV_ROWS)
            dz_ref[rows, 0:C_CONV] = (acc * sb[r0:r0 + CONV_ROWS]).astype(bf16)
            dz_ref[rows, C_CONV:2 * C_CONV] = (
                acc * a[r0:r0 + CONV_ROWS] * sb[r0:r0 + CONV_ROWS] * (1.0 - sb[r0:r0 + CONV_ROWS])).astype(bf16)
        for k in range(CONV_K):
            part = jnp.zeros((8, C_CONV), f32)
            for r0 in range(0, tm, 8):
                part = part + dbuf[pl.ds(r0, 8), :] * ubuf[pl.ds(r0 + off + k, 8), :]
            dcw_ref[pl.ds(k, 1), :] += jnp.sum(part, axis=0, keepdims=True)
        for g, w in enumerate(POOL_WINDOWS):
            cols = slice(g * POOL_GROUP, (g + 1) * POOL_GROUP)
            acc = pbuf[pl.ds(0, tm), cols]
            for d in range(1, w):
                acc = acc + pbuf[pl.ds(d, tm), cols]
            dz_ref[:, 2 * C_CONV + g * POOL_GROUP:2 * C_CONV + (g + 1) * POOL_GROUP] = (
                acc - dp_ref[:, cols]).astype(bf16)
        h, xh, r = _rms_fwd(x_ref[...], g_ref[...])
        dz = dz_ref[...]
        acc_ref[...] += _dot_tn(h.astype(bf16), dz)
        dh = _dot_nt(dz[:, 0:W_IN_CHUNK], w_ref[0])
        for j in range(1, N_CHIPS):
            dh = dh + _dot_nt(dz[:, j * W_IN_CHUNK:(j + 1) * W_IN_CHUNK], w_ref[j])
        dxb, dg = _rms_bwd(dh, xh, r, g_ref[...])
        gx_ref[...] = dx1_ref[...] + dxb
        small_ref[0:1, :] += dg

        @pl.when(i == n - 1)
        def _():
            for j in range(N_CHIPS):
                dw_ref[j] = acc_ref[:, j * W_IN_CHUNK:(j + 1) * W_IN_CHUNK].astype(bf16)

    nxt = lambda i: (jnp.minimum((i + 1) * hb, last), 0)
    return pl.pallas_call(
        body, name="in_bwd", grid=(n,),
        in_specs=[_row(0, (tm, C_CONV)), pl.BlockSpec((HALO, C_CONV), nxt),
                  _row(0, (tm, C_POOL)), pl.BlockSpec((HALO, C_POOL), nxt),
                  _row(0, (tm, W_IN_COLS)),
                  pl.BlockSpec((HALO, W_IN_COLS), lambda i: (jnp.maximum(i * hb - 1, 0), 0)),
                  _row(0, (tm, D_MODEL)), _row(0, (tm, D_MODEL)), _full((HALO, C_CONV)), _full((1, D_MODEL)),
                  _full((N_CHIPS, D_MODEL, W_IN_CHUNK))],
        out_specs=[_row(0, (tm, D_MODEL)), _full((N_CHIPS, D_MODEL, W_IN_CHUNK)), _full((HALO, C_CONV)),
                   _full((8, D_MODEL))],
        out_shape=[jax.ShapeDtypeStruct((s, D_MODEL), f32), jax.ShapeDtypeStruct((N_CHIPS, D_MODEL, W_IN_CHUNK), bf16),
                   jax.ShapeDtypeStruct((HALO, C_CONV), f32), jax.ShapeDtypeStruct((8, D_MODEL), f32)],
        scratch_shapes=[pltpu.VMEM((tm + HALO, C_CONV), f32), pltpu.VMEM((tm + HALO, C_POOL), f32),
                        pltpu.VMEM((HALO + tm, C_CONV), f32), pltpu.VMEM((tm, W_IN_COLS), bf16),
                        pltpu.VMEM((D_MODEL, W_IN_COLS), f32)],
        compiler_params=_cp("arbitrary"),
    )(du1, du1, dpo, dpo, z, z, x, dx1, conv_w, g_mix, w_in_g)


def _remote(src, dst, send_sem, recv_sem, dev):
    return pltpu.make_async_remote_copy(src_ref=src, dst_ref=dst, send_sem=send_sem, recv_sem=recv_sem,
                                        device_id=dev, device_id_type=MESH)


def _gather_weights(shards, conv_w):
    nk = len(shards)
    halves = [a.shape[0] // 2 for a in shards]

    def body(*refs):
        ins, cw_in = refs[:nk], refs[nk]
        outs, cw_out = refs[nk + 1:2 * nk + 1], refs[2 * nk + 1]
        send_sems, recv_sems, local_sems = refs[2 * nk + 2:]
        x, y, c, me, others = _place()
        sibling = (x, y, 1 - c)
        local = [pltpu.make_async_copy(ins[k], outs[k].at[me], local_sems.at[k]) for k in range(nk)]
        local.append(pltpu.make_async_copy(cw_in, cw_out.at[me], local_sems.at[nk]))
        for cp in local:
            cp.start()

        def piece(k, chip, half):
            return outs[k].at[chip, pl.ds(half * halves[k], halves[k]), :]

        sent = []
        for k in range(nk):
            for j, (ox, oy) in enumerate(others):
                cp = _remote(ins[k].at[pl.ds(c * halves[k], halves[k]), :], piece(k, me, c),
                             send_sems.at[6 * k + j], recv_sems.at[6 * k + j], (ox, oy, c))
                cp.start()
                sent.append(cp)
        for j, (ox, oy) in enumerate(others):
            cp = _remote(cw_in, cw_out.at[me], send_sems.at[6 * nk + j], recv_sems.at[6 * nk + j], (ox, oy, c))
            cp.start()
            sent.append(cp)
        for k in range(nk):
            for j, (ox, oy) in enumerate(others):
                got = piece(k, 2 * ox + oy, c)
                _remote(got, got, send_sems.at[6 * k + j], recv_sems.at[6 * k + j], (ox, oy, c)).wait_recv()
                cp = _remote(got, got, send_sems.at[6 * k + 3 + j], recv_sems.at[6 * k + 3 + j], sibling)
                cp.start()
                sent.append(cp)
        for j, (ox, oy) in enumerate(others):
            got = cw_out.at[2 * ox + oy]
            _remote(got, got, send_sems.at[6 * nk + j], recv_sems.at[6 * nk + j], (ox, oy, c)).wait_recv()
        for k in range(nk):
            for j, (ox, oy) in enumerate(others):
                got = piece(k, 2 * ox + oy, 1 - c)
                _remote(got, got, send_sems.at[6 * k + 3 + j], recv_sems.at[6 * k + 3 + j], sibling).wait_recv()
        for cp in sent:
            cp.wait_send()
        for cp in local:
            cp.wait()

    out_shape = [jax.ShapeDtypeStruct((N_CHIPS,) + a.shape, a.dtype) for a in shards]
    out_shape.append(jax.ShapeDtypeStruct((N_CHIPS,) + conv_w.shape, conv_w.dtype))
    nsem = 6 * nk + 3
    return pl.pallas_call(
        body, name="gather_weights", in_specs=[ANY] * (nk + 1), out_specs=[ANY] * (nk + 1), out_shape=out_shape,
        scratch_shapes=[pltpu.SemaphoreType.DMA((nsem,)), pltpu.SemaphoreType.DMA((nsem,)),
                        pltpu.SemaphoreType.DMA((nk + 1,))],
    )(*shards, conv_w)


def _rs_pair(grads):
    nk = len(grads)
    halves = [g.shape[1] // 2 for g in grads]

    def body(*refs):
        ins, outs = refs[:nk], refs[nk:2 * nk]
        send_sems, recv_sems = refs[2 * nk:]
        x, y, c, _, _ = _place()
        cps = [_remote(ins[k].at[:, pl.ds((1 - c) * halves[k], halves[k]), :], outs[k],
                       send_sems.at[k], recv_sems.at[k], (x, y, 1 - c)) for k in range(nk)]
        for cp in cps:
            cp.start()
        for cp in cps:
            cp.wait()

    return pl.pallas_call(
        body, name="rs_pair", in_specs=[ANY] * nk, out_specs=[ANY] * nk,
        out_shape=[jax.ShapeDtypeStruct((N_CHIPS, h, g.shape[2]), bf16) for g, h in zip(grads, halves)],
        scratch_shapes=[pltpu.SemaphoreType.DMA((nk,)), pltpu.SemaphoreType.DMA((nk,))],
    )(*grads)


def _rs_add_pair(k, core, grad, recv):
    _, half, cols = recv.shape

    def body(core_ref, g_ref, r_ref, o_ref):
        o_ref[...] = (g_ref[...].astype(f32) + r_ref[...].astype(f32)).astype(bf16)

    return pl.pallas_call(
        body, name=f"rs_add_pair_{k}",
        grid_spec=pltpu.PrefetchScalarGridSpec(
            num_scalar_prefetch=1, grid=(N_CHIPS,),
            in_specs=[pl.BlockSpec((1, half, cols), lambda j, core_ref: (j, core_ref[0], 0)),
                      pl.BlockSpec((1, half, cols), lambda j, core_ref: (j, 0, 0))],
            out_specs=pl.BlockSpec((1, half, cols), lambda j, core_ref: (j, 0, 0))),
        out_shape=jax.ShapeDtypeStruct(recv.shape, bf16),
        compiler_params=_cp("parallel"),
    )(core, grad, recv)


def _rs_cross(parts):
    nk = len(parts)

    def body(*refs):
        ins, outs = refs[:nk], refs[nk:2 * nk]
        send_sems, recv_sems, local_sems = refs[2 * nk:]
        _, _, c, me, others = _place()
        local = [pltpu.make_async_copy(ins[k].at[me], outs[k].at[me], local_sems.at[k]) for k in range(nk)]
        for cp in local:
            cp.start()
        sent = []
        for k in range(nk):
            for j, (ox, oy) in enumerate(others):
                cp = _remote(ins[k].at[2 * ox + oy], outs[k].at[me], send_sems.at[3 * k + j],
                             recv_sems.at[3 * k + j], (ox, oy, c))
                cp.start()
                sent.append(cp)
        for k in range(nk):
            for j, (ox, oy) in enumerate(others):
                got = outs[k].at[2 * ox + oy]
                _remote(got, got, send_sems.at[3 * k + j], recv_sems.at[3 * k + j], (ox, oy, c)).wait_recv()
        for cp in sent:
            cp.wait_send()
        for cp in local:
            cp.wait()

    return pl.pallas_call(
        body, name="rs_cross", in_specs=[ANY] * nk, out_specs=[ANY] * nk,
        out_shape=[jax.ShapeDtypeStruct(a.shape, bf16) for a in parts],
        scratch_shapes=[pltpu.SemaphoreType.DMA((3 * nk,)), pltpu.SemaphoreType.DMA((3 * nk,)),
                        pltpu.SemaphoreType.DMA((nk,))],
    )(*parts)


def _rs_sum_chips(k, got):
    _, half, cols = got.shape

    def body(g_ref, o_ref):
        acc = g_ref[0].astype(f32)
        for j in range(1, N_CHIPS):
            acc = acc + g_ref[j].astype(f32)
        o_ref[...] = acc

    return pl.pallas_call(
        body, name=f"rs_sum_chips_{k}", grid=(1,),
        in_specs=[_full((N_CHIPS, half, cols))], out_specs=_full((half, cols)),
        out_shape=jax.ShapeDtypeStruct((half, cols), f32),
        compiler_params=_cp("arbitrary"),
    )(got)


def _grad_pair(sums):
    nk = len(sums)

    def body(*refs):
        ins, outs = refs[:nk], refs[nk:2 * nk]
        send_sems, recv_sems, local_sems = refs[2 * nk:]
        x, y, c, _, _ = _place()
        cps, local = [], []
        for k in range(nk):
            half = ins[k].shape[0]
            mine = outs[k].at[pl.ds(c * half, half), :]
            local.append(pltpu.make_async_copy(ins[k], mine, local_sems.at[k]))
            cps.append(_remote(ins[k], mine, send_sems.at[k], recv_sems.at[k], (x, y, 1 - c)))
        for cp in local + cps:
            cp.start()
        for k in range(nk):
            half = ins[k].shape[0]
            theirs = outs[k].at[pl.ds((1 - c) * half, half), :]
            _remote(theirs, theirs, send_sems.at[k], recv_sems.at[k], (x, y, 1 - c)).wait_recv()
        for cp in cps:
            cp.wait_send()
        for cp in local:
            cp.wait()

    return pl.pallas_call(
        body, name="grad_pair", in_specs=[ANY] * nk, out_specs=[ANY] * nk,
        out_shape=[jax.ShapeDtypeStruct((2 * a.shape[0], a.shape[1]), f32) for a in sums],
        scratch_shapes=[pltpu.SemaphoreType.DMA((nk,)), pltpu.SemaphoreType.DMA((nk,)),
                        pltpu.SemaphoreType.DMA((nk,))],
    )(*sums)


def _adam_math(w, g, m, v):
    m = ADAM_B1 * m + (1.0 - ADAM_B1) * g
    v = ADAM_B2 * v + (1.0 - ADAM_B2) * (g * g)
    m_hat = m / (1.0 - ADAM_B1 ** ADAM_STEP)
    v_hat = v / (1.0 - ADAM_B2 ** ADAM_STEP)
    delta = -ADAM_LR * (m_hat / (jnp.sqrt(v_hat) + ADAM_EPS) + ADAM_WD * w)
    return delta, m, v


def _adam(k, w, g, m, v):
    rows, cols = w.shape
    tr = next(rows // d for d in (1, 2, 4, 8) if rows % (8 * d) == 0 and rows // d <= 256)

    def body(w_ref, g_ref, m_ref, v_ref, d_ref, nm_ref, nv_ref):
        d_ref[...], nm_ref[...], nv_ref[...] = _adam_math(w_ref[...], g_ref[...], m_ref[...], v_ref[...])

    spec = _row(0, (tr, cols))
    return pl.pallas_call(
        body, name=f"adam_{k}", grid=(rows // tr,), in_specs=[spec] * 4, out_specs=[spec] * 3,
        out_shape=[jax.ShapeDtypeStruct(w.shape, f32)] * 3,
        compiler_params=_cp("parallel"),
    )(w, g, m, v)


def _small_sync_adam(part, w, m, v):
    rows = part.shape[0]

    def body(p_ref, w_ref, m_ref, v_ref, g_ref, d_ref, nm_ref, nv_ref, buf, send_sems, recv_sems):
        x, y, c, _, others = _place()
        me, sibling = (x, y, c), (x, y, 1 - c)

        def slot(px, py, pc):
            return buf.at[4 * px + 2 * py + pc]

        def copy(k, block, to, src=None):
            return _remote(slot(*block) if src is None else src, slot(*block), send_sems.at[k], recv_sems.at[k], to)

        buf[4 * x + 2 * y + c] = p_ref[...]
        first = [copy(0, me, sibling, src=p_ref)]
        first += [copy(1 + j, me, (*chip, c), src=p_ref) for j, chip in enumerate(others)]
        for cp in first:
            cp.start()
        passed = [copy(4 + j, (*chip, c), sibling) for j, chip in enumerate(others)]
        for j, chip in enumerate(others):
            copy(1 + j, (*chip, c), me).wait_recv()
            passed[j].start()
        copy(0, sibling, me).wait_recv()
        for j, chip in enumerate(others):
            copy(4 + j, (*chip, 1 - c), me).wait_recv()
        for cp in first + passed:
            cp.wait_send()
        g = buf[0]
        for dev in range(1, N_DEV):
            g = g + buf[dev]
        g_ref[...] = g
        d_ref[...], nm_ref[...], nv_ref[...] = _adam_math(w_ref[...], g, m_ref[...], v_ref[...])

    shape = jax.ShapeDtypeStruct(part.shape, f32)
    return pl.pallas_call(
        body, name="small_sync_adam", in_specs=[VMEM] * 4, out_specs=[VMEM] * 4, out_shape=[shape] * 4,
        scratch_shapes=[pltpu.VMEM((N_DEV, rows, 128), f32), pltpu.SemaphoreType.DMA((7,)),
                        pltpu.SemaphoreType.DMA((7,))],
        compiler_params=pltpu.CompilerParams(vmem_limit_bytes=VMEM_LIMIT),
    )(part, w, m, v)


def _rows128(a):
    return a.reshape(-1, 128)


def _pad_rows(a, rows):
    return jnp.concatenate([a, jnp.zeros((rows - a.shape[0],) + a.shape[1:], a.dtype)], axis=0)


def _pack_small(me, g_pg, g_post, g_final, g_ffn, ln_g, ln_b, conv_b, pool_scale, pool_w, conv_w, g_mix):
    blk_ple = _pad_rows(jnp.concatenate([g_pg, g_post, g_final.reshape(1, D_MODEL)], axis=0), 8)
    blk_ffn = _pad_rows(g_ffn, 8)
    blk_mix = _pad_rows(jnp.concatenate([ln_g, ln_b, conv_b, pool_scale], axis=0), 8)
    cw = lax.dynamic_update_slice(jnp.zeros((N_CHIPS, HALO, 128), f32), _pad_rows(conv_w, HALO)[None], (me, 0, 0))
    blk_in = _pad_rows(g_mix, 8)
    return jnp.concatenate([_rows128(blk_ple), _rows128(blk_ffn), _rows128(blk_mix), _rows128(pool_w),
                            _rows128(cw), _rows128(blk_in)], axis=0)


def _unpack_small(me, packed):
    o = 0

    def take(rows):
        nonlocal o
        blk = packed[o:o + rows]
        o += rows
        return blk

    ple = take(64).reshape(8, D_MODEL)
    ffn = take(64).reshape(8, D_MODEL)
    mixb = take(32).reshape(8, C_CONV)
    pool_w = take(512).reshape(1, 4, POOL_GROUP, POOL_GROUP)
    cw = take(N_CHIPS * HALO).reshape(N_CHIPS, HALO, 128)
    inb = take(64).reshape(8, D_MODEL)
    conv_w = lax.dynamic_slice(cw, (me, 0, 0), (1, HALO, 128))[:, :CONV_K, :]
    return dict(g_ple_gate=ple[0:1], g_ple_post=ple[1:2], g_final=ple[2], loss=ple[3, 0], g_ffn=ffn[0:1],
                ln_g=mixb[0:1], ln_b=mixb[1:2], conv_b=mixb[2:3], pool_scale=mixb[3:4], pool_w=pool_w,
                conv_w=conv_w, g_mix=inb[0:1])


def kernel(x, p, g_mix, w_in, conv_w, conv_b, ln_g, ln_b, pool_w, pool_scale, w_out, g_ffn, w_gate_up, w_down, g_ple_gate, w_ple_gate, w_ple_up, g_ple_post, g_final, loss_target, m_g_mix, m_w_in, m_conv_w, m_conv_b, m_ln_g, m_ln_b, m_pool_w, m_pool_scale, m_w_out, m_g_ffn, m_w_gate_up, m_w_down, m_g_ple_gate, m_w_ple_gate, m_w_ple_up, m_g_ple_post, m_g_final, v_g_mix, v_w_in, v_conv_w, v_conv_b, v_ln_g, v_ln_b, v_pool_w, v_pool_scale, v_w_out, v_g_ffn, v_w_gate_up, v_w_down, v_g_ple_gate, v_w_ple_gate, v_w_ple_up, v_g_ple_post, v_g_final):
    seq = x.shape[1]
    me = 2 * lax.axis_index("x") + lax.axis_index("y")
    core = lax.axis_index("c").astype(jnp.int32).reshape(1)
    xs, ps, ts = x.reshape(seq, D_MODEL), p.reshape(seq, D_PLE), loss_target.reshape(seq, D_MODEL)

    big = [w_in[0], w_gate_up[0], w_out[0], w_down[0], w_ple_gate[0], w_ple_up[0]]
    big_m = [m_w_in[0], m_w_gate_up[0], m_w_out[0], m_w_down[0], m_w_ple_gate[0], m_w_ple_up[0]]
    big_v = [v_w_in[0], v_w_gate_up[0], v_w_out[0], v_w_down[0], v_w_ple_gate[0], v_w_ple_up[0]]
    gathered = _gather_weights([w.astype(bf16) for w in big], conv_w[0])
    w_in_g, w_gu_g, w_out_g, w_down_g, w_pg_g, w_pu_g, cw_g = gathered
    w_out_f = w_out_g.reshape(D_MODEL, D_MODEL)
    w_down_f = w_down_g.reshape(D_FF, D_MODEL)
    w_pg_f = w_pg_g.reshape(D_MODEL, D_MODEL)
    conv_w_f = _pad_rows(cw_g.transpose(1, 0, 2).reshape(CONV_K, C_CONV), HALO)

    z = _mix_in(xs, g_mix, w_in_g)
    x1, mix, u1, pooled = _conv_pool_out(z, xs, conv_w_f, conv_b, ln_g, ln_b, pool_w[0], pool_scale, w_out_f)
    x2, h2, gu = _ffn_fwd(x1, g_ffn, w_gu_g, w_down_f)
    dx2, d_w_pg, d_w_pu, small_ple = _ple_loss(x2, ps, ts, g_ple_gate, g_ple_post, g_final.reshape(1, D_MODEL),
                                               w_pg_f, w_pu_g)
    dx1, dgu, small_ffn = _ffn_bwd_dx(dx2, x1, gu, g_ffn, w_gu_g, w_down_f)
    d_w_gu = _ffn_bwd_dw_gu(h2, dgu)
    d_w_down = _ffn_bwd_dw_down(gu, dx2)
    du1, dpo, d_w_out, d_pool_w, small_mix = _mix_bwd_local(dx1, mix, u1, pooled, w_out_f, ln_g, ln_b, pool_w[0],
                                                             pool_scale)
    grad_x, d_w_in, d_conv_w, small_in = _in_bwd(du1, dpo, z, xs, dx1, conv_w_f, g_mix, w_in_g)

    grads = [d_w_in, d_w_gu, d_w_out.reshape(N_CHIPS, -1, D_MODEL), d_w_down.reshape(N_CHIPS, -1, D_MODEL),
             d_w_pg.reshape(N_CHIPS, -1, D_MODEL), d_w_pu]
    from_sibling = _rs_pair(grads)
    parts = [_rs_add_pair(k, core, g, r) for k, (g, r) in enumerate(zip(grads, from_sibling))]
    landed = _rs_cross(parts)
    sums = [_rs_sum_chips(k, a) for k, a in enumerate(landed)]
    big_g = _grad_pair(sums)
    big_upd = [_adam(k, w, g, m, v) for k, (w, g, m, v) in enumerate(zip(big, big_g, big_m, big_v))]

    cw_chunks = d_conv_w.reshape(HALO, N_CHIPS, 128).transpose(1, 0, 2)
    part = jnp.concatenate([_rows128(small_ple), _rows128(small_ffn), _rows128(small_mix), _rows128(d_pool_w),
                            _rows128(cw_chunks), _rows128(small_in)], axis=0)
    sw = _pack_small(me, g_ple_gate, g_ple_post, g_final, g_ffn, ln_g, ln_b, conv_b, pool_scale, pool_w, conv_w[0], g_mix)
    sm = _pack_small(me, m_g_ple_gate, m_g_ple_post, m_g_final, m_g_ffn, m_ln_g, m_ln_b, m_conv_b, m_pool_scale,
                     m_pool_w, m_conv_w[0], m_g_mix)
    sv = _pack_small(me, v_g_ple_gate, v_g_ple_post, v_g_final, v_g_ffn, v_ln_g, v_ln_b, v_conv_b, v_pool_scale,
                     v_pool_w, v_conv_w[0], v_g_mix)
    small = [_unpack_small(me, a) for a in _small_sync_adam(part, sw, sm, sv)]

    names = ["g_mix", "w_in", "conv_w", "conv_b", "ln_g", "ln_b", "pool_w", "pool_scale", "w_out", "g_ffn",
             "w_gate_up", "w_down", "g_ple_gate", "w_ple_gate", "w_ple_up", "g_ple_post", "g_final"]
    big_at = {"w_in": 0, "w_gate_up": 1, "w_out": 2, "w_down": 3, "w_ple_gate": 4, "w_ple_up": 5}
    out = [small[0]["loss"], grad_x.reshape(1, seq, D_MODEL)]
    for kind in range(4):
        for nm in names:
            if nm in big_at:
                k = big_at[nm]
                out.append((big_g[k] if kind == 0 else big_upd[k][kind - 1])[None])
            else:
                out.append(small[kind][nm])
    return tuple(out)
```

```python
import functools

import jax
import jax.numpy as jnp
from jax import lax
from jax.experimental import pallas as pl
from jax.experimental.pallas import tpu as pltpu

f32, bf16 = jnp.float32, jnp.bfloat16

EPS = 1e-6
D_MODEL = 1024
C_CONV = 512
C_POOL = 512
POOL_WINDOWS = (2, 4, 8, 16)
POOL_GROUP = 128
CONV_K = 31
D_FF = 2816
D_PLE = 256
N_CHIPS = 4
N_DEV = 8
W_IN_COLS = 2 * C_CONV + C_POOL
W_IN_CHUNK = W_IN_COLS // N_CHIPS
FF_CHUNK = 2 * D_FF // N_CHIPS
PLE_CHUNK = D_MODEL // N_CHIPS
HALO = 32
ROW_TILE = 512
CONV_ROWS = 32
FF_SUB = (0, 512, 1024, FF_CHUNK)
VMEM_LIMIT = 56 * 1024 * 1024

ADAM_LR = 0.001
ADAM_B1 = 0.9
ADAM_B2 = 0.999
ADAM_EPS = 1e-08
ADAM_WD = 0.01
ADAM_STEP = 10

MESH = pl.DeviceIdType.MESH
ANY = pl.BlockSpec(memory_space=pl.ANY)
VMEM = pl.BlockSpec(memory_space=pltpu.VMEM)


def _cp(*sem):
    return pltpu.CompilerParams(dimension_semantics=sem, vmem_limit_bytes=VMEM_LIMIT)


def _dot(a, b):
    return jnp.dot(a, b, preferred_element_type=f32)


def _dot_nt(a, b):
    return lax.dot_general(a, b, (((1,), (1,)), ((), ())), preferred_element_type=f32)


def _dot_tn(a, b):
    return lax.dot_general(a, b, (((0,), (0,)), ((), ())), preferred_element_type=f32)


def _sigmoid(v):
    return jax.nn.sigmoid(v)


def _rms_fwd(v, g):
    r = lax.rsqrt(jnp.mean(v * v, axis=-1, keepdims=True) + EPS)
    vh = v * r
    return vh * g, vh, r


def _rms_bwd(dy, vh, r, g):
    dvh = dy * g
    dv = r * (dvh - vh * jnp.mean(dvh * vh, axis=-1, keepdims=True))
    return dv, jnp.sum(dy * vh, axis=0, keepdims=True)


def _silu_grad(v, s):
    return s * (1.0 + v * (1.0 - s))


def _row(i, n):
    return pl.BlockSpec((n[0], n[1]), lambda *a: (a[i], 0))


def _full(shape):
    nd = len(shape)
    return pl.BlockSpec(shape, lambda *a: (0,) * nd)


def _place():
    x, y, c = lax.axis_index("x"), lax.axis_index("y"), lax.axis_index("c")
    others = [(1 - x, y), (x, 1 - y), (1 - x, 1 - y)]
    return x, y, c, 2 * x + y, others


def _remote(src, dst, send_sem, recv_sem, dev):
    return pltpu.make_async_remote_copy(src_ref=src, dst_ref=dst, send_sem=send_sem, recv_sem=recv_sem,
                                        device_id=dev, device_id_type=MESH)


def _cast_into_slot(name, me, w, dtype):
    rows, cols = w.shape
    tr = next(rows // d for d in (1, 2, 4, 8) if rows % (16 * d) == 0 and rows // d <= 512)

    def body(me_ref, w_ref, o_ref):
        o_ref[0] = w_ref[...].astype(dtype)

    return pl.pallas_call(
        body, name=f"cast_{name}",
        grid_spec=pltpu.PrefetchScalarGridSpec(
            num_scalar_prefetch=1, grid=(rows // tr,),
            in_specs=[pl.BlockSpec((tr, cols), lambda r, me_ref: (r, 0))],
            out_specs=pl.BlockSpec((1, tr, cols), lambda r, me_ref: (me_ref[0], r, 0))),
        out_shape=jax.ShapeDtypeStruct((N_CHIPS, rows, cols), dtype),
        compiler_params=_cp("parallel"),
    )(me, w)


class _Gather:
    def __init__(self, bufs, send_sems, recv_sems):
        self.bufs, self.send_sems, self.recv_sems = bufs, send_sems, recv_sems
        self.x, self.y, self.c, self.me, self.others = _place()
        self.halves = [b.shape[1] // 2 for b in bufs]

    def _piece(self, k, chip, half):
        return self.bufs[k].at[chip, pl.ds(half * self.halves[k], self.halves[k]), :]

    def _ici(self, k, j, chip):
        ox, oy = self.others[j]
        piece = self._piece(k, chip, self.c)
        return _remote(piece, piece, self.send_sems.at[6 * k + j], self.recv_sems.at[6 * k + j], (ox, oy, self.c))

    def _pair(self, k, j, half):
        ox, oy = self.others[j]
        piece = self._piece(k, 2 * ox + oy, half)
        return _remote(piece, piece, self.send_sems.at[6 * k + 3 + j], self.recv_sems.at[6 * k + 3 + j],
                       (self.x, self.y, 1 - self.c))

    def _each(self):
        return [(k, j) for k in range(len(self.bufs)) for j in range(3)]

    def start(self):
        for k, j in self._each():
            self._ici(k, j, self.me).start()

    def forward(self):
        for k, j in self._each():
            ox, oy = self.others[j]
            self._ici(k, j, 2 * ox + oy).wait_recv()
            self._pair(k, j, self.c).start()

    def finish(self):
        for k, j in self._each():
            self._pair(k, j, 1 - self.c).wait_recv()
        for k, j in self._each():
            self._ici(k, j, self.me).wait_send()
            self._pair(k, j, self.c).wait_send()

    @staticmethod
    def scratch(n):
        return [pltpu.SemaphoreType.DMA((6 * n,)), pltpu.SemaphoreType.DMA((6 * n,))]


def _gather_in(bufs):
    n = len(bufs)

    def body(*refs):
        g = _Gather(refs[n:2 * n], *refs[2 * n:])
        g.start()
        g.forward()
        g.finish()

    return pl.pallas_call(
        body, name="gather_in", in_specs=[ANY] * n, out_specs=[ANY] * n,
        out_shape=[jax.ShapeDtypeStruct(b.shape, b.dtype) for b in bufs],
        input_output_aliases={k: k for k in range(n)}, scratch_shapes=_Gather.scratch(n),
    )(*bufs)


def _carried(bufs):
    n = len(bufs)
    return dict(in_specs=[ANY] * n, out_specs=[ANY] * n,
                out_shape=[jax.ShapeDtypeStruct(b.shape, b.dtype) for b in bufs], scratch=_Gather.scratch(n))


def _mix_in(x, g_mix, w_in_g, carry):
    s = x.shape[0]
    tm = min(ROW_TILE, s)
    n = s // tm
    nc = len(carry)
    cs = _carried(carry)

    def body(x_ref, g_ref, w_ref, *refs):
        z_ref = refs[nc]
        gather = _Gather(refs[nc + 1:2 * nc + 1], *refs[2 * nc + 1:])
        i = pl.program_id(0)
        pl.when(i == 0)(gather.start)
        pl.when(i == max(n - 2, 0))(gather.forward)
        h, _, _ = _rms_fwd(x_ref[...], g_ref[...])
        hb = h.astype(bf16)
        for j in range(N_CHIPS):
            z_ref[:, j * W_IN_CHUNK:(j + 1) * W_IN_CHUNK] = _dot(hb, w_ref[j])
        pl.when(i == n - 1)(gather.finish)

    res = pl.pallas_call(
        body, name="mix_in", grid=(n,),
        in_specs=[_row(0, (tm, D_MODEL)), _full((1, D_MODEL)), _full((N_CHIPS, D_MODEL, W_IN_CHUNK))] + cs["in_specs"],
        out_specs=[_row(0, (tm, W_IN_COLS))] + cs["out_specs"],
        out_shape=[jax.ShapeDtypeStruct((s, W_IN_COLS), f32)] + cs["out_shape"],
        input_output_aliases={3 + k: 1 + k for k in range(nc)}, scratch_shapes=cs["scratch"],
        compiler_params=_cp("arbitrary"),
    )(x, g_mix, w_in_g, *carry)
    return res[0], res[1:]


def _pool_counts(tm, w, first_row):
    t1 = (lax.broadcasted_iota(jnp.int32, (tm, 1), 0) + first_row + 1).astype(f32)
    return jnp.minimum(t1, float(w))


def _conv_pool_out(z, x, conv_w, conv_b, ln_g, ln_b, pool_w, pool_scale, w_out, carry):
    s = x.shape[0]
    tm = min(ROW_TILE, s)
    n = s // tm
    hb = tm // HALO
    nc = len(carry)
    cs = _carried(carry)

    def body(z_ref, zp_ref, x_ref, cw_ref, cb_ref, lg_ref, lb_ref, pw_ref, ps_ref, wo_ref, *refs):
        x1_ref, mix_ref, u1_ref, pooled_ref = refs[nc:nc + 4]
        ubuf, vbuf = refs[2 * nc + 4:2 * nc + 6]
        gather = _Gather(refs[nc + 4:2 * nc + 4], *refs[2 * nc + 6:])
        i = pl.program_id(0)
        pl.when(i == 0)(gather.start)
        pl.when(i == max(n - 2, 0))(gather.forward)
        keep = (i > 0).astype(f32)
        zp = zp_ref[...] * keep
        ubuf[0:HALO, :] = zp[:, :C_CONV] * _sigmoid(zp[:, C_CONV:2 * C_CONV])
        vbuf[0:HALO, :] = zp[:, 2 * C_CONV:]
        ubuf[HALO:, :] = z_ref[:, :C_CONV] * _sigmoid(z_ref[:, C_CONV:2 * C_CONV])
        vbuf[HALO:, :] = z_ref[:, 2 * C_CONV:]
        off = HALO - (CONV_K - 1)
        for r0 in range(0, tm, CONV_ROWS):
            acc = jnp.zeros((CONV_ROWS, C_CONV), f32) + cb_ref[...]
            for k in range(CONV_K):
                acc = acc + cw_ref[pl.ds(k, 1), :] * ubuf[pl.ds(r0 + off + k, CONV_ROWS), :]
            u1_ref[r0:r0 + CONV_ROWS, :] = acc
        u1 = u1_ref[...]
        mu = jnp.mean(u1, axis=-1, keepdims=True)
        uc = u1 - mu
        rstd = lax.rsqrt(jnp.mean(uc * uc, axis=-1, keepdims=True) + EPS)
        u2 = uc * rstd * lg_ref[...] + lb_ref[...]
        mix_ref[:, :C_CONV] = (u2 * _sigmoid(u2)).astype(bf16)
        for g, w in enumerate(POOL_WINDOWS):
            cols = slice(g * POOL_GROUP, (g + 1) * POOL_GROUP)
            acc = vbuf[pl.ds(HALO, tm), cols]
            vg = acc
            for d in range(1, w):
                acc = acc + vbuf[pl.ds(HALO - d, tm), cols]
            pooled = (acc / _pool_counts(tm, w, i * tm) - vg).astype(bf16)
            pooled_ref[:, cols] = pooled
            mixed = _dot(pooled, pw_ref[g].astype(bf16))
            mix_ref[:, C_CONV + g * POOL_GROUP:C_CONV + (g + 1) * POOL_GROUP] = (mixed * ps_ref[:, cols]).astype(bf16)
        x1_ref[...] = x_ref[...] + _dot(mix_ref[...], wo_ref[...])
        pl.when(i == n - 1)(gather.finish)

    res = pl.pallas_call(
        body, name="conv_pool_out", grid=(n,),
        in_specs=[_row(0, (tm, W_IN_COLS)),
                  pl.BlockSpec((HALO, W_IN_COLS), lambda i: (jnp.maximum(i * hb - 1, 0), 0)),
                  _row(0, (tm, D_MODEL)), _full((HALO, C_CONV)), _full((1, C_CONV)), _full((1, C_CONV)),
                  _full((1, C_CONV)), _full((4, POOL_GROUP, POOL_GROUP)), _full((1, C_POOL)),
                  _full((D_MODEL, D_MODEL))] + cs["in_specs"],
        out_specs=[_row(0, (tm, D_MODEL)), _row(0, (tm, D_MODEL)), _row(0, (tm, C_CONV)), _row(0, (tm, C_POOL))]
        + cs["out_specs"],
        out_shape=[jax.ShapeDtypeStruct((s, D_MODEL), f32), jax.ShapeDtypeStruct((s, D_MODEL), bf16),
                   jax.ShapeDtypeStruct((s, C_CONV), f32), jax.ShapeDtypeStruct((s, C_POOL), bf16)] + cs["out_shape"],
        input_output_aliases={10 + k: 4 + k for k in range(nc)},
        scratch_shapes=[pltpu.VMEM((HALO + tm, C_CONV), f32), pltpu.VMEM((HALO + tm, C_POOL), f32)] + cs["scratch"],
        compiler_params=_cp("arbitrary"),
    )(z, z, x, conv_w, conv_b, ln_g, ln_b, pool_w, pool_scale, w_out, *carry)
    return res[:4], res[4:]


def _ffn_fwd(x1, g_ffn, w_gu_g, w_down, carry):
    s = x1.shape[0]
    tm = min(ROW_TILE, s)
    n = s // tm
    nc = len(carry)
    cs = _carried(carry)

    def body(x1_ref, g_ref, wg_ref, wu_ref, wd_ref, *refs):
        x2_ref, h2_ref, gu_ref = refs[nc:nc + 3]
        acc_ref = refs[2 * nc + 3]
        gather = _Gather(refs[nc + 3:2 * nc + 3], *refs[2 * nc + 4:])
        i, c = pl.program_id(0), pl.program_id(1)
        pl.when(jnp.logical_and(i == 0, c == 0))(gather.start)
        pl.when(jnp.logical_and(i == n - 1, c == 0))(gather.forward)

        @pl.when(c == 0)
        def _():
            h, _, _ = _rms_fwd(x1_ref[...], g_ref[...])
            h2_ref[...] = h.astype(bf16)
            acc_ref[...] = jnp.zeros_like(acc_ref)

        h = h2_ref[...]
        for lo, hi in zip(FF_SUB[:-1], FF_SUB[1:]):
            gate = _dot(h, wg_ref[0, :, lo:hi])
            up = _dot(h, wu_ref[0, :, lo:hi])
            gu_ref[0, :, lo:hi] = gate.astype(bf16)
            gu_ref[1, :, lo:hi] = up.astype(bf16)
            f = (gate * _sigmoid(gate) * up).astype(bf16)
            acc_ref[...] += _dot(f, wd_ref[lo:hi, :])

        @pl.when(c == 1)
        def _():
            x2_ref[...] = x1_ref[...] + acc_ref[...]

        pl.when(jnp.logical_and(i == n - 1, c == 1))(gather.finish)

    res = pl.pallas_call(
        body, name="ffn_fwd", grid=(n, 2),
        in_specs=[_row(0, (tm, D_MODEL)), _full((1, D_MODEL)),
                  pl.BlockSpec((1, D_MODEL, FF_CHUNK), lambda i, c: (c, 0, 0)),
                  pl.BlockSpec((1, D_MODEL, FF_CHUNK), lambda i, c: (2 + c, 0, 0)),
                  pl.BlockSpec((FF_CHUNK, D_MODEL), lambda i, c: (c, 0))] + cs["in_specs"],
        out_specs=[_row(0, (tm, D_MODEL)), _row(0, (tm, D_MODEL)),
                   pl.BlockSpec((2, tm, FF_CHUNK), lambda i, c: (0, i, c))] + cs["out_specs"],
        out_shape=[jax.ShapeDtypeStruct((s, D_MODEL), f32), jax.ShapeDtypeStruct((s, D_MODEL), bf16),
                   jax.ShapeDtypeStruct((2, s, D_FF), bf16)] + cs["out_shape"],
        input_output_aliases={5 + k: 3 + k for k in range(nc)},
        scratch_shapes=[pltpu.VMEM((tm, D_MODEL), f32)] + cs["scratch"],
        compiler_params=_cp("arbitrary", "arbitrary"),
    )(x1, g_ffn, w_gu_g, w_gu_g, w_down, *carry)
    return res[:3], res[3:]


def _ple_loss(x2, p, target, g_pg, g_post, g_final, w_pg, w_pu_g):
    s = x2.shape[0]
    tm = min(ROW_TILE, s)
    n = s // tm

    def body(x2_ref, p_ref, t_ref, gpg_ref, gpo_ref, gf_ref, wpg_ref, wpu_ref,
             dx2_ref, dwpg_ref, dwpu_ref, small_ref, apg_ref, apu_ref):
        i = pl.program_id(0)

        @pl.when(i == 0)
        def _():
            apg_ref[...] = jnp.zeros_like(apg_ref)
            apu_ref[...] = jnp.zeros_like(apu_ref)
            small_ref[...] = jnp.zeros_like(small_ref)

        x2 = x2_ref[...]
        h3, x2h, r2 = _rms_fwd(x2, gpg_ref[...])
        h3b = h3.astype(bf16)
        gate = _sigmoid(_dot(h3b, wpg_ref[...]))
        pb = p_ref[...].astype(bf16)
        pe = jnp.concatenate([_dot(pb, wpu_ref[j]) for j in range(N_CHIPS)], axis=-1)
        e, peh, rp = _rms_fwd(pe, gpo_ref[...])
        x3 = x2 + gate * e
        y, x3h, r3 = _rms_fwd(x3, gf_ref[...])
        d = y - t_ref[...]
        loss = 0.5 * jnp.sum(jnp.sum(d * d, axis=-1, keepdims=True) * (1.0 / D_MODEL), axis=0, keepdims=True)
        dx3, dgf = _rms_bwd(d * (1.0 / D_MODEL), x3h, r3, gf_ref[...])
        dpe, dgpo = _rms_bwd(dx3 * gate, peh, rp, gpo_ref[...])
        dgl = (dx3 * e * gate * (1.0 - gate)).astype(bf16)
        apg_ref[...] += _dot_tn(h3b, dgl)
        apu_ref[...] += _dot_tn(pb, dpe.astype(bf16))
        dh3 = _dot_nt(dgl, wpg_ref[...])
        dx2b, dgpg = _rms_bwd(dh3, x2h, r2, gpg_ref[...])
        dx2_ref[...] = dx3 + dx2b
        small_ref[0:1, :] += dgpg
        small_ref[1:2, :] += dgpo
        small_ref[2:3, :] += dgf
        small_ref[3:4, :] += jnp.broadcast_to(loss, (1, D_MODEL))

        @pl.when(i == n - 1)
        def _():
            dwpg_ref[...] = apg_ref[...].astype(bf16)
            for j in range(N_CHIPS):
                dwpu_ref[j] = apu_ref[:, j * PLE_CHUNK:(j + 1) * PLE_CHUNK].astype(bf16)

    return pl.pallas_call(
        body, name="ple_loss", grid=(n,),
        in_specs=[_row(0, (tm, D_MODEL)), _row(0, (tm, D_PLE)), _row(0, (tm, D_MODEL)),
                  _full((1, D_MODEL)), _full((1, D_MODEL)), _full((1, D_MODEL)),
                  _full((D_MODEL, D_MODEL)), _full((N_CHIPS, D_PLE, PLE_CHUNK))],
        out_specs=[_row(0, (tm, D_MODEL)), _full((D_MODEL, D_MODEL)), _full((N_CHIPS, D_PLE, PLE_CHUNK)),
                   _full((8, D_MODEL))],
        out_shape=[jax.ShapeDtypeStruct((s, D_MODEL), f32), jax.ShapeDtypeStruct((D_MODEL, D_MODEL), bf16),
                   jax.ShapeDtypeStruct((N_CHIPS, D_PLE, PLE_CHUNK), bf16), jax.ShapeDtypeStruct((8, D_MODEL), f32)],
        scratch_shapes=[pltpu.VMEM((D_MODEL, D_MODEL), f32), pltpu.VMEM((D_PLE, D_MODEL), f32)],
        compiler_params=_cp("arbitrary"),
    )(x2, p, target, g_pg, g_post, g_final, w_pg, w_pu_g)


def _ffn_bwd_dx(dx2, x1, gu, g_ffn, w_gu_g, w_down):
    s = x1.shape[0]
    tm = min(ROW_TILE, s)

    def body(dx2_ref, x1_ref, gu_ref, g_ref, wg_ref, wu_ref, wd_ref, dx1_ref, dgu_ref, small_ref, acc_ref):
        i, c = pl.program_id(0), pl.program_id(1)

        @pl.when(jnp.logical_and(i == 0, c == 0))
        def _():
            small_ref[...] = jnp.zeros_like(small_ref)

        @pl.when(c == 0)
        def _():
            acc_ref[...] = jnp.zeros_like(acc_ref)

        dyb = dx2_ref[...].astype(bf16)
        for lo, hi in zip(FF_SUB[:-1], FF_SUB[1:]):
            df = _dot_nt(dyb, wd_ref[lo:hi, :])
            gate = gu_ref[0, :, lo:hi].astype(f32)
            up = gu_ref[1, :, lo:hi].astype(f32)
            sg = _sigmoid(gate)
            dgate = (df * up * _silu_grad(gate, sg)).astype(bf16)
            dup = (df * gate * sg).astype(bf16)
            dgu_ref[0, :, lo:hi] = dgate
            dgu_ref[1, :, lo:hi] = dup
            acc_ref[...] += _dot_nt(dgate, wg_ref[0, :, lo:hi]) + _dot_nt(dup, wu_ref[0, :, lo:hi])

        @pl.when(c == 1)
        def _():
            _, x1h, r1 = _rms_fwd(x1_ref[...], g_ref[...])
            dx1b, dg = _rms_bwd(acc_ref[...], x1h, r1, g_ref[...])
            dx1_ref[...] = dx2_ref[...] + dx1b
            small_ref[0:1, :] += dg

    return pl.pallas_call(
        body, name="ffn_bwd_dx", grid=(s // tm, 2),
        in_specs=[_row(0, (tm, D_MODEL)), _row(0, (tm, D_MODEL)),
                  pl.BlockSpec((2, tm, FF_CHUNK), lambda i, c: (0, i, c)), _full((1, D_MODEL)),
                  pl.BlockSpec((1, D_MODEL, FF_CHUNK), lambda i, c: (c, 0, 0)),
                  pl.BlockSpec((1, D_MODEL, FF_CHUNK), lambda i, c: (2 + c, 0, 0)),
                  pl.BlockSpec((FF_CHUNK, D_MODEL), lambda i, c: (c, 0))],
        out_specs=[_row(0, (tm, D_MODEL)), pl.BlockSpec((2, tm, FF_CHUNK), lambda i, c: (0, i, c)),
                   _full((8, D_MODEL))],
        out_shape=[jax.ShapeDtypeStruct((s, D_MODEL), f32), jax.ShapeDtypeStruct((2, s, D_FF), bf16),
                   jax.ShapeDtypeStruct((8, D_MODEL), f32)],
        scratch_shapes=[pltpu.VMEM((tm, D_MODEL), f32)],
        compiler_params=_cp("arbitrary", "arbitrary"),
    )(dx2, x1, gu, g_ffn, w_gu_g, w_gu_g, w_down)


def _ffn_bwd_dw_gu(h2, dgu):
    s = h2.shape[0]
    ts = min(ROW_TILE, s)
    n = s // ts

    def body(h_ref, d_ref, o_ref, acc_ref):
        t = pl.program_id(1)

        @pl.when(t == 0)
        def _():
            acc_ref[...] = jnp.zeros_like(acc_ref)

        acc_ref[...] += _dot_tn(h_ref[...], d_ref[0])

        @pl.when(t == n - 1)
        def _():
            o_ref[0] = acc_ref[...].astype(bf16)

    return pl.pallas_call(
        body, name="ffn_bwd_dw_gu", grid=(N_CHIPS, n),
        in_specs=[pl.BlockSpec((ts, D_MODEL), lambda j, t: (t, 0)),
                  pl.BlockSpec((1, ts, FF_CHUNK), lambda j, t: (j // 2, t, j % 2))],
        out_specs=pl.BlockSpec((1, D_MODEL, FF_CHUNK), lambda j, t: (j, 0, 0)),
        out_shape=jax.ShapeDtypeStruct((N_CHIPS, D_MODEL, FF_CHUNK), bf16),
        scratch_shapes=[pltpu.VMEM((D_MODEL, FF_CHUNK), f32)],
        compiler_params=_cp("parallel", "arbitrary"),
    )(h2, dgu)


def _ffn_bwd_dw_down(gu, dx2):
    s = dx2.shape[0]
    ts = min(ROW_TILE, s)
    n = s // ts

    def body(gu_ref, d_ref, o_ref, acc_ref):
        t = pl.program_id(1)

        @pl.when(t == 0)
        def _():
            acc_ref[...] = jnp.zeros_like(acc_ref)

        gate = gu_ref[0].astype(f32)
        f = (gate * _sigmoid(gate) * gu_ref[1].astype(f32)).astype(bf16)
        acc_ref[...] += _dot_tn(f, d_ref[...].astype(bf16))

        @pl.when(t == n - 1)
        def _():
            o_ref[...] = acc_ref[...].astype(bf16)

    return pl.pallas_call(
        body, name="ffn_bwd_dw_down", grid=(2, n),
        in_specs=[pl.BlockSpec((2, ts, FF_CHUNK), lambda c, t: (0, t, c)),
                  pl.BlockSpec((ts, D_MODEL), lambda c, t: (t, 0))],
        out_specs=pl.BlockSpec((FF_CHUNK, D_MODEL), lambda c, t: (c, 0)),
        out_shape=jax.ShapeDtypeStruct((D_FF, D_MODEL), bf16),
        scratch_shapes=[pltpu.VMEM((FF_CHUNK, D_MODEL), f32)],
        compiler_params=_cp("parallel", "arbitrary"),
    )(gu, dx2)


def _mix_bwd_local(dx1, mix, u1, pooled, w_out, ln_g, ln_b, pool_w, pool_scale):
    s = dx1.shape[0]
    tm = min(ROW_TILE, s)
    n = s // tm

    def body(dx1_ref, mix_ref, u1_ref, po_ref, wo_ref, lg_ref, lb_ref, pw_ref, ps_ref,
             du1_ref, dpo_ref, dwo_ref, dpw_ref, small_ref, awo_ref):
        i = pl.program_id(0)

        @pl.when(i == 0)
        def _():
            awo_ref[...] = jnp.zeros_like(awo_ref)
            dpw_ref[...] = jnp.zeros_like(dpw_ref)
            small_ref[...] = jnp.zeros_like(small_ref)

        dyb = dx1_ref[...].astype(bf16)
        dmix = _dot_nt(dyb, wo_ref[...])
        awo_ref[...] += _dot_tn(mix_ref[...], dyb)
        u1 = u1_ref[...]
        mu = jnp.mean(u1, axis=-1, keepdims=True)
        uc = u1 - mu
        rstd = lax.rsqrt(jnp.mean(uc * uc, axis=-1, keepdims=True) + EPS)
        uh = uc * rstd
        u2 = uh * lg_ref[...] + lb_ref[...]
        du2 = dmix[:, :C_CONV] * _silu_grad(u2, _sigmoid(u2))
        duh = du2 * lg_ref[...]
        du1 = rstd * (duh - jnp.mean(duh, axis=-1, keepdims=True) - uh * jnp.mean(duh * uh, axis=-1, keepdims=True))
        du1_ref[...] = du1
        small_ref[0:1, :] += jnp.sum(du2 * uh, axis=0, keepdims=True)
        small_ref[1:2, :] += jnp.sum(du2, axis=0, keepdims=True)
        small_ref[2:3, :] += jnp.sum(du1, axis=0, keepdims=True)
        for g in range(len(POOL_WINDOWS)):
            cols = slice(g * POOL_GROUP, (g + 1) * POOL_GROUP)
            dq = dmix[:, C_CONV + g * POOL_GROUP:C_CONV + (g + 1) * POOL_GROUP]
            pwb = pw_ref[g].astype(bf16)
            pg = po_ref[:, cols]
            mixed = _dot(pg, pwb)
            small_ref[3:4, cols] += jnp.sum(dq * mixed, axis=0, keepdims=True)
            dmixed = (dq * ps_ref[:, cols]).astype(bf16)
            dpw_ref[g] += _dot_tn(pg, dmixed)
            dpo_ref[:, cols] = _dot_nt(dmixed, pwb)

        @pl.when(i == n - 1)
        def _():
            dwo_ref[...] = awo_ref[...].astype(bf16)

    return pl.pallas_call(
        body, name="mix_bwd_local", grid=(n,),
        in_specs=[_row(0, (tm, D_MODEL)), _row(0, (tm, D_MODEL)), _row(0, (tm, C_CONV)), _row(0, (tm, C_POOL)),
                  _full((D_MODEL, D_MODEL)), _full((1, C_CONV)), _full((1, C_CONV)),
                  _full((4, POOL_GROUP, POOL_GROUP)), _full((1, C_POOL))],
        out_specs=[_row(0, (tm, C_CONV)), _row(0, (tm, C_POOL)), _full((D_MODEL, D_MODEL)),
                   _full((4, POOL_GROUP, POOL_GROUP)), _full((8, C_CONV))],
        out_shape=[jax.ShapeDtypeStruct((s, C_CONV), f32), jax.ShapeDtypeStruct((s, C_POOL), f32),
                   jax.ShapeDtypeStruct((D_MODEL, D_MODEL), bf16),
                   jax.ShapeDtypeStruct((4, POOL_GROUP, POOL_GROUP), f32), jax.ShapeDtypeStruct((8, C_CONV), f32)],
        scratch_shapes=[pltpu.VMEM((D_MODEL, D_MODEL), f32)],
        compiler_params=_cp("arbitrary"),
    )(dx1, mix, u1, pooled, w_out, ln_g, ln_b, pool_w, pool_scale)


def _in_bwd(du1, dpo, z, x, dx1, conv_w, g_mix, w_in_g):
    s = x.shape[0]
    tm = min(ROW_TILE, s)
    n = s // tm
    hb = tm // HALO
    last = s // HALO - 1

    def body(du_ref, dun_ref, dp_ref, dpn_ref, z_ref, zp_ref, x_ref, dx1_ref, cw_ref, g_ref, w_ref,
             gx_ref, dw_ref, dcw_ref, small_ref, dbuf, pbuf, ubuf, dz_ref, acc_ref):
        i = pl.program_id(0)

        @pl.when(i == 0)
        def _():
            acc_ref[...] = jnp.zeros_like(acc_ref)
            dcw_ref[...] = jnp.zeros_like(dcw_ref)
            small_ref[...] = jnp.zeros_like(small_ref)

        keep_prev = (i > 0).astype(f32)
        keep_next = (i < n - 1).astype(f32)
        zp = zp_ref[...] * keep_prev
        ubuf[0:HALO, :] = zp[:, :C_CONV] * _sigmoid(zp[:, C_CONV:2 * C_CONV])
        a = z_ref[:, :C_CONV]
        sb = _sigmoid(z_ref[:, C_CONV:2 * C_CONV])
        ubuf[HALO:, :] = a * sb
        dbuf[0:tm, :] = du_ref[...]
        dbuf[tm:, :] = dun_ref[...] * keep_next
        for g, w in enumerate(POOL_WINDOWS):
            cols = slice(g * POOL_GROUP, (g + 1) * POOL_GROUP)
            pbuf[0:tm, cols] = dp_ref[:, cols] / _pool_counts(tm, w, i * tm)
            pbuf[tm:, cols] = dpn_ref[:, cols] * keep_next / _pool_counts(HALO, w, (i + 1) * tm)
        off = HALO - (CONV_K - 1)
        for r0 in range(0, tm, CONV_ROWS):
            acc = jnp.zeros((CONV_ROWS, C_CONV), f32)
            for k in range(CONV_K):
                acc = acc + cw_ref[pl.ds(k, 1), :] * dbuf[pl.ds(r0 + CONV_K - 1 - k, CONV_ROWS), :]
            rows = pl.ds(r0, CONV_ROWS)
            dz_ref[rows, 0:C_CONV] = (acc * sb[r0:r0 + CONV_ROWS]).astype(bf16)
            dz_ref[rows, C_CONV:2 * C_CONV] = (
                acc * a[r0:r0 + CONV_ROWS] * sb[r0:r0 + CONV_ROWS] * (1.0 - sb[r0:r0 + CONV_ROWS])).astype(bf16)
        for k in range(CONV_K):
            part = jnp.zeros((8, C_CONV), f32)
            for r0 in range(0, tm, 8):
                part = part + dbuf[pl.ds(r0, 8), :] * ubuf[pl.ds(r0 + off + k, 8), :]
            dcw_ref[pl.ds(k, 1), :] += jnp.sum(part, axis=0, keepdims=True)
        for g, w in enumerate(POOL_WINDOWS):
            cols = slice(g * POOL_GROUP, (g + 1) * POOL_GROUP)
            acc = pbuf[pl.ds(0, tm), cols]
            for d in range(1, w):
                acc = acc + pbuf[pl.ds(d, tm), cols]
            dz_ref[:, 2 * C_CONV + g * POOL_GROUP:2 * C_CONV + (g + 1) * POOL_GROUP] = (
                acc - dp_ref[:, cols]).astype(bf16)
        h, xh, r = _rms_fwd(x_ref[...], g_ref[...])
        dz = dz_ref[...]
        acc_ref[...] += _dot_tn(h.astype(bf16), dz)
        dh = _dot_nt(dz[:, 0:W_IN_CHUNK], w_ref[0])
        for j in range(1, N_CHIPS):
            dh = dh + _dot_nt(dz[:, j * W_IN_CHUNK:(j + 1) * W_IN_CHUNK], w_ref[j])
        dxb, dg = _rms_bwd(dh, xh, r, g_ref[...])
        gx_ref[...] = dx1_ref[...] + dxb
        small_ref[0:1, :] += dg

        @pl.when(i == n - 1)
        def _():
            for j in range(N_CHIPS):
                dw_ref[j] = acc_ref[:, j * W_IN_CHUNK:(j + 1) * W_IN_CHUNK].astype(bf16)

    nxt = lambda i: (jnp.minimum((i + 1) * hb, last), 0)
    return pl.pallas_call(
        body, name="in_bwd", grid=(n,),
        in_specs=[_row(0, (tm, C_CONV)), pl.BlockSpec((HALO, C_CONV), nxt),
                  _row(0, (tm, C_POOL)), pl.BlockSpec((HALO, C_POOL), nxt),
                  _row(0, (tm, W_IN_COLS)),
                  pl.BlockSpec((HALO, W_IN_COLS), lambda i: (jnp.maximum(i * hb - 1, 0), 0)),
                  _row(0, (tm, D_MODEL)), _row(0, (tm, D_MODEL)), _full((HALO, C_CONV)), _full((1, D_MODEL)),
                  _full((N_CHIPS, D_MODEL, W_IN_CHUNK))],
        out_specs=[_row(0, (tm, D_MODEL)), _full((N_CHIPS, D_MODEL, W_IN_CHUNK)), _full((HALO, C_CONV)),
                   _full((8, D_MODEL))],
        out_shape=[jax.ShapeDtypeStruct((s, D_MODEL), f32), jax.ShapeDtypeStruct((N_CHIPS, D_MODEL, W_IN_CHUNK), bf16),
                   jax.ShapeDtypeStruct((HALO, C_CONV), f32), jax.ShapeDtypeStruct((8, D_MODEL), f32)],
        scratch_shapes=[pltpu.VMEM((tm + HALO, C_CONV), f32), pltpu.VMEM((tm + HALO, C_POOL), f32),
                        pltpu.VMEM((HALO + tm, C_CONV), f32), pltpu.VMEM((tm, W_IN_COLS), bf16),
                        pltpu.VMEM((D_MODEL, W_IN_COLS), f32)],
        compiler_params=_cp("arbitrary"),
    )(du1, du1, dpo, dpo, z, z, x, dx1, conv_w, g_mix, w_in_g)


def _rs_pair(name, grads):
    nk = len(grads)
    halves = [g.shape[1] // 2 for g in grads]

    def body(*refs):
        ins, outs = refs[:nk], refs[nk:2 * nk]
        send_sems, recv_sems = refs[2 * nk:]
        x, y, c, _, _ = _place()
        cps = [_remote(ins[k].at[:, pl.ds((1 - c) * halves[k], halves[k]), :], outs[k],
                       send_sems.at[k], recv_sems.at[k], (x, y, 1 - c)) for k in range(nk)]
        for cp in cps:
            cp.start()
        for cp in cps:
            cp.wait()

    return pl.pallas_call(
        body, name=f"rs_pair_{name}", in_specs=[ANY] * nk, out_specs=[ANY] * nk,
        out_shape=[jax.ShapeDtypeStruct((N_CHIPS, h, g.shape[2]), bf16) for g, h in zip(grads, halves)],
        scratch_shapes=[pltpu.SemaphoreType.DMA((nk,)), pltpu.SemaphoreType.DMA((nk,))],
    )(*grads)


def _rs_add_pair(name, core, grad, recv):
    _, half, cols = recv.shape

    def body(core_ref, g_ref, r_ref, o_ref):
        o_ref[...] = (g_ref[...].astype(f32) + r_ref[...].astype(f32)).astype(bf16)

    return pl.pallas_call(
        body, name=f"rs_add_pair_{name}",
        grid_spec=pltpu.PrefetchScalarGridSpec(
            num_scalar_prefetch=1, grid=(N_CHIPS,),
            in_specs=[pl.BlockSpec((1, half, cols), lambda j, core_ref: (j, core_ref[0], 0)),
                      pl.BlockSpec((1, half, cols), lambda j, core_ref: (j, 0, 0))],
            out_specs=pl.BlockSpec((1, half, cols), lambda j, core_ref: (j, 0, 0))),
        out_shape=jax.ShapeDtypeStruct(recv.shape, bf16),
        compiler_params=_cp("parallel"),
    )(core, grad, recv)


class _Cross:
    def __init__(self, parts, landed, send_sems, recv_sems):
        self.parts, self.landed, self.send_sems, self.recv_sems = parts, landed, send_sems, recv_sems
        _, _, self.c, self.me, self.others = _place()

    def _copy(self, k, j, src_chunk, dst_slot):
        ox, oy = self.others[j]
        return _remote(self.parts[k].at[src_chunk], self.landed[k].at[dst_slot], self.send_sems.at[3 * k + j],
                       self.recv_sems.at[3 * k + j], (ox, oy, self.c))

    def _each(self):
        return [(k, j, 2 * self.others[j][0] + self.others[j][1]) for k in range(len(self.parts)) for j in range(3)]

    def start(self):
        for k, j, chip in self._each():
            self._copy(k, j, chip, self.me).start()

    def finish(self):
        for k, j, chip in self._each():
            self._copy(k, j, chip, chip).wait_recv()
        for k, j, chip in self._each():
            self._copy(k, j, chip, self.me).wait_send()

    @staticmethod
    def scratch(n):
        return [pltpu.SemaphoreType.DMA((3 * n,)), pltpu.SemaphoreType.DMA((3 * n,))]


def _rs_cross(name, parts):
    nk = len(parts)

    def body(*refs):
        cross = _Cross(refs[:nk], refs[nk:2 * nk], *refs[2 * nk:])
        cross.start()
        cross.finish()

    return pl.pallas_call(
        body, name=f"rs_cross_{name}", in_specs=[ANY] * nk, out_specs=[ANY] * nk,
        out_shape=[jax.ShapeDtypeStruct(a.shape, bf16) for a in parts], scratch_shapes=_Cross.scratch(nk),
    )(*parts)


def _rs_sum_chips(name, place, landed, part):
    _, half, cols = landed.shape

    def body(place_ref, l_ref, p_ref, o_ref):
        me = place_ref[0]
        own = p_ref[0].astype(f32)
        acc = jnp.where(me == 0, own, l_ref[0].astype(f32))
        for j in range(1, N_CHIPS):
            acc = acc + jnp.where(me == j, own, l_ref[j].astype(f32))
        o_ref[...] = acc

    return pl.pallas_call(
        body, name=f"rs_sum_chips_{name}",
        grid_spec=pltpu.PrefetchScalarGridSpec(
            num_scalar_prefetch=1, grid=(1,),
            in_specs=[pl.BlockSpec((N_CHIPS, half, cols), lambda t, place_ref: (0, 0, 0)),
                      pl.BlockSpec((1, half, cols), lambda t, place_ref: (place_ref[0], 0, 0))],
            out_specs=pl.BlockSpec((half, cols), lambda t, place_ref: (place_ref[1], 0))),
        out_shape=jax.ShapeDtypeStruct((2 * half, cols), f32),
        compiler_params=_cp("arbitrary"),
    )(place, landed, part)


def _grad_pair(shards):
    nk = len(shards)

    def body(*refs):
        outs = refs[nk:2 * nk]
        send_sems, recv_sems = refs[2 * nk:]
        x, y, c, _, _ = _place()

        def half(k, core):
            h = outs[k].shape[0] // 2
            return outs[k].at[pl.ds(core * h, h), :]

        cps = [_remote(half(k, c), half(k, c), send_sems.at[k], recv_sems.at[k], (x, y, 1 - c)) for k in range(nk)]
        for cp in cps:
            cp.start()
        for k in range(nk):
            _remote(half(k, 1 - c), half(k, 1 - c), send_sems.at[k], recv_sems.at[k], (x, y, 1 - c)).wait_recv()
        for cp in cps:
            cp.wait_send()

    return pl.pallas_call(
        body, name="grad_pair", in_specs=[ANY] * nk, out_specs=[ANY] * nk,
        out_shape=[jax.ShapeDtypeStruct(a.shape, f32) for a in shards],
        input_output_aliases={k: k for k in range(nk)},
        scratch_shapes=[pltpu.SemaphoreType.DMA((nk,)), pltpu.SemaphoreType.DMA((nk,))],
    )(*shards)


def _adam_math(w, g, m, v):
    m = ADAM_B1 * m + (1.0 - ADAM_B1) * g
    v = ADAM_B2 * v + (1.0 - ADAM_B2) * (g * g)
    m_hat = m / (1.0 - ADAM_B1 ** ADAM_STEP)
    v_hat = v / (1.0 - ADAM_B2 ** ADAM_STEP)
    delta = -ADAM_LR * (m_hat / (jnp.sqrt(v_hat) + ADAM_EPS) + ADAM_WD * w)
    return delta, m, v


def _adam(k, w, g, m, v):
    rows, cols = w.shape
    tr = next(rows // d for d in (1, 2, 4, 8) if rows % (8 * d) == 0 and rows // d <= 256)

    def body(w_ref, g_ref, m_ref, v_ref, d_ref, nm_ref, nv_ref):
        d_ref[...], nm_ref[...], nv_ref[...] = _adam_math(w_ref[...], g_ref[...], m_ref[...], v_ref[...])

    spec = _row(0, (tr, cols))
    return pl.pallas_call(
        body, name=f"adam_{k}", grid=(rows // tr,), in_specs=[spec] * 4, out_specs=[spec] * 3,
        out_shape=[jax.ShapeDtypeStruct(w.shape, f32)] * 3,
        compiler_params=_cp("parallel"),
    )(w, g, m, v)


def _small_sync_adam(part, w, m, v):
    rows = part.shape[0]

    def body(p_ref, w_ref, m_ref, v_ref, g_ref, d_ref, nm_ref, nv_ref, buf, send_sems, recv_sems):
        x, y, c, _, others = _place()
        me, sibling = (x, y, c), (x, y, 1 - c)

        def slot(px, py, pc):
            return buf.at[4 * px + 2 * py + pc]

        def copy(k, block, to, src=None):
            return _remote(slot(*block) if src is None else src, slot(*block), send_sems.at[k], recv_sems.at[k], to)

        buf[4 * x + 2 * y + c] = p_ref[...]
        first = [copy(0, me, sibling, src=p_ref)]
        first += [copy(1 + j, me, (*chip, c), src=p_ref) for j, chip in enumerate(others)]
        for cp in first:
            cp.start()
        passed = [copy(4 + j, (*chip, c), sibling) for j, chip in enumerate(others)]
        for j, chip in enumerate(others):
            copy(1 + j, (*chip, c), me).wait_recv()
            passed[j].start()
        copy(0, sibling, me).wait_recv()
        for j, chip in enumerate(others):
            copy(4 + j, (*chip, 1 - c), me).wait_recv()
        for cp in first + passed:
            cp.wait_send()
        g = buf[0]
        for dev in range(1, N_DEV):
            g = g + buf[dev]
        g_ref[...] = g
        d_ref[...], nm_ref[...], nv_ref[...] = _adam_math(w_ref[...], g, m_ref[...], v_ref[...])

    shape = jax.ShapeDtypeStruct(part.shape, f32)
    return pl.pallas_call(
        body, name="small_sync_adam", in_specs=[VMEM] * 4, out_specs=[VMEM] * 4, out_shape=[shape] * 4,
        scratch_shapes=[pltpu.VMEM((N_DEV, rows, 128), f32), pltpu.SemaphoreType.DMA((7,)),
                        pltpu.SemaphoreType.DMA((7,))],
        compiler_params=pltpu.CompilerParams(vmem_limit_bytes=VMEM_LIMIT),
    )(part, w, m, v)


def _rows128(a):
    return a.reshape(-1, 128)


def _pad_rows(a, rows):
    return jnp.concatenate([a, jnp.zeros((rows - a.shape[0],) + a.shape[1:], a.dtype)], axis=0)


def _pack_small(me, g_pg, g_post, g_final, g_ffn, ln_g, ln_b, conv_b, pool_scale, pool_w, conv_w, g_mix):
    blk_ple = _pad_rows(jnp.concatenate([g_pg, g_post, g_final.reshape(1, D_MODEL)], axis=0), 8)
    blk_ffn = _pad_rows(g_ffn, 8)
    blk_mix = _pad_rows(jnp.concatenate([ln_g, ln_b, conv_b, pool_scale], axis=0), 8)
    cw = lax.dynamic_update_slice(jnp.zeros((N_CHIPS, HALO, 128), f32), _pad_rows(conv_w, HALO)[None], (me, 0, 0))
    blk_in = _pad_rows(g_mix, 8)
    return jnp.concatenate([_rows128(blk_ple), _rows128(blk_ffn), _rows128(blk_mix), _rows128(pool_w),
                            _rows128(cw), _rows128(blk_in)], axis=0)


def _unpack_small(me, packed):
    o = 0

    def take(rows):
        nonlocal o
        blk = packed[o:o + rows]
        o += rows
        return blk

    ple = take(64).reshape(8, D_MODEL)
    ffn = take(64).reshape(8, D_MODEL)
    mixb = take(32).reshape(8, C_CONV)
    pool_w = take(512).reshape(1, 4, POOL_GROUP, POOL_GROUP)
    cw = take(N_CHIPS * HALO).reshape(N_CHIPS, HALO, 128)
    inb = take(64).reshape(8, D_MODEL)
    conv_w = lax.dynamic_slice(cw, (me, 0, 0), (1, HALO, 128))[:, :CONV_K, :]
    return dict(g_ple_gate=ple[0:1], g_ple_post=ple[1:2], g_final=ple[2], loss=ple[3, 0], g_ffn=ffn[0:1],
                ln_g=mixb[0:1], ln_b=mixb[1:2], conv_b=mixb[2:3], pool_scale=mixb[3:4], pool_w=pool_w,
                conv_w=conv_w, g_mix=inb[0:1])


def kernel(x, p, g_mix, w_in, conv_w, conv_b, ln_g, ln_b, pool_w, pool_scale, w_out, g_ffn, w_gate_up, w_down, g_ple_gate, w_ple_gate, w_ple_up, g_ple_post, g_final, loss_target, m_g_mix, m_w_in, m_conv_w, m_conv_b, m_ln_g, m_ln_b, m_pool_w, m_pool_scale, m_w_out, m_g_ffn, m_w_gate_up, m_w_down, m_g_ple_gate, m_w_ple_gate, m_w_ple_up, m_g_ple_post, m_g_final, v_g_mix, v_w_in, v_conv_w, v_conv_b, v_ln_g, v_ln_b, v_pool_w, v_pool_scale, v_w_out, v_g_ffn, v_w_gate_up, v_w_down, v_g_ple_gate, v_w_ple_gate, v_w_ple_up, v_g_ple_post, v_g_final):
    seq = x.shape[1]
    me = 2 * lax.axis_index("x") + lax.axis_index("y")
    chip = me.astype(jnp.int32).reshape(1)
    core = lax.axis_index("c").astype(jnp.int32).reshape(1)
    place = jnp.concatenate([chip, core])
    xs, ps, ts = x.reshape(seq, D_MODEL), p.reshape(seq, D_PLE), loss_target.reshape(seq, D_MODEL)

    big_names = ["w_in", "w_gu", "w_out", "w_down", "w_pg", "w_pu"]
    big = [w_in[0], w_gate_up[0], w_out[0], w_down[0], w_ple_gate[0], w_ple_up[0]]
    big_m = [m_w_in[0], m_w_gate_up[0], m_w_out[0], m_w_down[0], m_w_ple_gate[0], m_w_ple_up[0]]
    big_v = [v_w_in[0], v_w_gate_up[0], v_w_out[0], v_w_down[0], v_w_ple_gate[0], v_w_ple_up[0]]
    b_in, b_gu, b_out, b_down, b_pg, b_pu = [_cast_into_slot(nm, chip, w, bf16) for nm, w in zip(big_names, big)]
    b_cw = _cast_into_slot("conv_w", chip, _pad_rows(conv_w[0], HALO), f32)
    w_in_g, cw_g = _gather_in([b_in, b_cw])
    conv_w_f = cw_g.transpose(1, 0, 2).reshape(HALO, C_CONV)

    z, (w_out_g, w_down_g) = _mix_in(xs, g_mix, w_in_g, [b_out, b_down])
    w_out_f = w_out_g.reshape(D_MODEL, D_MODEL)
    w_down_f = w_down_g.reshape(D_FF, D_MODEL)
    (x1, mix, u1, pooled), (w_gu_g,) = _conv_pool_out(z, xs, conv_w_f, conv_b, ln_g, ln_b, pool_w[0], pool_scale,
                                                      w_out_f, [b_gu])
    (x2, h2, gu), (w_pg_g, w_pu_g) = _ffn_fwd(x1, g_ffn, w_gu_g, w_down_f, [b_pg, b_pu])
    w_pg_f = w_pg_g.reshape(D_MODEL, D_MODEL)
    dx2, d_w_pg, d_w_pu, small_ple = _ple_loss(x2, ps, ts, g_ple_gate, g_ple_post, g_final.reshape(1, D_MODEL),
                                               w_pg_f, w_pu_g)
    dx1, dgu, small_ffn = _ffn_bwd_dx(dx2, x1, gu, g_ffn, w_gu_g, w_down_f)
    d_w_gu = _ffn_bwd_dw_gu(h2, dgu)
    d_w_down = _ffn_bwd_dw_down(gu, dx2)
    du1, dpo, d_w_out, d_pool_w, small_mix = _mix_bwd_local(dx1, mix, u1, pooled, w_out_f, ln_g, ln_b, pool_w[0],
                                                             pool_scale)
    grad_x, d_w_in, d_conv_w, small_in = _in_bwd(du1, dpo, z, xs, dx1, conv_w_f, g_mix, w_in_g)

    grads = [d_w_in, d_w_gu, d_w_out.reshape(N_CHIPS, -1, D_MODEL), d_w_down.reshape(N_CHIPS, -1, D_MODEL),
             d_w_pg.reshape(N_CHIPS, -1, D_MODEL), d_w_pu]
    from_sibling = _rs_pair("all", grads)
    parts = [_rs_add_pair(nm, core, g, r) for nm, g, r in zip(big_names, grads, from_sibling)]
    landed = _rs_cross("all", parts)
    halves = [_rs_sum_chips(nm, place, a, b) for nm, a, b in zip(big_names, landed, parts)]
    big_g = _grad_pair(halves)
    big_upd = [_adam(nm, w, g, m, v) for nm, w, g, m, v in zip(big_names, big, big_g, big_m, big_v)]

    cw_chunks = d_conv_w.reshape(HALO, N_CHIPS, 128).transpose(1, 0, 2)
    part = jnp.concatenate([_rows128(small_ple), _rows128(small_ffn), _rows128(small_mix), _rows128(d_pool_w),
                            _rows128(cw_chunks), _rows128(small_in)], axis=0)
    sw = _pack_small(me, g_ple_gate, g_ple_post, g_final, g_ffn, ln_g, ln_b, conv_b, pool_scale, pool_w, conv_w[0], g_mix)
    sm = _pack_small(me, m_g_ple_gate, m_g_ple_post, m_g_final, m_g_ffn, m_ln_g, m_ln_b, m_conv_b, m_pool_scale,
                     m_pool_w, m_conv_w[0], m_g_mix)
    sv = _pack_small(me, v_g_ple_gate, v_g_ple_post, v_g_final, v_g_ffn, v_ln_g, v_ln_b, v_conv_b, v_pool_scale,
                     v_pool_w, v_conv_w[0], v_g_mix)
    small = [_unpack_small(me, a) for a in _small_sync_adam(part, sw, sm, sv)]

    names = ["g_mix", "w_in", "conv_w", "conv_b", "ln_g", "ln_b", "pool_w", "pool_scale", "w_out", "g_ffn",
             "w_gate_up", "w_down", "g_ple_gate", "w_ple_gate", "w_ple_up", "g_ple_post", "g_final"]
    big_at = {"w_in": 0, "w_gate_up": 1, "w_out": 2, "w_down": 3, "w_ple_gate": 4, "w_ple_up": 5}
    out = [small[0]["loss"], grad_x.reshape(1, seq, D_MODEL)]
    for kind in range(4):
        for nm in names:
            if nm in big_at:
                k = big_at[nm]
                out.append((big_g[k] if kind == 0 else big_upd[k][kind - 1])[None])
            else:
                out.append(small[kind][nm])
    return tuple(out)
```

```python
import functools

import jax
import jax.numpy as jnp
from jax import lax
from jax.experimental import pallas as pl
from jax.experimental.pallas import tpu as pltpu

f32, bf16 = jnp.float32, jnp.bfloat16

EPS = 1e-6
D_MODEL = 1024
C_CONV = 512
C_POOL = 512
POOL_WINDOWS = (2, 4, 8, 16)
POOL_GROUP = 128
CONV_K = 31
D_FF = 2816
D_PLE = 256
N_CHIPS = 4
N_DEV = 8
W_IN_COLS = 2 * C_CONV + C_POOL
W_IN_CHUNK = W_IN_COLS // N_CHIPS
FF_CHUNK = 2 * D_FF // N_CHIPS
PLE_CHUNK = D_MODEL // N_CHIPS
HALO = 32
ROW_TILE = 512
CONV_ROWS = 32
FF_SUB = (0, 512, 1024, FF_CHUNK)
VMEM_LIMIT = 56 * 1024 * 1024

ADAM_LR = 0.001
ADAM_B1 = 0.9
ADAM_B2 = 0.999
ADAM_EPS = 1e-08
ADAM_WD = 0.01
ADAM_STEP = 10

MESH = pl.DeviceIdType.MESH
ANY = pl.BlockSpec(memory_space=pl.ANY)
VMEM = pl.BlockSpec(memory_space=pltpu.VMEM)


def _cp(*sem):
    return pltpu.CompilerParams(dimension_semantics=sem, vmem_limit_bytes=VMEM_LIMIT)


def _dot(a, b):
    return jnp.dot(a, b, preferred_element_type=f32)


def _dot_nt(a, b):
    return lax.dot_general(a, b, (((1,), (1,)), ((), ())), preferred_element_type=f32)


def _dot_tn(a, b):
    return lax.dot_general(a, b, (((0,), (0,)), ((), ())), preferred_element_type=f32)


def _sigmoid(v):
    return jax.nn.sigmoid(v)


def _rms_fwd(v, g):
    r = lax.rsqrt(jnp.mean(v * v, axis=-1, keepdims=True) + EPS)
    vh = v * r
    return vh * g, vh, r


def _rms_bwd(dy, vh, r, g):
    dvh = dy * g
    dv = r * (dvh - vh * jnp.mean(dvh * vh, axis=-1, keepdims=True))
    return dv, jnp.sum(dy * vh, axis=0, keepdims=True)


def _silu_grad(v, s):
    return s * (1.0 + v * (1.0 - s))


def _row(i, n):
    return pl.BlockSpec((n[0], n[1]), lambda *a: (a[i], 0))


def _full(shape):
    nd = len(shape)
    return pl.BlockSpec(shape, lambda *a: (0,) * nd)


def _place():
    x, y, c = lax.axis_index("x"), lax.axis_index("y"), lax.axis_index("c")
    others = [(1 - x, y), (x, 1 - y), (1 - x, 1 - y)]
    return x, y, c, 2 * x + y, others


def _remote(src, dst, send_sem, recv_sem, dev):
    return pltpu.make_async_remote_copy(src_ref=src, dst_ref=dst, send_sem=send_sem, recv_sem=recv_sem,
                                        device_id=dev, device_id_type=MESH)


def _cast_into_slot(name, me, w, dtype):
    rows, cols = w.shape
    tr = next(rows // d for d in (1, 2, 4, 8) if rows % (16 * d) == 0 and rows // d <= 512)

    def body(me_ref, w_ref, o_ref):
        o_ref[0] = w_ref[...].astype(dtype)

    return pl.pallas_call(
        body, name=f"cast_{name}",
        grid_spec=pltpu.PrefetchScalarGridSpec(
            num_scalar_prefetch=1, grid=(rows // tr,),
            in_specs=[pl.BlockSpec((tr, cols), lambda r, me_ref: (r, 0))],
            out_specs=pl.BlockSpec((1, tr, cols), lambda r, me_ref: (me_ref[0], r, 0))),
        out_shape=jax.ShapeDtypeStruct((N_CHIPS, rows, cols), dtype),
        compiler_params=_cp("parallel"),
    )(me, w)


class _Gather:
    def __init__(self, bufs, send_sems, recv_sems):
        self.bufs, self.send_sems, self.recv_sems = bufs, send_sems, recv_sems
        self.x, self.y, self.c, self.me, self.others = _place()
        self.halves = [b.shape[1] // 2 for b in bufs]

    def _piece(self, k, chip, half):
        return self.bufs[k].at[chip, pl.ds(half * self.halves[k], self.halves[k]), :]

    def _ici(self, k, j, chip):
        ox, oy = self.others[j]
        piece = self._piece(k, chip, self.c)
        return _remote(piece, piece, self.send_sems.at[6 * k + j], self.recv_sems.at[6 * k + j], (ox, oy, self.c))

    def _pair(self, k, j, half):
        ox, oy = self.others[j]
        piece = self._piece(k, 2 * ox + oy, half)
        return _remote(piece, piece, self.send_sems.at[6 * k + 3 + j], self.recv_sems.at[6 * k + 3 + j],
                       (self.x, self.y, 1 - self.c))

    def _each(self):
        return [(k, j) for k in range(len(self.bufs)) for j in range(3)]

    def start(self):
        for k, j in self._each():
            self._ici(k, j, self.me).start()

    def forward(self):
        for k, j in self._each():
            ox, oy = self.others[j]
            self._ici(k, j, 2 * ox + oy).wait_recv()
            self._pair(k, j, self.c).start()

    def finish(self):
        for k, j in self._each():
            self._pair(k, j, 1 - self.c).wait_recv()
        for k, j in self._each():
            self._ici(k, j, self.me).wait_send()
            self._pair(k, j, self.c).wait_send()

    @staticmethod
    def scratch(n):
        return [pltpu.SemaphoreType.DMA((6 * n,)), pltpu.SemaphoreType.DMA((6 * n,))]


def _gather_in(bufs):
    n = len(bufs)

    def body(*refs):
        g = _Gather(refs[n:2 * n], *refs[2 * n:])
        g.start()
        g.forward()
        g.finish()

    return pl.pallas_call(
        body, name="gather_in", in_specs=[ANY] * n, out_specs=[ANY] * n,
        out_shape=[jax.ShapeDtypeStruct(b.shape, b.dtype) for b in bufs],
        input_output_aliases={k: k for k in range(n)}, scratch_shapes=_Gather.scratch(n),
    )(*bufs)


def _carried(bufs):
    n = len(bufs)
    return dict(in_specs=[ANY] * n, out_specs=[ANY] * n,
                out_shape=[jax.ShapeDtypeStruct(b.shape, b.dtype) for b in bufs], scratch=_Gather.scratch(n))


def _mix_in(x, g_mix, w_in_g, carry):
    s = x.shape[0]
    tm = min(ROW_TILE, s)
    n = s // tm
    nc = len(carry)
    cs = _carried(carry)

    def body(x_ref, g_ref, w_ref, *refs):
        z_ref = refs[nc]
        gather = _Gather(refs[nc + 1:2 * nc + 1], *refs[2 * nc + 1:])
        i = pl.program_id(0)
        pl.when(i == 0)(gather.start)
        pl.when(i == max(n - 2, 0))(gather.forward)
        h, _, _ = _rms_fwd(x_ref[...], g_ref[...])
        hb = h.astype(bf16)
        for j in range(N_CHIPS):
            z_ref[:, j * W_IN_CHUNK:(j + 1) * W_IN_CHUNK] = _dot(hb, w_ref[j])
        pl.when(i == n - 1)(gather.finish)

    res = pl.pallas_call(
        body, name="mix_in", grid=(n,),
        in_specs=[_row(0, (tm, D_MODEL)), _full((1, D_MODEL)), _full((N_CHIPS, D_MODEL, W_IN_CHUNK))] + cs["in_specs"],
        out_specs=[_row(0, (tm, W_IN_COLS))] + cs["out_specs"],
        out_shape=[jax.ShapeDtypeStruct((s, W_IN_COLS), f32)] + cs["out_shape"],
        input_output_aliases={3 + k: 1 + k for k in range(nc)}, scratch_shapes=cs["scratch"],
        compiler_params=_cp("arbitrary"),
    )(x, g_mix, w_in_g, *carry)
    return res[0], res[1:]


def _pool_counts(tm, w, first_row):
    t1 = (lax.broadcasted_iota(jnp.int32, (tm, 1), 0) + first_row + 1).astype(f32)
    return jnp.minimum(t1, float(w))


def _conv_pool_out(z, x, conv_w, conv_b, ln_g, ln_b, pool_w, pool_scale, w_out, carry):
    s = x.shape[0]
    tm = min(ROW_TILE, s)
    n = s // tm
    hb = tm // HALO
    nc = len(carry)
    cs = _carried(carry)

    def body(z_ref, zp_ref, x_ref, cw_ref, cb_ref, lg_ref, lb_ref, pw_ref, ps_ref, wo_ref, *refs):
        x1_ref, mix_ref, u1_ref, pooled_ref = refs[nc:nc + 4]
        ubuf, vbuf = refs[2 * nc + 4:2 * nc + 6]
        gather = _Gather(refs[nc + 4:2 * nc + 4], *refs[2 * nc + 6:])
        i = pl.program_id(0)
        pl.when(i == 0)(gather.start)
        pl.when(i == max(n - 2, 0))(gather.forward)
        keep = (i > 0).astype(f32)
        zp = zp_ref[...] * keep
        ubuf[0:HALO, :] = zp[:, :C_CONV] * _sigmoid(zp[:, C_CONV:2 * C_CONV])
        vbuf[0:HALO, :] = zp[:, 2 * C_CONV:]
        ubuf[HALO:, :] = z_ref[:, :C_CONV] * _sigmoid(z_ref[:, C_CONV:2 * C_CONV])
        vbuf[HALO:, :] = z_ref[:, 2 * C_CONV:]
        off = HALO - (CONV_K - 1)
        for r0 in range(0, tm, CONV_ROWS):
            acc = jnp.zeros((CONV_ROWS, C_CONV), f32) + cb_ref[...]
            for k in range(CONV_K):
                acc = acc + cw_ref[pl.ds(k, 1), :] * ubuf[pl.ds(r0 + off + k, CONV_ROWS), :]
            u1_ref[r0:r0 + CONV_ROWS, :] = acc
        u1 = u1_ref[...]
        mu = jnp.mean(u1, axis=-1, keepdims=True)
        uc = u1 - mu
        rstd = lax.rsqrt(jnp.mean(uc * uc, axis=-1, keepdims=True) + EPS)
        u2 = uc * rstd * lg_ref[...] + lb_ref[...]
        mix_ref[:, :C_CONV] = (u2 * _sigmoid(u2)).astype(bf16)
        for g, w in enumerate(POOL_WINDOWS):
            cols = slice(g * POOL_GROUP, (g + 1) * POOL_GROUP)
            acc = vbuf[pl.ds(HALO, tm), cols]
            vg = acc
            for d in range(1, w):
                acc = acc + vbuf[pl.ds(HALO - d, tm), cols]
            pooled = (acc / _pool_counts(tm, w, i * tm) - vg).astype(bf16)
            pooled_ref[:, cols] = pooled
            mixed = _dot(pooled, pw_ref[g].astype(bf16))
            mix_ref[:, C_CONV + g * POOL_GROUP:C_CONV + (g + 1) * POOL_GROUP] = (mixed * ps_ref[:, cols]).astype(bf16)
        x1_ref[...] = x_ref[...] + _dot(mix_ref[...], wo_ref[...])
        pl.when(i == n - 1)(gather.finish)

    res = pl.pallas_call(
        body, name="conv_pool_out", grid=(n,),
        in_specs=[_row(0, (tm, W_IN_COLS)),
                  pl.BlockSpec((HALO, W_IN_COLS), lambda i: (jnp.maximum(i * hb - 1, 0), 0)),
                  _row(0, (tm, D_MODEL)), _full((HALO, C_CONV)), _full((1, C_CONV)), _full((1, C_CONV)),
                  _full((1, C_CONV)), _full((4, POOL_GROUP, POOL_GROUP)), _full((1, C_POOL)),
                  _full((D_MODEL, D_MODEL))] + cs["in_specs"],
        out_specs=[_row(0, (tm, D_MODEL)), _row(0, (tm, D_MODEL)), _row(0, (tm, C_CONV)), _row(0, (tm, C_POOL))]
        + cs["out_specs"],
        out_shape=[jax.ShapeDtypeStruct((s, D_MODEL), f32), jax.ShapeDtypeStruct((s, D_MODEL), bf16),
                   jax.ShapeDtypeStruct((s, C_CONV), f32), jax.ShapeDtypeStruct((s, C_POOL), bf16)] + cs["out_shape"],
        input_output_aliases={10 + k: 4 + k for k in range(nc)},
        scratch_shapes=[pltpu.VMEM((HALO + tm, C_CONV), f32), pltpu.VMEM((HALO + tm, C_POOL), f32)] + cs["scratch"],
        compiler_params=_cp("arbitrary"),
    )(z, z, x, conv_w, conv_b, ln_g, ln_b, pool_w, pool_scale, w_out, *carry)
    return res[:4], res[4:]


def _ffn_fwd(x1, g_ffn, w_gu_g, w_down, carry):
    s = x1.shape[0]
    tm = min(ROW_TILE, s)
    n = s // tm
    nc = len(carry)
    cs = _carried(carry)

    def body(x1_ref, g_ref, wg_ref, wu_ref, wd_ref, *refs):
        x2_ref, h2_ref, gu_ref = refs[nc:nc + 3]
        acc_ref = refs[2 * nc + 3]
        gather = _Gather(refs[nc + 3:2 * nc + 3], *refs[2 * nc + 4:])
        i, c = pl.program_id(0), pl.program_id(1)
        pl.when(jnp.logical_and(i == 0, c == 0))(gather.start)
        pl.when(jnp.logical_and(i == n - 1, c == 0))(gather.forward)

        @pl.when(c == 0)
        def _():
            h, _, _ = _rms_fwd(x1_ref[...], g_ref[...])
            h2_ref[...] = h.astype(bf16)
            acc_ref[...] = jnp.zeros_like(acc_ref)

        h = h2_ref[...]
        for lo, hi in zip(FF_SUB[:-1], FF_SUB[1:]):
            gate = _dot(h, wg_ref[0, :, lo:hi])
            up = _dot(h, wu_ref[0, :, lo:hi])
            gu_ref[0, :, lo:hi] = gate.astype(bf16)
            gu_ref[1, :, lo:hi] = up.astype(bf16)
            f = (gate * _sigmoid(gate) * up).astype(bf16)
            acc_ref[...] += _dot(f, wd_ref[lo:hi, :])

        @pl.when(c == 1)
        def _():
            x2_ref[...] = x1_ref[...] + acc_ref[...]

        pl.when(jnp.logical_and(i == n - 1, c == 1))(gather.finish)

    res = pl.pallas_call(
        body, name="ffn_fwd", grid=(n, 2),
        in_specs=[_row(0, (tm, D_MODEL)), _full((1, D_MODEL)),
                  pl.BlockSpec((1, D_MODEL, FF_CHUNK), lambda i, c: (c, 0, 0)),
                  pl.BlockSpec((1, D_MODEL, FF_CHUNK), lambda i, c: (2 + c, 0, 0)),
                  pl.BlockSpec((FF_CHUNK, D_MODEL), lambda i, c: (c, 0))] + cs["in_specs"],
        out_specs=[_row(0, (tm, D_MODEL)), _row(0, (tm, D_MODEL)),
                   pl.BlockSpec((2, tm, FF_CHUNK), lambda i, c: (0, i, c))] + cs["out_specs"],
        out_shape=[jax.ShapeDtypeStruct((s, D_MODEL), f32), jax.ShapeDtypeStruct((s, D_MODEL), bf16),
                   jax.ShapeDtypeStruct((2, s, D_FF), bf16)] + cs["out_shape"],
        input_output_aliases={5 + k: 3 + k for k in range(nc)},
        scratch_shapes=[pltpu.VMEM((tm, D_MODEL), f32)] + cs["scratch"],
        compiler_params=_cp("arbitrary", "arbitrary"),
    )(x1, g_ffn, w_gu_g, w_gu_g, w_down, *carry)
    return res[:3], res[3:]


def _ple_loss(x2, p, target, g_pg, g_post, g_final, w_pg, w_pu_g):
    s = x2.shape[0]
    tm = min(ROW_TILE, s)
    n = s // tm

    def body(x2_ref, p_ref, t_ref, gpg_ref, gpo_ref, gf_ref, wpg_ref, wpu_ref,
             dx2_ref, dwpg_ref, dwpu_ref, small_ref, apg_ref, apu_ref):
        i = pl.program_id(0)

        @pl.when(i == 0)
        def _():
            apg_ref[...] = jnp.zeros_like(apg_ref)
            apu_ref[...] = jnp.zeros_like(apu_ref)
            small_ref[...] = jnp.zeros_like(small_ref)

        x2 = x2_ref[...]
        h3, x2h, r2 = _rms_fwd(x2, gpg_ref[...])
        h3b = h3.astype(bf16)
        gate = _sigmoid(_dot(h3b, wpg_ref[...]))
        pb = p_ref[...].astype(bf16)
        pe = jnp.concatenate([_dot(pb, wpu_ref[j]) for j in range(N_CHIPS)], axis=-1)
        e, peh, rp = _rms_fwd(pe, gpo_ref[...])
        x3 = x2 + gate * e
        y, x3h, r3 = _rms_fwd(x3, gf_ref[...])
        d = y - t_ref[...]
        loss = 0.5 * jnp.sum(jnp.sum(d * d, axis=-1, keepdims=True) * (1.0 / D_MODEL), axis=0, keepdims=True)
        dx3, dgf = _rms_bwd(d * (1.0 / D_MODEL), x3h, r3, gf_ref[...])
        dpe, dgpo = _rms_bwd(dx3 * gate, peh, rp, gpo_ref[...])
        dgl = (dx3 * e * gate * (1.0 - gate)).astype(bf16)
        apg_ref[...] += _dot_tn(h3b, dgl)
        apu_ref[...] += _dot_tn(pb, dpe.astype(bf16))
        dh3 = _dot_nt(dgl, wpg_ref[...])
        dx2b, dgpg = _rms_bwd(dh3, x2h, r2, gpg_ref[...])
        dx2_ref[...] = dx3 + dx2b
        small_ref[0:1, :] += dgpg
        small_ref[1:2, :] += dgpo
        small_ref[2:3, :] += dgf
        small_ref[3:4, :] += jnp.broadcast_to(loss, (1, D_MODEL))

        @pl.when(i == n - 1)
        def _():
            dwpg_ref[...] = apg_ref[...].astype(bf16)
            for j in range(N_CHIPS):
                dwpu_ref[j] = apu_ref[:, j * PLE_CHUNK:(j + 1) * PLE_CHUNK].astype(bf16)

    return pl.pallas_call(
        body, name="ple_loss", grid=(n,),
        in_specs=[_row(0, (tm, D_MODEL)), _row(0, (tm, D_PLE)), _row(0, (tm, D_MODEL)),
                  _full((1, D_MODEL)), _full((1, D_MODEL)), _full((1, D_MODEL)),
                  _full((D_MODEL, D_MODEL)), _full((N_CHIPS, D_PLE, PLE_CHUNK))],
        out_specs=[_row(0, (tm, D_MODEL)), _full((D_MODEL, D_MODEL)), _full((N_CHIPS, D_PLE, PLE_CHUNK)),
                   _full((8, D_MODEL))],
        out_shape=[jax.ShapeDtypeStruct((s, D_MODEL), f32), jax.ShapeDtypeStruct((D_MODEL, D_MODEL), bf16),
                   jax.ShapeDtypeStruct((N_CHIPS, D_PLE, PLE_CHUNK), bf16), jax.ShapeDtypeStruct((8, D_MODEL), f32)],
        scratch_shapes=[pltpu.VMEM((D_MODEL, D_MODEL), f32), pltpu.VMEM((D_PLE, D_MODEL), f32)],
        compiler_params=_cp("arbitrary"),
    )(x2, p, target, g_pg, g_post, g_final, w_pg, w_pu_g)


def _crossed(parts):
    n = len(parts)
    return dict(in_specs=[ANY] * n, out_specs=[ANY] * n,
                out_shape=[jax.ShapeDtypeStruct(a.shape, a.dtype) for a in parts], scratch=_Cross.scratch(n))


def _ffn_bwd_dx(dx2, x1, gu, g_ffn, w_gu_g, w_down, parts):
    s = x1.shape[0]
    tm = min(ROW_TILE, s)
    n = s // tm
    nc = len(parts)
    cs = _crossed(parts)

    def body(dx2_ref, x1_ref, gu_ref, g_ref, wg_ref, wu_ref, wd_ref, *refs):
        dx1_ref, dgu_ref, small_ref = refs[nc:nc + 3]
        acc_ref = refs[2 * nc + 3]
        cross = _Cross(refs[:nc], refs[nc + 3:2 * nc + 3], *refs[2 * nc + 4:])
        i, c = pl.program_id(0), pl.program_id(1)
        pl.when(jnp.logical_and(i == 0, c == 0))(cross.start)

        @pl.when(jnp.logical_and(i == 0, c == 0))
        def _():
            small_ref[...] = jnp.zeros_like(small_ref)

        @pl.when(c == 0)
        def _():
            acc_ref[...] = jnp.zeros_like(acc_ref)

        dyb = dx2_ref[...].astype(bf16)
        for lo, hi in zip(FF_SUB[:-1], FF_SUB[1:]):
            df = _dot_nt(dyb, wd_ref[lo:hi, :])
            gate = gu_ref[0, :, lo:hi].astype(f32)
            up = gu_ref[1, :, lo:hi].astype(f32)
            sg = _sigmoid(gate)
            dgate = (df * up * _silu_grad(gate, sg)).astype(bf16)
            dup = (df * gate * sg).astype(bf16)
            dgu_ref[0, :, lo:hi] = dgate
            dgu_ref[1, :, lo:hi] = dup
            acc_ref[...] += _dot_nt(dgate, wg_ref[0, :, lo:hi]) + _dot_nt(dup, wu_ref[0, :, lo:hi])

        @pl.when(c == 1)
        def _():
            _, x1h, r1 = _rms_fwd(x1_ref[...], g_ref[...])
            dx1b, dg = _rms_bwd(acc_ref[...], x1h, r1, g_ref[...])
            dx1_ref[...] = dx2_ref[...] + dx1b
            small_ref[0:1, :] += dg

        pl.when(jnp.logical_and(i == n - 1, c == 1))(cross.finish)

    res = pl.pallas_call(
        body, name="ffn_bwd_dx", grid=(n, 2),
        in_specs=[_row(0, (tm, D_MODEL)), _row(0, (tm, D_MODEL)),
                  pl.BlockSpec((2, tm, FF_CHUNK), lambda i, c: (0, i, c)), _full((1, D_MODEL)),
                  pl.BlockSpec((1, D_MODEL, FF_CHUNK), lambda i, c: (c, 0, 0)),
                  pl.BlockSpec((1, D_MODEL, FF_CHUNK), lambda i, c: (2 + c, 0, 0)),
                  pl.BlockSpec((FF_CHUNK, D_MODEL), lambda i, c: (c, 0))] + cs["in_specs"],
        out_specs=[_row(0, (tm, D_MODEL)), pl.BlockSpec((2, tm, FF_CHUNK), lambda i, c: (0, i, c)),
                   _full((8, D_MODEL))] + cs["out_specs"],
        out_shape=[jax.ShapeDtypeStruct((s, D_MODEL), f32), jax.ShapeDtypeStruct((2, s, D_FF), bf16),
                   jax.ShapeDtypeStruct((8, D_MODEL), f32)] + cs["out_shape"],
        scratch_shapes=[pltpu.VMEM((tm, D_MODEL), f32)] + cs["scratch"],
        compiler_params=_cp("arbitrary", "arbitrary"),
    )(dx2, x1, gu, g_ffn, w_gu_g, w_gu_g, w_down, *parts)
    return res[:3], res[3:]


def _ffn_bwd_dw_gu(h2, dgu):
    s = h2.shape[0]
    ts = min(ROW_TILE, s)
    n = s // ts

    def body(h_ref, d_ref, o_ref, acc_ref):
        t = pl.program_id(1)

        @pl.when(t == 0)
        def _():
            acc_ref[...] = jnp.zeros_like(acc_ref)

        acc_ref[...] += _dot_tn(h_ref[...], d_ref[0])

        @pl.when(t == n - 1)
        def _():
            o_ref[0] = acc_ref[...].astype(bf16)

    return pl.pallas_call(
        body, name="ffn_bwd_dw_gu", grid=(N_CHIPS, n),
        in_specs=[pl.BlockSpec((ts, D_MODEL), lambda j, t: (t, 0)),
                  pl.BlockSpec((1, ts, FF_CHUNK), lambda j, t: (j // 2, t, j % 2))],
        out_specs=pl.BlockSpec((1, D_MODEL, FF_CHUNK), lambda j, t: (j, 0, 0)),
        out_shape=jax.ShapeDtypeStruct((N_CHIPS, D_MODEL, FF_CHUNK), bf16),
        scratch_shapes=[pltpu.VMEM((D_MODEL, FF_CHUNK), f32)],
        compiler_params=_cp("parallel", "arbitrary"),
    )(h2, dgu)


def _ffn_bwd_dw_down(gu, dx2):
    s = dx2.shape[0]
    ts = min(ROW_TILE, s)
    n = s // ts

    def body(gu_ref, d_ref, o_ref, acc_ref):
        t = pl.program_id(1)

        @pl.when(t == 0)
        def _():
            acc_ref[...] = jnp.zeros_like(acc_ref)

        gate = gu_ref[0].astype(f32)
        f = (gate * _sigmoid(gate) * gu_ref[1].astype(f32)).astype(bf16)
        acc_ref[...] += _dot_tn(f, d_ref[...].astype(bf16))

        @pl.when(t == n - 1)
        def _():
            o_ref[...] = acc_ref[...].astype(bf16)

    return pl.pallas_call(
        body, name="ffn_bwd_dw_down", grid=(2, n),
        in_specs=[pl.BlockSpec((2, ts, FF_CHUNK), lambda c, t: (0, t, c)),
                  pl.BlockSpec((ts, D_MODEL), lambda c, t: (t, 0))],
        out_specs=pl.BlockSpec((FF_CHUNK, D_MODEL), lambda c, t: (c, 0)),
        out_shape=jax.ShapeDtypeStruct((D_FF, D_MODEL), bf16),
        scratch_shapes=[pltpu.VMEM((FF_CHUNK, D_MODEL), f32)],
        compiler_params=_cp("parallel", "arbitrary"),
    )(gu, dx2)


def _mix_bwd_local(dx1, mix, u1, pooled, w_out, ln_g, ln_b, pool_w, pool_scale):
    s = dx1.shape[0]
    tm = min(ROW_TILE, s)
    n = s // tm

    def body(dx1_ref, mix_ref, u1_ref, po_ref, wo_ref, lg_ref, lb_ref, pw_ref, ps_ref,
             du1_ref, dpo_ref, dwo_ref, dpw_ref, small_ref, awo_ref):
        i = pl.program_id(0)

        @pl.when(i == 0)
        def _():
            awo_ref[...] = jnp.zeros_like(awo_ref)
            dpw_ref[...] = jnp.zeros_like(dpw_ref)
            small_ref[...] = jnp.zeros_like(small_ref)

        dyb = dx1_ref[...].astype(bf16)
        dmix = _dot_nt(dyb, wo_ref[...])
        awo_ref[...] += _dot_tn(mix_ref[...], dyb)
        u1 = u1_ref[...]
        mu = jnp.mean(u1, axis=-1, keepdims=True)
        uc = u1 - mu
        rstd = lax.rsqrt(jnp.mean(uc * uc, axis=-1, keepdims=True) + EPS)
        uh = uc * rstd
        u2 = uh * lg_ref[...] + lb_ref[...]
        du2 = dmix[:, :C_CONV] * _silu_grad(u2, _sigmoid(u2))
        duh = du2 * lg_ref[...]
        du1 = rstd * (duh - jnp.mean(duh, axis=-1, keepdims=True) - uh * jnp.mean(duh * uh, axis=-1, keepdims=True))
        du1_ref[...] = du1
        small_ref[0:1, :] += jnp.sum(du2 * uh, axis=0, keepdims=True)
        small_ref[1:2, :] += jnp.sum(du2, axis=0, keepdims=True)
        small_ref[2:3, :] += jnp.sum(du1, axis=0, keepdims=True)
        for g in range(len(POOL_WINDOWS)):
            cols = slice(g * POOL_GROUP, (g + 1) * POOL_GROUP)
            dq = dmix[:, C_CONV + g * POOL_GROUP:C_CONV + (g + 1) * POOL_GROUP]
            pwb = pw_ref[g].astype(bf16)
            pg = po_ref[:, cols]
            mixed = _dot(pg, pwb)
            small_ref[3:4, cols] += jnp.sum(dq * mixed, axis=0, keepdims=True)
            dmixed = (dq * ps_ref[:, cols]).astype(bf16)
            dpw_ref[g] += _dot_tn(pg, dmixed)
            dpo_ref[:, cols] = _dot_nt(dmixed, pwb)

        @pl.when(i == n - 1)
        def _():
            dwo_ref[...] = awo_ref[...].astype(bf16)

    return pl.pallas_call(
        body, name="mix_bwd_local", grid=(n,),
        in_specs=[_row(0, (tm, D_MODEL)), _row(0, (tm, D_MODEL)), _row(0, (tm, C_CONV)), _row(0, (tm, C_POOL)),
                  _full((D_MODEL, D_MODEL)), _full((1, C_CONV)), _full((1, C_CONV)),
                  _full((4, POOL_GROUP, POOL_GROUP)), _full((1, C_POOL))],
        out_specs=[_row(0, (tm, C_CONV)), _row(0, (tm, C_POOL)), _full((D_MODEL, D_MODEL)),
                   _full((4, POOL_GROUP, POOL_GROUP)), _full((8, C_CONV))],
        out_shape=[jax.ShapeDtypeStruct((s, C_CONV), f32), jax.ShapeDtypeStruct((s, C_POOL), f32),
                   jax.ShapeDtypeStruct((D_MODEL, D_MODEL), bf16),
                   jax.ShapeDtypeStruct((4, POOL_GROUP, POOL_GROUP), f32), jax.ShapeDtypeStruct((8, C_CONV), f32)],
        scratch_shapes=[pltpu.VMEM((D_MODEL, D_MODEL), f32)],
        compiler_params=_cp("arbitrary"),
    )(dx1, mix, u1, pooled, w_out, ln_g, ln_b, pool_w, pool_scale)


def _in_bwd(du1, dpo, z, x, dx1, conv_w, g_mix, w_in_g, parts):
    s = x.shape[0]
    tm = min(ROW_TILE, s)
    n = s // tm
    hb = tm // HALO
    last = s // HALO - 1
    nc = len(parts)
    cs = _crossed(parts)

    def body(du_ref, dun_ref, dp_ref, dpn_ref, z_ref, zp_ref, x_ref, dx1_ref, cw_ref, g_ref, w_ref, *refs):
        gx_ref, dw_ref, dcw_ref, small_ref = refs[nc:nc + 4]
        dbuf, pbuf, ubuf, dz_ref, acc_ref = refs[2 * nc + 4:2 * nc + 9]
        cross = _Cross(refs[:nc], refs[nc + 4:2 * nc + 4], *refs[2 * nc + 9:])
        i = pl.program_id(0)
        pl.when(i == 0)(cross.start)

        @pl.when(i == 0)
        def _():
            acc_ref[...] = jnp.zeros_like(acc_ref)
            dcw_ref[...] = jnp.zeros_like(dcw_ref)
            small_ref[...] = jnp.zeros_like(small_ref)

        keep_prev = (i > 0).astype(f32)
        keep_next = (i < n - 1).astype(f32)
        zp = zp_ref[...] * keep_prev
        ubuf[0:HALO, :] = zp[:, :C_CONV] * _sigmoid(zp[:, C_CONV:2 * C_CONV])
        a = z_ref[:, :C_CONV]
        sb = _sigmoid(z_ref[:, C_CONV:2 * C_CONV])
        ubuf[HALO:, :] = a * sb
        dbuf[0:tm, :] = du_ref[...]
        dbuf[tm:, :] = dun_ref[...] * keep_next
        for g, w in enumerate(POOL_WINDOWS):
            cols = slice(g * POOL_GROUP, (g + 1) * POOL_GROUP)
            pbuf[0:tm, cols] = dp_ref[:, cols] / _pool_counts(tm, w, i * tm)
            pbuf[tm:, cols] = dpn_ref[:, cols] * keep_next / _pool_counts(HALO, w, (i + 1) * tm)
        off = HALO - (CONV_K - 1)
        for r0 in range(0, tm, CONV_ROWS):
            acc = jnp.zeros((CONV_ROWS, C_CONV), f32)
            for k in range(CONV_K):
                acc = acc + cw_ref[pl.ds(k, 1), :] * dbuf[pl.ds(r0 + CONV_K - 1 - k, CONV_ROWS), :]
            rows = pl.ds(r0, CONV_ROWS)
            dz_ref[rows, 0:C_CONV] = (acc * sb[r0:r0 + CONV_ROWS]).astype(bf16)
            dz_ref[rows, C_CONV:2 * C_CONV] = (
                acc * a[r0:r0 + CONV_ROWS] * sb[r0:r0 + CONV_ROWS] * (1.0 - sb[r0:r0 + CONV_ROWS])).astype(bf16)
        for k in range(CONV_K):
            part = jnp.zeros((8, C_CONV), f32)
            for r0 in range(0, tm, 8):
                part = part + dbuf[pl.ds(r0, 8), :] * ubuf[pl.ds(r0 + off + k, 8), :]
            dcw_ref[pl.ds(k, 1), :] += jnp.sum(part, axis=0, keepdims=True)
        for g, w in enumerate(POOL_WINDOWS):
            cols = slice(g * POOL_GROUP, (g + 1) * POOL_GROUP)
            acc = pbuf[pl.ds(0, tm), cols]
            for d in range(1, w):
                acc = acc + pbuf[pl.ds(d, tm), cols]
            dz_ref[:, 2 * C_CONV + g * POOL_GROUP:2 * C_CONV + (g + 1) * POOL_GROUP] = (
                acc - dp_ref[:, cols]).astype(bf16)
        h, xh, r = _rms_fwd(x_ref[...], g_ref[...])
        dz = dz_ref[...]
        acc_ref[...] += _dot_tn(h.astype(bf16), dz)
        dh = _dot_nt(dz[:, 0:W_IN_CHUNK], w_ref[0])
        for j in range(1, N_CHIPS):
            dh = dh + _dot_nt(dz[:, j * W_IN_CHUNK:(j + 1) * W_IN_CHUNK], w_ref[j])
        dxb, dg = _rms_bwd(dh, xh, r, g_ref[...])
        gx_ref[...] = dx1_ref[...] + dxb
        small_ref[0:1, :] += dg

        @pl.when(i == n - 1)
        def _():
            for j in range(N_CHIPS):
                dw_ref[j] = acc_ref[:, j * W_IN_CHUNK:(j + 1) * W_IN_CHUNK].astype(bf16)

        pl.when(i == n - 1)(cross.finish)

    nxt = lambda i: (jnp.minimum((i + 1) * hb, last), 0)
    res = pl.pallas_call(
        body, name="in_bwd", grid=(n,),
        in_specs=[_row(0, (tm, C_CONV)), pl.BlockSpec((HALO, C_CONV), nxt),
                  _row(0, (tm, C_POOL)), pl.BlockSpec((HALO, C_POOL), nxt),
                  _row(0, (tm, W_IN_COLS)),
                  pl.BlockSpec((HALO, W_IN_COLS), lambda i: (jnp.maximum(i * hb - 1, 0), 0)),
                  _row(0, (tm, D_MODEL)), _row(0, (tm, D_MODEL)), _full((HALO, C_CONV)), _full((1, D_MODEL)),
                  _full((N_CHIPS, D_MODEL, W_IN_CHUNK))] + cs["in_specs"],
        out_specs=[_row(0, (tm, D_MODEL)), _full((N_CHIPS, D_MODEL, W_IN_CHUNK)), _full((HALO, C_CONV)),
                   _full((8, D_MODEL))] + cs["out_specs"],
        out_shape=[jax.ShapeDtypeStruct((s, D_MODEL), f32), jax.ShapeDtypeStruct((N_CHIPS, D_MODEL, W_IN_CHUNK), bf16),
                   jax.ShapeDtypeStruct((HALO, C_CONV), f32), jax.ShapeDtypeStruct((8, D_MODEL), f32)] + cs["out_shape"],
        scratch_shapes=[pltpu.VMEM((tm + HALO, C_CONV), f32), pltpu.VMEM((tm + HALO, C_POOL), f32),
                        pltpu.VMEM((HALO + tm, C_CONV), f32), pltpu.VMEM((tm, W_IN_COLS), bf16),
                        pltpu.VMEM((D_MODEL, W_IN_COLS), f32)] + cs["scratch"],
        compiler_params=_cp("arbitrary"),
    )(du1, du1, dpo, dpo, z, z, x, dx1, conv_w, g_mix, w_in_g, *parts)
    return res[:4], res[4:]


def _rs_pair(name, grads):
    nk = len(grads)
    halves = [g.shape[1] // 2 for g in grads]

    def body(*refs):
        ins, outs = refs[:nk], refs[nk:2 * nk]
        send_sems, recv_sems = refs[2 * nk:]
        x, y, c, _, _ = _place()
        cps = [_remote(ins[k].at[:, pl.ds((1 - c) * halves[k], halves[k]), :], outs[k],
                       send_sems.at[k], recv_sems.at[k], (x, y, 1 - c)) for k in range(nk)]
        for cp in cps:
            cp.start()
        for cp in cps:
            cp.wait()

    return pl.pallas_call(
        body, name=f"rs_pair_{name}", in_specs=[ANY] * nk, out_specs=[ANY] * nk,
        out_shape=[jax.ShapeDtypeStruct((N_CHIPS, h, g.shape[2]), bf16) for g, h in zip(grads, halves)],
        scratch_shapes=[pltpu.SemaphoreType.DMA((nk,)), pltpu.SemaphoreType.DMA((nk,))],
    )(*grads)


def _rs_add_pair(name, core, grad, recv):
    _, half, cols = recv.shape

    def body(core_ref, g_ref, r_ref, o_ref):
        o_ref[...] = (g_ref[...].astype(f32) + r_ref[...].astype(f32)).astype(bf16)

    return pl.pallas_call(
        body, name=f"rs_add_pair_{name}",
        grid_spec=pltpu.PrefetchScalarGridSpec(
            num_scalar_prefetch=1, grid=(N_CHIPS,),
            in_specs=[pl.BlockSpec((1, half, cols), lambda j, core_ref: (j, core_ref[0], 0)),
                      pl.BlockSpec((1, half, cols), lambda j, core_ref: (j, 0, 0))],
            out_specs=pl.BlockSpec((1, half, cols), lambda j, core_ref: (j, 0, 0))),
        out_shape=jax.ShapeDtypeStruct(recv.shape, bf16),
        compiler_params=_cp("parallel"),
    )(core, grad, recv)


class _Cross:
    def __init__(self, parts, landed, send_sems, recv_sems):
        self.parts, self.landed, self.send_sems, self.recv_sems = parts, landed, send_sems, recv_sems
        _, _, self.c, self.me, self.others = _place()

    def _copy(self, k, j, src_chunk, dst_slot):
        ox, oy = self.others[j]
        return _remote(self.parts[k].at[src_chunk], self.landed[k].at[dst_slot], self.send_sems.at[3 * k + j],
                       self.recv_sems.at[3 * k + j], (ox, oy, self.c))

    def _each(self):
        return [(k, j, 2 * self.others[j][0] + self.others[j][1]) for k in range(len(self.parts)) for j in range(3)]

    def start(self):
        for k, j, chip in self._each():
            self._copy(k, j, chip, self.me).start()

    def finish(self):
        for k, j, chip in self._each():
            self._copy(k, j, chip, chip).wait_recv()
        for k, j, chip in self._each():
            self._copy(k, j, chip, self.me).wait_send()

    @staticmethod
    def scratch(n):
        return [pltpu.SemaphoreType.DMA((3 * n,)), pltpu.SemaphoreType.DMA((3 * n,))]


def _rs_cross(name, parts):
    nk = len(parts)

    def body(*refs):
        cross = _Cross(refs[:nk], refs[nk:2 * nk], *refs[2 * nk:])
        cross.start()
        cross.finish()

    return pl.pallas_call(
        body, name=f"rs_cross_{name}", in_specs=[ANY] * nk, out_specs=[ANY] * nk,
        out_shape=[jax.ShapeDtypeStruct(a.shape, bf16) for a in parts], scratch_shapes=_Cross.scratch(nk),
    )(*parts)


def _rs_sum_chips(name, place, landed, part):
    _, half, cols = landed.shape

    def body(place_ref, l_ref, p_ref, o_ref):
        me = place_ref[0]
        own = p_ref[0].astype(f32)
        acc = jnp.where(me == 0, own, l_ref[0].astype(f32))
        for j in range(1, N_CHIPS):
            acc = acc + jnp.where(me == j, own, l_ref[j].astype(f32))
        o_ref[...] = acc

    return pl.pallas_call(
        body, name=f"rs_sum_chips_{name}",
        grid_spec=pltpu.PrefetchScalarGridSpec(
            num_scalar_prefetch=1, grid=(1,),
            in_specs=[pl.BlockSpec((N_CHIPS, half, cols), lambda t, place_ref: (0, 0, 0)),
                      pl.BlockSpec((1, half, cols), lambda t, place_ref: (place_ref[0], 0, 0))],
            out_specs=pl.BlockSpec((half, cols), lambda t, place_ref: (place_ref[1], 0))),
        out_shape=jax.ShapeDtypeStruct((2 * half, cols), f32),
        compiler_params=_cp("arbitrary"),
    )(place, landed, part)


def _grad_pair(shards):
    nk = len(shards)

    def body(*refs):
        outs = refs[nk:2 * nk]
        send_sems, recv_sems = refs[2 * nk:]
        x, y, c, _, _ = _place()

        def half(k, core):
            h = outs[k].shape[0] // 2
            return outs[k].at[pl.ds(core * h, h), :]

        cps = [_remote(half(k, c), half(k, c), send_sems.at[k], recv_sems.at[k], (x, y, 1 - c)) for k in range(nk)]
        for cp in cps:
            cp.start()
        for k in range(nk):
            _remote(half(k, 1 - c), half(k, 1 - c), send_sems.at[k], recv_sems.at[k], (x, y, 1 - c)).wait_recv()
        for cp in cps:
            cp.wait_send()

    return pl.pallas_call(
        body, name="grad_pair", in_specs=[ANY] * nk, out_specs=[ANY] * nk,
        out_shape=[jax.ShapeDtypeStruct(a.shape, f32) for a in shards],
        input_output_aliases={k: k for k in range(nk)},
        scratch_shapes=[pltpu.SemaphoreType.DMA((nk,)), pltpu.SemaphoreType.DMA((nk,))],
    )(*shards)


def _adam_math(w, g, m, v):
    m = ADAM_B1 * m + (1.0 - ADAM_B1) * g
    v = ADAM_B2 * v + (1.0 - ADAM_B2) * (g * g)
    m_hat = m / (1.0 - ADAM_B1 ** ADAM_STEP)
    v_hat = v / (1.0 - ADAM_B2 ** ADAM_STEP)
    delta = -ADAM_LR * (m_hat / (jnp.sqrt(v_hat) + ADAM_EPS) + ADAM_WD * w)
    return delta, m, v


def _adam(k, w, g, m, v):
    rows, cols = w.shape
    tr = next(rows // d for d in (1, 2, 4, 8) if rows % (8 * d) == 0 and rows // d <= 256)

    def body(w_ref, g_ref, m_ref, v_ref, d_ref, nm_ref, nv_ref):
        d_ref[...], nm_ref[...], nv_ref[...] = _adam_math(w_ref[...], g_ref[...], m_ref[...], v_ref[...])

    spec = _row(0, (tr, cols))
    return pl.pallas_call(
        body, name=f"adam_{k}", grid=(rows // tr,), in_specs=[spec] * 4, out_specs=[spec] * 3,
        out_shape=[jax.ShapeDtypeStruct(w.shape, f32)] * 3,
        compiler_params=_cp("parallel"),
    )(w, g, m, v)


def _small_sync_adam(part, w, m, v):
    rows = part.shape[0]

    def body(p_ref, w_ref, m_ref, v_ref, g_ref, d_ref, nm_ref, nv_ref, buf, send_sems, recv_sems):
        x, y, c, _, others = _place()
        me, sibling = (x, y, c), (x, y, 1 - c)

        def slot(px, py, pc):
            return buf.at[4 * px + 2 * py + pc]

        def copy(k, block, to, src=None):
            return _remote(slot(*block) if src is None else src, slot(*block), send_sems.at[k], recv_sems.at[k], to)

        buf[4 * x + 2 * y + c] = p_ref[...]
        first = [copy(0, me, sibling, src=p_ref)]
        first += [copy(1 + j, me, (*chip, c), src=p_ref) for j, chip in enumerate(others)]
        for cp in first:
            cp.start()
        passed = [copy(4 + j, (*chip, c), sibling) for j, chip in enumerate(others)]
        for j, chip in enumerate(others):
            copy(1 + j, (*chip, c), me).wait_recv()
            passed[j].start()
        copy(0, sibling, me).wait_recv()
        for j, chip in enumerate(others):
            copy(4 + j, (*chip, 1 - c), me).wait_recv()
        for cp in first + passed:
            cp.wait_send()
        g = buf[0]
        for dev in range(1, N_DEV):
            g = g + buf[dev]
        g_ref[...] = g
        d_ref[...], nm_ref[...], nv_ref[...] = _adam_math(w_ref[...], g, m_ref[...], v_ref[...])

    shape = jax.ShapeDtypeStruct(part.shape, f32)
    return pl.pallas_call(
        body, name="small_sync_adam", in_specs=[VMEM] * 4, out_specs=[VMEM] * 4, out_shape=[shape] * 4,
        scratch_shapes=[pltpu.VMEM((N_DEV, rows, 128), f32), pltpu.SemaphoreType.DMA((7,)),
                        pltpu.SemaphoreType.DMA((7,))],
        compiler_params=pltpu.CompilerParams(vmem_limit_bytes=VMEM_LIMIT),
    )(part, w, m, v)


def _rows128(a):
    return a.reshape(-1, 128)


def _pad_rows(a, rows):
    return jnp.concatenate([a, jnp.zeros((rows - a.shape[0],) + a.shape[1:], a.dtype)], axis=0)


def _pack_small(me, g_pg, g_post, g_final, g_ffn, ln_g, ln_b, conv_b, pool_scale, pool_w, conv_w, g_mix):
    blk_ple = _pad_rows(jnp.concatenate([g_pg, g_post, g_final.reshape(1, D_MODEL)], axis=0), 8)
    blk_ffn = _pad_rows(g_ffn, 8)
    blk_mix = _pad_rows(jnp.concatenate([ln_g, ln_b, conv_b, pool_scale], axis=0), 8)
    cw = lax.dynamic_update_slice(jnp.zeros((N_CHIPS, HALO, 128), f32), _pad_rows(conv_w, HALO)[None], (me, 0, 0))
    blk_in = _pad_rows(g_mix, 8)
    return jnp.concatenate([_rows128(blk_ple), _rows128(blk_ffn), _rows128(blk_mix), _rows128(pool_w),
                            _rows128(cw), _rows128(blk_in)], axis=0)


def _unpack_small(me, packed):
    o = 0

    def take(rows):
        nonlocal o
        blk = packed[o:o + rows]
        o += rows
        return blk

    ple = take(64).reshape(8, D_MODEL)
    ffn = take(64).reshape(8, D_MODEL)
    mixb = take(32).reshape(8, C_CONV)
    pool_w = take(512).reshape(1, 4, POOL_GROUP, POOL_GROUP)
    cw = take(N_CHIPS * HALO).reshape(N_CHIPS, HALO, 128)
    inb = take(64).reshape(8, D_MODEL)
    conv_w = lax.dynamic_slice(cw, (me, 0, 0), (1, HALO, 128))[:, :CONV_K, :]
    return dict(g_ple_gate=ple[0:1], g_ple_post=ple[1:2], g_final=ple[2], loss=ple[3, 0], g_ffn=ffn[0:1],
                ln_g=mixb[0:1], ln_b=mixb[1:2], conv_b=mixb[2:3], pool_scale=mixb[3:4], pool_w=pool_w,
                conv_w=conv_w, g_mix=inb[0:1])


def kernel(x, p, g_mix, w_in, conv_w, conv_b, ln_g, ln_b, pool_w, pool_scale, w_out, g_ffn, w_gate_up, w_down, g_ple_gate, w_ple_gate, w_ple_up, g_ple_post, g_final, loss_target, m_g_mix, m_w_in, m_conv_w, m_conv_b, m_ln_g, m_ln_b, m_pool_w, m_pool_scale, m_w_out, m_g_ffn, m_w_gate_up, m_w_down, m_g_ple_gate, m_w_ple_gate, m_w_ple_up, m_g_ple_post, m_g_final, v_g_mix, v_w_in, v_conv_w, v_conv_b, v_ln_g, v_ln_b, v_pool_w, v_pool_scale, v_w_out, v_g_ffn, v_w_gate_up, v_w_down, v_g_ple_gate, v_w_ple_gate, v_w_ple_up, v_g_ple_post, v_g_final):
    seq = x.shape[1]
    me = 2 * lax.axis_index("x") + lax.axis_index("y")
    chip = me.astype(jnp.int32).reshape(1)
    core = lax.axis_index("c").astype(jnp.int32).reshape(1)
    place = jnp.concatenate([chip, core])
    xs, ps, ts = x.reshape(seq, D_MODEL), p.reshape(seq, D_PLE), loss_target.reshape(seq, D_MODEL)

    big_names = ["w_in", "w_gu", "w_out", "w_down", "w_pg", "w_pu"]
    big = [w_in[0], w_gate_up[0], w_out[0], w_down[0], w_ple_gate[0], w_ple_up[0]]
    big_m = [m_w_in[0], m_w_gate_up[0], m_w_out[0], m_w_down[0], m_w_ple_gate[0], m_w_ple_up[0]]
    big_v = [v_w_in[0], v_w_gate_up[0], v_w_out[0], v_w_down[0], v_w_ple_gate[0], v_w_ple_up[0]]
    b_in, b_gu, b_out, b_down, b_pg, b_pu = [_cast_into_slot(nm, chip, w, bf16) for nm, w in zip(big_names, big)]
    b_cw = _cast_into_slot("conv_w", chip, _pad_rows(conv_w[0], HALO), f32)
    w_in_g, cw_g = _gather_in([b_in, b_cw])
    conv_w_f = cw_g.transpose(1, 0, 2).reshape(HALO, C_CONV)

    z, (w_out_g, w_down_g) = _mix_in(xs, g_mix, w_in_g, [b_out, b_down])
    w_out_f = w_out_g.reshape(D_MODEL, D_MODEL)
    w_down_f = w_down_g.reshape(D_FF, D_MODEL)
    (x1, mix, u1, pooled), (w_gu_g,) = _conv_pool_out(z, xs, conv_w_f, conv_b, ln_g, ln_b, pool_w[0], pool_scale,
                                                      w_out_f, [b_gu])
    (x2, h2, gu), (w_pg_g, w_pu_g) = _ffn_fwd(x1, g_ffn, w_gu_g, w_down_f, [b_pg, b_pu])
    w_pg_f = w_pg_g.reshape(D_MODEL, D_MODEL)
    dx2, d_w_pg, d_w_pu, small_ple = _ple_loss(x2, ps, ts, g_ple_gate, g_ple_post, g_final.reshape(1, D_MODEL),
                                               w_pg_f, w_pu_g)
    def pair_reduce(group, names, grads):
        got = _rs_pair(group, grads)
        return [_rs_add_pair(nm, core, g, r) for nm, g, r in zip(names, grads, got)]

    d_w_down = _ffn_bwd_dw_down(gu, dx2)
    parts_a = pair_reduce("a", ["w_pg", "w_pu", "w_down"],
                          [d_w_pg.reshape(N_CHIPS, -1, D_MODEL), d_w_pu, d_w_down.reshape(N_CHIPS, -1, D_MODEL)])
    (dx1, dgu, small_ffn), landed_a = _ffn_bwd_dx(dx2, x1, gu, g_ffn, w_gu_g, w_down_f, parts_a)
    d_w_gu = _ffn_bwd_dw_gu(h2, dgu)
    du1, dpo, d_w_out, d_pool_w, small_mix = _mix_bwd_local(dx1, mix, u1, pooled, w_out_f, ln_g, ln_b, pool_w[0],
                                                             pool_scale)
    parts_b = pair_reduce("b", ["w_gu", "w_out"], [d_w_gu, d_w_out.reshape(N_CHIPS, -1, D_MODEL)])
    (grad_x, d_w_in, d_conv_w, small_in), landed_b = _in_bwd(du1, dpo, z, xs, dx1, conv_w_f, g_mix, w_in_g, parts_b)
    parts_c = pair_reduce("c", ["w_in"], [d_w_in])
    landed_c = _rs_cross("c", parts_c)
    parts = [parts_c[0], parts_b[0], parts_b[1], parts_a[2], parts_a[0], parts_a[1]]
    landed = [landed_c[0], landed_b[0], landed_b[1], landed_a[2], landed_a[0], landed_a[1]]
    halves = [_rs_sum_chips(nm, place, a, b) for nm, a, b in zip(big_names, landed, parts)]
    big_g = _grad_pair(halves)
    big_upd = [_adam(nm, w, g, m, v) for nm, w, g, m, v in zip(big_names, big, big_g, big_m, big_v)]

    cw_chunks = d_conv_w.reshape(HALO, N_CHIPS, 128).transpose(1, 0, 2)
    part = jnp.concatenate([_rows128(small_ple), _rows128(small_ffn), _rows128(small_mix), _rows128(d_pool_w),
                            _rows128(cw_chunks), _rows128(small_in)], axis=0)
    sw = _pack_small(me, g_ple_gate, g_ple_post, g_final, g_ffn, ln_g, ln_b, conv_b, pool_scale, pool_w, conv_w[0], g_mix)
    sm = _pack_small(me, m_g_ple_gate, m_g_ple_post, m_g_final, m_g_ffn, m_ln_g, m_ln_b, m_conv_b, m_pool_scale,
                     m_pool_w, m_conv_w[0], m_g_mix)
    sv = _pack_small(me, v_g_ple_gate, v_g_ple_post, v_g_final, v_g_ffn, v_ln_g, v_ln_b, v_conv_b, v_pool_scale,
                     v_pool_w, v_conv_w[0], v_g_mix)
    small = [_unpack_small(me, a) for a in _small_sync_adam(part, sw, sm, sv)]

    names = ["g_mix", "w_in", "conv_w", "conv_b", "ln_g", "ln_b", "pool_w", "pool_scale", "w_out", "g_ffn",
             "w_gate_up", "w_down", "g_ple_gate", "w_ple_gate", "w_ple_up", "g_ple_post", "g_final"]
    big_at = {"w_in": 0, "w_gate_up": 1, "w_out": 2, "w_down": 3, "w_ple_gate": 4, "w_ple_up": 5}
    out = [small[0]["loss"], grad_x.reshape(1, seq, D_MODEL)]
    for kind in range(4):
        for nm in names:
            if nm in big_at:
                k = big_at[nm]
                out.append((big_g[k] if kind == 0 else big_upd[k][kind - 1])[None])
            else:
                out.append(small[kind][nm])
    return tuple(out)
```

```python
import functools

import jax
import jax.numpy as jnp
from jax import lax
from jax.experimental import pallas as pl
from jax.experimental.pallas import tpu as pltpu

f32, bf16 = jnp.float32, jnp.bfloat16

EPS = 1e-6
D_MODEL = 1024
C_CONV = 512
C_POOL = 512
POOL_WINDOWS = (2, 4, 8, 16)
POOL_GROUP = 128
CONV_K = 31
D_FF = 2816
D_PLE = 256
N_CHIPS = 4
N_DEV = 8
W_IN_COLS = 2 * C_CONV + C_POOL
W_IN_CHUNK = W_IN_COLS // N_CHIPS
FF_CHUNK = 2 * D_FF // N_CHIPS
PLE_CHUNK = D_MODEL // N_CHIPS
HALO = 32
ROW_TILE = 512
CONV_ROWS = 64
CONV_COLS = (slice(0, 256), slice(256, 512))
FF_SUB = (0, 512, 1024, FF_CHUNK)
VMEM_LIMIT = 56 * 1024 * 1024

ADAM_LR = 0.001
ADAM_B1 = 0.9
ADAM_B2 = 0.999
ADAM_EPS = 1e-08
ADAM_WD = 0.01
ADAM_STEP = 10

MESH = pl.DeviceIdType.MESH
ANY = pl.BlockSpec(memory_space=pl.ANY)
VMEM = pl.BlockSpec(memory_space=pltpu.VMEM)


def _cp(*sem):
    return pltpu.CompilerParams(dimension_semantics=sem, vmem_limit_bytes=VMEM_LIMIT)


def _dot(a, b):
    return jnp.dot(a, b, preferred_element_type=f32)


def _dot_nt(a, b):
    return lax.dot_general(a, b, (((1,), (1,)), ((), ())), preferred_element_type=f32)


def _dot_tn(a, b):
    return lax.dot_general(a, b, (((0,), (0,)), ((), ())), preferred_element_type=f32)


def _sigmoid(v):
    return jax.nn.sigmoid(v)


def _rms_fwd(v, g):
    r = lax.rsqrt(jnp.mean(v * v, axis=-1, keepdims=True) + EPS)
    vh = v * r
    return vh * g, vh, r


def _rms_bwd(dy, vh, r, g):
    dvh = dy * g
    dv = r * (dvh - vh * jnp.mean(dvh * vh, axis=-1, keepdims=True))
    return dv, jnp.sum(dy * vh, axis=0, keepdims=True)


def _silu_grad(v, s):
    return s * (1.0 + v * (1.0 - s))


def _row(i, n):
    return pl.BlockSpec((n[0], n[1]), lambda *a: (a[i], 0))


def _full(shape):
    nd = len(shape)
    return pl.BlockSpec(shape, lambda *a: (0,) * nd)


def _place():
    x, y, c = lax.axis_index("x"), lax.axis_index("y"), lax.axis_index("c")
    others = [(1 - x, y), (x, 1 - y), (1 - x, 1 - y)]
    return x, y, c, 2 * x + y, others


def _remote(src, dst, send_sem, recv_sem, dev):
    return pltpu.make_async_remote_copy(src_ref=src, dst_ref=dst, send_sem=send_sem, recv_sem=recv_sem,
                                        device_id=dev, device_id_type=MESH)


def _cast_into_slot(name, me, w, dtype):
    rows, cols = w.shape
    tr = next(rows // d for d in (1, 2, 4, 8) if rows % (16 * d) == 0 and rows // d <= 512)

    def body(me_ref, w_ref, o_ref):
        o_ref[0] = w_ref[...].astype(dtype)

    return pl.pallas_call(
        body, name=f"cast_{name}",
        grid_spec=pltpu.PrefetchScalarGridSpec(
            num_scalar_prefetch=1, grid=(rows // tr,),
            in_specs=[pl.BlockSpec((tr, cols), lambda r, me_ref: (r, 0))],
            out_specs=pl.BlockSpec((1, tr, cols), lambda r, me_ref: (me_ref[0], r, 0))),
        out_shape=jax.ShapeDtypeStruct((N_CHIPS, rows, cols), dtype),
        compiler_params=_cp("parallel"),
    )(me, w)


class _Gather:
    def __init__(self, bufs, send_sems, recv_sems):
        self.bufs, self.send_sems, self.recv_sems = bufs, send_sems, recv_sems
        self.x, self.y, self.c, self.me, self.others = _place()
        self.halves = [b.shape[1] // 2 for b in bufs]

    def _piece(self, k, chip, half):
        return self.bufs[k].at[chip, pl.ds(half * self.halves[k], self.halves[k]), :]

    def _ici(self, k, j, chip):
        ox, oy = self.others[j]
        piece = self._piece(k, chip, self.c)
        return _remote(piece, piece, self.send_sems.at[6 * k + j], self.recv_sems.at[6 * k + j], (ox, oy, self.c))

    def _pair(self, k, j, half):
        ox, oy = self.others[j]
        piece = self._piece(k, 2 * ox + oy, half)
        return _remote(piece, piece, self.send_sems.at[6 * k + 3 + j], self.recv_sems.at[6 * k + 3 + j],
                       (self.x, self.y, 1 - self.c))

    def _each(self, ks=None):
        return [(k, j) for k in (range(len(self.bufs)) if ks is None else ks) for j in range(3)]

    def chip(self, j):
        ox, oy = self.others[j]
        return 2 * ox + oy

    def start(self):
        for k, j in self._each():
            self._ici(k, j, self.me).start()

    def forward(self, pairs=None):
        for k, j in self._each() if pairs is None else pairs:
            self._ici(k, j, self.chip(j)).wait_recv()
            self._pair(k, j, self.c).start()

    def landed(self, pairs):
        for k, j in pairs:
            self._pair(k, j, 1 - self.c).wait_recv()

    def finish(self, ks=None):
        self.landed(self._each(ks))
        for k, j in self._each():
            self._ici(k, j, self.me).wait_send()
            self._pair(k, j, self.c).wait_send()

    @staticmethod
    def scratch(n):
        return [pltpu.SemaphoreType.DMA((6 * n,)), pltpu.SemaphoreType.DMA((6 * n,))]


def _carried(bufs):
    n = len(bufs)
    return dict(in_specs=[ANY] * n, out_specs=[ANY] * n,
                out_shape=[jax.ShapeDtypeStruct(b.shape, b.dtype) for b in bufs], scratch=_Gather.scratch(n))


def _mix_in(x, g_mix, order, carry):
    s = x.shape[0]
    tm = min(2 * ROW_TILE, s)
    n = s // tm
    nc = len(carry)
    cs = _carried(carry)

    def body(order_ref, x_ref, g_ref, *refs):
        z_ref = refs[nc]
        bufs = refs[nc + 1:2 * nc + 1]
        h_ref, w_ref, w_sem = refs[2 * nc + 1:2 * nc + 4]
        gather = _Gather(bufs, *refs[2 * nc + 4:])
        q, i = pl.program_id(0), pl.program_id(1)
        first = i == 0
        pl.when(jnp.logical_and(q == 0, first))(gather.start)
        for j in range(3):

            @pl.when(jnp.logical_and(q == j + 1, first))
            def _():
                gather.forward([(0, j)])
                gather.landed([(0, j)])

        @pl.when(first)
        def _():
            load = pltpu.make_async_copy(bufs[0].at[order_ref[q]], w_ref, w_sem)
            load.start()
            load.wait()

        @pl.when(q == 0)
        def _():
            h, _, _ = _rms_fwd(x_ref[...], g_ref[...])
            h_ref[i] = h.astype(bf16)

        z_ref[...] = _dot(h_ref[i], w_ref[...])

        @pl.when(jnp.logical_and(q == N_CHIPS - 1, i == n - 1))
        def _():
            rest = gather._each(range(1, nc))
            gather.forward(rest)
            gather.finish(range(1, nc))

    res = pl.pallas_call(
        body, name="mix_in",
        grid_spec=pltpu.PrefetchScalarGridSpec(
            num_scalar_prefetch=1, grid=(N_CHIPS, n),
            in_specs=[pl.BlockSpec((tm, D_MODEL), lambda q, i, order_ref: (jnp.where(q == 0, i, 0), 0)),
                      pl.BlockSpec((1, D_MODEL), lambda q, i, order_ref: (0, 0))] + cs["in_specs"],
            out_specs=[pl.BlockSpec((tm, W_IN_CHUNK), lambda q, i, order_ref: (i, order_ref[q]))] + cs["out_specs"],
            scratch_shapes=[pltpu.VMEM((n, tm, D_MODEL), bf16), pltpu.VMEM((D_MODEL, W_IN_CHUNK), bf16),
                            pltpu.SemaphoreType.DMA(())] + cs["scratch"]),
        out_shape=[jax.ShapeDtypeStruct((s, W_IN_COLS), f32)] + cs["out_shape"],
        input_output_aliases={3 + k: 1 + k for k in range(nc)},
        compiler_params=_cp("arbitrary", "arbitrary"),
    )(order, x, g_mix, *carry)
    return res[0], res[1:]


def _tap_offsets(lo, hi):
    groups = [[o for o in range(lo, hi + 1) if o % 8 == s] for s in range(8)]
    return [g for g in groups if g]


def _tap_sum(buf, w_ref, row0, cols, tap_of):
    acc = jnp.zeros((CONV_ROWS, cols.stop - cols.start), f32)
    for offs in _tap_offsets(0, CONV_K - 1):
        slab = buf[pl.ds(row0 + offs[0], offs[-1] - offs[0] + CONV_ROWS), cols]
        for o in offs:
            acc = acc + w_ref[pl.ds(tap_of(o), 1), cols] * slab[o - offs[0]:o - offs[0] + CONV_ROWS]
    return acc


def _pool_counts(tm, w, first_row):
    t1 = (lax.broadcasted_iota(jnp.int32, (tm, 1), 0) + first_row + 1).astype(f32)
    return jnp.minimum(t1, float(w))


def _conv_pool_out(z, x, conv_w, conv_b, ln_g, ln_b, pool_w, pool_scale, w_out, carry):
    s = x.shape[0]
    tm = min(ROW_TILE, s)
    n = s // tm
    hb = tm // HALO
    nc = len(carry)
    cs = _carried(carry)

    def body(z_ref, zp_ref, x_ref, cw_ref, cb_ref, lg_ref, lb_ref, pw_ref, ps_ref, wo_ref, *refs):
        x1_ref, mix_ref, u1_ref, pooled_ref = refs[nc:nc + 4]
        ubuf, vbuf = refs[2 * nc + 4:2 * nc + 6]
        gather = _Gather(refs[nc + 4:2 * nc + 4], *refs[2 * nc + 6:])
        i = pl.program_id(0)
        pl.when(i == 0)(gather.start)
        pl.when(i == max(n - 2, 0))(gather.forward)
        keep = (i > 0).astype(f32)
        zp = zp_ref[...] * keep
        ubuf[0:HALO, :] = zp[:, :C_CONV] * _sigmoid(zp[:, C_CONV:2 * C_CONV])
        vbuf[0:HALO, :] = zp[:, 2 * C_CONV:]
        ubuf[HALO:, :] = z_ref[:, :C_CONV] * _sigmoid(z_ref[:, C_CONV:2 * C_CONV])
        vbuf[HALO:, :] = z_ref[:, 2 * C_CONV:]
        off = HALO - (CONV_K - 1)
        for r0 in range(0, tm, CONV_ROWS):
            for cols in CONV_COLS:
                u1_ref[r0:r0 + CONV_ROWS, cols] = cb_ref[:, cols] + _tap_sum(ubuf, cw_ref, r0 + off, cols, lambda o: o)
        u1 = u1_ref[...]
        mu = jnp.mean(u1, axis=-1, keepdims=True)
        uc = u1 - mu
        rstd = lax.rsqrt(jnp.mean(uc * uc, axis=-1, keepdims=True) + EPS)
        u2 = uc * rstd * lg_ref[...] + lb_ref[...]
        mix_ref[:, :C_CONV] = (u2 * _sigmoid(u2)).astype(bf16)
        for g, w in enumerate(POOL_WINDOWS):
            cols = slice(g * POOL_GROUP, (g + 1) * POOL_GROUP)
            acc = vbuf[pl.ds(HALO, tm), cols]
            vg = acc
            for d in range(1, w):
                acc = acc + vbuf[pl.ds(HALO - d, tm), cols]
            pooled = (acc / _pool_counts(tm, w, i * tm) - vg).astype(bf16)
            pooled_ref[:, cols] = pooled
            mixed = _dot(pooled, pw_ref[g].astype(bf16))
            mix_ref[:, C_CONV + g * POOL_GROUP:C_CONV + (g + 1) * POOL_GROUP] = (mixed * ps_ref[:, cols]).astype(bf16)
        x1_ref[...] = x_ref[...] + _dot(mix_ref[...], wo_ref[...])
        pl.when(i == n - 1)(gather.finish)

    res = pl.pallas_call(
        body, name="conv_pool_out", grid=(n,),
        in_specs=[_row(0, (tm, W_IN_COLS)),
                  pl.BlockSpec((HALO, W_IN_COLS), lambda i: (jnp.maximum(i * hb - 1, 0), 0)),
                  _row(0, (tm, D_MODEL)), _full((HALO, C_CONV)), _full((1, C_CONV)), _full((1, C_CONV)),
                  _full((1, C_CONV)), _full((4, POOL_GROUP, POOL_GROUP)), _full((1, C_POOL)),
                  _full((D_MODEL, D_MODEL))] + cs["in_specs"],
        out_specs=[_row(0, (tm, D_MODEL)), _row(0, (tm, D_MODEL)), _row(0, (tm, C_CONV)), _row(0, (tm, C_POOL))]
        + cs["out_specs"],
        out_shape=[jax.ShapeDtypeStruct((s, D_MODEL), f32), jax.ShapeDtypeStruct((s, D_MODEL), bf16),
                   jax.ShapeDtypeStruct((s, C_CONV), f32), jax.ShapeDtypeStruct((s, C_POOL), bf16)] + cs["out_shape"],
        input_output_aliases={10 + k: 4 + k for k in range(nc)},
        scratch_shapes=[pltpu.VMEM((HALO + tm, C_CONV), f32), pltpu.VMEM((HALO + tm, C_POOL), f32)] + cs["scratch"],
        compiler_params=_cp("arbitrary"),
    )(z, z, x, conv_w, conv_b, ln_g, ln_b, pool_w, pool_scale, w_out, *carry)
    return res[:4], res[4:]


def _ffn_fwd(x1, g_ffn, w_gu_g, w_down, carry):
    s = x1.shape[0]
    tm = min(ROW_TILE, s)
    n = s // tm
    nc = len(carry)
    cs = _carried(carry)

    def body(x1_ref, g_ref, wg_ref, wu_ref, wd_ref, *refs):
        x2_ref, h2_ref, gu_ref = refs[nc:nc + 3]
        acc_ref = refs[2 * nc + 3]
        gather = _Gather(refs[nc + 3:2 * nc + 3], *refs[2 * nc + 4:])
        i, c = pl.program_id(0), pl.program_id(1)
        pl.when(jnp.logical_and(i == 0, c == 0))(gather.start)
        pl.when(jnp.logical_and(i == n - 1, c == 0))(gather.forward)

        @pl.when(c == 0)
        def _():
            h, _, _ = _rms_fwd(x1_ref[...], g_ref[...])
            h2_ref[...] = h.astype(bf16)
            acc_ref[...] = jnp.zeros_like(acc_ref)

        h = h2_ref[...]
        for lo, hi in zip(FF_SUB[:-1], FF_SUB[1:]):
            gate = _dot(h, wg_ref[0, :, lo:hi])
            up = _dot(h, wu_ref[0, :, lo:hi])
            gu_ref[0, :, lo:hi] = gate.astype(bf16)
            gu_ref[1, :, lo:hi] = up.astype(bf16)
            f = (gate * _sigmoid(gate) * up).astype(bf16)
            acc_ref[...] += _dot(f, wd_ref[lo:hi, :])

        @pl.when(c == 1)
        def _():
            x2_ref[...] = x1_ref[...] + acc_ref[...]

        pl.when(jnp.logical_and(i == n - 1, c == 1))(gather.finish)

    res = pl.pallas_call(
        body, name="ffn_fwd", grid=(n, 2),
        in_specs=[_row(0, (tm, D_MODEL)), _full((1, D_MODEL)),
                  pl.BlockSpec((1, D_MODEL, FF_CHUNK), lambda i, c: (c, 0, 0)),
                  pl.BlockSpec((1, D_MODEL, FF_CHUNK), lambda i, c: (2 + c, 0, 0)),
                  pl.BlockSpec((FF_CHUNK, D_MODEL), lambda i, c: (c, 0))] + cs["in_specs"],
        out_specs=[_row(0, (tm, D_MODEL)), _row(0, (tm, D_MODEL)),
                   pl.BlockSpec((2, tm, FF_CHUNK), lambda i, c: (0, i, c))] + cs["out_specs"],
        out_shape=[jax.ShapeDtypeStruct((s, D_MODEL), f32), jax.ShapeDtypeStruct((s, D_MODEL), bf16),
                   jax.ShapeDtypeStruct((2, s, D_FF), bf16)] + cs["out_shape"],
        input_output_aliases={5 + k: 3 + k for k in range(nc)},
        scratch_shapes=[pltpu.VMEM((tm, D_MODEL), f32)] + cs["scratch"],
        compiler_params=_cp("arbitrary", "arbitrary"),
    )(x1, g_ffn, w_gu_g, w_gu_g, w_down, *carry)
    return res[:3], res[3:]


def _ple_loss(x2, p, target, g_pg, g_post, g_final, w_pg, w_pu_g):
    s = x2.shape[0]
    tm = min(ROW_TILE, s)
    n = s // tm

    def body(x2_ref, p_ref, t_ref, gpg_ref, gpo_ref, gf_ref, wpg_ref, wpu_ref,
             dx2_ref, dwpg_ref, dwpu_ref, small_ref, apg_ref, apu_ref):
        i = pl.program_id(0)

        @pl.when(i == 0)
        def _():
            apg_ref[...] = jnp.zeros_like(apg_ref)
            apu_ref[...] = jnp.zeros_like(apu_ref)
            small_ref[...] = jnp.zeros_like(small_ref)

        x2 = x2_ref[...]
        h3, x2h, r2 = _rms_fwd(x2, gpg_ref[...])
        h3b = h3.astype(bf16)
        gate = _sigmoid(_dot(h3b, wpg_ref[...]))
        pb = p_ref[...].astype(bf16)
        pe = jnp.concatenate([_dot(pb, wpu_ref[j]) for j in range(N_CHIPS)], axis=-1)
        e, peh, rp = _rms_fwd(pe, gpo_ref[...])
        x3 = x2 + gate * e
        y, x3h, r3 = _rms_fwd(x3, gf_ref[...])
        d = y - t_ref[...]
        loss = 0.5 * jnp.sum(jnp.sum(d * d, axis=-1, keepdims=True) * (1.0 / D_MODEL), axis=0, keepdims=True)
        dx3, dgf = _rms_bwd(d * (1.0 / D_MODEL), x3h, r3, gf_ref[...])
        dpe, dgpo = _rms_bwd(dx3 * gate, peh, rp, gpo_ref[...])
        dgl = (dx3 * e * gate * (1.0 - gate)).astype(bf16)
        apg_ref[...] += _dot_tn(h3b, dgl)
        apu_ref[...] += _dot_tn(pb, dpe.astype(bf16))
        dh3 = _dot_nt(dgl, wpg_ref[...])
        dx2b, dgpg = _rms_bwd(dh3, x2h, r2, gpg_ref[...])
        dx2_ref[...] = dx3 + dx2b
        small_ref[0:1, :] += dgpg
        small_ref[1:2, :] += dgpo
        small_ref[2:3, :] += dgf
        small_ref[3:4, :] += jnp.broadcast_to(loss, (1, D_MODEL))

        @pl.when(i == n - 1)
        def _():
            dwpg_ref[...] = apg_ref[...].astype(bf16)
            for j in range(N_CHIPS):
                dwpu_ref[j] = apu_ref[:, j * PLE_CHUNK:(j + 1) * PLE_CHUNK].astype(bf16)

    return pl.pallas_call(
        body, name="ple_loss", grid=(n,),
        in_specs=[_row(0, (tm, D_MODEL)), _row(0, (tm, D_PLE)), _row(0, (tm, D_MODEL)),
                  _full((1, D_MODEL)), _full((1, D_MODEL)), _full((1, D_MODEL)),
                  _full((D_MODEL, D_MODEL)), _full((N_CHIPS, D_PLE, PLE_CHUNK))],
        out_specs=[_row(0, (tm, D_MODEL)), _full((D_MODEL, D_MODEL)), _full((N_CHIPS, D_PLE, PLE_CHUNK)),
                   _full((8, D_MODEL))],
        out_shape=[jax.ShapeDtypeStruct((s, D_MODEL), f32), jax.ShapeDtypeStruct((D_MODEL, D_MODEL), bf16),
                   jax.ShapeDtypeStruct((N_CHIPS, D_PLE, PLE_CHUNK), bf16), jax.ShapeDtypeStruct((8, D_MODEL), f32)],
        scratch_shapes=[pltpu.VMEM((D_MODEL, D_MODEL), f32), pltpu.VMEM((D_PLE, D_MODEL), f32)],
        compiler_params=_cp("arbitrary"),
    )(x2, p, target, g_pg, g_post, g_final, w_pg, w_pu_g)


def _crossed(parts):
    n = len(parts)
    return dict(in_specs=[ANY] * n, out_specs=[ANY] * n,
                out_shape=[jax.ShapeDtypeStruct(a.shape, a.dtype) for a in parts], scratch=_Cross.scratch(n))


def _ffn_bwd_dx(dx2, x1, gu, g_ffn, w_gu_g, w_down, parts):
    s = x1.shape[0]
    tm = min(ROW_TILE, s)
    n = s // tm
    nc = len(parts)
    cs = _crossed(parts)

    def body(dx2_ref, x1_ref, gu_ref, g_ref, wg_ref, wu_ref, wd_ref, *refs):
        dx1_ref, dgu_ref, small_ref = refs[nc:nc + 3]
        acc_ref = refs[2 * nc + 3]
        cross = _Cross(refs[:nc], refs[nc + 3:2 * nc + 3], *refs[2 * nc + 4:])
        i, c = pl.program_id(0), pl.program_id(1)
        pl.when(jnp.logical_and(i == 0, c == 0))(cross.start)

        @pl.when(jnp.logical_and(i == 0, c == 0))
        def _():
            small_ref[...] = jnp.zeros_like(small_ref)

        @pl.when(c == 0)
        def _():
            acc_ref[...] = jnp.zeros_like(acc_ref)

        dyb = dx2_ref[...].astype(bf16)
        for lo, hi in zip(FF_SUB[:-1], FF_SUB[1:]):
            df = _dot_nt(dyb, wd_ref[lo:hi, :])
            gate = gu_ref[0, :, lo:hi].astype(f32)
            up = gu_ref[1, :, lo:hi].astype(f32)
            sg = _sigmoid(gate)
            dgate = (df * up * _silu_grad(gate, sg)).astype(bf16)
            dup = (df * gate * sg).astype(bf16)
            dgu_ref[0, :, lo:hi] = dgate
            dgu_ref[1, :, lo:hi] = dup
            acc_ref[...] += _dot_nt(dgate, wg_ref[0, :, lo:hi]) + _dot_nt(dup, wu_ref[0, :, lo:hi])

        @pl.when(c == 1)
        def _():
            _, x1h, r1 = _rms_fwd(x1_ref[...], g_ref[...])
            dx1b, dg = _rms_bwd(acc_ref[...], x1h, r1, g_ref[...])
            dx1_ref[...] = dx2_ref[...] + dx1b
            small_ref[0:1, :] += dg

        pl.when(jnp.logical_and(i == n - 1, c == 1))(cross.finish)

    res = pl.pallas_call(
        body, name="ffn_bwd_dx", grid=(n, 2),
        in_specs=[_row(0, (tm, D_MODEL)), _row(0, (tm, D_MODEL)),
                  pl.BlockSpec((2, tm, FF_CHUNK), lambda i, c: (0, i, c)), _full((1, D_MODEL)),
                  pl.BlockSpec((1, D_MODEL, FF_CHUNK), lambda i, c: (c, 0, 0)),
                  pl.BlockSpec((1, D_MODEL, FF_CHUNK), lambda i, c: (2 + c, 0, 0)),
                  pl.BlockSpec((FF_CHUNK, D_MODEL), lambda i, c: (c, 0))] + cs["in_specs"],
        out_specs=[_row(0, (tm, D_MODEL)), pl.BlockSpec((2, tm, FF_CHUNK), lambda i, c: (0, i, c)),
                   _full((8, D_MODEL))] + cs["out_specs"],
        out_shape=[jax.ShapeDtypeStruct((s, D_MODEL), f32), jax.ShapeDtypeStruct((2, s, D_FF), bf16),
                   jax.ShapeDtypeStruct((8, D_MODEL), f32)] + cs["out_shape"],
        scratch_shapes=[pltpu.VMEM((tm, D_MODEL), f32)] + cs["scratch"],
        compiler_params=_cp("arbitrary", "arbitrary"),
    )(dx2, x1, gu, g_ffn, w_gu_g, w_gu_g, w_down, *parts)
    return res[:3], res[3:]


def _ffn_bwd_dw_gu(h2, dgu):
    s = h2.shape[0]
    ts = min(ROW_TILE, s)
    n = s // ts

    def body(h_ref, d_ref, o_ref, acc_ref):
        t = pl.program_id(1)

        @pl.when(t == 0)
        def _():
            acc_ref[...] = jnp.zeros_like(acc_ref)

        acc_ref[...] += _dot_tn(h_ref[...], d_ref[0])

        @pl.when(t == n - 1)
        def _():
            o_ref[0] = acc_ref[...].astype(bf16)

    return pl.pallas_call(
        body, name="ffn_bwd_dw_gu", grid=(N_CHIPS, n),
        in_specs=[pl.BlockSpec((ts, D_MODEL), lambda j, t: (t, 0)),
                  pl.BlockSpec((1, ts, FF_CHUNK), lambda j, t: (j // 2, t, j % 2))],
        out_specs=pl.BlockSpec((1, D_MODEL, FF_CHUNK), lambda j, t: (j, 0, 0)),
        out_shape=jax.ShapeDtypeStruct((N_CHIPS, D_MODEL, FF_CHUNK), bf16),
        scratch_shapes=[pltpu.VMEM((D_MODEL, FF_CHUNK), f32)],
        compiler_params=_cp("parallel", "arbitrary"),
    )(h2, dgu)


def _ffn_bwd_dw_down(gu, dx2):
    s = dx2.shape[0]
    ts = min(ROW_TILE, s)
    n = s // ts

    def body(gu_ref, d_ref, o_ref, acc_ref):
        t = pl.program_id(1)

        @pl.when(t == 0)
        def _():
            acc_ref[...] = jnp.zeros_like(acc_ref)

        gate = gu_ref[0].astype(f32)
        f = (gate * _sigmoid(gate) * gu_ref[1].astype(f32)).astype(bf16)
        acc_ref[...] += _dot_tn(f, d_ref[...].astype(bf16))

        @pl.when(t == n - 1)
        def _():
            o_ref[...] = acc_ref[...].astype(bf16)

    return pl.pallas_call(
        body, name="ffn_bwd_dw_down", grid=(2, n),
        in_specs=[pl.BlockSpec((2, ts, FF_CHUNK), lambda c, t: (0, t, c)),
                  pl.BlockSpec((ts, D_MODEL), lambda c, t: (t, 0))],
        out_specs=pl.BlockSpec((FF_CHUNK, D_MODEL), lambda c, t: (c, 0)),
        out_shape=jax.ShapeDtypeStruct((D_FF, D_MODEL), bf16),
        scratch_shapes=[pltpu.VMEM((FF_CHUNK, D_MODEL), f32)],
        compiler_params=_cp("parallel", "arbitrary"),
    )(gu, dx2)


def _mix_bwd_local(dx1, mix, u1, pooled, w_out, ln_g, ln_b, pool_w, pool_scale):
    s = dx1.shape[0]
    tm = min(ROW_TILE, s)
    n = s // tm

    def body(dx1_ref, mix_ref, u1_ref, po_ref, wo_ref, lg_ref, lb_ref, pw_ref, ps_ref,
             du1_ref, dpo_ref, dwo_ref, dpw_ref, small_ref, awo_ref):
        i = pl.program_id(0)

        @pl.when(i == 0)
        def _():
            awo_ref[...] = jnp.zeros_like(awo_ref)
            dpw_ref[...] = jnp.zeros_like(dpw_ref)
            small_ref[...] = jnp.zeros_like(small_ref)

        dyb = dx1_ref[...].astype(bf16)
        dmix = _dot_nt(dyb, wo_ref[...])
        awo_ref[...] += _dot_tn(mix_ref[...], dyb)
        u1 = u1_ref[...]
        mu = jnp.mean(u1, axis=-1, keepdims=True)
        uc = u1 - mu
        rstd = lax.rsqrt(jnp.mean(uc * uc, axis=-1, keepdims=True) + EPS)
        uh = uc * rstd
        u2 = uh * lg_ref[...] + lb_ref[...]
        du2 = dmix[:, :C_CONV] * _silu_grad(u2, _sigmoid(u2))
        duh = du2 * lg_ref[...]
        du1 = rstd * (duh - jnp.mean(duh, axis=-1, keepdims=True) - uh * jnp.mean(duh * uh, axis=-1, keepdims=True))
        du1_ref[...] = du1
        small_ref[0:1, :] += jnp.sum(du2 * uh, axis=0, keepdims=True)
        small_ref[1:2, :] += jnp.sum(du2, axis=0, keepdims=True)
        small_ref[2:3, :] += jnp.sum(du1, axis=0, keepdims=True)
        for g in range(len(POOL_WINDOWS)):
            cols = slice(g * POOL_GROUP, (g + 1) * POOL_GROUP)
            dq = dmix[:, C_CONV + g * POOL_GROUP:C_CONV + (g + 1) * POOL_GROUP]
            pwb = pw_ref[g].astype(bf16)
            pg = po_ref[:, cols]
            mixed = _dot(pg, pwb)
            small_ref[3:4, cols] += jnp.sum(dq * mixed, axis=0, keepdims=True)
            dmixed = (dq * ps_ref[:, cols]).astype(bf16)
            dpw_ref[g] += _dot_tn(pg, dmixed)
            dpo_ref[:, cols] = _dot_nt(dmixed, pwb)

        @pl.when(i == n - 1)
        def _():
            dwo_ref[...] = awo_ref[...].astype(bf16)

    return pl.pallas_call(
        body, name="mix_bwd_local", grid=(n,),
        in_specs=[_row(0, (tm, D_MODEL)), _row(0, (tm, D_MODEL)), _row(0, (tm, C_CONV)), _row(0, (tm, C_POOL)),
                  _full((D_MODEL, D_MODEL)), _full((1, C_CONV)), _full((1, C_CONV)),
                  _full((4, POOL_GROUP, POOL_GROUP)), _full((1, C_POOL))],
        out_specs=[_row(0, (tm, C_CONV)), _row(0, (tm, C_POOL)), _full((D_MODEL, D_MODEL)),
                   _full((4, POOL_GROUP, POOL_GROUP)), _full((8, C_CONV))],
        out_shape=[jax.ShapeDtypeStruct((s, C_CONV), f32), jax.ShapeDtypeStruct((s, C_POOL), f32),
                   jax.ShapeDtypeStruct((D_MODEL, D_MODEL), bf16),
                   jax.ShapeDtypeStruct((4, POOL_GROUP, POOL_GROUP), f32), jax.ShapeDtypeStruct((8, C_CONV), f32)],
        scratch_shapes=[pltpu.VMEM((D_MODEL, D_MODEL), f32)],
        compiler_params=_cp("arbitrary"),
    )(dx1, mix, u1, pooled, w_out, ln_g, ln_b, pool_w, pool_scale)


def _in_bwd(du1, dpo, z, x, dx1, conv_w, g_mix, w_in_g, parts):
    s = x.shape[0]
    tm = min(ROW_TILE, s)
    n = s // tm
    hb = tm // HALO
    last = s // HALO - 1
    nc = len(parts)
    cs = _crossed(parts)

    def body(du_ref, dun_ref, dp_ref, dpn_ref, z_ref, zp_ref, x_ref, dx1_ref, cw_ref, g_ref, w_ref, *refs):
        gx_ref, dw_ref, dcw_ref, small_ref = refs[nc:nc + 4]
        dbuf, pbuf, ubuf, dz_ref, acc_ref, dcw_acc = refs[2 * nc + 4:2 * nc + 10]
        cross = _Cross(refs[:nc], refs[nc + 4:2 * nc + 4], *refs[2 * nc + 10:])
        i = pl.program_id(0)
        pl.when(i == 0)(cross.start)

        @pl.when(i == 0)
        def _():
            acc_ref[...] = jnp.zeros_like(acc_ref)
            dcw_acc[...] = jnp.zeros_like(dcw_acc)
            small_ref[...] = jnp.zeros_like(small_ref)

        keep_prev = (i > 0).astype(f32)
        keep_next = (i < n - 1).astype(f32)
        zp = zp_ref[...] * keep_prev
        ubuf[0:HALO, :] = zp[:, :C_CONV] * _sigmoid(zp[:, C_CONV:2 * C_CONV])
        a = z_ref[:, :C_CONV]
        sb = _sigmoid(z_ref[:, C_CONV:2 * C_CONV])
        ubuf[HALO:, :] = a * sb
        dbuf[0:tm, :] = du_ref[...]
        dbuf[tm:, :] = dun_ref[...] * keep_next
        for g, w in enumerate(POOL_WINDOWS):
            cols = slice(g * POOL_GROUP, (g + 1) * POOL_GROUP)
            pbuf[0:tm, cols] = dp_ref[:, cols] / _pool_counts(tm, w, i * tm)
            pbuf[tm:, cols] = dpn_ref[:, cols] * keep_next / _pool_counts(HALO, w, (i + 1) * tm)
        off = HALO - (CONV_K - 1)
        for r0 in range(0, tm, CONV_ROWS):
            rows = slice(r0, r0 + CONV_ROWS)
            for cols in CONV_COLS:
                du0 = _tap_sum(dbuf, cw_ref, r0, cols, lambda o: CONV_K - 1 - o)
                av, sv = z_ref[rows, cols], _sigmoid(z_ref[rows, C_CONV + cols.start:C_CONV + cols.stop])
                dz_ref[rows, cols] = (du0 * sv).astype(bf16)
                dz_ref[rows, C_CONV + cols.start:C_CONV + cols.stop] = (du0 * av * sv * (1.0 - sv)).astype(bf16)
                d = dbuf[rows, cols]
                for offs in _tap_offsets(off, off + CONV_K - 1):
                    slab = ubuf[pl.ds(r0 + offs[0], offs[-1] - offs[0] + CONV_ROWS), cols]
                    for o in offs:
                        prod = d * slab[o - offs[0]:o - offs[0] + CONV_ROWS]
                        fold = prod[0:8]
                        for r in range(8, CONV_ROWS, 8):
                            fold = fold + prod[r:r + 8]
                        dcw_acc[o - off, :, cols] += fold
        for g, w in enumerate(POOL_WINDOWS):
            cols = slice(g * POOL_GROUP, (g + 1) * POOL_GROUP)
            acc = pbuf[pl.ds(0, tm), cols]
            for d in range(1, w):
                acc = acc + pbuf[pl.ds(d, tm), cols]
            dz_ref[:, 2 * C_CONV + g * POOL_GROUP:2 * C_CONV + (g + 1) * POOL_GROUP] = (
                acc - dp_ref[:, cols]).astype(bf16)
        h, xh, r = _rms_fwd(x_ref[...], g_ref[...])
        dz = dz_ref[...]
        acc_ref[...] += _dot_tn(h.astype(bf16), dz)
        dh = _dot_nt(dz[:, 0:W_IN_CHUNK], w_ref[0])
        for j in range(1, N_CHIPS):
            dh = dh + _dot_nt(dz[:, j * W_IN_CHUNK:(j + 1) * W_IN_CHUNK], w_ref[j])
        dxb, dg = _rms_bwd(dh, xh, r, g_ref[...])
        gx_ref[...] = dx1_ref[...] + dxb
        small_ref[0:1, :] += dg

        @pl.when(i == n - 1)
        def _():
            for j in range(N_CHIPS):
                dw_ref[j] = acc_ref[:, j * W_IN_CHUNK:(j + 1) * W_IN_CHUNK].astype(bf16)
            dcw_ref[...] = jnp.sum(dcw_acc[...], axis=1)

        pl.when(i == n - 1)(cross.finish)

    nxt = lambda i: (jnp.minimum((i + 1) * hb, last), 0)
    res = pl.pallas_call(
        body, name="in_bwd", grid=(n,),
        in_specs=[_row(0, (tm, C_CONV)), pl.BlockSpec((HALO, C_CONV), nxt),
                  _row(0, (tm, C_POOL)), pl.BlockSpec((HALO, C_POOL), nxt),
                  _row(0, (tm, W_IN_COLS)),
                  pl.BlockSpec((HALO, W_IN_COLS), lambda i: (jnp.maximum(i * hb - 1, 0), 0)),
                  _row(0, (tm, D_MODEL)), _row(0, (tm, D_MODEL)), _full((HALO, C_CONV)), _full((1, D_MODEL)),
                  _full((N_CHIPS, D_MODEL, W_IN_CHUNK))] + cs["in_specs"],
        out_specs=[_row(0, (tm, D_MODEL)), _full((N_CHIPS, D_MODEL, W_IN_CHUNK)), _full((HALO, C_CONV)),
                   _full((8, D_MODEL))] + cs["out_specs"],
        out_shape=[jax.ShapeDtypeStruct((s, D_MODEL), f32), jax.ShapeDtypeStruct((N_CHIPS, D_MODEL, W_IN_CHUNK), bf16),
                   jax.ShapeDtypeStruct((HALO, C_CONV), f32), jax.ShapeDtypeStruct((8, D_MODEL), f32)] + cs["out_shape"],
        scratch_shapes=[pltpu.VMEM((tm + HALO, C_CONV), f32), pltpu.VMEM((tm + HALO, C_POOL), f32),
                        pltpu.VMEM((HALO + tm, C_CONV), f32), pltpu.VMEM((tm, W_IN_COLS), bf16),
                        pltpu.VMEM((D_MODEL, W_IN_COLS), f32), pltpu.VMEM((HALO, 8, C_CONV), f32)] + cs["scratch"],
        compiler_params=_cp("arbitrary"),
    )(du1, du1, dpo, dpo, z, z, x, dx1, conv_w, g_mix, w_in_g, *parts)
    return res[:4], res[4:]


def _rs_pair(name, grads):
    nk = len(grads)
    halves = [g.shape[1] // 2 for g in grads]

    def body(*refs):
        ins, outs = refs[:nk], refs[nk:2 * nk]
        send_sems, recv_sems = refs[2 * nk:]
        x, y, c, _, _ = _place()
        cps = [_remote(ins[k].at[:, pl.ds((1 - c) * halves[k], halves[k]), :], outs[k],
                       send_sems.at[k], recv_sems.at[k], (x, y, 1 - c)) for k in range(nk)]
        for cp in cps:
            cp.start()
        for cp in cps:
            cp.wait()

    return pl.pallas_call(
        body, name=f"rs_pair_{name}", in_specs=[ANY] * nk, out_specs=[ANY] * nk,
        out_shape=[jax.ShapeDtypeStruct((N_CHIPS, h, g.shape[2]), bf16) for g, h in zip(grads, halves)],
        scratch_shapes=[pltpu.SemaphoreType.DMA((nk,)), pltpu.SemaphoreType.DMA((nk,))],
    )(*grads)


def _rs_add_pair(name, core, grad, recv):
    _, half, cols = recv.shape

    def body(core_ref, g_ref, r_ref, o_ref):
        o_ref[...] = (g_ref[...].astype(f32) + r_ref[...].astype(f32)).astype(bf16)

    return pl.pallas_call(
        body, name=f"rs_add_pair_{name}",
        grid_spec=pltpu.PrefetchScalarGridSpec(
            num_scalar_prefetch=1, grid=(N_CHIPS,),
            in_specs=[pl.BlockSpec((1, half, cols), lambda j, core_ref: (j, core_ref[0], 0)),
                      pl.BlockSpec((1, half, cols), lambda j, core_ref: (j, 0, 0))],
            out_specs=pl.BlockSpec((1, half, cols), lambda j, core_ref: (j, 0, 0))),
        out_shape=jax.ShapeDtypeStruct(recv.shape, bf16),
        compiler_params=_cp("parallel"),
    )(core, grad, recv)


class _Cross:
    def __init__(self, parts, landed, send_sems, recv_sems):
        self.parts, self.landed, self.send_sems, self.recv_sems = parts, landed, send_sems, recv_sems
        _, _, self.c, self.me, self.others = _place()

    def _copy(self, k, j, src_chunk, dst_slot):
        ox, oy = self.others[j]
        return _remote(self.parts[k].at[src_chunk], self.landed[k].at[dst_slot], self.send_sems.at[3 * k + j],
                       self.recv_sems.at[3 * k + j], (ox, oy, self.c))

    def _each(self):
        return [(k, j, 2 * self.others[j][0] + self.others[j][1]) for k in range(len(self.parts)) for j in range(3)]

    def start(self):
        for k, j, chip in self._each():
            self._copy(k, j, chip, self.me).start()

    def finish(self):
        for k, j, chip in self._each():
            self._copy(k, j, chip, chip).wait_recv()
        for k, j, chip in self._each():
            self._copy(k, j, chip, self.me).wait_send()

    @staticmethod
    def scratch(n):
        return [pltpu.SemaphoreType.DMA((3 * n,)), pltpu.SemaphoreType.DMA((3 * n,))]


def _rs_cross(name, parts):
    nk = len(parts)

    def body(*refs):
        cross = _Cross(refs[:nk], refs[nk:2 * nk], *refs[2 * nk:])
        cross.start()
        cross.finish()

    return pl.pallas_call(
        body, name=f"rs_cross_{name}", in_specs=[ANY] * nk, out_specs=[ANY] * nk,
        out_shape=[jax.ShapeDtypeStruct(a.shape, bf16) for a in parts], scratch_shapes=_Cross.scratch(nk),
    )(*parts)


def _rs_sum_chips(name, place, landed, part):
    _, half, cols = landed.shape

    def body(place_ref, l_ref, p_ref, o_ref):
        me = place_ref[0]
        own = p_ref[0].astype(f32)
        acc = jnp.where(me == 0, own, l_ref[0].astype(f32))
        for j in range(1, N_CHIPS):
            acc = acc + jnp.where(me == j, own, l_ref[j].astype(f32))
        o_ref[...] = acc

    return pl.pallas_call(
        body, name=f"rs_sum_chips_{name}",
        grid_spec=pltpu.PrefetchScalarGridSpec(
            num_scalar_prefetch=1, grid=(1,),
            in_specs=[pl.BlockSpec((N_CHIPS, half, cols), lambda t, place_ref: (0, 0, 0)),
                      pl.BlockSpec((1, half, cols), lambda t, place_ref: (place_ref[0], 0, 0))],
            out_specs=pl.BlockSpec((half, cols), lambda t, place_ref: (place_ref[1], 0))),
        out_shape=jax.ShapeDtypeStruct((2 * half, cols), f32),
        compiler_params=_cp("arbitrary"),
    )(place, landed, part)


def _grad_pair(shards):
    nk = len(shards)

    def body(*refs):
        outs = refs[nk:2 * nk]
        send_sems, recv_sems = refs[2 * nk:]
        x, y, c, _, _ = _place()

        def half(k, core):
            h = outs[k].shape[0] // 2
            return outs[k].at[pl.ds(core * h, h), :]

        cps = [_remote(half(k, c), half(k, c), send_sems.at[k], recv_sems.at[k], (x, y, 1 - c)) for k in range(nk)]
        for cp in cps:
            cp.start()
        for k in range(nk):
            _remote(half(k, 1 - c), half(k, 1 - c), send_sems.at[k], recv_sems.at[k], (x, y, 1 - c)).wait_recv()
        for cp in cps:
            cp.wait_send()

    return pl.pallas_call(
        body, name="grad_pair", in_specs=[ANY] * nk, out_specs=[ANY] * nk,
        out_shape=[jax.ShapeDtypeStruct(a.shape, f32) for a in shards],
        input_output_aliases={k: k for k in range(nk)},
        scratch_shapes=[pltpu.SemaphoreType.DMA((nk,)), pltpu.SemaphoreType.DMA((nk,))],
    )(*shards)


def _adam_math(w, g, m, v):
    m = ADAM_B1 * m + (1.0 - ADAM_B1) * g
    v = ADAM_B2 * v + (1.0 - ADAM_B2) * (g * g)
    m_hat = m / (1.0 - ADAM_B1 ** ADAM_STEP)
    v_hat = v / (1.0 - ADAM_B2 ** ADAM_STEP)
    delta = -ADAM_LR * (m_hat / (jnp.sqrt(v_hat) + ADAM_EPS) + ADAM_WD * w)
    return delta, m, v


def _adam(k, w, g, m, v):
    rows, cols = w.shape
    tr = next(rows // d for d in (1, 2, 4, 8) if rows % (8 * d) == 0 and rows // d <= 256)

    def body(w_ref, g_ref, m_ref, v_ref, d_ref, nm_ref, nv_ref):
        d_ref[...], nm_ref[...], nv_ref[...] = _adam_math(w_ref[...], g_ref[...], m_ref[...], v_ref[...])

    spec = _row(0, (tr, cols))
    return pl.pallas_call(
        body, name=f"adam_{k}", grid=(rows // tr,), in_specs=[spec] * 4, out_specs=[spec] * 3,
        out_shape=[jax.ShapeDtypeStruct(w.shape, f32)] * 3,
        compiler_params=_cp("parallel"),
    )(w, g, m, v)


def _small_sync_adam(part, w, m, v):
    rows = part.shape[0]

    def body(p_ref, w_ref, m_ref, v_ref, g_ref, d_ref, nm_ref, nv_ref, buf, send_sems, recv_sems):
        x, y, c, _, others = _place()
        me, sibling = (x, y, c), (x, y, 1 - c)

        def slot(px, py, pc):
            return buf.at[4 * px + 2 * py + pc]

        def copy(k, block, to, src=None):
            return _remote(slot(*block) if src is None else src, slot(*block), send_sems.at[k], recv_sems.at[k], to)

        buf[4 * x + 2 * y + c] = p_ref[...]
        first = [copy(0, me, sibling, src=p_ref)]
        first += [copy(1 + j, me, (*chip, c), src=p_ref) for j, chip in enumerate(others)]
        for cp in first:
            cp.start()
        passed = [copy(4 + j, (*chip, c), sibling) for j, chip in enumerate(others)]
        for j, chip in enumerate(others):
            copy(1 + j, (*chip, c), me).wait_recv()
            passed[j].start()
        copy(0, sibling, me).wait_recv()
        for j, chip in enumerate(others):
            copy(4 + j, (*chip, 1 - c), me).wait_recv()
        for cp in first + passed:
            cp.wait_send()
        g = buf[0]
        for dev in range(1, N_DEV):
            g = g + buf[dev]
        g_ref[...] = g
        d_ref[...], nm_ref[...], nv_ref[...] = _adam_math(w_ref[...], g, m_ref[...], v_ref[...])

    shape = jax.ShapeDtypeStruct(part.shape, f32)
    return pl.pallas_call(
        body, name="small_sync_adam", in_specs=[VMEM] * 4, out_specs=[VMEM] * 4, out_shape=[shape] * 4,
        scratch_shapes=[pltpu.VMEM((N_DEV, rows, 128), f32), pltpu.SemaphoreType.DMA((7,)),
                        pltpu.SemaphoreType.DMA((7,))],
        compiler_params=pltpu.CompilerParams(vmem_limit_bytes=VMEM_LIMIT),
    )(part, w, m, v)


def _rows128(a):
    return a.reshape(-1, 128)


def _pad_rows(a, rows):
    return jnp.concatenate([a, jnp.zeros((rows - a.shape[0],) + a.shape[1:], a.dtype)], axis=0)


def _pack_small(me, g_pg, g_post, g_final, g_ffn, ln_g, ln_b, conv_b, pool_scale, pool_w, conv_w, g_mix):
    blk_ple = _pad_rows(jnp.concatenate([g_pg, g_post, g_final.reshape(1, D_MODEL)], axis=0), 8)
    blk_ffn = _pad_rows(g_ffn, 8)
    blk_mix = _pad_rows(jnp.concatenate([ln_g, ln_b, conv_b, pool_scale], axis=0), 8)
    cw = lax.dynamic_update_slice(jnp.zeros((N_CHIPS, HALO, 128), f32), _pad_rows(conv_w, HALO)[None], (me, 0, 0))
    blk_in = _pad_rows(g_mix, 8)
    return jnp.concatenate([_rows128(blk_ple), _rows128(blk_ffn), _rows128(blk_mix), _rows128(pool_w),
                            _rows128(cw), _rows128(blk_in)], axis=0)


def _unpack_small(me, packed):
    o = 0

    def take(rows):
        nonlocal o
        blk = packed[o:o + rows]
        o += rows
        return blk

    ple = take(64).reshape(8, D_MODEL)
    ffn = take(64).reshape(8, D_MODEL)
    mixb = take(32).reshape(8, C_CONV)
    pool_w = take(512).reshape(1, 4, POOL_GROUP, POOL_GROUP)
    cw = take(N_CHIPS * HALO).reshape(N_CHIPS, HALO, 128)
    inb = take(64).reshape(8, D_MODEL)
    conv_w = lax.dynamic_slice(cw, (me, 0, 0), (1, HALO, 128))[:, :CONV_K, :]
    return dict(g_ple_gate=ple[0:1], g_ple_post=ple[1:2], g_final=ple[2], loss=ple[3, 0], g_ffn=ffn[0:1],
                ln_g=mixb[0:1], ln_b=mixb[1:2], conv_b=mixb[2:3], pool_scale=mixb[3:4], pool_w=pool_w,
                conv_w=conv_w, g_mix=inb[0:1])


def kernel(x, p, g_mix, w_in, conv_w, conv_b, ln_g, ln_b, pool_w, pool_scale, w_out, g_ffn, w_gate_up, w_down, g_ple_gate, w_ple_gate, w_ple_up, g_ple_post, g_final, loss_target, m_g_mix, m_w_in, m_conv_w, m_conv_b, m_ln_g, m_ln_b, m_pool_w, m_pool_scale, m_w_out, m_g_ffn, m_w_gate_up, m_w_down, m_g_ple_gate, m_w_ple_gate, m_w_ple_up, m_g_ple_post, m_g_final, v_g_mix, v_w_in, v_conv_w, v_conv_b, v_ln_g, v_ln_b, v_pool_w, v_pool_scale, v_w_out, v_g_ffn, v_w_gate_up, v_w_down, v_g_ple_gate, v_w_ple_gate, v_w_ple_up, v_g_ple_post, v_g_final):
    seq = x.shape[1]
    me = 2 * lax.axis_index("x") + lax.axis_index("y")
    chip = me.astype(jnp.int32).reshape(1)
    core = lax.axis_index("c").astype(jnp.int32).reshape(1)
    place = jnp.concatenate([chip, core])
    xs, ps, ts = x.reshape(seq, D_MODEL), p.reshape(seq, D_PLE), loss_target.reshape(seq, D_MODEL)

    big_names = ["w_in", "w_gu", "w_out", "w_down", "w_pg", "w_pu"]
    big = [w_in[0], w_gate_up[0], w_out[0], w_down[0], w_ple_gate[0], w_ple_up[0]]
    big_m = [m_w_in[0], m_w_gate_up[0], m_w_out[0], m_w_down[0], m_w_ple_gate[0], m_w_ple_up[0]]
    big_v = [v_w_in[0], v_w_gate_up[0], v_w_out[0], v_w_down[0], v_w_ple_gate[0], v_w_ple_up[0]]
    b_in, b_gu, b_out, b_down, b_pg, b_pu = [_cast_into_slot(nm, chip, w, bf16) for nm, w in zip(big_names, big)]
    b_cw = _cast_into_slot("conv_w", chip, _pad_rows(conv_w[0], HALO), f32)
    xi, yi = lax.axis_index("x"), lax.axis_index("y")
    order = jnp.stack([me, 2 * (1 - xi) + yi, 2 * xi + 1 - yi, 2 * (1 - xi) + 1 - yi]).astype(jnp.int32)

    z, (w_in_g, cw_g, w_out_g) = _mix_in(xs, g_mix, order, [b_in, b_cw, b_out])
    conv_w_f = cw_g.transpose(1, 0, 2).reshape(HALO, C_CONV)
    w_out_f = w_out_g.reshape(D_MODEL, D_MODEL)
    (x1, mix, u1, pooled), (w_gu_g, w_down_g) = _conv_pool_out(z, xs, conv_w_f, conv_b, ln_g, ln_b, pool_w[0],
                                                                pool_scale, w_out_f, [b_gu, b_down])
    w_down_f = w_down_g.reshape(D_FF, D_MODEL)
    (x2, h2, gu), (w_pg_g, w_pu_g) = _ffn_fwd(x1, g_ffn, w_gu_g, w_down_f, [b_pg, b_pu])
    w_pg_f = w_pg_g.reshape(D_MODEL, D_MODEL)
    dx2, d_w_pg, d_w_pu, small_ple = _ple_loss(x2, ps, ts, g_ple_gate, g_ple_post, g_final.reshape(1, D_MODEL),
                                               w_pg_f, w_pu_g)
    def pair_reduce(group, names, grads):
        got = _rs_pair(group, grads)
        return [_rs_add_pair(nm, core, g, r) for nm, g, r in zip(names, grads, got)]

    d_w_down = _ffn_bwd_dw_down(gu, dx2)
    parts_a = pair_reduce("a", ["w_pg", "w_pu", "w_down"],
                          [d_w_pg.reshape(N_CHIPS, -1, D_MODEL), d_w_pu, d_w_down.reshape(N_CHIPS, -1, D_MODEL)])
    (dx1, dgu, small_ffn), landed_a = _ffn_bwd_dx(dx2, x1, gu, g_ffn, w_gu_g, w_down_f, parts_a)
    d_w_gu = _ffn_bwd_dw_gu(h2, dgu)
    du1, dpo, d_w_out, d_pool_w, small_mix = _mix_bwd_local(dx1, mix, u1, pooled, w_out_f, ln_g, ln_b, pool_w[0],
                                                             pool_scale)
    parts_b = pair_reduce("b", ["w_gu", "w_out"], [d_w_gu, d_w_out.reshape(N_CHIPS, -1, D_MODEL)])
    (grad_x, d_w_in, d_conv_w, small_in), landed_b = _in_bwd(du1, dpo, z, xs, dx1, conv_w_f, g_mix, w_in_g, parts_b)
    parts_c = pair_reduce("c", ["w_in"], [d_w_in])
    landed_c = _rs_cross("c", parts_c)
    parts = [parts_c[0], parts_b[0], parts_b[1], parts_a[2], parts_a[0], parts_a[1]]
    landed = [landed_c[0], landed_b[0], landed_b[1], landed_a[2], landed_a[0], landed_a[1]]
    halves = [_rs_sum_chips(nm, place, a, b) for nm, a, b in zip(big_names, landed, parts)]
    big_g = _grad_pair(halves)
    big_upd = [_adam(nm, w, g, m, v) for nm, w, g, m, v in zip(big_names, big, big_g, big_m, big_v)]

    cw_chunks = d_conv_w.reshape(HALO, N_CHIPS, 128).transpose(1, 0, 2)
    part = jnp.concatenate([_rows128(small_ple), _rows128(small_ffn), _rows128(small_mix), _rows128(d_pool_w),
                            _rows128(cw_chunks), _rows128(small_in)], axis=0)
    sw = _pack_small(me, g_ple_gate, g_ple_post, g_final, g_ffn, ln_g, ln_b, conv_b, pool_scale, pool_w, conv_w[0], g_mix)
    sm = _pack_small(me, m_g_ple_gate, m_g_ple_post, m_g_final, m_g_ffn, m_ln_g, m_ln_b, m_conv_b, m_pool_scale,
                     m_pool_w, m_conv_w[0], m_g_mix)
    sv = _pack_small(me, v_g_ple_gate, v_g_ple_post, v_g_final, v_g_ffn, v_ln_g, v_ln_b, v_conv_b, v_pool_scale,
                     v_pool_w, v_conv_w[0], v_g_mix)
    small = [_unpack_small(me, a) for a in _small_sync_adam(part, sw, sm, sv)]

    names = ["g_mix", "w_in", "conv_w", "conv_b", "ln_g", "ln_b", "pool_w", "pool_scale", "w_out", "g_ffn",
             "w_gate_up", "w_down", "g_ple_gate", "w_ple_gate", "w_ple_up", "g_ple_post", "g_final"]
    big_at = {"w_in": 0, "w_gate_up": 1, "w_out": 2, "w_down": 3, "w_ple_gate": 4, "w_ple_up": 5}
    out = [small[0]["loss"], grad_x.reshape(1, seq, D_MODEL)]
    for kind in range(4):
        for nm in names:
            if nm in big_at:
                k = big_at[nm]
                out.append((big_g[k] if kind == 0 else big_upd[k][kind - 1])[None])
            else:
                out.append(small[kind][nm])
    return tuple(out)
```

```python
import functools

import jax
import jax.numpy as jnp
from jax import lax
from jax.experimental import pallas as pl
from jax.experimental.pallas import tpu as pltpu

f32, bf16 = jnp.float32, jnp.bfloat16

EPS = 1e-6
D_MODEL = 1024
C_CONV = 512
C_POOL = 512
POOL_WINDOWS = (2, 4, 8, 16)
POOL_GROUP = 128
CONV_K = 31
D_FF = 2816
D_PLE = 256
N_CHIPS = 4
N_DEV = 8
W_IN_COLS = 2 * C_CONV + C_POOL
W_IN_CHUNK = W_IN_COLS // N_CHIPS
FF_CHUNK = 2 * D_FF // N_CHIPS
PLE_CHUNK = D_MODEL // N_CHIPS
HALO = 32
ROW_TILE = 512
CONV_ROWS = 64
CONV_COLS = (slice(0, 256), slice(256, 512))
SHIFT_PAD = 32
SHIFT_GROUPS = 16
FF_SUB = (0, 512, 1024, FF_CHUNK)
VMEM_LIMIT = 56 * 1024 * 1024

ADAM_LR = 0.001
ADAM_B1 = 0.9
ADAM_B2 = 0.999
ADAM_EPS = 1e-08
ADAM_WD = 0.01
ADAM_STEP = 10

MESH = pl.DeviceIdType.MESH
ANY = pl.BlockSpec(memory_space=pl.ANY)
VMEM = pl.BlockSpec(memory_space=pltpu.VMEM)


def _cp(*sem):
    return pltpu.CompilerParams(dimension_semantics=sem, vmem_limit_bytes=VMEM_LIMIT)


def _dot(a, b):
    return jnp.dot(a, b, preferred_element_type=f32)


def _dot_nt(a, b):
    return lax.dot_general(a, b, (((1,), (1,)), ((), ())), preferred_element_type=f32)


def _dot_tn(a, b):
    return lax.dot_general(a, b, (((0,), (0,)), ((), ())), preferred_element_type=f32)


def _sigmoid(v):
    return jax.nn.sigmoid(v)


def _rms_fwd(v, g):
    r = lax.rsqrt(jnp.mean(v * v, axis=-1, keepdims=True) + EPS)
    vh = v * r
    return vh * g, vh, r


def _rms_bwd(dy, vh, r, g):
    dvh = dy * g
    dv = r * (dvh - vh * jnp.mean(dvh * vh, axis=-1, keepdims=True))
    return dv, jnp.sum(dy * vh, axis=0, keepdims=True)


def _silu_grad(v, s):
    return s * (1.0 + v * (1.0 - s))


def _row(i, n):
    return pl.BlockSpec((n[0], n[1]), lambda *a: (a[i], 0))


def _full(shape):
    nd = len(shape)
    return pl.BlockSpec(shape, lambda *a: (0,) * nd)


def _place():
    x, y, c = lax.axis_index("x"), lax.axis_index("y"), lax.axis_index("c")
    others = [(1 - x, y), (x, 1 - y), (1 - x, 1 - y)]
    return x, y, c, 2 * x + y, others


def _remote(src, dst, send_sem, recv_sem, dev):
    return pltpu.make_async_remote_copy(src_ref=src, dst_ref=dst, send_sem=send_sem, recv_sem=recv_sem,
                                        device_id=dev, device_id_type=MESH)


def _cast_into_slot(name, me, w, dtype):
    rows, cols = w.shape
    tr = next(rows // d for d in (1, 2, 4, 8) if rows % (16 * d) == 0 and rows // d <= 512)

    def body(me_ref, w_ref, o_ref):
        o_ref[0] = w_ref[...].astype(dtype)

    return pl.pallas_call(
        body, name=f"cast_{name}",
        grid_spec=pltpu.PrefetchScalarGridSpec(
            num_scalar_prefetch=1, grid=(rows // tr,),
            in_specs=[pl.BlockSpec((tr, cols), lambda r, me_ref: (r, 0))],
            out_specs=pl.BlockSpec((1, tr, cols), lambda r, me_ref: (me_ref[0], r, 0))),
        out_shape=jax.ShapeDtypeStruct((N_CHIPS, rows, cols), dtype),
        compiler_params=_cp("parallel"),
    )(me, w)


class _Gather:
    def __init__(self, bufs, send_sems, recv_sems):
        self.bufs, self.send_sems, self.recv_sems = bufs, send_sems, recv_sems
        self.x, self.y, self.c, self.me, self.others = _place()
        self.halves = [b.shape[1] // 2 for b in bufs]

    def _piece(self, k, chip, half):
        return self.bufs[k].at[chip, pl.ds(half * self.halves[k], self.halves[k]), :]

    def _ici(self, k, j, chip):
        ox, oy = self.others[j]
        piece = self._piece(k, chip, self.c)
        return _remote(piece, piece, self.send_sems.at[6 * k + j], self.recv_sems.at[6 * k + j], (ox, oy, self.c))

    def _pair(self, k, j, half):
        ox, oy = self.others[j]
        piece = self._piece(k, 2 * ox + oy, half)
        return _remote(piece, piece, self.send_sems.at[6 * k + 3 + j], self.recv_sems.at[6 * k + 3 + j],
                       (self.x, self.y, 1 - self.c))

    def _each(self, ks=None):
        return [(k, j) for k in (range(len(self.bufs)) if ks is None else ks) for j in range(3)]

    def chip(self, j):
        ox, oy = self.others[j]
        return 2 * ox + oy

    def start(self):
        for k, j in self._each():
            self._ici(k, j, self.me).start()

    def forward(self, pairs=None):
        for k, j in self._each() if pairs is None else pairs:
            self._ici(k, j, self.chip(j)).wait_recv()
            self._pair(k, j, self.c).start()

    def landed(self, pairs):
        for k, j in pairs:
            self._pair(k, j, 1 - self.c).wait_recv()

    def finish(self, ks=None):
        self.landed(self._each(ks))
        for k, j in self._each():
            self._ici(k, j, self.me).wait_send()
            self._pair(k, j, self.c).wait_send()

    @staticmethod
    def scratch(n):
        return [pltpu.SemaphoreType.DMA((6 * n,)), pltpu.SemaphoreType.DMA((6 * n,))]


def _carried(bufs):
    n = len(bufs)
    return dict(in_specs=[ANY] * n, out_specs=[ANY] * n,
                out_shape=[jax.ShapeDtypeStruct(b.shape, b.dtype) for b in bufs], scratch=_Gather.scratch(n))


def _mix_in(x, g_mix, order, carry):
    s = x.shape[0]
    tm = min(2 * ROW_TILE, s)
    n = s // tm
    nc = len(carry)
    cs = _carried(carry)

    def body(order_ref, x_ref, g_ref, *refs):
        z_ref = refs[nc]
        bufs = refs[nc + 1:2 * nc + 1]
        h_ref, w_ref, w_sem = refs[2 * nc + 1:2 * nc + 4]
        gather = _Gather(bufs, *refs[2 * nc + 4:])
        q, i = pl.program_id(0), pl.program_id(1)
        first = i == 0
        pl.when(jnp.logical_and(q == 0, first))(gather.start)
        for j in range(3):

            @pl.when(jnp.logical_and(q == j + 1, first))
            def _():
                gather.forward([(0, j)])
                gather.landed([(0, j)])

        @pl.when(first)
        def _():
            load = pltpu.make_async_copy(bufs[0].at[order_ref[q]], w_ref, w_sem)
            load.start()
            load.wait()

        @pl.when(q == 0)
        def _():
            h, _, _ = _rms_fwd(x_ref[...], g_ref[...])
            h_ref[i] = h.astype(bf16)

        z_ref[...] = _dot(h_ref[i], w_ref[...])

        @pl.when(jnp.logical_and(q == N_CHIPS - 1, i == n - 1))
        def _():
            rest = gather._each(range(1, nc))
            gather.forward(rest)
            gather.finish(range(1, nc))

    res = pl.pallas_call(
        body, name="mix_in",
        grid_spec=pltpu.PrefetchScalarGridSpec(
            num_scalar_prefetch=1, grid=(N_CHIPS, n),
            in_specs=[pl.BlockSpec((tm, D_MODEL), lambda q, i, order_ref: (jnp.where(q == 0, i, 0), 0)),
                      pl.BlockSpec((1, D_MODEL), lambda q, i, order_ref: (0, 0))] + cs["in_specs"],
            out_specs=[pl.BlockSpec((tm, W_IN_CHUNK), lambda q, i, order_ref: (i, order_ref[q]))] + cs["out_specs"],
            scratch_shapes=[pltpu.VMEM((n, tm, D_MODEL), bf16), pltpu.VMEM((D_MODEL, W_IN_CHUNK), bf16),
                            pltpu.SemaphoreType.DMA(())] + cs["scratch"]),
        out_shape=[jax.ShapeDtypeStruct((s, W_IN_COLS), f32)] + cs["out_shape"],
        input_output_aliases={3 + k: 1 + k for k in range(nc)},
        compiler_params=_cp("arbitrary", "arbitrary"),
    )(order, x, g_mix, *carry)
    return res[0], res[1:]


def _tap_offsets(lo, hi):
    groups = [[o for o in range(lo, hi + 1) if o % 8 == s] for s in range(8)]
    return [g for g in groups if g]


def _tap_sum(buf, w_ref, row0, cols, tap_of):
    acc = jnp.zeros((CONV_ROWS, cols.stop - cols.start), f32)
    for offs in _tap_offsets(0, CONV_K - 1):
        slab = buf[pl.ds(row0 + offs[0], offs[-1] - offs[0] + CONV_ROWS), cols]
        for o in offs:
            acc = acc + w_ref[pl.ds(tap_of(o), 1), cols] * slab[o - offs[0]:o - offs[0] + CONV_ROWS]
    return acc


def _pool_counts(tm, w, first_row):
    t1 = (lax.broadcasted_iota(jnp.int32, (tm, 1), 0) + first_row + 1).astype(f32)
    return jnp.minimum(t1, float(w))


def _conv_pool_out(z, x, conv_w, conv_b, ln_g, ln_b, pool_w, pool_scale, w_out, carry):
    s = x.shape[0]
    tm = min(ROW_TILE, s)
    n = s // tm
    hb = tm // HALO
    nc = len(carry)
    cs = _carried(carry)

    def body(z_ref, zp_ref, x_ref, cw_ref, cb_ref, lg_ref, lb_ref, pw_ref, ps_ref, wo_ref, *refs):
        x1_ref, mix_ref, u1_ref, pooled_ref = refs[nc:nc + 4]
        ubuf, vbuf = refs[2 * nc + 4:2 * nc + 6]
        gather = _Gather(refs[nc + 4:2 * nc + 4], *refs[2 * nc + 6:])
        i = pl.program_id(0)
        pl.when(i == 0)(gather.start)
        pl.when(i == max(n - 2, 0))(gather.forward)
        keep = (i > 0).astype(f32)
        zp = zp_ref[...] * keep
        ubuf[0:HALO, :] = zp[:, :C_CONV] * _sigmoid(zp[:, C_CONV:2 * C_CONV])
        vbuf[0:HALO, :] = zp[:, 2 * C_CONV:]
        ubuf[HALO:, :] = z_ref[:, :C_CONV] * _sigmoid(z_ref[:, C_CONV:2 * C_CONV])
        vbuf[HALO:, :] = z_ref[:, 2 * C_CONV:]
        off = HALO - (CONV_K - 1)
        for r0 in range(0, tm, CONV_ROWS):
            for cols in CONV_COLS:
                u1_ref[r0:r0 + CONV_ROWS, cols] = cb_ref[:, cols] + _tap_sum(ubuf, cw_ref, r0 + off, cols, lambda o: o)
        u1 = u1_ref[...]
        mu = jnp.mean(u1, axis=-1, keepdims=True)
        uc = u1 - mu
        rstd = lax.rsqrt(jnp.mean(uc * uc, axis=-1, keepdims=True) + EPS)
        u2 = uc * rstd * lg_ref[...] + lb_ref[...]
        mix_ref[:, :C_CONV] = (u2 * _sigmoid(u2)).astype(bf16)
        for g, w in enumerate(POOL_WINDOWS):
            cols = slice(g * POOL_GROUP, (g + 1) * POOL_GROUP)
            acc = vbuf[pl.ds(HALO, tm), cols]
            vg = acc
            for d in range(1, w):
                acc = acc + vbuf[pl.ds(HALO - d, tm), cols]
            pooled = (acc / _pool_counts(tm, w, i * tm) - vg).astype(bf16)
            pooled_ref[:, cols] = pooled
            mixed = _dot(pooled, pw_ref[g].astype(bf16))
            mix_ref[:, C_CONV + g * POOL_GROUP:C_CONV + (g + 1) * POOL_GROUP] = (mixed * ps_ref[:, cols]).astype(bf16)
        x1_ref[...] = x_ref[...] + _dot(mix_ref[...], wo_ref[...])
        pl.when(i == n - 1)(gather.finish)

    res = pl.pallas_call(
        body, name="conv_pool_out", grid=(n,),
        in_specs=[_row(0, (tm, W_IN_COLS)),
                  pl.BlockSpec((HALO, W_IN_COLS), lambda i: (jnp.maximum(i * hb - 1, 0), 0)),
                  _row(0, (tm, D_MODEL)), _full((HALO, C_CONV)), _full((1, C_CONV)), _full((1, C_CONV)),
                  _full((1, C_CONV)), _full((4, POOL_GROUP, POOL_GROUP)), _full((1, C_POOL)),
                  _full((D_MODEL, D_MODEL))] + cs["in_specs"],
        out_specs=[_row(0, (tm, D_MODEL)), _row(0, (tm, D_MODEL)), _row(0, (tm, C_CONV)), _row(0, (tm, C_POOL))]
        + cs["out_specs"],
        out_shape=[jax.ShapeDtypeStruct((s, D_MODEL), f32), jax.ShapeDtypeStruct((s, D_MODEL), bf16),
                   jax.ShapeDtypeStruct((s, C_CONV), f32), jax.ShapeDtypeStruct((s, C_POOL), bf16)] + cs["out_shape"],
        input_output_aliases={10 + k: 4 + k for k in range(nc)},
        scratch_shapes=[pltpu.VMEM((HALO + tm, C_CONV), f32), pltpu.VMEM((HALO + tm, C_POOL), f32)] + cs["scratch"],
        compiler_params=_cp("arbitrary"),
    )(z, z, x, conv_w, conv_b, ln_g, ln_b, pool_w, pool_scale, w_out, *carry)
    return res[:4], res[4:]


def _ffn_fwd(x1, g_ffn, w_gu_g, w_down, carry):
    s = x1.shape[0]
    tm = min(ROW_TILE, s)
    n = s // tm
    nc = len(carry)
    cs = _carried(carry)

    def body(x1_ref, g_ref, wg_ref, wu_ref, wd_ref, *refs):
        x2_ref, h2_ref, gu_ref, f_ref = refs[nc:nc + 4]
        acc_ref = refs[2 * nc + 4]
        gather = _Gather(refs[nc + 4:2 * nc + 4], *refs[2 * nc + 5:])
        i, c = pl.program_id(0), pl.program_id(1)
        pl.when(jnp.logical_and(i == 0, c == 0))(gather.start)
        pl.when(jnp.logical_and(i == n - 1, c == 0))(gather.forward)

        @pl.when(c == 0)
        def _():
            h, _, _ = _rms_fwd(x1_ref[...], g_ref[...])
            h2_ref[...] = h.astype(bf16)
            acc_ref[...] = jnp.zeros_like(acc_ref)

        h = h2_ref[...]
        for lo, hi in zip(FF_SUB[:-1], FF_SUB[1:]):
            gate = _dot(h, wg_ref[0, :, lo:hi])
            up = _dot(h, wu_ref[0, :, lo:hi])
            gu_ref[0, :, lo:hi] = gate.astype(bf16)
            gu_ref[1, :, lo:hi] = up.astype(bf16)
            f = (gate * _sigmoid(gate) * up).astype(bf16)
            f_ref[:, lo:hi] = f
            acc_ref[...] += _dot(f, wd_ref[lo:hi, :])

        @pl.when(c == 1)
        def _():
            x2_ref[...] = x1_ref[...] + acc_ref[...]

        pl.when(jnp.logical_and(i == n - 1, c == 1))(gather.finish)

    res = pl.pallas_call(
        body, name="ffn_fwd", grid=(n, 2),
        in_specs=[_row(0, (tm, D_MODEL)), _full((1, D_MODEL)),
                  pl.BlockSpec((1, D_MODEL, FF_CHUNK), lambda i, c: (c, 0, 0)),
                  pl.BlockSpec((1, D_MODEL, FF_CHUNK), lambda i, c: (2 + c, 0, 0)),
                  pl.BlockSpec((FF_CHUNK, D_MODEL), lambda i, c: (c, 0))] + cs["in_specs"],
        out_specs=[_row(0, (tm, D_MODEL)), _row(0, (tm, D_MODEL)),
                   pl.BlockSpec((2, tm, FF_CHUNK), lambda i, c: (0, i, c)),
                   pl.BlockSpec((tm, FF_CHUNK), lambda i, c: (i, c))] + cs["out_specs"],
        out_shape=[jax.ShapeDtypeStruct((s, D_MODEL), f32), jax.ShapeDtypeStruct((s, D_MODEL), bf16),
                   jax.ShapeDtypeStruct((2, s, D_FF), bf16), jax.ShapeDtypeStruct((s, D_FF), bf16)] + cs["out_shape"],
        input_output_aliases={5 + k: 4 + k for k in range(nc)},
        scratch_shapes=[pltpu.VMEM((tm, D_MODEL), f32)] + cs["scratch"],
        compiler_params=_cp("arbitrary", "arbitrary"),
    )(x1, g_ffn, w_gu_g, w_gu_g, w_down, *carry)
    return res[:4], res[4:]


def _ple_loss(x2, p, target, g_pg, g_post, g_final, w_pg, w_pu_g):
    s = x2.shape[0]
    tm = min(ROW_TILE, s)
    n = s // tm

    def body(x2_ref, p_ref, t_ref, gpg_ref, gpo_ref, gf_ref, wpg_ref, wpu_ref,
             dx2_ref, dwpg_ref, dwpu_ref, small_ref, apg_ref, apu_ref):
        i = pl.program_id(0)

        @pl.when(i == 0)
        def _():
            apg_ref[...] = jnp.zeros_like(apg_ref)
            apu_ref[...] = jnp.zeros_like(apu_ref)
            small_ref[...] = jnp.zeros_like(small_ref)

        x2 = x2_ref[...]
        h3, x2h, r2 = _rms_fwd(x2, gpg_ref[...])
        h3b = h3.astype(bf16)
        gate = _sigmoid(_dot(h3b, wpg_ref[...]))
        pb = p_ref[...].astype(bf16)
        pe = jnp.concatenate([_dot(pb, wpu_ref[j]) for j in range(N_CHIPS)], axis=-1)
        e, peh, rp = _rms_fwd(pe, gpo_ref[...])
        x3 = x2 + gate * e
        y, x3h, r3 = _rms_fwd(x3, gf_ref[...])
        d = y - t_ref[...]
        loss = 0.5 * jnp.sum(jnp.sum(d * d, axis=-1, keepdims=True) * (1.0 / D_MODEL), axis=0, keepdims=True)
        dx3, dgf = _rms_bwd(d * (1.0 / D_MODEL), x3h, r3, gf_ref[...])
        dpe, dgpo = _rms_bwd(dx3 * gate, peh, rp, gpo_ref[...])
        dgl = (dx3 * e * gate * (1.0 - gate)).astype(bf16)
        apg_ref[...] += _dot_tn(h3b, dgl)
        apu_ref[...] += _dot_tn(pb, dpe.astype(bf16))
        dh3 = _dot_nt(dgl, wpg_ref[...])
        dx2b, dgpg = _rms_bwd(dh3, x2h, r2, gpg_ref[...])
        dx2_ref[...] = dx3 + dx2b
        small_ref[0:1, :] += dgpg
        small_ref[1:2, :] += dgpo
        small_ref[2:3, :] += dgf
        small_ref[3:4, :] += jnp.broadcast_to(loss, (1, D_MODEL))

        @pl.when(i == n - 1)
        def _():
            dwpg_ref[...] = apg_ref[...].astype(bf16)
            for j in range(N_CHIPS):
                dwpu_ref[j] = apu_ref[:, j * PLE_CHUNK:(j + 1) * PLE_CHUNK].astype(bf16)

    return pl.pallas_call(
        body, name="ple_loss", grid=(n,),
        in_specs=[_row(0, (tm, D_MODEL)), _row(0, (tm, D_PLE)), _row(0, (tm, D_MODEL)),
                  _full((1, D_MODEL)), _full((1, D_MODEL)), _full((1, D_MODEL)),
                  _full((D_MODEL, D_MODEL)), _full((N_CHIPS, D_PLE, PLE_CHUNK))],
        out_specs=[_row(0, (tm, D_MODEL)), _full((D_MODEL, D_MODEL)), _full((N_CHIPS, D_PLE, PLE_CHUNK)),
                   _full((8, D_MODEL))],
        out_shape=[jax.ShapeDtypeStruct((s, D_MODEL), f32), jax.ShapeDtypeStruct((D_MODEL, D_MODEL), bf16),
                   jax.ShapeDtypeStruct((N_CHIPS, D_PLE, PLE_CHUNK), bf16), jax.ShapeDtypeStruct((8, D_MODEL), f32)],
        scratch_shapes=[pltpu.VMEM((D_MODEL, D_MODEL), f32), pltpu.VMEM((D_PLE, D_MODEL), f32)],
        compiler_params=_cp("arbitrary"),
    )(x2, p, target, g_pg, g_post, g_final, w_pg, w_pu_g)


def _crossed(parts):
    n = len(parts)
    return dict(in_specs=[ANY] * n, out_specs=[ANY] * n,
                out_shape=[jax.ShapeDtypeStruct(a.shape, a.dtype) for a in parts], scratch=_Cross.scratch(n))


def _ffn_bwd_dx(dx2, x1, gu, g_ffn, w_gu_g, w_down, parts):
    s = x1.shape[0]
    tm = min(ROW_TILE, s)
    n = s // tm
    nc = len(parts)
    cs = _crossed(parts)

    def body(dx2_ref, x1_ref, gu_ref, g_ref, wg_ref, wu_ref, wd_ref, *refs):
        dx1_ref, dgu_ref, small_ref = refs[nc:nc + 3]
        acc_ref = refs[2 * nc + 3]
        cross = _Cross(refs[:nc], refs[nc + 3:2 * nc + 3], *refs[2 * nc + 4:])
        i, c = pl.program_id(0), pl.program_id(1)
        pl.when(jnp.logical_and(i == 0, c == 0))(cross.start)

        @pl.when(jnp.logical_and(i == 0, c == 0))
        def _():
            small_ref[...] = jnp.zeros_like(small_ref)

        @pl.when(c == 0)
        def _():
            acc_ref[...] = jnp.zeros_like(acc_ref)

        dyb = dx2_ref[...].astype(bf16)
        for lo, hi in zip(FF_SUB[:-1], FF_SUB[1:]):
            df = _dot_nt(dyb, wd_ref[lo:hi, :])
            gate = gu_ref[0, :, lo:hi].astype(f32)
            up = gu_ref[1, :, lo:hi].astype(f32)
            sg = _sigmoid(gate)
            dgate = (df * up * _silu_grad(gate, sg)).astype(bf16)
            dup = (df * gate * sg).astype(bf16)
            dgu_ref[0, :, lo:hi] = dgate
            dgu_ref[1, :, lo:hi] = dup
            acc_ref[...] += _dot_nt(dgate, wg_ref[0, :, lo:hi]) + _dot_nt(dup, wu_ref[0, :, lo:hi])

        @pl.when(c == 1)
        def _():
            _, x1h, r1 = _rms_fwd(x1_ref[...], g_ref[...])
            dx1b, dg = _rms_bwd(acc_ref[...], x1h, r1, g_ref[...])
            dx1_ref[...] = dx2_ref[...] + dx1b
            small_ref[0:1, :] += dg

        pl.when(jnp.logical_and(i == n - 1, c == 1))(cross.finish)

    res = pl.pallas_call(
        body, name="ffn_bwd_dx", grid=(n, 2),
        in_specs=[_row(0, (tm, D_MODEL)), _row(0, (tm, D_MODEL)),
                  pl.BlockSpec((2, tm, FF_CHUNK), lambda i, c: (0, i, c)), _full((1, D_MODEL)),
                  pl.BlockSpec((1, D_MODEL, FF_CHUNK), lambda i, c: (c, 0, 0)),
                  pl.BlockSpec((1, D_MODEL, FF_CHUNK), lambda i, c: (2 + c, 0, 0)),
                  pl.BlockSpec((FF_CHUNK, D_MODEL), lambda i, c: (c, 0))] + cs["in_specs"],
        out_specs=[_row(0, (tm, D_MODEL)), pl.BlockSpec((2, tm, FF_CHUNK), lambda i, c: (0, i, c)),
                   _full((8, D_MODEL))] + cs["out_specs"],
        out_shape=[jax.ShapeDtypeStruct((s, D_MODEL), f32), jax.ShapeDtypeStruct((2, s, D_FF), bf16),
                   jax.ShapeDtypeStruct((8, D_MODEL), f32)] + cs["out_shape"],
        scratch_shapes=[pltpu.VMEM((tm, D_MODEL), f32)] + cs["scratch"],
        compiler_params=_cp("arbitrary", "arbitrary"),
    )(dx2, x1, gu, g_ffn, w_gu_g, w_gu_g, w_down, *parts)
    return res[:3], res[3:]


def _ffn_bwd_dw_gu(h2, dgu):
    s = h2.shape[0]
    ts = min(ROW_TILE, s)
    n = s // ts

    def body(h_ref, d_ref, o_ref, acc_ref):
        t = pl.program_id(1)

        @pl.when(t == 0)
        def _():
            acc_ref[...] = jnp.zeros_like(acc_ref)

        acc_ref[...] += _dot_tn(h_ref[...], d_ref[0])

        @pl.when(t == n - 1)
        def _():
            o_ref[0] = acc_ref[...].astype(bf16)

    return pl.pallas_call(
        body, name="ffn_bwd_dw_gu", grid=(N_CHIPS, n),
        in_specs=[pl.BlockSpec((ts, D_MODEL), lambda j, t: (t, 0)),
                  pl.BlockSpec((1, ts, FF_CHUNK), lambda j, t: (j // 2, t, j % 2))],
        out_specs=pl.BlockSpec((1, D_MODEL, FF_CHUNK), lambda j, t: (j, 0, 0)),
        out_shape=jax.ShapeDtypeStruct((N_CHIPS, D_MODEL, FF_CHUNK), bf16),
        scratch_shapes=[pltpu.VMEM((D_MODEL, FF_CHUNK), f32)],
        compiler_params=_cp("parallel", "arbitrary"),
    )(h2, dgu)


def _ffn_bwd_dw_down(f, dx2):
    s = dx2.shape[0]
    ts = min(ROW_TILE, s)
    n = s // ts

    def body(f_ref, d_ref, o_ref, acc_ref):
        t = pl.program_id(1)

        @pl.when(t == 0)
        def _():
            acc_ref[...] = jnp.zeros_like(acc_ref)

        acc_ref[...] += _dot_tn(f_ref[...], d_ref[...].astype(bf16))

        @pl.when(t == n - 1)
        def _():
            o_ref[...] = acc_ref[...].astype(bf16)

    return pl.pallas_call(
        body, name="ffn_bwd_dw_down", grid=(2, n),
        in_specs=[pl.BlockSpec((ts, FF_CHUNK), lambda c, t: (t, c)),
                  pl.BlockSpec((ts, D_MODEL), lambda c, t: (t, 0))],
        out_specs=pl.BlockSpec((FF_CHUNK, D_MODEL), lambda c, t: (c, 0)),
        out_shape=jax.ShapeDtypeStruct((D_FF, D_MODEL), bf16),
        scratch_shapes=[pltpu.VMEM((FF_CHUNK, D_MODEL), f32)],
        compiler_params=_cp("parallel", "arbitrary"),
    )(f, dx2)


def _mix_bwd_local(dx1, mix, u1, pooled, w_out, ln_g, ln_b, pool_w, pool_scale):
    s = dx1.shape[0]
    tm = min(ROW_TILE, s)
    n = s // tm

    def body(dx1_ref, mix_ref, u1_ref, po_ref, wo_ref, lg_ref, lb_ref, pw_ref, ps_ref,
             du1_ref, dpo_ref, dwo_ref, dpw_ref, small_ref, awo_ref):
        i = pl.program_id(0)

        @pl.when(i == 0)
        def _():
            awo_ref[...] = jnp.zeros_like(awo_ref)
            dpw_ref[...] = jnp.zeros_like(dpw_ref)
            small_ref[...] = jnp.zeros_like(small_ref)

        dyb = dx1_ref[...].astype(bf16)
        dmix = _dot_nt(dyb, wo_ref[...])
        awo_ref[...] += _dot_tn(mix_ref[...], dyb)
        u1 = u1_ref[...]
        mu = jnp.mean(u1, axis=-1, keepdims=True)
        uc = u1 - mu
        rstd = lax.rsqrt(jnp.mean(uc * uc, axis=-1, keepdims=True) + EPS)
        uh = uc * rstd
        u2 = uh * lg_ref[...] + lb_ref[...]
        du2 = dmix[:, :C_CONV] * _silu_grad(u2, _sigmoid(u2))
        duh = du2 * lg_ref[...]
        du1 = rstd * (duh - jnp.mean(duh, axis=-1, keepdims=True) - uh * jnp.mean(duh * uh, axis=-1, keepdims=True))
        du1_ref[...] = du1
        small_ref[0:1, :] += jnp.sum(du2 * uh, axis=0, keepdims=True)
        small_ref[1:2, :] += jnp.sum(du2, axis=0, keepdims=True)
        small_ref[2:3, :] += jnp.sum(du1, axis=0, keepdims=True)
        for g in range(len(POOL_WINDOWS)):
            cols = slice(g * POOL_GROUP, (g + 1) * POOL_GROUP)
            dq = dmix[:, C_CONV + g * POOL_GROUP:C_CONV + (g + 1) * POOL_GROUP]
            pwb = pw_ref[g].astype(bf16)
            pg = po_ref[:, cols]
            mixed = _dot(pg, pwb)
            small_ref[3:4, cols] += jnp.sum(dq * mixed, axis=0, keepdims=True)
            dmixed = (dq * ps_ref[:, cols]).astype(bf16)
            dpw_ref[g] += _dot_tn(pg, dmixed)
            dpo_ref[:, cols] = _dot_nt(dmixed, pwb)

        @pl.when(i == n - 1)
        def _():
            dwo_ref[...] = awo_ref[...].astype(bf16)

    return pl.pallas_call(
        body, name="mix_bwd_local", grid=(n,),
        in_specs=[_row(0, (tm, D_MODEL)), _row(0, (tm, D_MODEL)), _row(0, (tm, C_CONV)), _row(0, (tm, C_POOL)),
                  _full((D_MODEL, D_MODEL)), _full((1, C_CONV)), _full((1, C_CONV)),
                  _full((4, POOL_GROUP, POOL_GROUP)), _full((1, C_POOL))],
        out_specs=[_row(0, (tm, C_CONV)), _row(0, (tm, C_POOL)), _full((D_MODEL, D_MODEL)),
                   _full((4, POOL_GROUP, POOL_GROUP)), _full((8, C_CONV))],
        out_shape=[jax.ShapeDtypeStruct((s, C_CONV), f32), jax.ShapeDtypeStruct((s, C_POOL), f32),
                   jax.ShapeDtypeStruct((D_MODEL, D_MODEL), bf16),
                   jax.ShapeDtypeStruct((4, POOL_GROUP, POOL_GROUP), f32), jax.ShapeDtypeStruct((8, C_CONV), f32)],
        scratch_shapes=[pltpu.VMEM((D_MODEL, D_MODEL), f32)],
        compiler_params=_cp("arbitrary"),
    )(dx1, mix, u1, pooled, w_out, ln_g, ln_b, pool_w, pool_scale)


def _in_bwd(du1, dpo, z, x, dx1, conv_w, g_mix, w_in_g, parts):
    s = x.shape[0]
    tm = min(ROW_TILE, s)
    n = s // tm
    hb = tm // HALO
    last = s // HALO - 1
    nv = tm // 8
    assert nv >= SHIFT_PAD and nv % SHIFT_GROUPS == 0
    nc = len(parts)
    cs = _crossed(parts)

    def body(du_ref, dun_ref, dp_ref, dpn_ref, z_ref, zp_ref, x_ref, dx1_ref, cw_ref, g_ref, w_ref, *refs):
        gx_ref, dw_ref, dcw_ref, small_ref = refs[nc:nc + 4]
        eu, ed, ep, ss, u0_ref, dz_ref, acc_ref, dcw_acc = refs[2 * nc + 4:2 * nc + 12]
        cross = _Cross(refs[:nc], refs[nc + 4:2 * nc + 4], *refs[2 * nc + 12:])
        i = pl.program_id(0)
        pl.when(i == 0)(cross.start)

        @pl.when(i == 0)
        def _():
            acc_ref[...] = jnp.zeros_like(acc_ref)
            dcw_acc[...] = jnp.zeros_like(dcw_acc)
            small_ref[...] = jnp.zeros_like(small_ref)

        keep_prev = (i > 0).astype(f32)
        keep_next = (i < n - 1).astype(f32)
        zp = zp_ref[...] * keep_prev
        u0_prev = zp[:, :C_CONV] * _sigmoid(zp[:, C_CONV:2 * C_CONV])
        u0_ref[...] = z_ref[:, :C_CONV] * _sigmoid(z_ref[:, C_CONV:2 * C_CONV])
        du_next = dun_ref[...] * keep_next
        for c, w_pool in enumerate(POOL_WINDOWS):
            lanes = slice(c * 128, (c + 1) * 128)
            eu[c, pl.ds(0, SHIFT_PAD, stride=8), :] = u0_prev[:, lanes]
            ed[c, pl.ds(nv * 8 + 7, SHIFT_PAD, stride=8), :] = du_next[:, lanes]
            ep[c, pl.ds(nv * 8 + 7, SHIFT_PAD, stride=8), :] = (
                dpn_ref[:, lanes] * keep_next / _pool_counts(HALO, w_pool, (i + 1) * tm))
            for j in range(8):
                rows = slice(j * nv, (j + 1) * nv)
                eu[c, pl.ds(SHIFT_PAD * 8 + j, nv, stride=8), :] = u0_ref[rows, lanes]
                ed[c, pl.ds(j, nv, stride=8), :] = du_ref[rows, lanes]
                ep[c, pl.ds(j, nv, stride=8), :] = dp_ref[rows, lanes] / _pool_counts(nv, w_pool, i * tm + j * nv)
                if j >= 1:
                    eu[c, pl.ds(j, SHIFT_PAD, stride=8), :] = u0_ref[j * nv - SHIFT_PAD:j * nv, lanes]
                if j <= 6:
                    edge = slice((j + 1) * nv, (j + 1) * nv + SHIFT_PAD)
                    ed[c, pl.ds(nv * 8 + j, SHIFT_PAD, stride=8), :] = du_ref[edge, lanes]
                    ep[c, pl.ds(nv * 8 + j, SHIFT_PAD, stride=8), :] = (
                        dp_ref[edge, lanes] / _pool_counts(SHIFT_PAD, w_pool, i * tm + (j + 1) * nv))
        for c, w_pool in enumerate(POOL_WINDOWS):
            lanes = slice(c * 128, (c + 1) * 128)
            b_lanes = slice(C_CONV + c * 128, C_CONV + (c + 1) * 128)
            v_lanes = slice(2 * C_CONV + c * 128, 2 * C_CONV + (c + 1) * 128)
            for v0 in range(0, nv, SHIFT_GROUPS):
                span = SHIFT_GROUPS * 8
                acc = jnp.zeros((span, 128), f32)
                for k in range(CONV_K):
                    acc = acc + cw_ref[pl.ds(k, 1), lanes] * ed[c, pl.ds((v0 + CONV_K - 1 - k) * 8, span), :]
                ss[0, v0 * 8:v0 * 8 + span, :] = acc
                acc = ep[c, pl.ds(v0 * 8, span), :]
                for d in range(1, w_pool):
                    acc = acc + ep[c, pl.ds((v0 + d) * 8, span), :]
                ss[1, v0 * 8:v0 * 8 + span, :] = acc
                d1 = ed[c, pl.ds(v0 * 8, span), :]
                for k in range(CONV_K):
                    prod = d1 * eu[c, pl.ds((SHIFT_PAD - (CONV_K - 1) + v0 + k) * 8, span), :]
                    fold = prod[0:8]
                    for r in range(8, span, 8):
                        fold = fold + prod[r:r + 8]
                    dcw_acc[k, :, lanes] += fold
            for j in range(8):
                rows = slice(j * nv, (j + 1) * nv)
                du0 = ss[0, pl.ds(j, nv, stride=8), :]
                av, sv = z_ref[rows, lanes], _sigmoid(z_ref[rows, b_lanes])
                dz_ref[rows, lanes] = (du0 * sv).astype(bf16)
                dz_ref[rows, b_lanes] = (du0 * av * sv * (1.0 - sv)).astype(bf16)
                dz_ref[rows, v_lanes] = (ss[1, pl.ds(j, nv, stride=8), :] - dp_ref[rows, lanes]).astype(bf16)
        h, xh, r = _rms_fwd(x_ref[...], g_ref[...])
        dz = dz_ref[...]
        acc_ref[...] += _dot_tn(h.astype(bf16), dz)
        dh = _dot_nt(dz[:, 0:W_IN_CHUNK], w_ref[0])
        for j in range(1, N_CHIPS):
            dh = dh + _dot_nt(dz[:, j * W_IN_CHUNK:(j + 1) * W_IN_CHUNK], w_ref[j])
        dxb, dg = _rms_bwd(dh, xh, r, g_ref[...])
        gx_ref[...] = dx1_ref[...] + dxb
        small_ref[0:1, :] += dg

        @pl.when(i == n - 1)
        def _():
            for j in range(N_CHIPS):
                dw_ref[j] = acc_ref[:, j * W_IN_CHUNK:(j + 1) * W_IN_CHUNK].astype(bf16)
            dcw_ref[...] = jnp.sum(dcw_acc[...], axis=1)

        pl.when(i == n - 1)(cross.finish)

    nxt = lambda i: (jnp.minimum((i + 1) * hb, last), 0)
    res = pl.pallas_call(
        body, name="in_bwd", grid=(n,),
        in_specs=[_row(0, (tm, C_CONV)), pl.BlockSpec((HALO, C_CONV), nxt),
                  _row(0, (tm, C_POOL)), pl.BlockSpec((HALO, C_POOL), nxt),
                  _row(0, (tm, W_IN_COLS)),
                  pl.BlockSpec((HALO, W_IN_COLS), lambda i: (jnp.maximum(i * hb - 1, 0), 0)),
                  _row(0, (tm, D_MODEL)), _row(0, (tm, D_MODEL)), _full((HALO, C_CONV)), _full((1, D_MODEL)),
                  _full((N_CHIPS, D_MODEL, W_IN_CHUNK))] + cs["in_specs"],
        out_specs=[_row(0, (tm, D_MODEL)), _full((N_CHIPS, D_MODEL, W_IN_CHUNK)), _full((HALO, C_CONV)),
                   _full((8, D_MODEL))] + cs["out_specs"],
        out_shape=[jax.ShapeDtypeStruct((s, D_MODEL), f32), jax.ShapeDtypeStruct((N_CHIPS, D_MODEL, W_IN_CHUNK), bf16),
                   jax.ShapeDtypeStruct((HALO, C_CONV), f32), jax.ShapeDtypeStruct((8, D_MODEL), f32)] + cs["out_shape"],
        scratch_shapes=[pltpu.VMEM((4, (SHIFT_PAD + nv) * 8, 128), f32), pltpu.VMEM((4, (nv + SHIFT_PAD) * 8, 128), f32),
                        pltpu.VMEM((4, (nv + SHIFT_PAD) * 8, 128), f32), pltpu.VMEM((2, tm, 128), f32),
                        pltpu.VMEM((tm, C_CONV), f32), pltpu.VMEM((tm, W_IN_COLS), bf16),
                        pltpu.VMEM((D_MODEL, W_IN_COLS), f32), pltpu.VMEM((HALO, 8, C_CONV), f32)] + cs["scratch"],
        compiler_params=_cp("arbitrary"),
    )(du1, du1, dpo, dpo, z, z, x, dx1, conv_w, g_mix, w_in_g, *parts)
    return res[:4], res[4:]


def _rs_pair(name, grads):
    nk = len(grads)
    halves = [g.shape[1] // 2 for g in grads]

    def body(*refs):
        ins, outs = refs[:nk], refs[nk:2 * nk]
        send_sems, recv_sems = refs[2 * nk:]
        x, y, c, _, _ = _place()
        cps = [_remote(ins[k].at[:, pl.ds((1 - c) * halves[k], halves[k]), :], outs[k],
                       send_sems.at[k], recv_sems.at[k], (x, y, 1 - c)) for k in range(nk)]
        for cp in cps:
            cp.start()
        for cp in cps:
            cp.wait()

    return pl.pallas_call(
        body, name=f"rs_pair_{name}", in_specs=[ANY] * nk, out_specs=[ANY] * nk,
        out_shape=[jax.ShapeDtypeStruct((N_CHIPS, h, g.shape[2]), bf16) for g, h in zip(grads, halves)],
        scratch_shapes=[pltpu.SemaphoreType.DMA((nk,)), pltpu.SemaphoreType.DMA((nk,))],
    )(*grads)


def _rs_add_pair(name, core, grad, recv):
    _, half, cols = recv.shape

    def body(core_ref, g_ref, r_ref, o_ref):
        o_ref[...] = (g_ref[...].astype(f32) + r_ref[...].astype(f32)).astype(bf16)

    return pl.pallas_call(
        body, name=f"rs_add_pair_{name}",
        grid_spec=pltpu.PrefetchScalarGridSpec(
            num_scalar_prefetch=1, grid=(N_CHIPS,),
            in_specs=[pl.BlockSpec((1, half, cols), lambda j, core_ref: (j, core_ref[0], 0)),
                      pl.BlockSpec((1, half, cols), lambda j, core_ref: (j, 0, 0))],
            out_specs=pl.BlockSpec((1, half, cols), lambda j, core_ref: (j, 0, 0))),
        out_shape=jax.ShapeDtypeStruct(recv.shape, bf16),
        compiler_params=_cp("parallel"),
    )(core, grad, recv)


class _Cross:
    def __init__(self, parts, landed, send_sems, recv_sems):
        self.parts, self.landed, self.send_sems, self.recv_sems = parts, landed, send_sems, recv_sems
        _, _, self.c, self.me, self.others = _place()

    def _copy(self, k, j, src_chunk, dst_slot):
        ox, oy = self.others[j]
        return _remote(self.parts[k].at[src_chunk], self.landed[k].at[dst_slot], self.send_sems.at[3 * k + j],
                       self.recv_sems.at[3 * k + j], (ox, oy, self.c))

    def _each(self):
        return [(k, j, 2 * self.others[j][0] + self.others[j][1]) for k in range(len(self.parts)) for j in range(3)]

    def start(self):
        for k, j, chip in self._each():
            self._copy(k, j, chip, self.me).start()

    def finish(self):
        for k, j, chip in self._each():
            self._copy(k, j, chip, chip).wait_recv()
        for k, j, chip in self._each():
            self._copy(k, j, chip, self.me).wait_send()

    @staticmethod
    def scratch(n):
        return [pltpu.SemaphoreType.DMA((3 * n,)), pltpu.SemaphoreType.DMA((3 * n,))]


def _rs_cross(name, parts):
    nk = len(parts)

    def body(*refs):
        cross = _Cross(refs[:nk], refs[nk:2 * nk], *refs[2 * nk:])
        cross.start()
        cross.finish()

    return pl.pallas_call(
        body, name=f"rs_cross_{name}", in_specs=[ANY] * nk, out_specs=[ANY] * nk,
        out_shape=[jax.ShapeDtypeStruct(a.shape, bf16) for a in parts], scratch_shapes=_Cross.scratch(nk),
    )(*parts)


def _rs_sum_chips(name, place, landed, part):
    _, half, cols = landed.shape

    def body(place_ref, l_ref, p_ref, o_ref):
        me = place_ref[0]
        own = p_ref[0].astype(f32)
        acc = jnp.where(me == 0, own, l_ref[0].astype(f32))
        for j in range(1, N_CHIPS):
            acc = acc + jnp.where(me == j, own, l_ref[j].astype(f32))
        o_ref[...] = acc

    return pl.pallas_call(
        body, name=f"rs_sum_chips_{name}",
        grid_spec=pltpu.PrefetchScalarGridSpec(
            num_scalar_prefetch=1, grid=(1,),
            in_specs=[pl.BlockSpec((N_CHIPS, half, cols), lambda t, place_ref: (0, 0, 0)),
                      pl.BlockSpec((1, half, cols), lambda t, place_ref: (place_ref[0], 0, 0))],
            out_specs=pl.BlockSpec((half, cols), lambda t, place_ref: (place_ref[1], 0))),
        out_shape=jax.ShapeDtypeStruct((2 * half, cols), f32),
        compiler_params=_cp("arbitrary"),
    )(place, landed, part)


def _grad_pair(shards):
    nk = len(shards)

    def body(*refs):
        outs = refs[nk:2 * nk]
        send_sems, recv_sems = refs[2 * nk:]
        x, y, c, _, _ = _place()

        def half(k, core):
            h = outs[k].shape[0] // 2
            return outs[k].at[pl.ds(core * h, h), :]

        cps = [_remote(half(k, c), half(k, c), send_sems.at[k], recv_sems.at[k], (x, y, 1 - c)) for k in range(nk)]
        for cp in cps:
            cp.start()
        for k in range(nk):
            _remote(half(k, 1 - c), half(k, 1 - c), send_sems.at[k], recv_sems.at[k], (x, y, 1 - c)).wait_recv()
        for cp in cps:
            cp.wait_send()

    return pl.pallas_call(
        body, name="grad_pair", in_specs=[ANY] * nk, out_specs=[ANY] * nk,
        out_shape=[jax.ShapeDtypeStruct(a.shape, f32) for a in shards],
        input_output_aliases={k: k for k in range(nk)},
        scratch_shapes=[pltpu.SemaphoreType.DMA((nk,)), pltpu.SemaphoreType.DMA((nk,))],
    )(*shards)


def _adam_math(w, g, m, v):
    m = ADAM_B1 * m + (1.0 - ADAM_B1) * g
    v = ADAM_B2 * v + (1.0 - ADAM_B2) * (g * g)
    m_hat = m / (1.0 - ADAM_B1 ** ADAM_STEP)
    v_hat = v / (1.0 - ADAM_B2 ** ADAM_STEP)
    delta = -ADAM_LR * (m_hat / (jnp.sqrt(v_hat) + ADAM_EPS) + ADAM_WD * w)
    return delta, m, v


def _adam(k, w, g, m, v):
    rows, cols = w.shape
    tr = next(rows // d for d in (1, 2, 4, 8) if rows % (8 * d) == 0 and rows // d <= 256)

    def body(w_ref, g_ref, m_ref, v_ref, d_ref, nm_ref, nv_ref):
        d_ref[...], nm_ref[...], nv_ref[...] = _adam_math(w_ref[...], g_ref[...], m_ref[...], v_ref[...])

    spec = _row(0, (tr, cols))
    return pl.pallas_call(
        body, name=f"adam_{k}", grid=(rows // tr,), in_specs=[spec] * 4, out_specs=[spec] * 3,
        out_shape=[jax.ShapeDtypeStruct(w.shape, f32)] * 3,
        compiler_params=_cp("parallel"),
    )(w, g, m, v)


def _small_sync_adam(part, w, m, v):
    rows = part.shape[0]

    def body(p_ref, w_ref, m_ref, v_ref, g_ref, d_ref, nm_ref, nv_ref, buf, send_sems, recv_sems):
        x, y, c, _, others = _place()
        me, sibling = (x, y, c), (x, y, 1 - c)

        def slot(px, py, pc):
            return buf.at[4 * px + 2 * py + pc]

        def copy(k, block, to, src=None):
            return _remote(slot(*block) if src is None else src, slot(*block), send_sems.at[k], recv_sems.at[k], to)

        buf[4 * x + 2 * y + c] = p_ref[...]
        first = [copy(0, me, sibling, src=p_ref)]
        first += [copy(1 + j, me, (*chip, c), src=p_ref) for j, chip in enumerate(others)]
        for cp in first:
            cp.start()
        passed = [copy(4 + j, (*chip, c), sibling) for j, chip in enumerate(others)]
        for j, chip in enumerate(others):
            copy(1 + j, (*chip, c), me).wait_recv()
            passed[j].start()
        copy(0, sibling, me).wait_recv()
        for j, chip in enumerate(others):
            copy(4 + j, (*chip, 1 - c), me).wait_recv()
        for cp in first + passed:
            cp.wait_send()
        g = buf[0]
        for dev in range(1, N_DEV):
            g = g + buf[dev]
        g_ref[...] = g
        d_ref[...], nm_ref[...], nv_ref[...] = _adam_math(w_ref[...], g, m_ref[...], v_ref[...])

    shape = jax.ShapeDtypeStruct(part.shape, f32)
    return pl.pallas_call(
        body, name="small_sync_adam", in_specs=[VMEM] * 4, out_specs=[VMEM] * 4, out_shape=[shape] * 4,
        scratch_shapes=[pltpu.VMEM((N_DEV, rows, 128), f32), pltpu.SemaphoreType.DMA((7,)),
                        pltpu.SemaphoreType.DMA((7,))],
        compiler_params=pltpu.CompilerParams(vmem_limit_bytes=VMEM_LIMIT),
    )(part, w, m, v)


def _rows128(a):
    return a.reshape(-1, 128)


def _pad_rows(a, rows):
    return jnp.concatenate([a, jnp.zeros((rows - a.shape[0],) + a.shape[1:], a.dtype)], axis=0)


def _pack_small(me, g_pg, g_post, g_final, g_ffn, ln_g, ln_b, conv_b, pool_scale, pool_w, conv_w, g_mix):
    blk_ple = _pad_rows(jnp.concatenate([g_pg, g_post, g_final.reshape(1, D_MODEL)], axis=0), 8)
    blk_ffn = _pad_rows(g_ffn, 8)
    blk_mix = _pad_rows(jnp.concatenate([ln_g, ln_b, conv_b, pool_scale], axis=0), 8)
    cw = lax.dynamic_update_slice(jnp.zeros((N_CHIPS, HALO, 128), f32), _pad_rows(conv_w, HALO)[None], (me, 0, 0))
    blk_in = _pad_rows(g_mix, 8)
    return jnp.concatenate([_rows128(blk_ple), _rows128(blk_ffn), _rows128(blk_mix), _rows128(pool_w),
                            _rows128(cw), _rows128(blk_in)], axis=0)


def _unpack_small(me, packed):
    o = 0

    def take(rows):
        nonlocal o
        blk = packed[o:o + rows]
        o += rows
        return blk

    ple = take(64).reshape(8, D_MODEL)
    ffn = take(64).reshape(8, D_MODEL)
    mixb = take(32).reshape(8, C_CONV)
    pool_w = take(512).reshape(1, 4, POOL_GROUP, POOL_GROUP)
    cw = take(N_CHIPS * HALO).reshape(N_CHIPS, HALO, 128)
    inb = take(64).reshape(8, D_MODEL)
    conv_w = lax.dynamic_slice(cw, (me, 0, 0), (1, HALO, 128))[:, :CONV_K, :]
    return dict(g_ple_gate=ple[0:1], g_ple_post=ple[1:2], g_final=ple[2], loss=ple[3, 0], g_ffn=ffn[0:1],
                ln_g=mixb[0:1], ln_b=mixb[1:2], conv_b=mixb[2:3], pool_scale=mixb[3:4], pool_w=pool_w,
                conv_w=conv_w, g_mix=inb[0:1])


def kernel(x, p, g_mix, w_in, conv_w, conv_b, ln_g, ln_b, pool_w, pool_scale, w_out, g_ffn, w_gate_up, w_down, g_ple_gate, w_ple_gate, w_ple_up, g_ple_post, g_final, loss_target, m_g_mix, m_w_in, m_conv_w, m_conv_b, m_ln_g, m_ln_b, m_pool_w, m_pool_scale, m_w_out, m_g_ffn, m_w_gate_up, m_w_down, m_g_ple_gate, m_w_ple_gate, m_w_ple_up, m_g_ple_post, m_g_final, v_g_mix, v_w_in, v_conv_w, v_conv_b, v_ln_g, v_ln_b, v_pool_w, v_pool_scale, v_w_out, v_g_ffn, v_w_gate_up, v_w_down, v_g_ple_gate, v_w_ple_gate, v_w_ple_up, v_g_ple_post, v_g_final):
    seq = x.shape[1]
    me = 2 * lax.axis_index("x") + lax.axis_index("y")
    chip = me.astype(jnp.int32).reshape(1)
    core = lax.axis_index("c").astype(jnp.int32).reshape(1)
    place = jnp.concatenate([chip, core])
    xs, ps, ts = x.reshape(seq, D_MODEL), p.reshape(seq, D_PLE), loss_target.reshape(seq, D_MODEL)

    big_names = ["w_in", "w_gu", "w_out", "w_down", "w_pg", "w_pu"]
    big = [w_in[0], w_gate_up[0], w_out[0], w_down[0], w_ple_gate[0], w_ple_up[0]]
    big_m = [m_w_in[0], m_w_gate_up[0], m_w_out[0], m_w_down[0], m_w_ple_gate[0], m_w_ple_up[0]]
    big_v = [v_w_in[0], v_w_gate_up[0], v_w_out[0], v_w_down[0], v_w_ple_gate[0], v_w_ple_up[0]]
    b_in, b_gu, b_out, b_down, b_pg, b_pu = [_cast_into_slot(nm, chip, w, bf16) for nm, w in zip(big_names, big)]
    b_cw = _cast_into_slot("conv_w", chip, _pad_rows(conv_w[0], HALO), f32)
    xi, yi = lax.axis_index("x"), lax.axis_index("y")
    order = jnp.stack([me, 2 * (1 - xi) + yi, 2 * xi + 1 - yi, 2 * (1 - xi) + 1 - yi]).astype(jnp.int32)

    z, (w_in_g, cw_g, w_out_g) = _mix_in(xs, g_mix, order, [b_in, b_cw, b_out])
    conv_w_f = cw_g.transpose(1, 0, 2).reshape(HALO, C_CONV)
    w_out_f = w_out_g.reshape(D_MODEL, D_MODEL)
    (x1, mix, u1, pooled), (w_gu_g, w_down_g) = _conv_pool_out(z, xs, conv_w_f, conv_b, ln_g, ln_b, pool_w[0],
                                                                pool_scale, w_out_f, [b_gu, b_down])
    w_down_f = w_down_g.reshape(D_FF, D_MODEL)
    (x2, h2, gu, ffn_f), (w_pg_g, w_pu_g) = _ffn_fwd(x1, g_ffn, w_gu_g, w_down_f, [b_pg, b_pu])
    w_pg_f = w_pg_g.reshape(D_MODEL, D_MODEL)
    dx2, d_w_pg, d_w_pu, small_ple = _ple_loss(x2, ps, ts, g_ple_gate, g_ple_post, g_final.reshape(1, D_MODEL),
                                               w_pg_f, w_pu_g)
    def pair_reduce(group, names, grads):
        got = _rs_pair(group, grads)
        return [_rs_add_pair(nm, core, g, r) for nm, g, r in zip(names, grads, got)]

    d_w_down = _ffn_bwd_dw_down(ffn_f, dx2)
    parts_a = pair_reduce("a", ["w_pg", "w_pu", "w_down"],
                          [d_w_pg.reshape(N_CHIPS, -1, D_MODEL), d_w_pu, d_w_down.reshape(N_CHIPS, -1, D_MODEL)])
    (dx1, dgu, small_ffn), landed_a = _ffn_bwd_dx(dx2, x1, gu, g_ffn, w_gu_g, w_down_f, parts_a)
    d_w_gu = _ffn_bwd_dw_gu(h2, dgu)
    du1, dpo, d_w_out, d_pool_w, small_mix = _mix_bwd_local(dx1, mix, u1, pooled, w_out_f, ln_g, ln_b, pool_w[0],
                                                             pool_scale)
    parts_b = pair_reduce("b", ["w_gu", "w_out"], [d_w_gu, d_w_out.reshape(N_CHIPS, -1, D_MODEL)])
    (grad_x, d_w_in, d_conv_w, small_in), landed_b = _in_bwd(du1, dpo, z, xs, dx1, conv_w_f, g_mix, w_in_g, parts_b)
    parts_c = pair_reduce("c", ["w_in"], [d_w_in])
    landed_c = _rs_cross("c", parts_c)
    parts = [parts_c[0], parts_b[0], parts_b[1], parts_a[2], parts_a[0], parts_a[1]]
    landed = [landed_c[0], landed_b[0], landed_b[1], landed_a[2], landed_a[0], landed_a[1]]
    halves = [_rs_sum_chips(nm, place, a, b) for nm, a, b in zip(big_names, landed, parts)]
    big_g = _grad_pair(halves)
    big_upd = [_adam(nm, w, g, m, v) for nm, w, g, m, v in zip(big_names, big, big_g, big_m, big_v)]

    cw_chunks = d_conv_w.reshape(HALO, N_CHIPS, 128).transpose(1, 0, 2)
    part = jnp.concatenate([_rows128(small_ple), _rows128(small_ffn), _rows128(small_mix), _rows128(d_pool_w),
                            _rows128(cw_chunks), _rows128(small_in)], axis=0)
    sw = _pack_small(me, g_ple_gate, g_ple_post, g_final, g_ffn, ln_g, ln_b, conv_b, pool_scale, pool_w, conv_w[0], g_mix)
    sm = _pack_small(me, m_g_ple_gate, m_g_ple_post, m_g_final, m_g_ffn, m_ln_g, m_ln_b, m_conv_b, m_pool_scale,
                     m_pool_w, m_conv_w[0], m_g_mix)
    sv = _pack_small(me, v_g_ple_gate, v_g_ple_post, v_g_final, v_g_ffn, v_ln_g, v_ln_b, v_conv_b, v_pool_scale,
                     v_pool_w, v_conv_w[0], v_g_mix)
    small = [_unpack_small(me, a) for a in _small_sync_adam(part, sw, sm, sv)]

    names = ["g_mix", "w_in", "conv_w", "conv_b", "ln_g", "ln_b", "pool_w", "pool_scale", "w_out", "g_ffn",
             "w_gate_up", "w_down", "g_ple_gate", "w_ple_gate", "w_ple_up", "g_ple_post", "g_final"]
    big_at = {"w_in": 0, "w_gate_up": 1, "w_out": 2, "w_down": 3, "w_ple_gate": 4, "w_ple_up": 5}
    out = [small[0]["loss"], grad_x.reshape(1, seq, D_MODEL)]
    for kind in range(4):
        for nm in names:
            if nm in big_at:
                k = big_at[nm]
                out.append((big_g[k] if kind == 0 else big_upd[k][kind - 1])[None])
            else:
                out.append(small[kind][nm])
    return tuple(out)
```

```python
import functools

import jax
import jax.numpy as jnp
from jax import lax
from jax.experimental import pallas as pl
from jax.experimental.pallas import tpu as pltpu

f32, bf16 = jnp.float32, jnp.bfloat16

EPS = 1e-6
D_MODEL = 1024
C_CONV = 512
C_POOL = 512
POOL_WINDOWS = (2, 4, 8, 16)
POOL_GROUP = 128
CONV_K = 31
D_FF = 2816
D_PLE = 256
N_CHIPS = 4
N_DEV = 8
W_IN_COLS = 2 * C_CONV + C_POOL
W_IN_CHUNK = W_IN_COLS // N_CHIPS
FF_CHUNK = 2 * D_FF // N_CHIPS
PLE_CHUNK = D_MODEL // N_CHIPS
HALO = 32
ROW_TILE = 512
CONV_ROWS = 64
CONV_COLS = (slice(0, 256), slice(256, 512))
SHIFT_PAD = 32
SHIFT_GROUPS = 16
FF_SUB = (0, 512, 1024, FF_CHUNK)
VMEM_LIMIT = 56 * 1024 * 1024

ADAM_LR = 0.001
ADAM_B1 = 0.9
ADAM_B2 = 0.999
ADAM_EPS = 1e-08
ADAM_WD = 0.01
ADAM_STEP = 10

MESH = pl.DeviceIdType.MESH
ANY = pl.BlockSpec(memory_space=pl.ANY)
VMEM = pl.BlockSpec(memory_space=pltpu.VMEM)


def _cp(*sem):
    return pltpu.CompilerParams(dimension_semantics=sem, vmem_limit_bytes=VMEM_LIMIT)


def _dot(a, b):
    return jnp.dot(a, b, preferred_element_type=f32)


def _dot_nt(a, b):
    return lax.dot_general(a, b, (((1,), (1,)), ((), ())), preferred_element_type=f32)


def _dot_tn(a, b):
    return lax.dot_general(a, b, (((0,), (0,)), ((), ())), preferred_element_type=f32)


def _sigmoid(v):
    return jax.nn.sigmoid(v)


def _rms_fwd(v, g):
    r = lax.rsqrt(jnp.mean(v * v, axis=-1, keepdims=True) + EPS)
    vh = v * r
    return vh * g, vh, r


def _rms_bwd(dy, vh, r, g):
    dvh = dy * g
    dv = r * (dvh - vh * jnp.mean(dvh * vh, axis=-1, keepdims=True))
    return dv, jnp.sum(dy * vh, axis=0, keepdims=True)


def _silu_grad(v, s):
    return s * (1.0 + v * (1.0 - s))


def _row(i, n):
    return pl.BlockSpec((n[0], n[1]), lambda *a: (a[i], 0))


def _full(shape):
    nd = len(shape)
    return pl.BlockSpec(shape, lambda *a: (0,) * nd)


def _place():
    x, y, c = lax.axis_index("x"), lax.axis_index("y"), lax.axis_index("c")
    others = [(1 - x, y), (x, 1 - y), (1 - x, 1 - y)]
    return x, y, c, 2 * x + y, others


def _remote(src, dst, send_sem, recv_sem, dev):
    return pltpu.make_async_remote_copy(src_ref=src, dst_ref=dst, send_sem=send_sem, recv_sem=recv_sem,
                                        device_id=dev, device_id_type=MESH)


def _cast_into_slot(name, me, w, dtype):
    rows, cols = w.shape
    tr = next(rows // d for d in (1, 2, 4, 8) if rows % (16 * d) == 0 and rows // d <= 512)

    def body(me_ref, w_ref, o_ref):
        o_ref[0] = w_ref[...].astype(dtype)

    return pl.pallas_call(
        body, name=f"cast_{name}",
        grid_spec=pltpu.PrefetchScalarGridSpec(
            num_scalar_prefetch=1, grid=(rows // tr,),
            in_specs=[pl.BlockSpec((tr, cols), lambda r, me_ref: (r, 0))],
            out_specs=pl.BlockSpec((1, tr, cols), lambda r, me_ref: (me_ref[0], r, 0))),
        out_shape=jax.ShapeDtypeStruct((N_CHIPS, rows, cols), dtype),
        compiler_params=_cp("parallel"),
    )(me, w)


class _Gather:
    def __init__(self, bufs, send_sems, recv_sems):
        self.bufs, self.send_sems, self.recv_sems = bufs, send_sems, recv_sems
        self.x, self.y, self.c, self.me, self.others = _place()
        self.halves = [b.shape[1] // 2 for b in bufs]

    def _piece(self, k, chip, half):
        return self.bufs[k].at[chip, pl.ds(half * self.halves[k], self.halves[k]), :]

    def _ici(self, k, j, chip):
        ox, oy = self.others[j]
        piece = self._piece(k, chip, self.c)
        return _remote(piece, piece, self.send_sems.at[6 * k + j], self.recv_sems.at[6 * k + j], (ox, oy, self.c))

    def _pair(self, k, j, half):
        ox, oy = self.others[j]
        piece = self._piece(k, 2 * ox + oy, half)
        return _remote(piece, piece, self.send_sems.at[6 * k + 3 + j], self.recv_sems.at[6 * k + 3 + j],
                       (self.x, self.y, 1 - self.c))

    def _each(self, ks=None):
        return [(k, j) for k in (range(len(self.bufs)) if ks is None else ks) for j in range(3)]

    def chip(self, j):
        ox, oy = self.others[j]
        return 2 * ox + oy

    def start(self):
        for k, j in self._each():
            self._ici(k, j, self.me).start()

    def forward(self, pairs=None):
        for k, j in self._each() if pairs is None else pairs:
            self._ici(k, j, self.chip(j)).wait_recv()
            self._pair(k, j, self.c).start()

    def landed(self, pairs):
        for k, j in pairs:
            self._pair(k, j, 1 - self.c).wait_recv()

    def finish(self, ks=None):
        self.landed(self._each(ks))
        for k, j in self._each():
            self._ici(k, j, self.me).wait_send()
            self._pair(k, j, self.c).wait_send()

    @staticmethod
    def scratch(n):
        return [pltpu.SemaphoreType.DMA((6 * n,)), pltpu.SemaphoreType.DMA((6 * n,))]


def _carried(bufs):
    n = len(bufs)
    return dict(in_specs=[ANY] * n, out_specs=[ANY] * n,
                out_shape=[jax.ShapeDtypeStruct(b.shape, b.dtype) for b in bufs], scratch=_Gather.scratch(n))


def _mix_in(x, g_mix, order, carry):
    s = x.shape[0]
    tm = min(2 * ROW_TILE, s)
    n = s // tm
    nc = len(carry)
    cs = _carried(carry)

    def body(order_ref, x_ref, g_ref, *refs):
        z_ref = refs[nc]
        bufs = refs[nc + 1:2 * nc + 1]
        h_ref, w_ref, w_sem = refs[2 * nc + 1:2 * nc + 4]
        gather = _Gather(bufs, *refs[2 * nc + 4:])
        q, i = pl.program_id(0), pl.program_id(1)
        first = i == 0
        pl.when(jnp.logical_and(q == 0, first))(gather.start)
        for j in range(3):

            @pl.when(jnp.logical_and(q == j + 1, first))
            def _():
                gather.forward([(0, j)])
                gather.landed([(0, j)])

        @pl.when(first)
        def _():
            load = pltpu.make_async_copy(bufs[0].at[order_ref[q]], w_ref, w_sem)
            load.start()
            load.wait()

        @pl.when(q == 0)
        def _():
            h, _, _ = _rms_fwd(x_ref[...], g_ref[...])
            h_ref[i] = h.astype(bf16)

        z_ref[...] = _dot(h_ref[i], w_ref[...])

        @pl.when(jnp.logical_and(q == N_CHIPS - 1, i == n - 1))
        def _():
            rest = gather._each(range(1, nc))
            gather.forward(rest)
            gather.finish(range(1, nc))

    res = pl.pallas_call(
        body, name="mix_in",
        grid_spec=pltpu.PrefetchScalarGridSpec(
            num_scalar_prefetch=1, grid=(N_CHIPS, n),
            in_specs=[pl.BlockSpec((tm, D_MODEL), lambda q, i, order_ref: (jnp.where(q == 0, i, 0), 0)),
                      pl.BlockSpec((1, D_MODEL), lambda q, i, order_ref: (0, 0))] + cs["in_specs"],
            out_specs=[pl.BlockSpec((tm, W_IN_CHUNK), lambda q, i, order_ref: (i, order_ref[q]))] + cs["out_specs"],
            scratch_shapes=[pltpu.VMEM((n, tm, D_MODEL), bf16), pltpu.VMEM((D_MODEL, W_IN_CHUNK), bf16),
                            pltpu.SemaphoreType.DMA(())] + cs["scratch"]),
        out_shape=[jax.ShapeDtypeStruct((s, W_IN_COLS), f32)] + cs["out_shape"],
        input_output_aliases={3 + k: 1 + k for k in range(nc)},
        compiler_params=_cp("arbitrary", "arbitrary"),
    )(order, x, g_mix, *carry)
    return res[0], res[1:]


def _tap_offsets(lo, hi):
    groups = [[o for o in range(lo, hi + 1) if o % 8 == s] for s in range(8)]
    return [g for g in groups if g]


def _tap_sum(buf, w_ref, row0, cols, tap_of):
    acc = jnp.zeros((CONV_ROWS, cols.stop - cols.start), f32)
    for offs in _tap_offsets(0, CONV_K - 1):
        slab = buf[pl.ds(row0 + offs[0], offs[-1] - offs[0] + CONV_ROWS), cols]
        for o in offs:
            acc = acc + w_ref[pl.ds(tap_of(o), 1), cols] * slab[o - offs[0]:o - offs[0] + CONV_ROWS]
    return acc


def _pool_counts(tm, w, first_row):
    t1 = (lax.broadcasted_iota(jnp.int32, (tm, 1), 0) + first_row + 1).astype(f32)
    return jnp.minimum(t1, float(w))


def _conv_pool_out(z, x, conv_w, conv_b, ln_g, ln_b, pool_w, pool_scale, w_out, carry):
    s = x.shape[0]
    tm = min(ROW_TILE, s)
    n = s // tm
    hb = tm // HALO
    nc = len(carry)
    cs = _carried(carry)

    def body(z_ref, zp_ref, x_ref, cw_ref, cb_ref, lg_ref, lb_ref, pw_ref, ps_ref, wo_ref, *refs):
        x1_ref, mix_ref, u1_ref, pooled_ref = refs[nc:nc + 4]
        ubuf, vbuf = refs[2 * nc + 4:2 * nc + 6]
        gather = _Gather(refs[nc + 4:2 * nc + 4], *refs[2 * nc + 6:])
        i = pl.program_id(0)
        pl.when(i == 0)(gather.start)
        keep = (i > 0).astype(f32)
        zp = zp_ref[...] * keep
        ubuf[0:HALO, :] = zp[:, :C_CONV] * _sigmoid(zp[:, C_CONV:2 * C_CONV])
        vbuf[0:HALO, :] = zp[:, 2 * C_CONV:]
        ubuf[HALO:, :] = z_ref[:, :C_CONV] * _sigmoid(z_ref[:, C_CONV:2 * C_CONV])
        vbuf[HALO:, :] = z_ref[:, 2 * C_CONV:]
        off = HALO - (CONV_K - 1)
        for r0 in range(0, tm, CONV_ROWS):
            for cols in CONV_COLS:
                u1_ref[r0:r0 + CONV_ROWS, cols] = cb_ref[:, cols] + _tap_sum(ubuf, cw_ref, r0 + off, cols, lambda o: o)
        u1 = u1_ref[...]
        mu = jnp.mean(u1, axis=-1, keepdims=True)
        uc = u1 - mu
        rstd = lax.rsqrt(jnp.mean(uc * uc, axis=-1, keepdims=True) + EPS)
        u2 = uc * rstd * lg_ref[...] + lb_ref[...]
        mix_ref[:, :C_CONV] = (u2 * _sigmoid(u2)).astype(bf16)
        for g, w in enumerate(POOL_WINDOWS):
            cols = slice(g * POOL_GROUP, (g + 1) * POOL_GROUP)
            acc = vbuf[pl.ds(HALO, tm), cols]
            vg = acc
            for d in range(1, w):
                acc = acc + vbuf[pl.ds(HALO - d, tm), cols]
            pooled = (acc / _pool_counts(tm, w, i * tm) - vg).astype(bf16)
            pooled_ref[:, cols] = pooled
            mixed = _dot(pooled, pw_ref[g].astype(bf16))
            mix_ref[:, C_CONV + g * POOL_GROUP:C_CONV + (g + 1) * POOL_GROUP] = (mixed * ps_ref[:, cols]).astype(bf16)
        x1_ref[...] = x_ref[...] + _dot(mix_ref[...], wo_ref[...])
        @pl.when(i == n - 1)
        def _():
            gather.forward()
            gather.finish()

    res = pl.pallas_call(
        body, name="conv_pool_out", grid=(n,),
        in_specs=[_row(0, (tm, W_IN_COLS)),
                  pl.BlockSpec((HALO, W_IN_COLS), lambda i: (jnp.maximum(i * hb - 1, 0), 0)),
                  _row(0, (tm, D_MODEL)), _full((HALO, C_CONV)), _full((1, C_CONV)), _full((1, C_CONV)),
                  _full((1, C_CONV)), _full((4, POOL_GROUP, POOL_GROUP)), _full((1, C_POOL)),
                  _full((D_MODEL, D_MODEL))] + cs["in_specs"],
        out_specs=[_row(0, (tm, D_MODEL)), _row(0, (tm, D_MODEL)), _row(0, (tm, C_CONV)), _row(0, (tm, C_POOL))]
        + cs["out_specs"],
        out_shape=[jax.ShapeDtypeStruct((s, D_MODEL), f32), jax.ShapeDtypeStruct((s, D_MODEL), bf16),
                   jax.ShapeDtypeStruct((s, C_CONV), f32), jax.ShapeDtypeStruct((s, C_POOL), bf16)] + cs["out_shape"],
        input_output_aliases={10 + k: 4 + k for k in range(nc)},
        scratch_shapes=[pltpu.VMEM((HALO + tm, C_CONV), f32), pltpu.VMEM((HALO + tm, C_POOL), f32)] + cs["scratch"],
        compiler_params=_cp("arbitrary"),
    )(z, z, x, conv_w, conv_b, ln_g, ln_b, pool_w, pool_scale, w_out, *carry)
    return res[:4], res[4:]


def _ffn_fwd(x1, g_ffn, w_gu_g, w_down, carry):
    s = x1.shape[0]
    tm = min(ROW_TILE, s)
    n = s // tm
    nc = len(carry)
    cs = _carried(carry)

    def body(x1_ref, g_ref, wg_ref, wu_ref, wd_ref, *refs):
        x2_ref, h2_ref, gu_ref, f_ref = refs[nc:nc + 4]
        acc_ref = refs[2 * nc + 4]
        gather = _Gather(refs[nc + 4:2 * nc + 4], *refs[2 * nc + 5:])
        i, c = pl.program_id(0), pl.program_id(1)
        pl.when(jnp.logical_and(i == 0, c == 0))(gather.start)
        pl.when(jnp.logical_and(i == n - 1, c == 0))(gather.forward)

        @pl.when(c == 0)
        def _():
            h, _, _ = _rms_fwd(x1_ref[...], g_ref[...])
            h2_ref[...] = h.astype(bf16)
            acc_ref[...] = jnp.zeros_like(acc_ref)

        h = h2_ref[...]
        for lo, hi in zip(FF_SUB[:-1], FF_SUB[1:]):
            gate = _dot(h, wg_ref[0, :, lo:hi])
            up = _dot(h, wu_ref[0, :, lo:hi])
            gu_ref[0, :, lo:hi] = gate.astype(bf16)
            gu_ref[1, :, lo:hi] = up.astype(bf16)
            f = (gate * _sigmoid(gate) * up).astype(bf16)
            f_ref[:, lo:hi] = f
            acc_ref[...] += _dot(f, wd_ref[lo:hi, :])

        @pl.when(c == 1)
        def _():
            x2_ref[...] = x1_ref[...] + acc_ref[...]

        pl.when(jnp.logical_and(i == n - 1, c == 1))(gather.finish)

    res = pl.pallas_call(
        body, name="ffn_fwd", grid=(n, 2),
        in_specs=[_row(0, (tm, D_MODEL)), _full((1, D_MODEL)),
                  pl.BlockSpec((1, D_MODEL, FF_CHUNK), lambda i, c: (c, 0, 0)),
                  pl.BlockSpec((1, D_MODEL, FF_CHUNK), lambda i, c: (2 + c, 0, 0)),
                  pl.BlockSpec((FF_CHUNK, D_MODEL), lambda i, c: (c, 0))] + cs["in_specs"],
        out_specs=[_row(0, (tm, D_MODEL)), _row(0, (tm, D_MODEL)),
                   pl.BlockSpec((2, tm, FF_CHUNK), lambda i, c: (0, i, c)),
                   pl.BlockSpec((tm, FF_CHUNK), lambda i, c: (i, c))] + cs["out_specs"],
        out_shape=[jax.ShapeDtypeStruct((s, D_MODEL), f32), jax.ShapeDtypeStruct((s, D_MODEL), bf16),
                   jax.ShapeDtypeStruct((2, s, D_FF), bf16), jax.ShapeDtypeStruct((s, D_FF), bf16)] + cs["out_shape"],
        input_output_aliases={5 + k: 4 + k for k in range(nc)},
        scratch_shapes=[pltpu.VMEM((tm, D_MODEL), f32)] + cs["scratch"],
        compiler_params=_cp("arbitrary", "arbitrary"),
    )(x1, g_ffn, w_gu_g, w_gu_g, w_down, *carry)
    return res[:4], res[4:]


def _ple_loss(x2, p, target, g_pg, g_post, g_final, w_pg, w_pu_g):
    s = x2.shape[0]
    tm = min(ROW_TILE, s)
    n = s // tm

    def body(x2_ref, p_ref, t_ref, gpg_ref, gpo_ref, gf_ref, wpg_ref, wpu_ref,
             dx2_ref, dwpg_ref, dwpu_ref, small_ref, apg_ref, apu_ref, h3_ref, dgl_ref, dpe_ref):
        i = pl.program_id(0)

        @pl.when(i == 0)
        def _():
            apg_ref[...] = jnp.zeros_like(apg_ref)
            apu_ref[...] = jnp.zeros_like(apu_ref)
            small_ref[...] = jnp.zeros_like(small_ref)

        halves = (slice(0, tm // 2), slice(tm // 2, tm))
        state = ({}, {})

        def head(k):
            rows, a = halves[k], state[k]
            a["x2"] = x2_ref[rows, :]
            h3, a["x2h"], a["r2"] = _rms_fwd(a["x2"], gpg_ref[...])
            h3_ref[rows, :] = h3.astype(bf16)
            a["gl"] = _dot(h3_ref[rows, :], wpg_ref[...])
            pb = p_ref[rows, :].astype(bf16)
            a["pe"] = jnp.concatenate([_dot(pb, wpu_ref[j]) for j in range(N_CHIPS)], axis=-1)

        def middle(k):
            rows, a = halves[k], state[k]
            gate = _sigmoid(a["gl"])
            e, peh, rp = _rms_fwd(a["pe"], gpo_ref[...])
            x3 = a["x2"] + gate * e
            y, x3h, r3 = _rms_fwd(x3, gf_ref[...])
            d = y - t_ref[rows, :]
            loss = 0.5 * jnp.sum(jnp.sum(d * d, axis=-1, keepdims=True) * (1.0 / D_MODEL), axis=0, keepdims=True)
            dx3, dgf = _rms_bwd(d * (1.0 / D_MODEL), x3h, r3, gf_ref[...])
            dpe, dgpo = _rms_bwd(dx3 * gate, peh, rp, gpo_ref[...])
            dgl_ref[rows, :] = (dx3 * e * gate * (1.0 - gate)).astype(bf16)
            dpe_ref[rows, :] = dpe.astype(bf16)
            a["dx3"] = dx3
            a["dh3"] = _dot_nt(dgl_ref[rows, :], wpg_ref[...])
            small_ref[1:2, :] += dgpo
            small_ref[2:3, :] += dgf
            small_ref[3:4, :] += jnp.broadcast_to(loss, (1, D_MODEL))

        def tail(k):
            rows, a = halves[k], state[k]
            dx2b, dgpg = _rms_bwd(a["dh3"], a["x2h"], a["r2"], gpg_ref[...])
            dx2_ref[rows, :] = a["dx3"] + dx2b
            small_ref[0:1, :] += dgpg

        for stage in (head, middle, tail):
            stage(0)
            stage(1)
        apg_ref[...] += _dot_tn(h3_ref[...], dgl_ref[...])
        apu_ref[...] += _dot_tn(p_ref[...].astype(bf16), dpe_ref[...])

        @pl.when(i == n - 1)
        def _():
            dwpg_ref[...] = apg_ref[...].astype(bf16)
            for j in range(N_CHIPS):
                dwpu_ref[j] = apu_ref[:, j * PLE_CHUNK:(j + 1) * PLE_CHUNK].astype(bf16)

    return pl.pallas_call(
        body, name="ple_loss", grid=(n,),
        in_specs=[_row(0, (tm, D_MODEL)), _row(0, (tm, D_PLE)), _row(0, (tm, D_MODEL)),
                  _full((1, D_MODEL)), _full((1, D_MODEL)), _full((1, D_MODEL)),
                  _full((D_MODEL, D_MODEL)), _full((N_CHIPS, D_PLE, PLE_CHUNK))],
        out_specs=[_row(0, (tm, D_MODEL)), _full((D_MODEL, D_MODEL)), _full((N_CHIPS, D_PLE, PLE_CHUNK)),
                   _full((8, D_MODEL))],
        out_shape=[jax.ShapeDtypeStruct((s, D_MODEL), f32), jax.ShapeDtypeStruct((D_MODEL, D_MODEL), bf16),
                   jax.ShapeDtypeStruct((N_CHIPS, D_PLE, PLE_CHUNK), bf16), jax.ShapeDtypeStruct((8, D_MODEL), f32)],
        scratch_shapes=[pltpu.VMEM((D_MODEL, D_MODEL), f32), pltpu.VMEM((D_PLE, D_MODEL), f32),
                        pltpu.VMEM((tm, D_MODEL), bf16), pltpu.VMEM((tm, D_MODEL), bf16), pltpu.VMEM((tm, D_MODEL), bf16)],
        compiler_params=_cp("arbitrary"),
    )(x2, p, target, g_pg, g_post, g_final, w_pg, w_pu_g)


def _crossed(parts):
    n = len(parts)
    return dict(in_specs=[ANY] * n, out_specs=[ANY] * n,
                out_shape=[jax.ShapeDtypeStruct(a.shape, a.dtype) for a in parts], scratch=_Cross.scratch(n))


def _ffn_bwd_dx(dx2, x1, gu, g_ffn, w_gu_g, w_down, parts):
    s = x1.shape[0]
    tm = min(ROW_TILE, s)
    n = s // tm
    nc = len(parts)
    cs = _crossed(parts)

    def body(dx2_ref, x1_ref, gu_ref, g_ref, wg_ref, wu_ref, wd_ref, *refs):
        dx1_ref, dgu_ref, small_ref = refs[nc:nc + 3]
        acc_ref = refs[2 * nc + 3]
        cross = _Cross(refs[:nc], refs[nc + 3:2 * nc + 3], *refs[2 * nc + 4:])
        i, c = pl.program_id(0), pl.program_id(1)
        pl.when(jnp.logical_and(i == 0, c == 0))(cross.start)

        @pl.when(jnp.logical_and(i == 0, c == 0))
        def _():
            small_ref[...] = jnp.zeros_like(small_ref)

        @pl.when(c == 0)
        def _():
            acc_ref[...] = jnp.zeros_like(acc_ref)

        dyb = dx2_ref[...].astype(bf16)
        for lo, hi in zip(FF_SUB[:-1], FF_SUB[1:]):
            df = _dot_nt(dyb, wd_ref[lo:hi, :])
            gate = gu_ref[0, :, lo:hi].astype(f32)
            up = gu_ref[1, :, lo:hi].astype(f32)
            sg = _sigmoid(gate)
            dgate = (df * up * _silu_grad(gate, sg)).astype(bf16)
            dup = (df * gate * sg).astype(bf16)
            dgu_ref[0, :, lo:hi] = dgate
            dgu_ref[1, :, lo:hi] = dup
            acc_ref[...] += _dot_nt(dgate, wg_ref[0, :, lo:hi]) + _dot_nt(dup, wu_ref[0, :, lo:hi])

        @pl.when(c == 1)
        def _():
            _, x1h, r1 = _rms_fwd(x1_ref[...], g_ref[...])
            dx1b, dg = _rms_bwd(acc_ref[...], x1h, r1, g_ref[...])
            dx1_ref[...] = dx2_ref[...] + dx1b
            small_ref[0:1, :] += dg

        pl.when(jnp.logical_and(i == n - 1, c == 1))(cross.finish)

    res = pl.pallas_call(
        body, name="ffn_bwd_dx", grid=(n, 2),
        in_specs=[_row(0, (tm, D_MODEL)), _row(0, (tm, D_MODEL)),
                  pl.BlockSpec((2, tm, FF_CHUNK), lambda i, c: (0, i, c)), _full((1, D_MODEL)),
                  pl.BlockSpec((1, D_MODEL, FF_CHUNK), lambda i, c: (c, 0, 0)),
                  pl.BlockSpec((1, D_MODEL, FF_CHUNK), lambda i, c: (2 + c, 0, 0)),
                  pl.BlockSpec((FF_CHUNK, D_MODEL), lambda i, c: (c, 0))] + cs["in_specs"],
        out_specs=[_row(0, (tm, D_MODEL)), pl.BlockSpec((2, tm, FF_CHUNK), lambda i, c: (0, i, c)),
                   _full((8, D_MODEL))] + cs["out_specs"],
        out_shape=[jax.ShapeDtypeStruct((s, D_MODEL), f32), jax.ShapeDtypeStruct((2, s, D_FF), bf16),
                   jax.ShapeDtypeStruct((8, D_MODEL), f32)] + cs["out_shape"],
        scratch_shapes=[pltpu.VMEM((tm, D_MODEL), f32)] + cs["scratch"],
        compiler_params=_cp("arbitrary", "arbitrary"),
    )(dx2, x1, gu, g_ffn, w_gu_g, w_gu_g, w_down, *parts)
    return res[:3], res[3:]


def _ffn_bwd_dw_gu(h2, dgu):
    s = h2.shape[0]
    ts = min(2 * ROW_TILE, s)
    n = s // ts

    def body(h_ref, d_ref, o_ref, acc_ref):
        t = pl.program_id(1)

        @pl.when(t == 0)
        def _():
            acc_ref[...] = jnp.zeros_like(acc_ref)

        acc_ref[...] += _dot_tn(h_ref[...], d_ref[0])

        @pl.when(t == n - 1)
        def _():
            o_ref[0] = acc_ref[...].astype(bf16)

    return pl.pallas_call(
        body, name="ffn_bwd_dw_gu", grid=(N_CHIPS, n),
        in_specs=[pl.BlockSpec((ts, D_MODEL), lambda j, t: (t, 0)),
                  pl.BlockSpec((1, ts, FF_CHUNK), lambda j, t: (j // 2, t, j % 2))],
        out_specs=pl.BlockSpec((1, D_MODEL, FF_CHUNK), lambda j, t: (j, 0, 0)),
        out_shape=jax.ShapeDtypeStruct((N_CHIPS, D_MODEL, FF_CHUNK), bf16),
        scratch_shapes=[pltpu.VMEM((D_MODEL, FF_CHUNK), f32)],
        compiler_params=_cp("parallel", "arbitrary"),
    )(h2, dgu)


def _ffn_bwd_dw_down(f, dx2):
    s = dx2.shape[0]
    ts = min(2 * ROW_TILE, s)
    n = s // ts

    def body(f_ref, d_ref, o_ref, acc_ref):
        t = pl.program_id(1)

        @pl.when(t == 0)
        def _():
            acc_ref[...] = jnp.zeros_like(acc_ref)

        acc_ref[...] += _dot_tn(f_ref[...], d_ref[...].astype(bf16))

        @pl.when(t == n - 1)
        def _():
            o_ref[...] = acc_ref[...].astype(bf16)

    return pl.pallas_call(
        body, name="ffn_bwd_dw_down", grid=(2, n),
        in_specs=[pl.BlockSpec((ts, FF_CHUNK), lambda c, t: (t, c)),
                  pl.BlockSpec((ts, D_MODEL), lambda c, t: (t, 0))],
        out_specs=pl.BlockSpec((FF_CHUNK, D_MODEL), lambda c, t: (c, 0)),
        out_shape=jax.ShapeDtypeStruct((D_FF, D_MODEL), bf16),
        scratch_shapes=[pltpu.VMEM((FF_CHUNK, D_MODEL), f32)],
        compiler_params=_cp("parallel", "arbitrary"),
    )(f, dx2)


def _mix_bwd_local(dx1, mix, u1, pooled, w_out, ln_g, ln_b, pool_w, pool_scale):
    s = dx1.shape[0]
    tm = min(ROW_TILE, s)
    n = s // tm

    def body(dx1_ref, mix_ref, u1_ref, po_ref, wo_ref, lg_ref, lb_ref, pw_ref, ps_ref,
             du1_ref, dpo_ref, dwo_ref, dpw_ref, small_ref, awo_ref):
        i = pl.program_id(0)

        @pl.when(i == 0)
        def _():
            awo_ref[...] = jnp.zeros_like(awo_ref)
            dpw_ref[...] = jnp.zeros_like(dpw_ref)
            small_ref[...] = jnp.zeros_like(small_ref)

        dyb = dx1_ref[...].astype(bf16)
        dmix = _dot_nt(dyb, wo_ref[...])
        awo_ref[...] += _dot_tn(mix_ref[...], dyb)
        u1 = u1_ref[...]
        mu = jnp.mean(u1, axis=-1, keepdims=True)
        uc = u1 - mu
        rstd = lax.rsqrt(jnp.mean(uc * uc, axis=-1, keepdims=True) + EPS)
        uh = uc * rstd
        u2 = uh * lg_ref[...] + lb_ref[...]
        du2 = dmix[:, :C_CONV] * _silu_grad(u2, _sigmoid(u2))
        duh = du2 * lg_ref[...]
        du1 = rstd * (duh - jnp.mean(duh, axis=-1, keepdims=True) - uh * jnp.mean(duh * uh, axis=-1, keepdims=True))
        du1_ref[...] = du1
        small_ref[0:1, :] += jnp.sum(du2 * uh, axis=0, keepdims=True)
        small_ref[1:2, :] += jnp.sum(du2, axis=0, keepdims=True)
        small_ref[2:3, :] += jnp.sum(du1, axis=0, keepdims=True)
        for g in range(len(POOL_WINDOWS)):
            cols = slice(g * POOL_GROUP, (g + 1) * POOL_GROUP)
            dq = dmix[:, C_CONV + g * POOL_GROUP:C_CONV + (g + 1) * POOL_GROUP]
            pwb = pw_ref[g].astype(bf16)
            pg = po_ref[:, cols]
            mixed = _dot(pg, pwb)
            small_ref[3:4, cols] += jnp.sum(dq * mixed, axis=0, keepdims=True)
            dmixed = (dq * ps_ref[:, cols]).astype(bf16)
            dpw_ref[g] += _dot_tn(pg, dmixed)
            dpo_ref[:, cols] = _dot_nt(dmixed, pwb)

        @pl.when(i == n - 1)
        def _():
            dwo_ref[...] = awo_ref[...].astype(bf16)

    return pl.pallas_call(
        body, name="mix_bwd_local", grid=(n,),
        in_specs=[_row(0, (tm, D_MODEL)), _row(0, (tm, D_MODEL)), _row(0, (tm, C_CONV)), _row(0, (tm, C_POOL)),
                  _full((D_MODEL, D_MODEL)), _full((1, C_CONV)), _full((1, C_CONV)),
                  _full((4, POOL_GROUP, POOL_GROUP)), _full((1, C_POOL))],
        out_specs=[_row(0, (tm, C_CONV)), _row(0, (tm, C_POOL)), _full((D_MODEL, D_MODEL)),
                   _full((4, POOL_GROUP, POOL_GROUP)), _full((8, C_CONV))],
        out_shape=[jax.ShapeDtypeStruct((s, C_CONV), f32), jax.ShapeDtypeStruct((s, C_POOL), f32),
                   jax.ShapeDtypeStruct((D_MODEL, D_MODEL), bf16),
                   jax.ShapeDtypeStruct((4, POOL_GROUP, POOL_GROUP), f32), jax.ShapeDtypeStruct((8, C_CONV), f32)],
        scratch_shapes=[pltpu.VMEM((D_MODEL, D_MODEL), f32)],
        compiler_params=_cp("arbitrary"),
    )(dx1, mix, u1, pooled, w_out, ln_g, ln_b, pool_w, pool_scale)


def _in_bwd(du1, dpo, z, x, dx1, conv_w, g_mix, w_in_g, parts):
    s = x.shape[0]
    tm = min(ROW_TILE, s)
    n = s // tm
    hb = tm // HALO
    last = s // HALO - 1
    nv = tm // 8
    assert nv >= SHIFT_PAD and nv % SHIFT_GROUPS == 0
    nc = len(parts)
    cs = _crossed(parts)

    def body(du_ref, dun_ref, dp_ref, dpn_ref, z_ref, zp_ref, x_ref, dx1_ref, cw_ref, g_ref, w_ref, *refs):
        gx_ref, dw_ref, dcw_ref, small_ref = refs[nc:nc + 4]
        eu, ed, ep, ss, u0_ref, dz_ref, acc_ref, dcw_acc = refs[2 * nc + 4:2 * nc + 12]
        cross = _Cross(refs[:nc], refs[nc + 4:2 * nc + 4], *refs[2 * nc + 12:])
        i = pl.program_id(0)
        pl.when(i == 0)(cross.start)

        @pl.when(i == 0)
        def _():
            acc_ref[...] = jnp.zeros_like(acc_ref)
            dcw_acc[...] = jnp.zeros_like(dcw_acc)
            small_ref[...] = jnp.zeros_like(small_ref)

        keep_prev = (i > 0).astype(f32)
        keep_next = (i < n - 1).astype(f32)
        zp = zp_ref[...] * keep_prev
        u0_prev = zp[:, :C_CONV] * _sigmoid(zp[:, C_CONV:2 * C_CONV])
        u0_ref[...] = z_ref[:, :C_CONV] * _sigmoid(z_ref[:, C_CONV:2 * C_CONV])
        du_next = dun_ref[...] * keep_next
        for c, w_pool in enumerate(POOL_WINDOWS):
            lanes = slice(c * 128, (c + 1) * 128)
            eu[c, pl.ds(0, SHIFT_PAD, stride=8), :] = u0_prev[:, lanes]
            ed[c, pl.ds(nv * 8 + 7, SHIFT_PAD, stride=8), :] = du_next[:, lanes]
            ep[c, pl.ds(nv * 8 + 7, SHIFT_PAD, stride=8), :] = (
                dpn_ref[:, lanes] * keep_next / _pool_counts(HALO, w_pool, (i + 1) * tm))
            for j in range(8):
                rows = slice(j * nv, (j + 1) * nv)
                eu[c, pl.ds(SHIFT_PAD * 8 + j, nv, stride=8), :] = u0_ref[rows, lanes]
                ed[c, pl.ds(j, nv, stride=8), :] = du_ref[rows, lanes]
                ep[c, pl.ds(j, nv, stride=8), :] = dp_ref[rows, lanes] / _pool_counts(nv, w_pool, i * tm + j * nv)
                if j >= 1:
                    eu[c, pl.ds(j, SHIFT_PAD, stride=8), :] = u0_ref[j * nv - SHIFT_PAD:j * nv, lanes]
                if j <= 6:
                    edge = slice((j + 1) * nv, (j + 1) * nv + SHIFT_PAD)
                    ed[c, pl.ds(nv * 8 + j, SHIFT_PAD, stride=8), :] = du_ref[edge, lanes]
                    ep[c, pl.ds(nv * 8 + j, SHIFT_PAD, stride=8), :] = (
                        dp_ref[edge, lanes] / _pool_counts(SHIFT_PAD, w_pool, i * tm + (j + 1) * nv))
        for c, w_pool in enumerate(POOL_WINDOWS):
            lanes = slice(c * 128, (c + 1) * 128)
            b_lanes = slice(C_CONV + c * 128, C_CONV + (c + 1) * 128)
            v_lanes = slice(2 * C_CONV + c * 128, 2 * C_CONV + (c + 1) * 128)
            for v0 in range(0, nv, SHIFT_GROUPS):
                span = SHIFT_GROUPS * 8
                acc = jnp.zeros((span, 128), f32)
                for k in range(CONV_K):
                    acc = acc + cw_ref[pl.ds(k, 1), lanes] * ed[c, pl.ds((v0 + CONV_K - 1 - k) * 8, span), :]
                ss[0, v0 * 8:v0 * 8 + span, :] = acc
                acc = ep[c, pl.ds(v0 * 8, span), :]
                for d in range(1, w_pool):
                    acc = acc + ep[c, pl.ds((v0 + d) * 8, span), :]
                ss[1, v0 * 8:v0 * 8 + span, :] = acc
                d1 = ed[c, pl.ds(v0 * 8, span), :]
                for k in range(CONV_K):
                    prod = d1 * eu[c, pl.ds((SHIFT_PAD - (CONV_K - 1) + v0 + k) * 8, span), :]
                    fold = prod[0:8]
                    for r in range(8, span, 8):
                        fold = fold + prod[r:r + 8]
                    dcw_acc[k, :, lanes] += fold
            for j in range(8):
                rows = slice(j * nv, (j + 1) * nv)
                du0 = ss[0, pl.ds(j, nv, stride=8), :]
                av, sv = z_ref[rows, lanes], _sigmoid(z_ref[rows, b_lanes])
                dz_ref[rows, lanes] = (du0 * sv).astype(bf16)
                dz_ref[rows, b_lanes] = (du0 * av * sv * (1.0 - sv)).astype(bf16)
                dz_ref[rows, v_lanes] = (ss[1, pl.ds(j, nv, stride=8), :] - dp_ref[rows, lanes]).astype(bf16)
        h, xh, r = _rms_fwd(x_ref[...], g_ref[...])
        dz = dz_ref[...]
        acc_ref[...] += _dot_tn(h.astype(bf16), dz)
        dh = _dot_nt(dz[:, 0:W_IN_CHUNK], w_ref[0])
        for j in range(1, N_CHIPS):
            dh = dh + _dot_nt(dz[:, j * W_IN_CHUNK:(j + 1) * W_IN_CHUNK], w_ref[j])
        dxb, dg = _rms_bwd(dh, xh, r, g_ref[...])
        gx_ref[...] = dx1_ref[...] + dxb
        small_ref[0:1, :] += dg

        @pl.when(i == n - 1)
        def _():
            for j in range(N_CHIPS):
                dw_ref[j] = acc_ref[:, j * W_IN_CHUNK:(j + 1) * W_IN_CHUNK].astype(bf16)
            dcw_ref[...] = jnp.sum(dcw_acc[...], axis=1)

        pl.when(i == n - 1)(cross.finish)

    nxt = lambda i: (jnp.minimum((i + 1) * hb, last), 0)
    res = pl.pallas_call(
        body, name="in_bwd", grid=(n,),
        in_specs=[_row(0, (tm, C_CONV)), pl.BlockSpec((HALO, C_CONV), nxt),
                  _row(0, (tm, C_POOL)), pl.BlockSpec((HALO, C_POOL), nxt),
                  _row(0, (tm, W_IN_COLS)),
                  pl.BlockSpec((HALO, W_IN_COLS), lambda i: (jnp.maximum(i * hb - 1, 0), 0)),
                  _row(0, (tm, D_MODEL)), _row(0, (tm, D_MODEL)), _full((HALO, C_CONV)), _full((1, D_MODEL)),
                  _full((N_CHIPS, D_MODEL, W_IN_CHUNK))] + cs["in_specs"],
        out_specs=[_row(0, (tm, D_MODEL)), _full((N_CHIPS, D_MODEL, W_IN_CHUNK)), _full((HALO, C_CONV)),
                   _full((8, D_MODEL))] + cs["out_specs"],
        out_shape=[jax.ShapeDtypeStruct((s, D_MODEL), f32), jax.ShapeDtypeStruct((N_CHIPS, D_MODEL, W_IN_CHUNK), bf16),
                   jax.ShapeDtypeStruct((HALO, C_CONV), f32), jax.ShapeDtypeStruct((8, D_MODEL), f32)] + cs["out_shape"],
        scratch_shapes=[pltpu.VMEM((4, (SHIFT_PAD + nv) * 8, 128), f32), pltpu.VMEM((4, (nv + SHIFT_PAD) * 8, 128), f32),
                        pltpu.VMEM((4, (nv + SHIFT_PAD) * 8, 128), f32), pltpu.VMEM((2, tm, 128), f32),
                        pltpu.VMEM((tm, C_CONV), f32), pltpu.VMEM((tm, W_IN_COLS), bf16),
                        pltpu.VMEM((D_MODEL, W_IN_COLS), f32), pltpu.VMEM((HALO, 8, C_CONV), f32)] + cs["scratch"],
        compiler_params=_cp("arbitrary"),
    )(du1, du1, dpo, dpo, z, z, x, dx1, conv_w, g_mix, w_in_g, *parts)
    return res[:4], res[4:]


def _rs_pair(name, grads):
    nk = len(grads)
    halves = [g.shape[1] // 2 for g in grads]

    def body(*refs):
        ins, outs = refs[:nk], refs[nk:2 * nk]
        send_sems, recv_sems = refs[2 * nk:]
        x, y, c, _, _ = _place()
        cps = [_remote(ins[k].at[:, pl.ds((1 - c) * halves[k], halves[k]), :], outs[k],
                       send_sems.at[k], recv_sems.at[k], (x, y, 1 - c)) for k in range(nk)]
        for cp in cps:
            cp.start()
        for cp in cps:
            cp.wait()

    return pl.pallas_call(
        body, name=f"rs_pair_{name}", in_specs=[ANY] * nk, out_specs=[ANY] * nk,
        out_shape=[jax.ShapeDtypeStruct((N_CHIPS, h, g.shape[2]), bf16) for g, h in zip(grads, halves)],
        scratch_shapes=[pltpu.SemaphoreType.DMA((nk,)), pltpu.SemaphoreType.DMA((nk,))],
    )(*grads)


def _rs_add_pair(name, core, grad, recv):
    _, half, cols = recv.shape

    def body(core_ref, g_ref, r_ref, o_ref):
        o_ref[...] = (g_ref[...].astype(f32) + r_ref[...].astype(f32)).astype(bf16)

    return pl.pallas_call(
        body, name=f"rs_add_pair_{name}",
        grid_spec=pltpu.PrefetchScalarGridSpec(
            num_scalar_prefetch=1, grid=(N_CHIPS,),
            in_specs=[pl.BlockSpec((1, half, cols), lambda j, core_ref: (j, core_ref[0], 0)),
                      pl.BlockSpec((1, half, cols), lambda j, core_ref: (j, 0, 0))],
            out_specs=pl.BlockSpec((1, half, cols), lambda j, core_ref: (j, 0, 0))),
        out_shape=jax.ShapeDtypeStruct(recv.shape, bf16),
        compiler_params=_cp("parallel"),
    )(core, grad, recv)


class _Cross:
    def __init__(self, parts, landed, send_sems, recv_sems):
        self.parts, self.landed, self.send_sems, self.recv_sems = parts, landed, send_sems, recv_sems
        _, _, self.c, self.me, self.others = _place()

    def _copy(self, k, j, src_chunk, dst_slot):
        ox, oy = self.others[j]
        return _remote(self.parts[k].at[src_chunk], self.landed[k].at[dst_slot], self.send_sems.at[3 * k + j],
                       self.recv_sems.at[3 * k + j], (ox, oy, self.c))

    def _each(self):
        return [(k, j, 2 * self.others[j][0] + self.others[j][1]) for k in range(len(self.parts)) for j in range(3)]

    def start(self):
        for k, j, chip in self._each():
            self._copy(k, j, chip, self.me).start()

    def finish(self):
        for k, j, chip in self._each():
            self._copy(k, j, chip, chip).wait_recv()
        for k, j, chip in self._each():
            self._copy(k, j, chip, self.me).wait_send()

    @staticmethod
    def scratch(n):
        return [pltpu.SemaphoreType.DMA((3 * n,)), pltpu.SemaphoreType.DMA((3 * n,))]


def _rs_cross(name, parts):
    nk = len(parts)

    def body(*refs):
        cross = _Cross(refs[:nk], refs[nk:2 * nk], *refs[2 * nk:])
        cross.start()
        cross.finish()

    return pl.pallas_call(
        body, name=f"rs_cross_{name}", in_specs=[ANY] * nk, out_specs=[ANY] * nk,
        out_shape=[jax.ShapeDtypeStruct(a.shape, bf16) for a in parts], scratch_shapes=_Cross.scratch(nk),
    )(*parts)


def _rs_sum_chips(name, place, landed, part):
    _, half, cols = landed.shape

    def body(place_ref, l_ref, p_ref, o_ref):
        me = place_ref[0]
        own = p_ref[0].astype(f32)
        acc = jnp.where(me == 0, own, l_ref[0].astype(f32))
        for j in range(1, N_CHIPS):
            acc = acc + jnp.where(me == j, own, l_ref[j].astype(f32))
        o_ref[...] = acc

    return pl.pallas_call(
        body, name=f"rs_sum_chips_{name}",
        grid_spec=pltpu.PrefetchScalarGridSpec(
            num_scalar_prefetch=1, grid=(1,),
            in_specs=[pl.BlockSpec((N_CHIPS, half, cols), lambda t, place_ref: (0, 0, 0)),
                      pl.BlockSpec((1, half, cols), lambda t, place_ref: (place_ref[0], 0, 0))],
            out_specs=pl.BlockSpec((half, cols), lambda t, place_ref: (place_ref[1], 0))),
        out_shape=jax.ShapeDtypeStruct((2 * half, cols), f32),
        compiler_params=_cp("arbitrary"),
    )(place, landed, part)


def _grad_pair(shards):
    nk = len(shards)

    def body(*refs):
        outs = refs[nk:2 * nk]
        send_sems, recv_sems = refs[2 * nk:]
        x, y, c, _, _ = _place()

        def half(k, core):
            h = outs[k].shape[0] // 2
            return outs[k].at[pl.ds(core * h, h), :]

        cps = [_remote(half(k, c), half(k, c), send_sems.at[k], recv_sems.at[k], (x, y, 1 - c)) for k in range(nk)]
        for cp in cps:
            cp.start()
        for k in range(nk):
            _remote(half(k, 1 - c), half(k, 1 - c), send_sems.at[k], recv_sems.at[k], (x, y, 1 - c)).wait_recv()
        for cp in cps:
            cp.wait_send()

    return pl.pallas_call(
        body, name="grad_pair", in_specs=[ANY] * nk, out_specs=[ANY] * nk,
        out_shape=[jax.ShapeDtypeStruct(a.shape, f32) for a in shards],
        input_output_aliases={k: k for k in range(nk)},
        scratch_shapes=[pltpu.SemaphoreType.DMA((nk,)), pltpu.SemaphoreType.DMA((nk,))],
    )(*shards)


def _adam_math(w, g, m, v):
    m = ADAM_B1 * m + (1.0 - ADAM_B1) * g
    v = ADAM_B2 * v + (1.0 - ADAM_B2) * (g * g)
    m_hat = m / (1.0 - ADAM_B1 ** ADAM_STEP)
    v_hat = v / (1.0 - ADAM_B2 ** ADAM_STEP)
    delta = -ADAM_LR * (m_hat / (jnp.sqrt(v_hat) + ADAM_EPS) + ADAM_WD * w)
    return delta, m, v


def _adam(k, w, g, m, v):
    rows, cols = w.shape
    tr = next(rows // d for d in (1, 2, 4, 8) if rows % (8 * d) == 0 and rows // d <= 256)

    def body(w_ref, g_ref, m_ref, v_ref, d_ref, nm_ref, nv_ref):
        d_ref[...], nm_ref[...], nv_ref[...] = _adam_math(w_ref[...], g_ref[...], m_ref[...], v_ref[...])

    spec = _row(0, (tr, cols))
    return pl.pallas_call(
        body, name=f"adam_{k}", grid=(rows // tr,), in_specs=[spec] * 4, out_specs=[spec] * 3,
        out_shape=[jax.ShapeDtypeStruct(w.shape, f32)] * 3,
        compiler_params=_cp("parallel"),
    )(w, g, m, v)


def _small_sync_adam(part, w, m, v):
    rows = part.shape[0]

    def body(p_ref, w_ref, m_ref, v_ref, g_ref, d_ref, nm_ref, nv_ref, buf, send_sems, recv_sems):
        x, y, c, _, others = _place()
        me, sibling = (x, y, c), (x, y, 1 - c)

        def slot(px, py, pc):
            return buf.at[4 * px + 2 * py + pc]

        def copy(k, block, to, src=None):
            return _remote(slot(*block) if src is None else src, slot(*block), send_sems.at[k], recv_sems.at[k], to)

        buf[4 * x + 2 * y + c] = p_ref[...]
        first = [copy(0, me, sibling, src=p_ref)]
        first += [copy(1 + j, me, (*chip, c), src=p_ref) for j, chip in enumerate(others)]
        for cp in first:
            cp.start()
        passed = [copy(4 + j, (*chip, c), sibling) for j, chip in enumerate(others)]
        for j, chip in enumerate(others):
            copy(1 + j, (*chip, c), me).wait_recv()
            passed[j].start()
        copy(0, sibling, me).wait_recv()
        for j, chip in enumerate(others):
            copy(4 + j, (*chip, 1 - c), me).wait_recv()
        for cp in first + passed:
            cp.wait_send()
        g = buf[0]
        for dev in range(1, N_DEV):
            g = g + buf[dev]
        g_ref[...] = g
        d_ref[...], nm_ref[...], nv_ref[...] = _adam_math(w_ref[...], g, m_ref[...], v_ref[...])

    shape = jax.ShapeDtypeStruct(part.shape, f32)
    return pl.pallas_call(
        body, name="small_sync_adam", in_specs=[VMEM] * 4, out_specs=[VMEM] * 4, out_shape=[shape] * 4,
        scratch_shapes=[pltpu.VMEM((N_DEV, rows, 128), f32), pltpu.SemaphoreType.DMA((7,)),
                        pltpu.SemaphoreType.DMA((7,))],
        compiler_params=pltpu.CompilerParams(vmem_limit_bytes=VMEM_LIMIT),
    )(part, w, m, v)


def _rows128(a):
    return a.reshape(-1, 128)


def _pad_rows(a, rows):
    return jnp.concatenate([a, jnp.zeros((rows - a.shape[0],) + a.shape[1:], a.dtype)], axis=0)


def _pack_small(me, g_pg, g_post, g_final, g_ffn, ln_g, ln_b, conv_b, pool_scale, pool_w, conv_w, g_mix):
    blk_ple = _pad_rows(jnp.concatenate([g_pg, g_post, g_final.reshape(1, D_MODEL)], axis=0), 8)
    blk_ffn = _pad_rows(g_ffn, 8)
    blk_mix = _pad_rows(jnp.concatenate([ln_g, ln_b, conv_b, pool_scale], axis=0), 8)
    cw = lax.dynamic_update_slice(jnp.zeros((N_CHIPS, HALO, 128), f32), _pad_rows(conv_w, HALO)[None], (me, 0, 0))
    blk_in = _pad_rows(g_mix, 8)
    return jnp.concatenate([_rows128(blk_ple), _rows128(blk_ffn), _rows128(blk_mix), _rows128(pool_w),
                            _rows128(cw), _rows128(blk_in)], axis=0)


def _unpack_small(me, packed):
    o = 0

    def take(rows):
        nonlocal o
        blk = packed[o:o + rows]
        o += rows
        return blk

    ple = take(64).reshape(8, D_MODEL)
    ffn = take(64).reshape(8, D_MODEL)
    mixb = take(32).reshape(8, C_CONV)
    pool_w = take(512).reshape(1, 4, POOL_GROUP, POOL_GROUP)
    cw = take(N_CHIPS * HALO).reshape(N_CHIPS, HALO, 128)
    inb = take(64).reshape(8, D_MODEL)
    conv_w = lax.dynamic_slice(cw, (me, 0, 0), (1, HALO, 128))[:, :CONV_K, :]
    return dict(g_ple_gate=ple[0:1], g_ple_post=ple[1:2], g_final=ple[2], loss=ple[3, 0], g_ffn=ffn[0:1],
                ln_g=mixb[0:1], ln_b=mixb[1:2], conv_b=mixb[2:3], pool_scale=mixb[3:4], pool_w=pool_w,
                conv_w=conv_w, g_mix=inb[0:1])


def kernel(x, p, g_mix, w_in, conv_w, conv_b, ln_g, ln_b, pool_w, pool_scale, w_out, g_ffn, w_gate_up, w_down, g_ple_gate, w_ple_gate, w_ple_up, g_ple_post, g_final, loss_target, m_g_mix, m_w_in, m_conv_w, m_conv_b, m_ln_g, m_ln_b, m_pool_w, m_pool_scale, m_w_out, m_g_ffn, m_w_gate_up, m_w_down, m_g_ple_gate, m_w_ple_gate, m_w_ple_up, m_g_ple_post, m_g_final, v_g_mix, v_w_in, v_conv_w, v_conv_b, v_ln_g, v_ln_b, v_pool_w, v_pool_scale, v_w_out, v_g_ffn, v_w_gate_up, v_w_down, v_g_ple_gate, v_w_ple_gate, v_w_ple_up, v_g_ple_post, v_g_final):
    seq = x.shape[1]
    me = 2 * lax.axis_index("x") + lax.axis_index("y")
    chip = me.astype(jnp.int32).reshape(1)
    core = lax.axis_index("c").astype(jnp.int32).reshape(1)
    place = jnp.concatenate([chip, core])
    xs, ps, ts = x.reshape(seq, D_MODEL), p.reshape(seq, D_PLE), loss_target.reshape(seq, D_MODEL)

    big_names = ["w_in", "w_gu", "w_out", "w_down", "w_pg", "w_pu"]
    big = [w_in[0], w_gate_up[0], w_out[0], w_down[0], w_ple_gate[0], w_ple_up[0]]
    big_m = [m_w_in[0], m_w_gate_up[0], m_w_out[0], m_w_down[0], m_w_ple_gate[0], m_w_ple_up[0]]
    big_v = [v_w_in[0], v_w_gate_up[0], v_w_out[0], v_w_down[0], v_w_ple_gate[0], v_w_ple_up[0]]
    b_in, b_gu, b_out, b_down, b_pg, b_pu = [_cast_into_slot(nm, chip, w, bf16) for nm, w in zip(big_names, big)]
    b_cw = _cast_into_slot("conv_w", chip, _pad_rows(conv_w[0], HALO), f32)
    xi, yi = lax.axis_index("x"), lax.axis_index("y")
    order = jnp.stack([me, 2 * (1 - xi) + yi, 2 * xi + 1 - yi, 2 * (1 - xi) + 1 - yi]).astype(jnp.int32)

    z, (w_in_g, cw_g, w_out_g) = _mix_in(xs, g_mix, order, [b_in, b_cw, b_out])
    conv_w_f = cw_g.transpose(1, 0, 2).reshape(HALO, C_CONV)
    w_out_f = w_out_g.reshape(D_MODEL, D_MODEL)
    (x1, mix, u1, pooled), (w_gu_g, w_down_g) = _conv_pool_out(z, xs, conv_w_f, conv_b, ln_g, ln_b, pool_w[0],
                                                                pool_scale, w_out_f, [b_gu, b_down])
    w_down_f = w_down_g.reshape(D_FF, D_MODEL)
    (x2, h2, gu, ffn_f), (w_pg_g, w_pu_g) = _ffn_fwd(x1, g_ffn, w_gu_g, w_down_f, [b_pg, b_pu])
    w_pg_f = w_pg_g.reshape(D_MODEL, D_MODEL)
    dx2, d_w_pg, d_w_pu, small_ple = _ple_loss(x2, ps, ts, g_ple_gate, g_ple_post, g_final.reshape(1, D_MODEL),
                                               w_pg_f, w_pu_g)
    def pair_reduce(group, names, grads):
        got = _rs_pair(group, grads)
        return [_rs_add_pair(nm, core, g, r) for nm, g, r in zip(names, grads, got)]

    d_w_down = _ffn_bwd_dw_down(ffn_f, dx2)
    parts_a = pair_reduce("a", ["w_pg", "w_pu", "w_down"],
                          [d_w_pg.reshape(N_CHIPS, -1, D_MODEL), d_w_pu, d_w_down.reshape(N_CHIPS, -1, D_MODEL)])
    (dx1, dgu, small_ffn), landed_a = _ffn_bwd_dx(dx2, x1, gu, g_ffn, w_gu_g, w_down_f, parts_a)
    d_w_gu = _ffn_bwd_dw_gu(h2, dgu)
    du1, dpo, d_w_out, d_pool_w, small_mix = _mix_bwd_local(dx1, mix, u1, pooled, w_out_f, ln_g, ln_b, pool_w[0],
                                                             pool_scale)
    parts_b = pair_reduce("b", ["w_gu", "w_out"], [d_w_gu, d_w_out.reshape(N_CHIPS, -1, D_MODEL)])
    (grad_x, d_w_in, d_conv_w, small_in), landed_b = _in_bwd(du1, dpo, z, xs, dx1, conv_w_f, g_mix, w_in_g, parts_b)
    parts_c = pair_reduce("c", ["w_in"], [d_w_in])
    landed_c = _rs_cross("c", parts_c)
    parts = [parts_c[0], parts_b[0], parts_b[1], parts_a[2], parts_a[0], parts_a[1]]
    landed = [landed_c[0], landed_b[0], landed_b[1], landed_a[2], landed_a[0], landed_a[1]]
    halves = [_rs_sum_chips(nm, place, a, b) for nm, a, b in zip(big_names, landed, parts)]
    big_g = _grad_pair(halves)
    big_upd = [_adam(nm, w, g, m, v) for nm, w, g, m, v in zip(big_names, big, big_g, big_m, big_v)]

    cw_chunks = d_conv_w.reshape(HALO, N_CHIPS, 128).transpose(1, 0, 2)
    part = jnp.concatenate([_rows128(small_ple), _rows128(small_ffn), _rows128(small_mix), _rows128(d_pool_w),
                            _rows128(cw_chunks), _rows128(small_in)], axis=0)
    sw = _pack_small(me, g_ple_gate, g_ple_post, g_final, g_ffn, ln_g, ln_b, conv_b, pool_scale, pool_w, conv_w[0], g_mix)
    sm = _pack_small(me, m_g_ple_gate, m_g_ple_post, m_g_final, m_g_ffn, m_ln_g, m_ln_b, m_conv_b, m_pool_scale,
                     m_pool_w, m_conv_w[0], m_g_mix)
    sv = _pack_small(me, v_g_ple_gate, v_g_ple_post, v_g_final, v_g_ffn, v_ln_g, v_ln_b, v_conv_b, v_pool_scale,
                     v_pool_w, v_conv_w[0], v_g_mix)
    small = [_unpack_small(me, a) for a in _small_sync_adam(part, sw, sm, sv)]

    names = ["g_mix", "w_in", "conv_w", "conv_b", "ln_g", "ln_b", "pool_w", "pool_scale", "w_out", "g_ffn",
             "w_gate_up", "w_down", "g_ple_gate", "w_ple_gate", "w_ple_up", "g_ple_post", "g_final"]
    big_at = {"w_in": 0, "w_gate_up": 1, "w_out": 2, "w_down": 3, "w_ple_gate": 4, "w_ple_up": 5}
    out = [small[0]["loss"], grad_x.reshape(1, seq, D_MODEL)]
    for kind in range(4):
        for nm in names:
            if nm in big_at:
                k = big_at[nm]
                out.append((big_g[k] if kind == 0 else big_upd[k][kind - 1])[None])
            else:
                out.append(small[kind][nm])
    return tuple(out)
```

```python
import functools

import jax
import jax.numpy as jnp
from jax import lax
from jax.experimental import pallas as pl
from jax.experimental.pallas import tpu as pltpu

f32, bf16 = jnp.float32, jnp.bfloat16

EPS = 1e-6
D_MODEL = 1024
C_CONV = 512
C_POOL = 512
POOL_WINDOWS = (2, 4, 8, 16)
POOL_GROUP = 128
CONV_K = 31
D_FF = 2816
D_PLE = 256
N_CHIPS = 4
N_DEV = 8
W_IN_COLS = 2 * C_CONV + C_POOL
W_IN_CHUNK = W_IN_COLS // N_CHIPS
FF_CHUNK = 2 * D_FF // N_CHIPS
PLE_CHUNK = D_MODEL // N_CHIPS
HALO = 32
ROW_TILE = 512
CONV_ROWS = 64
CONV_COLS = (slice(0, 256), slice(256, 512))
SHIFT_PAD = 32
SHIFT_GROUPS = 16
FF_SUB = (0, 512, 1024, FF_CHUNK)
VMEM_LIMIT = 56 * 1024 * 1024

ADAM_LR = 0.001
ADAM_B1 = 0.9
ADAM_B2 = 0.999
ADAM_EPS = 1e-08
ADAM_WD = 0.01
ADAM_STEP = 10

MESH = pl.DeviceIdType.MESH
ANY = pl.BlockSpec(memory_space=pl.ANY)
VMEM = pl.BlockSpec(memory_space=pltpu.VMEM)


def _cp(*sem):
    return pltpu.CompilerParams(dimension_semantics=sem, vmem_limit_bytes=VMEM_LIMIT)


def _dot(a, b):
    return jnp.dot(a, b, preferred_element_type=f32)


def _dot_nt(a, b):
    return lax.dot_general(a, b, (((1,), (1,)), ((), ())), preferred_element_type=f32)


def _dot_tn(a, b):
    return lax.dot_general(a, b, (((0,), (0,)), ((), ())), preferred_element_type=f32)


def _sigmoid(v):
    return jax.nn.sigmoid(v)


def _rms_fwd(v, g):
    r = lax.rsqrt(jnp.mean(v * v, axis=-1, keepdims=True) + EPS)
    vh = v * r
    return vh * g, vh, r


def _rms_bwd(dy, vh, r, g):
    dvh = dy * g
    dv = r * (dvh - vh * jnp.mean(dvh * vh, axis=-1, keepdims=True))
    return dv, jnp.sum(dy * vh, axis=0, keepdims=True)


def _silu_grad(v, s):
    return s * (1.0 + v * (1.0 - s))


def _row(i, n):
    return pl.BlockSpec((n[0], n[1]), lambda *a: (a[i], 0))


def _full(shape):
    nd = len(shape)
    return pl.BlockSpec(shape, lambda *a: (0,) * nd)


def _place():
    x, y, c = lax.axis_index("x"), lax.axis_index("y"), lax.axis_index("c")
    others = [(1 - x, y), (x, 1 - y), (1 - x, 1 - y)]
    return x, y, c, 2 * x + y, others


def _remote(src, dst, send_sem, recv_sem, dev):
    return pltpu.make_async_remote_copy(src_ref=src, dst_ref=dst, send_sem=send_sem, recv_sem=recv_sem,
                                        device_id=dev, device_id_type=MESH)


def _cast_into_slot(name, me, w, dtype):
    rows, cols = w.shape
    tr = next(rows // d for d in (1, 2, 4, 8) if rows % (16 * d) == 0 and rows // d <= 512)

    def body(me_ref, w_ref, o_ref):
        o_ref[0] = w_ref[...].astype(dtype)

    return pl.pallas_call(
        body, name=f"cast_{name}",
        grid_spec=pltpu.PrefetchScalarGridSpec(
            num_scalar_prefetch=1, grid=(rows // tr,),
            in_specs=[pl.BlockSpec((tr, cols), lambda r, me_ref: (r, 0))],
            out_specs=pl.BlockSpec((1, tr, cols), lambda r, me_ref: (me_ref[0], r, 0))),
        out_shape=jax.ShapeDtypeStruct((N_CHIPS, rows, cols), dtype),
        compiler_params=_cp("parallel"),
    )(me, w)


class _Gather:
    def __init__(self, bufs, send_sems, recv_sems):
        self.bufs, self.send_sems, self.recv_sems = bufs, send_sems, recv_sems
        self.x, self.y, self.c, self.me, self.others = _place()
        self.halves = [b.shape[1] // 2 for b in bufs]

    def _piece(self, k, chip, half):
        return self.bufs[k].at[chip, pl.ds(half * self.halves[k], self.halves[k]), :]

    def _ici(self, k, j, chip):
        ox, oy = self.others[j]
        piece = self._piece(k, chip, self.c)
        return _remote(piece, piece, self.send_sems.at[6 * k + j], self.recv_sems.at[6 * k + j], (ox, oy, self.c))

    def _pair(self, k, j, half):
        ox, oy = self.others[j]
        piece = self._piece(k, 2 * ox + oy, half)
        return _remote(piece, piece, self.send_sems.at[6 * k + 3 + j], self.recv_sems.at[6 * k + 3 + j],
                       (self.x, self.y, 1 - self.c))

    def _each(self, ks=None):
        return [(k, j) for k in (range(len(self.bufs)) if ks is None else ks) for j in range(3)]

    def chip(self, j):
        ox, oy = self.others[j]
        return 2 * ox + oy

    def start(self):
        for k, j in self._each():
            self._ici(k, j, self.me).start()

    def forward(self, pairs=None):
        for k, j in self._each() if pairs is None else pairs:
            self._ici(k, j, self.chip(j)).wait_recv()
            self._pair(k, j, self.c).start()

    def landed(self, pairs):
        for k, j in pairs:
            self._pair(k, j, 1 - self.c).wait_recv()

    def finish(self, ks=None):
        self.landed(self._each(ks))
        for k, j in self._each():
            self._ici(k, j, self.me).wait_send()
            self._pair(k, j, self.c).wait_send()

    @staticmethod
    def scratch(n):
        return [pltpu.SemaphoreType.DMA((6 * n,)), pltpu.SemaphoreType.DMA((6 * n,))]


def _carried(bufs):
    n = len(bufs)
    return dict(in_specs=[ANY] * n, out_specs=[ANY] * n,
                out_shape=[jax.ShapeDtypeStruct(b.shape, b.dtype) for b in bufs], scratch=_Gather.scratch(n))


def _mix_in(x, g_mix, order, carry):
    s = x.shape[0]
    tm = min(2 * ROW_TILE, s)
    n = s // tm
    nc = len(carry)
    cs = _carried(carry)

    def body(order_ref, x_ref, g_ref, *refs):
        z_ref = refs[nc]
        bufs = refs[nc + 1:2 * nc + 1]
        h_ref, w_ref, w_sem = refs[2 * nc + 1:2 * nc + 4]
        gather = _Gather(bufs, *refs[2 * nc + 4:])
        q, i = pl.program_id(0), pl.program_id(1)
        first = i == 0
        pl.when(jnp.logical_and(q == 0, first))(gather.start)
        for j in range(3):

            @pl.when(jnp.logical_and(q == j + 1, first))
            def _():
                gather.forward([(0, j)])
                gather.landed([(0, j)])

        @pl.when(first)
        def _():
            load = pltpu.make_async_copy(bufs[0].at[order_ref[q]], w_ref, w_sem)
            load.start()
            load.wait()

        @pl.when(q == 0)
        def _():
            h, _, _ = _rms_fwd(x_ref[...], g_ref[...])
            h_ref[i] = h.astype(bf16)

        z_ref[...] = _dot(h_ref[i], w_ref[...])

        @pl.when(jnp.logical_and(q == N_CHIPS - 1, i == n - 1))
        def _():
            rest = gather._each(range(1, nc))
            gather.forward(rest)
            gather.finish(range(1, nc))

    res = pl.pallas_call(
        body, name="mix_in",
        grid_spec=pltpu.PrefetchScalarGridSpec(
            num_scalar_prefetch=1, grid=(N_CHIPS, n),
            in_specs=[pl.BlockSpec((tm, D_MODEL), lambda q, i, order_ref: (jnp.where(q == 0, i, 0), 0)),
                      pl.BlockSpec((1, D_MODEL), lambda q, i, order_ref: (0, 0))] + cs["in_specs"],
            out_specs=[pl.BlockSpec((tm, W_IN_CHUNK), lambda q, i, order_ref: (i, order_ref[q]))] + cs["out_specs"],
            scratch_shapes=[pltpu.VMEM((n, tm, D_MODEL), bf16), pltpu.VMEM((D_MODEL, W_IN_CHUNK), bf16),
                            pltpu.SemaphoreType.DMA(())] + cs["scratch"]),
        out_shape=[jax.ShapeDtypeStruct((s, W_IN_COLS), f32)] + cs["out_shape"],
        input_output_aliases={3 + k: 1 + k for k in range(nc)},
        compiler_params=_cp("arbitrary", "arbitrary"),
    )(order, x, g_mix, *carry)
    return res[0], res[1:]


def _tap_offsets(lo, hi):
    groups = [[o for o in range(lo, hi + 1) if o % 8 == s] for s in range(8)]
    return [g for g in groups if g]


def _tap_sum(buf, w_ref, row0, cols, tap_of):
    acc = jnp.zeros((CONV_ROWS, cols.stop - cols.start), f32)
    for offs in _tap_offsets(0, CONV_K - 1):
        slab = buf[pl.ds(row0 + offs[0], offs[-1] - offs[0] + CONV_ROWS), cols]
        for o in offs:
            acc = acc + w_ref[pl.ds(tap_of(o), 1), cols] * slab[o - offs[0]:o - offs[0] + CONV_ROWS]
    return acc


def _pool_counts(tm, w, first_row):
    t1 = (lax.broadcasted_iota(jnp.int32, (tm, 1), 0) + first_row + 1).astype(f32)
    return jnp.minimum(t1, float(w))


def _conv_pool_out(z, x, conv_w, conv_b, ln_g, ln_b, pool_w, pool_scale, w_out, carry):
    s = x.shape[0]
    tm = min(ROW_TILE, s)
    n = s // tm
    hb = tm // HALO
    nc = len(carry)
    cs = _carried(carry)

    def body(z_ref, zp_ref, x_ref, cw_ref, cb_ref, lg_ref, lb_ref, pw_ref, ps_ref, wo_ref, *refs):
        x1_ref, mix_ref, u1_ref, pooled_ref = refs[nc:nc + 4]
        ubuf, vbuf = refs[2 * nc + 4:2 * nc + 6]
        gather = _Gather(refs[nc + 4:2 * nc + 4], *refs[2 * nc + 6:])
        i = pl.program_id(0)
        pl.when(i == 0)(gather.start)
        keep = (i > 0).astype(f32)
        zp = zp_ref[...] * keep
        ubuf[0:HALO, :] = zp[:, :C_CONV] * _sigmoid(zp[:, C_CONV:2 * C_CONV])
        vbuf[0:HALO, :] = zp[:, 2 * C_CONV:]
        ubuf[HALO:, :] = z_ref[:, :C_CONV] * _sigmoid(z_ref[:, C_CONV:2 * C_CONV])
        vbuf[HALO:, :] = z_ref[:, 2 * C_CONV:]
        off = HALO - (CONV_K - 1)
        for r0 in range(0, tm, CONV_ROWS):
            for cols in CONV_COLS:
                u1_ref[r0:r0 + CONV_ROWS, cols] = cb_ref[:, cols] + _tap_sum(ubuf, cw_ref, r0 + off, cols, lambda o: o)
        u1 = u1_ref[...]
        mu = jnp.mean(u1, axis=-1, keepdims=True)
        uc = u1 - mu
        rstd = lax.rsqrt(jnp.mean(uc * uc, axis=-1, keepdims=True) + EPS)
        u2 = uc * rstd * lg_ref[...] + lb_ref[...]
        mix_ref[:, :C_CONV] = (u2 * _sigmoid(u2)).astype(bf16)
        for g, w in enumerate(POOL_WINDOWS):
            cols = slice(g * POOL_GROUP, (g + 1) * POOL_GROUP)
            acc = vbuf[pl.ds(HALO, tm), cols]
            vg = acc
            for d in range(1, w):
                acc = acc + vbuf[pl.ds(HALO - d, tm), cols]
            pooled = (acc / _pool_counts(tm, w, i * tm) - vg).astype(bf16)
            pooled_ref[:, cols] = pooled
            mixed = _dot(pooled, pw_ref[g].astype(bf16))
            mix_ref[:, C_CONV + g * POOL_GROUP:C_CONV + (g + 1) * POOL_GROUP] = (mixed * ps_ref[:, cols]).astype(bf16)
        x1_ref[...] = x_ref[...] + _dot(mix_ref[...], wo_ref[...])
        @pl.when(i == n - 1)
        def _():
            gather.forward()
            gather.finish()

    res = pl.pallas_call(
        body, name="conv_pool_out", grid=(n,),
        in_specs=[_row(0, (tm, W_IN_COLS)),
                  pl.BlockSpec((HALO, W_IN_COLS), lambda i: (jnp.maximum(i * hb - 1, 0), 0)),
                  _row(0, (tm, D_MODEL)), _full((HALO, C_CONV)), _full((1, C_CONV)), _full((1, C_CONV)),
                  _full((1, C_CONV)), _full((4, POOL_GROUP, POOL_GROUP)), _full((1, C_POOL)),
                  _full((D_MODEL, D_MODEL))] + cs["in_specs"],
        out_specs=[_row(0, (tm, D_MODEL)), _row(0, (tm, D_MODEL)), _row(0, (tm, C_CONV)), _row(0, (tm, C_POOL))]
        + cs["out_specs"],
        out_shape=[jax.ShapeDtypeStruct((s, D_MODEL), f32), jax.ShapeDtypeStruct((s, D_MODEL), bf16),
                   jax.ShapeDtypeStruct((s, C_CONV), f32), jax.ShapeDtypeStruct((s, C_POOL), bf16)] + cs["out_shape"],
        input_output_aliases={10 + k: 4 + k for k in range(nc)},
        scratch_shapes=[pltpu.VMEM((HALO + tm, C_CONV), f32), pltpu.VMEM((HALO + tm, C_POOL), f32)] + cs["scratch"],
        compiler_params=_cp("arbitrary"),
    )(z, z, x, conv_w, conv_b, ln_g, ln_b, pool_w, pool_scale, w_out, *carry)
    return res[:4], res[4:]


def _ffn_fwd(x1, g_ffn, w_gu_g, w_down, carry):
    s = x1.shape[0]
    tm = min(ROW_TILE, s)
    n = s // tm
    nc = len(carry)
    cs = _carried(carry)

    def body(x1_ref, g_ref, wg_ref, wu_ref, wd_ref, *refs):
        x2_ref, h2_ref, gu_ref, f_ref = refs[nc:nc + 4]
        acc_ref = refs[2 * nc + 4]
        gather = _Gather(refs[nc + 4:2 * nc + 4], *refs[2 * nc + 5:])
        i, c = pl.program_id(0), pl.program_id(1)
        pl.when(jnp.logical_and(i == 0, c == 0))(gather.start)
        pl.when(jnp.logical_and(i == n - 1, c == 0))(gather.forward)

        @pl.when(c == 0)
        def _():
            h, _, _ = _rms_fwd(x1_ref[...], g_ref[...])
            h2_ref[...] = h.astype(bf16)
            acc_ref[...] = jnp.zeros_like(acc_ref)

        h = h2_ref[...]
        for lo, hi in zip(FF_SUB[:-1], FF_SUB[1:]):
            gate = _dot(h, wg_ref[0, :, lo:hi])
            up = _dot(h, wu_ref[0, :, lo:hi])
            gu_ref[0, :, lo:hi] = gate.astype(bf16)
            gu_ref[1, :, lo:hi] = up.astype(bf16)
            f = (gate * _sigmoid(gate) * up).astype(bf16)
            f_ref[:, lo:hi] = f
            acc_ref[...] += _dot(f, wd_ref[lo:hi, :])

        @pl.when(c == 1)
        def _():
            x2_ref[...] = x1_ref[...] + acc_ref[...]

        pl.when(jnp.logical_and(i == n - 1, c == 1))(gather.finish)

    res = pl.pallas_call(
        body, name="ffn_fwd", grid=(n, 2),
        in_specs=[_row(0, (tm, D_MODEL)), _full((1, D_MODEL)),
                  pl.BlockSpec((1, D_MODEL, FF_CHUNK), lambda i, c: (c, 0, 0)),
                  pl.BlockSpec((1, D_MODEL, FF_CHUNK), lambda i, c: (2 + c, 0, 0)),
                  pl.BlockSpec((FF_CHUNK, D_MODEL), lambda i, c: (c, 0))] + cs["in_specs"],
        out_specs=[_row(0, (tm, D_MODEL)), _row(0, (tm, D_MODEL)),
                   pl.BlockSpec((2, tm, FF_CHUNK), lambda i, c: (0, i, c)),
                   pl.BlockSpec((tm, FF_CHUNK), lambda i, c: (i, c))] + cs["out_specs"],
        out_shape=[jax.ShapeDtypeStruct((s, D_MODEL), f32), jax.ShapeDtypeStruct((s, D_MODEL), bf16),
                   jax.ShapeDtypeStruct((2, s, D_FF), bf16), jax.ShapeDtypeStruct((s, D_FF), bf16)] + cs["out_shape"],
        input_output_aliases={5 + k: 4 + k for k in range(nc)},
        scratch_shapes=[pltpu.VMEM((tm, D_MODEL), f32)] + cs["scratch"],
        compiler_params=_cp("arbitrary", "arbitrary"),
    )(x1, g_ffn, w_gu_g, w_gu_g, w_down, *carry)
    return res[:4], res[4:]


def _ple_loss(x2, p, target, g_pg, g_post, g_final, w_pg, w_pu_g):
    s = x2.shape[0]
    tm = min(ROW_TILE, s)
    n = s // tm

    def body(x2_ref, p_ref, t_ref, gpg_ref, gpo_ref, gf_ref, wpg_ref, wpu_ref,
             dx2_ref, dwpg_ref, dwpu_ref, small_ref, apg_ref, apu_ref):
        i = pl.program_id(0)

        @pl.when(i == 0)
        def _():
            apg_ref[...] = jnp.zeros_like(apg_ref)
            apu_ref[...] = jnp.zeros_like(apu_ref)
            small_ref[...] = jnp.zeros_like(small_ref)

        x2 = x2_ref[...]
        h3, x2h, r2 = _rms_fwd(x2, gpg_ref[...])
        h3b = h3.astype(bf16)
        gate = _sigmoid(_dot(h3b, wpg_ref[...]))
        pb = p_ref[...].astype(bf16)
        pe = jnp.concatenate([_dot(pb, wpu_ref[j]) for j in range(N_CHIPS)], axis=-1)
        e, peh, rp = _rms_fwd(pe, gpo_ref[...])
        x3 = x2 + gate * e
        y, x3h, r3 = _rms_fwd(x3, gf_ref[...])
        d = y - t_ref[...]
        loss = 0.5 * jnp.sum(jnp.sum(d * d, axis=-1, keepdims=True) * (1.0 / D_MODEL), axis=0, keepdims=True)
        dx3, dgf = _rms_bwd(d * (1.0 / D_MODEL), x3h, r3, gf_ref[...])
        dpe, dgpo = _rms_bwd(dx3 * gate, peh, rp, gpo_ref[...])
        dgl = (dx3 * e * gate * (1.0 - gate)).astype(bf16)
        apg_ref[...] += _dot_tn(h3b, dgl)
        apu_ref[...] += _dot_tn(pb, dpe.astype(bf16))
        dh3 = _dot_nt(dgl, wpg_ref[...])
        dx2b, dgpg = _rms_bwd(dh3, x2h, r2, gpg_ref[...])
        dx2_ref[...] = dx3 + dx2b
        small_ref[0:1, :] += dgpg
        small_ref[1:2, :] += dgpo
        small_ref[2:3, :] += dgf
        small_ref[3:4, :] += jnp.broadcast_to(loss, (1, D_MODEL))

        @pl.when(i == n - 1)
        def _():
            dwpg_ref[...] = apg_ref[...].astype(bf16)
            for j in range(N_CHIPS):
                dwpu_ref[j] = apu_ref[:, j * PLE_CHUNK:(j + 1) * PLE_CHUNK].astype(bf16)

    return pl.pallas_call(
        body, name="ple_loss", grid=(n,),
        in_specs=[_row(0, (tm, D_MODEL)), _row(0, (tm, D_PLE)), _row(0, (tm, D_MODEL)),
                  _full((1, D_MODEL)), _full((1, D_MODEL)), _full((1, D_MODEL)),
                  _full((D_MODEL, D_MODEL)), _full((N_CHIPS, D_PLE, PLE_CHUNK))],
        out_specs=[_row(0, (tm, D_MODEL)), _full((D_MODEL, D_MODEL)), _full((N_CHIPS, D_PLE, PLE_CHUNK)),
                   _full((8, D_MODEL))],
        out_shape=[jax.ShapeDtypeStruct((s, D_MODEL), f32), jax.ShapeDtypeStruct((D_MODEL, D_MODEL), bf16),
                   jax.ShapeDtypeStruct((N_CHIPS, D_PLE, PLE_CHUNK), bf16), jax.ShapeDtypeStruct((8, D_MODEL), f32)],
        scratch_shapes=[pltpu.VMEM((D_MODEL, D_MODEL), f32), pltpu.VMEM((D_PLE, D_MODEL), f32)],
        compiler_params=_cp("arbitrary"),
    )(x2, p, target, g_pg, g_post, g_final, w_pg, w_pu_g)


def _crossed(parts):
    n = len(parts)
    return dict(in_specs=[ANY] * n, out_specs=[ANY] * n,
                out_shape=[jax.ShapeDtypeStruct(a.shape, a.dtype) for a in parts], scratch=_Cross.scratch(n))


def _ffn_bwd_dx(dx2, x1, gu, g_ffn, w_gu_g, w_down, parts):
    s = x1.shape[0]
    tm = min(ROW_TILE, s)
    n = s // tm
    nc = len(parts)
    cs = _crossed(parts)

    def body(dx2_ref, x1_ref, gu_ref, g_ref, wg_ref, wu_ref, wd_ref, *refs):
        dx1_ref, dgu_ref, small_ref = refs[nc:nc + 3]
        acc_ref = refs[2 * nc + 3]
        cross = _Cross(refs[:nc], refs[nc + 3:2 * nc + 3], *refs[2 * nc + 4:])
        i, c = pl.program_id(0), pl.program_id(1)
        pl.when(jnp.logical_and(i == 0, c == 0))(cross.start)

        @pl.when(jnp.logical_and(i == 0, c == 0))
        def _():
            small_ref[...] = jnp.zeros_like(small_ref)

        @pl.when(c == 0)
        def _():
            acc_ref[...] = jnp.zeros_like(acc_ref)

        dyb = dx2_ref[...].astype(bf16)
        for lo, hi in zip(FF_SUB[:-1], FF_SUB[1:]):
            df = _dot_nt(dyb, wd_ref[lo:hi, :])
            gate = gu_ref[0, :, lo:hi].astype(f32)
            up = gu_ref[1, :, lo:hi].astype(f32)
            sg = _sigmoid(gate)
            dgate = (df * up * _silu_grad(gate, sg)).astype(bf16)
            dup = (df * gate * sg).astype(bf16)
            dgu_ref[0, :, lo:hi] = dgate
            dgu_ref[1, :, lo:hi] = dup
            acc_ref[...] += _dot_nt(dgate, wg_ref[0, :, lo:hi]) + _dot_nt(dup, wu_ref[0, :, lo:hi])

        @pl.when(c == 1)
        def _():
            _, x1h, r1 = _rms_fwd(x1_ref[...], g_ref[...])
            dx1b, dg = _rms_bwd(acc_ref[...], x1h, r1, g_ref[...])
            dx1_ref[...] = dx2_ref[...] + dx1b
            small_ref[0:1, :] += dg

        pl.when(jnp.logical_and(i == n - 1, c == 1))(cross.finish)

    res = pl.pallas_call(
        body, name="ffn_bwd_dx", grid=(n, 2),
        in_specs=[_row(0, (tm, D_MODEL)), _row(0, (tm, D_MODEL)),
                  pl.BlockSpec((2, tm, FF_CHUNK), lambda i, c: (0, i, c)), _full((1, D_MODEL)),
                  pl.BlockSpec((1, D_MODEL, FF_CHUNK), lambda i, c: (c, 0, 0)),
                  pl.BlockSpec((1, D_MODEL, FF_CHUNK), lambda i, c: (2 + c, 0, 0)),
                  pl.BlockSpec((FF_CHUNK, D_MODEL), lambda i, c: (c, 0))] + cs["in_specs"],
        out_specs=[_row(0, (tm, D_MODEL)), pl.BlockSpec((2, tm, FF_CHUNK), lambda i, c: (0, i, c)),
                   _full((8, D_MODEL))] + cs["out_specs"],
        out_shape=[jax.ShapeDtypeStruct((s, D_MODEL), f32), jax.ShapeDtypeStruct((2, s, D_FF), bf16),
                   jax.ShapeDtypeStruct((8, D_MODEL), f32)] + cs["out_shape"],
        scratch_shapes=[pltpu.VMEM((tm, D_MODEL), f32)] + cs["scratch"],
        compiler_params=_cp("arbitrary", "arbitrary"),
    )(dx2, x1, gu, g_ffn, w_gu_g, w_gu_g, w_down, *parts)
    return res[:3], res[3:]


def _ffn_bwd_dw_gu(h2, dgu):
    s = h2.shape[0]
    ts = min(2 * ROW_TILE, s)
    n = s // ts

    def body(h_ref, d_ref, o_ref, acc_ref):
        t = pl.program_id(1)

        @pl.when(t == 0)
        def _():
            acc_ref[...] = jnp.zeros_like(acc_ref)

        acc_ref[...] += _dot_tn(h_ref[...], d_ref[0])

        @pl.when(t == n - 1)
        def _():
            o_ref[0] = acc_ref[...].astype(bf16)

    return pl.pallas_call(
        body, name="ffn_bwd_dw_gu", grid=(N_CHIPS, n),
        in_specs=[pl.BlockSpec((ts, D_MODEL), lambda j, t: (t, 0)),
                  pl.BlockSpec((1, ts, FF_CHUNK), lambda j, t: (j // 2, t, j % 2))],
        out_specs=pl.BlockSpec((1, D_MODEL, FF_CHUNK), lambda j, t: (j, 0, 0)),
        out_shape=jax.ShapeDtypeStruct((N_CHIPS, D_MODEL, FF_CHUNK), bf16),
        scratch_shapes=[pltpu.VMEM((D_MODEL, FF_CHUNK), f32)],
        compiler_params=_cp("parallel", "arbitrary"),
    )(h2, dgu)


def _ffn_bwd_dw_down(f, dx2):
    s = dx2.shape[0]
    ts = min(2 * ROW_TILE, s)
    n = s // ts

    def body(f_ref, d_ref, o_ref, acc_ref):
        t = pl.program_id(1)

        @pl.when(t == 0)
        def _():
            acc_ref[...] = jnp.zeros_like(acc_ref)

        acc_ref[...] += _dot_tn(f_ref[...], d_ref[...].astype(bf16))

        @pl.when(t == n - 1)
        def _():
            o_ref[...] = acc_ref[...].astype(bf16)

    return pl.pallas_call(
        body, name="ffn_bwd_dw_down", grid=(2, n),
        in_specs=[pl.BlockSpec((ts, FF_CHUNK), lambda c, t: (t, c)),
                  pl.BlockSpec((ts, D_MODEL), lambda c, t: (t, 0))],
        out_specs=pl.BlockSpec((FF_CHUNK, D_MODEL), lambda c, t: (c, 0)),
        out_shape=jax.ShapeDtypeStruct((D_FF, D_MODEL), bf16),
        scratch_shapes=[pltpu.VMEM((FF_CHUNK, D_MODEL), f32)],
        compiler_params=_cp("parallel", "arbitrary"),
    )(f, dx2)


def _mix_bwd_local(dx1, mix, u1, pooled, w_out, ln_g, ln_b, pool_w, pool_scale):
    s = dx1.shape[0]
    tm = min(ROW_TILE, s)
    n = s // tm

    def body(dx1_ref, mix_ref, u1_ref, po_ref, wo_ref, lg_ref, lb_ref, pw_ref, ps_ref,
             du1_ref, dpo_ref, dwo_ref, dpw_ref, small_ref, awo_ref):
        i = pl.program_id(0)

        @pl.when(i == 0)
        def _():
            awo_ref[...] = jnp.zeros_like(awo_ref)
            dpw_ref[...] = jnp.zeros_like(dpw_ref)
            small_ref[...] = jnp.zeros_like(small_ref)

        dyb = dx1_ref[...].astype(bf16)
        dmix = _dot_nt(dyb, wo_ref[...])
        awo_ref[...] += _dot_tn(mix_ref[...], dyb)
        u1 = u1_ref[...]
        mu = jnp.mean(u1, axis=-1, keepdims=True)
        uc = u1 - mu
        rstd = lax.rsqrt(jnp.mean(uc * uc, axis=-1, keepdims=True) + EPS)
        uh = uc * rstd
        u2 = uh * lg_ref[...] + lb_ref[...]
        du2 = dmix[:, :C_CONV] * _silu_grad(u2, _sigmoid(u2))
        duh = du2 * lg_ref[...]
        du1 = rstd * (duh - jnp.mean(duh, axis=-1, keepdims=True) - uh * jnp.mean(duh * uh, axis=-1, keepdims=True))
        du1_ref[...] = du1
        small_ref[0:1, :] += jnp.sum(du2 * uh, axis=0, keepdims=True)
        small_ref[1:2, :] += jnp.sum(du2, axis=0, keepdims=True)
        small_ref[2:3, :] += jnp.sum(du1, axis=0, keepdims=True)
        for g in range(len(POOL_WINDOWS)):
            cols = slice(g * POOL_GROUP, (g + 1) * POOL_GROUP)
            dq = dmix[:, C_CONV + g * POOL_GROUP:C_CONV + (g + 1) * POOL_GROUP]
            pwb = pw_ref[g].astype(bf16)
            pg = po_ref[:, cols]
            mixed = _dot(pg, pwb)
            small_ref[3:4, cols] += jnp.sum(dq * mixed, axis=0, keepdims=True)
            dmixed = (dq * ps_ref[:, cols]).astype(bf16)
            dpw_ref[g] += _dot_tn(pg, dmixed)
            dpo_ref[:, cols] = _dot_nt(dmixed, pwb)

        @pl.when(i == n - 1)
        def _():
            dwo_ref[...] = awo_ref[...].astype(bf16)

    return pl.pallas_call(
        body, name="mix_bwd_local", grid=(n,),
        in_specs=[_row(0, (tm, D_MODEL)), _row(0, (tm, D_MODEL)), _row(0, (tm, C_CONV)), _row(0, (tm, C_POOL)),
                  _full((D_MODEL, D_MODEL)), _full((1, C_CONV)), _full((1, C_CONV)),
                  _full((4, POOL_GROUP, POOL_GROUP)), _full((1, C_POOL))],
        out_specs=[_row(0, (tm, C_CONV)), _row(0, (tm, C_POOL)), _full((D_MODEL, D_MODEL)),
                   _full((4, POOL_GROUP, POOL_GROUP)), _full((8, C_CONV))],
        out_shape=[jax.ShapeDtypeStruct((s, C_CONV), f32), jax.ShapeDtypeStruct((s, C_POOL), f32),
                   jax.ShapeDtypeStruct((D_MODEL, D_MODEL), bf16),
                   jax.ShapeDtypeStruct((4, POOL_GROUP, POOL_GROUP), f32), jax.ShapeDtypeStruct((8, C_CONV), f32)],
        scratch_shapes=[pltpu.VMEM((D_MODEL, D_MODEL), f32)],
        compiler_params=_cp("arbitrary"),
    )(dx1, mix, u1, pooled, w_out, ln_g, ln_b, pool_w, pool_scale)


def _in_bwd(du1, dpo, z, x, dx1, conv_w, g_mix, w_in_g, parts):
    s = x.shape[0]
    tm = min(ROW_TILE, s)
    n = s // tm
    hb = tm // HALO
    last = s // HALO - 1
    nv = tm // 8
    assert nv >= SHIFT_PAD and nv % SHIFT_GROUPS == 0
    nc = len(parts)
    cs = _crossed(parts)

    def body(du_ref, dun_ref, dp_ref, dpn_ref, z_ref, zp_ref, x_ref, dx1_ref, cw_ref, g_ref, w_ref, *refs):
        gx_ref, dw_ref, dcw_ref, small_ref = refs[nc:nc + 4]
        eu, ed, ep, ss, u0_ref, dz_ref, acc_ref, dcw_acc = refs[2 * nc + 4:2 * nc + 12]
        cross = _Cross(refs[:nc], refs[nc + 4:2 * nc + 4], *refs[2 * nc + 12:])
        i = pl.program_id(0)
        pl.when(i == 0)(cross.start)

        @pl.when(i == 0)
        def _():
            acc_ref[...] = jnp.zeros_like(acc_ref)
            dcw_acc[...] = jnp.zeros_like(dcw_acc)
            small_ref[...] = jnp.zeros_like(small_ref)

        keep_prev = (i > 0).astype(f32)
        keep_next = (i < n - 1).astype(f32)
        zp = zp_ref[...] * keep_prev
        u0_prev = zp[:, :C_CONV] * _sigmoid(zp[:, C_CONV:2 * C_CONV])
        u0_ref[...] = z_ref[:, :C_CONV] * _sigmoid(z_ref[:, C_CONV:2 * C_CONV])
        du_next = dun_ref[...] * keep_next
        for c, w_pool in enumerate(POOL_WINDOWS):
            lanes = slice(c * 128, (c + 1) * 128)
            eu[c, pl.ds(0, SHIFT_PAD, stride=8), :] = u0_prev[:, lanes]
            ed[c, pl.ds(nv * 8 + 7, SHIFT_PAD, stride=8), :] = du_next[:, lanes]
            ep[c, pl.ds(nv * 8 + 7, SHIFT_PAD, stride=8), :] = (
                dpn_ref[:, lanes] * keep_next / _pool_counts(HALO, w_pool, (i + 1) * tm))
            for j in range(8):
                rows = slice(j * nv, (j + 1) * nv)
                eu[c, pl.ds(SHIFT_PAD * 8 + j, nv, stride=8), :] = u0_ref[rows, lanes]
                ed[c, pl.ds(j, nv, stride=8), :] = du_ref[rows, lanes]
                ep[c, pl.ds(j, nv, stride=8), :] = dp_ref[rows, lanes] / _pool_counts(nv, w_pool, i * tm + j * nv)
                if j >= 1:
                    eu[c, pl.ds(j, SHIFT_PAD, stride=8), :] = u0_ref[j * nv - SHIFT_PAD:j * nv, lanes]
                if j <= 6:
                    edge = slice((j + 1) * nv, (j + 1) * nv + SHIFT_PAD)
                    ed[c, pl.ds(nv * 8 + j, SHIFT_PAD, stride=8), :] = du_ref[edge, lanes]
                    ep[c, pl.ds(nv * 8 + j, SHIFT_PAD, stride=8), :] = (
                        dp_ref[edge, lanes] / _pool_counts(SHIFT_PAD, w_pool, i * tm + (j + 1) * nv))
        for c, w_pool in enumerate(POOL_WINDOWS):
            lanes = slice(c * 128, (c + 1) * 128)
            b_lanes = slice(C_CONV + c * 128, C_CONV + (c + 1) * 128)
            v_lanes = slice(2 * C_CONV + c * 128, 2 * C_CONV + (c + 1) * 128)
            for v0 in range(0, nv, SHIFT_GROUPS):
                span = SHIFT_GROUPS * 8
                acc = jnp.zeros((span, 128), f32)
                for k in range(CONV_K):
                    acc = acc + cw_ref[pl.ds(k, 1), lanes] * ed[c, pl.ds((v0 + CONV_K - 1 - k) * 8, span), :]
                ss[0, v0 * 8:v0 * 8 + span, :] = acc
                acc = ep[c, pl.ds(v0 * 8, span), :]
                for d in range(1, w_pool):
                    acc = acc + ep[c, pl.ds((v0 + d) * 8, span), :]
                ss[1, v0 * 8:v0 * 8 + span, :] = acc
                d1 = ed[c, pl.ds(v0 * 8, span), :]
                for k in range(CONV_K):
                    prod = d1 * eu[c, pl.ds((SHIFT_PAD - (CONV_K - 1) + v0 + k) * 8, span), :]
                    fold = prod[0:8]
                    for r in range(8, span, 8):
                        fold = fold + prod[r:r + 8]
                    dcw_acc[k, :, lanes] += fold
            for j in range(8):
                rows = slice(j * nv, (j + 1) * nv)
                du0 = ss[0, pl.ds(j, nv, stride=8), :]
                av, sv = z_ref[rows, lanes], _sigmoid(z_ref[rows, b_lanes])
                dz_ref[rows, lanes] = (du0 * sv).astype(bf16)
                dz_ref[rows, b_lanes] = (du0 * av * sv * (1.0 - sv)).astype(bf16)
                dz_ref[rows, v_lanes] = (ss[1, pl.ds(j, nv, stride=8), :] - dp_ref[rows, lanes]).astype(bf16)
        h, xh, r = _rms_fwd(x_ref[...], g_ref[...])
        dz = dz_ref[...]
        acc_ref[...] += _dot_tn(h.astype(bf16), dz)
        dh = _dot_nt(dz[:, 0:W_IN_CHUNK], w_ref[0])
        for j in range(1, N_CHIPS):
            dh = dh + _dot_nt(dz[:, j * W_IN_CHUNK:(j + 1) * W_IN_CHUNK], w_ref[j])
        dxb, dg = _rms_bwd(dh, xh, r, g_ref[...])
        gx_ref[...] = dx1_ref[...] + dxb
        small_ref[0:1, :] += dg

        @pl.when(i == n - 1)
        def _():
            for j in range(N_CHIPS):
                dw_ref[j] = acc_ref[:, j * W_IN_CHUNK:(j + 1) * W_IN_CHUNK].astype(bf16)
            dcw_ref[...] = jnp.sum(dcw_acc[...], axis=1)

        pl.when(i == n - 1)(cross.finish)

    nxt = lambda i: (jnp.minimum((i + 1) * hb, last), 0)
    res = pl.pallas_call(
        body, name="in_bwd", grid=(n,),
        in_specs=[_row(0, (tm, C_CONV)), pl.BlockSpec((HALO, C_CONV), nxt),
                  _row(0, (tm, C_POOL)), pl.BlockSpec((HALO, C_POOL), nxt),
                  _row(0, (tm, W_IN_COLS)),
                  pl.BlockSpec((HALO, W_IN_COLS), lambda i: (jnp.maximum(i * hb - 1, 0), 0)),
                  _row(0, (tm, D_MODEL)), _row(0, (tm, D_MODEL)), _full((HALO, C_CONV)), _full((1, D_MODEL)),
                  _full((N_CHIPS, D_MODEL, W_IN_CHUNK))] + cs["in_specs"],
        out_specs=[_row(0, (tm, D_MODEL)), _full((N_CHIPS, D_MODEL, W_IN_CHUNK)), _full((HALO, C_CONV)),
                   _full((8, D_MODEL))] + cs["out_specs"],
        out_shape=[jax.ShapeDtypeStruct((s, D_MODEL), f32), jax.ShapeDtypeStruct((N_CHIPS, D_MODEL, W_IN_CHUNK), bf16),
                   jax.ShapeDtypeStruct((HALO, C_CONV), f32), jax.ShapeDtypeStruct((8, D_MODEL), f32)] + cs["out_shape"],
        scratch_shapes=[pltpu.VMEM((4, (SHIFT_PAD + nv) * 8, 128), f32), pltpu.VMEM((4, (nv + SHIFT_PAD) * 8, 128), f32),
                        pltpu.VMEM((4, (nv + SHIFT_PAD) * 8, 128), f32), pltpu.VMEM((2, tm, 128), f32),
                        pltpu.VMEM((tm, C_CONV), f32), pltpu.VMEM((tm, W_IN_COLS), bf16),
                        pltpu.VMEM((D_MODEL, W_IN_COLS), f32), pltpu.VMEM((HALO, 8, C_CONV), f32)] + cs["scratch"],
        compiler_params=_cp("arbitrary"),
    )(du1, du1, dpo, dpo, z, z, x, dx1, conv_w, g_mix, w_in_g, *parts)
    return res[:4], res[4:]


def _rs_pair(name, grads):
    nk = len(grads)
    halves = [g.shape[1] // 2 for g in grads]

    def body(*refs):
        ins, outs = refs[:nk], refs[nk:2 * nk]
        send_sems, recv_sems = refs[2 * nk:]
        x, y, c, _, _ = _place()
        cps = [_remote(ins[k].at[:, pl.ds((1 - c) * halves[k], halves[k]), :], outs[k],
                       send_sems.at[k], recv_sems.at[k], (x, y, 1 - c)) for k in range(nk)]
        for cp in cps:
            cp.start()
        for cp in cps:
            cp.wait()

    return pl.pallas_call(
        body, name=f"rs_pair_{name}", in_specs=[ANY] * nk, out_specs=[ANY] * nk,
        out_shape=[jax.ShapeDtypeStruct((N_CHIPS, h, g.shape[2]), bf16) for g, h in zip(grads, halves)],
        scratch_shapes=[pltpu.SemaphoreType.DMA((nk,)), pltpu.SemaphoreType.DMA((nk,))],
    )(*grads)


def _rs_add_pair(name, core, grad, recv):
    _, half, cols = recv.shape

    def body(core_ref, g_ref, r_ref, o_ref):
        o_ref[...] = (g_ref[...].astype(f32) + r_ref[...].astype(f32)).astype(bf16)

    return pl.pallas_call(
        body, name=f"rs_add_pair_{name}",
        grid_spec=pltpu.PrefetchScalarGridSpec(
            num_scalar_prefetch=1, grid=(N_CHIPS,),
            in_specs=[pl.BlockSpec((1, half, cols), lambda j, core_ref: (j, core_ref[0], 0)),
                      pl.BlockSpec((1, half, cols), lambda j, core_ref: (j, 0, 0))],
            out_specs=pl.BlockSpec((1, half, cols), lambda j, core_ref: (j, 0, 0))),
        out_shape=jax.ShapeDtypeStruct(recv.shape, bf16),
        compiler_params=_cp("parallel"),
    )(core, grad, recv)


class _Cross:
    def __init__(self, parts, landed, send_sems, recv_sems):
        self.parts, self.landed, self.send_sems, self.recv_sems = parts, landed, send_sems, recv_sems
        _, _, self.c, self.me, self.others = _place()

    def _copy(self, k, j, src_chunk, dst_slot):
        ox, oy = self.others[j]
        return _remote(self.parts[k].at[src_chunk], self.landed[k].at[dst_slot], self.send_sems.at[3 * k + j],
                       self.recv_sems.at[3 * k + j], (ox, oy, self.c))

    def _each(self):
        return [(k, j, 2 * self.others[j][0] + self.others[j][1]) for k in range(len(self.parts)) for j in range(3)]

    def start(self):
        for k, j, chip in self._each():
            self._copy(k, j, chip, self.me).start()

    def finish(self):
        for k, j, chip in self._each():
            self._copy(k, j, chip, chip).wait_recv()
        for k, j, chip in self._each():
            self._copy(k, j, chip, self.me).wait_send()

    @staticmethod
    def scratch(n):
        return [pltpu.SemaphoreType.DMA((3 * n,)), pltpu.SemaphoreType.DMA((3 * n,))]


def _rs_sum_chips(name, place, landed, part):
    _, half, cols = landed.shape

    def body(place_ref, l_ref, p_ref, o_ref):
        me = place_ref[0]
        own = p_ref[0].astype(f32)
        acc = jnp.where(me == 0, own, l_ref[0].astype(f32))
        for j in range(1, N_CHIPS):
            acc = acc + jnp.where(me == j, own, l_ref[j].astype(f32))
        o_ref[...] = acc

    return pl.pallas_call(
        body, name=f"rs_sum_chips_{name}",
        grid_spec=pltpu.PrefetchScalarGridSpec(
            num_scalar_prefetch=1, grid=(1,),
            in_specs=[pl.BlockSpec((N_CHIPS, half, cols), lambda t, place_ref: (0, 0, 0)),
                      pl.BlockSpec((1, half, cols), lambda t, place_ref: (place_ref[0], 0, 0))],
            out_specs=pl.BlockSpec((half, cols), lambda t, place_ref: (place_ref[1], 0))),
        out_shape=jax.ShapeDtypeStruct((2 * half, cols), f32),
        compiler_params=_cp("arbitrary"),
    )(place, landed, part)


def _grad_pair(shards):
    nk = len(shards)

    def body(*refs):
        outs = refs[nk:2 * nk]
        send_sems, recv_sems = refs[2 * nk:]
        x, y, c, _, _ = _place()

        def half(k, core):
            h = outs[k].shape[0] // 2
            return outs[k].at[pl.ds(core * h, h), :]

        cps = [_remote(half(k, c), half(k, c), send_sems.at[k], recv_sems.at[k], (x, y, 1 - c)) for k in range(nk)]
        for cp in cps:
            cp.start()
        for k in range(nk):
            _remote(half(k, 1 - c), half(k, 1 - c), send_sems.at[k], recv_sems.at[k], (x, y, 1 - c)).wait_recv()
        for cp in cps:
            cp.wait_send()

    return pl.pallas_call(
        body, name="grad_pair", in_specs=[ANY] * nk, out_specs=[ANY] * nk,
        out_shape=[jax.ShapeDtypeStruct(a.shape, f32) for a in shards],
        input_output_aliases={k: k for k in range(nk)},
        scratch_shapes=[pltpu.SemaphoreType.DMA((nk,)), pltpu.SemaphoreType.DMA((nk,))],
    )(*shards)


def _adam_math(w, g, m, v):
    m = ADAM_B1 * m + (1.0 - ADAM_B1) * g
    v = ADAM_B2 * v + (1.0 - ADAM_B2) * (g * g)
    m_hat = m / (1.0 - ADAM_B1 ** ADAM_STEP)
    v_hat = v / (1.0 - ADAM_B2 ** ADAM_STEP)
    delta = -ADAM_LR * (m_hat / (jnp.sqrt(v_hat) + ADAM_EPS) + ADAM_WD * w)
    return delta, m, v


def _adam(k, w, g, m, v):
    rows, cols = w.shape
    tr = next(rows // d for d in (1, 2, 4, 8) if rows % (8 * d) == 0 and rows // d <= 256)

    def body(w_ref, g_ref, m_ref, v_ref, d_ref, nm_ref, nv_ref):
        d_ref[...], nm_ref[...], nv_ref[...] = _adam_math(w_ref[...], g_ref[...], m_ref[...], v_ref[...])

    spec = _row(0, (tr, cols))
    return pl.pallas_call(
        body, name=f"adam_{k}", grid=(rows // tr,), in_specs=[spec] * 4, out_specs=[spec] * 3,
        out_shape=[jax.ShapeDtypeStruct(w.shape, f32)] * 3,
        compiler_params=_cp("parallel"),
    )(w, g, m, v)


class _SmallGather:
    def __init__(self, part, buf, send_sems, recv_sems):
        self.part, self.buf, self.send_sems, self.recv_sems = part, buf, send_sems, recv_sems
        self.x, self.y, self.c, _, self.others = _place()
        self.sibling = (self.x, self.y, 1 - self.c)

    def _copy(self, k, block, to, src=None):
        slot = self.buf.at[4 * block[0] + 2 * block[1] + block[2]]
        return _remote(slot if src is None else src, slot, self.send_sems.at[k], self.recv_sems.at[k], to)

    def _first(self):
        me = (self.x, self.y, self.c)
        return [self._copy(0, me, self.sibling, src=self.part)] + [
            self._copy(1 + j, me, (*chip, self.c), src=self.part) for j, chip in enumerate(self.others)]

    def _passed(self):
        return [self._copy(4 + j, (*chip, self.c), self.sibling) for j, chip in enumerate(self.others)]

    def start(self):
        for cp in self._first():
            cp.start()

    def finish(self):
        passed = self._passed()
        for j, chip in enumerate(self.others):
            self._copy(1 + j, (*chip, self.c), self.sibling).wait_recv()
            passed[j].start()
        self._copy(0, self.sibling, self.sibling).wait_recv()
        for j, chip in enumerate(self.others):
            self._copy(4 + j, (*chip, 1 - self.c), self.sibling).wait_recv()
        for cp in self._first() + passed:
            cp.wait_send()


def _rs_sum_group(name, place, landed, parts, cross_parts, small_part):
    nk, nx = len(landed), len(cross_parts)
    dims = [a.shape[1:] for a in landed]
    rows = small_part.shape[0]

    def body(place_ref, *refs):
        l_refs, p_refs = refs[:nk], refs[nk:2 * nk]
        x_refs, sp_ref = refs[2 * nk:2 * nk + nx], refs[2 * nk + nx]
        outs = refs[2 * nk + nx + 1:]
        o_refs, xl_refs, sbuf = outs[:nk], outs[nk:nk + nx], outs[nk + nx]
        sems = outs[nk + nx + 1:]
        cross = _Cross(x_refs, xl_refs, sems[0], sems[1])
        small = _SmallGather(sp_ref, sbuf, sems[2], sems[3])
        t = pl.program_id(0)

        @pl.when(t == 0)
        def _():
            cross.start()
            small.start()

        me = place_ref[0]
        for l_ref, p_ref, o_ref in zip(l_refs, p_refs, o_refs):
            own = p_ref[0].astype(f32)
            acc = jnp.where(me == 0, own, l_ref[0].astype(f32))
            for j in range(1, N_CHIPS):
                acc = acc + jnp.where(me == j, own, l_ref[j].astype(f32))
            o_ref[...] = acc

        @pl.when(t == 1)
        def _():
            small.finish()
            cross.finish()

    def halves(h, c, lead, index):
        return pl.BlockSpec((lead, h // 2, c) if lead else (h // 2, c), index)

    in_specs = [halves(h, c, N_CHIPS, lambda t, pr: (0, t, 0)) for h, c in dims]
    in_specs += [halves(h, c, 1, lambda t, pr: (pr[0], t, 0)) for h, c in dims]
    in_specs += [ANY] * (nx + 1)
    out_specs = [halves(h, c, 0, lambda t, pr: (2 * pr[1] + t, 0)) for h, c in dims] + [ANY] * (nx + 1)
    out_shape = [jax.ShapeDtypeStruct((2 * h, c), f32) for h, c in dims]
    out_shape += [jax.ShapeDtypeStruct(a.shape, a.dtype) for a in cross_parts]
    out_shape.append(jax.ShapeDtypeStruct((N_DEV, rows, 128), f32))
    res = pl.pallas_call(
        body, name=f"rs_sum_{name}",
        grid_spec=pltpu.PrefetchScalarGridSpec(
            num_scalar_prefetch=1, grid=(2,), in_specs=in_specs, out_specs=out_specs,
            scratch_shapes=_Cross.scratch(nx) + [pltpu.SemaphoreType.DMA((7,)), pltpu.SemaphoreType.DMA((7,))]),
        out_shape=out_shape, compiler_params=_cp("arbitrary"),
    )(place, *landed, *parts, *cross_parts, small_part)
    return res[:nk], res[nk:nk + nx], res[nk + nx]


def _small_adam(dev, gathered, part, w, m, v):
    def body(dev_ref, b_ref, p_ref, w_ref, m_ref, v_ref, g_ref, d_ref, nm_ref, nv_ref):
        me = dev_ref[0]
        own = p_ref[...]
        g = jnp.where(me == 0, own, b_ref[0])
        for d in range(1, N_DEV):
            g = g + jnp.where(me == d, own, b_ref[d])
        g_ref[...] = g
        d_ref[...], nm_ref[...], nv_ref[...] = _adam_math(w_ref[...], g, m_ref[...], v_ref[...])

    shape = jax.ShapeDtypeStruct(part.shape, f32)
    flat = pl.BlockSpec(part.shape, lambda t, dev_ref: (0, 0))
    return pl.pallas_call(
        body, name="small_adam",
        grid_spec=pltpu.PrefetchScalarGridSpec(
            num_scalar_prefetch=1, grid=(1,),
            in_specs=[pl.BlockSpec(gathered.shape, lambda t, dev_ref: (0, 0, 0))] + [flat] * 4, out_specs=[flat] * 4),
        out_shape=[shape] * 4, compiler_params=_cp("arbitrary"),
    )(dev, gathered, part, w, m, v)


def _rows128(a):
    return a.reshape(-1, 128)


def _pad_rows(a, rows):
    return jnp.concatenate([a, jnp.zeros((rows - a.shape[0],) + a.shape[1:], a.dtype)], axis=0)


def _pack_small(me, g_pg, g_post, g_final, g_ffn, ln_g, ln_b, conv_b, pool_scale, pool_w, conv_w, g_mix):
    blk_ple = _pad_rows(jnp.concatenate([g_pg, g_post, g_final.reshape(1, D_MODEL)], axis=0), 8)
    blk_ffn = _pad_rows(g_ffn, 8)
    blk_mix = _pad_rows(jnp.concatenate([ln_g, ln_b, conv_b, pool_scale], axis=0), 8)
    cw = lax.dynamic_update_slice(jnp.zeros((N_CHIPS, HALO, 128), f32), _pad_rows(conv_w, HALO)[None], (me, 0, 0))
    blk_in = _pad_rows(g_mix, 8)
    return jnp.concatenate([_rows128(blk_ple), _rows128(blk_ffn), _rows128(blk_mix), _rows128(pool_w),
                            _rows128(cw), _rows128(blk_in)], axis=0)


def _unpack_small(me, packed):
    o = 0

    def take(rows):
        nonlocal o
        blk = packed[o:o + rows]
        o += rows
        return blk

    ple = take(64).reshape(8, D_MODEL)
    ffn = take(64).reshape(8, D_MODEL)
    mixb = take(32).reshape(8, C_CONV)
    pool_w = take(512).reshape(1, 4, POOL_GROUP, POOL_GROUP)
    cw = take(N_CHIPS * HALO).reshape(N_CHIPS, HALO, 128)
    inb = take(64).reshape(8, D_MODEL)
    conv_w = lax.dynamic_slice(cw, (me, 0, 0), (1, HALO, 128))[:, :CONV_K, :]
    return dict(g_ple_gate=ple[0:1], g_ple_post=ple[1:2], g_final=ple[2], loss=ple[3, 0], g_ffn=ffn[0:1],
                ln_g=mixb[0:1], ln_b=mixb[1:2], conv_b=mixb[2:3], pool_scale=mixb[3:4], pool_w=pool_w,
                conv_w=conv_w, g_mix=inb[0:1])


def kernel(x, p, g_mix, w_in, conv_w, conv_b, ln_g, ln_b, pool_w, pool_scale, w_out, g_ffn, w_gate_up, w_down, g_ple_gate, w_ple_gate, w_ple_up, g_ple_post, g_final, loss_target, m_g_mix, m_w_in, m_conv_w, m_conv_b, m_ln_g, m_ln_b, m_pool_w, m_pool_scale, m_w_out, m_g_ffn, m_w_gate_up, m_w_down, m_g_ple_gate, m_w_ple_gate, m_w_ple_up, m_g_ple_post, m_g_final, v_g_mix, v_w_in, v_conv_w, v_conv_b, v_ln_g, v_ln_b, v_pool_w, v_pool_scale, v_w_out, v_g_ffn, v_w_gate_up, v_w_down, v_g_ple_gate, v_w_ple_gate, v_w_ple_up, v_g_ple_post, v_g_final):
    seq = x.shape[1]
    me = 2 * lax.axis_index("x") + lax.axis_index("y")
    chip = me.astype(jnp.int32).reshape(1)
    core = lax.axis_index("c").astype(jnp.int32).reshape(1)
    place = jnp.concatenate([chip, core])
    xs, ps, ts = x.reshape(seq, D_MODEL), p.reshape(seq, D_PLE), loss_target.reshape(seq, D_MODEL)

    big_names = ["w_in", "w_gu", "w_out", "w_down", "w_pg", "w_pu"]
    big = [w_in[0], w_gate_up[0], w_out[0], w_down[0], w_ple_gate[0], w_ple_up[0]]
    big_m = [m_w_in[0], m_w_gate_up[0], m_w_out[0], m_w_down[0], m_w_ple_gate[0], m_w_ple_up[0]]
    big_v = [v_w_in[0], v_w_gate_up[0], v_w_out[0], v_w_down[0], v_w_ple_gate[0], v_w_ple_up[0]]
    b_in, b_gu, b_out, b_down, b_pg, b_pu = [_cast_into_slot(nm, chip, w, bf16) for nm, w in zip(big_names, big)]
    b_cw = _cast_into_slot("conv_w", chip, _pad_rows(conv_w[0], HALO), f32)
    xi, yi = lax.axis_index("x"), lax.axis_index("y")
    order = jnp.stack([me, 2 * (1 - xi) + yi, 2 * xi + 1 - yi, 2 * (1 - xi) + 1 - yi]).astype(jnp.int32)

    z, (w_in_g, cw_g, w_out_g) = _mix_in(xs, g_mix, order, [b_in, b_cw, b_out])
    conv_w_f = cw_g.transpose(1, 0, 2).reshape(HALO, C_CONV)
    w_out_f = w_out_g.reshape(D_MODEL, D_MODEL)
    (x1, mix, u1, pooled), (w_gu_g, w_down_g) = _conv_pool_out(z, xs, conv_w_f, conv_b, ln_g, ln_b, pool_w[0],
                                                                pool_scale, w_out_f, [b_gu, b_down])
    w_down_f = w_down_g.reshape(D_FF, D_MODEL)
    (x2, h2, gu, ffn_f), (w_pg_g, w_pu_g) = _ffn_fwd(x1, g_ffn, w_gu_g, w_down_f, [b_pg, b_pu])
    w_pg_f = w_pg_g.reshape(D_MODEL, D_MODEL)
    dx2, d_w_pg, d_w_pu, small_ple = _ple_loss(x2, ps, ts, g_ple_gate, g_ple_post, g_final.reshape(1, D_MODEL),
                                               w_pg_f, w_pu_g)
    def pair_reduce(group, names, grads):
        got = _rs_pair(group, grads)
        return [_rs_add_pair(nm, core, g, r) for nm, g, r in zip(names, grads, got)]

    d_w_down = _ffn_bwd_dw_down(ffn_f, dx2)
    parts_a = pair_reduce("a", ["w_pg", "w_pu", "w_down"],
                          [d_w_pg.reshape(N_CHIPS, -1, D_MODEL), d_w_pu, d_w_down.reshape(N_CHIPS, -1, D_MODEL)])
    (dx1, dgu, small_ffn), landed_a = _ffn_bwd_dx(dx2, x1, gu, g_ffn, w_gu_g, w_down_f, parts_a)
    d_w_gu = _ffn_bwd_dw_gu(h2, dgu)
    du1, dpo, d_w_out, d_pool_w, small_mix = _mix_bwd_local(dx1, mix, u1, pooled, w_out_f, ln_g, ln_b, pool_w[0],
                                                             pool_scale)
    parts_b = pair_reduce("b", ["w_gu", "w_out"], [d_w_gu, d_w_out.reshape(N_CHIPS, -1, D_MODEL)])
    (grad_x, d_w_in, d_conv_w, small_in), landed_b = _in_bwd(du1, dpo, z, xs, dx1, conv_w_f, g_mix, w_in_g, parts_b)
    parts_c = pair_reduce("c", ["w_in"], [d_w_in])
    cw_chunks = d_conv_w.reshape(HALO, N_CHIPS, 128).transpose(1, 0, 2)
    part = jnp.concatenate([_rows128(small_ple), _rows128(small_ffn), _rows128(small_mix), _rows128(d_pool_w),
                            _rows128(cw_chunks), _rows128(small_in)], axis=0)
    (h_gu, h_out, h_down, h_pg, h_pu), landed_c, small_all = _rs_sum_group(
        "ab", place, [landed_b[0], landed_b[1], landed_a[2], landed_a[0], landed_a[1]],
        [parts_b[0], parts_b[1], parts_a[2], parts_a[0], parts_a[1]], parts_c, part)
    h_in = _rs_sum_chips("w_in", place, landed_c[0], parts_c[0])
    big_g = _grad_pair([h_in, h_gu, h_out, h_down, h_pg, h_pu])
    big_upd = [_adam(nm, w, g, m, v) for nm, w, g, m, v in zip(big_names, big, big_g, big_m, big_v)]

    sw = _pack_small(me, g_ple_gate, g_ple_post, g_final, g_ffn, ln_g, ln_b, conv_b, pool_scale, pool_w, conv_w[0], g_mix)
    sm = _pack_small(me, m_g_ple_gate, m_g_ple_post, m_g_final, m_g_ffn, m_ln_g, m_ln_b, m_conv_b, m_pool_scale,
                     m_pool_w, m_conv_w[0], m_g_mix)
    sv = _pack_small(me, v_g_ple_gate, v_g_ple_post, v_g_final, v_g_ffn, v_ln_g, v_ln_b, v_conv_b, v_pool_scale,
                     v_pool_w, v_conv_w[0], v_g_mix)
    dev = (2 * chip + core).astype(jnp.int32)
    small = [_unpack_small(me, a) for a in _small_adam(dev, small_all, part, sw, sm, sv)]

    names = ["g_mix", "w_in", "conv_w", "conv_b", "ln_g", "ln_b", "pool_w", "pool_scale", "w_out", "g_ffn",
             "w_gate_up", "w_down", "g_ple_gate", "w_ple_gate", "w_ple_up", "g_ple_post", "g_final"]
    big_at = {"w_in": 0, "w_gate_up": 1, "w_out": 2, "w_down": 3, "w_ple_gate": 4, "w_ple_up": 5}
    out = [small[0]["loss"], grad_x.reshape(1, seq, D_MODEL)]
    for kind in range(4):
        for nm in names:
            if nm in big_at:
                k = big_at[nm]
                out.append((big_g[k] if kind == 0 else big_upd[k][kind - 1])[None])
            else:
                out.append(small[kind][nm])
    return tuple(out)
```

```python
import functools

import jax
import jax.numpy as jnp
from jax import lax
from jax.experimental import pallas as pl
from jax.experimental.pallas import tpu as pltpu

f32, bf16 = jnp.float32, jnp.bfloat16

EPS = 1e-6
D_MODEL = 1024
C_CONV = 512
C_POOL = 512
POOL_WINDOWS = (2, 4, 8, 16)
POOL_GROUP = 128
CONV_K = 31
D_FF = 2816
D_PLE = 256
N_CHIPS = 4
N_DEV = 8
W_IN_COLS = 2 * C_CONV + C_POOL
W_IN_CHUNK = W_IN_COLS // N_CHIPS
FF_CHUNK = 2 * D_FF // N_CHIPS
PLE_CHUNK = D_MODEL // N_CHIPS
HALO = 32
ROW_TILE = 512
CONV_ROWS = 64
CONV_COLS = (slice(0, 256), slice(256, 512))
SHIFT_PAD = 32
SHIFT_GROUPS = 16
FF_SUB = (0, 512, 1024, FF_CHUNK)
VMEM_LIMIT = 56 * 1024 * 1024

ADAM_LR = 0.001
ADAM_B1 = 0.9
ADAM_B2 = 0.999
ADAM_EPS = 1e-08
ADAM_WD = 0.01
ADAM_STEP = 10

MESH = pl.DeviceIdType.MESH
ANY = pl.BlockSpec(memory_space=pl.ANY)
VMEM = pl.BlockSpec(memory_space=pltpu.VMEM)


def _cp(*sem):
    return pltpu.CompilerParams(dimension_semantics=sem, vmem_limit_bytes=VMEM_LIMIT)


def _dot(a, b):
    return jnp.dot(a, b, preferred_element_type=f32)


def _dot_nt(a, b):
    return lax.dot_general(a, b, (((1,), (1,)), ((), ())), preferred_element_type=f32)


def _dot_tn(a, b):
    return lax.dot_general(a, b, (((0,), (0,)), ((), ())), preferred_element_type=f32)


def _sigmoid(v):
    return jax.nn.sigmoid(v)


def _rms_fwd(v, g):
    r = lax.rsqrt(jnp.mean(v * v, axis=-1, keepdims=True) + EPS)
    vh = v * r
    return vh * g, vh, r


def _rms_bwd(dy, vh, r, g):
    dvh = dy * g
    dv = r * (dvh - vh * jnp.mean(dvh * vh, axis=-1, keepdims=True))
    return dv, jnp.sum(dy * vh, axis=0, keepdims=True)


def _silu_grad(v, s):
    return s * (1.0 + v * (1.0 - s))


def _row(i, n):
    return pl.BlockSpec((n[0], n[1]), lambda *a: (a[i], 0))


def _full(shape):
    nd = len(shape)
    return pl.BlockSpec(shape, lambda *a: (0,) * nd)


def _place():
    x, y, c = lax.axis_index("x"), lax.axis_index("y"), lax.axis_index("c")
    others = [(1 - x, y), (x, 1 - y), (1 - x, 1 - y)]
    return x, y, c, 2 * x + y, others


def _remote(src, dst, send_sem, recv_sem, dev):
    return pltpu.make_async_remote_copy(src_ref=src, dst_ref=dst, send_sem=send_sem, recv_sem=recv_sem,
                                        device_id=dev, device_id_type=MESH)


def _cast_into_slot(name, me, w, dtype):
    rows, cols = w.shape
    tr = next(rows // d for d in (1, 2, 4, 8) if rows % (16 * d) == 0 and rows // d <= 512)

    def body(me_ref, w_ref, o_ref):
        o_ref[0] = w_ref[...].astype(dtype)

    return pl.pallas_call(
        body, name=f"cast_{name}",
        grid_spec=pltpu.PrefetchScalarGridSpec(
            num_scalar_prefetch=1, grid=(rows // tr,),
            in_specs=[pl.BlockSpec((tr, cols), lambda r, me_ref: (r, 0))],
            out_specs=pl.BlockSpec((1, tr, cols), lambda r, me_ref: (me_ref[0], r, 0))),
        out_shape=jax.ShapeDtypeStruct((N_CHIPS, rows, cols), dtype),
        compiler_params=_cp("parallel"),
    )(me, w)


class _Gather:
    def __init__(self, bufs, send_sems, recv_sems):
        self.bufs, self.send_sems, self.recv_sems = bufs, send_sems, recv_sems
        self.x, self.y, self.c, self.me, self.others = _place()
        self.halves = [b.shape[1] // 2 for b in bufs]

    def _piece(self, k, chip, half):
        return self.bufs[k].at[chip, pl.ds(half * self.halves[k], self.halves[k]), :]

    def _ici(self, k, j, chip):
        ox, oy = self.others[j]
        piece = self._piece(k, chip, self.c)
        return _remote(piece, piece, self.send_sems.at[6 * k + j], self.recv_sems.at[6 * k + j], (ox, oy, self.c))

    def _relay(self, k):
        first = self.c == 0
        piece = self._piece(k, jnp.where(first, self.chip(0), self.chip(1)), self.c)
        to = (jnp.where(first, self.others[1][0], self.others[0][0]),
              jnp.where(first, self.others[1][1], self.others[0][1]), self.c)
        return _remote(piece, piece, self.send_sems.at[6 * k + 2], self.recv_sems.at[6 * k + 2], to)

    def _pair(self, k, j, half):
        ox, oy = self.others[j]
        piece = self._piece(k, 2 * ox + oy, half)
        return _remote(piece, piece, self.send_sems.at[6 * k + 3 + j], self.recv_sems.at[6 * k + 3 + j],
                       (self.x, self.y, 1 - self.c))

    def _each(self, ks=None):
        return [(k, j) for k in (range(len(self.bufs)) if ks is None else ks) for j in range(3)]

    def chip(self, j):
        ox, oy = self.others[j]
        return 2 * ox + oy

    def start(self):
        for k in range(len(self.bufs)):
            for j in range(2):
                self._ici(k, j, self.me).start()

    def forward(self, pairs=None):
        for k, j in self._each() if pairs is None else pairs:
            self._ici(k, j, self.chip(j)).wait_recv()
            self._pair(k, j, self.c).start()
            if j < 2:
                pl.when(self.c == j)(self._relay(k).start)

    def landed(self, pairs):
        for k, j in pairs:
            self._pair(k, j, 1 - self.c).wait_recv()

    def finish(self, ks=None):
        self.landed(self._each(ks))
        for k in range(len(self.bufs)):
            for j in range(2):
                self._ici(k, j, self.me).wait_send()
            self._relay(k).wait_send()
            for j in range(3):
                self._pair(k, j, self.c).wait_send()

    @staticmethod
    def scratch(n):
        return [pltpu.SemaphoreType.DMA((6 * n,)), pltpu.SemaphoreType.DMA((6 * n,))]


def _carried(bufs):
    n = len(bufs)
    return dict(in_specs=[ANY] * n, out_specs=[ANY] * n,
                out_shape=[jax.ShapeDtypeStruct(b.shape, b.dtype) for b in bufs], scratch=_Gather.scratch(n))


def _mix_in(x, g_mix, order, carry):
    s = x.shape[0]
    tm = min(2 * ROW_TILE, s)
    n = s // tm
    nc = len(carry)
    cs = _carried(carry)

    def body(order_ref, x_ref, g_ref, *refs):
        z_ref = refs[nc]
        bufs = refs[nc + 1:2 * nc + 1]
        h_ref, w_ref, w_sem = refs[2 * nc + 1:2 * nc + 4]
        gather = _Gather(bufs, *refs[2 * nc + 4:])
        q, i = pl.program_id(0), pl.program_id(1)
        first = i == 0
        pl.when(jnp.logical_and(q == 0, first))(gather.start)
        for j in range(3):

            @pl.when(jnp.logical_and(q == j + 1, first))
            def _():
                gather.forward([(0, j)])
                gather.landed([(0, j)])

        @pl.when(first)
        def _():
            load = pltpu.make_async_copy(bufs[0].at[order_ref[q]], w_ref, w_sem)
            load.start()
            load.wait()

        @pl.when(q == 0)
        def _():
            h, _, _ = _rms_fwd(x_ref[...], g_ref[...])
            h_ref[i] = h.astype(bf16)

        z_ref[...] = _dot(h_ref[i], w_ref[...])

        @pl.when(jnp.logical_and(q == N_CHIPS - 1, i == n - 1))
        def _():
            rest = gather._each(range(1, nc))
            gather.forward(rest)
            gather.finish(range(1, nc))

    res = pl.pallas_call(
        body, name="mix_in",
        grid_spec=pltpu.PrefetchScalarGridSpec(
            num_scalar_prefetch=1, grid=(N_CHIPS, n),
            in_specs=[pl.BlockSpec((tm, D_MODEL), lambda q, i, order_ref: (jnp.where(q == 0, i, 0), 0)),
                      pl.BlockSpec((1, D_MODEL), lambda q, i, order_ref: (0, 0))] + cs["in_specs"],
            out_specs=[pl.BlockSpec((tm, W_IN_CHUNK), lambda q, i, order_ref: (i, order_ref[q]))] + cs["out_specs"],
            scratch_shapes=[pltpu.VMEM((n, tm, D_MODEL), bf16), pltpu.VMEM((D_MODEL, W_IN_CHUNK), bf16),
                            pltpu.SemaphoreType.DMA(())] + cs["scratch"]),
        out_shape=[jax.ShapeDtypeStruct((s, W_IN_COLS), f32)] + cs["out_shape"],
        input_output_aliases={3 + k: 1 + k for k in range(nc)},
        compiler_params=_cp("arbitrary", "arbitrary"),
    )(order, x, g_mix, *carry)
    return res[0], res[1:]


def _tap_offsets(lo, hi):
    groups = [[o for o in range(lo, hi + 1) if o % 8 == s] for s in range(8)]
    return [g for g in groups if g]


def _tap_sum(buf, w_ref, row0, cols, tap_of):
    acc = jnp.zeros((CONV_ROWS, cols.stop - cols.start), f32)
    for offs in _tap_offsets(0, CONV_K - 1):
        slab = buf[pl.ds(row0 + offs[0], offs[-1] - offs[0] + CONV_ROWS), cols]
        for o in offs:
            acc = acc + w_ref[pl.ds(tap_of(o), 1), cols] * slab[o - offs[0]:o - offs[0] + CONV_ROWS]
    return acc


def _pool_counts(tm, w, first_row):
    t1 = (lax.broadcasted_iota(jnp.int32, (tm, 1), 0) + first_row + 1).astype(f32)
    return jnp.minimum(t1, float(w))


def _conv_pool_out(z, x, conv_w, conv_b, ln_g, ln_b, pool_w, pool_scale, w_out, carry):
    s = x.shape[0]
    tm = min(ROW_TILE, s)
    n = s // tm
    hb = tm // HALO
    nc = len(carry)
    cs = _carried(carry)

    def body(z_ref, zp_ref, x_ref, cw_ref, cb_ref, lg_ref, lb_ref, pw_ref, ps_ref, wo_ref, *refs):
        x1_ref, mix_ref, u1_ref, pooled_ref = refs[nc:nc + 4]
        ubuf, vbuf = refs[2 * nc + 4:2 * nc + 6]
        gather = _Gather(refs[nc + 4:2 * nc + 4], *refs[2 * nc + 6:])
        i = pl.program_id(0)
        pl.when(i == 0)(gather.start)
        for k in range(nc):
            pl.when(i == min(n // 2 + 2 * k, n - 1))(functools.partial(gather.forward, [(k, 0), (k, 1)]))
        keep = (i > 0).astype(f32)
        zp = zp_ref[...] * keep
        ubuf[0:HALO, :] = zp[:, :C_CONV] * _sigmoid(zp[:, C_CONV:2 * C_CONV])
        vbuf[0:HALO, :] = zp[:, 2 * C_CONV:]
        ubuf[HALO:, :] = z_ref[:, :C_CONV] * _sigmoid(z_ref[:, C_CONV:2 * C_CONV])
        vbuf[HALO:, :] = z_ref[:, 2 * C_CONV:]
        off = HALO - (CONV_K - 1)
        for r0 in range(0, tm, CONV_ROWS):
            for cols in CONV_COLS:
                u1_ref[r0:r0 + CONV_ROWS, cols] = cb_ref[:, cols] + _tap_sum(ubuf, cw_ref, r0 + off, cols, lambda o: o)
        u1 = u1_ref[...]
        mu = jnp.mean(u1, axis=-1, keepdims=True)
        uc = u1 - mu
        rstd = lax.rsqrt(jnp.mean(uc * uc, axis=-1, keepdims=True) + EPS)
        u2 = uc * rstd * lg_ref[...] + lb_ref[...]
        mix_ref[:, :C_CONV] = (u2 * _sigmoid(u2)).astype(bf16)
        for g, w in enumerate(POOL_WINDOWS):
            cols = slice(g * POOL_GROUP, (g + 1) * POOL_GROUP)
            acc = vbuf[pl.ds(HALO, tm), cols]
            vg = acc
            for d in range(1, w):
                acc = acc + vbuf[pl.ds(HALO - d, tm), cols]
            pooled = (acc / _pool_counts(tm, w, i * tm) - vg).astype(bf16)
            pooled_ref[:, cols] = pooled
            mixed = _dot(pooled, pw_ref[g].astype(bf16))
            mix_ref[:, C_CONV + g * POOL_GROUP:C_CONV + (g + 1) * POOL_GROUP] = (mixed * ps_ref[:, cols]).astype(bf16)
        x1_ref[...] = x_ref[...] + _dot(mix_ref[...], wo_ref[...])
        @pl.when(i == n - 1)
        def _():
            gather.forward([(k, 2) for k in range(nc)])
            gather.finish()

    res = pl.pallas_call(
        body, name="conv_pool_out", grid=(n,),
        in_specs=[_row(0, (tm, W_IN_COLS)),
                  pl.BlockSpec((HALO, W_IN_COLS), lambda i: (jnp.maximum(i * hb - 1, 0), 0)),
                  _row(0, (tm, D_MODEL)), _full((HALO, C_CONV)), _full((1, C_CONV)), _full((1, C_CONV)),
                  _full((1, C_CONV)), _full((4, POOL_GROUP, POOL_GROUP)), _full((1, C_POOL)),
                  _full((D_MODEL, D_MODEL))] + cs["in_specs"],
        out_specs=[_row(0, (tm, D_MODEL)), _row(0, (tm, D_MODEL)), _row(0, (tm, C_CONV)), _row(0, (tm, C_POOL))]
        + cs["out_specs"],
        out_shape=[jax.ShapeDtypeStruct((s, D_MODEL), f32), jax.ShapeDtypeStruct((s, D_MODEL), bf16),
                   jax.ShapeDtypeStruct((s, C_CONV), f32), jax.ShapeDtypeStruct((s, C_POOL), bf16)] + cs["out_shape"],
        input_output_aliases={10 + k: 4 + k for k in range(nc)},
        scratch_shapes=[pltpu.VMEM((HALO + tm, C_CONV), f32), pltpu.VMEM((HALO + tm, C_POOL), f32)] + cs["scratch"],
        compiler_params=_cp("arbitrary"),
    )(z, z, x, conv_w, conv_b, ln_g, ln_b, pool_w, pool_scale, w_out, *carry)
    return res[:4], res[4:]


def _ffn_fwd(x1, g_ffn, w_gu_g, w_down, carry):
    s = x1.shape[0]
    tm = min(ROW_TILE, s)
    n = s // tm
    nc = len(carry)
    cs = _carried(carry)

    def body(x1_ref, g_ref, wg_ref, wu_ref, wd_ref, *refs):
        x2_ref, h2_ref, gu_ref, f_ref = refs[nc:nc + 4]
        acc_ref = refs[2 * nc + 4]
        gather = _Gather(refs[nc + 4:2 * nc + 4], *refs[2 * nc + 5:])
        i, c = pl.program_id(0), pl.program_id(1)
        pl.when(jnp.logical_and(i == 0, c == 0))(gather.start)
        pl.when(jnp.logical_and(i == n - 1, c == 0))(gather.forward)

        @pl.when(c == 0)
        def _():
            h, _, _ = _rms_fwd(x1_ref[...], g_ref[...])
            h2_ref[...] = h.astype(bf16)
            acc_ref[...] = jnp.zeros_like(acc_ref)

        h = h2_ref[...]
        for lo, hi in zip(FF_SUB[:-1], FF_SUB[1:]):
            gate = _dot(h, wg_ref[0, :, lo:hi])
            up = _dot(h, wu_ref[0, :, lo:hi])
            gu_ref[0, :, lo:hi] = gate.astype(bf16)
            gu_ref[1, :, lo:hi] = up.astype(bf16)
            f = (gate * _sigmoid(gate) * up).astype(bf16)
            f_ref[:, lo:hi] = f
            acc_ref[...] += _dot(f, wd_ref[lo:hi, :])

        @pl.when(c == 1)
        def _():
            x2_ref[...] = x1_ref[...] + acc_ref[...]

        pl.when(jnp.logical_and(i == n - 1, c == 1))(gather.finish)

    res = pl.pallas_call(
        body, name="ffn_fwd", grid=(n, 2),
        in_specs=[_row(0, (tm, D_MODEL)), _full((1, D_MODEL)),
                  pl.BlockSpec((1, D_MODEL, FF_CHUNK), lambda i, c: (c, 0, 0)),
                  pl.BlockSpec((1, D_MODEL, FF_CHUNK), lambda i, c: (2 + c, 0, 0)),
                  pl.BlockSpec((FF_CHUNK, D_MODEL), lambda i, c: (c, 0))] + cs["in_specs"],
        out_specs=[_row(0, (tm, D_MODEL)), _row(0, (tm, D_MODEL)),
                   pl.BlockSpec((2, tm, FF_CHUNK), lambda i, c: (0, i, c)),
                   pl.BlockSpec((tm, FF_CHUNK), lambda i, c: (i, c))] + cs["out_specs"],
        out_shape=[jax.ShapeDtypeStruct((s, D_MODEL), f32), jax.ShapeDtypeStruct((s, D_MODEL), bf16),
                   jax.ShapeDtypeStruct((2, s, D_FF), bf16), jax.ShapeDtypeStruct((s, D_FF), bf16)] + cs["out_shape"],
        input_output_aliases={5 + k: 4 + k for k in range(nc)},
        scratch_shapes=[pltpu.VMEM((tm, D_MODEL), f32)] + cs["scratch"],
        compiler_params=_cp("arbitrary", "arbitrary"),
    )(x1, g_ffn, w_gu_g, w_gu_g, w_down, *carry)
    return res[:4], res[4:]


def _ple_loss(x2, p, target, g_pg, g_post, g_final, w_pg, w_pu_g):
    s = x2.shape[0]
    tm = min(ROW_TILE, s)
    n = s // tm

    def body(x2_ref, p_ref, t_ref, gpg_ref, gpo_ref, gf_ref, wpg_ref, wpu_ref,
             dx2_ref, dwpg_ref, dwpu_ref, small_ref, apg_ref, apu_ref):
        i = pl.program_id(0)

        @pl.when(i == 0)
        def _():
            apg_ref[...] = jnp.zeros_like(apg_ref)
            apu_ref[...] = jnp.zeros_like(apu_ref)
            small_ref[...] = jnp.zeros_like(small_ref)

        x2 = x2_ref[...]
        h3, x2h, r2 = _rms_fwd(x2, gpg_ref[...])
        h3b = h3.astype(bf16)
        gate = _sigmoid(_dot(h3b, wpg_ref[...]))
        pb = p_ref[...].astype(bf16)
        pe = jnp.concatenate([_dot(pb, wpu_ref[j]) for j in range(N_CHIPS)], axis=-1)
        e, peh, rp = _rms_fwd(pe, gpo_ref[...])
        x3 = x2 + gate * e
        y, x3h, r3 = _rms_fwd(x3, gf_ref[...])
        d = y - t_ref[...]
        loss = 0.5 * jnp.sum(jnp.sum(d * d, axis=-1, keepdims=True) * (1.0 / D_MODEL), axis=0, keepdims=True)
        dx3, dgf = _rms_bwd(d * (1.0 / D_MODEL), x3h, r3, gf_ref[...])
        dpe, dgpo = _rms_bwd(dx3 * gate, peh, rp, gpo_ref[...])
        dgl = (dx3 * e * gate * (1.0 - gate)).astype(bf16)
        apg_ref[...] += _dot_tn(h3b, dgl)
        apu_ref[...] += _dot_tn(pb, dpe.astype(bf16))
        dh3 = _dot_nt(dgl, wpg_ref[...])
        dx2b, dgpg = _rms_bwd(dh3, x2h, r2, gpg_ref[...])
        dx2_ref[...] = dx3 + dx2b
        small_ref[0:1, :] += dgpg
        small_ref[1:2, :] += dgpo
        small_ref[2:3, :] += dgf
        small_ref[3:4, :] += jnp.broadcast_to(loss, (1, D_MODEL))

        @pl.when(i == n - 1)
        def _():
            dwpg_ref[...] = apg_ref[...].astype(bf16)
            for j in range(N_CHIPS):
                dwpu_ref[j] = apu_ref[:, j * PLE_CHUNK:(j + 1) * PLE_CHUNK].astype(bf16)

    return pl.pallas_call(
        body, name="ple_loss", grid=(n,),
        in_specs=[_row(0, (tm, D_MODEL)), _row(0, (tm, D_PLE)), _row(0, (tm, D_MODEL)),
                  _full((1, D_MODEL)), _full((1, D_MODEL)), _full((1, D_MODEL)),
                  _full((D_MODEL, D_MODEL)), _full((N_CHIPS, D_PLE, PLE_CHUNK))],
        out_specs=[_row(0, (tm, D_MODEL)), _full((D_MODEL, D_MODEL)), _full((N_CHIPS, D_PLE, PLE_CHUNK)),
                   _full((8, D_MODEL))],
        out_shape=[jax.ShapeDtypeStruct((s, D_MODEL), f32), jax.ShapeDtypeStruct((D_MODEL, D_MODEL), bf16),
                   jax.ShapeDtypeStruct((N_CHIPS, D_PLE, PLE_CHUNK), bf16), jax.ShapeDtypeStruct((8, D_MODEL), f32)],
        scratch_shapes=[pltpu.VMEM((D_MODEL, D_MODEL), f32), pltpu.VMEM((D_PLE, D_MODEL), f32)],
        compiler_params=_cp("arbitrary"),
    )(x2, p, target, g_pg, g_post, g_final, w_pg, w_pu_g)


def _crossed(parts):
    n = len(parts)
    return dict(in_specs=[ANY] * n, out_specs=[ANY] * n,
                out_shape=[jax.ShapeDtypeStruct(a.shape, a.dtype) for a in parts], scratch=_Cross.scratch(n))


def _ffn_bwd_dx(dx2, x1, gu, g_ffn, w_gu_g, w_down, parts):
    s = x1.shape[0]
    tm = min(ROW_TILE, s)
    n = s // tm
    nc = len(parts)
    cs = _crossed(parts)

    def body(dx2_ref, x1_ref, gu_ref, g_ref, wg_ref, wu_ref, wd_ref, *refs):
        dx1_ref, dgu_ref, small_ref = refs[nc:nc + 3]
        acc_ref = refs[2 * nc + 3]
        cross = _Cross(refs[:nc], refs[nc + 3:2 * nc + 3], *refs[2 * nc + 4:])
        i, c = pl.program_id(0), pl.program_id(1)
        pl.when(jnp.logical_and(i == 0, c == 0))(cross.start)

        @pl.when(jnp.logical_and(i == 0, c == 0))
        def _():
            small_ref[...] = jnp.zeros_like(small_ref)

        @pl.when(c == 0)
        def _():
            acc_ref[...] = jnp.zeros_like(acc_ref)

        dyb = dx2_ref[...].astype(bf16)
        for lo, hi in zip(FF_SUB[:-1], FF_SUB[1:]):
            df = _dot_nt(dyb, wd_ref[lo:hi, :])
            gate = gu_ref[0, :, lo:hi].astype(f32)
            up = gu_ref[1, :, lo:hi].astype(f32)
            sg = _sigmoid(gate)
            dgate = (df * up * _silu_grad(gate, sg)).astype(bf16)
            dup = (df * gate * sg).astype(bf16)
            dgu_ref[0, :, lo:hi] = dgate
            dgu_ref[1, :, lo:hi] = dup
            acc_ref[...] += _dot_nt(dgate, wg_ref[0, :, lo:hi]) + _dot_nt(dup, wu_ref[0, :, lo:hi])

        @pl.when(c == 1)
        def _():
            _, x1h, r1 = _rms_fwd(x1_ref[...], g_ref[...])
            dx1b, dg = _rms_bwd(acc_ref[...], x1h, r1, g_ref[...])
            dx1_ref[...] = dx2_ref[...] + dx1b
            small_ref[0:1, :] += dg

        pl.when(jnp.logical_and(i == n - 1, c == 1))(cross.finish)

    res = pl.pallas_call(
        body, name="ffn_bwd_dx", grid=(n, 2),
        in_specs=[_row(0, (tm, D_MODEL)), _row(0, (tm, D_MODEL)),
                  pl.BlockSpec((2, tm, FF_CHUNK), lambda i, c: (0, i, c)), _full((1, D_MODEL)),
                  pl.BlockSpec((1, D_MODEL, FF_CHUNK), lambda i, c: (c, 0, 0)),
                  pl.BlockSpec((1, D_MODEL, FF_CHUNK), lambda i, c: (2 + c, 0, 0)),
                  pl.BlockSpec((FF_CHUNK, D_MODEL), lambda i, c: (c, 0))] + cs["in_specs"],
        out_specs=[_row(0, (tm, D_MODEL)), pl.BlockSpec((2, tm, FF_CHUNK), lambda i, c: (0, i, c)),
                   _full((8, D_MODEL))] + cs["out_specs"],
        out_shape=[jax.ShapeDtypeStruct((s, D_MODEL), f32), jax.ShapeDtypeStruct((2, s, D_FF), bf16),
                   jax.ShapeDtypeStruct((8, D_MODEL), f32)] + cs["out_shape"],
        scratch_shapes=[pltpu.VMEM((tm, D_MODEL), f32)] + cs["scratch"],
        compiler_params=_cp("arbitrary", "arbitrary"),
    )(dx2, x1, gu, g_ffn, w_gu_g, w_gu_g, w_down, *parts)
    return res[:3], res[3:]


def _ffn_bwd_dw_gu(h2, dgu):
    s = h2.shape[0]
    ts = min(2 * ROW_TILE, s)
    n = s // ts

    def body(h_ref, d_ref, o_ref, acc_ref):
        t = pl.program_id(1)

        @pl.when(t == 0)
        def _():
            acc_ref[...] = jnp.zeros_like(acc_ref)

        acc_ref[...] += _dot_tn(h_ref[...], d_ref[0])

        @pl.when(t == n - 1)
        def _():
            o_ref[0] = acc_ref[...].astype(bf16)

    return pl.pallas_call(
        body, name="ffn_bwd_dw_gu", grid=(N_CHIPS, n),
        in_specs=[pl.BlockSpec((ts, D_MODEL), lambda j, t: (t, 0)),
                  pl.BlockSpec((1, ts, FF_CHUNK), lambda j, t: (j // 2, t, j % 2))],
        out_specs=pl.BlockSpec((1, D_MODEL, FF_CHUNK), lambda j, t: (j, 0, 0)),
        out_shape=jax.ShapeDtypeStruct((N_CHIPS, D_MODEL, FF_CHUNK), bf16),
        scratch_shapes=[pltpu.VMEM((D_MODEL, FF_CHUNK), f32)],
        compiler_params=_cp("parallel", "arbitrary"),
    )(h2, dgu)


def _ffn_bwd_dw_down(f, dx2):
    s = dx2.shape[0]
    ts = min(2 * ROW_TILE, s)
    n = s // ts

    def body(f_ref, d_ref, o_ref, acc_ref):
        t = pl.program_id(1)

        @pl.when(t == 0)
        def _():
            acc_ref[...] = jnp.zeros_like(acc_ref)

        acc_ref[...] += _dot_tn(f_ref[...], d_ref[...].astype(bf16))

        @pl.when(t == n - 1)
        def _():
            o_ref[...] = acc_ref[...].astype(bf16)

    return pl.pallas_call(
        body, name="ffn_bwd_dw_down", grid=(2, n),
        in_specs=[pl.BlockSpec((ts, FF_CHUNK), lambda c, t: (t, c)),
                  pl.BlockSpec((ts, D_MODEL), lambda c, t: (t, 0))],
        out_specs=pl.BlockSpec((FF_CHUNK, D_MODEL), lambda c, t: (c, 0)),
        out_shape=jax.ShapeDtypeStruct((D_FF, D_MODEL), bf16),
        scratch_shapes=[pltpu.VMEM((FF_CHUNK, D_MODEL), f32)],
        compiler_params=_cp("parallel", "arbitrary"),
    )(f, dx2)


def _mix_bwd_local(dx1, mix, u1, pooled, w_out, ln_g, ln_b, pool_w, pool_scale):
    s = dx1.shape[0]
    tm = min(ROW_TILE, s)
    n = s // tm

    def body(dx1_ref, mix_ref, u1_ref, po_ref, wo_ref, lg_ref, lb_ref, pw_ref, ps_ref,
             du1_ref, dpo_ref, dwo_ref, dpw_ref, small_ref, awo_ref):
        i = pl.program_id(0)

        @pl.when(i == 0)
        def _():
            awo_ref[...] = jnp.zeros_like(awo_ref)
            dpw_ref[...] = jnp.zeros_like(dpw_ref)
            small_ref[...] = jnp.zeros_like(small_ref)

        dyb = dx1_ref[...].astype(bf16)
        dmix = _dot_nt(dyb, wo_ref[...])
        awo_ref[...] += _dot_tn(mix_ref[...], dyb)
        u1 = u1_ref[...]
        mu = jnp.mean(u1, axis=-1, keepdims=True)
        uc = u1 - mu
        rstd = lax.rsqrt(jnp.mean(uc * uc, axis=-1, keepdims=True) + EPS)
        uh = uc * rstd
        u2 = uh * lg_ref[...] + lb_ref[...]
        du2 = dmix[:, :C_CONV] * _silu_grad(u2, _sigmoid(u2))
        duh = du2 * lg_ref[...]
        du1 = rstd * (duh - jnp.mean(duh, axis=-1, keepdims=True) - uh * jnp.mean(duh * uh, axis=-1, keepdims=True))
        du1_ref[...] = du1
        small_ref[0:1, :] += jnp.sum(du2 * uh, axis=0, keepdims=True)
        small_ref[1:2, :] += jnp.sum(du2, axis=0, keepdims=True)
        small_ref[2:3, :] += jnp.sum(du1, axis=0, keepdims=True)
        for g in range(len(POOL_WINDOWS)):
            cols = slice(g * POOL_GROUP, (g + 1) * POOL_GROUP)
            dq = dmix[:, C_CONV + g * POOL_GROUP:C_CONV + (g + 1) * POOL_GROUP]
            pwb = pw_ref[g].astype(bf16)
            pg = po_ref[:, cols]
            mixed = _dot(pg, pwb)
            small_ref[3:4, cols] += jnp.sum(dq * mixed, axis=0, keepdims=True)
            dmixed = (dq * ps_ref[:, cols]).astype(bf16)
            dpw_ref[g] += _dot_tn(pg, dmixed)
            dpo_ref[:, cols] = _dot_nt(dmixed, pwb)

        @pl.when(i == n - 1)
        def _():
            dwo_ref[...] = awo_ref[...].astype(bf16)

    return pl.pallas_call(
        body, name="mix_bwd_local", grid=(n,),
        in_specs=[_row(0, (tm, D_MODEL)), _row(0, (tm, D_MODEL)), _row(0, (tm, C_CONV)), _row(0, (tm, C_POOL)),
                  _full((D_MODEL, D_MODEL)), _full((1, C_CONV)), _full((1, C_CONV)),
                  _full((4, POOL_GROUP, POOL_GROUP)), _full((1, C_POOL))],
        out_specs=[_row(0, (tm, C_CONV)), _row(0, (tm, C_POOL)), _full((D_MODEL, D_MODEL)),
                   _full((4, POOL_GROUP, POOL_GROUP)), _full((8, C_CONV))],
        out_shape=[jax.ShapeDtypeStruct((s, C_CONV), f32), jax.ShapeDtypeStruct((s, C_POOL), f32),
                   jax.ShapeDtypeStruct((D_MODEL, D_MODEL), bf16),
                   jax.ShapeDtypeStruct((4, POOL_GROUP, POOL_GROUP), f32), jax.ShapeDtypeStruct((8, C_CONV), f32)],
        scratch_shapes=[pltpu.VMEM((D_MODEL, D_MODEL), f32)],
        compiler_params=_cp("arbitrary"),
    )(dx1, mix, u1, pooled, w_out, ln_g, ln_b, pool_w, pool_scale)


def _in_bwd(du1, dpo, z, x, dx1, conv_w, g_mix, w_in_g, parts):
    s = x.shape[0]
    tm = min(ROW_TILE, s)
    n = s // tm
    hb = tm // HALO
    last = s // HALO - 1
    nv = tm // 8
    assert nv >= SHIFT_PAD and nv % SHIFT_GROUPS == 0
    nc = len(parts)
    cs = _crossed(parts)

    def body(du_ref, dun_ref, dp_ref, dpn_ref, z_ref, zp_ref, x_ref, dx1_ref, cw_ref, g_ref, w_ref, *refs):
        gx_ref, dw_ref, dcw_ref, small_ref = refs[nc:nc + 4]
        eu, ed, ep, ss, u0_ref, dz_ref, acc_ref, dcw_acc = refs[2 * nc + 4:2 * nc + 12]
        cross = _Cross(refs[:nc], refs[nc + 4:2 * nc + 4], *refs[2 * nc + 12:])
        i = pl.program_id(0)
        pl.when(i == 0)(cross.start)

        @pl.when(i == 0)
        def _():
            acc_ref[...] = jnp.zeros_like(acc_ref)
            dcw_acc[...] = jnp.zeros_like(dcw_acc)
            small_ref[...] = jnp.zeros_like(small_ref)

        keep_prev = (i > 0).astype(f32)
        keep_next = (i < n - 1).astype(f32)
        zp = zp_ref[...] * keep_prev
        u0_prev = zp[:, :C_CONV] * _sigmoid(zp[:, C_CONV:2 * C_CONV])
        u0_ref[...] = z_ref[:, :C_CONV] * _sigmoid(z_ref[:, C_CONV:2 * C_CONV])
        du_next = dun_ref[...] * keep_next
        for c, w_pool in enumerate(POOL_WINDOWS):
            lanes = slice(c * 128, (c + 1) * 128)
            eu[c, pl.ds(0, SHIFT_PAD, stride=8), :] = u0_prev[:, lanes]
            ed[c, pl.ds(nv * 8 + 7, SHIFT_PAD, stride=8), :] = du_next[:, lanes]
            ep[c, pl.ds(nv * 8 + 7, SHIFT_PAD, stride=8), :] = (
                dpn_ref[:, lanes] * keep_next / _pool_counts(HALO, w_pool, (i + 1) * tm))
            for j in range(8):
                rows = slice(j * nv, (j + 1) * nv)
                eu[c, pl.ds(SHIFT_PAD * 8 + j, nv, stride=8), :] = u0_ref[rows, lanes]
                ed[c, pl.ds(j, nv, stride=8), :] = du_ref[rows, lanes]
                ep[c, pl.ds(j, nv, stride=8), :] = dp_ref[rows, lanes] / _pool_counts(nv, w_pool, i * tm + j * nv)
                if j >= 1:
                    eu[c, pl.ds(j, SHIFT_PAD, stride=8), :] = u0_ref[j * nv - SHIFT_PAD:j * nv, lanes]
                if j <= 6:
                    edge = slice((j + 1) * nv, (j + 1) * nv + SHIFT_PAD)
                    ed[c, pl.ds(nv * 8 + j, SHIFT_PAD, stride=8), :] = du_ref[edge, lanes]
                    ep[c, pl.ds(nv * 8 + j, SHIFT_PAD, stride=8), :] = (
                        dp_ref[edge, lanes] / _pool_counts(SHIFT_PAD, w_pool, i * tm + (j + 1) * nv))
        for c, w_pool in enumerate(POOL_WINDOWS):
            lanes = slice(c * 128, (c + 1) * 128)
            b_lanes = slice(C_CONV + c * 128, C_CONV + (c + 1) * 128)
            v_lanes = slice(2 * C_CONV + c * 128, 2 * C_CONV + (c + 1) * 128)
            for v0 in range(0, nv, SHIFT_GROUPS):
                span = SHIFT_GROUPS * 8
                acc = jnp.zeros((span, 128), f32)
                for k in range(CONV_K):
                    acc = acc + cw_ref[pl.ds(k, 1), lanes] * ed[c, pl.ds((v0 + CONV_K - 1 - k) * 8, span), :]
                ss[0, v0 * 8:v0 * 8 + span, :] = acc
                acc = ep[c, pl.ds(v0 * 8, span), :]
                for d in range(1, w_pool):
                    acc = acc + ep[c, pl.ds((v0 + d) * 8, span), :]
                ss[1, v0 * 8:v0 * 8 + span, :] = acc
                d1 = ed[c, pl.ds(v0 * 8, span), :]
                for k in range(CONV_K):
                    prod = d1 * eu[c, pl.ds((SHIFT_PAD - (CONV_K - 1) + v0 + k) * 8, span), :]
                    fold = prod[0:8]
                    for r in range(8, span, 8):
                        fold = fold + prod[r:r + 8]
                    dcw_acc[k, :, lanes] += fold
            for j in range(8):
                rows = slice(j * nv, (j + 1) * nv)
                du0 = ss[0, pl.ds(j, nv, stride=8), :]
                av, sv = z_ref[rows, lanes], _sigmoid(z_ref[rows, b_lanes])
                dz_ref[rows, lanes] = (du0 * sv).astype(bf16)
                dz_ref[rows, b_lanes] = (du0 * av * sv * (1.0 - sv)).astype(bf16)
                dz_ref[rows, v_lanes] = (ss[1, pl.ds(j, nv, stride=8), :] - dp_ref[rows, lanes]).astype(bf16)
        h, xh, r = _rms_fwd(x_ref[...], g_ref[...])
        dz = dz_ref[...]
        acc_ref[...] += _dot_tn(h.astype(bf16), dz)
        dh = _dot_nt(dz[:, 0:W_IN_CHUNK], w_ref[0])
        for j in range(1, N_CHIPS):
            dh = dh + _dot_nt(dz[:, j * W_IN_CHUNK:(j + 1) * W_IN_CHUNK], w_ref[j])
        dxb, dg = _rms_bwd(dh, xh, r, g_ref[...])
        gx_ref[...] = dx1_ref[...] + dxb
        small_ref[0:1, :] += dg

        @pl.when(i == n - 1)
        def _():
            for j in range(N_CHIPS):
                dw_ref[j] = acc_ref[:, j * W_IN_CHUNK:(j + 1) * W_IN_CHUNK].astype(bf16)
            dcw_ref[...] = jnp.sum(dcw_acc[...], axis=1)

        pl.when(i == n - 1)(cross.finish)

    nxt = lambda i: (jnp.minimum((i + 1) * hb, last), 0)
    res = pl.pallas_call(
        body, name="in_bwd", grid=(n,),
        in_specs=[_row(0, (tm, C_CONV)), pl.BlockSpec((HALO, C_CONV), nxt),
                  _row(0, (tm, C_POOL)), pl.BlockSpec((HALO, C_POOL), nxt),
                  _row(0, (tm, W_IN_COLS)),
                  pl.BlockSpec((HALO, W_IN_COLS), lambda i: (jnp.maximum(i * hb - 1, 0), 0)),
                  _row(0, (tm, D_MODEL)), _row(0, (tm, D_MODEL)), _full((HALO, C_CONV)), _full((1, D_MODEL)),
                  _full((N_CHIPS, D_MODEL, W_IN_CHUNK))] + cs["in_specs"],
        out_specs=[_row(0, (tm, D_MODEL)), _full((N_CHIPS, D_MODEL, W_IN_CHUNK)), _full((HALO, C_CONV)),
                   _full((8, D_MODEL))] + cs["out_specs"],
        out_shape=[jax.ShapeDtypeStruct((s, D_MODEL), f32), jax.ShapeDtypeStruct((N_CHIPS, D_MODEL, W_IN_CHUNK), bf16),
                   jax.ShapeDtypeStruct((HALO, C_CONV), f32), jax.ShapeDtypeStruct((8, D_MODEL), f32)] + cs["out_shape"],
        scratch_shapes=[pltpu.VMEM((4, (SHIFT_PAD + nv) * 8, 128), f32), pltpu.VMEM((4, (nv + SHIFT_PAD) * 8, 128), f32),
                        pltpu.VMEM((4, (nv + SHIFT_PAD) * 8, 128), f32), pltpu.VMEM((2, tm, 128), f32),
                        pltpu.VMEM((tm, C_CONV), f32), pltpu.VMEM((tm, W_IN_COLS), bf16),
                        pltpu.VMEM((D_MODEL, W_IN_COLS), f32), pltpu.VMEM((HALO, 8, C_CONV), f32)] + cs["scratch"],
        compiler_params=_cp("arbitrary"),
    )(du1, du1, dpo, dpo, z, z, x, dx1, conv_w, g_mix, w_in_g, *parts)
    return res[:4], res[4:]


def _rs_pair(name, grads):
    nk = len(grads)
    halves = [g.shape[1] // 2 for g in grads]

    def body(*refs):
        ins, outs = refs[:nk], refs[nk:2 * nk]
        send_sems, recv_sems = refs[2 * nk:]
        x, y, c, _, _ = _place()
        cps = [_remote(ins[k].at[:, pl.ds((1 - c) * halves[k], halves[k]), :], outs[k],
                       send_sems.at[k], recv_sems.at[k], (x, y, 1 - c)) for k in range(nk)]
        for cp in cps:
            cp.start()
        for cp in cps:
            cp.wait()

    return pl.pallas_call(
        body, name=f"rs_pair_{name}", in_specs=[ANY] * nk, out_specs=[ANY] * nk,
        out_shape=[jax.ShapeDtypeStruct((N_CHIPS, h, g.shape[2]), bf16) for g, h in zip(grads, halves)],
        scratch_shapes=[pltpu.SemaphoreType.DMA((nk,)), pltpu.SemaphoreType.DMA((nk,))],
    )(*grads)


def _rs_add_pair(name, core, grad, recv):
    _, half, cols = recv.shape

    def body(core_ref, g_ref, r_ref, o_ref):
        o_ref[...] = (g_ref[...].astype(f32) + r_ref[...].astype(f32)).astype(bf16)

    return pl.pallas_call(
        body, name=f"rs_add_pair_{name}",
        grid_spec=pltpu.PrefetchScalarGridSpec(
            num_scalar_prefetch=1, grid=(N_CHIPS,),
            in_specs=[pl.BlockSpec((1, half, cols), lambda j, core_ref: (j, core_ref[0], 0)),
                      pl.BlockSpec((1, half, cols), lambda j, core_ref: (j, 0, 0))],
            out_specs=pl.BlockSpec((1, half, cols), lambda j, core_ref: (j, 0, 0))),
        out_shape=jax.ShapeDtypeStruct(recv.shape, bf16),
        compiler_params=_cp("parallel"),
    )(core, grad, recv)


class _Cross:
    def __init__(self, parts, landed, send_sems, recv_sems):
        self.parts, self.landed, self.send_sems, self.recv_sems = parts, landed, send_sems, recv_sems
        _, _, self.c, self.me, self.others = _place()

    def _copy(self, k, j, src_chunk, dst_slot):
        ox, oy = self.others[j]
        return _remote(self.parts[k].at[src_chunk], self.landed[k].at[dst_slot], self.send_sems.at[3 * k + j],
                       self.recv_sems.at[3 * k + j], (ox, oy, self.c))

    def _each(self):
        return [(k, j, 2 * self.others[j][0] + self.others[j][1]) for k in range(len(self.parts)) for j in range(3)]

    def start(self):
        for k, j, chip in self._each():
            self._copy(k, j, chip, self.me).start()

    def finish(self):
        for k, j, chip in self._each():
            self._copy(k, j, chip, chip).wait_recv()
        for k, j, chip in self._each():
            self._copy(k, j, chip, self.me).wait_send()

    @staticmethod
    def scratch(n):
        return [pltpu.SemaphoreType.DMA((3 * n,)), pltpu.SemaphoreType.DMA((3 * n,))]


def _rs_sum_chips(name, place, landed, part):
    _, half, cols = landed.shape

    def body(place_ref, l_ref, p_ref, o_ref):
        me = place_ref[0]
        own = p_ref[0].astype(f32)
        acc = jnp.where(me == 0, own, l_ref[0].astype(f32))
        for j in range(1, N_CHIPS):
            acc = acc + jnp.where(me == j, own, l_ref[j].astype(f32))
        o_ref[...] = acc

    return pl.pallas_call(
        body, name=f"rs_sum_chips_{name}",
        grid_spec=pltpu.PrefetchScalarGridSpec(
            num_scalar_prefetch=1, grid=(1,),
            in_specs=[pl.BlockSpec((N_CHIPS, half, cols), lambda t, place_ref: (0, 0, 0)),
                      pl.BlockSpec((1, half, cols), lambda t, place_ref: (place_ref[0], 0, 0))],
            out_specs=pl.BlockSpec((half, cols), lambda t, place_ref: (place_ref[1], 0))),
        out_shape=jax.ShapeDtypeStruct((2 * half, cols), f32),
        compiler_params=_cp("arbitrary"),
    )(place, landed, part)


def _grad_pair(shards):
    nk = len(shards)

    def body(*refs):
        outs = refs[nk:2 * nk]
        send_sems, recv_sems = refs[2 * nk:]
        x, y, c, _, _ = _place()

        def half(k, core):
            h = outs[k].shape[0] // 2
            return outs[k].at[pl.ds(core * h, h), :]

        cps = [_remote(half(k, c), half(k, c), send_sems.at[k], recv_sems.at[k], (x, y, 1 - c)) for k in range(nk)]
        for cp in cps:
            cp.start()
        for k in range(nk):
            _remote(half(k, 1 - c), half(k, 1 - c), send_sems.at[k], recv_sems.at[k], (x, y, 1 - c)).wait_recv()
        for cp in cps:
            cp.wait_send()

    return pl.pallas_call(
        body, name="grad_pair", in_specs=[ANY] * nk, out_specs=[ANY] * nk,
        out_shape=[jax.ShapeDtypeStruct(a.shape, f32) for a in shards],
        input_output_aliases={k: k for k in range(nk)},
        scratch_shapes=[pltpu.SemaphoreType.DMA((nk,)), pltpu.SemaphoreType.DMA((nk,))],
    )(*shards)


def _adam_math(w, g, m, v):
    m = ADAM_B1 * m + (1.0 - ADAM_B1) * g
    v = ADAM_B2 * v + (1.0 - ADAM_B2) * (g * g)
    m_hat = m / (1.0 - ADAM_B1 ** ADAM_STEP)
    v_hat = v / (1.0 - ADAM_B2 ** ADAM_STEP)
    delta = -ADAM_LR * (m_hat / (jnp.sqrt(v_hat) + ADAM_EPS) + ADAM_WD * w)
    return delta, m, v


def _adam(k, w, g, m, v):
    rows, cols = w.shape
    tr = next(rows // d for d in (1, 2, 4, 8) if rows % (8 * d) == 0 and rows // d <= 256)

    def body(w_ref, g_ref, m_ref, v_ref, d_ref, nm_ref, nv_ref):
        d_ref[...], nm_ref[...], nv_ref[...] = _adam_math(w_ref[...], g_ref[...], m_ref[...], v_ref[...])

    spec = _row(0, (tr, cols))
    return pl.pallas_call(
        body, name=f"adam_{k}", grid=(rows // tr,), in_specs=[spec] * 4, out_specs=[spec] * 3,
        out_shape=[jax.ShapeDtypeStruct(w.shape, f32)] * 3,
        compiler_params=_cp("parallel"),
    )(w, g, m, v)


class _SmallGather:
    def __init__(self, part, buf, send_sems, recv_sems):
        self.part, self.buf, self.send_sems, self.recv_sems = part, buf, send_sems, recv_sems
        self.x, self.y, self.c, _, self.others = _place()
        self.sibling = (self.x, self.y, 1 - self.c)

    def _copy(self, k, block, to, src=None):
        slot = self.buf.at[4 * block[0] + 2 * block[1] + block[2]]
        return _remote(slot if src is None else src, slot, self.send_sems.at[k], self.recv_sems.at[k], to)

    def _first(self):
        me = (self.x, self.y, self.c)
        return [self._copy(0, me, self.sibling, src=self.part)] + [
            self._copy(1 + j, me, (*chip, self.c), src=self.part) for j, chip in enumerate(self.others)]

    def _passed(self):
        return [self._copy(4 + j, (*chip, self.c), self.sibling) for j, chip in enumerate(self.others)]

    def start(self):
        for cp in self._first():
            cp.start()

    def finish(self):
        passed = self._passed()
        for j, chip in enumerate(self.others):
            self._copy(1 + j, (*chip, self.c), self.sibling).wait_recv()
            passed[j].start()
        self._copy(0, self.sibling, self.sibling).wait_recv()
        for j, chip in enumerate(self.others):
            self._copy(4 + j, (*chip, 1 - self.c), self.sibling).wait_recv()
        for cp in self._first() + passed:
            cp.wait_send()


def _rs_sum_group(name, place, landed, parts, cross_parts, small_part):
    nk, nx = len(landed), len(cross_parts)
    dims = [a.shape[1:] for a in landed]
    rows = small_part.shape[0]

    def body(place_ref, *refs):
        l_refs, p_refs = refs[:nk], refs[nk:2 * nk]
        x_refs, sp_ref = refs[2 * nk:2 * nk + nx], refs[2 * nk + nx]
        outs = refs[2 * nk + nx + 1:]
        o_refs, xl_refs, sbuf = outs[:nk], outs[nk:nk + nx], outs[nk + nx]
        sems = outs[nk + nx + 1:]
        cross = _Cross(x_refs, xl_refs, sems[0], sems[1])
        small = _SmallGather(sp_ref, sbuf, sems[2], sems[3])
        t = pl.program_id(0)

        @pl.when(t == 0)
        def _():
            cross.start()
            small.start()

        me = place_ref[0]
        for l_ref, p_ref, o_ref in zip(l_refs, p_refs, o_refs):
            own = p_ref[0].astype(f32)
            acc = jnp.where(me == 0, own, l_ref[0].astype(f32))
            for j in range(1, N_CHIPS):
                acc = acc + jnp.where(me == j, own, l_ref[j].astype(f32))
            o_ref[...] = acc

        @pl.when(t == 1)
        def _():
            small.finish()
            cross.finish()

    def halves(h, c, lead, index):
        return pl.BlockSpec((lead, h // 2, c) if lead else (h // 2, c), index)

    in_specs = [halves(h, c, N_CHIPS, lambda t, pr: (0, t, 0)) for h, c in dims]
    in_specs += [halves(h, c, 1, lambda t, pr: (pr[0], t, 0)) for h, c in dims]
    in_specs += [ANY] * (nx + 1)
    out_specs = [halves(h, c, 0, lambda t, pr: (2 * pr[1] + t, 0)) for h, c in dims] + [ANY] * (nx + 1)
    out_shape = [jax.ShapeDtypeStruct((2 * h, c), f32) for h, c in dims]
    out_shape += [jax.ShapeDtypeStruct(a.shape, a.dtype) for a in cross_parts]
    out_shape.append(jax.ShapeDtypeStruct((N_DEV, rows, 128), f32))
    res = pl.pallas_call(
        body, name=f"rs_sum_{name}",
        grid_spec=pltpu.PrefetchScalarGridSpec(
            num_scalar_prefetch=1, grid=(2,), in_specs=in_specs, out_specs=out_specs,
            scratch_shapes=_Cross.scratch(nx) + [pltpu.SemaphoreType.DMA((7,)), pltpu.SemaphoreType.DMA((7,))]),
        out_shape=out_shape, compiler_params=_cp("arbitrary"),
    )(place, *landed, *parts, *cross_parts, small_part)
    return res[:nk], res[nk:nk + nx], res[nk + nx]


def _small_adam(dev, gathered, part, w, m, v):
    def body(dev_ref, b_ref, p_ref, w_ref, m_ref, v_ref, g_ref, d_ref, nm_ref, nv_ref):
        me = dev_ref[0]
        own = p_ref[...]
        g = jnp.where(me == 0, own, b_ref[0])
        for d in range(1, N_DEV):
            g = g + jnp.where(me == d, own, b_ref[d])
        g_ref[...] = g
        d_ref[...], nm_ref[...], nv_ref[...] = _adam_math(w_ref[...], g, m_ref[...], v_ref[...])

    shape = jax.ShapeDtypeStruct(part.shape, f32)
    flat = pl.BlockSpec(part.shape, lambda t, dev_ref: (0, 0))
    return pl.pallas_call(
        body, name="small_adam",
        grid_spec=pltpu.PrefetchScalarGridSpec(
            num_scalar_prefetch=1, grid=(1,),
            in_specs=[pl.BlockSpec(gathered.shape, lambda t, dev_ref: (0, 0, 0))] + [flat] * 4, out_specs=[flat] * 4),
        out_shape=[shape] * 4, compiler_params=_cp("arbitrary"),
    )(dev, gathered, part, w, m, v)


def _rows128(a):
    return a.reshape(-1, 128)


def _pad_rows(a, rows):
    return jnp.concatenate([a, jnp.zeros((rows - a.shape[0],) + a.shape[1:], a.dtype)], axis=0)


def _pack_small(me, g_pg, g_post, g_final, g_ffn, ln_g, ln_b, conv_b, pool_scale, pool_w, conv_w, g_mix):
    blk_ple = _pad_rows(jnp.concatenate([g_pg, g_post, g_final.reshape(1, D_MODEL)], axis=0), 8)
    blk_ffn = _pad_rows(g_ffn, 8)
    blk_mix = _pad_rows(jnp.concatenate([ln_g, ln_b, conv_b, pool_scale], axis=0), 8)
    cw = lax.dynamic_update_slice(jnp.zeros((N_CHIPS, HALO, 128), f32), _pad_rows(conv_w, HALO)[None], (me, 0, 0))
    blk_in = _pad_rows(g_mix, 8)
    return jnp.concatenate([_rows128(blk_ple), _rows128(blk_ffn), _rows128(blk_mix), _rows128(pool_w),
                            _rows128(cw), _rows128(blk_in)], axis=0)


def _unpack_small(me, packed):
    o = 0

    def take(rows):
        nonlocal o
        blk = packed[o:o + rows]
        o += rows
        return blk

    ple = take(64).reshape(8, D_MODEL)
    ffn = take(64).reshape(8, D_MODEL)
    mixb = take(32).reshape(8, C_CONV)
    pool_w = take(512).reshape(1, 4, POOL_GROUP, POOL_GROUP)
    cw = take(N_CHIPS * HALO).reshape(N_CHIPS, HALO, 128)
    inb = take(64).reshape(8, D_MODEL)
    conv_w = lax.dynamic_slice(cw, (me, 0, 0), (1, HALO, 128))[:, :CONV_K, :]
    return dict(g_ple_gate=ple[0:1], g_ple_post=ple[1:2], g_final=ple[2], loss=ple[3, 0], g_ffn=ffn[0:1],
                ln_g=mixb[0:1], ln_b=mixb[1:2], conv_b=mixb[2:3], pool_scale=mixb[3:4], pool_w=pool_w,
                conv_w=conv_w, g_mix=inb[0:1])


def kernel(x, p, g_mix, w_in, conv_w, conv_b, ln_g, ln_b, pool_w, pool_scale, w_out, g_ffn, w_gate_up, w_down, g_ple_gate, w_ple_gate, w_ple_up, g_ple_post, g_final, loss_target, m_g_mix, m_w_in, m_conv_w, m_conv_b, m_ln_g, m_ln_b, m_pool_w, m_pool_scale, m_w_out, m_g_ffn, m_w_gate_up, m_w_down, m_g_ple_gate, m_w_ple_gate, m_w_ple_up, m_g_ple_post, m_g_final, v_g_mix, v_w_in, v_conv_w, v_conv_b, v_ln_g, v_ln_b, v_pool_w, v_pool_scale, v_w_out, v_g_ffn, v_w_gate_up, v_w_down, v_g_ple_gate, v_w_ple_gate, v_w_ple_up, v_g_ple_post, v_g_final):
    seq = x.shape[1]
    me = 2 * lax.axis_index("x") + lax.axis_index("y")
    chip = me.astype(jnp.int32).reshape(1)
    core = lax.axis_index("c").astype(jnp.int32).reshape(1)
    place = jnp.concatenate([chip, core])
    xs, ps, ts = x.reshape(seq, D_MODEL), p.reshape(seq, D_PLE), loss_target.reshape(seq, D_MODEL)

    big_names = ["w_in", "w_gu", "w_out", "w_down", "w_pg", "w_pu"]
    big = [w_in[0], w_gate_up[0], w_out[0], w_down[0], w_ple_gate[0], w_ple_up[0]]
    big_m = [m_w_in[0], m_w_gate_up[0], m_w_out[0], m_w_down[0], m_w_ple_gate[0], m_w_ple_up[0]]
    big_v = [v_w_in[0], v_w_gate_up[0], v_w_out[0], v_w_down[0], v_w_ple_gate[0], v_w_ple_up[0]]
    b_in, b_gu, b_out, b_down, b_pg, b_pu = [_cast_into_slot(nm, chip, w, bf16) for nm, w in zip(big_names, big)]
    b_cw = _cast_into_slot("conv_w", chip, _pad_rows(conv_w[0], HALO), f32)
    xi, yi = lax.axis_index("x"), lax.axis_index("y")
    order = jnp.stack([me, 2 * (1 - xi) + yi, 2 * xi + 1 - yi, 2 * (1 - xi) + 1 - yi]).astype(jnp.int32)

    z, (w_in_g, cw_g, w_out_g) = _mix_in(xs, g_mix, order, [b_in, b_cw, b_out])
    conv_w_f = cw_g.transpose(1, 0, 2).reshape(HALO, C_CONV)
    w_out_f = w_out_g.reshape(D_MODEL, D_MODEL)
    (x1, mix, u1, pooled), (w_gu_g, w_down_g) = _conv_pool_out(z, xs, conv_w_f, conv_b, ln_g, ln_b, pool_w[0],
                                                                pool_scale, w_out_f, [b_gu, b_down])
    w_down_f = w_down_g.reshape(D_FF, D_MODEL)
    (x2, h2, gu, ffn_f), (w_pg_g, w_pu_g) = _ffn_fwd(x1, g_ffn, w_gu_g, w_down_f, [b_pg, b_pu])
    w_pg_f = w_pg_g.reshape(D_MODEL, D_MODEL)
    dx2, d_w_pg, d_w_pu, small_ple = _ple_loss(x2, ps, ts, g_ple_gate, g_ple_post, g_final.reshape(1, D_MODEL),
                                               w_pg_f, w_pu_g)
    def pair_reduce(group, names, grads):
        got = _rs_pair(group, grads)
        return [_rs_add_pair(nm, core, g, r) for nm, g, r in zip(names, grads, got)]

    d_w_down = _ffn_bwd_dw_down(ffn_f, dx2)
    parts_a = pair_reduce("a", ["w_pg", "w_pu", "w_down"],
                          [d_w_pg.reshape(N_CHIPS, -1, D_MODEL), d_w_pu, d_w_down.reshape(N_CHIPS, -1, D_MODEL)])
    (dx1, dgu, small_ffn), landed_a = _ffn_bwd_dx(dx2, x1, gu, g_ffn, w_gu_g, w_down_f, parts_a)
    d_w_gu = _ffn_bwd_dw_gu(h2, dgu)
    du1, dpo, d_w_out, d_pool_w, small_mix = _mix_bwd_local(dx1, mix, u1, pooled, w_out_f, ln_g, ln_b, pool_w[0],
                                                             pool_scale)
    parts_b = pair_reduce("b", ["w_gu", "w_out"], [d_w_gu, d_w_out.reshape(N_CHIPS, -1, D_MODEL)])
    (grad_x, d_w_in, d_conv_w, small_in), landed_b = _in_bwd(du1, dpo, z, xs, dx1, conv_w_f, g_mix, w_in_g, parts_b)
    parts_c = pair_reduce("c", ["w_in"], [d_w_in])
    cw_chunks = d_conv_w.reshape(HALO, N_CHIPS, 128).transpose(1, 0, 2)
    part = jnp.concatenate([_rows128(small_ple), _rows128(small_ffn), _rows128(small_mix), _rows128(d_pool_w),
                            _rows128(cw_chunks), _rows128(small_in)], axis=0)
    (h_gu, h_out, h_down, h_pg, h_pu), landed_c, small_all = _rs_sum_group(
        "ab", place, [landed_b[0], landed_b[1], landed_a[2], landed_a[0], landed_a[1]],
        [parts_b[0], parts_b[1], parts_a[2], parts_a[0], parts_a[1]], parts_c, part)
    h_in = _rs_sum_chips("w_in", place, landed_c[0], parts_c[0])
    big_g = _grad_pair([h_in, h_gu, h_out, h_down, h_pg, h_pu])
    big_upd = [_adam(nm, w, g, m, v) for nm, w, g, m, v in zip(big_names, big, big_g, big_m, big_v)]

    sw = _pack_small(me, g_ple_gate, g_ple_post, g_final, g_ffn, ln_g, ln_b, conv_b, pool_scale, pool_w, conv_w[0], g_mix)
    sm = _pack_small(me, m_g_ple_gate, m_g_ple_post, m_g_final, m_g_ffn, m_ln_g, m_ln_b, m_conv_b, m_pool_scale,
                     m_pool_w, m_conv_w[0], m_g_mix)
    sv = _pack_small(me, v_g_ple_gate, v_g_ple_post, v_g_final, v_g_ffn, v_ln_g, v_ln_b, v_conv_b, v_pool_scale,
                     v_pool_w, v_conv_w[0], v_g_mix)
    dev = (2 * chip + core).astype(jnp.int32)
    small = [_unpack_small(me, a) for a in _small_adam(dev, small_all, part, sw, sm, sv)]

    names = ["g_mix", "w_in", "conv_w", "conv_b", "ln_g", "ln_b", "pool_w", "pool_scale", "w_out", "g_ffn",
             "w_gate_up", "w_down", "g_ple_gate", "w_ple_gate", "w_ple_up", "g_ple_post", "g_final"]
    big_at = {"w_in": 0, "w_gate_up": 1, "w_out": 2, "w_down": 3, "w_ple_gate": 4, "w_ple_up": 5}
    out = [small[0]["loss"], grad_x.reshape(1, seq, D_MODEL)]
    for kind in range(4):
        for nm in names:
            if nm in big_at:
                k = big_at[nm]
                out.append((big_g[k] if kind == 0 else big_upd[k][kind - 1])[None])
            else:
                out.append(small[kind][nm])
    return tuple(out)
```

```python
import functools

import jax
import jax.numpy as jnp
from jax import lax
from jax.experimental import pallas as pl
from jax.experimental.pallas import tpu as pltpu

f32, bf16 = jnp.float32, jnp.bfloat16

EPS = 1e-6
D_MODEL = 1024
C_CONV = 512
C_POOL = 512
POOL_WINDOWS = (2, 4, 8, 16)
POOL_GROUP = 128
CONV_K = 31
D_FF = 2816
D_PLE = 256
N_CHIPS = 4
N_DEV = 8
W_IN_COLS = 2 * C_CONV + C_POOL
W_IN_CHUNK = W_IN_COLS // N_CHIPS
FF_CHUNK = 2 * D_FF // N_CHIPS
PLE_CHUNK = D_MODEL // N_CHIPS
HALO = 32
ROW_TILE = 512
CONV_ROWS = 64
CONV_COLS = (slice(0, 256), slice(256, 512))
SHIFT_PAD = 32
SHIFT_GROUPS = 16
FF_SUB = (0, 512, 1024, FF_CHUNK)
VMEM_LIMIT = 56 * 1024 * 1024

ADAM_LR = 0.001
ADAM_B1 = 0.9
ADAM_B2 = 0.999
ADAM_EPS = 1e-08
ADAM_WD = 0.01
ADAM_STEP = 10

MESH = pl.DeviceIdType.MESH
ANY = pl.BlockSpec(memory_space=pl.ANY)
VMEM = pl.BlockSpec(memory_space=pltpu.VMEM)


def _cp(*sem):
    return pltpu.CompilerParams(dimension_semantics=sem, vmem_limit_bytes=VMEM_LIMIT)


def _dot(a, b):
    return jnp.dot(a, b, preferred_element_type=f32)


def _dot_nt(a, b):
    return lax.dot_general(a, b, (((1,), (1,)), ((), ())), preferred_element_type=f32)


def _dot_tn(a, b):
    return lax.dot_general(a, b, (((0,), (0,)), ((), ())), preferred_element_type=f32)


def _sigmoid(v):
    return jax.nn.sigmoid(v)


def _rms_fwd(v, g):
    r = lax.rsqrt(jnp.mean(v * v, axis=-1, keepdims=True) + EPS)
    vh = v * r
    return vh * g, vh, r


def _rms_bwd(dy, vh, r, g):
    dvh = dy * g
    dv = r * (dvh - vh * jnp.mean(dvh * vh, axis=-1, keepdims=True))
    return dv, jnp.sum(dy * vh, axis=0, keepdims=True)


def _silu_grad(v, s):
    return s * (1.0 + v * (1.0 - s))


def _row(i, n):
    return pl.BlockSpec((n[0], n[1]), lambda *a: (a[i], 0))


def _full(shape):
    nd = len(shape)
    return pl.BlockSpec(shape, lambda *a: (0,) * nd)


def _place():
    x, y, c = lax.axis_index("x"), lax.axis_index("y"), lax.axis_index("c")
    others = [(1 - x, y), (x, 1 - y), (1 - x, 1 - y)]
    return x, y, c, 2 * x + y, others


def _remote(src, dst, send_sem, recv_sem, dev):
    return pltpu.make_async_remote_copy(src_ref=src, dst_ref=dst, send_sem=send_sem, recv_sem=recv_sem,
                                        device_id=dev, device_id_type=MESH)


def _cast_into_slot(name, me, w, dtype):
    rows, cols = w.shape
    tr = next(rows // d for d in (1, 2, 4, 8) if rows % (16 * d) == 0 and rows // d <= 512)

    def body(me_ref, w_ref, o_ref):
        o_ref[0] = w_ref[...].astype(dtype)

    return pl.pallas_call(
        body, name=f"cast_{name}",
        grid_spec=pltpu.PrefetchScalarGridSpec(
            num_scalar_prefetch=1, grid=(rows // tr,),
            in_specs=[pl.BlockSpec((tr, cols), lambda r, me_ref: (r, 0))],
            out_specs=pl.BlockSpec((1, tr, cols), lambda r, me_ref: (me_ref[0], r, 0))),
        out_shape=jax.ShapeDtypeStruct((N_CHIPS, rows, cols), dtype),
        compiler_params=_cp("parallel"),
    )(me, w)


class _Gather:
    def __init__(self, bufs, send_sems, recv_sems):
        self.bufs, self.send_sems, self.recv_sems = bufs, send_sems, recv_sems
        self.x, self.y, self.c, self.me, self.others = _place()
        self.halves = [b.shape[1] // 2 for b in bufs]

    def _piece(self, k, chip, half):
        return self.bufs[k].at[chip, pl.ds(half * self.halves[k], self.halves[k]), :]

    def _ici(self, k, j, chip):
        ox, oy = self.others[j]
        piece = self._piece(k, chip, self.c)
        return _remote(piece, piece, self.send_sems.at[6 * k + j], self.recv_sems.at[6 * k + j], (ox, oy, self.c))

    def _relay(self, k):
        first = self.c == 0
        piece = self._piece(k, jnp.where(first, self.chip(0), self.chip(1)), self.c)
        to = (jnp.where(first, self.others[1][0], self.others[0][0]),
              jnp.where(first, self.others[1][1], self.others[0][1]), self.c)
        return _remote(piece, piece, self.send_sems.at[6 * k + 2], self.recv_sems.at[6 * k + 2], to)

    def _pair(self, k, j, half):
        ox, oy = self.others[j]
        piece = self._piece(k, 2 * ox + oy, half)
        return _remote(piece, piece, self.send_sems.at[6 * k + 3 + j], self.recv_sems.at[6 * k + 3 + j],
                       (self.x, self.y, 1 - self.c))

    def _each(self, ks=None):
        return [(k, j) for k in (range(len(self.bufs)) if ks is None else ks) for j in range(3)]

    def chip(self, j):
        ox, oy = self.others[j]
        return 2 * ox + oy

    def start(self):
        for k in range(len(self.bufs)):
            for j in range(2):
                self._ici(k, j, self.me).start()

    def forward(self, pairs=None):
        for k, j in self._each() if pairs is None else pairs:
            self._ici(k, j, self.chip(j)).wait_recv()
            self._pair(k, j, self.c).start()
            if j < 2:
                pl.when(self.c == j)(self._relay(k).start)

    def landed(self, pairs):
        for k, j in pairs:
            self._pair(k, j, 1 - self.c).wait_recv()

    def finish(self, ks=None):
        self.landed(self._each(ks))
        for k in range(len(self.bufs)):
            for j in range(2):
                self._ici(k, j, self.me).wait_send()
            self._relay(k).wait_send()
            for j in range(3):
                self._pair(k, j, self.c).wait_send()

    @staticmethod
    def scratch(n):
        return [pltpu.SemaphoreType.DMA((6 * n,)), pltpu.SemaphoreType.DMA((6 * n,))]


def _carried(bufs):
    n = len(bufs)
    return dict(in_specs=[ANY] * n, out_specs=[ANY] * n,
                out_shape=[jax.ShapeDtypeStruct(b.shape, b.dtype) for b in bufs], scratch=_Gather.scratch(n))


def _mix_in(x, g_mix, order, carry):
    s = x.shape[0]
    tm = min(2 * ROW_TILE, s)
    n = s // tm
    nc = len(carry)
    cs = _carried(carry)

    def body(order_ref, x_ref, g_ref, *refs):
        z_ref = refs[nc]
        bufs = refs[nc + 1:2 * nc + 1]
        h_ref, w_ref, w_sem = refs[2 * nc + 1:2 * nc + 4]
        gather = _Gather(bufs, *refs[2 * nc + 4:])
        q, i = pl.program_id(0), pl.program_id(1)
        first = i == 0
        pl.when(jnp.logical_and(q == 0, first))(gather.start)
        for j in range(3):

            @pl.when(jnp.logical_and(q == j + 1, first))
            def _():
                gather.forward([(0, j)])
                gather.landed([(0, j)])
                if j == 1:
                    gather.forward([(k, jj) for k in range(1, nc) for jj in range(2)])

        @pl.when(first)
        def _():
            load = pltpu.make_async_copy(bufs[0].at[order_ref[q]], w_ref, w_sem)
            load.start()
            load.wait()

        @pl.when(q == 0)
        def _():
            h, _, _ = _rms_fwd(x_ref[...], g_ref[...])
            h_ref[i] = h.astype(bf16)

        z_ref[...] = _dot(h_ref[i], w_ref[...])

        @pl.when(jnp.logical_and(q == N_CHIPS - 1, i == n - 1))
        def _():
            gather.forward([(k, 2) for k in range(1, nc)])
            gather.finish(range(1, nc))

    res = pl.pallas_call(
        body, name="mix_in",
        grid_spec=pltpu.PrefetchScalarGridSpec(
            num_scalar_prefetch=1, grid=(N_CHIPS, n),
            in_specs=[pl.BlockSpec((tm, D_MODEL), lambda q, i, order_ref: (jnp.where(q == 0, i, 0), 0)),
                      pl.BlockSpec((1, D_MODEL), lambda q, i, order_ref: (0, 0))] + cs["in_specs"],
            out_specs=[pl.BlockSpec((tm, W_IN_CHUNK), lambda q, i, order_ref: (i, order_ref[q]))] + cs["out_specs"],
            scratch_shapes=[pltpu.VMEM((n, tm, D_MODEL), bf16), pltpu.VMEM((D_MODEL, W_IN_CHUNK), bf16),
                            pltpu.SemaphoreType.DMA(())] + cs["scratch"]),
        out_shape=[jax.ShapeDtypeStruct((s, W_IN_COLS), f32)] + cs["out_shape"],
        input_output_aliases={3 + k: 1 + k for k in range(nc)},
        compiler_params=_cp("arbitrary", "arbitrary"),
    )(order, x, g_mix, *carry)
    return res[0], res[1:]


def _tap_offsets(lo, hi):
    groups = [[o for o in range(lo, hi + 1) if o % 8 == s] for s in range(8)]
    return [g for g in groups if g]


def _tap_sum(buf, w_ref, row0, cols, tap_of):
    acc = jnp.zeros((CONV_ROWS, cols.stop - cols.start), f32)
    for offs in _tap_offsets(0, CONV_K - 1):
        slab = buf[pl.ds(row0 + offs[0], offs[-1] - offs[0] + CONV_ROWS), cols]
        for o in offs:
            acc = acc + w_ref[pl.ds(tap_of(o), 1), cols] * slab[o - offs[0]:o - offs[0] + CONV_ROWS]
    return acc


def _pool_counts(tm, w, first_row):
    t1 = (lax.broadcasted_iota(jnp.int32, (tm, 1), 0) + first_row + 1).astype(f32)
    return jnp.minimum(t1, float(w))


def _conv_pool_out(z, x, conv_w, conv_b, ln_g, ln_b, pool_w, pool_scale, w_out, carry):
    s = x.shape[0]
    tm = min(ROW_TILE, s)
    n = s // tm
    hb = tm // HALO
    nc = len(carry)
    cs = _carried(carry)

    def body(z_ref, zp_ref, x_ref, cw_ref, cb_ref, lg_ref, lb_ref, pw_ref, ps_ref, wo_ref, *refs):
        x1_ref, mix_ref, u1_ref, pooled_ref = refs[nc:nc + 4]
        ubuf, vbuf = refs[2 * nc + 4:2 * nc + 6]
        gather = _Gather(refs[nc + 4:2 * nc + 4], *refs[2 * nc + 6:])
        i = pl.program_id(0)
        pl.when(i == 0)(gather.start)
        for k in range(nc):
            pl.when(i == min(n // 2 + 2 * k, n - 1))(functools.partial(gather.forward, [(k, 0), (k, 1)]))
        keep = (i > 0).astype(f32)
        zp = zp_ref[...] * keep
        ubuf[0:HALO, :] = zp[:, :C_CONV] * _sigmoid(zp[:, C_CONV:2 * C_CONV])
        vbuf[0:HALO, :] = zp[:, 2 * C_CONV:]
        ubuf[HALO:, :] = z_ref[:, :C_CONV] * _sigmoid(z_ref[:, C_CONV:2 * C_CONV])
        vbuf[HALO:, :] = z_ref[:, 2 * C_CONV:]
        off = HALO - (CONV_K - 1)
        for r0 in range(0, tm, CONV_ROWS):
            for cols in CONV_COLS:
                u1_ref[r0:r0 + CONV_ROWS, cols] = cb_ref[:, cols] + _tap_sum(ubuf, cw_ref, r0 + off, cols, lambda o: o)
        u1 = u1_ref[...]
        mu = jnp.mean(u1, axis=-1, keepdims=True)
        uc = u1 - mu
        rstd = lax.rsqrt(jnp.mean(uc * uc, axis=-1, keepdims=True) + EPS)
        u2 = uc * rstd * lg_ref[...] + lb_ref[...]
        mix_ref[:, :C_CONV] = (u2 * _sigmoid(u2)).astype(bf16)
        for g, w in enumerate(POOL_WINDOWS):
            cols = slice(g * POOL_GROUP, (g + 1) * POOL_GROUP)
            acc = vbuf[pl.ds(HALO, tm), cols]
            vg = acc
            for d in range(1, w):
                acc = acc + vbuf[pl.ds(HALO - d, tm), cols]
            pooled = (acc / _pool_counts(tm, w, i * tm) - vg).astype(bf16)
            pooled_ref[:, cols] = pooled
            mixed = _dot(pooled, pw_ref[g].astype(bf16))
            mix_ref[:, C_CONV + g * POOL_GROUP:C_CONV + (g + 1) * POOL_GROUP] = (mixed * ps_ref[:, cols]).astype(bf16)
        x1_ref[...] = x_ref[...] + _dot(mix_ref[...], wo_ref[...])
        @pl.when(i == n - 1)
        def _():
            gather.forward([(k, 2) for k in range(nc)])
            gather.finish()

    res = pl.pallas_call(
        body, name="conv_pool_out", grid=(n,),
        in_specs=[_row(0, (tm, W_IN_COLS)),
                  pl.BlockSpec((HALO, W_IN_COLS), lambda i: (jnp.maximum(i * hb - 1, 0), 0)),
                  _row(0, (tm, D_MODEL)), _full((HALO, C_CONV)), _full((1, C_CONV)), _full((1, C_CONV)),
                  _full((1, C_CONV)), _full((4, POOL_GROUP, POOL_GROUP)), _full((1, C_POOL)),
                  _full((D_MODEL, D_MODEL))] + cs["in_specs"],
        out_specs=[_row(0, (tm, D_MODEL)), _row(0, (tm, D_MODEL)), _row(0, (tm, C_CONV)), _row(0, (tm, C_POOL))]
        + cs["out_specs"],
        out_shape=[jax.ShapeDtypeStruct((s, D_MODEL), f32), jax.ShapeDtypeStruct((s, D_MODEL), bf16),
                   jax.ShapeDtypeStruct((s, C_CONV), f32), jax.ShapeDtypeStruct((s, C_POOL), bf16)] + cs["out_shape"],
        input_output_aliases={10 + k: 4 + k for k in range(nc)},
        scratch_shapes=[pltpu.VMEM((HALO + tm, C_CONV), f32), pltpu.VMEM((HALO + tm, C_POOL), f32)] + cs["scratch"],
        compiler_params=_cp("arbitrary"),
    )(z, z, x, conv_w, conv_b, ln_g, ln_b, pool_w, pool_scale, w_out, *carry)
    return res[:4], res[4:]


def _ffn_fwd(x1, g_ffn, w_gu_g, w_down, carry):
    s = x1.shape[0]
    tm = min(ROW_TILE, s)
    n = s // tm
    nc = len(carry)
    cs = _carried(carry)

    def body(x1_ref, g_ref, wg_ref, wu_ref, wd_ref, *refs):
        x2_ref, h2_ref, gu_ref, f_ref = refs[nc:nc + 4]
        acc_ref = refs[2 * nc + 4]
        gather = _Gather(refs[nc + 4:2 * nc + 4], *refs[2 * nc + 5:])
        i, c = pl.program_id(0), pl.program_id(1)
        pl.when(jnp.logical_and(i == 0, c == 0))(gather.start)
        direct = [(k, j) for k in range(nc) for j in range(2)]
        pl.when(jnp.logical_and(i == n // 2, c == 0))(functools.partial(gather.forward, direct))
        pl.when(jnp.logical_and(i == n - 1, c == 0))(functools.partial(gather.forward, [(k, 2) for k in range(nc)]))

        @pl.when(c == 0)
        def _():
            h, _, _ = _rms_fwd(x1_ref[...], g_ref[...])
            h2_ref[...] = h.astype(bf16)
            acc_ref[...] = jnp.zeros_like(acc_ref)

        h = h2_ref[...]
        for lo, hi in zip(FF_SUB[:-1], FF_SUB[1:]):
            gate = _dot(h, wg_ref[0, :, lo:hi])
            up = _dot(h, wu_ref[0, :, lo:hi])
            gu_ref[0, :, lo:hi] = gate.astype(bf16)
            gu_ref[1, :, lo:hi] = up.astype(bf16)
            f = (gate * _sigmoid(gate) * up).astype(bf16)
            f_ref[:, lo:hi] = f
            acc_ref[...] += _dot(f, wd_ref[lo:hi, :])

        @pl.when(c == 1)
        def _():
            x2_ref[...] = x1_ref[...] + acc_ref[...]

        pl.when(jnp.logical_and(i == n - 1, c == 1))(gather.finish)

    res = pl.pallas_call(
        body, name="ffn_fwd", grid=(n, 2),
        in_specs=[_row(0, (tm, D_MODEL)), _full((1, D_MODEL)),
                  pl.BlockSpec((1, D_MODEL, FF_CHUNK), lambda i, c: (c, 0, 0)),
                  pl.BlockSpec((1, D_MODEL, FF_CHUNK), lambda i, c: (2 + c, 0, 0)),
                  pl.BlockSpec((FF_CHUNK, D_MODEL), lambda i, c: (c, 0))] + cs["in_specs"],
        out_specs=[_row(0, (tm, D_MODEL)), _row(0, (tm, D_MODEL)),
                   pl.BlockSpec((2, tm, FF_CHUNK), lambda i, c: (0, i, c)),
                   pl.BlockSpec((tm, FF_CHUNK), lambda i, c: (i, c))] + cs["out_specs"],
        out_shape=[jax.ShapeDtypeStruct((s, D_MODEL), f32), jax.ShapeDtypeStruct((s, D_MODEL), bf16),
                   jax.ShapeDtypeStruct((2, s, D_FF), bf16), jax.ShapeDtypeStruct((s, D_FF), bf16)] + cs["out_shape"],
        input_output_aliases={5 + k: 4 + k for k in range(nc)},
        scratch_shapes=[pltpu.VMEM((tm, D_MODEL), f32)] + cs["scratch"],
        compiler_params=_cp("arbitrary", "arbitrary"),
    )(x1, g_ffn, w_gu_g, w_gu_g, w_down, *carry)
    return res[:4], res[4:]


def _ple_loss(x2, p, target, g_pg, g_post, g_final, w_pg, w_pu_g):
    s = x2.shape[0]
    tm = min(ROW_TILE, s)
    n = s // tm

    def body(x2_ref, p_ref, t_ref, gpg_ref, gpo_ref, gf_ref, wpg_ref, wpu_ref,
             dx2_ref, dwpg_ref, dwpu_ref, small_ref, apg_ref, apu_ref):
        i = pl.program_id(0)

        @pl.when(i == 0)
        def _():
            apg_ref[...] = jnp.zeros_like(apg_ref)
            apu_ref[...] = jnp.zeros_like(apu_ref)
            small_ref[...] = jnp.zeros_like(small_ref)

        x2 = x2_ref[...]
        h3, x2h, r2 = _rms_fwd(x2, gpg_ref[...])
        h3b = h3.astype(bf16)
        gate = _sigmoid(_dot(h3b, wpg_ref[...]))
        pb = p_ref[...].astype(bf16)
        pe = jnp.concatenate([_dot(pb, wpu_ref[j]) for j in range(N_CHIPS)], axis=-1)
        e, peh, rp = _rms_fwd(pe, gpo_ref[...])
        x3 = x2 + gate * e
        y, x3h, r3 = _rms_fwd(x3, gf_ref[...])
        d = y - t_ref[...]
        loss = 0.5 * jnp.sum(jnp.sum(d * d, axis=-1, keepdims=True) * (1.0 / D_MODEL), axis=0, keepdims=True)
        dx3, dgf = _rms_bwd(d * (1.0 / D_MODEL), x3h, r3, gf_ref[...])
        dpe, dgpo = _rms_bwd(dx3 * gate, peh, rp, gpo_ref[...])
        dgl = (dx3 * e * gate * (1.0 - gate)).astype(bf16)
        apg_ref[...] += _dot_tn(h3b, dgl)
        apu_ref[...] += _dot_tn(pb, dpe.astype(bf16))
        dh3 = _dot_nt(dgl, wpg_ref[...])
        dx2b, dgpg = _rms_bwd(dh3, x2h, r2, gpg_ref[...])
        dx2_ref[...] = dx3 + dx2b
        small_ref[0:1, :] += dgpg
        small_ref[1:2, :] += dgpo
        small_ref[2:3, :] += dgf
        small_ref[3:4, :] += jnp.broadcast_to(loss, (1, D_MODEL))

        @pl.when(i == n - 1)
        def _():
            dwpg_ref[...] = apg_ref[...].astype(bf16)
            for j in range(N_CHIPS):
                dwpu_ref[j] = apu_ref[:, j * PLE_CHUNK:(j + 1) * PLE_CHUNK].astype(bf16)

    return pl.pallas_call(
        body, name="ple_loss", grid=(n,),
        in_specs=[_row(0, (tm, D_MODEL)), _row(0, (tm, D_PLE)), _row(0, (tm, D_MODEL)),
                  _full((1, D_MODEL)), _full((1, D_MODEL)), _full((1, D_MODEL)),
                  _full((D_MODEL, D_MODEL)), _full((N_CHIPS, D_PLE, PLE_CHUNK))],
        out_specs=[_row(0, (tm, D_MODEL)), _full((D_MODEL, D_MODEL)), _full((N_CHIPS, D_PLE, PLE_CHUNK)),
                   _full((8, D_MODEL))],
        out_shape=[jax.ShapeDtypeStruct((s, D_MODEL), f32), jax.ShapeDtypeStruct((D_MODEL, D_MODEL), bf16),
                   jax.ShapeDtypeStruct((N_CHIPS, D_PLE, PLE_CHUNK), bf16), jax.ShapeDtypeStruct((8, D_MODEL), f32)],
        scratch_shapes=[pltpu.VMEM((D_MODEL, D_MODEL), f32), pltpu.VMEM((D_PLE, D_MODEL), f32)],
        compiler_params=_cp("arbitrary"),
    )(x2, p, target, g_pg, g_post, g_final, w_pg, w_pu_g)


def _crossed(parts):
    n = len(parts)
    return dict(in_specs=[ANY] * n, out_specs=[ANY] * n,
                out_shape=[jax.ShapeDtypeStruct(a.shape, a.dtype) for a in parts], scratch=_Cross.scratch(n))


def _ffn_bwd_dx(dx2, x1, gu, g_ffn, w_gu_g, w_down, parts):
    s = x1.shape[0]
    tm = min(ROW_TILE, s)
    n = s // tm
    nc = len(parts)
    cs = _crossed(parts)

    def body(dx2_ref, x1_ref, gu_ref, g_ref, wg_ref, wu_ref, wd_ref, *refs):
        dx1_ref, dgu_ref, small_ref = refs[nc:nc + 3]
        acc_ref = refs[2 * nc + 3]
        cross = _Cross(refs[:nc], refs[nc + 3:2 * nc + 3], *refs[2 * nc + 4:])
        i, c = pl.program_id(0), pl.program_id(1)
        pl.when(jnp.logical_and(i == 0, c == 0))(cross.start)

        @pl.when(jnp.logical_and(i == 0, c == 0))
        def _():
            small_ref[...] = jnp.zeros_like(small_ref)

        @pl.when(c == 0)
        def _():
            acc_ref[...] = jnp.zeros_like(acc_ref)

        dyb = dx2_ref[...].astype(bf16)
        for lo, hi in zip(FF_SUB[:-1], FF_SUB[1:]):
            df = _dot_nt(dyb, wd_ref[lo:hi, :])
            gate = gu_ref[0, :, lo:hi].astype(f32)
            up = gu_ref[1, :, lo:hi].astype(f32)
            sg = _sigmoid(gate)
            dgate = (df * up * _silu_grad(gate, sg)).astype(bf16)
            dup = (df * gate * sg).astype(bf16)
            dgu_ref[0, :, lo:hi] = dgate
            dgu_ref[1, :, lo:hi] = dup
            acc_ref[...] += _dot_nt(dgate, wg_ref[0, :, lo:hi]) + _dot_nt(dup, wu_ref[0, :, lo:hi])

        @pl.when(c == 1)
        def _():
            _, x1h, r1 = _rms_fwd(x1_ref[...], g_ref[...])
            dx1b, dg = _rms_bwd(acc_ref[...], x1h, r1, g_ref[...])
            dx1_ref[...] = dx2_ref[...] + dx1b
            small_ref[0:1, :] += dg

        pl.when(jnp.logical_and(i == n - 1, c == 1))(cross.finish)

    res = pl.pallas_call(
        body, name="ffn_bwd_dx", grid=(n, 2),
        in_specs=[_row(0, (tm, D_MODEL)), _row(0, (tm, D_MODEL)),
                  pl.BlockSpec((2, tm, FF_CHUNK), lambda i, c: (0, i, c)), _full((1, D_MODEL)),
                  pl.BlockSpec((1, D_MODEL, FF_CHUNK), lambda i, c: (c, 0, 0)),
                  pl.BlockSpec((1, D_MODEL, FF_CHUNK), lambda i, c: (2 + c, 0, 0)),
                  pl.BlockSpec((FF_CHUNK, D_MODEL), lambda i, c: (c, 0))] + cs["in_specs"],
        out_specs=[_row(0, (tm, D_MODEL)), pl.BlockSpec((2, tm, FF_CHUNK), lambda i, c: (0, i, c)),
                   _full((8, D_MODEL))] + cs["out_specs"],
        out_shape=[jax.ShapeDtypeStruct((s, D_MODEL), f32), jax.ShapeDtypeStruct((2, s, D_FF), bf16),
                   jax.ShapeDtypeStruct((8, D_MODEL), f32)] + cs["out_shape"],
        scratch_shapes=[pltpu.VMEM((tm, D_MODEL), f32)] + cs["scratch"],
        compiler_params=_cp("arbitrary", "arbitrary"),
    )(dx2, x1, gu, g_ffn, w_gu_g, w_gu_g, w_down, *parts)
    return res[:3], res[3:]


def _ffn_bwd_dw_gu(h2, dgu):
    s = h2.shape[0]
    ts = min(2 * ROW_TILE, s)
    n = s // ts

    def body(h_ref, d_ref, o_ref, acc_ref):
        t = pl.program_id(1)

        @pl.when(t == 0)
        def _():
            acc_ref[...] = jnp.zeros_like(acc_ref)

        acc_ref[...] += _dot_tn(h_ref[...], d_ref[0])

        @pl.when(t == n - 1)
        def _():
            o_ref[0] = acc_ref[...].astype(bf16)

    return pl.pallas_call(
        body, name="ffn_bwd_dw_gu", grid=(N_CHIPS, n),
        in_specs=[pl.BlockSpec((ts, D_MODEL), lambda j, t: (t, 0)),
                  pl.BlockSpec((1, ts, FF_CHUNK), lambda j, t: (j // 2, t, j % 2))],
        out_specs=pl.BlockSpec((1, D_MODEL, FF_CHUNK), lambda j, t: (j, 0, 0)),
        out_shape=jax.ShapeDtypeStruct((N_CHIPS, D_MODEL, FF_CHUNK), bf16),
        scratch_shapes=[pltpu.VMEM((D_MODEL, FF_CHUNK), f32)],
        compiler_params=_cp("parallel", "arbitrary"),
    )(h2, dgu)


def _ffn_bwd_dw_down(f, dx2):
    s = dx2.shape[0]
    ts = min(2 * ROW_TILE, s)
    n = s // ts

    def body(f_ref, d_ref, o_ref, acc_ref):
        t = pl.program_id(1)

        @pl.when(t == 0)
        def _():
            acc_ref[...] = jnp.zeros_like(acc_ref)

        acc_ref[...] += _dot_tn(f_ref[...], d_ref[...].astype(bf16))

        @pl.when(t == n - 1)
        def _():
            o_ref[...] = acc_ref[...].astype(bf16)

    return pl.pallas_call(
        body, name="ffn_bwd_dw_down", grid=(2, n),
        in_specs=[pl.BlockSpec((ts, FF_CHUNK), lambda c, t: (t, c)),
                  pl.BlockSpec((ts, D_MODEL), lambda c, t: (t, 0))],
        out_specs=pl.BlockSpec((FF_CHUNK, D_MODEL), lambda c, t: (c, 0)),
        out_shape=jax.ShapeDtypeStruct((D_FF, D_MODEL), bf16),
        scratch_shapes=[pltpu.VMEM((FF_CHUNK, D_MODEL), f32)],
        compiler_params=_cp("parallel", "arbitrary"),
    )(f, dx2)


def _mix_bwd_local(dx1, mix, u1, pooled, w_out, ln_g, ln_b, pool_w, pool_scale):
    s = dx1.shape[0]
    tm = min(ROW_TILE, s)
    n = s // tm

    def body(dx1_ref, mix_ref, u1_ref, po_ref, wo_ref, lg_ref, lb_ref, pw_ref, ps_ref,
             du1_ref, dpo_ref, dwo_ref, dpw_ref, small_ref, awo_ref):
        i = pl.program_id(0)

        @pl.when(i == 0)
        def _():
            awo_ref[...] = jnp.zeros_like(awo_ref)
            dpw_ref[...] = jnp.zeros_like(dpw_ref)
            small_ref[...] = jnp.zeros_like(small_ref)

        dyb = dx1_ref[...].astype(bf16)
        dmix = _dot_nt(dyb, wo_ref[...])
        awo_ref[...] += _dot_tn(mix_ref[...], dyb)
        u1 = u1_ref[...]
        mu = jnp.mean(u1, axis=-1, keepdims=True)
        uc = u1 - mu
        rstd = lax.rsqrt(jnp.mean(uc * uc, axis=-1, keepdims=True) + EPS)
        uh = uc * rstd
        u2 = uh * lg_ref[...] + lb_ref[...]
        du2 = dmix[:, :C_CONV] * _silu_grad(u2, _sigmoid(u2))
        duh = du2 * lg_ref[...]
        du1 = rstd * (duh - jnp.mean(duh, axis=-1, keepdims=True) - uh * jnp.mean(duh * uh, axis=-1, keepdims=True))
        du1_ref[...] = du1
        small_ref[0:1, :] += jnp.sum(du2 * uh, axis=0, keepdims=True)
        small_ref[1:2, :] += jnp.sum(du2, axis=0, keepdims=True)
        small_ref[2:3, :] += jnp.sum(du1, axis=0, keepdims=True)
        for g in range(len(POOL_WINDOWS)):
            cols = slice(g * POOL_GROUP, (g + 1) * POOL_GROUP)
            dq = dmix[:, C_CONV + g * POOL_GROUP:C_CONV + (g + 1) * POOL_GROUP]
            pwb = pw_ref[g].astype(bf16)
            pg = po_ref[:, cols]
            mixed = _dot(pg, pwb)
            small_ref[3:4, cols] += jnp.sum(dq * mixed, axis=0, keepdims=True)
            dmixed = (dq * ps_ref[:, cols]).astype(bf16)
            dpw_ref[g] += _dot_tn(pg, dmixed)
            dpo_ref[:, cols] = _dot_nt(dmixed, pwb)

        @pl.when(i == n - 1)
        def _():
            dwo_ref[...] = awo_ref[...].astype(bf16)

    return pl.pallas_call(
        body, name="mix_bwd_local", grid=(n,),
        in_specs=[_row(0, (tm, D_MODEL)), _row(0, (tm, D_MODEL)), _row(0, (tm, C_CONV)), _row(0, (tm, C_POOL)),
                  _full((D_MODEL, D_MODEL)), _full((1, C_CONV)), _full((1, C_CONV)),
                  _full((4, POOL_GROUP, POOL_GROUP)), _full((1, C_POOL))],
        out_specs=[_row(0, (tm, C_CONV)), _row(0, (tm, C_POOL)), _full((D_MODEL, D_MODEL)),
                   _full((4, POOL_GROUP, POOL_GROUP)), _full((8, C_CONV))],
        out_shape=[jax.ShapeDtypeStruct((s, C_CONV), f32), jax.ShapeDtypeStruct((s, C_POOL), f32),
                   jax.ShapeDtypeStruct((D_MODEL, D_MODEL), bf16),
                   jax.ShapeDtypeStruct((4, POOL_GROUP, POOL_GROUP), f32), jax.ShapeDtypeStruct((8, C_CONV), f32)],
        scratch_shapes=[pltpu.VMEM((D_MODEL, D_MODEL), f32)],
        compiler_params=_cp("arbitrary"),
    )(dx1, mix, u1, pooled, w_out, ln_g, ln_b, pool_w, pool_scale)


def _in_bwd(du1, dpo, z, x, dx1, conv_w, g_mix, w_in_g, parts, small_part):
    s = x.shape[0]
    tm = min(ROW_TILE, s)
    n = s // tm
    hb = tm // HALO
    last = s // HALO - 1
    nv = tm // 8
    assert nv >= SHIFT_PAD and nv % SHIFT_GROUPS == 0
    nc = len(parts)
    cs = _crossed(parts)

    def body(du_ref, dun_ref, dp_ref, dpn_ref, z_ref, zp_ref, x_ref, dx1_ref, cw_ref, g_ref, w_ref, *refs):
        gx_ref, dw_ref, dcw_ref, small_ref = refs[nc + 1:nc + 5]
        eu, ed, ep, ss, u0_ref, dz_ref, acc_ref, dcw_acc = refs[2 * nc + 6:2 * nc + 14]
        sems = refs[2 * nc + 14:]
        cross = _Cross(refs[:nc], refs[nc + 5:2 * nc + 5], sems[0], sems[1])
        gather = _SmallGather(refs[nc], refs[2 * nc + 5], sems[2], sems[3])
        i = pl.program_id(0)
        pl.when(i == 0)(cross.start)
        pl.when(i == 0)(gather.start)

        @pl.when(i == 0)
        def _():
            acc_ref[...] = jnp.zeros_like(acc_ref)
            dcw_acc[...] = jnp.zeros_like(dcw_acc)
            small_ref[...] = jnp.zeros_like(small_ref)

        keep_prev = (i > 0).astype(f32)
        keep_next = (i < n - 1).astype(f32)
        zp = zp_ref[...] * keep_prev
        u0_prev = zp[:, :C_CONV] * _sigmoid(zp[:, C_CONV:2 * C_CONV])
        u0_ref[...] = z_ref[:, :C_CONV] * _sigmoid(z_ref[:, C_CONV:2 * C_CONV])
        du_next = dun_ref[...] * keep_next
        for c, w_pool in enumerate(POOL_WINDOWS):
            lanes = slice(c * 128, (c + 1) * 128)
            eu[c, pl.ds(0, SHIFT_PAD, stride=8), :] = u0_prev[:, lanes]
            ed[c, pl.ds(nv * 8 + 7, SHIFT_PAD, stride=8), :] = du_next[:, lanes]
            ep[c, pl.ds(nv * 8 + 7, SHIFT_PAD, stride=8), :] = (
                dpn_ref[:, lanes] * keep_next / _pool_counts(HALO, w_pool, (i + 1) * tm))
            for j in range(8):
                rows = slice(j * nv, (j + 1) * nv)
                eu[c, pl.ds(SHIFT_PAD * 8 + j, nv, stride=8), :] = u0_ref[rows, lanes]
                ed[c, pl.ds(j, nv, stride=8), :] = du_ref[rows, lanes]
                ep[c, pl.ds(j, nv, stride=8), :] = dp_ref[rows, lanes] / _pool_counts(nv, w_pool, i * tm + j * nv)
                if j >= 1:
                    eu[c, pl.ds(j, SHIFT_PAD, stride=8), :] = u0_ref[j * nv - SHIFT_PAD:j * nv, lanes]
                if j <= 6:
                    edge = slice((j + 1) * nv, (j + 1) * nv + SHIFT_PAD)
                    ed[c, pl.ds(nv * 8 + j, SHIFT_PAD, stride=8), :] = du_ref[edge, lanes]
                    ep[c, pl.ds(nv * 8 + j, SHIFT_PAD, stride=8), :] = (
                        dp_ref[edge, lanes] / _pool_counts(SHIFT_PAD, w_pool, i * tm + (j + 1) * nv))
        for c, w_pool in enumerate(POOL_WINDOWS):
            lanes = slice(c * 128, (c + 1) * 128)
            b_lanes = slice(C_CONV + c * 128, C_CONV + (c + 1) * 128)
            v_lanes = slice(2 * C_CONV + c * 128, 2 * C_CONV + (c + 1) * 128)
            for v0 in range(0, nv, SHIFT_GROUPS):
                span = SHIFT_GROUPS * 8
                acc = jnp.zeros((span, 128), f32)
                for k in range(CONV_K):
                    acc = acc + cw_ref[pl.ds(k, 1), lanes] * ed[c, pl.ds((v0 + CONV_K - 1 - k) * 8, span), :]
                ss[0, v0 * 8:v0 * 8 + span, :] = acc
                acc = ep[c, pl.ds(v0 * 8, span), :]
                for d in range(1, w_pool):
                    acc = acc + ep[c, pl.ds((v0 + d) * 8, span), :]
                ss[1, v0 * 8:v0 * 8 + span, :] = acc
                d1 = ed[c, pl.ds(v0 * 8, span), :]
                for k in range(CONV_K):
                    prod = d1 * eu[c, pl.ds((SHIFT_PAD - (CONV_K - 1) + v0 + k) * 8, span), :]
                    fold = prod[0:8]
                    for r in range(8, span, 8):
                        fold = fold + prod[r:r + 8]
                    dcw_acc[k, :, lanes] += fold
            for j in range(8):
                rows = slice(j * nv, (j + 1) * nv)
                du0 = ss[0, pl.ds(j, nv, stride=8), :]
                av, sv = z_ref[rows, lanes], _sigmoid(z_ref[rows, b_lanes])
                dz_ref[rows, lanes] = (du0 * sv).astype(bf16)
                dz_ref[rows, b_lanes] = (du0 * av * sv * (1.0 - sv)).astype(bf16)
                dz_ref[rows, v_lanes] = (ss[1, pl.ds(j, nv, stride=8), :] - dp_ref[rows, lanes]).astype(bf16)
        h, xh, r = _rms_fwd(x_ref[...], g_ref[...])
        dz = dz_ref[...]
        acc_ref[...] += _dot_tn(h.astype(bf16), dz)
        dh = _dot_nt(dz[:, 0:W_IN_CHUNK], w_ref[0])
        for j in range(1, N_CHIPS):
            dh = dh + _dot_nt(dz[:, j * W_IN_CHUNK:(j + 1) * W_IN_CHUNK], w_ref[j])
        dxb, dg = _rms_bwd(dh, xh, r, g_ref[...])
        gx_ref[...] = dx1_ref[...] + dxb
        small_ref[0:1, :] += dg

        @pl.when(i == n - 1)
        def _():
            for j in range(N_CHIPS):
                dw_ref[j] = acc_ref[:, j * W_IN_CHUNK:(j + 1) * W_IN_CHUNK].astype(bf16)
            dcw_ref[...] = jnp.sum(dcw_acc[...], axis=1)

        pl.when(i == n - 1)(gather.finish)
        pl.when(i == n - 1)(cross.finish)

    nxt = lambda i: (jnp.minimum((i + 1) * hb, last), 0)
    res = pl.pallas_call(
        body, name="in_bwd", grid=(n,),
        in_specs=[_row(0, (tm, C_CONV)), pl.BlockSpec((HALO, C_CONV), nxt),
                  _row(0, (tm, C_POOL)), pl.BlockSpec((HALO, C_POOL), nxt),
                  _row(0, (tm, W_IN_COLS)),
                  pl.BlockSpec((HALO, W_IN_COLS), lambda i: (jnp.maximum(i * hb - 1, 0), 0)),
                  _row(0, (tm, D_MODEL)), _row(0, (tm, D_MODEL)), _full((HALO, C_CONV)), _full((1, D_MODEL)),
                  _full((N_CHIPS, D_MODEL, W_IN_CHUNK))] + cs["in_specs"] + [ANY],
        out_specs=[_row(0, (tm, D_MODEL)), _full((N_CHIPS, D_MODEL, W_IN_CHUNK)), _full((HALO, C_CONV)),
                   _full((8, D_MODEL))] + cs["out_specs"] + [ANY],
        out_shape=[jax.ShapeDtypeStruct((s, D_MODEL), f32), jax.ShapeDtypeStruct((N_CHIPS, D_MODEL, W_IN_CHUNK), bf16),
                   jax.ShapeDtypeStruct((HALO, C_CONV), f32), jax.ShapeDtypeStruct((8, D_MODEL), f32)] + cs["out_shape"]
        + [jax.ShapeDtypeStruct((N_DEV,) + small_part.shape, f32)],
        scratch_shapes=[pltpu.VMEM((4, (SHIFT_PAD + nv) * 8, 128), f32), pltpu.VMEM((4, (nv + SHIFT_PAD) * 8, 128), f32),
                        pltpu.VMEM((4, (nv + SHIFT_PAD) * 8, 128), f32), pltpu.VMEM((2, tm, 128), f32),
                        pltpu.VMEM((tm, C_CONV), f32), pltpu.VMEM((tm, W_IN_COLS), bf16),
                        pltpu.VMEM((D_MODEL, W_IN_COLS), f32), pltpu.VMEM((HALO, 8, C_CONV), f32)] + cs["scratch"]
        + _SmallGather.scratch(),
        compiler_params=_cp("arbitrary"),
    )(du1, du1, dpo, dpo, z, z, x, dx1, conv_w, g_mix, w_in_g, *parts, small_part)
    return res[:4], res[4:4 + nc], res[4 + nc]


def _rs_pair(name, grads):
    nk = len(grads)
    halves = [g.shape[1] // 2 for g in grads]

    def body(*refs):
        ins, outs = refs[:nk], refs[nk:2 * nk]
        send_sems, recv_sems = refs[2 * nk:]
        x, y, c, _, _ = _place()
        cps = [_remote(ins[k].at[:, pl.ds((1 - c) * halves[k], halves[k]), :], outs[k],
                       send_sems.at[k], recv_sems.at[k], (x, y, 1 - c)) for k in range(nk)]
        for cp in cps:
            cp.start()
        for cp in cps:
            cp.wait()

    return pl.pallas_call(
        body, name=f"rs_pair_{name}", in_specs=[ANY] * nk, out_specs=[ANY] * nk,
        out_shape=[jax.ShapeDtypeStruct((N_CHIPS, h, g.shape[2]), bf16) for g, h in zip(grads, halves)],
        scratch_shapes=[pltpu.SemaphoreType.DMA((nk,)), pltpu.SemaphoreType.DMA((nk,))],
    )(*grads)


def _rs_add_pair(name, core, grad, recv):
    _, half, cols = recv.shape

    def body(core_ref, g_ref, r_ref, o_ref):
        o_ref[...] = (g_ref[...].astype(f32) + r_ref[...].astype(f32)).astype(bf16)

    return pl.pallas_call(
        body, name=f"rs_add_pair_{name}",
        grid_spec=pltpu.PrefetchScalarGridSpec(
            num_scalar_prefetch=1, grid=(N_CHIPS,),
            in_specs=[pl.BlockSpec((1, half, cols), lambda j, core_ref: (j, core_ref[0], 0)),
                      pl.BlockSpec((1, half, cols), lambda j, core_ref: (j, 0, 0))],
            out_specs=pl.BlockSpec((1, half, cols), lambda j, core_ref: (j, 0, 0))),
        out_shape=jax.ShapeDtypeStruct(recv.shape, bf16),
        compiler_params=_cp("parallel"),
    )(core, grad, recv)


class _Cross:
    def __init__(self, parts, landed, send_sems, recv_sems):
        self.parts, self.landed, self.send_sems, self.recv_sems = parts, landed, send_sems, recv_sems
        _, _, self.c, self.me, self.others = _place()

    def _copy(self, k, j, src_chunk, dst_slot):
        ox, oy = self.others[j]
        return _remote(self.parts[k].at[src_chunk], self.landed[k].at[dst_slot], self.send_sems.at[3 * k + j],
                       self.recv_sems.at[3 * k + j], (ox, oy, self.c))

    def _each(self):
        return [(k, j, 2 * self.others[j][0] + self.others[j][1]) for k in range(len(self.parts)) for j in range(3)]

    def start(self):
        for k, j, chip in self._each():
            self._copy(k, j, chip, self.me).start()

    def finish(self):
        for k, j, chip in self._each():
            self._copy(k, j, chip, chip).wait_recv()
        for k, j, chip in self._each():
            self._copy(k, j, chip, self.me).wait_send()

    @staticmethod
    def scratch(n):
        return [pltpu.SemaphoreType.DMA((3 * n,)), pltpu.SemaphoreType.DMA((3 * n,))]


def _rs_sum_chips(name, place, landed, part):
    _, half, cols = landed.shape

    def body(place_ref, l_ref, p_ref, o_ref):
        me = place_ref[0]
        own = p_ref[0].astype(f32)
        acc = jnp.where(me == 0, own, l_ref[0].astype(f32))
        for j in range(1, N_CHIPS):
            acc = acc + jnp.where(me == j, own, l_ref[j].astype(f32))
        o_ref[...] = acc

    return pl.pallas_call(
        body, name=f"rs_sum_chips_{name}",
        grid_spec=pltpu.PrefetchScalarGridSpec(
            num_scalar_prefetch=1, grid=(1,),
            in_specs=[pl.BlockSpec((N_CHIPS, half, cols), lambda t, place_ref: (0, 0, 0)),
                      pl.BlockSpec((1, half, cols), lambda t, place_ref: (place_ref[0], 0, 0))],
            out_specs=pl.BlockSpec((half, cols), lambda t, place_ref: (place_ref[1], 0))),
        out_shape=jax.ShapeDtypeStruct((2 * half, cols), f32),
        compiler_params=_cp("arbitrary"),
    )(place, landed, part)


def _grad_pair(shards):
    nk = len(shards)

    def body(*refs):
        outs = refs[nk:2 * nk]
        send_sems, recv_sems = refs[2 * nk:]
        x, y, c, _, _ = _place()

        def half(k, core):
            h = outs[k].shape[0] // 2
            return outs[k].at[pl.ds(core * h, h), :]

        cps = [_remote(half(k, c), half(k, c), send_sems.at[k], recv_sems.at[k], (x, y, 1 - c)) for k in range(nk)]
        for cp in cps:
            cp.start()
        for k in range(nk):
            _remote(half(k, 1 - c), half(k, 1 - c), send_sems.at[k], recv_sems.at[k], (x, y, 1 - c)).wait_recv()
        for cp in cps:
            cp.wait_send()

    return pl.pallas_call(
        body, name="grad_pair", in_specs=[ANY] * nk, out_specs=[ANY] * nk,
        out_shape=[jax.ShapeDtypeStruct(a.shape, f32) for a in shards],
        input_output_aliases={k: k for k in range(nk)},
        scratch_shapes=[pltpu.SemaphoreType.DMA((nk,)), pltpu.SemaphoreType.DMA((nk,))],
    )(*shards)


def _adam_math(w, g, m, v):
    m = ADAM_B1 * m + (1.0 - ADAM_B1) * g
    v = ADAM_B2 * v + (1.0 - ADAM_B2) * (g * g)
    m_hat = m / (1.0 - ADAM_B1 ** ADAM_STEP)
    v_hat = v / (1.0 - ADAM_B2 ** ADAM_STEP)
    delta = -ADAM_LR * (m_hat / (jnp.sqrt(v_hat) + ADAM_EPS) + ADAM_WD * w)
    return delta, m, v


def _adam(k, w, g, m, v):
    rows, cols = w.shape
    tr = next(rows // d for d in (1, 2, 4, 8) if rows % (8 * d) == 0 and rows // d <= 256)

    def body(w_ref, g_ref, m_ref, v_ref, d_ref, nm_ref, nv_ref):
        d_ref[...], nm_ref[...], nv_ref[...] = _adam_math(w_ref[...], g_ref[...], m_ref[...], v_ref[...])

    spec = _row(0, (tr, cols))
    return pl.pallas_call(
        body, name=f"adam_{k}", grid=(rows // tr,), in_specs=[spec] * 4, out_specs=[spec] * 3,
        out_shape=[jax.ShapeDtypeStruct(w.shape, f32)] * 3,
        compiler_params=_cp("parallel"),
    )(w, g, m, v)


class _SmallGather:
    def __init__(self, part, buf, send_sems, recv_sems):
        self.part, self.buf, self.send_sems, self.recv_sems = part, buf, send_sems, recv_sems
        self.x, self.y, self.c, _, self.others = _place()
        self.sibling = (self.x, self.y, 1 - self.c)

    def _copy(self, k, block, to, src=None):
        slot = self.buf.at[4 * block[0] + 2 * block[1] + block[2]]
        return _remote(slot if src is None else src, slot, self.send_sems.at[k], self.recv_sems.at[k], to)

    def _first(self):
        me = (self.x, self.y, self.c)
        return [self._copy(0, me, self.sibling, src=self.part)] + [
            self._copy(1 + j, me, (*chip, self.c), src=self.part) for j, chip in enumerate(self.others)]

    def _passed(self):
        return [self._copy(4 + j, (*chip, self.c), self.sibling) for j, chip in enumerate(self.others)]

    @staticmethod
    def scratch():
        return [pltpu.SemaphoreType.DMA((7,)), pltpu.SemaphoreType.DMA((7,))]

    def start(self):
        for cp in self._first():
            cp.start()

    def finish(self):
        passed = self._passed()
        for j, chip in enumerate(self.others):
            self._copy(1 + j, (*chip, self.c), self.sibling).wait_recv()
            passed[j].start()
        self._copy(0, self.sibling, self.sibling).wait_recv()
        for j, chip in enumerate(self.others):
            self._copy(4 + j, (*chip, 1 - self.c), self.sibling).wait_recv()
        for cp in self._first() + passed:
            cp.wait_send()


def _rs_sum_group(name, place, landed, parts, cross_parts, small_part):
    nk, nx = len(landed), len(cross_parts)
    dims = [a.shape[1:] for a in landed]
    rows = small_part.shape[0]

    def body(place_ref, *refs):
        l_refs, p_refs = refs[:nk], refs[nk:2 * nk]
        x_refs, sp_ref = refs[2 * nk:2 * nk + nx], refs[2 * nk + nx]
        outs = refs[2 * nk + nx + 1:]
        o_refs, xl_refs, sbuf = outs[:nk], outs[nk:nk + nx], outs[nk + nx]
        sems = outs[nk + nx + 1:]
        cross = _Cross(x_refs, xl_refs, sems[0], sems[1])
        small = _SmallGather(sp_ref, sbuf, sems[2], sems[3])
        t = pl.program_id(0)

        @pl.when(t == 0)
        def _():
            cross.start()
            small.start()

        me = place_ref[0]
        for l_ref, p_ref, o_ref in zip(l_refs, p_refs, o_refs):
            own = p_ref[0].astype(f32)
            acc = jnp.where(me == 0, own, l_ref[0].astype(f32))
            for j in range(1, N_CHIPS):
                acc = acc + jnp.where(me == j, own, l_ref[j].astype(f32))
            o_ref[...] = acc

        @pl.when(t == 1)
        def _():
            small.finish()
            cross.finish()

    def halves(h, c, lead, index):
        return pl.BlockSpec((lead, h // 2, c) if lead else (h // 2, c), index)

    in_specs = [halves(h, c, N_CHIPS, lambda t, pr: (0, t, 0)) for h, c in dims]
    in_specs += [halves(h, c, 1, lambda t, pr: (pr[0], t, 0)) for h, c in dims]
    in_specs += [ANY] * (nx + 1)
    out_specs = [halves(h, c, 0, lambda t, pr: (2 * pr[1] + t, 0)) for h, c in dims] + [ANY] * (nx + 1)
    out_shape = [jax.ShapeDtypeStruct((2 * h, c), f32) for h, c in dims]
    out_shape += [jax.ShapeDtypeStruct(a.shape, a.dtype) for a in cross_parts]
    out_shape.append(jax.ShapeDtypeStruct((N_DEV, rows, 128), f32))
    res = pl.pallas_call(
        body, name=f"rs_sum_{name}",
        grid_spec=pltpu.PrefetchScalarGridSpec(
            num_scalar_prefetch=1, grid=(2,), in_specs=in_specs, out_specs=out_specs,
            scratch_shapes=_Cross.scratch(nx) + _SmallGather.scratch()),
        out_shape=out_shape, compiler_params=_cp("arbitrary"),
    )(place, *landed, *parts, *cross_parts, small_part)
    return res[:nk], res[nk:nk + nx], res[nk + nx]


def _small_adam(dev, gathered, parts, w, m, v):
    rows = [p.shape[0] for p in parts]

    def body(dev_ref, b0_ref, b1_ref, p0_ref, p1_ref, w_ref, m_ref, v_ref, g_ref, d_ref, nm_ref, nv_ref):
        me = dev_ref[0]
        lo = 0
        for b_ref, p_ref, n_rows in ((b0_ref, p0_ref, rows[0]), (b1_ref, p1_ref, rows[1])):
            own = p_ref[...]
            g = jnp.where(me == 0, own, b_ref[0])
            for d in range(1, N_DEV):
                g = g + jnp.where(me == d, own, b_ref[d])
            sl = slice(lo, lo + n_rows)
            g_ref[sl, :] = g
            d_ref[sl, :], nm_ref[sl, :], nv_ref[sl, :] = _adam_math(w_ref[sl, :], g, m_ref[sl, :], v_ref[sl, :])
            lo += n_rows

    shape = jax.ShapeDtypeStruct(w.shape, f32)
    flat = pl.BlockSpec(w.shape, lambda t, dev_ref: (0, 0))
    whole = lambda a: pl.BlockSpec(a.shape, lambda t, dev_ref: (0,) * a.ndim)
    return pl.pallas_call(
        body, name="small_adam",
        grid_spec=pltpu.PrefetchScalarGridSpec(
            num_scalar_prefetch=1, grid=(1,),
            in_specs=[whole(gathered[0]), whole(gathered[1]), whole(parts[0]), whole(parts[1])] + [flat] * 3,
            out_specs=[flat] * 4),
        out_shape=[shape] * 4, compiler_params=_cp("arbitrary"),
    )(dev, *gathered, *parts, w, m, v)


def _rows128(a):
    return a.reshape(-1, 128)


def _pad_rows(a, rows):
    return jnp.concatenate([a, jnp.zeros((rows - a.shape[0],) + a.shape[1:], a.dtype)], axis=0)


def _pack_small(me, g_pg, g_post, g_final, g_ffn, ln_g, ln_b, conv_b, pool_scale, pool_w, conv_w, g_mix):
    blk_ple = _pad_rows(jnp.concatenate([g_pg, g_post, g_final.reshape(1, D_MODEL)], axis=0), 8)
    blk_ffn = _pad_rows(g_ffn, 8)
    blk_mix = _pad_rows(jnp.concatenate([ln_g, ln_b, conv_b, pool_scale], axis=0), 8)
    cw = lax.dynamic_update_slice(jnp.zeros((N_CHIPS, HALO, 128), f32), _pad_rows(conv_w, HALO)[None], (me, 0, 0))
    blk_in = _pad_rows(g_mix, 8)
    return jnp.concatenate([_rows128(blk_ple), _rows128(blk_ffn), _rows128(blk_mix), _rows128(pool_w),
                            _rows128(cw), _rows128(blk_in)], axis=0)


def _unpack_small(me, packed):
    o = 0

    def take(rows):
        nonlocal o
        blk = packed[o:o + rows]
        o += rows
        return blk

    ple = take(64).reshape(8, D_MODEL)
    ffn = take(64).reshape(8, D_MODEL)
    mixb = take(32).reshape(8, C_CONV)
    pool_w = take(512).reshape(1, 4, POOL_GROUP, POOL_GROUP)
    cw = take(N_CHIPS * HALO).reshape(N_CHIPS, HALO, 128)
    inb = take(64).reshape(8, D_MODEL)
    conv_w = lax.dynamic_slice(cw, (me, 0, 0), (1, HALO, 128))[:, :CONV_K, :]
    return dict(g_ple_gate=ple[0:1], g_ple_post=ple[1:2], g_final=ple[2], loss=ple[3, 0], g_ffn=ffn[0:1],
                ln_g=mixb[0:1], ln_b=mixb[1:2], conv_b=mixb[2:3], pool_scale=mixb[3:4], pool_w=pool_w,
                conv_w=conv_w, g_mix=inb[0:1])


def kernel(x, p, g_mix, w_in, conv_w, conv_b, ln_g, ln_b, pool_w, pool_scale, w_out, g_ffn, w_gate_up, w_down, g_ple_gate, w_ple_gate, w_ple_up, g_ple_post, g_final, loss_target, m_g_mix, m_w_in, m_conv_w, m_conv_b, m_ln_g, m_ln_b, m_pool_w, m_pool_scale, m_w_out, m_g_ffn, m_w_gate_up, m_w_down, m_g_ple_gate, m_w_ple_gate, m_w_ple_up, m_g_ple_post, m_g_final, v_g_mix, v_w_in, v_conv_w, v_conv_b, v_ln_g, v_ln_b, v_pool_w, v_pool_scale, v_w_out, v_g_ffn, v_w_gate_up, v_w_down, v_g_ple_gate, v_w_ple_gate, v_w_ple_up, v_g_ple_post, v_g_final):
    seq = x.shape[1]
    me = 2 * lax.axis_index("x") + lax.axis_index("y")
    chip = me.astype(jnp.int32).reshape(1)
    core = lax.axis_index("c").astype(jnp.int32).reshape(1)
    place = jnp.concatenate([chip, core])
    xs, ps, ts = x.reshape(seq, D_MODEL), p.reshape(seq, D_PLE), loss_target.reshape(seq, D_MODEL)

    big_names = ["w_in", "w_gu", "w_out", "w_down", "w_pg", "w_pu"]
    big = [w_in[0], w_gate_up[0], w_out[0], w_down[0], w_ple_gate[0], w_ple_up[0]]
    big_m = [m_w_in[0], m_w_gate_up[0], m_w_out[0], m_w_down[0], m_w_ple_gate[0], m_w_ple_up[0]]
    big_v = [v_w_in[0], v_w_gate_up[0], v_w_out[0], v_w_down[0], v_w_ple_gate[0], v_w_ple_up[0]]
    b_in, b_gu, b_out, b_down, b_pg, b_pu = [_cast_into_slot(nm, chip, w, bf16) for nm, w in zip(big_names, big)]
    b_cw = _cast_into_slot("conv_w", chip, _pad_rows(conv_w[0], HALO), f32)
    xi, yi = lax.axis_index("x"), lax.axis_index("y")
    order = jnp.stack([me, 2 * (1 - xi) + yi, 2 * xi + 1 - yi, 2 * (1 - xi) + 1 - yi]).astype(jnp.int32)

    z, (w_in_g, cw_g, w_out_g) = _mix_in(xs, g_mix, order, [b_in, b_cw, b_out])
    conv_w_f = cw_g.transpose(1, 0, 2).reshape(HALO, C_CONV)
    w_out_f = w_out_g.reshape(D_MODEL, D_MODEL)
    (x1, mix, u1, pooled), (w_gu_g, w_down_g) = _conv_pool_out(z, xs, conv_w_f, conv_b, ln_g, ln_b, pool_w[0],
                                                                pool_scale, w_out_f, [b_gu, b_down])
    w_down_f = w_down_g.reshape(D_FF, D_MODEL)
    (x2, h2, gu, ffn_f), (w_pg_g, w_pu_g) = _ffn_fwd(x1, g_ffn, w_gu_g, w_down_f, [b_pg, b_pu])
    w_pg_f = w_pg_g.reshape(D_MODEL, D_MODEL)
    dx2, d_w_pg, d_w_pu, small_ple = _ple_loss(x2, ps, ts, g_ple_gate, g_ple_post, g_final.reshape(1, D_MODEL),
                                               w_pg_f, w_pu_g)
    def pair_reduce(group, names, grads):
        got = _rs_pair(group, grads)
        return [_rs_add_pair(nm, core, g, r) for nm, g, r in zip(names, grads, got)]

    d_w_down = _ffn_bwd_dw_down(ffn_f, dx2)
    parts_a = pair_reduce("a", ["w_pg", "w_pu", "w_down"],
                          [d_w_pg.reshape(N_CHIPS, -1, D_MODEL), d_w_pu, d_w_down.reshape(N_CHIPS, -1, D_MODEL)])
    (dx1, dgu, small_ffn), landed_a = _ffn_bwd_dx(dx2, x1, gu, g_ffn, w_gu_g, w_down_f, parts_a)
    d_w_gu = _ffn_bwd_dw_gu(h2, dgu)
    du1, dpo, d_w_out, d_pool_w, small_mix = _mix_bwd_local(dx1, mix, u1, pooled, w_out_f, ln_g, ln_b, pool_w[0],
                                                             pool_scale)
    parts_b = pair_reduce("b", ["w_gu", "w_out"], [d_w_gu, d_w_out.reshape(N_CHIPS, -1, D_MODEL)])
    part_0 = jnp.concatenate([_rows128(small_ple), _rows128(small_ffn), _rows128(small_mix), _rows128(d_pool_w)],
                             axis=0)
    (grad_x, d_w_in, d_conv_w, small_in), landed_b, small_all_0 = _in_bwd(du1, dpo, z, xs, dx1, conv_w_f, g_mix,
                                                                            w_in_g, parts_b, part_0)
    parts_c = pair_reduce("c", ["w_in"], [d_w_in])
    cw_chunks = d_conv_w.reshape(HALO, N_CHIPS, 128).transpose(1, 0, 2)
    part_1 = jnp.concatenate([_rows128(cw_chunks), _rows128(small_in)], axis=0)
    (h_gu, h_out, h_down, h_pg, h_pu), landed_c, small_all_1 = _rs_sum_group(
        "ab", place, [landed_b[0], landed_b[1], landed_a[2], landed_a[0], landed_a[1]],
        [parts_b[0], parts_b[1], parts_a[2], parts_a[0], parts_a[1]], parts_c, part_1)
    h_in = _rs_sum_chips("w_in", place, landed_c[0], parts_c[0])
    big_g = _grad_pair([h_in, h_gu, h_out, h_down, h_pg, h_pu])
    big_upd = [_adam(nm, w, g, m, v) for nm, w, g, m, v in zip(big_names, big, big_g, big_m, big_v)]

    sw = _pack_small(me, g_ple_gate, g_ple_post, g_final, g_ffn, ln_g, ln_b, conv_b, pool_scale, pool_w, conv_w[0], g_mix)
    sm = _pack_small(me, m_g_ple_gate, m_g_ple_post, m_g_final, m_g_ffn, m_ln_g, m_ln_b, m_conv_b, m_pool_scale,
                     m_pool_w, m_conv_w[0], m_g_mix)
    sv = _pack_small(me, v_g_ple_gate, v_g_ple_post, v_g_final, v_g_ffn, v_ln_g, v_ln_b, v_conv_b, v_pool_scale,
                     v_pool_w, v_conv_w[0], v_g_mix)
    dev = (2 * chip + core).astype(jnp.int32)
    small = [_unpack_small(me, a) for a in _small_adam(dev, [small_all_0, small_all_1], [part_0, part_1], sw, sm, sv)]

    names = ["g_mix", "w_in", "conv_w", "conv_b", "ln_g", "ln_b", "pool_w", "pool_scale", "w_out", "g_ffn",
             "w_gate_up", "w_down", "g_ple_gate", "w_ple_gate", "w_ple_up", "g_ple_post", "g_final"]
    big_at = {"w_in": 0, "w_gate_up": 1, "w_out": 2, "w_down": 3, "w_ple_gate": 4, "w_ple_up": 5}
    out = [small[0]["loss"], grad_x.reshape(1, seq, D_MODEL)]
    for kind in range(4):
        for nm in names:
            if nm in big_at:
                k = big_at[nm]
                out.append((big_g[k] if kind == 0 else big_upd[k][kind - 1])[None])
            else:
                out.append(small[kind][nm])
    return tuple(out)
```

```python
import functools

import jax
import jax.numpy as jnp
from jax import lax
from jax.experimental import pallas as pl
from jax.experimental.pallas import tpu as pltpu

f32, bf16 = jnp.float32, jnp.bfloat16

EPS = 1e-6
D_MODEL = 1024
C_CONV = 512
C_POOL = 512
POOL_WINDOWS = (2, 4, 8, 16)
POOL_GROUP = 128
CONV_K = 31
D_FF = 2816
D_PLE = 256
N_CHIPS = 4
N_DEV = 8
W_IN_COLS = 2 * C_CONV + C_POOL
W_IN_CHUNK = W_IN_COLS // N_CHIPS
FF_CHUNK = 2 * D_FF // N_CHIPS
PLE_CHUNK = D_MODEL // N_CHIPS
HALO = 32
ROW_TILE = 512
CONV_ROWS = 64
CONV_COLS = (slice(0, 256), slice(256, 512))
SHIFT_PAD = 32
SHIFT_GROUPS = 16
FF_SUB = (0, 512, 1024, FF_CHUNK)
VMEM_LIMIT = 56 * 1024 * 1024

ADAM_LR = 0.001
ADAM_B1 = 0.9
ADAM_B2 = 0.999
ADAM_EPS = 1e-08
ADAM_WD = 0.01
ADAM_STEP = 10

MESH = pl.DeviceIdType.MESH
ANY = pl.BlockSpec(memory_space=pl.ANY)
VMEM = pl.BlockSpec(memory_space=pltpu.VMEM)


def _cp(*sem):
    return pltpu.CompilerParams(dimension_semantics=sem, vmem_limit_bytes=VMEM_LIMIT)


def _dot(a, b):
    return jnp.dot(a, b, preferred_element_type=f32)


def _dot_nt(a, b):
    return lax.dot_general(a, b, (((1,), (1,)), ((), ())), preferred_element_type=f32)


def _dot_tn(a, b):
    return lax.dot_general(a, b, (((0,), (0,)), ((), ())), preferred_element_type=f32)


def _sigmoid(v):
    return jax.nn.sigmoid(v)


def _rms_fwd(v, g):
    r = lax.rsqrt(jnp.mean(v * v, axis=-1, keepdims=True) + EPS)
    vh = v * r
    return vh * g, vh, r


def _rms_bwd(dy, vh, r, g):
    dvh = dy * g
    dv = r * (dvh - vh * jnp.mean(dvh * vh, axis=-1, keepdims=True))
    return dv, jnp.sum(dy * vh, axis=0, keepdims=True)


def _silu_grad(v, s):
    return s * (1.0 + v * (1.0 - s))


def _row(i, n):
    return pl.BlockSpec((n[0], n[1]), lambda *a: (a[i], 0))


def _full(shape):
    nd = len(shape)
    return pl.BlockSpec(shape, lambda *a: (0,) * nd)


def _place():
    x, y, c = lax.axis_index("x"), lax.axis_index("y"), lax.axis_index("c")
    others = [(1 - x, y), (x, 1 - y), (1 - x, 1 - y)]
    return x, y, c, 2 * x + y, others


def _remote(src, dst, send_sem, recv_sem, dev):
    return pltpu.make_async_remote_copy(src_ref=src, dst_ref=dst, send_sem=send_sem, recv_sem=recv_sem,
                                        device_id=dev, device_id_type=MESH)


def _cast_into_slot(name, me, w, dtype):
    rows, cols = w.shape
    tr = next(rows // d for d in (1, 2, 4, 8) if rows % (16 * d) == 0 and rows // d <= 512)

    def body(me_ref, w_ref, o_ref):
        o_ref[0] = w_ref[...].astype(dtype)

    return pl.pallas_call(
        body, name=f"cast_{name}",
        grid_spec=pltpu.PrefetchScalarGridSpec(
            num_scalar_prefetch=1, grid=(rows // tr,),
            in_specs=[pl.BlockSpec((tr, cols), lambda r, me_ref: (r, 0))],
            out_specs=pl.BlockSpec((1, tr, cols), lambda r, me_ref: (me_ref[0], r, 0))),
        out_shape=jax.ShapeDtypeStruct((N_CHIPS, rows, cols), dtype),
        compiler_params=_cp("parallel"),
    )(me, w)


class _Gather:
    def __init__(self, bufs, send_sems, recv_sems):
        self.bufs, self.send_sems, self.recv_sems = bufs, send_sems, recv_sems
        self.x, self.y, self.c, self.me, self.others = _place()
        self.halves = [b.shape[1] // 2 for b in bufs]

    def _piece(self, k, chip, half):
        return self.bufs[k].at[chip, pl.ds(half * self.halves[k], self.halves[k]), :]

    def _ici(self, k, j, chip):
        ox, oy = self.others[j]
        piece = self._piece(k, chip, self.c)
        return _remote(piece, piece, self.send_sems.at[6 * k + j], self.recv_sems.at[6 * k + j], (ox, oy, self.c))

    def _relay(self, k):
        first = self.c == 0
        piece = self._piece(k, jnp.where(first, self.chip(0), self.chip(1)), self.c)
        to = (jnp.where(first, self.others[1][0], self.others[0][0]),
              jnp.where(first, self.others[1][1], self.others[0][1]), self.c)
        return _remote(piece, piece, self.send_sems.at[6 * k + 2], self.recv_sems.at[6 * k + 2], to)

    def _pair(self, k, j, half):
        ox, oy = self.others[j]
        piece = self._piece(k, 2 * ox + oy, half)
        return _remote(piece, piece, self.send_sems.at[6 * k + 3 + j], self.recv_sems.at[6 * k + 3 + j],
                       (self.x, self.y, 1 - self.c))

    def _each(self, ks=None):
        return [(k, j) for k in (range(len(self.bufs)) if ks is None else ks) for j in range(3)]

    def chip(self, j):
        ox, oy = self.others[j]
        return 2 * ox + oy

    def start(self):
        for k in range(len(self.bufs)):
            for j in range(2):
                self._ici(k, j, self.me).start()

    def forward(self, pairs=None):
        for k, j in self._each() if pairs is None else pairs:
            self._ici(k, j, self.chip(j)).wait_recv()
            self._pair(k, j, self.c).start()
            if j < 2:
                pl.when(self.c == j)(self._relay(k).start)

    def landed(self, pairs):
        for k, j in pairs:
            self._pair(k, j, 1 - self.c).wait_recv()

    def finish(self, ks=None):
        self.landed(self._each(ks))
        for k in range(len(self.bufs)):
            for j in range(2):
                self._ici(k, j, self.me).wait_send()
            self._relay(k).wait_send()
            for j in range(3):
                self._pair(k, j, self.c).wait_send()

    @staticmethod
    def scratch(n):
        return [pltpu.SemaphoreType.DMA((6 * n,)), pltpu.SemaphoreType.DMA((6 * n,))]


def _carried(bufs):
    n = len(bufs)
    return dict(in_specs=[ANY] * n, out_specs=[ANY] * n,
                out_shape=[jax.ShapeDtypeStruct(b.shape, b.dtype) for b in bufs], scratch=_Gather.scratch(n))


def _mix_in(x, g_mix, order, carry):
    s = x.shape[0]
    tm = min(2 * ROW_TILE, s)
    n = s // tm
    nc = len(carry)
    cs = _carried(carry)

    def body(order_ref, x_ref, g_ref, *refs):
        z_ref = refs[nc]
        bufs = refs[nc + 1:2 * nc + 1]
        h_ref, w_ref, w_sem = refs[2 * nc + 1:2 * nc + 4]
        gather = _Gather(bufs, *refs[2 * nc + 4:])
        q, i = pl.program_id(0), pl.program_id(1)
        first = i == 0
        pl.when(jnp.logical_and(q == 0, first))(gather.start)
        for j in range(3):

            @pl.when(jnp.logical_and(q == j + 1, first))
            def _():
                gather.forward([(0, j)])
                gather.landed([(0, j)])
                if j == 1:
                    gather.forward([(k, jj) for k in range(1, nc) for jj in range(2)])

        @pl.when(first)
        def _():
            load = pltpu.make_async_copy(bufs[0].at[order_ref[q]], w_ref, w_sem)
            load.start()
            load.wait()

        @pl.when(q == 0)
        def _():
            h, _, _ = _rms_fwd(x_ref[...], g_ref[...])
            h_ref[i] = h.astype(bf16)

        z_ref[...] = _dot(h_ref[i], w_ref[...])

        @pl.when(jnp.logical_and(q == N_CHIPS - 1, i == n - 1))
        def _():
            gather.forward([(k, 2) for k in range(1, nc)])
            gather.finish(range(1, nc))

    res = pl.pallas_call(
        body, name="mix_in",
        grid_spec=pltpu.PrefetchScalarGridSpec(
            num_scalar_prefetch=1, grid=(N_CHIPS, n),
            in_specs=[pl.BlockSpec((tm, D_MODEL), lambda q, i, order_ref: (jnp.where(q == 0, i, 0), 0)),
                      pl.BlockSpec((1, D_MODEL), lambda q, i, order_ref: (0, 0))] + cs["in_specs"],
            out_specs=[pl.BlockSpec((tm, W_IN_CHUNK), lambda q, i, order_ref: (i, order_ref[q]))] + cs["out_specs"],
            scratch_shapes=[pltpu.VMEM((n, tm, D_MODEL), bf16), pltpu.VMEM((D_MODEL, W_IN_CHUNK), bf16),
                            pltpu.SemaphoreType.DMA(())] + cs["scratch"]),
        out_shape=[jax.ShapeDtypeStruct((s, W_IN_COLS), f32)] + cs["out_shape"],
        input_output_aliases={3 + k: 1 + k for k in range(nc)},
        compiler_params=_cp("arbitrary", "arbitrary"),
    )(order, x, g_mix, *carry)
    return res[0], res[1:]


def _tap_offsets(lo, hi):
    groups = [[o for o in range(lo, hi + 1) if o % 8 == s] for s in range(8)]
    return [g for g in groups if g]


def _tap_sum(buf, w_ref, row0, cols, tap_of):
    acc = jnp.zeros((CONV_ROWS, cols.stop - cols.start), f32)
    for offs in _tap_offsets(0, CONV_K - 1):
        slab = buf[pl.ds(row0 + offs[0], offs[-1] - offs[0] + CONV_ROWS), cols]
        for o in offs:
            acc = acc + w_ref[pl.ds(tap_of(o), 1), cols] * slab[o - offs[0]:o - offs[0] + CONV_ROWS]
    return acc


def _pool_counts(tm, w, first_row):
    t1 = (lax.broadcasted_iota(jnp.int32, (tm, 1), 0) + first_row + 1).astype(f32)
    return jnp.minimum(t1, float(w))


def _conv_pool_out(z, x, conv_w, conv_b, ln_g, ln_b, pool_w, pool_scale, w_out, carry):
    s = x.shape[0]
    tm = min(ROW_TILE, s)
    n = s // tm
    hb = tm // HALO
    nc = len(carry)
    cs = _carried(carry)

    def body(z_ref, zp_ref, x_ref, cw_ref, cb_ref, lg_ref, lb_ref, pw_ref, ps_ref, wo_ref, *refs):
        x1_ref, mix_ref, u1_ref, pooled_ref = refs[nc:nc + 4]
        ubuf, vbuf = refs[2 * nc + 4:2 * nc + 6]
        gather = _Gather(refs[nc + 4:2 * nc + 4], *refs[2 * nc + 6:])
        i = pl.program_id(0)
        pl.when(i == 0)(gather.start)
        for k in range(nc):
            pl.when(i == min(n // 2 + 2 * k, n - 1))(functools.partial(gather.forward, [(k, 0), (k, 1)]))
        keep = (i > 0).astype(f32)
        zp = zp_ref[...] * keep
        ubuf[0:HALO, :] = zp[:, :C_CONV] * _sigmoid(zp[:, C_CONV:2 * C_CONV])
        vbuf[0:HALO, :] = zp[:, 2 * C_CONV:]
        ubuf[HALO:, :] = z_ref[:, :C_CONV] * _sigmoid(z_ref[:, C_CONV:2 * C_CONV])
        vbuf[HALO:, :] = z_ref[:, 2 * C_CONV:]
        off = HALO - (CONV_K - 1)
        for r0 in range(0, tm, CONV_ROWS):
            for cols in CONV_COLS:
                u1_ref[r0:r0 + CONV_ROWS, cols] = cb_ref[:, cols] + _tap_sum(ubuf, cw_ref, r0 + off, cols, lambda o: o)
        u1 = u1_ref[...]
        mu = jnp.mean(u1, axis=-1, keepdims=True)
        uc = u1 - mu
        rstd = lax.rsqrt(jnp.mean(uc * uc, axis=-1, keepdims=True) + EPS)
        u2 = uc * rstd * lg_ref[...] + lb_ref[...]
        mix_ref[:, :C_CONV] = (u2 * _sigmoid(u2)).astype(bf16)
        for g, w in enumerate(POOL_WINDOWS):
            cols = slice(g * POOL_GROUP, (g + 1) * POOL_GROUP)
            acc = vbuf[pl.ds(HALO, tm), cols]
            vg = acc
            for d in range(1, w):
                acc = acc + vbuf[pl.ds(HALO - d, tm), cols]
            pooled = (acc / _pool_counts(tm, w, i * tm) - vg).astype(bf16)
            pooled_ref[:, cols] = pooled
            mixed = _dot(pooled, pw_ref[g].astype(bf16))
            mix_ref[:, C_CONV + g * POOL_GROUP:C_CONV + (g + 1) * POOL_GROUP] = (mixed * ps_ref[:, cols]).astype(bf16)
        x1_ref[...] = x_ref[...] + _dot(mix_ref[...], wo_ref[...])
        @pl.when(i == n - 1)
        def _():
            gather.forward([(k, 2) for k in range(nc)])
            gather.finish()

    res = pl.pallas_call(
        body, name="conv_pool_out", grid=(n,),
        in_specs=[_row(0, (tm, W_IN_COLS)),
                  pl.BlockSpec((HALO, W_IN_COLS), lambda i: (jnp.maximum(i * hb - 1, 0), 0)),
                  _row(0, (tm, D_MODEL)), _full((HALO, C_CONV)), _full((1, C_CONV)), _full((1, C_CONV)),
                  _full((1, C_CONV)), _full((4, POOL_GROUP, POOL_GROUP)), _full((1, C_POOL)),
                  _full((D_MODEL, D_MODEL))] + cs["in_specs"],
        out_specs=[_row(0, (tm, D_MODEL)), _row(0, (tm, D_MODEL)), _row(0, (tm, C_CONV)), _row(0, (tm, C_POOL))]
        + cs["out_specs"],
        out_shape=[jax.ShapeDtypeStruct((s, D_MODEL), f32), jax.ShapeDtypeStruct((s, D_MODEL), bf16),
                   jax.ShapeDtypeStruct((s, C_CONV), f32), jax.ShapeDtypeStruct((s, C_POOL), bf16)] + cs["out_shape"],
        input_output_aliases={10 + k: 4 + k for k in range(nc)},
        scratch_shapes=[pltpu.VMEM((HALO + tm, C_CONV), f32), pltpu.VMEM((HALO + tm, C_POOL), f32)] + cs["scratch"],
        compiler_params=_cp("arbitrary"),
    )(z, z, x, conv_w, conv_b, ln_g, ln_b, pool_w, pool_scale, w_out, *carry)
    return res[:4], res[4:]


def _ffn_fwd(x1, g_ffn, w_gu_g, w_down, carry):
    s = x1.shape[0]
    tm = min(ROW_TILE, s)
    n = s // tm
    nc = len(carry)
    cs = _carried(carry)

    def body(x1_ref, g_ref, wg_ref, wu_ref, wd_ref, *refs):
        x2_ref, h2_ref, gu_ref, f_ref = refs[nc:nc + 4]
        acc_ref = refs[2 * nc + 4]
        gather = _Gather(refs[nc + 4:2 * nc + 4], *refs[2 * nc + 5:])
        i, c = pl.program_id(0), pl.program_id(1)
        pl.when(jnp.logical_and(i == 0, c == 0))(gather.start)
        direct = [(k, j) for k in range(nc) for j in range(2)]
        pl.when(jnp.logical_and(i == n // 2, c == 0))(functools.partial(gather.forward, direct))
        pl.when(jnp.logical_and(i == n - 1, c == 0))(functools.partial(gather.forward, [(k, 2) for k in range(nc)]))

        @pl.when(c == 0)
        def _():
            h, _, _ = _rms_fwd(x1_ref[...], g_ref[...])
            h2_ref[...] = h.astype(bf16)
            acc_ref[...] = jnp.zeros_like(acc_ref)

        h = h2_ref[...]
        for lo, hi in zip(FF_SUB[:-1], FF_SUB[1:]):
            gate = _dot(h, wg_ref[0, :, lo:hi])
            up = _dot(h, wu_ref[0, :, lo:hi])
            gu_ref[0, :, lo:hi] = gate.astype(bf16)
            gu_ref[1, :, lo:hi] = up.astype(bf16)
            f = (gate * _sigmoid(gate) * up).astype(bf16)
            f_ref[:, lo:hi] = f
            acc_ref[...] += _dot(f, wd_ref[lo:hi, :])

        @pl.when(c == 1)
        def _():
            x2_ref[...] = x1_ref[...] + acc_ref[...]

        pl.when(jnp.logical_and(i == n - 1, c == 1))(gather.finish)

    res = pl.pallas_call(
        body, name="ffn_fwd", grid=(n, 2),
        in_specs=[_row(0, (tm, D_MODEL)), _full((1, D_MODEL)),
                  pl.BlockSpec((1, D_MODEL, FF_CHUNK), lambda i, c: (c, 0, 0)),
                  pl.BlockSpec((1, D_MODEL, FF_CHUNK), lambda i, c: (2 + c, 0, 0)),
                  pl.BlockSpec((FF_CHUNK, D_MODEL), lambda i, c: (c, 0))] + cs["in_specs"],
        out_specs=[_row(0, (tm, D_MODEL)), _row(0, (tm, D_MODEL)),
                   pl.BlockSpec((2, tm, FF_CHUNK), lambda i, c: (0, i, c)),
                   pl.BlockSpec((tm, FF_CHUNK), lambda i, c: (i, c))] + cs["out_specs"],
        out_shape=[jax.ShapeDtypeStruct((s, D_MODEL), f32), jax.ShapeDtypeStruct((s, D_MODEL), bf16),
                   jax.ShapeDtypeStruct((2, s, D_FF), bf16), jax.ShapeDtypeStruct((s, D_FF), bf16)] + cs["out_shape"],
        input_output_aliases={5 + k: 4 + k for k in range(nc)},
        scratch_shapes=[pltpu.VMEM((tm, D_MODEL), f32)] + cs["scratch"],
        compiler_params=_cp("arbitrary", "arbitrary"),
    )(x1, g_ffn, w_gu_g, w_gu_g, w_down, *carry)
    return res[:4], res[4:]


def _ple_loss(x2, p, target, g_pg, g_post, g_final, w_pg, w_pu_g):
    s = x2.shape[0]
    tm = min(ROW_TILE, s)
    n = s // tm

    def body(x2_ref, p_ref, t_ref, gpg_ref, gpo_ref, gf_ref, wpg_ref, wpu_ref,
             dx2_ref, dwpg_ref, dwpu_ref, small_ref, apg_ref, apu_ref):
        i = pl.program_id(0)

        @pl.when(i == 0)
        def _():
            apg_ref[...] = jnp.zeros_like(apg_ref)
            apu_ref[...] = jnp.zeros_like(apu_ref)
            small_ref[...] = jnp.zeros_like(small_ref)

        x2 = x2_ref[...]
        h3, x2h, r2 = _rms_fwd(x2, gpg_ref[...])
        h3b = h3.astype(bf16)
        gate = _sigmoid(_dot(h3b, wpg_ref[...]))
        pb = p_ref[...].astype(bf16)
        pe = jnp.concatenate([_dot(pb, wpu_ref[j]) for j in range(N_CHIPS)], axis=-1)
        e, peh, rp = _rms_fwd(pe, gpo_ref[...])
        x3 = x2 + gate * e
        y, x3h, r3 = _rms_fwd(x3, gf_ref[...])
        d = y - t_ref[...]
        loss = 0.5 * jnp.sum(jnp.sum(d * d, axis=-1, keepdims=True) * (1.0 / D_MODEL), axis=0, keepdims=True)
        dx3, dgf = _rms_bwd(d * (1.0 / D_MODEL), x3h, r3, gf_ref[...])
        dpe, dgpo = _rms_bwd(dx3 * gate, peh, rp, gpo_ref[...])
        dgl = (dx3 * e * gate * (1.0 - gate)).astype(bf16)
        apg_ref[...] += _dot_tn(h3b, dgl)
        apu_ref[...] += _dot_tn(pb, dpe.astype(bf16))
        dh3 = _dot_nt(dgl, wpg_ref[...])
        dx2b, dgpg = _rms_bwd(dh3, x2h, r2, gpg_ref[...])
        dx2_ref[...] = dx3 + dx2b
        small_ref[0:1, :] += dgpg
        small_ref[1:2, :] += dgpo
        small_ref[2:3, :] += dgf
        small_ref[3:4, :] += jnp.broadcast_to(loss, (1, D_MODEL))

        @pl.when(i == n - 1)
        def _():
            dwpg_ref[...] = apg_ref[...].astype(bf16)
            for j in range(N_CHIPS):
                dwpu_ref[j] = apu_ref[:, j * PLE_CHUNK:(j + 1) * PLE_CHUNK].astype(bf16)

    return pl.pallas_call(
        body, name="ple_loss", grid=(n,),
        in_specs=[_row(0, (tm, D_MODEL)), _row(0, (tm, D_PLE)), _row(0, (tm, D_MODEL)),
                  _full((1, D_MODEL)), _full((1, D_MODEL)), _full((1, D_MODEL)),
                  _full((D_MODEL, D_MODEL)), _full((N_CHIPS, D_PLE, PLE_CHUNK))],
        out_specs=[_row(0, (tm, D_MODEL)), _full((D_MODEL, D_MODEL)), _full((N_CHIPS, D_PLE, PLE_CHUNK)),
                   _full((8, D_MODEL))],
        out_shape=[jax.ShapeDtypeStruct((s, D_MODEL), f32), jax.ShapeDtypeStruct((D_MODEL, D_MODEL), bf16),
                   jax.ShapeDtypeStruct((N_CHIPS, D_PLE, PLE_CHUNK), bf16), jax.ShapeDtypeStruct((8, D_MODEL), f32)],
        scratch_shapes=[pltpu.VMEM((D_MODEL, D_MODEL), f32), pltpu.VMEM((D_PLE, D_MODEL), f32)],
        compiler_params=_cp("arbitrary"),
    )(x2, p, target, g_pg, g_post, g_final, w_pg, w_pu_g)


def _crossed(parts):
    n = len(parts)
    return dict(in_specs=[ANY] * n, out_specs=[ANY] * n,
                out_shape=[jax.ShapeDtypeStruct(a.shape, a.dtype) for a in parts], scratch=_Cross.scratch(n))


def _ffn_bwd_dx(dx2, x1, gu, g_ffn, w_gu_g, w_down, parts):
    s = x1.shape[0]
    tm = min(ROW_TILE, s)
    n = s // tm
    nc = len(parts)
    cs = _crossed(parts)

    def body(dx2_ref, x1_ref, gu_ref, g_ref, wg_ref, wu_ref, wd_ref, *refs):
        dx1_ref, dgu_ref, small_ref = refs[nc:nc + 3]
        acc_ref = refs[2 * nc + 3]
        cross = _Cross(refs[:nc], refs[nc + 3:2 * nc + 3], *refs[2 * nc + 4:])
        i, c = pl.program_id(0), pl.program_id(1)
        pl.when(jnp.logical_and(i == 0, c == 0))(cross.start)

        @pl.when(jnp.logical_and(i == 0, c == 0))
        def _():
            small_ref[...] = jnp.zeros_like(small_ref)

        @pl.when(c == 0)
        def _():
            acc_ref[...] = jnp.zeros_like(acc_ref)

        dyb = dx2_ref[...].astype(bf16)
        for lo, hi in zip(FF_SUB[:-1], FF_SUB[1:]):
            df = _dot_nt(dyb, wd_ref[lo:hi, :])
            gate = gu_ref[0, :, lo:hi].astype(f32)
            up = gu_ref[1, :, lo:hi].astype(f32)
            sg = _sigmoid(gate)
            dgate = (df * up * _silu_grad(gate, sg)).astype(bf16)
            dup = (df * gate * sg).astype(bf16)
            dgu_ref[0, :, lo:hi] = dgate
            dgu_ref[1, :, lo:hi] = dup
            acc_ref[...] += _dot_nt(dgate, wg_ref[0, :, lo:hi]) + _dot_nt(dup, wu_ref[0, :, lo:hi])

        @pl.when(c == 1)
        def _():
            _, x1h, r1 = _rms_fwd(x1_ref[...], g_ref[...])
            dx1b, dg = _rms_bwd(acc_ref[...], x1h, r1, g_ref[...])
            dx1_ref[...] = dx2_ref[...] + dx1b
            small_ref[0:1, :] += dg

        pl.when(jnp.logical_and(i == n - 1, c == 1))(cross.finish)

    res = pl.pallas_call(
        body, name="ffn_bwd_dx", grid=(n, 2),
        in_specs=[_row(0, (tm, D_MODEL)), _row(0, (tm, D_MODEL)),
                  pl.BlockSpec((2, tm, FF_CHUNK), lambda i, c: (0, i, c)), _full((1, D_MODEL)),
                  pl.BlockSpec((1, D_MODEL, FF_CHUNK), lambda i, c: (c, 0, 0)),
                  pl.BlockSpec((1, D_MODEL, FF_CHUNK), lambda i, c: (2 + c, 0, 0)),
                  pl.BlockSpec((FF_CHUNK, D_MODEL), lambda i, c: (c, 0))] + cs["in_specs"],
        out_specs=[_row(0, (tm, D_MODEL)), pl.BlockSpec((2, tm, FF_CHUNK), lambda i, c: (0, i, c)),
                   _full((8, D_MODEL))] + cs["out_specs"],
        out_shape=[jax.ShapeDtypeStruct((s, D_MODEL), f32), jax.ShapeDtypeStruct((2, s, D_FF), bf16),
                   jax.ShapeDtypeStruct((8, D_MODEL), f32)] + cs["out_shape"],
        scratch_shapes=[pltpu.VMEM((tm, D_MODEL), f32)] + cs["scratch"],
        compiler_params=_cp("arbitrary", "arbitrary"),
    )(dx2, x1, gu, g_ffn, w_gu_g, w_gu_g, w_down, *parts)
    return res[:3], res[3:]


def _ffn_bwd_dw_gu(h2, dgu):
    s = h2.shape[0]
    ts = min(2 * ROW_TILE, s)
    n = s // ts

    def body(h_ref, d_ref, o_ref, acc_ref):
        t = pl.program_id(1)

        @pl.when(t == 0)
        def _():
            acc_ref[...] = jnp.zeros_like(acc_ref)

        acc_ref[...] += _dot_tn(h_ref[...], d_ref[0])

        @pl.when(t == n - 1)
        def _():
            o_ref[0] = acc_ref[...].astype(bf16)

    return pl.pallas_call(
        body, name="ffn_bwd_dw_gu", grid=(N_CHIPS, n),
        in_specs=[pl.BlockSpec((ts, D_MODEL), lambda j, t: (t, 0)),
                  pl.BlockSpec((1, ts, FF_CHUNK), lambda j, t: (j // 2, t, j % 2))],
        out_specs=pl.BlockSpec((1, D_MODEL, FF_CHUNK), lambda j, t: (j, 0, 0)),
        out_shape=jax.ShapeDtypeStruct((N_CHIPS, D_MODEL, FF_CHUNK), bf16),
        scratch_shapes=[pltpu.VMEM((D_MODEL, FF_CHUNK), f32)],
        compiler_params=_cp("parallel", "arbitrary"),
    )(h2, dgu)


def _ffn_bwd_dw_down(f, dx2):
    s = dx2.shape[0]
    ts = min(2 * ROW_TILE, s)
    n = s // ts

    def body(f_ref, d_ref, o_ref, acc_ref):
        t = pl.program_id(1)

        @pl.when(t == 0)
        def _():
            acc_ref[...] = jnp.zeros_like(acc_ref)

        acc_ref[...] += _dot_tn(f_ref[...], d_ref[...].astype(bf16))

        @pl.when(t == n - 1)
        def _():
            o_ref[...] = acc_ref[...].astype(bf16)

    return pl.pallas_call(
        body, name="ffn_bwd_dw_down", grid=(2, n),
        in_specs=[pl.BlockSpec((ts, FF_CHUNK), lambda c, t: (t, c)),
                  pl.BlockSpec((ts, D_MODEL), lambda c, t: (t, 0))],
        out_specs=pl.BlockSpec((FF_CHUNK, D_MODEL), lambda c, t: (c, 0)),
        out_shape=jax.ShapeDtypeStruct((D_FF, D_MODEL), bf16),
        scratch_shapes=[pltpu.VMEM((FF_CHUNK, D_MODEL), f32)],
        compiler_params=_cp("parallel", "arbitrary"),
    )(f, dx2)


def _mix_bwd_local(dx1, mix, u1, pooled, w_out, ln_g, ln_b, pool_w, pool_scale):
    s = dx1.shape[0]
    tm = min(ROW_TILE, s)
    n = s // tm

    def body(dx1_ref, mix_ref, u1_ref, po_ref, wo_ref, lg_ref, lb_ref, pw_ref, ps_ref,
             du1_ref, dpo_ref, dwo_ref, dpw_ref, small_ref, awo_ref):
        i = pl.program_id(0)

        @pl.when(i == 0)
        def _():
            awo_ref[...] = jnp.zeros_like(awo_ref)
            dpw_ref[...] = jnp.zeros_like(dpw_ref)
            small_ref[...] = jnp.zeros_like(small_ref)

        dyb = dx1_ref[...].astype(bf16)
        dmix = _dot_nt(dyb, wo_ref[...])
        awo_ref[...] += _dot_tn(mix_ref[...], dyb)
        u1 = u1_ref[...]
        mu = jnp.mean(u1, axis=-1, keepdims=True)
        uc = u1 - mu
        rstd = lax.rsqrt(jnp.mean(uc * uc, axis=-1, keepdims=True) + EPS)
        uh = uc * rstd
        u2 = uh * lg_ref[...] + lb_ref[...]
        du2 = dmix[:, :C_CONV] * _silu_grad(u2, _sigmoid(u2))
        duh = du2 * lg_ref[...]
        du1 = rstd * (duh - jnp.mean(duh, axis=-1, keepdims=True) - uh * jnp.mean(duh * uh, axis=-1, keepdims=True))
        du1_ref[...] = du1
        small_ref[0:1, :] += jnp.sum(du2 * uh, axis=0, keepdims=True)
        small_ref[1:2, :] += jnp.sum(du2, axis=0, keepdims=True)
        small_ref[2:3, :] += jnp.sum(du1, axis=0, keepdims=True)
        for g in range(len(POOL_WINDOWS)):
            cols = slice(g * POOL_GROUP, (g + 1) * POOL_GROUP)
            dq = dmix[:, C_CONV + g * POOL_GROUP:C_CONV + (g + 1) * POOL_GROUP]
            pwb = pw_ref[g].astype(bf16)
            pg = po_ref[:, cols]
            mixed = _dot(pg, pwb)
            small_ref[3:4, cols] += jnp.sum(dq * mixed, axis=0, keepdims=True)
            dmixed = (dq * ps_ref[:, cols]).astype(bf16)
            dpw_ref[g] += _dot_tn(pg, dmixed)
            dpo_ref[:, cols] = _dot_nt(dmixed, pwb)

        @pl.when(i == n - 1)
        def _():
            dwo_ref[...] = awo_ref[...].astype(bf16)

    return pl.pallas_call(
        body, name="mix_bwd_local", grid=(n,),
        in_specs=[_row(0, (tm, D_MODEL)), _row(0, (tm, D_MODEL)), _row(0, (tm, C_CONV)), _row(0, (tm, C_POOL)),
                  _full((D_MODEL, D_MODEL)), _full((1, C_CONV)), _full((1, C_CONV)),
                  _full((4, POOL_GROUP, POOL_GROUP)), _full((1, C_POOL))],
        out_specs=[_row(0, (tm, C_CONV)), _row(0, (tm, C_POOL)), _full((D_MODEL, D_MODEL)),
                   _full((4, POOL_GROUP, POOL_GROUP)), _full((8, C_CONV))],
        out_shape=[jax.ShapeDtypeStruct((s, C_CONV), f32), jax.ShapeDtypeStruct((s, C_POOL), f32),
                   jax.ShapeDtypeStruct((D_MODEL, D_MODEL), bf16),
                   jax.ShapeDtypeStruct((4, POOL_GROUP, POOL_GROUP), f32), jax.ShapeDtypeStruct((8, C_CONV), f32)],
        scratch_shapes=[pltpu.VMEM((D_MODEL, D_MODEL), f32)],
        compiler_params=_cp("arbitrary"),
    )(dx1, mix, u1, pooled, w_out, ln_g, ln_b, pool_w, pool_scale)


def _in_bwd(du1, dpo, z, x, dx1, conv_w, g_mix, w_in_g, parts, small_part):
    s = x.shape[0]
    tm = min(ROW_TILE, s)
    n = s // tm
    hb = tm // HALO
    last = s // HALO - 1
    nv = tm // 8
    assert nv >= SHIFT_PAD and nv % SHIFT_GROUPS == 0
    nc = len(parts)
    cs = _crossed(parts)

    def body(du_ref, dun_ref, dp_ref, dpn_ref, z_ref, zp_ref, x_ref, dx1_ref, cw_ref, g_ref, w_ref, *refs):
        gx_ref, dw_ref, dcw_ref, small_ref = refs[nc + 1:nc + 5]
        eu, ed, ep, ss, u0_ref, dz_ref, acc_ref, dcw_acc = refs[2 * nc + 6:2 * nc + 14]
        sems = refs[2 * nc + 14:]
        cross = _Cross(refs[:nc], refs[nc + 5:2 * nc + 5], sems[0], sems[1])
        gather = _SmallGather(refs[nc], refs[2 * nc + 5], sems[2], sems[3])
        i = pl.program_id(0)
        pl.when(i == 0)(cross.start)
        pl.when(i == 0)(gather.start)

        @pl.when(i == 0)
        def _():
            acc_ref[...] = jnp.zeros_like(acc_ref)
            dcw_acc[...] = jnp.zeros_like(dcw_acc)
            small_ref[...] = jnp.zeros_like(small_ref)

        keep_prev = (i > 0).astype(f32)
        keep_next = (i < n - 1).astype(f32)
        zp = zp_ref[...] * keep_prev
        u0_prev = zp[:, :C_CONV] * _sigmoid(zp[:, C_CONV:2 * C_CONV])
        u0_ref[...] = z_ref[:, :C_CONV] * _sigmoid(z_ref[:, C_CONV:2 * C_CONV])
        du_next = dun_ref[...] * keep_next
        for c, w_pool in enumerate(POOL_WINDOWS):
            lanes = slice(c * 128, (c + 1) * 128)
            eu[c, pl.ds(0, SHIFT_PAD, stride=8), :] = u0_prev[:, lanes]
            ed[c, pl.ds(nv * 8 + 7, SHIFT_PAD, stride=8), :] = du_next[:, lanes]
            ep[c, pl.ds(nv * 8 + 7, SHIFT_PAD, stride=8), :] = (
                dpn_ref[:, lanes] * keep_next / _pool_counts(HALO, w_pool, (i + 1) * tm))
            for j in range(8):
                rows = slice(j * nv, (j + 1) * nv)
                eu[c, pl.ds(SHIFT_PAD * 8 + j, nv, stride=8), :] = u0_ref[rows, lanes]
                ed[c, pl.ds(j, nv, stride=8), :] = du_ref[rows, lanes]
                ep[c, pl.ds(j, nv, stride=8), :] = dp_ref[rows, lanes] / _pool_counts(nv, w_pool, i * tm + j * nv)
                if j >= 1:
                    eu[c, pl.ds(j, SHIFT_PAD, stride=8), :] = u0_ref[j * nv - SHIFT_PAD:j * nv, lanes]
                if j <= 6:
                    edge = slice((j + 1) * nv, (j + 1) * nv + SHIFT_PAD)
                    ed[c, pl.ds(nv * 8 + j, SHIFT_PAD, stride=8), :] = du_ref[edge, lanes]
                    ep[c, pl.ds(nv * 8 + j, SHIFT_PAD, stride=8), :] = (
                        dp_ref[edge, lanes] / _pool_counts(SHIFT_PAD, w_pool, i * tm + (j + 1) * nv))
        for c, w_pool in enumerate(POOL_WINDOWS):
            lanes = slice(c * 128, (c + 1) * 128)
            b_lanes = slice(C_CONV + c * 128, C_CONV + (c + 1) * 128)
            v_lanes = slice(2 * C_CONV + c * 128, 2 * C_CONV + (c + 1) * 128)
            for v0 in range(0, nv, SHIFT_GROUPS):
                span = SHIFT_GROUPS * 8
                acc = jnp.zeros((span, 128), f32)
                for k in range(CONV_K):
                    acc = acc + cw_ref[pl.ds(k, 1), lanes] * ed[c, pl.ds((v0 + CONV_K - 1 - k) * 8, span), :]
                ss[0, v0 * 8:v0 * 8 + span, :] = acc
                acc = ep[c, pl.ds(v0 * 8, span), :]
                for d in range(1, w_pool):
                    acc = acc + ep[c, pl.ds((v0 + d) * 8, span), :]
                ss[1, v0 * 8:v0 * 8 + span, :] = acc
                d1 = ed[c, pl.ds(v0 * 8, span), :]
                for k in range(CONV_K):
                    prod = d1 * eu[c, pl.ds((SHIFT_PAD - (CONV_K - 1) + v0 + k) * 8, span), :]
                    fold = prod[0:8]
                    for r in range(8, span, 8):
                        fold = fold + prod[r:r + 8]
                    dcw_acc[k, :, lanes] += fold
            for j in range(8):
                rows = slice(j * nv, (j + 1) * nv)
                du0 = ss[0, pl.ds(j, nv, stride=8), :]
                av, sv = z_ref[rows, lanes], _sigmoid(z_ref[rows, b_lanes])
                dz_ref[rows, lanes] = (du0 * sv).astype(bf16)
                dz_ref[rows, b_lanes] = (du0 * av * sv * (1.0 - sv)).astype(bf16)
                dz_ref[rows, v_lanes] = (ss[1, pl.ds(j, nv, stride=8), :] - dp_ref[rows, lanes]).astype(bf16)
        h, xh, r = _rms_fwd(x_ref[...], g_ref[...])
        dz = dz_ref[...]
        acc_ref[...] += _dot_tn(h.astype(bf16), dz)
        dh = _dot_nt(dz[:, 0:W_IN_CHUNK], w_ref[0])
        for j in range(1, N_CHIPS):
            dh = dh + _dot_nt(dz[:, j * W_IN_CHUNK:(j + 1) * W_IN_CHUNK], w_ref[j])
        dxb, dg = _rms_bwd(dh, xh, r, g_ref[...])
        gx_ref[...] = dx1_ref[...] + dxb
        small_ref[0:1, :] += dg

        @pl.when(i == n - 1)
        def _():
            for j in range(N_CHIPS):
                dw_ref[j] = acc_ref[:, j * W_IN_CHUNK:(j + 1) * W_IN_CHUNK].astype(bf16)
            dcw_ref[...] = jnp.sum(dcw_acc[...], axis=1)

        pl.when(i == n - 1)(gather.finish)
        pl.when(i == n - 1)(cross.finish)

    nxt = lambda i: (jnp.minimum((i + 1) * hb, last), 0)
    res = pl.pallas_call(
        body, name="in_bwd", grid=(n,),
        in_specs=[_row(0, (tm, C_CONV)), pl.BlockSpec((HALO, C_CONV), nxt),
                  _row(0, (tm, C_POOL)), pl.BlockSpec((HALO, C_POOL), nxt),
                  _row(0, (tm, W_IN_COLS)),
                  pl.BlockSpec((HALO, W_IN_COLS), lambda i: (jnp.maximum(i * hb - 1, 0), 0)),
                  _row(0, (tm, D_MODEL)), _row(0, (tm, D_MODEL)), _full((HALO, C_CONV)), _full((1, D_MODEL)),
                  _full((N_CHIPS, D_MODEL, W_IN_CHUNK))] + cs["in_specs"] + [ANY],
        out_specs=[_row(0, (tm, D_MODEL)), _full((N_CHIPS, D_MODEL, W_IN_CHUNK)), _full((HALO, C_CONV)),
                   _full((8, D_MODEL))] + cs["out_specs"] + [ANY],
        out_shape=[jax.ShapeDtypeStruct((s, D_MODEL), f32), jax.ShapeDtypeStruct((N_CHIPS, D_MODEL, W_IN_CHUNK), bf16),
                   jax.ShapeDtypeStruct((HALO, C_CONV), f32), jax.ShapeDtypeStruct((8, D_MODEL), f32)] + cs["out_shape"]
        + [jax.ShapeDtypeStruct((N_DEV,) + small_part.shape, f32)],
        scratch_shapes=[pltpu.VMEM((4, (SHIFT_PAD + nv) * 8, 128), f32), pltpu.VMEM((4, (nv + SHIFT_PAD) * 8, 128), f32),
                        pltpu.VMEM((4, (nv + SHIFT_PAD) * 8, 128), f32), pltpu.VMEM((2, tm, 128), f32),
                        pltpu.VMEM((tm, C_CONV), f32), pltpu.VMEM((tm, W_IN_COLS), bf16),
                        pltpu.VMEM((D_MODEL, W_IN_COLS), f32), pltpu.VMEM((HALO, 8, C_CONV), f32)] + cs["scratch"]
        + _SmallGather.scratch(),
        compiler_params=_cp("arbitrary"),
    )(du1, du1, dpo, dpo, z, z, x, dx1, conv_w, g_mix, w_in_g, *parts, small_part)
    return res[:4], res[4:4 + nc], res[4 + nc]


def _pair_reduce(name, grads):
    nk = len(grads)
    halves = [g.shape[1] // 2 for g in grads]

    def body(*refs):
        ins, outs, got = refs[:nk], refs[nk:2 * nk], refs[2 * nk:3 * nk]
        send_sems, recv_sems = refs[3 * nk:]
        x, y, c, _, _ = _place()

        def half(k, core):
            return pl.ds(pl.multiple_of(core * halves[k], 16), halves[k])

        cps = [_remote(ins[k].at[:, half(k, 1 - c), :], got[k], send_sems.at[k], recv_sems.at[k], (x, y, 1 - c))
               for k in range(nk)]
        for cp in cps:
            cp.start()
        for k, cp in enumerate(cps):
            cp.wait_recv()
            outs[k][...] = (ins[k][:, half(k, c), :].astype(f32) + got[k][...].astype(f32)).astype(bf16)
        for cp in cps:
            cp.wait_send()

    shapes = [(N_CHIPS, h, g.shape[2]) for g, h in zip(grads, halves)]
    return pl.pallas_call(
        body, name=f"pair_reduce_{name}", in_specs=[VMEM] * nk, out_specs=[VMEM] * nk,
        out_shape=[jax.ShapeDtypeStruct(s, bf16) for s in shapes],
        scratch_shapes=[pltpu.VMEM(s, bf16) for s in shapes]
        + [pltpu.SemaphoreType.DMA((nk,)), pltpu.SemaphoreType.DMA((nk,))],
        compiler_params=pltpu.CompilerParams(vmem_limit_bytes=VMEM_LIMIT),
    )(*grads)


class _Cross:
    def __init__(self, parts, landed, send_sems, recv_sems):
        self.parts, self.landed, self.send_sems, self.recv_sems = parts, landed, send_sems, recv_sems
        _, _, self.c, self.me, self.others = _place()

    def _copy(self, k, j, src_chunk, dst_slot):
        ox, oy = self.others[j]
        return _remote(self.parts[k].at[src_chunk], self.landed[k].at[dst_slot], self.send_sems.at[3 * k + j],
                       self.recv_sems.at[3 * k + j], (ox, oy, self.c))

    def _each(self):
        return [(k, j, 2 * self.others[j][0] + self.others[j][1]) for k in range(len(self.parts)) for j in range(3)]

    def start(self):
        for k, j, chip in self._each():
            self._copy(k, j, chip, self.me).start()

    def finish(self):
        for k, j, chip in self._each():
            self._copy(k, j, chip, chip).wait_recv()
        for k, j, chip in self._each():
            self._copy(k, j, chip, self.me).wait_send()

    @staticmethod
    def scratch(n):
        return [pltpu.SemaphoreType.DMA((3 * n,)), pltpu.SemaphoreType.DMA((3 * n,))]


def _rs_sum_chips(name, place, landed, part):
    _, half, cols = landed.shape

    def body(place_ref, l_ref, p_ref, o_ref):
        me = place_ref[0]
        own = p_ref[0].astype(f32)
        acc = jnp.where(me == 0, own, l_ref[0].astype(f32))
        for j in range(1, N_CHIPS):
            acc = acc + jnp.where(me == j, own, l_ref[j].astype(f32))
        o_ref[...] = acc

    return pl.pallas_call(
        body, name=f"rs_sum_chips_{name}",
        grid_spec=pltpu.PrefetchScalarGridSpec(
            num_scalar_prefetch=1, grid=(1,),
            in_specs=[pl.BlockSpec((N_CHIPS, half, cols), lambda t, place_ref: (0, 0, 0)),
                      pl.BlockSpec((1, half, cols), lambda t, place_ref: (place_ref[0], 0, 0))],
            out_specs=pl.BlockSpec((half, cols), lambda t, place_ref: (place_ref[1], 0))),
        out_shape=jax.ShapeDtypeStruct((2 * half, cols), f32),
        compiler_params=_cp("arbitrary"),
    )(place, landed, part)


def _grad_pair(shards):
    nk = len(shards)

    def body(*refs):
        outs = refs[nk:2 * nk]
        send_sems, recv_sems = refs[2 * nk:]
        x, y, c, _, _ = _place()

        def half(k, core):
            h = outs[k].shape[0] // 2
            return outs[k].at[pl.ds(core * h, h), :]

        cps = [_remote(half(k, c), half(k, c), send_sems.at[k], recv_sems.at[k], (x, y, 1 - c)) for k in range(nk)]
        for cp in cps:
            cp.start()
        for k in range(nk):
            _remote(half(k, 1 - c), half(k, 1 - c), send_sems.at[k], recv_sems.at[k], (x, y, 1 - c)).wait_recv()
        for cp in cps:
            cp.wait_send()

    return pl.pallas_call(
        body, name="grad_pair", in_specs=[ANY] * nk, out_specs=[ANY] * nk,
        out_shape=[jax.ShapeDtypeStruct(a.shape, f32) for a in shards],
        input_output_aliases={k: k for k in range(nk)},
        scratch_shapes=[pltpu.SemaphoreType.DMA((nk,)), pltpu.SemaphoreType.DMA((nk,))],
    )(*shards)


def _adam_math(w, g, m, v):
    m = ADAM_B1 * m + (1.0 - ADAM_B1) * g
    v = ADAM_B2 * v + (1.0 - ADAM_B2) * (g * g)
    m_hat = m / (1.0 - ADAM_B1 ** ADAM_STEP)
    v_hat = v / (1.0 - ADAM_B2 ** ADAM_STEP)
    delta = -ADAM_LR * (m_hat / (jnp.sqrt(v_hat) + ADAM_EPS) + ADAM_WD * w)
    return delta, m, v


def _adam(k, w, g, m, v):
    rows, cols = w.shape
    tr = next(rows // d for d in (1, 2, 4, 8) if rows % (8 * d) == 0 and rows // d <= 256)

    def body(w_ref, g_ref, m_ref, v_ref, d_ref, nm_ref, nv_ref):
        d_ref[...], nm_ref[...], nv_ref[...] = _adam_math(w_ref[...], g_ref[...], m_ref[...], v_ref[...])

    spec = _row(0, (tr, cols))
    return pl.pallas_call(
        body, name=f"adam_{k}", grid=(rows // tr,), in_specs=[spec] * 4, out_specs=[spec] * 3,
        out_shape=[jax.ShapeDtypeStruct(w.shape, f32)] * 3,
        compiler_params=_cp("parallel"),
    )(w, g, m, v)


class _SmallGather:
    def __init__(self, part, buf, send_sems, recv_sems):
        self.part, self.buf, self.send_sems, self.recv_sems = part, buf, send_sems, recv_sems
        self.x, self.y, self.c, _, self.others = _place()
        self.sibling = (self.x, self.y, 1 - self.c)

    def _copy(self, k, block, to, src=None):
        slot = self.buf.at[4 * block[0] + 2 * block[1] + block[2]]
        return _remote(slot if src is None else src, slot, self.send_sems.at[k], self.recv_sems.at[k], to)

    def _first(self):
        me = (self.x, self.y, self.c)
        return [self._copy(0, me, self.sibling, src=self.part)] + [
            self._copy(1 + j, me, (*chip, self.c), src=self.part) for j, chip in enumerate(self.others)]

    def _passed(self):
        return [self._copy(4 + j, (*chip, self.c), self.sibling) for j, chip in enumerate(self.others)]

    @staticmethod
    def scratch():
        return [pltpu.SemaphoreType.DMA((7,)), pltpu.SemaphoreType.DMA((7,))]

    def start(self):
        for cp in self._first():
            cp.start()

    def finish(self):
        passed = self._passed()
        for j, chip in enumerate(self.others):
            self._copy(1 + j, (*chip, self.c), self.sibling).wait_recv()
            passed[j].start()
        self._copy(0, self.sibling, self.sibling).wait_recv()
        for j, chip in enumerate(self.others):
            self._copy(4 + j, (*chip, 1 - self.c), self.sibling).wait_recv()
        for cp in self._first() + passed:
            cp.wait_send()


def _rs_sum_group(name, place, landed, parts, cross_parts, small_part):
    nk, nx = len(landed), len(cross_parts)
    dims = [a.shape[1:] for a in landed]
    rows = small_part.shape[0]

    def body(place_ref, *refs):
        l_refs, p_refs = refs[:nk], refs[nk:2 * nk]
        x_refs, sp_ref = refs[2 * nk:2 * nk + nx], refs[2 * nk + nx]
        outs = refs[2 * nk + nx + 1:]
        o_refs, xl_refs, sbuf = outs[:nk], outs[nk:nk + nx], outs[nk + nx]
        sems = outs[nk + nx + 1:]
        cross = _Cross(x_refs, xl_refs, sems[0], sems[1])
        small = _SmallGather(sp_ref, sbuf, sems[2], sems[3])
        t = pl.program_id(0)

        @pl.when(t == 0)
        def _():
            cross.start()
            small.start()

        me = place_ref[0]
        for l_ref, p_ref, o_ref in zip(l_refs, p_refs, o_refs):
            own = p_ref[0].astype(f32)
            acc = jnp.where(me == 0, own, l_ref[0].astype(f32))
            for j in range(1, N_CHIPS):
                acc = acc + jnp.where(me == j, own, l_ref[j].astype(f32))
            o_ref[...] = acc

        @pl.when(t == 1)
        def _():
            small.finish()
            cross.finish()

    def halves(h, c, lead, index):
        return pl.BlockSpec((lead, h // 2, c) if lead else (h // 2, c), index)

    in_specs = [halves(h, c, N_CHIPS, lambda t, pr: (0, t, 0)) for h, c in dims]
    in_specs += [halves(h, c, 1, lambda t, pr: (pr[0], t, 0)) for h, c in dims]
    in_specs += [ANY] * (nx + 1)
    out_specs = [halves(h, c, 0, lambda t, pr: (2 * pr[1] + t, 0)) for h, c in dims] + [ANY] * (nx + 1)
    out_shape = [jax.ShapeDtypeStruct((2 * h, c), f32) for h, c in dims]
    out_shape += [jax.ShapeDtypeStruct(a.shape, a.dtype) for a in cross_parts]
    out_shape.append(jax.ShapeDtypeStruct((N_DEV, rows, 128), f32))
    res = pl.pallas_call(
        body, name=f"rs_sum_{name}",
        grid_spec=pltpu.PrefetchScalarGridSpec(
            num_scalar_prefetch=1, grid=(2,), in_specs=in_specs, out_specs=out_specs,
            scratch_shapes=_Cross.scratch(nx) + _SmallGather.scratch()),
        out_shape=out_shape, compiler_params=_cp("arbitrary"),
    )(place, *landed, *parts, *cross_parts, small_part)
    return res[:nk], res[nk:nk + nx], res[nk + nx]


def _small_adam(dev, gathered, parts, w, m, v):
    rows = [p.shape[0] for p in parts]

    def body(dev_ref, b0_ref, b1_ref, p0_ref, p1_ref, w_ref, m_ref, v_ref, g_ref, d_ref, nm_ref, nv_ref):
        me = dev_ref[0]
        lo = 0
        for b_ref, p_ref, n_rows in ((b0_ref, p0_ref, rows[0]), (b1_ref, p1_ref, rows[1])):
            own = p_ref[...]
            g = jnp.where(me == 0, own, b_ref[0])
            for d in range(1, N_DEV):
                g = g + jnp.where(me == d, own, b_ref[d])
            sl = slice(lo, lo + n_rows)
            g_ref[sl, :] = g
            d_ref[sl, :], nm_ref[sl, :], nv_ref[sl, :] = _adam_math(w_ref[sl, :], g, m_ref[sl, :], v_ref[sl, :])
            lo += n_rows

    shape = jax.ShapeDtypeStruct(w.shape, f32)
    flat = pl.BlockSpec(w.shape, lambda t, dev_ref: (0, 0))
    whole = lambda a: pl.BlockSpec(a.shape, lambda t, dev_ref: (0,) * a.ndim)
    return pl.pallas_call(
        body, name="small_adam",
        grid_spec=pltpu.PrefetchScalarGridSpec(
            num_scalar_prefetch=1, grid=(1,),
            in_specs=[whole(gathered[0]), whole(gathered[1]), whole(parts[0]), whole(parts[1])] + [flat] * 3,
            out_specs=[flat] * 4),
        out_shape=[shape] * 4, compiler_params=_cp("arbitrary"),
    )(dev, *gathered, *parts, w, m, v)


def _rows128(a):
    return a.reshape(-1, 128)


def _pad_rows(a, rows):
    return jnp.concatenate([a, jnp.zeros((rows - a.shape[0],) + a.shape[1:], a.dtype)], axis=0)


def _pack_small(me, g_pg, g_post, g_final, g_ffn, ln_g, ln_b, conv_b, pool_scale, pool_w, conv_w, g_mix):
    blk_ple = _pad_rows(jnp.concatenate([g_pg, g_post, g_final.reshape(1, D_MODEL)], axis=0), 8)
    blk_ffn = _pad_rows(g_ffn, 8)
    blk_mix = _pad_rows(jnp.concatenate([ln_g, ln_b, conv_b, pool_scale], axis=0), 8)
    cw = lax.dynamic_update_slice(jnp.zeros((N_CHIPS, HALO, 128), f32), _pad_rows(conv_w, HALO)[None], (me, 0, 0))
    blk_in = _pad_rows(g_mix, 8)
    return jnp.concatenate([_rows128(blk_ple), _rows128(blk_ffn), _rows128(blk_mix), _rows128(pool_w),
                            _rows128(cw), _rows128(blk_in)], axis=0)


def _unpack_small(me, packed):
    o = 0

    def take(rows):
        nonlocal o
        blk = packed[o:o + rows]
        o += rows
        return blk

    ple = take(64).reshape(8, D_MODEL)
    ffn = take(64).reshape(8, D_MODEL)
    mixb = take(32).reshape(8, C_CONV)
    pool_w = take(512).reshape(1, 4, POOL_GROUP, POOL_GROUP)
    cw = take(N_CHIPS * HALO).reshape(N_CHIPS, HALO, 128)
    inb = take(64).reshape(8, D_MODEL)
    conv_w = lax.dynamic_slice(cw, (me, 0, 0), (1, HALO, 128))[:, :CONV_K, :]
    return dict(g_ple_gate=ple[0:1], g_ple_post=ple[1:2], g_final=ple[2], loss=ple[3, 0], g_ffn=ffn[0:1],
                ln_g=mixb[0:1], ln_b=mixb[1:2], conv_b=mixb[2:3], pool_scale=mixb[3:4], pool_w=pool_w,
                conv_w=conv_w, g_mix=inb[0:1])


def kernel(x, p, g_mix, w_in, conv_w, conv_b, ln_g, ln_b, pool_w, pool_scale, w_out, g_ffn, w_gate_up, w_down, g_ple_gate, w_ple_gate, w_ple_up, g_ple_post, g_final, loss_target, m_g_mix, m_w_in, m_conv_w, m_conv_b, m_ln_g, m_ln_b, m_pool_w, m_pool_scale, m_w_out, m_g_ffn, m_w_gate_up, m_w_down, m_g_ple_gate, m_w_ple_gate, m_w_ple_up, m_g_ple_post, m_g_final, v_g_mix, v_w_in, v_conv_w, v_conv_b, v_ln_g, v_ln_b, v_pool_w, v_pool_scale, v_w_out, v_g_ffn, v_w_gate_up, v_w_down, v_g_ple_gate, v_w_ple_gate, v_w_ple_up, v_g_ple_post, v_g_final):
    seq = x.shape[1]
    me = 2 * lax.axis_index("x") + lax.axis_index("y")
    chip = me.astype(jnp.int32).reshape(1)
    core = lax.axis_index("c").astype(jnp.int32).reshape(1)
    place = jnp.concatenate([chip, core])
    xs, ps, ts = x.reshape(seq, D_MODEL), p.reshape(seq, D_PLE), loss_target.reshape(seq, D_MODEL)

    big_names = ["w_in", "w_gu", "w_out", "w_down", "w_pg", "w_pu"]
    big = [w_in[0], w_gate_up[0], w_out[0], w_down[0], w_ple_gate[0], w_ple_up[0]]
    big_m = [m_w_in[0], m_w_gate_up[0], m_w_out[0], m_w_down[0], m_w_ple_gate[0], m_w_ple_up[0]]
    big_v = [v_w_in[0], v_w_gate_up[0], v_w_out[0], v_w_down[0], v_w_ple_gate[0], v_w_ple_up[0]]
    b_in, b_gu, b_out, b_down, b_pg, b_pu = [_cast_into_slot(nm, chip, w, bf16) for nm, w in zip(big_names, big)]
    b_cw = _cast_into_slot("conv_w", chip, _pad_rows(conv_w[0], HALO), f32)
    xi, yi = lax.axis_index("x"), lax.axis_index("y")
    order = jnp.stack([me, 2 * (1 - xi) + yi, 2 * xi + 1 - yi, 2 * (1 - xi) + 1 - yi]).astype(jnp.int32)

    z, (w_in_g, cw_g, w_out_g) = _mix_in(xs, g_mix, order, [b_in, b_cw, b_out])
    conv_w_f = cw_g.transpose(1, 0, 2).reshape(HALO, C_CONV)
    w_out_f = w_out_g.reshape(D_MODEL, D_MODEL)
    (x1, mix, u1, pooled), (w_gu_g, w_down_g) = _conv_pool_out(z, xs, conv_w_f, conv_b, ln_g, ln_b, pool_w[0],
                                                                pool_scale, w_out_f, [b_gu, b_down])
    w_down_f = w_down_g.reshape(D_FF, D_MODEL)
    (x2, h2, gu, ffn_f), (w_pg_g, w_pu_g) = _ffn_fwd(x1, g_ffn, w_gu_g, w_down_f, [b_pg, b_pu])
    w_pg_f = w_pg_g.reshape(D_MODEL, D_MODEL)
    dx2, d_w_pg, d_w_pu, small_ple = _ple_loss(x2, ps, ts, g_ple_gate, g_ple_post, g_final.reshape(1, D_MODEL),
                                               w_pg_f, w_pu_g)
    d_w_down = _ffn_bwd_dw_down(ffn_f, dx2)
    parts_a = _pair_reduce("a", [d_w_pg.reshape(N_CHIPS, -1, D_MODEL), d_w_pu,
                                 d_w_down.reshape(N_CHIPS, -1, D_MODEL)])
    (dx1, dgu, small_ffn), landed_a = _ffn_bwd_dx(dx2, x1, gu, g_ffn, w_gu_g, w_down_f, parts_a)
    d_w_gu = _ffn_bwd_dw_gu(h2, dgu)
    du1, dpo, d_w_out, d_pool_w, small_mix = _mix_bwd_local(dx1, mix, u1, pooled, w_out_f, ln_g, ln_b, pool_w[0],
                                                             pool_scale)
    parts_b = _pair_reduce("b", [d_w_gu, d_w_out.reshape(N_CHIPS, -1, D_MODEL)])
    part_0 = jnp.concatenate([_rows128(small_ple), _rows128(small_ffn), _rows128(small_mix), _rows128(d_pool_w)],
                             axis=0)
    (grad_x, d_w_in, d_conv_w, small_in), landed_b, small_all_0 = _in_bwd(du1, dpo, z, xs, dx1, conv_w_f, g_mix,
                                                                            w_in_g, parts_b, part_0)
    parts_c = _pair_reduce("c", [d_w_in])
    cw_chunks = d_conv_w.reshape(HALO, N_CHIPS, 128).transpose(1, 0, 2)
    part_1 = jnp.concatenate([_rows128(cw_chunks), _rows128(small_in)], axis=0)
    (h_gu, h_out, h_down, h_pg, h_pu), landed_c, small_all_1 = _rs_sum_group(
        "ab", place, [landed_b[0], landed_b[1], landed_a[2], landed_a[0], landed_a[1]],
        [parts_b[0], parts_b[1], parts_a[2], parts_a[0], parts_a[1]], parts_c, part_1)
    h_in = _rs_sum_chips("w_in", place, landed_c[0], parts_c[0])
    big_g = _grad_pair([h_in, h_gu, h_out, h_down, h_pg, h_pu])
    big_upd = [_adam(nm, w, g, m, v) for nm, w, g, m, v in zip(big_names, big, big_g, big_m, big_v)]

    sw = _pack_small(me, g_ple_gate, g_ple_post, g_final, g_ffn, ln_g, ln_b, conv_b, pool_scale, pool_w, conv_w[0], g_mix)
    sm = _pack_small(me, m_g_ple_gate, m_g_ple_post, m_g_final, m_g_ffn, m_ln_g, m_ln_b, m_conv_b, m_pool_scale,
                     m_pool_w, m_conv_w[0], m_g_mix)
    sv = _pack_small(me, v_g_ple_gate, v_g_ple_post, v_g_final, v_g_ffn, v_ln_g, v_ln_b, v_conv_b, v_pool_scale,
                     v_pool_w, v_conv_w[0], v_g_mix)
    dev = (2 * chip + core).astype(jnp.int32)
    small = [_unpack_small(me, a) for a in _small_adam(dev, [small_all_0, small_all_1], [part_0, part_1], sw, sm, sv)]

    names = ["g_mix", "w_in", "conv_w", "conv_b", "ln_g", "ln_b", "pool_w", "pool_scale", "w_out", "g_ffn",
             "w_gate_up", "w_down", "g_ple_gate", "w_ple_gate", "w_ple_up", "g_ple_post", "g_final"]
    big_at = {"w_in": 0, "w_gate_up": 1, "w_out": 2, "w_down": 3, "w_ple_gate": 4, "w_ple_up": 5}
    out = [small[0]["loss"], grad_x.reshape(1, seq, D_MODEL)]
    for kind in range(4):
        for nm in names:
            if nm in big_at:
                k = big_at[nm]
                out.append((big_g[k] if kind == 0 else big_upd[k][kind - 1])[None])
            else:
                out.append(small[kind][nm])
    return tuple(out)
```

```python
import functools

import jax
import jax.numpy as jnp
from jax import lax
from jax.experimental import pallas as pl
from jax.experimental.pallas import tpu as pltpu

f32, bf16 = jnp.float32, jnp.bfloat16

EPS = 1e-6
D_MODEL = 1024
C_CONV = 512
C_POOL = 512
POOL_WINDOWS = (2, 4, 8, 16)
POOL_GROUP = 128
CONV_K = 31
D_FF = 2816
D_PLE = 256
N_CHIPS = 4
N_DEV = 8
W_IN_COLS = 2 * C_CONV + C_POOL
W_IN_CHUNK = W_IN_COLS // N_CHIPS
FF_CHUNK = 2 * D_FF // N_CHIPS
PLE_CHUNK = D_MODEL // N_CHIPS
HALO = 32
ROW_TILE = 512
CONV_ROWS = 64
CONV_COLS = (slice(0, 256), slice(256, 512))
SHIFT_PAD = 32
SHIFT_GROUPS = 16
FF_SUB = (0, 512, 1024, FF_CHUNK)
VMEM_LIMIT = 56 * 1024 * 1024

ADAM_LR = 0.001
ADAM_B1 = 0.9
ADAM_B2 = 0.999
ADAM_EPS = 1e-08
ADAM_WD = 0.01
ADAM_STEP = 10

MESH = pl.DeviceIdType.MESH
ANY = pl.BlockSpec(memory_space=pl.ANY)
VMEM = pl.BlockSpec(memory_space=pltpu.VMEM)


def _cp(*sem):
    return pltpu.CompilerParams(dimension_semantics=sem, vmem_limit_bytes=VMEM_LIMIT)


def _dot(a, b):
    return jnp.dot(a, b, preferred_element_type=f32)


def _dot_nt(a, b):
    return lax.dot_general(a, b, (((1,), (1,)), ((), ())), preferred_element_type=f32)


def _dot_tn(a, b):
    return lax.dot_general(a, b, (((0,), (0,)), ((), ())), preferred_element_type=f32)


def _sigmoid(v):
    return jax.nn.sigmoid(v)


def _rms_fwd(v, g):
    r = lax.rsqrt(jnp.mean(v * v, axis=-1, keepdims=True) + EPS)
    vh = v * r
    return vh * g, vh, r


def _rms_bwd(dy, vh, r, g):
    dvh = dy * g
    dv = r * (dvh - vh * jnp.mean(dvh * vh, axis=-1, keepdims=True))
    return dv, jnp.sum(dy * vh, axis=0, keepdims=True)


def _silu_grad(v, s):
    return s * (1.0 + v * (1.0 - s))


def _row(i, n):
    return pl.BlockSpec((n[0], n[1]), lambda *a: (a[i], 0))


def _full(shape):
    nd = len(shape)
    return pl.BlockSpec(shape, lambda *a: (0,) * nd)


def _place():
    x, y, c = lax.axis_index("x"), lax.axis_index("y"), lax.axis_index("c")
    others = [(1 - x, y), (x, 1 - y), (1 - x, 1 - y)]
    return x, y, c, 2 * x + y, others


def _remote(src, dst, send_sem, recv_sem, dev):
    return pltpu.make_async_remote_copy(src_ref=src, dst_ref=dst, send_sem=send_sem, recv_sem=recv_sem,
                                        device_id=dev, device_id_type=MESH)


SHARD_STEPS = 4


def _cast_into_slots(me, ws, dtypes):
    n = len(ws)

    def body(me_ref, *refs):
        for w_ref, o_ref, dtype in zip(refs[:n], refs[n:], dtypes):
            o_ref[0] = w_ref[...].astype(dtype)

    return pl.pallas_call(
        body, name="cast_shards",
        grid_spec=pltpu.PrefetchScalarGridSpec(
            num_scalar_prefetch=1, grid=(SHARD_STEPS,),
            in_specs=[pl.BlockSpec((w.shape[0] // SHARD_STEPS, w.shape[1]), lambda r, me_ref: (r, 0)) for w in ws],
            out_specs=[pl.BlockSpec((1, w.shape[0] // SHARD_STEPS, w.shape[1]), lambda r, me_ref: (me_ref[0], r, 0))
                       for w in ws]),
        out_shape=[jax.ShapeDtypeStruct((N_CHIPS,) + w.shape, dt) for w, dt in zip(ws, dtypes)],
        compiler_params=_cp("parallel"),
    )(me, *ws)


class _Gather:
    def __init__(self, bufs, send_sems, recv_sems):
        self.bufs, self.send_sems, self.recv_sems = bufs, send_sems, recv_sems
        self.x, self.y, self.c, self.me, self.others = _place()
        self.halves = [b.shape[1] // 2 for b in bufs]

    def _piece(self, k, chip, half):
        return self.bufs[k].at[chip, pl.ds(half * self.halves[k], self.halves[k]), :]

    def _ici(self, k, j, chip):
        ox, oy = self.others[j]
        piece = self._piece(k, chip, self.c)
        return _remote(piece, piece, self.send_sems.at[6 * k + j], self.recv_sems.at[6 * k + j], (ox, oy, self.c))

    def _relay(self, k):
        first = self.c == 0
        piece = self._piece(k, jnp.where(first, self.chip(0), self.chip(1)), self.c)
        to = (jnp.where(first, self.others[1][0], self.others[0][0]),
              jnp.where(first, self.others[1][1], self.others[0][1]), self.c)
        return _remote(piece, piece, self.send_sems.at[6 * k + 2], self.recv_sems.at[6 * k + 2], to)

    def _pair(self, k, j, half):
        ox, oy = self.others[j]
        piece = self._piece(k, 2 * ox + oy, half)
        return _remote(piece, piece, self.send_sems.at[6 * k + 3 + j], self.recv_sems.at[6 * k + 3 + j],
                       (self.x, self.y, 1 - self.c))

    def _each(self, ks=None):
        return [(k, j) for k in (range(len(self.bufs)) if ks is None else ks) for j in range(3)]

    def chip(self, j):
        ox, oy = self.others[j]
        return 2 * ox + oy

    def start(self):
        for k in range(len(self.bufs)):
            for j in range(2):
                self._ici(k, j, self.me).start()

    def forward(self, pairs=None):
        for k, j in self._each() if pairs is None else pairs:
            self._ici(k, j, self.chip(j)).wait_recv()
            self._pair(k, j, self.c).start()
            if j < 2:
                pl.when(self.c == j)(self._relay(k).start)

    def landed(self, pairs):
        for k, j in pairs:
            self._pair(k, j, 1 - self.c).wait_recv()

    def finish(self, ks=None):
        self.landed(self._each(ks))
        for k in range(len(self.bufs)):
            for j in range(2):
                self._ici(k, j, self.me).wait_send()
            self._relay(k).wait_send()
            for j in range(3):
                self._pair(k, j, self.c).wait_send()

    @staticmethod
    def scratch(n):
        return [pltpu.SemaphoreType.DMA((6 * n,)), pltpu.SemaphoreType.DMA((6 * n,))]


def _carried(bufs):
    n = len(bufs)
    return dict(in_specs=[ANY] * n, out_specs=[ANY] * n,
                out_shape=[jax.ShapeDtypeStruct(b.shape, b.dtype) for b in bufs], scratch=_Gather.scratch(n))


def _mix_in(x, g_mix, order, carry):
    s = x.shape[0]
    tm = min(2 * ROW_TILE, s)
    n = s // tm
    nc = len(carry)
    cs = _carried(carry)

    def body(order_ref, x_ref, g_ref, *refs):
        z_ref = refs[nc]
        bufs = refs[nc + 1:2 * nc + 1]
        h_ref, w_ref, w_sem = refs[2 * nc + 1:2 * nc + 4]
        gather = _Gather(bufs, *refs[2 * nc + 4:])
        q, i = pl.program_id(0), pl.program_id(1)
        first = i == 0
        pl.when(jnp.logical_and(q == 0, first))(gather.start)
        for j in range(3):

            @pl.when(jnp.logical_and(q == j + 1, first))
            def _():
                gather.forward([(0, j)])
                gather.landed([(0, j)])
                if j == 1:
                    gather.forward([(k, jj) for k in range(1, nc) for jj in range(2)])

        @pl.when(first)
        def _():
            load = pltpu.make_async_copy(bufs[0].at[order_ref[q]], w_ref, w_sem)
            load.start()
            load.wait()

        @pl.when(q == 0)
        def _():
            h, _, _ = _rms_fwd(x_ref[...], g_ref[...])
            h_ref[i] = h.astype(bf16)

        z_ref[...] = _dot(h_ref[i], w_ref[...])

        @pl.when(jnp.logical_and(q == N_CHIPS - 1, i == n - 1))
        def _():
            gather.forward([(k, 2) for k in range(1, nc)])
            gather.finish(range(1, nc))

    res = pl.pallas_call(
        body, name="mix_in",
        grid_spec=pltpu.PrefetchScalarGridSpec(
            num_scalar_prefetch=1, grid=(N_CHIPS, n),
            in_specs=[pl.BlockSpec((tm, D_MODEL), lambda q, i, order_ref: (jnp.where(q == 0, i, 0), 0)),
                      pl.BlockSpec((1, D_MODEL), lambda q, i, order_ref: (0, 0))] + cs["in_specs"],
            out_specs=[pl.BlockSpec((tm, W_IN_CHUNK), lambda q, i, order_ref: (i, order_ref[q]))] + cs["out_specs"],
            scratch_shapes=[pltpu.VMEM((n, tm, D_MODEL), bf16), pltpu.VMEM((D_MODEL, W_IN_CHUNK), bf16),
                            pltpu.SemaphoreType.DMA(())] + cs["scratch"]),
        out_shape=[jax.ShapeDtypeStruct((s, W_IN_COLS), f32)] + cs["out_shape"],
        input_output_aliases={3 + k: 1 + k for k in range(nc)},
        compiler_params=_cp("arbitrary", "arbitrary"),
    )(order, x, g_mix, *carry)
    return res[0], res[1:]


def _tap_offsets(lo, hi):
    groups = [[o for o in range(lo, hi + 1) if o % 8 == s] for s in range(8)]
    return [g for g in groups if g]


def _tap_sum(buf, w_ref, row0, cols, tap_of):
    acc = jnp.zeros((CONV_ROWS, cols.stop - cols.start), f32)
    for offs in _tap_offsets(0, CONV_K - 1):
        slab = buf[pl.ds(row0 + offs[0], offs[-1] - offs[0] + CONV_ROWS), cols]
        for o in offs:
            acc = acc + w_ref[pl.ds(tap_of(o), 1), cols] * slab[o - offs[0]:o - offs[0] + CONV_ROWS]
    return acc


def _pool_counts(tm, w, first_row):
    t1 = (lax.broadcasted_iota(jnp.int32, (tm, 1), 0) + first_row + 1).astype(f32)
    return jnp.minimum(t1, float(w))


def _conv_pool_out(z, x, conv_w, conv_b, ln_g, ln_b, pool_w, pool_scale, w_out, carry):
    s = x.shape[0]
    tm = min(ROW_TILE, s)
    n = s // tm
    hb = tm // HALO
    nc = len(carry)
    cs = _carried(carry)

    def body(z_ref, zp_ref, x_ref, cw_ref, cb_ref, lg_ref, lb_ref, pw_ref, ps_ref, wo_ref, *refs):
        x1_ref, mix_ref, u1_ref, pooled_ref = refs[nc:nc + 4]
        ubuf, vbuf = refs[2 * nc + 4:2 * nc + 6]
        gather = _Gather(refs[nc + 4:2 * nc + 4], *refs[2 * nc + 6:])
        i = pl.program_id(0)
        pl.when(i == 0)(gather.start)
        for k in range(nc):
            pl.when(i == min(n // 2 + 2 * k, n - 1))(functools.partial(gather.forward, [(k, 0), (k, 1)]))
        keep = (i > 0).astype(f32)
        zp = zp_ref[...] * keep
        ubuf[0:HALO, :] = zp[:, :C_CONV] * _sigmoid(zp[:, C_CONV:2 * C_CONV])
        vbuf[0:HALO, :] = zp[:, 2 * C_CONV:]
        ubuf[HALO:, :] = z_ref[:, :C_CONV] * _sigmoid(z_ref[:, C_CONV:2 * C_CONV])
        vbuf[HALO:, :] = z_ref[:, 2 * C_CONV:]
        off = HALO - (CONV_K - 1)
        for r0 in range(0, tm, CONV_ROWS):
            for cols in CONV_COLS:
                u1_ref[r0:r0 + CONV_ROWS, cols] = cb_ref[:, cols] + _tap_sum(ubuf, cw_ref, r0 + off, cols, lambda o: o)
        u1 = u1_ref[...]
        mu = jnp.mean(u1, axis=-1, keepdims=True)
        uc = u1 - mu
        rstd = lax.rsqrt(jnp.mean(uc * uc, axis=-1, keepdims=True) + EPS)
        u2 = uc * rstd * lg_ref[...] + lb_ref[...]
        mix_ref[:, :C_CONV] = (u2 * _sigmoid(u2)).astype(bf16)
        for g, w in enumerate(POOL_WINDOWS):
            cols = slice(g * POOL_GROUP, (g + 1) * POOL_GROUP)
            acc = vbuf[pl.ds(HALO, tm), cols]
            vg = acc
            for d in range(1, w):
                acc = acc + vbuf[pl.ds(HALO - d, tm), cols]
            pooled = (acc / _pool_counts(tm, w, i * tm) - vg).astype(bf16)
            pooled_ref[:, cols] = pooled
            mixed = _dot(pooled, pw_ref[g].astype(bf16))
            mix_ref[:, C_CONV + g * POOL_GROUP:C_CONV + (g + 1) * POOL_GROUP] = (mixed * ps_ref[:, cols]).astype(bf16)
        x1_ref[...] = x_ref[...] + _dot(mix_ref[...], wo_ref[...])
        @pl.when(i == n - 1)
        def _():
            gather.forward([(k, 2) for k in range(nc)])
            gather.finish()

    res = pl.pallas_call(
        body, name="conv_pool_out", grid=(n,),
        in_specs=[_row(0, (tm, W_IN_COLS)),
                  pl.BlockSpec((HALO, W_IN_COLS), lambda i: (jnp.maximum(i * hb - 1, 0), 0)),
                  _row(0, (tm, D_MODEL)), _full((HALO, C_CONV)), _full((1, C_CONV)), _full((1, C_CONV)),
                  _full((1, C_CONV)), _full((4, POOL_GROUP, POOL_GROUP)), _full((1, C_POOL)),
                  _full((D_MODEL, D_MODEL))] + cs["in_specs"],
        out_specs=[_row(0, (tm, D_MODEL)), _row(0, (tm, D_MODEL)), _row(0, (tm, C_CONV)), _row(0, (tm, C_POOL))]
        + cs["out_specs"],
        out_shape=[jax.ShapeDtypeStruct((s, D_MODEL), f32), jax.ShapeDtypeStruct((s, D_MODEL), bf16),
                   jax.ShapeDtypeStruct((s, C_CONV), f32), jax.ShapeDtypeStruct((s, C_POOL), bf16)] + cs["out_shape"],
        input_output_aliases={10 + k: 4 + k for k in range(nc)},
        scratch_shapes=[pltpu.VMEM((HALO + tm, C_CONV), f32), pltpu.VMEM((HALO + tm, C_POOL), f32)] + cs["scratch"],
        compiler_params=_cp("arbitrary"),
    )(z, z, x, conv_w, conv_b, ln_g, ln_b, pool_w, pool_scale, w_out, *carry)
    return res[:4], res[4:]


def _ffn_fwd(x1, g_ffn, w_gu_g, w_down, carry):
    s = x1.shape[0]
    tm = min(ROW_TILE, s)
    n = s // tm
    nc = len(carry)
    cs = _carried(carry)

    def body(x1_ref, g_ref, wg_ref, wu_ref, wd_ref, *refs):
        x2_ref, h2_ref, gu_ref, f_ref = refs[nc:nc + 4]
        acc_ref = refs[2 * nc + 4]
        gather = _Gather(refs[nc + 4:2 * nc + 4], *refs[2 * nc + 5:])
        i, c = pl.program_id(0), pl.program_id(1)
        pl.when(jnp.logical_and(i == 0, c == 0))(gather.start)
        direct = [(k, j) for k in range(nc) for j in range(2)]
        pl.when(jnp.logical_and(i == n // 2, c == 0))(functools.partial(gather.forward, direct))
        pl.when(jnp.logical_and(i == n - 1, c == 0))(functools.partial(gather.forward, [(k, 2) for k in range(nc)]))

        @pl.when(c == 0)
        def _():
            h, _, _ = _rms_fwd(x1_ref[...], g_ref[...])
            h2_ref[...] = h.astype(bf16)
            acc_ref[...] = jnp.zeros_like(acc_ref)

        h = h2_ref[...]
        for lo, hi in zip(FF_SUB[:-1], FF_SUB[1:]):
            gate = _dot(h, wg_ref[0, :, lo:hi])
            up = _dot(h, wu_ref[0, :, lo:hi])
            gu_ref[0, :, lo:hi] = gate.astype(bf16)
            gu_ref[1, :, lo:hi] = up.astype(bf16)
            f = (gate * _sigmoid(gate) * up).astype(bf16)
            f_ref[:, lo:hi] = f
            acc_ref[...] += _dot(f, wd_ref[lo:hi, :])

        @pl.when(c == 1)
        def _():
            x2_ref[...] = x1_ref[...] + acc_ref[...]

        pl.when(jnp.logical_and(i == n - 1, c == 1))(gather.finish)

    res = pl.pallas_call(
        body, name="ffn_fwd", grid=(n, 2),
        in_specs=[_row(0, (tm, D_MODEL)), _full((1, D_MODEL)),
                  pl.BlockSpec((1, D_MODEL, FF_CHUNK), lambda i, c: (c, 0, 0)),
                  pl.BlockSpec((1, D_MODEL, FF_CHUNK), lambda i, c: (2 + c, 0, 0)),
                  pl.BlockSpec((FF_CHUNK, D_MODEL), lambda i, c: (c, 0))] + cs["in_specs"],
        out_specs=[_row(0, (tm, D_MODEL)), _row(0, (tm, D_MODEL)),
                   pl.BlockSpec((2, tm, FF_CHUNK), lambda i, c: (0, i, c)),
                   pl.BlockSpec((tm, FF_CHUNK), lambda i, c: (i, c))] + cs["out_specs"],
        out_shape=[jax.ShapeDtypeStruct((s, D_MODEL), f32), jax.ShapeDtypeStruct((s, D_MODEL), bf16),
                   jax.ShapeDtypeStruct((2, s, D_FF), bf16), jax.ShapeDtypeStruct((s, D_FF), bf16)] + cs["out_shape"],
        input_output_aliases={5 + k: 4 + k for k in range(nc)},
        scratch_shapes=[pltpu.VMEM((tm, D_MODEL), f32)] + cs["scratch"],
        compiler_params=_cp("arbitrary", "arbitrary"),
    )(x1, g_ffn, w_gu_g, w_gu_g, w_down, *carry)
    return res[:4], res[4:]


def _ple_loss(x2, p, target, g_pg, g_post, g_final, w_pg, w_pu_g):
    s = x2.shape[0]
    tm = min(ROW_TILE, s)
    n = s // tm

    def body(x2_ref, p_ref, t_ref, gpg_ref, gpo_ref, gf_ref, wpg_ref, wpu_ref,
             dx2_ref, dwpg_ref, dwpu_ref, small_ref, apg_ref, apu_ref):
        i = pl.program_id(0)

        @pl.when(i == 0)
        def _():
            apg_ref[...] = jnp.zeros_like(apg_ref)
            apu_ref[...] = jnp.zeros_like(apu_ref)
            small_ref[...] = jnp.zeros_like(small_ref)

        x2 = x2_ref[...]
        h3, x2h, r2 = _rms_fwd(x2, gpg_ref[...])
        h3b = h3.astype(bf16)
        gate = _sigmoid(_dot(h3b, wpg_ref[...]))
        pb = p_ref[...].astype(bf16)
        pe = jnp.concatenate([_dot(pb, wpu_ref[j]) for j in range(N_CHIPS)], axis=-1)
        e, peh, rp = _rms_fwd(pe, gpo_ref[...])
        x3 = x2 + gate * e
        y, x3h, r3 = _rms_fwd(x3, gf_ref[...])
        d = y - t_ref[...]
        loss = 0.5 * jnp.sum(jnp.sum(d * d, axis=-1, keepdims=True) * (1.0 / D_MODEL), axis=0, keepdims=True)
        dx3, dgf = _rms_bwd(d * (1.0 / D_MODEL), x3h, r3, gf_ref[...])
        dpe, dgpo = _rms_bwd(dx3 * gate, peh, rp, gpo_ref[...])
        dgl = (dx3 * e * gate * (1.0 - gate)).astype(bf16)
        apg_ref[...] += _dot_tn(h3b, dgl)
        apu_ref[...] += _dot_tn(pb, dpe.astype(bf16))
        dh3 = _dot_nt(dgl, wpg_ref[...])
        dx2b, dgpg = _rms_bwd(dh3, x2h, r2, gpg_ref[...])
        dx2_ref[...] = dx3 + dx2b
        small_ref[0:1, :] += dgpg
        small_ref[1:2, :] += dgpo
        small_ref[2:3, :] += dgf
        small_ref[3:4, :] += jnp.broadcast_to(loss, (1, D_MODEL))

        @pl.when(i == n - 1)
        def _():
            dwpg_ref[...] = apg_ref[...].astype(bf16)
            for j in range(N_CHIPS):
                dwpu_ref[j] = apu_ref[:, j * PLE_CHUNK:(j + 1) * PLE_CHUNK].astype(bf16)

    return pl.pallas_call(
        body, name="ple_loss", grid=(n,),
        in_specs=[_row(0, (tm, D_MODEL)), _row(0, (tm, D_PLE)), _row(0, (tm, D_MODEL)),
                  _full((1, D_MODEL)), _full((1, D_MODEL)), _full((1, D_MODEL)),
                  _full((D_MODEL, D_MODEL)), _full((N_CHIPS, D_PLE, PLE_CHUNK))],
        out_specs=[_row(0, (tm, D_MODEL)), _full((D_MODEL, D_MODEL)), _full((N_CHIPS, D_PLE, PLE_CHUNK)),
                   _full((8, D_MODEL))],
        out_shape=[jax.ShapeDtypeStruct((s, D_MODEL), f32), jax.ShapeDtypeStruct((D_MODEL, D_MODEL), bf16),
                   jax.ShapeDtypeStruct((N_CHIPS, D_PLE, PLE_CHUNK), bf16), jax.ShapeDtypeStruct((8, D_MODEL), f32)],
        scratch_shapes=[pltpu.VMEM((D_MODEL, D_MODEL), f32), pltpu.VMEM((D_PLE, D_MODEL), f32)],
        compiler_params=_cp("arbitrary"),
    )(x2, p, target, g_pg, g_post, g_final, w_pg, w_pu_g)


def _crossed(parts):
    n = len(parts)
    return dict(in_specs=[ANY] * n, out_specs=[ANY] * n,
                out_shape=[jax.ShapeDtypeStruct(a.shape, a.dtype) for a in parts], scratch=_Cross.scratch(n))


def _ffn_bwd_dx(dx2, x1, gu, g_ffn, w_gu_g, w_down, parts):
    s = x1.shape[0]
    tm = min(ROW_TILE, s)
    n = s // tm
    nc = len(parts)
    cs = _crossed(parts)

    def body(dx2_ref, x1_ref, gu_ref, g_ref, wg_ref, wu_ref, wd_ref, *refs):
        dx1_ref, dgu_ref, small_ref = refs[nc:nc + 3]
        acc_ref = refs[2 * nc + 3]
        cross = _Cross(refs[:nc], refs[nc + 3:2 * nc + 3], *refs[2 * nc + 4:])
        i, c = pl.program_id(0), pl.program_id(1)
        pl.when(jnp.logical_and(i == 0, c == 0))(cross.start)

        @pl.when(jnp.logical_and(i == 0, c == 0))
        def _():
            small_ref[...] = jnp.zeros_like(small_ref)

        @pl.when(c == 0)
        def _():
            acc_ref[...] = jnp.zeros_like(acc_ref)

        dyb = dx2_ref[...].astype(bf16)
        for lo, hi in zip(FF_SUB[:-1], FF_SUB[1:]):
            df = _dot_nt(dyb, wd_ref[lo:hi, :])
            gate = gu_ref[0, :, lo:hi].astype(f32)
            up = gu_ref[1, :, lo:hi].astype(f32)
            sg = _sigmoid(gate)
            dgate = (df * up * _silu_grad(gate, sg)).astype(bf16)
            dup = (df * gate * sg).astype(bf16)
            dgu_ref[0, :, lo:hi] = dgate
            dgu_ref[1, :, lo:hi] = dup
            acc_ref[...] += _dot_nt(dgate, wg_ref[0, :, lo:hi]) + _dot_nt(dup, wu_ref[0, :, lo:hi])

        @pl.when(c == 1)
        def _():
            _, x1h, r1 = _rms_fwd(x1_ref[...], g_ref[...])
            dx1b, dg = _rms_bwd(acc_ref[...], x1h, r1, g_ref[...])
            dx1_ref[...] = dx2_ref[...] + dx1b
            small_ref[0:1, :] += dg

        pl.when(jnp.logical_and(i == n - 1, c == 1))(cross.finish)

    res = pl.pallas_call(
        body, name="ffn_bwd_dx", grid=(n, 2),
        in_specs=[_row(0, (tm, D_MODEL)), _row(0, (tm, D_MODEL)),
                  pl.BlockSpec((2, tm, FF_CHUNK), lambda i, c: (0, i, c)), _full((1, D_MODEL)),
                  pl.BlockSpec((1, D_MODEL, FF_CHUNK), lambda i, c: (c, 0, 0)),
                  pl.BlockSpec((1, D_MODEL, FF_CHUNK), lambda i, c: (2 + c, 0, 0)),
                  pl.BlockSpec((FF_CHUNK, D_MODEL), lambda i, c: (c, 0))] + cs["in_specs"],
        out_specs=[_row(0, (tm, D_MODEL)), pl.BlockSpec((2, tm, FF_CHUNK), lambda i, c: (0, i, c)),
                   _full((8, D_MODEL))] + cs["out_specs"],
        out_shape=[jax.ShapeDtypeStruct((s, D_MODEL), f32), jax.ShapeDtypeStruct((2, s, D_FF), bf16),
                   jax.ShapeDtypeStruct((8, D_MODEL), f32)] + cs["out_shape"],
        scratch_shapes=[pltpu.VMEM((tm, D_MODEL), f32)] + cs["scratch"],
        compiler_params=_cp("arbitrary", "arbitrary"),
    )(dx2, x1, gu, g_ffn, w_gu_g, w_gu_g, w_down, *parts)
    return res[:3], res[3:]


def _ffn_bwd_dw_gu(h2, dgu):
    s = h2.shape[0]
    ts = min(2 * ROW_TILE, s)
    n = s // ts

    def body(h_ref, d_ref, o_ref, acc_ref):
        t = pl.program_id(1)

        @pl.when(t == 0)
        def _():
            acc_ref[...] = jnp.zeros_like(acc_ref)

        acc_ref[...] += _dot_tn(h_ref[...], d_ref[0])

        @pl.when(t == n - 1)
        def _():
            o_ref[0] = acc_ref[...].astype(bf16)

    return pl.pallas_call(
        body, name="ffn_bwd_dw_gu", grid=(N_CHIPS, n),
        in_specs=[pl.BlockSpec((ts, D_MODEL), lambda j, t: (t, 0)),
                  pl.BlockSpec((1, ts, FF_CHUNK), lambda j, t: (j // 2, t, j % 2))],
        out_specs=pl.BlockSpec((1, D_MODEL, FF_CHUNK), lambda j, t: (j, 0, 0)),
        out_shape=jax.ShapeDtypeStruct((N_CHIPS, D_MODEL, FF_CHUNK), bf16),
        scratch_shapes=[pltpu.VMEM((D_MODEL, FF_CHUNK), f32)],
        compiler_params=_cp("parallel", "arbitrary"),
    )(h2, dgu)


def _ffn_bwd_dw_down(f, dx2):
    s = dx2.shape[0]
    ts = min(2 * ROW_TILE, s)
    n = s // ts

    def body(f_ref, d_ref, o_ref, acc_ref):
        t = pl.program_id(1)

        @pl.when(t == 0)
        def _():
            acc_ref[...] = jnp.zeros_like(acc_ref)

        acc_ref[...] += _dot_tn(f_ref[...], d_ref[...].astype(bf16))

        @pl.when(t == n - 1)
        def _():
            o_ref[...] = acc_ref[...].astype(bf16)

    return pl.pallas_call(
        body, name="ffn_bwd_dw_down", grid=(2, n),
        in_specs=[pl.BlockSpec((ts, FF_CHUNK), lambda c, t: (t, c)),
                  pl.BlockSpec((ts, D_MODEL), lambda c, t: (t, 0))],
        out_specs=pl.BlockSpec((FF_CHUNK, D_MODEL), lambda c, t: (c, 0)),
        out_shape=jax.ShapeDtypeStruct((D_FF, D_MODEL), bf16),
        scratch_shapes=[pltpu.VMEM((FF_CHUNK, D_MODEL), f32)],
        compiler_params=_cp("parallel", "arbitrary"),
    )(f, dx2)


def _mix_bwd_local(dx1, mix, u1, pooled, w_out, ln_g, ln_b, pool_w, pool_scale):
    s = dx1.shape[0]
    tm = min(ROW_TILE, s)
    n = s // tm

    def body(dx1_ref, mix_ref, u1_ref, po_ref, wo_ref, lg_ref, lb_ref, pw_ref, ps_ref,
             du1_ref, dpo_ref, dwo_ref, dpw_ref, small_ref, awo_ref):
        i = pl.program_id(0)

        @pl.when(i == 0)
        def _():
            awo_ref[...] = jnp.zeros_like(awo_ref)
            dpw_ref[...] = jnp.zeros_like(dpw_ref)
            small_ref[...] = jnp.zeros_like(small_ref)

        dyb = dx1_ref[...].astype(bf16)
        dmix = _dot_nt(dyb, wo_ref[...])
        awo_ref[...] += _dot_tn(mix_ref[...], dyb)
        u1 = u1_ref[...]
        mu = jnp.mean(u1, axis=-1, keepdims=True)
        uc = u1 - mu
        rstd = lax.rsqrt(jnp.mean(uc * uc, axis=-1, keepdims=True) + EPS)
        uh = uc * rstd
        u2 = uh * lg_ref[...] + lb_ref[...]
        du2 = dmix[:, :C_CONV] * _silu_grad(u2, _sigmoid(u2))
        duh = du2 * lg_ref[...]
        du1 = rstd * (duh - jnp.mean(duh, axis=-1, keepdims=True) - uh * jnp.mean(duh * uh, axis=-1, keepdims=True))
        du1_ref[...] = du1
        small_ref[0:1, :] += jnp.sum(du2 * uh, axis=0, keepdims=True)
        small_ref[1:2, :] += jnp.sum(du2, axis=0, keepdims=True)
        small_ref[2:3, :] += jnp.sum(du1, axis=0, keepdims=True)
        for g in range(len(POOL_WINDOWS)):
            cols = slice(g * POOL_GROUP, (g + 1) * POOL_GROUP)
            dq = dmix[:, C_CONV + g * POOL_GROUP:C_CONV + (g + 1) * POOL_GROUP]
            pwb = pw_ref[g].astype(bf16)
            pg = po_ref[:, cols]
            mixed = _dot(pg, pwb)
            small_ref[3:4, cols] += jnp.sum(dq * mixed, axis=0, keepdims=True)
            dmixed = (dq * ps_ref[:, cols]).astype(bf16)
            dpw_ref[g] += _dot_tn(pg, dmixed)
            dpo_ref[:, cols] = _dot_nt(dmixed, pwb)

        @pl.when(i == n - 1)
        def _():
            dwo_ref[...] = awo_ref[...].astype(bf16)

    return pl.pallas_call(
        body, name="mix_bwd_local", grid=(n,),
        in_specs=[_row(0, (tm, D_MODEL)), _row(0, (tm, D_MODEL)), _row(0, (tm, C_CONV)), _row(0, (tm, C_POOL)),
                  _full((D_MODEL, D_MODEL)), _full((1, C_CONV)), _full((1, C_CONV)),
                  _full((4, POOL_GROUP, POOL_GROUP)), _full((1, C_POOL))],
        out_specs=[_row(0, (tm, C_CONV)), _row(0, (tm, C_POOL)), _full((D_MODEL, D_MODEL)),
                   _full((4, POOL_GROUP, POOL_GROUP)), _full((8, C_CONV))],
        out_shape=[jax.ShapeDtypeStruct((s, C_CONV), f32), jax.ShapeDtypeStruct((s, C_POOL), f32),
                   jax.ShapeDtypeStruct((D_MODEL, D_MODEL), bf16),
                   jax.ShapeDtypeStruct((4, POOL_GROUP, POOL_GROUP), f32), jax.ShapeDtypeStruct((8, C_CONV), f32)],
        scratch_shapes=[pltpu.VMEM((D_MODEL, D_MODEL), f32)],
        compiler_params=_cp("arbitrary"),
    )(dx1, mix, u1, pooled, w_out, ln_g, ln_b, pool_w, pool_scale)


def _in_bwd(du1, dpo, z, x, dx1, conv_w, g_mix, w_in_g, parts, small_part):
    s = x.shape[0]
    tm = min(ROW_TILE, s)
    n = s // tm
    hb = tm // HALO
    last = s // HALO - 1
    nv = tm // 8
    assert nv >= SHIFT_PAD and nv % SHIFT_GROUPS == 0
    nc = len(parts)
    cs = _crossed(parts)

    def body(du_ref, dun_ref, dp_ref, dpn_ref, z_ref, zp_ref, x_ref, dx1_ref, cw_ref, g_ref, w_ref, *refs):
        gx_ref, dw_ref, dcw_ref, small_ref = refs[nc + 1:nc + 5]
        eu, ed, ep, ss, u0_ref, dz_ref, acc_ref, dcw_acc = refs[2 * nc + 6:2 * nc + 14]
        sems = refs[2 * nc + 14:]
        cross = _Cross(refs[:nc], refs[nc + 5:2 * nc + 5], sems[0], sems[1])
        gather = _SmallGather(refs[nc], refs[2 * nc + 5], sems[2], sems[3])
        i = pl.program_id(0)
        pl.when(i == 0)(cross.start)
        pl.when(i == 0)(gather.start)

        @pl.when(i == 0)
        def _():
            acc_ref[...] = jnp.zeros_like(acc_ref)
            dcw_acc[...] = jnp.zeros_like(dcw_acc)
            small_ref[...] = jnp.zeros_like(small_ref)

        keep_prev = (i > 0).astype(f32)
        keep_next = (i < n - 1).astype(f32)
        zp = zp_ref[...] * keep_prev
        u0_prev = zp[:, :C_CONV] * _sigmoid(zp[:, C_CONV:2 * C_CONV])
        u0_ref[...] = z_ref[:, :C_CONV] * _sigmoid(z_ref[:, C_CONV:2 * C_CONV])
        du_next = dun_ref[...] * keep_next
        for c, w_pool in enumerate(POOL_WINDOWS):
            lanes = slice(c * 128, (c + 1) * 128)
            eu[c, pl.ds(0, SHIFT_PAD, stride=8), :] = u0_prev[:, lanes]
            ed[c, pl.ds(nv * 8 + 7, SHIFT_PAD, stride=8), :] = du_next[:, lanes]
            ep[c, pl.ds(nv * 8 + 7, SHIFT_PAD, stride=8), :] = (
                dpn_ref[:, lanes] * keep_next / _pool_counts(HALO, w_pool, (i + 1) * tm))
            for j in range(8):
                rows = slice(j * nv, (j + 1) * nv)
                eu[c, pl.ds(SHIFT_PAD * 8 + j, nv, stride=8), :] = u0_ref[rows, lanes]
                ed[c, pl.ds(j, nv, stride=8), :] = du_ref[rows, lanes]
                ep[c, pl.ds(j, nv, stride=8), :] = dp_ref[rows, lanes] / _pool_counts(nv, w_pool, i * tm + j * nv)
                if j >= 1:
                    eu[c, pl.ds(j, SHIFT_PAD, stride=8), :] = u0_ref[j * nv - SHIFT_PAD:j * nv, lanes]
                if j <= 6:
                    edge = slice((j + 1) * nv, (j + 1) * nv + SHIFT_PAD)
                    ed[c, pl.ds(nv * 8 + j, SHIFT_PAD, stride=8), :] = du_ref[edge, lanes]
                    ep[c, pl.ds(nv * 8 + j, SHIFT_PAD, stride=8), :] = (
                        dp_ref[edge, lanes] / _pool_counts(SHIFT_PAD, w_pool, i * tm + (j + 1) * nv))
        for c, w_pool in enumerate(POOL_WINDOWS):
            lanes = slice(c * 128, (c + 1) * 128)
            b_lanes = slice(C_CONV + c * 128, C_CONV + (c + 1) * 128)
            v_lanes = slice(2 * C_CONV + c * 128, 2 * C_CONV + (c + 1) * 128)
            for v0 in range(0, nv, SHIFT_GROUPS):
                span = SHIFT_GROUPS * 8
                acc = jnp.zeros((span, 128), f32)
                for k in range(CONV_K):
                    acc = acc + cw_ref[pl.ds(k, 1), lanes] * ed[c, pl.ds((v0 + CONV_K - 1 - k) * 8, span), :]
                ss[0, v0 * 8:v0 * 8 + span, :] = acc
                acc = ep[c, pl.ds(v0 * 8, span), :]
                for d in range(1, w_pool):
                    acc = acc + ep[c, pl.ds((v0 + d) * 8, span), :]
                ss[1, v0 * 8:v0 * 8 + span, :] = acc
                d1 = ed[c, pl.ds(v0 * 8, span), :]
                for k in range(CONV_K):
                    prod = d1 * eu[c, pl.ds((SHIFT_PAD - (CONV_K - 1) + v0 + k) * 8, span), :]
                    fold = prod[0:8]
                    for r in range(8, span, 8):
                        fold = fold + prod[r:r + 8]
                    dcw_acc[k, :, lanes] += fold
            for j in range(8):
                rows = slice(j * nv, (j + 1) * nv)
                du0 = ss[0, pl.ds(j, nv, stride=8), :]
                av, sv = z_ref[rows, lanes], _sigmoid(z_ref[rows, b_lanes])
                dz_ref[rows, lanes] = (du0 * sv).astype(bf16)
                dz_ref[rows, b_lanes] = (du0 * av * sv * (1.0 - sv)).astype(bf16)
                dz_ref[rows, v_lanes] = (ss[1, pl.ds(j, nv, stride=8), :] - dp_ref[rows, lanes]).astype(bf16)
        h, xh, r = _rms_fwd(x_ref[...], g_ref[...])
        dz = dz_ref[...]
        acc_ref[...] += _dot_tn(h.astype(bf16), dz)
        dh = _dot_nt(dz[:, 0:W_IN_CHUNK], w_ref[0])
        for j in range(1, N_CHIPS):
            dh = dh + _dot_nt(dz[:, j * W_IN_CHUNK:(j + 1) * W_IN_CHUNK], w_ref[j])
        dxb, dg = _rms_bwd(dh, xh, r, g_ref[...])
        gx_ref[...] = dx1_ref[...] + dxb
        small_ref[0:1, :] += dg

        @pl.when(i == n - 1)
        def _():
            for j in range(N_CHIPS):
                dw_ref[j] = acc_ref[:, j * W_IN_CHUNK:(j + 1) * W_IN_CHUNK].astype(bf16)
            dcw_ref[...] = jnp.sum(dcw_acc[...], axis=1)

        pl.when(i == n - 1)(gather.finish)
        pl.when(i == n - 1)(cross.finish)

    nxt = lambda i: (jnp.minimum((i + 1) * hb, last), 0)
    res = pl.pallas_call(
        body, name="in_bwd", grid=(n,),
        in_specs=[_row(0, (tm, C_CONV)), pl.BlockSpec((HALO, C_CONV), nxt),
                  _row(0, (tm, C_POOL)), pl.BlockSpec((HALO, C_POOL), nxt),
                  _row(0, (tm, W_IN_COLS)),
                  pl.BlockSpec((HALO, W_IN_COLS), lambda i: (jnp.maximum(i * hb - 1, 0), 0)),
                  _row(0, (tm, D_MODEL)), _row(0, (tm, D_MODEL)), _full((HALO, C_CONV)), _full((1, D_MODEL)),
                  _full((N_CHIPS, D_MODEL, W_IN_CHUNK))] + cs["in_specs"] + [ANY],
        out_specs=[_row(0, (tm, D_MODEL)), _full((N_CHIPS, D_MODEL, W_IN_CHUNK)), _full((HALO, C_CONV)),
                   _full((8, D_MODEL))] + cs["out_specs"] + [ANY],
        out_shape=[jax.ShapeDtypeStruct((s, D_MODEL), f32), jax.ShapeDtypeStruct((N_CHIPS, D_MODEL, W_IN_CHUNK), bf16),
                   jax.ShapeDtypeStruct((HALO, C_CONV), f32), jax.ShapeDtypeStruct((8, D_MODEL), f32)] + cs["out_shape"]
        + [jax.ShapeDtypeStruct((N_DEV,) + small_part.shape, f32)],
        scratch_shapes=[pltpu.VMEM((4, (SHIFT_PAD + nv) * 8, 128), f32), pltpu.VMEM((4, (nv + SHIFT_PAD) * 8, 128), f32),
                        pltpu.VMEM((4, (nv + SHIFT_PAD) * 8, 128), f32), pltpu.VMEM((2, tm, 128), f32),
                        pltpu.VMEM((tm, C_CONV), f32), pltpu.VMEM((tm, W_IN_COLS), bf16),
                        pltpu.VMEM((D_MODEL, W_IN_COLS), f32), pltpu.VMEM((HALO, 8, C_CONV), f32)] + cs["scratch"]
        + _SmallGather.scratch(),
        compiler_params=_cp("arbitrary"),
    )(du1, du1, dpo, dpo, z, z, x, dx1, conv_w, g_mix, w_in_g, *parts, small_part)
    return res[:4], res[4:4 + nc], res[4 + nc]


def _pair_reduce(name, grads):
    nk = len(grads)
    halves = [g.shape[1] // 2 for g in grads]

    def body(*refs):
        ins, outs, got = refs[:nk], refs[nk:2 * nk], refs[2 * nk:3 * nk]
        send_sems, recv_sems = refs[3 * nk:]
        x, y, c, _, _ = _place()

        def half(k, core):
            return pl.ds(pl.multiple_of(core * halves[k], 16), halves[k])

        cps = [_remote(ins[k].at[:, half(k, 1 - c), :], got[k], send_sems.at[k], recv_sems.at[k], (x, y, 1 - c))
               for k in range(nk)]
        for cp in cps:
            cp.start()
        for k, cp in enumerate(cps):
            cp.wait_recv()
            outs[k][...] = (ins[k][:, half(k, c), :].astype(f32) + got[k][...].astype(f32)).astype(bf16)
        for cp in cps:
            cp.wait_send()

    shapes = [(N_CHIPS, h, g.shape[2]) for g, h in zip(grads, halves)]
    return pl.pallas_call(
        body, name=f"pair_reduce_{name}", in_specs=[VMEM] * nk, out_specs=[VMEM] * nk,
        out_shape=[jax.ShapeDtypeStruct(s, bf16) for s in shapes],
        scratch_shapes=[pltpu.VMEM(s, bf16) for s in shapes]
        + [pltpu.SemaphoreType.DMA((nk,)), pltpu.SemaphoreType.DMA((nk,))],
        compiler_params=pltpu.CompilerParams(vmem_limit_bytes=VMEM_LIMIT),
    )(*grads)


class _Cross:
    def __init__(self, parts, landed, send_sems, recv_sems):
        self.parts, self.landed, self.send_sems, self.recv_sems = parts, landed, send_sems, recv_sems
        _, _, self.c, self.me, self.others = _place()

    def _copy(self, k, j, src_chunk, dst_slot):
        ox, oy = self.others[j]
        return _remote(self.parts[k].at[src_chunk], self.landed[k].at[dst_slot], self.send_sems.at[3 * k + j],
                       self.recv_sems.at[3 * k + j], (ox, oy, self.c))

    def _each(self):
        return [(k, j, 2 * self.others[j][0] + self.others[j][1]) for k in range(len(self.parts)) for j in range(3)]

    def start(self):
        for k, j, chip in self._each():
            self._copy(k, j, chip, self.me).start()

    def finish(self):
        for k, j, chip in self._each():
            self._copy(k, j, chip, chip).wait_recv()
        for k, j, chip in self._each():
            self._copy(k, j, chip, self.me).wait_send()

    @staticmethod
    def scratch(n):
        return [pltpu.SemaphoreType.DMA((3 * n,)), pltpu.SemaphoreType.DMA((3 * n,))]


def _rs_sum_chips(name, place, landed, part):
    _, half, cols = landed.shape

    def body(place_ref, l_ref, p_ref, o_ref):
        me = place_ref[0]
        own = p_ref[0].astype(f32)
        acc = jnp.where(me == 0, own, l_ref[0].astype(f32))
        for j in range(1, N_CHIPS):
            acc = acc + jnp.where(me == j, own, l_ref[j].astype(f32))
        o_ref[...] = acc

    return pl.pallas_call(
        body, name=f"rs_sum_chips_{name}",
        grid_spec=pltpu.PrefetchScalarGridSpec(
            num_scalar_prefetch=1, grid=(1,),
            in_specs=[pl.BlockSpec((N_CHIPS, half, cols), lambda t, place_ref: (0, 0, 0)),
                      pl.BlockSpec((1, half, cols), lambda t, place_ref: (place_ref[0], 0, 0))],
            out_specs=pl.BlockSpec((half, cols), lambda t, place_ref: (place_ref[1], 0))),
        out_shape=jax.ShapeDtypeStruct((2 * half, cols), f32),
        compiler_params=_cp("arbitrary"),
    )(place, landed, part)


def _grad_pair(shards):
    nk = len(shards)

    def body(*refs):
        outs = refs[nk:2 * nk]
        send_sems, recv_sems = refs[2 * nk:]
        x, y, c, _, _ = _place()

        def half(k, core):
            h = outs[k].shape[0] // 2
            return outs[k].at[pl.ds(core * h, h), :]

        cps = [_remote(half(k, c), half(k, c), send_sems.at[k], recv_sems.at[k], (x, y, 1 - c)) for k in range(nk)]
        for cp in cps:
            cp.start()
        for k in range(nk):
            _remote(half(k, 1 - c), half(k, 1 - c), send_sems.at[k], recv_sems.at[k], (x, y, 1 - c)).wait_recv()
        for cp in cps:
            cp.wait_send()

    return pl.pallas_call(
        body, name="grad_pair", in_specs=[ANY] * nk, out_specs=[ANY] * nk,
        out_shape=[jax.ShapeDtypeStruct(a.shape, f32) for a in shards],
        input_output_aliases={k: k for k in range(nk)},
        scratch_shapes=[pltpu.SemaphoreType.DMA((nk,)), pltpu.SemaphoreType.DMA((nk,))],
    )(*shards)


def _adam_math(w, g, m, v):
    m = ADAM_B1 * m + (1.0 - ADAM_B1) * g
    v = ADAM_B2 * v + (1.0 - ADAM_B2) * (g * g)
    m_hat = m / (1.0 - ADAM_B1 ** ADAM_STEP)
    v_hat = v / (1.0 - ADAM_B2 ** ADAM_STEP)
    delta = -ADAM_LR * (m_hat / (jnp.sqrt(v_hat) + ADAM_EPS) + ADAM_WD * w)
    return delta, m, v


def _adam(ws, gs, ms, vs):
    n = len(ws)
    steps = 2 * SHARD_STEPS

    def body(*refs):
        for k in range(n):
            w_ref, g_ref, m_ref, v_ref = (refs[j * n + k] for j in range(4))
            d_ref, nm_ref, nv_ref = (refs[(4 + j) * n + k] for j in range(3))
            d_ref[...], nm_ref[...], nv_ref[...] = _adam_math(w_ref[...], g_ref[...], m_ref[...], v_ref[...])

    specs = [_row(0, (w.shape[0] // steps, w.shape[1])) for w in ws]
    res = pl.pallas_call(
        body, name="adam_shards", grid=(steps,), in_specs=specs * 4, out_specs=specs * 3,
        out_shape=[jax.ShapeDtypeStruct(w.shape, f32) for w in ws] * 3,
        compiler_params=_cp("parallel"),
    )(*ws, *gs, *ms, *vs)
    return [(res[k], res[n + k], res[2 * n + k]) for k in range(n)]


class _SmallGather:
    def __init__(self, part, buf, send_sems, recv_sems):
        self.part, self.buf, self.send_sems, self.recv_sems = part, buf, send_sems, recv_sems
        self.x, self.y, self.c, _, self.others = _place()
        self.sibling = (self.x, self.y, 1 - self.c)

    def _copy(self, k, block, to, src=None):
        slot = self.buf.at[4 * block[0] + 2 * block[1] + block[2]]
        return _remote(slot if src is None else src, slot, self.send_sems.at[k], self.recv_sems.at[k], to)

    def _first(self):
        me = (self.x, self.y, self.c)
        return [self._copy(0, me, self.sibling, src=self.part)] + [
            self._copy(1 + j, me, (*chip, self.c), src=self.part) for j, chip in enumerate(self.others)]

    def _passed(self):
        return [self._copy(4 + j, (*chip, self.c), self.sibling) for j, chip in enumerate(self.others)]

    @staticmethod
    def scratch():
        return [pltpu.SemaphoreType.DMA((7,)), pltpu.SemaphoreType.DMA((7,))]

    def start(self):
        for cp in self._first():
            cp.start()

    def finish(self):
        passed = self._passed()
        for j, chip in enumerate(self.others):
            self._copy(1 + j, (*chip, self.c), self.sibling).wait_recv()
            passed[j].start()
        self._copy(0, self.sibling, self.sibling).wait_recv()
        for j, chip in enumerate(self.others):
            self._copy(4 + j, (*chip, 1 - self.c), self.sibling).wait_recv()
        for cp in self._first() + passed:
            cp.wait_send()


def _rs_sum_group(name, place, landed, parts, cross_parts, small_part):
    nk, nx = len(landed), len(cross_parts)
    dims = [a.shape[1:] for a in landed]
    rows = small_part.shape[0]

    def body(place_ref, *refs):
        l_refs, p_refs = refs[:nk], refs[nk:2 * nk]
        x_refs, sp_ref = refs[2 * nk:2 * nk + nx], refs[2 * nk + nx]
        outs = refs[2 * nk + nx + 1:]
        o_refs, xl_refs, sbuf = outs[:nk], outs[nk:nk + nx], outs[nk + nx]
        sems = outs[nk + nx + 1:]
        cross = _Cross(x_refs, xl_refs, sems[0], sems[1])
        small = _SmallGather(sp_ref, sbuf, sems[2], sems[3])
        t = pl.program_id(0)

        @pl.when(t == 0)
        def _():
            cross.start()
            small.start()

        me = place_ref[0]
        for l_ref, p_ref, o_ref in zip(l_refs, p_refs, o_refs):
            own = p_ref[0].astype(f32)
            acc = jnp.where(me == 0, own, l_ref[0].astype(f32))
            for j in range(1, N_CHIPS):
                acc = acc + jnp.where(me == j, own, l_ref[j].astype(f32))
            o_ref[...] = acc

        @pl.when(t == 1)
        def _():
            small.finish()
            cross.finish()

    def halves(h, c, lead, index):
        return pl.BlockSpec((lead, h // 2, c) if lead else (h // 2, c), index)

    in_specs = [halves(h, c, N_CHIPS, lambda t, pr: (0, t, 0)) for h, c in dims]
    in_specs += [halves(h, c, 1, lambda t, pr: (pr[0], t, 0)) for h, c in dims]
    in_specs += [ANY] * (nx + 1)
    out_specs = [halves(h, c, 0, lambda t, pr: (2 * pr[1] + t, 0)) for h, c in dims] + [ANY] * (nx + 1)
    out_shape = [jax.ShapeDtypeStruct((2 * h, c), f32) for h, c in dims]
    out_shape += [jax.ShapeDtypeStruct(a.shape, a.dtype) for a in cross_parts]
    out_shape.append(jax.ShapeDtypeStruct((N_DEV, rows, 128), f32))
    res = pl.pallas_call(
        body, name=f"rs_sum_{name}",
        grid_spec=pltpu.PrefetchScalarGridSpec(
            num_scalar_prefetch=1, grid=(2,), in_specs=in_specs, out_specs=out_specs,
            scratch_shapes=_Cross.scratch(nx) + _SmallGather.scratch()),
        out_shape=out_shape, compiler_params=_cp("arbitrary"),
    )(place, *landed, *parts, *cross_parts, small_part)
    return res[:nk], res[nk:nk + nx], res[nk + nx]


def _small_adam(dev, gathered, parts, w, m, v):
    rows = [p.shape[0] for p in parts]

    def body(dev_ref, b0_ref, b1_ref, p0_ref, p1_ref, w_ref, m_ref, v_ref, g_ref, d_ref, nm_ref, nv_ref):
        me = dev_ref[0]
        lo = 0
        for b_ref, p_ref, n_rows in ((b0_ref, p0_ref, rows[0]), (b1_ref, p1_ref, rows[1])):
            own = p_ref[...]
            g = jnp.where(me == 0, own, b_ref[0])
            for d in range(1, N_DEV):
                g = g + jnp.where(me == d, own, b_ref[d])
            sl = slice(lo, lo + n_rows)
            g_ref[sl, :] = g
            d_ref[sl, :], nm_ref[sl, :], nv_ref[sl, :] = _adam_math(w_ref[sl, :], g, m_ref[sl, :], v_ref[sl, :])
            lo += n_rows

    shape = jax.ShapeDtypeStruct(w.shape, f32)
    flat = pl.BlockSpec(w.shape, lambda t, dev_ref: (0, 0))
    whole = lambda a: pl.BlockSpec(a.shape, lambda t, dev_ref: (0,) * a.ndim)
    return pl.pallas_call(
        body, name="small_adam",
        grid_spec=pltpu.PrefetchScalarGridSpec(
            num_scalar_prefetch=1, grid=(1,),
            in_specs=[whole(gathered[0]), whole(gathered[1]), whole(parts[0]), whole(parts[1])] + [flat] * 3,
            out_specs=[flat] * 4),
        out_shape=[shape] * 4, compiler_params=_cp("arbitrary"),
    )(dev, *gathered, *parts, w, m, v)


def _rows128(a):
    return a.reshape(-1, 128)


def _pad_rows(a, rows):
    return jnp.concatenate([a, jnp.zeros((rows - a.shape[0],) + a.shape[1:], a.dtype)], axis=0)


def _pack_small(me, g_pg, g_post, g_final, g_ffn, ln_g, ln_b, conv_b, pool_scale, pool_w, conv_w, g_mix):
    blk_ple = _pad_rows(jnp.concatenate([g_pg, g_post, g_final.reshape(1, D_MODEL)], axis=0), 8)
    blk_ffn = _pad_rows(g_ffn, 8)
    blk_mix = _pad_rows(jnp.concatenate([ln_g, ln_b, conv_b, pool_scale], axis=0), 8)
    cw = lax.dynamic_update_slice(jnp.zeros((N_CHIPS, HALO, 128), f32), _pad_rows(conv_w, HALO)[None], (me, 0, 0))
    blk_in = _pad_rows(g_mix, 8)
    return jnp.concatenate([_rows128(blk_ple), _rows128(blk_ffn), _rows128(blk_mix), _rows128(pool_w),
                            _rows128(cw), _rows128(blk_in)], axis=0)


def _unpack_small(me, packed):
    o = 0

    def take(rows):
        nonlocal o
        blk = packed[o:o + rows]
        o += rows
        return blk

    ple = take(64).reshape(8, D_MODEL)
    ffn = take(64).reshape(8, D_MODEL)
    mixb = take(32).reshape(8, C_CONV)
    pool_w = take(512).reshape(1, 4, POOL_GROUP, POOL_GROUP)
    cw = take(N_CHIPS * HALO).reshape(N_CHIPS, HALO, 128)
    inb = take(64).reshape(8, D_MODEL)
    conv_w = lax.dynamic_slice(cw, (me, 0, 0), (1, HALO, 128))[:, :CONV_K, :]
    return dict(g_ple_gate=ple[0:1], g_ple_post=ple[1:2], g_final=ple[2], loss=ple[3, 0], g_ffn=ffn[0:1],
                ln_g=mixb[0:1], ln_b=mixb[1:2], conv_b=mixb[2:3], pool_scale=mixb[3:4], pool_w=pool_w,
                conv_w=conv_w, g_mix=inb[0:1])


def kernel(x, p, g_mix, w_in, conv_w, conv_b, ln_g, ln_b, pool_w, pool_scale, w_out, g_ffn, w_gate_up, w_down, g_ple_gate, w_ple_gate, w_ple_up, g_ple_post, g_final, loss_target, m_g_mix, m_w_in, m_conv_w, m_conv_b, m_ln_g, m_ln_b, m_pool_w, m_pool_scale, m_w_out, m_g_ffn, m_w_gate_up, m_w_down, m_g_ple_gate, m_w_ple_gate, m_w_ple_up, m_g_ple_post, m_g_final, v_g_mix, v_w_in, v_conv_w, v_conv_b, v_ln_g, v_ln_b, v_pool_w, v_pool_scale, v_w_out, v_g_ffn, v_w_gate_up, v_w_down, v_g_ple_gate, v_w_ple_gate, v_w_ple_up, v_g_ple_post, v_g_final):
    seq = x.shape[1]
    me = 2 * lax.axis_index("x") + lax.axis_index("y")
    chip = me.astype(jnp.int32).reshape(1)
    core = lax.axis_index("c").astype(jnp.int32).reshape(1)
    place = jnp.concatenate([chip, core])
    xs, ps, ts = x.reshape(seq, D_MODEL), p.reshape(seq, D_PLE), loss_target.reshape(seq, D_MODEL)

    big = [w_in[0], w_gate_up[0], w_out[0], w_down[0], w_ple_gate[0], w_ple_up[0]]
    big_m = [m_w_in[0], m_w_gate_up[0], m_w_out[0], m_w_down[0], m_w_ple_gate[0], m_w_ple_up[0]]
    big_v = [v_w_in[0], v_w_gate_up[0], v_w_out[0], v_w_down[0], v_w_ple_gate[0], v_w_ple_up[0]]
    b_in, b_gu, b_out, b_down, b_pg, b_pu, b_cw = _cast_into_slots(
        chip, big + [_pad_rows(conv_w[0], HALO)], [bf16] * len(big) + [f32])
    xi, yi = lax.axis_index("x"), lax.axis_index("y")
    order = jnp.stack([me, 2 * (1 - xi) + yi, 2 * xi + 1 - yi, 2 * (1 - xi) + 1 - yi]).astype(jnp.int32)

    z, (w_in_g, cw_g, w_out_g) = _mix_in(xs, g_mix, order, [b_in, b_cw, b_out])
    conv_w_f = cw_g.transpose(1, 0, 2).reshape(HALO, C_CONV)
    w_out_f = w_out_g.reshape(D_MODEL, D_MODEL)
    (x1, mix, u1, pooled), (w_gu_g, w_down_g) = _conv_pool_out(z, xs, conv_w_f, conv_b, ln_g, ln_b, pool_w[0],
                                                                pool_scale, w_out_f, [b_gu, b_down])
    w_down_f = w_down_g.reshape(D_FF, D_MODEL)
    (x2, h2, gu, ffn_f), (w_pg_g, w_pu_g) = _ffn_fwd(x1, g_ffn, w_gu_g, w_down_f, [b_pg, b_pu])
    w_pg_f = w_pg_g.reshape(D_MODEL, D_MODEL)
    dx2, d_w_pg, d_w_pu, small_ple = _ple_loss(x2, ps, ts, g_ple_gate, g_ple_post, g_final.reshape(1, D_MODEL),
                                               w_pg_f, w_pu_g)
    d_w_down = _ffn_bwd_dw_down(ffn_f, dx2)
    parts_a = _pair_reduce("a", [d_w_pg.reshape(N_CHIPS, -1, D_MODEL), d_w_pu,
                                 d_w_down.reshape(N_CHIPS, -1, D_MODEL)])
    (dx1, dgu, small_ffn), landed_a = _ffn_bwd_dx(dx2, x1, gu, g_ffn, w_gu_g, w_down_f, parts_a)
    d_w_gu = _ffn_bwd_dw_gu(h2, dgu)
    du1, dpo, d_w_out, d_pool_w, small_mix = _mix_bwd_local(dx1, mix, u1, pooled, w_out_f, ln_g, ln_b, pool_w[0],
                                                             pool_scale)
    parts_b = _pair_reduce("b", [d_w_gu, d_w_out.reshape(N_CHIPS, -1, D_MODEL)])
    part_0 = jnp.concatenate([_rows128(small_ple), _rows128(small_ffn), _rows128(small_mix), _rows128(d_pool_w)],
                             axis=0)
    (grad_x, d_w_in, d_conv_w, small_in), landed_b, small_all_0 = _in_bwd(du1, dpo, z, xs, dx1, conv_w_f, g_mix,
                                                                            w_in_g, parts_b, part_0)
    parts_c = _pair_reduce("c", [d_w_in])
    cw_chunks = d_conv_w.reshape(HALO, N_CHIPS, 128).transpose(1, 0, 2)
    part_1 = jnp.concatenate([_rows128(cw_chunks), _rows128(small_in)], axis=0)
    (h_gu, h_out, h_down, h_pg, h_pu), landed_c, small_all_1 = _rs_sum_group(
        "ab", place, [landed_b[0], landed_b[1], landed_a[2], landed_a[0], landed_a[1]],
        [parts_b[0], parts_b[1], parts_a[2], parts_a[0], parts_a[1]], parts_c, part_1)
    h_in = _rs_sum_chips("w_in", place, landed_c[0], parts_c[0])
    big_g = _grad_pair([h_in, h_gu, h_out, h_down, h_pg, h_pu])
    big_upd = _adam(big, big_g, big_m, big_v)

    sw = _pack_small(me, g_ple_gate, g_ple_post, g_final, g_ffn, ln_g, ln_b, conv_b, pool_scale, pool_w, conv_w[0], g_mix)
    sm = _pack_small(me, m_g_ple_gate, m_g_ple_post, m_g_final, m_g_ffn, m_ln_g, m_ln_b, m_conv_b, m_pool_scale,
                     m_pool_w, m_conv_w[0], m_g_mix)
    sv = _pack_small(me, v_g_ple_gate, v_g_ple_post, v_g_final, v_g_ffn, v_ln_g, v_ln_b, v_conv_b, v_pool_scale,
                     v_pool_w, v_conv_w[0], v_g_mix)
    dev = (2 * chip + core).astype(jnp.int32)
    small = [_unpack_small(me, a) for a in _small_adam(dev, [small_all_0, small_all_1], [part_0, part_1], sw, sm, sv)]

    names = ["g_mix", "w_in", "conv_w", "conv_b", "ln_g", "ln_b", "pool_w", "pool_scale", "w_out", "g_ffn",
             "w_gate_up", "w_down", "g_ple_gate", "w_ple_gate", "w_ple_up", "g_ple_post", "g_final"]
    big_at = {"w_in": 0, "w_gate_up": 1, "w_out": 2, "w_down": 3, "w_ple_gate": 4, "w_ple_up": 5}
    out = [small[0]["loss"], grad_x.reshape(1, seq, D_MODEL)]
    for kind in range(4):
        for nm in names:
            if nm in big_at:
                k = big_at[nm]
                out.append((big_g[k] if kind == 0 else big_upd[k][kind - 1])[None])
            else:
                out.append(small[kind][nm])
    return tuple(out)
```

```python
import functools

import jax
import jax.numpy as jnp
from jax import lax
from jax.experimental import pallas as pl
from jax.experimental.pallas import tpu as pltpu

f32, bf16 = jnp.float32, jnp.bfloat16

EPS = 1e-6
D_MODEL = 1024
C_CONV = 512
C_POOL = 512
POOL_WINDOWS = (2, 4, 8, 16)
POOL_GROUP = 128
CONV_K = 31
D_FF = 2816
D_PLE = 256
N_CHIPS = 4
N_DEV = 8
W_IN_COLS = 2 * C_CONV + C_POOL
W_IN_CHUNK = W_IN_COLS // N_CHIPS
FF_CHUNK = 2 * D_FF // N_CHIPS
PLE_CHUNK = D_MODEL // N_CHIPS
HALO = 32
ROW_TILE = 512
CONV_ROWS = 64
CONV_COLS = (slice(0, 256), slice(256, 512))
SHIFT_PAD = 32
SHIFT_GROUPS = 16
FF_SUB = (0, 512, 1024, FF_CHUNK)
VMEM_LIMIT = 56 * 1024 * 1024

ADAM_LR = 0.001
ADAM_B1 = 0.9
ADAM_B2 = 0.999
ADAM_EPS = 1e-08
ADAM_WD = 0.01
ADAM_STEP = 10

MESH = pl.DeviceIdType.MESH
ANY = pl.BlockSpec(memory_space=pl.ANY)
VMEM = pl.BlockSpec(memory_space=pltpu.VMEM)


def _cp(*sem):
    return pltpu.CompilerParams(dimension_semantics=sem, vmem_limit_bytes=VMEM_LIMIT)


def _dot(a, b):
    return jnp.dot(a, b, preferred_element_type=f32)


def _dot_nt(a, b):
    return lax.dot_general(a, b, (((1,), (1,)), ((), ())), preferred_element_type=f32)


def _dot_tn(a, b):
    return lax.dot_general(a, b, (((0,), (0,)), ((), ())), preferred_element_type=f32)


def _sigmoid(v):
    return jax.nn.sigmoid(v)


def _rms_fwd(v, g):
    r = lax.rsqrt(jnp.mean(v * v, axis=-1, keepdims=True) + EPS)
    vh = v * r
    return vh * g, vh, r


def _rms_bwd(dy, vh, r, g):
    dvh = dy * g
    dv = r * (dvh - vh * jnp.mean(dvh * vh, axis=-1, keepdims=True))
    return dv, jnp.sum(dy * vh, axis=0, keepdims=True)


def _silu_grad(v, s):
    return s * (1.0 + v * (1.0 - s))


def _row(i, n):
    return pl.BlockSpec((n[0], n[1]), lambda *a: (a[i], 0))


def _full(shape):
    nd = len(shape)
    return pl.BlockSpec(shape, lambda *a: (0,) * nd)


def _place():
    x, y, c = lax.axis_index("x"), lax.axis_index("y"), lax.axis_index("c")
    others = [(1 - x, y), (x, 1 - y), (1 - x, 1 - y)]
    return x, y, c, 2 * x + y, others


def _remote(src, dst, send_sem, recv_sem, dev):
    return pltpu.make_async_remote_copy(src_ref=src, dst_ref=dst, send_sem=send_sem, recv_sem=recv_sem,
                                        device_id=dev, device_id_type=MESH)


SHARD_STEPS = 4


def _cast_into_slots(me, ws, dtypes):
    n = len(ws)

    def body(me_ref, *refs):
        for w_ref, o_ref, dtype in zip(refs[:n], refs[n:], dtypes):
            o_ref[0] = w_ref[...].astype(dtype)

    return pl.pallas_call(
        body, name="cast_shards",
        grid_spec=pltpu.PrefetchScalarGridSpec(
            num_scalar_prefetch=1, grid=(SHARD_STEPS,),
            in_specs=[pl.BlockSpec((w.shape[0] // SHARD_STEPS, w.shape[1]), lambda r, me_ref: (r, 0)) for w in ws],
            out_specs=[pl.BlockSpec((1, w.shape[0] // SHARD_STEPS, w.shape[1]), lambda r, me_ref: (me_ref[0], r, 0))
                       for w in ws]),
        out_shape=[jax.ShapeDtypeStruct((N_CHIPS,) + w.shape, dt) for w, dt in zip(ws, dtypes)],
        compiler_params=_cp("parallel"),
    )(me, *ws)


class _Gather:
    def __init__(self, bufs, send_sems, recv_sems):
        self.bufs, self.send_sems, self.recv_sems = bufs, send_sems, recv_sems
        self.x, self.y, self.c, self.me, self.others = _place()
        self.halves = [b.shape[1] // 2 for b in bufs]

    def _piece(self, k, chip, half):
        return self.bufs[k].at[chip, pl.ds(half * self.halves[k], self.halves[k]), :]

    def _ici(self, k, j, chip):
        ox, oy = self.others[j]
        piece = self._piece(k, chip, self.c)
        return _remote(piece, piece, self.send_sems.at[6 * k + j], self.recv_sems.at[6 * k + j], (ox, oy, self.c))

    def _relay(self, k):
        first = self.c == 0
        piece = self._piece(k, jnp.where(first, self.chip(0), self.chip(1)), self.c)
        to = (jnp.where(first, self.others[1][0], self.others[0][0]),
              jnp.where(first, self.others[1][1], self.others[0][1]), self.c)
        return _remote(piece, piece, self.send_sems.at[6 * k + 2], self.recv_sems.at[6 * k + 2], to)

    def _pair(self, k, j, half):
        ox, oy = self.others[j]
        piece = self._piece(k, 2 * ox + oy, half)
        return _remote(piece, piece, self.send_sems.at[6 * k + 3 + j], self.recv_sems.at[6 * k + 3 + j],
                       (self.x, self.y, 1 - self.c))

    def _each(self, ks=None):
        return [(k, j) for k in (range(len(self.bufs)) if ks is None else ks) for j in range(3)]

    def chip(self, j):
        ox, oy = self.others[j]
        return 2 * ox + oy

    def start(self):
        for k in range(len(self.bufs)):
            for j in range(2):
                self._ici(k, j, self.me).start()

    def forward(self, pairs=None):
        for k, j in self._each() if pairs is None else pairs:
            self._ici(k, j, self.chip(j)).wait_recv()
            self._pair(k, j, self.c).start()
            if j < 2:
                pl.when(self.c == j)(self._relay(k).start)

    def landed(self, pairs):
        for k, j in pairs:
            self._pair(k, j, 1 - self.c).wait_recv()

    def finish(self, ks=None):
        self.landed(self._each(ks))
        for k in range(len(self.bufs)):
            for j in range(2):
                self._ici(k, j, self.me).wait_send()
            self._relay(k).wait_send()
            for j in range(3):
                self._pair(k, j, self.c).wait_send()

    @staticmethod
    def scratch(n):
        return [pltpu.SemaphoreType.DMA((6 * n,)), pltpu.SemaphoreType.DMA((6 * n,))]


def _carried(bufs):
    n = len(bufs)
    return dict(in_specs=[ANY] * n, out_specs=[ANY] * n,
                out_shape=[jax.ShapeDtypeStruct(b.shape, b.dtype) for b in bufs], scratch=_Gather.scratch(n))


def _mix_in(x, g_mix, order, carry):
    s = x.shape[0]
    tm = min(2 * ROW_TILE, s)
    n = s // tm
    nc = len(carry)
    cs = _carried(carry)

    def body(order_ref, x_ref, g_ref, *refs):
        z_ref = refs[nc]
        bufs = refs[nc + 1:2 * nc + 1]
        h_ref, w_ref, w_sem = refs[2 * nc + 1:2 * nc + 4]
        gather = _Gather(bufs, *refs[2 * nc + 4:])
        q, i = pl.program_id(0), pl.program_id(1)
        first = i == 0
        pl.when(jnp.logical_and(q == 0, first))(gather.start)
        for j in range(3):

            @pl.when(jnp.logical_and(q == j + 1, first))
            def _():
                gather.forward([(0, j)])
                gather.landed([(0, j)])
                if j == 1:
                    gather.forward([(k, jj) for k in range(1, nc) for jj in range(2)])

        @pl.when(first)
        def _():
            load = pltpu.make_async_copy(bufs[0].at[order_ref[q]], w_ref, w_sem)
            load.start()
            load.wait()

        @pl.when(q == 0)
        def _():
            h, _, _ = _rms_fwd(x_ref[...], g_ref[...])
            h_ref[i] = h.astype(bf16)

        z_ref[...] = _dot(h_ref[i], w_ref[...])

        @pl.when(jnp.logical_and(q == N_CHIPS - 1, i == n - 1))
        def _():
            gather.forward([(k, 2) for k in range(1, nc)])
            gather.finish(range(1, nc))

    res = pl.pallas_call(
        body, name="mix_in",
        grid_spec=pltpu.PrefetchScalarGridSpec(
            num_scalar_prefetch=1, grid=(N_CHIPS, n),
            in_specs=[pl.BlockSpec((tm, D_MODEL), lambda q, i, order_ref: (jnp.where(q == 0, i, 0), 0)),
                      pl.BlockSpec((1, D_MODEL), lambda q, i, order_ref: (0, 0))] + cs["in_specs"],
            out_specs=[pl.BlockSpec((tm, W_IN_CHUNK), lambda q, i, order_ref: (i, order_ref[q]))] + cs["out_specs"],
            scratch_shapes=[pltpu.VMEM((n, tm, D_MODEL), bf16), pltpu.VMEM((D_MODEL, W_IN_CHUNK), bf16),
                            pltpu.SemaphoreType.DMA(())] + cs["scratch"]),
        out_shape=[jax.ShapeDtypeStruct((s, W_IN_COLS), f32)] + cs["out_shape"],
        input_output_aliases={3 + k: 1 + k for k in range(nc)},
        compiler_params=_cp("arbitrary", "arbitrary"),
    )(order, x, g_mix, *carry)
    return res[0], res[1:]


def _tap_offsets(lo, hi):
    groups = [[o for o in range(lo, hi + 1) if o % 8 == s] for s in range(8)]
    return [g for g in groups if g]


def _tap_sum(buf, w_ref, row0, cols, tap_of):
    acc = jnp.zeros((CONV_ROWS, cols.stop - cols.start), f32)
    for offs in _tap_offsets(0, CONV_K - 1):
        slab = buf[pl.ds(row0 + offs[0], offs[-1] - offs[0] + CONV_ROWS), cols]
        for o in offs:
            acc = acc + w_ref[pl.ds(tap_of(o), 1), cols] * slab[o - offs[0]:o - offs[0] + CONV_ROWS]
    return acc


def _pool_counts(tm, w, first_row):
    t1 = (lax.broadcasted_iota(jnp.int32, (tm, 1), 0) + first_row + 1).astype(f32)
    return jnp.minimum(t1, float(w))


def _conv_pool_out(z, x, conv_w, conv_b, ln_g, ln_b, pool_w, pool_scale, w_out, carry):
    s = x.shape[0]
    tm = min(ROW_TILE, s)
    n = s // tm
    hb = tm // HALO
    nc = len(carry)
    cs = _carried(carry)

    def body(z_ref, zp_ref, x_ref, cw_ref, cb_ref, lg_ref, lb_ref, pw_ref, ps_ref, wo_ref, *refs):
        x1_ref, mix_ref, u1_ref, pooled_ref = refs[nc:nc + 4]
        ubuf, vbuf = refs[2 * nc + 4:2 * nc + 6]
        gather = _Gather(refs[nc + 4:2 * nc + 4], *refs[2 * nc + 6:])
        i = pl.program_id(0)
        pl.when(i == 0)(gather.start)
        for k in range(nc):
            pl.when(i == min(n // 2 + 2 * k, n - 1))(functools.partial(gather.forward, [(k, 0), (k, 1)]))
        keep = (i > 0).astype(f32)
        zp = zp_ref[...] * keep
        ubuf[0:HALO, :] = zp[:, :C_CONV] * _sigmoid(zp[:, C_CONV:2 * C_CONV])
        vbuf[0:HALO, :] = zp[:, 2 * C_CONV:]
        ubuf[HALO:, :] = z_ref[:, :C_CONV] * _sigmoid(z_ref[:, C_CONV:2 * C_CONV])
        vbuf[HALO:, :] = z_ref[:, 2 * C_CONV:]
        off = HALO - (CONV_K - 1)
        for r0 in range(0, tm, CONV_ROWS):
            for cols in CONV_COLS:
                u1_ref[r0:r0 + CONV_ROWS, cols] = cb_ref[:, cols] + _tap_sum(ubuf, cw_ref, r0 + off, cols, lambda o: o)
        u1 = u1_ref[...]
        mu = jnp.mean(u1, axis=-1, keepdims=True)
        uc = u1 - mu
        rstd = lax.rsqrt(jnp.mean(uc * uc, axis=-1, keepdims=True) + EPS)
        u2 = uc * rstd * lg_ref[...] + lb_ref[...]
        mix_ref[:, :C_CONV] = (u2 * _sigmoid(u2)).astype(bf16)
        for g, w in enumerate(POOL_WINDOWS):
            cols = slice(g * POOL_GROUP, (g + 1) * POOL_GROUP)
            acc = vbuf[pl.ds(HALO, tm), cols]
            vg = acc
            for d in range(1, w):
                acc = acc + vbuf[pl.ds(HALO - d, tm), cols]
            pooled = (acc / _pool_counts(tm, w, i * tm) - vg).astype(bf16)
            pooled_ref[:, cols] = pooled
            mixed = _dot(pooled, pw_ref[g].astype(bf16))
            mix_ref[:, C_CONV + g * POOL_GROUP:C_CONV + (g + 1) * POOL_GROUP] = (mixed * ps_ref[:, cols]).astype(bf16)
        x1_ref[...] = x_ref[...] + _dot(mix_ref[...], wo_ref[...])
        @pl.when(i == n - 1)
        def _():
            gather.forward([(k, 2) for k in range(nc)])
            gather.finish()

    res = pl.pallas_call(
        body, name="conv_pool_out", grid=(n,),
        in_specs=[_row(0, (tm, W_IN_COLS)),
                  pl.BlockSpec((HALO, W_IN_COLS), lambda i: (jnp.maximum(i * hb - 1, 0), 0)),
                  _row(0, (tm, D_MODEL)), _full((HALO, C_CONV)), _full((1, C_CONV)), _full((1, C_CONV)),
                  _full((1, C_CONV)), _full((4, POOL_GROUP, POOL_GROUP)), _full((1, C_POOL)),
                  _full((D_MODEL, D_MODEL))] + cs["in_specs"],
        out_specs=[_row(0, (tm, D_MODEL)), _row(0, (tm, D_MODEL)), _row(0, (tm, C_CONV)), _row(0, (tm, C_POOL))]
        + cs["out_specs"],
        out_shape=[jax.ShapeDtypeStruct((s, D_MODEL), f32), jax.ShapeDtypeStruct((s, D_MODEL), bf16),
                   jax.ShapeDtypeStruct((s, C_CONV), f32), jax.ShapeDtypeStruct((s, C_POOL), bf16)] + cs["out_shape"],
        input_output_aliases={10 + k: 4 + k for k in range(nc)},
        scratch_shapes=[pltpu.VMEM((HALO + tm, C_CONV), f32), pltpu.VMEM((HALO + tm, C_POOL), f32)] + cs["scratch"],
        compiler_params=_cp("arbitrary"),
    )(z, z, x, conv_w, conv_b, ln_g, ln_b, pool_w, pool_scale, w_out, *carry)
    return res[:4], res[4:]


def _ffn_fwd(x1, g_ffn, w_gu_g, w_down, carry):
    s = x1.shape[0]
    tm = min(ROW_TILE, s)
    n = s // tm
    nc = len(carry)
    cs = _carried(carry)

    def body(x1_ref, g_ref, wg_ref, wu_ref, wd_ref, *refs):
        x2_ref, h2_ref, gu_ref, f_ref = refs[nc:nc + 4]
        acc_ref = refs[2 * nc + 4]
        gather = _Gather(refs[nc + 4:2 * nc + 4], *refs[2 * nc + 5:])
        i, c = pl.program_id(0), pl.program_id(1)
        pl.when(jnp.logical_and(i == 0, c == 0))(gather.start)
        direct = [(k, j) for k in range(nc) for j in range(2)]
        pl.when(jnp.logical_and(i == n // 2, c == 0))(functools.partial(gather.forward, direct))
        pl.when(jnp.logical_and(i == n - 1, c == 0))(functools.partial(gather.forward, [(k, 2) for k in range(nc)]))

        @pl.when(c == 0)
        def _():
            h, _, _ = _rms_fwd(x1_ref[...], g_ref[...])
            h2_ref[...] = h.astype(bf16)
            acc_ref[...] = jnp.zeros_like(acc_ref)

        h = h2_ref[...]
        for lo, hi in zip(FF_SUB[:-1], FF_SUB[1:]):
            gate = _dot(h, wg_ref[0, :, lo:hi])
            up = _dot(h, wu_ref[0, :, lo:hi])
            gu_ref[0, :, lo:hi] = gate.astype(bf16)
            gu_ref[1, :, lo:hi] = up.astype(bf16)
            f = (gate * _sigmoid(gate) * up).astype(bf16)
            f_ref[:, lo:hi] = f
            acc_ref[...] += _dot(f, wd_ref[lo:hi, :])

        @pl.when(c == 1)
        def _():
            x2_ref[...] = x1_ref[...] + acc_ref[...]

        pl.when(jnp.logical_and(i == n - 1, c == 1))(gather.finish)

    res = pl.pallas_call(
        body, name="ffn_fwd", grid=(n, 2),
        in_specs=[_row(0, (tm, D_MODEL)), _full((1, D_MODEL)),
                  pl.BlockSpec((1, D_MODEL, FF_CHUNK), lambda i, c: (c, 0, 0)),
                  pl.BlockSpec((1, D_MODEL, FF_CHUNK), lambda i, c: (2 + c, 0, 0)),
                  pl.BlockSpec((FF_CHUNK, D_MODEL), lambda i, c: (c, 0))] + cs["in_specs"],
        out_specs=[_row(0, (tm, D_MODEL)), _row(0, (tm, D_MODEL)),
                   pl.BlockSpec((2, tm, FF_CHUNK), lambda i, c: (0, i, c)),
                   pl.BlockSpec((tm, FF_CHUNK), lambda i, c: (i, c))] + cs["out_specs"],
        out_shape=[jax.ShapeDtypeStruct((s, D_MODEL), f32), jax.ShapeDtypeStruct((s, D_MODEL), bf16),
                   jax.ShapeDtypeStruct((2, s, D_FF), bf16), jax.ShapeDtypeStruct((s, D_FF), bf16)] + cs["out_shape"],
        input_output_aliases={5 + k: 4 + k for k in range(nc)},
        scratch_shapes=[pltpu.VMEM((tm, D_MODEL), f32)] + cs["scratch"],
        compiler_params=_cp("arbitrary", "arbitrary"),
    )(x1, g_ffn, w_gu_g, w_gu_g, w_down, *carry)
    return res[:4], res[4:]


def _ple_loss(x2, p, target, g_pg, g_post, g_final, w_pg, w_pu_g):
    s = x2.shape[0]
    tm = min(ROW_TILE, s)
    n = s // tm

    def body(x2_ref, p_ref, t_ref, gpg_ref, gpo_ref, gf_ref, wpg_ref, wpu_ref,
             dx2_ref, dwpg_ref, dwpu_ref, small_ref, apg_ref, apu_ref):
        i = pl.program_id(0)

        @pl.when(i == 0)
        def _():
            apg_ref[...] = jnp.zeros_like(apg_ref)
            apu_ref[...] = jnp.zeros_like(apu_ref)
            small_ref[...] = jnp.zeros_like(small_ref)

        x2 = x2_ref[...]
        h3, x2h, r2 = _rms_fwd(x2, gpg_ref[...])
        h3b = h3.astype(bf16)
        gate = _sigmoid(_dot(h3b, wpg_ref[...]))
        pb = p_ref[...].astype(bf16)
        pe = jnp.concatenate([_dot(pb, wpu_ref[j]) for j in range(N_CHIPS)], axis=-1)
        e, peh, rp = _rms_fwd(pe, gpo_ref[...])
        x3 = x2 + gate * e
        y, x3h, r3 = _rms_fwd(x3, gf_ref[...])
        d = y - t_ref[...]
        loss = 0.5 * jnp.sum(jnp.sum(d * d, axis=-1, keepdims=True) * (1.0 / D_MODEL), axis=0, keepdims=True)
        dx3, dgf = _rms_bwd(d * (1.0 / D_MODEL), x3h, r3, gf_ref[...])
        dpe, dgpo = _rms_bwd(dx3 * gate, peh, rp, gpo_ref[...])
        dgl = (dx3 * e * gate * (1.0 - gate)).astype(bf16)
        apg_ref[...] += _dot_tn(h3b, dgl)
        apu_ref[...] += _dot_tn(pb, dpe.astype(bf16))
        dh3 = _dot_nt(dgl, wpg_ref[...])
        dx2b, dgpg = _rms_bwd(dh3, x2h, r2, gpg_ref[...])
        dx2_ref[...] = dx3 + dx2b
        small_ref[0:1, :] += dgpg
        small_ref[1:2, :] += dgpo
        small_ref[2:3, :] += dgf
        small_ref[3:4, :] += jnp.broadcast_to(loss, (1, D_MODEL))

        @pl.when(i == n - 1)
        def _():
            dwpg_ref[...] = apg_ref[...].astype(bf16)
            for j in range(N_CHIPS):
                dwpu_ref[j] = apu_ref[:, j * PLE_CHUNK:(j + 1) * PLE_CHUNK].astype(bf16)

    return pl.pallas_call(
        body, name="ple_loss", grid=(n,),
        in_specs=[_row(0, (tm, D_MODEL)), _row(0, (tm, D_PLE)), _row(0, (tm, D_MODEL)),
                  _full((1, D_MODEL)), _full((1, D_MODEL)), _full((1, D_MODEL)),
                  _full((D_MODEL, D_MODEL)), _full((N_CHIPS, D_PLE, PLE_CHUNK))],
        out_specs=[_row(0, (tm, D_MODEL)), _full((D_MODEL, D_MODEL)), _full((N_CHIPS, D_PLE, PLE_CHUNK)),
                   _full((8, D_MODEL))],
        out_shape=[jax.ShapeDtypeStruct((s, D_MODEL), f32), jax.ShapeDtypeStruct((D_MODEL, D_MODEL), bf16),
                   jax.ShapeDtypeStruct((N_CHIPS, D_PLE, PLE_CHUNK), bf16), jax.ShapeDtypeStruct((8, D_MODEL), f32)],
        scratch_shapes=[pltpu.VMEM((D_MODEL, D_MODEL), f32), pltpu.VMEM((D_PLE, D_MODEL), f32)],
        compiler_params=_cp("arbitrary"),
    )(x2, p, target, g_pg, g_post, g_final, w_pg, w_pu_g)


def _crossed(parts):
    n = len(parts)
    return dict(in_specs=[ANY] * n, out_specs=[ANY] * n,
                out_shape=[jax.ShapeDtypeStruct(a.shape, a.dtype) for a in parts], scratch=_Cross.scratch(n))


def _ffn_bwd_dx(dx2, x1, gu, g_ffn, w_gu_g, w_down, parts):
    s = x1.shape[0]
    tm = min(ROW_TILE, s)
    n = s // tm
    nc = len(parts)
    cs = _crossed(parts)

    def body(dx2_ref, x1_ref, gu_ref, g_ref, wg_ref, wu_ref, wd_ref, *refs):
        dx1_ref, dgu_ref, small_ref = refs[nc:nc + 3]
        acc_ref = refs[2 * nc + 3]
        cross = _Cross(refs[:nc], refs[nc + 3:2 * nc + 3], *refs[2 * nc + 4:])
        i, c = pl.program_id(0), pl.program_id(1)
        pl.when(jnp.logical_and(i == 0, c == 0))(cross.start)

        @pl.when(jnp.logical_and(i == 0, c == 0))
        def _():
            small_ref[...] = jnp.zeros_like(small_ref)

        @pl.when(c == 0)
        def _():
            acc_ref[...] = jnp.zeros_like(acc_ref)

        dyb = dx2_ref[...].astype(bf16)
        for lo, hi in zip(FF_SUB[:-1], FF_SUB[1:]):
            df = _dot_nt(dyb, wd_ref[lo:hi, :])
            gate = gu_ref[0, :, lo:hi].astype(f32)
            up = gu_ref[1, :, lo:hi].astype(f32)
            sg = _sigmoid(gate)
            dgate = (df * up * _silu_grad(gate, sg)).astype(bf16)
            dup = (df * gate * sg).astype(bf16)
            dgu_ref[0, :, lo:hi] = dgate
            dgu_ref[1, :, lo:hi] = dup
            acc_ref[...] += _dot_nt(dgate, wg_ref[0, :, lo:hi]) + _dot_nt(dup, wu_ref[0, :, lo:hi])

        @pl.when(c == 1)
        def _():
            _, x1h, r1 = _rms_fwd(x1_ref[...], g_ref[...])
            dx1b, dg = _rms_bwd(acc_ref[...], x1h, r1, g_ref[...])
            dx1_ref[...] = dx2_ref[...] + dx1b
            small_ref[0:1, :] += dg

        pl.when(jnp.logical_and(i == n - 1, c == 1))(cross.finish)

    res = pl.pallas_call(
        body, name="ffn_bwd_dx", grid=(n, 2),
        in_specs=[_row(0, (tm, D_MODEL)), _row(0, (tm, D_MODEL)),
                  pl.BlockSpec((2, tm, FF_CHUNK), lambda i, c: (0, i, c)), _full((1, D_MODEL)),
                  pl.BlockSpec((1, D_MODEL, FF_CHUNK), lambda i, c: (c, 0, 0)),
                  pl.BlockSpec((1, D_MODEL, FF_CHUNK), lambda i, c: (2 + c, 0, 0)),
                  pl.BlockSpec((FF_CHUNK, D_MODEL), lambda i, c: (c, 0))] + cs["in_specs"],
        out_specs=[_row(0, (tm, D_MODEL)), pl.BlockSpec((2, tm, FF_CHUNK), lambda i, c: (0, i, c)),
                   _full((8, D_MODEL))] + cs["out_specs"],
        out_shape=[jax.ShapeDtypeStruct((s, D_MODEL), f32), jax.ShapeDtypeStruct((2, s, D_FF), bf16),
                   jax.ShapeDtypeStruct((8, D_MODEL), f32)] + cs["out_shape"],
        scratch_shapes=[pltpu.VMEM((tm, D_MODEL), f32)] + cs["scratch"],
        compiler_params=_cp("arbitrary", "arbitrary"),
    )(dx2, x1, gu, g_ffn, w_gu_g, w_gu_g, w_down, *parts)
    return res[:3], res[3:]


def _ffn_bwd_dw_gu(h2, dgu):
    s = h2.shape[0]
    ts = min(2 * ROW_TILE, s)
    n = s // ts

    def body(h_ref, d_ref, o_ref, acc_ref):
        t = pl.program_id(1)

        @pl.when(t == 0)
        def _():
            acc_ref[...] = jnp.zeros_like(acc_ref)

        acc_ref[...] += _dot_tn(h_ref[...], d_ref[0])

        @pl.when(t == n - 1)
        def _():
            o_ref[0] = acc_ref[...].astype(bf16)

    return pl.pallas_call(
        body, name="ffn_bwd_dw_gu", grid=(N_CHIPS, n),
        in_specs=[pl.BlockSpec((ts, D_MODEL), lambda j, t: (t, 0)),
                  pl.BlockSpec((1, ts, FF_CHUNK), lambda j, t: (j // 2, t, j % 2))],
        out_specs=pl.BlockSpec((1, D_MODEL, FF_CHUNK), lambda j, t: (j, 0, 0)),
        out_shape=jax.ShapeDtypeStruct((N_CHIPS, D_MODEL, FF_CHUNK), bf16),
        scratch_shapes=[pltpu.VMEM((D_MODEL, FF_CHUNK), f32)],
        compiler_params=_cp("parallel", "arbitrary"),
    )(h2, dgu)


def _ffn_bwd_dw_down(f, dx2):
    s = dx2.shape[0]
    ts = min(2 * ROW_TILE, s)
    n = s // ts

    def body(f_ref, d_ref, o_ref, acc_ref):
        t = pl.program_id(1)

        @pl.when(t == 0)
        def _():
            acc_ref[...] = jnp.zeros_like(acc_ref)

        acc_ref[...] += _dot_tn(f_ref[...], d_ref[...].astype(bf16))

        @pl.when(t == n - 1)
        def _():
            o_ref[...] = acc_ref[...].astype(bf16)

    return pl.pallas_call(
        body, name="ffn_bwd_dw_down", grid=(2, n),
        in_specs=[pl.BlockSpec((ts, FF_CHUNK), lambda c, t: (t, c)),
                  pl.BlockSpec((ts, D_MODEL), lambda c, t: (t, 0))],
        out_specs=pl.BlockSpec((FF_CHUNK, D_MODEL), lambda c, t: (c, 0)),
        out_shape=jax.ShapeDtypeStruct((D_FF, D_MODEL), bf16),
        scratch_shapes=[pltpu.VMEM((FF_CHUNK, D_MODEL), f32)],
        compiler_params=_cp("parallel", "arbitrary"),
    )(f, dx2)


def _mix_bwd_local(dx1, mix, u1, pooled, w_out, ln_g, ln_b, pool_w, pool_scale):
    s = dx1.shape[0]
    tm = min(ROW_TILE, s)
    n = s // tm

    def body(dx1_ref, mix_ref, u1_ref, po_ref, wo_ref, lg_ref, lb_ref, pw_ref, ps_ref,
             du1_ref, dpo_ref, dwo_ref, dpw_ref, small_ref, awo_ref):
        i = pl.program_id(0)

        @pl.when(i == 0)
        def _():
            awo_ref[...] = jnp.zeros_like(awo_ref)
            dpw_ref[...] = jnp.zeros_like(dpw_ref)
            small_ref[...] = jnp.zeros_like(small_ref)

        dyb = dx1_ref[...].astype(bf16)
        dmix = _dot_nt(dyb, wo_ref[...])
        awo_ref[...] += _dot_tn(mix_ref[...], dyb)
        u1 = u1_ref[...]
        mu = jnp.mean(u1, axis=-1, keepdims=True)
        uc = u1 - mu
        rstd = lax.rsqrt(jnp.mean(uc * uc, axis=-1, keepdims=True) + EPS)
        uh = uc * rstd
        u2 = uh * lg_ref[...] + lb_ref[...]
        du2 = dmix[:, :C_CONV] * _silu_grad(u2, _sigmoid(u2))
        duh = du2 * lg_ref[...]
        du1 = rstd * (duh - jnp.mean(duh, axis=-1, keepdims=True) - uh * jnp.mean(duh * uh, axis=-1, keepdims=True))
        du1_ref[...] = du1
        small_ref[0:1, :] += jnp.sum(du2 * uh, axis=0, keepdims=True)
        small_ref[1:2, :] += jnp.sum(du2, axis=0, keepdims=True)
        small_ref[2:3, :] += jnp.sum(du1, axis=0, keepdims=True)
        for g in range(len(POOL_WINDOWS)):
            cols = slice(g * POOL_GROUP, (g + 1) * POOL_GROUP)
            dq = dmix[:, C_CONV + g * POOL_GROUP:C_CONV + (g + 1) * POOL_GROUP]
            pwb = pw_ref[g].astype(bf16)
            pg = po_ref[:, cols]
            mixed = _dot(pg, pwb)
            small_ref[3:4, cols] += jnp.sum(dq * mixed, axis=0, keepdims=True)
            dmixed = (dq * ps_ref[:, cols]).astype(bf16)
            dpw_ref[g] += _dot_tn(pg, dmixed)
            dpo_ref[:, cols] = _dot_nt(dmixed, pwb)

        @pl.when(i == n - 1)
        def _():
            dwo_ref[...] = awo_ref[...].astype(bf16)

    return pl.pallas_call(
        body, name="mix_bwd_local", grid=(n,),
        in_specs=[_row(0, (tm, D_MODEL)), _row(0, (tm, D_MODEL)), _row(0, (tm, C_CONV)), _row(0, (tm, C_POOL)),
                  _full((D_MODEL, D_MODEL)), _full((1, C_CONV)), _full((1, C_CONV)),
                  _full((4, POOL_GROUP, POOL_GROUP)), _full((1, C_POOL))],
        out_specs=[_row(0, (tm, C_CONV)), _row(0, (tm, C_POOL)), _full((D_MODEL, D_MODEL)),
                   _full((4, POOL_GROUP, POOL_GROUP)), _full((8, C_CONV))],
        out_shape=[jax.ShapeDtypeStruct((s, C_CONV), f32), jax.ShapeDtypeStruct((s, C_POOL), f32),
                   jax.ShapeDtypeStruct((D_MODEL, D_MODEL), bf16),
                   jax.ShapeDtypeStruct((4, POOL_GROUP, POOL_GROUP), f32), jax.ShapeDtypeStruct((8, C_CONV), f32)],
        scratch_shapes=[pltpu.VMEM((D_MODEL, D_MODEL), f32)],
        compiler_params=_cp("arbitrary"),
    )(dx1, mix, u1, pooled, w_out, ln_g, ln_b, pool_w, pool_scale)


def _in_bwd(du1, dpo, z, x, dx1, conv_w, g_mix, w_in_g, parts, small_parts):
    s = x.shape[0]
    tm = min(ROW_TILE, s)
    n = s // tm
    hb = tm // HALO
    last = s // HALO - 1
    nv = tm // 8
    assert nv >= SHIFT_PAD and nv % SHIFT_GROUPS == 0
    nc, ns = len(parts), len(small_parts)
    cs = _crossed(parts)

    def body(du_ref, dun_ref, dp_ref, dpn_ref, z_ref, zp_ref, x_ref, dx1_ref, cw_ref, g_ref, w_ref, *refs):
        outs = refs[nc + ns:]
        gx_ref, dw_ref, dcw_ref, small_ref = outs[:4]
        eu, ed, ep, ss, u0_ref, dz_ref, acc_ref, dcw_acc = outs[4 + nc + ns:12 + nc + ns]
        sems = outs[12 + nc + ns:]
        cross = _Cross(refs[:nc], outs[4:4 + nc], sems[0], sems[1])
        gather = _SmallGather(refs[nc:nc + ns], outs[4 + nc:4 + nc + ns], sems[2], sems[3])
        i = pl.program_id(0)
        pl.when(i == 0)(cross.start)
        pl.when(i == 0)(gather.start)

        @pl.when(i == 0)
        def _():
            acc_ref[...] = jnp.zeros_like(acc_ref)
            dcw_acc[...] = jnp.zeros_like(dcw_acc)
            small_ref[...] = jnp.zeros_like(small_ref)

        keep_prev = (i > 0).astype(f32)
        keep_next = (i < n - 1).astype(f32)
        zp = zp_ref[...] * keep_prev
        u0_prev = zp[:, :C_CONV] * _sigmoid(zp[:, C_CONV:2 * C_CONV])
        u0_ref[...] = z_ref[:, :C_CONV] * _sigmoid(z_ref[:, C_CONV:2 * C_CONV])
        du_next = dun_ref[...] * keep_next
        for c, w_pool in enumerate(POOL_WINDOWS):
            lanes = slice(c * 128, (c + 1) * 128)
            eu[c, pl.ds(0, SHIFT_PAD, stride=8), :] = u0_prev[:, lanes]
            ed[c, pl.ds(nv * 8 + 7, SHIFT_PAD, stride=8), :] = du_next[:, lanes]
            ep[c, pl.ds(nv * 8 + 7, SHIFT_PAD, stride=8), :] = (
                dpn_ref[:, lanes] * keep_next / _pool_counts(HALO, w_pool, (i + 1) * tm))
            for j in range(8):
                rows = slice(j * nv, (j + 1) * nv)
                eu[c, pl.ds(SHIFT_PAD * 8 + j, nv, stride=8), :] = u0_ref[rows, lanes]
                ed[c, pl.ds(j, nv, stride=8), :] = du_ref[rows, lanes]
                ep[c, pl.ds(j, nv, stride=8), :] = dp_ref[rows, lanes] / _pool_counts(nv, w_pool, i * tm + j * nv)
                if j >= 1:
                    eu[c, pl.ds(j, SHIFT_PAD, stride=8), :] = u0_ref[j * nv - SHIFT_PAD:j * nv, lanes]
                if j <= 6:
                    edge = slice((j + 1) * nv, (j + 1) * nv + SHIFT_PAD)
                    ed[c, pl.ds(nv * 8 + j, SHIFT_PAD, stride=8), :] = du_ref[edge, lanes]
                    ep[c, pl.ds(nv * 8 + j, SHIFT_PAD, stride=8), :] = (
                        dp_ref[edge, lanes] / _pool_counts(SHIFT_PAD, w_pool, i * tm + (j + 1) * nv))
        for c, w_pool in enumerate(POOL_WINDOWS):
            lanes = slice(c * 128, (c + 1) * 128)
            b_lanes = slice(C_CONV + c * 128, C_CONV + (c + 1) * 128)
            v_lanes = slice(2 * C_CONV + c * 128, 2 * C_CONV + (c + 1) * 128)
            for v0 in range(0, nv, SHIFT_GROUPS):
                span = SHIFT_GROUPS * 8
                acc = jnp.zeros((span, 128), f32)
                for k in range(CONV_K):
                    acc = acc + cw_ref[pl.ds(k, 1), lanes] * ed[c, pl.ds((v0 + CONV_K - 1 - k) * 8, span), :]
                ss[0, v0 * 8:v0 * 8 + span, :] = acc
                acc = ep[c, pl.ds(v0 * 8, span), :]
                for d in range(1, w_pool):
                    acc = acc + ep[c, pl.ds((v0 + d) * 8, span), :]
                ss[1, v0 * 8:v0 * 8 + span, :] = acc
                d1 = ed[c, pl.ds(v0 * 8, span), :]
                for k in range(CONV_K):
                    prod = d1 * eu[c, pl.ds((SHIFT_PAD - (CONV_K - 1) + v0 + k) * 8, span), :]
                    fold = prod[0:8]
                    for r in range(8, span, 8):
                        fold = fold + prod[r:r + 8]
                    dcw_acc[k, :, lanes] += fold
            for j in range(8):
                rows = slice(j * nv, (j + 1) * nv)
                du0 = ss[0, pl.ds(j, nv, stride=8), :]
                av, sv = z_ref[rows, lanes], _sigmoid(z_ref[rows, b_lanes])
                dz_ref[rows, lanes] = (du0 * sv).astype(bf16)
                dz_ref[rows, b_lanes] = (du0 * av * sv * (1.0 - sv)).astype(bf16)
                dz_ref[rows, v_lanes] = (ss[1, pl.ds(j, nv, stride=8), :] - dp_ref[rows, lanes]).astype(bf16)
        h, xh, r = _rms_fwd(x_ref[...], g_ref[...])
        dz = dz_ref[...]
        acc_ref[...] += _dot_tn(h.astype(bf16), dz)
        dh = _dot_nt(dz[:, 0:W_IN_CHUNK], w_ref[0])
        for j in range(1, N_CHIPS):
            dh = dh + _dot_nt(dz[:, j * W_IN_CHUNK:(j + 1) * W_IN_CHUNK], w_ref[j])
        dxb, dg = _rms_bwd(dh, xh, r, g_ref[...])
        gx_ref[...] = dx1_ref[...] + dxb
        small_ref[0:1, :] += dg

        @pl.when(i == n - 1)
        def _():
            for j in range(N_CHIPS):
                dw_ref[j] = acc_ref[:, j * W_IN_CHUNK:(j + 1) * W_IN_CHUNK].astype(bf16)
            dcw_ref[...] = jnp.sum(dcw_acc[...], axis=1)

        pl.when(i == n - 1)(gather.finish)
        pl.when(i == n - 1)(cross.finish)

    nxt = lambda i: (jnp.minimum((i + 1) * hb, last), 0)
    res = pl.pallas_call(
        body, name="in_bwd", grid=(n,),
        in_specs=[_row(0, (tm, C_CONV)), pl.BlockSpec((HALO, C_CONV), nxt),
                  _row(0, (tm, C_POOL)), pl.BlockSpec((HALO, C_POOL), nxt),
                  _row(0, (tm, W_IN_COLS)),
                  pl.BlockSpec((HALO, W_IN_COLS), lambda i: (jnp.maximum(i * hb - 1, 0), 0)),
                  _row(0, (tm, D_MODEL)), _row(0, (tm, D_MODEL)), _full((HALO, C_CONV)), _full((1, D_MODEL)),
                  _full((N_CHIPS, D_MODEL, W_IN_CHUNK))] + cs["in_specs"] + [ANY] * ns,
        out_specs=[_row(0, (tm, D_MODEL)), _full((N_CHIPS, D_MODEL, W_IN_CHUNK)), _full((HALO, C_CONV)),
                   _full((8, D_MODEL))] + cs["out_specs"] + [ANY] * ns,
        out_shape=[jax.ShapeDtypeStruct((s, D_MODEL), f32), jax.ShapeDtypeStruct((N_CHIPS, D_MODEL, W_IN_CHUNK), bf16),
                   jax.ShapeDtypeStruct((HALO, C_CONV), f32), jax.ShapeDtypeStruct((8, D_MODEL), f32)] + cs["out_shape"]
        + _gathered_shapes(small_parts),
        scratch_shapes=[pltpu.VMEM((4, (SHIFT_PAD + nv) * 8, 128), f32), pltpu.VMEM((4, (nv + SHIFT_PAD) * 8, 128), f32),
                        pltpu.VMEM((4, (nv + SHIFT_PAD) * 8, 128), f32), pltpu.VMEM((2, tm, 128), f32),
                        pltpu.VMEM((tm, C_CONV), f32), pltpu.VMEM((tm, W_IN_COLS), bf16),
                        pltpu.VMEM((D_MODEL, W_IN_COLS), f32), pltpu.VMEM((HALO, 8, C_CONV), f32)] + cs["scratch"]
        + _SmallGather.scratch(ns),
        compiler_params=_cp("arbitrary"),
    )(du1, du1, dpo, dpo, z, z, x, dx1, conv_w, g_mix, w_in_g, *parts, *small_parts)
    return res[:4], res[4:4 + nc], res[4 + nc:]


def _pair_reduce(name, grads):
    nk = len(grads)
    halves = [g.shape[1] // 2 for g in grads]

    def body(*refs):
        ins, outs, got = refs[:nk], refs[nk:2 * nk], refs[2 * nk:3 * nk]
        send_sems, recv_sems = refs[3 * nk:]
        x, y, c, _, _ = _place()

        def half(k, core):
            return pl.ds(pl.multiple_of(core * halves[k], 16), halves[k])

        cps = [_remote(ins[k].at[:, half(k, 1 - c), :], got[k], send_sems.at[k], recv_sems.at[k], (x, y, 1 - c))
               for k in range(nk)]
        for cp in cps:
            cp.start()
        for k, cp in enumerate(cps):
            cp.wait_recv()
            outs[k][...] = (ins[k][:, half(k, c), :].astype(f32) + got[k][...].astype(f32)).astype(bf16)
        for cp in cps:
            cp.wait_send()

    shapes = [(N_CHIPS, h, g.shape[2]) for g, h in zip(grads, halves)]
    return pl.pallas_call(
        body, name=f"pair_reduce_{name}", in_specs=[VMEM] * nk, out_specs=[VMEM] * nk,
        out_shape=[jax.ShapeDtypeStruct(s, bf16) for s in shapes],
        scratch_shapes=[pltpu.VMEM(s, bf16) for s in shapes]
        + [pltpu.SemaphoreType.DMA((nk,)), pltpu.SemaphoreType.DMA((nk,))],
        compiler_params=pltpu.CompilerParams(vmem_limit_bytes=VMEM_LIMIT),
    )(*grads)


class _Cross:
    def __init__(self, parts, landed, send_sems, recv_sems):
        self.parts, self.landed, self.send_sems, self.recv_sems = parts, landed, send_sems, recv_sems
        _, _, self.c, self.me, self.others = _place()

    def _copy(self, k, j, src_chunk, dst_slot):
        ox, oy = self.others[j]
        return _remote(self.parts[k].at[src_chunk], self.landed[k].at[dst_slot], self.send_sems.at[3 * k + j],
                       self.recv_sems.at[3 * k + j], (ox, oy, self.c))

    def _each(self):
        return [(k, j, 2 * self.others[j][0] + self.others[j][1]) for k in range(len(self.parts)) for j in range(3)]

    def start(self):
        for k, j, chip in self._each():
            self._copy(k, j, chip, self.me).start()

    def finish(self):
        for k, j, chip in self._each():
            self._copy(k, j, chip, chip).wait_recv()
        for k, j, chip in self._each():
            self._copy(k, j, chip, self.me).wait_send()

    @staticmethod
    def scratch(n):
        return [pltpu.SemaphoreType.DMA((3 * n,)), pltpu.SemaphoreType.DMA((3 * n,))]


def _rs_sum_chips(name, place, landed, part):
    _, half, cols = landed.shape

    def body(place_ref, l_ref, p_ref, o_ref):
        me = place_ref[0]
        own = p_ref[0].astype(f32)
        acc = jnp.where(me == 0, own, l_ref[0].astype(f32))
        for j in range(1, N_CHIPS):
            acc = acc + jnp.where(me == j, own, l_ref[j].astype(f32))
        o_ref[...] = acc

    return pl.pallas_call(
        body, name=f"rs_sum_chips_{name}",
        grid_spec=pltpu.PrefetchScalarGridSpec(
            num_scalar_prefetch=1, grid=(1,),
            in_specs=[pl.BlockSpec((N_CHIPS, half, cols), lambda t, place_ref: (0, 0, 0)),
                      pl.BlockSpec((1, half, cols), lambda t, place_ref: (place_ref[0], 0, 0))],
            out_specs=pl.BlockSpec((half, cols), lambda t, place_ref: (place_ref[1], 0))),
        out_shape=jax.ShapeDtypeStruct((2 * half, cols), f32),
        compiler_params=_cp("arbitrary"),
    )(place, landed, part)


def _grad_pair(shards):
    nk = len(shards)

    def body(*refs):
        outs = refs[nk:2 * nk]
        send_sems, recv_sems = refs[2 * nk:]
        x, y, c, _, _ = _place()

        def half(k, core):
            h = outs[k].shape[0] // 2
            return outs[k].at[pl.ds(core * h, h), :]

        cps = [_remote(half(k, c), half(k, c), send_sems.at[k], recv_sems.at[k], (x, y, 1 - c)) for k in range(nk)]
        for cp in cps:
            cp.start()
        for k in range(nk):
            _remote(half(k, 1 - c), half(k, 1 - c), send_sems.at[k], recv_sems.at[k], (x, y, 1 - c)).wait_recv()
        for cp in cps:
            cp.wait_send()

    return pl.pallas_call(
        body, name="grad_pair", in_specs=[ANY] * nk, out_specs=[ANY] * nk,
        out_shape=[jax.ShapeDtypeStruct(a.shape, f32) for a in shards],
        input_output_aliases={k: k for k in range(nk)},
        scratch_shapes=[pltpu.SemaphoreType.DMA((nk,)), pltpu.SemaphoreType.DMA((nk,))],
    )(*shards)


def _adam_math(w, g, m, v):
    m = ADAM_B1 * m + (1.0 - ADAM_B1) * g
    v = ADAM_B2 * v + (1.0 - ADAM_B2) * (g * g)
    m_hat = m / (1.0 - ADAM_B1 ** ADAM_STEP)
    v_hat = v / (1.0 - ADAM_B2 ** ADAM_STEP)
    delta = -ADAM_LR * (m_hat / (jnp.sqrt(v_hat) + ADAM_EPS) + ADAM_WD * w)
    return delta, m, v


def _adam(ws, gs, ms, vs):
    n = len(ws)
    steps = 2 * SHARD_STEPS

    def body(*refs):
        for k in range(n):
            w_ref, g_ref, m_ref, v_ref = (refs[j * n + k] for j in range(4))
            d_ref, nm_ref, nv_ref = (refs[(4 + j) * n + k] for j in range(3))
            d_ref[...], nm_ref[...], nv_ref[...] = _adam_math(w_ref[...], g_ref[...], m_ref[...], v_ref[...])

    specs = [_row(0, (w.shape[0] // steps, w.shape[1])) for w in ws]
    res = pl.pallas_call(
        body, name="adam_shards", grid=(steps,), in_specs=specs * 4, out_specs=specs * 3,
        out_shape=[jax.ShapeDtypeStruct(w.shape, f32) for w in ws] * 3,
        compiler_params=_cp("parallel"),
    )(*ws, *gs, *ms, *vs)
    return [(res[k], res[n + k], res[2 * n + k]) for k in range(n)]


class _SmallGather:
    def __init__(self, parts, bufs, send_sems, recv_sems):
        self.parts, self.bufs, self.send_sems, self.recv_sems = parts, bufs, send_sems, recv_sems
        self.x, self.y, self.c, _, self.others = _place()
        self.sibling = (self.x, self.y, 1 - self.c)

    def _copy(self, a, k, block, to, src=None):
        slot = self.bufs[a].at[4 * block[0] + 2 * block[1] + block[2]]
        return _remote(slot if src is None else src, slot, self.send_sems.at[7 * a + k], self.recv_sems.at[7 * a + k],
                       to)

    def _first(self, a):
        me = (self.x, self.y, self.c)
        return [self._copy(a, 0, me, self.sibling, src=self.parts[a])] + [
            self._copy(a, 1 + j, me, (*chip, self.c), src=self.parts[a]) for j, chip in enumerate(self.others)]

    def _passed(self, a):
        return [self._copy(a, 4 + j, (*chip, self.c), self.sibling) for j, chip in enumerate(self.others)]

    @staticmethod
    def scratch(n):
        return [pltpu.SemaphoreType.DMA((7 * n,)), pltpu.SemaphoreType.DMA((7 * n,))]

    def start(self):
        for a in range(len(self.parts)):
            for cp in self._first(a):
                cp.start()

    def finish(self):
        sent = []
        for a in range(len(self.parts)):
            passed = self._passed(a)
            for j, chip in enumerate(self.others):
                self._copy(a, 1 + j, (*chip, self.c), self.sibling).wait_recv()
                passed[j].start()
            sent += self._first(a) + passed
        for a in range(len(self.parts)):
            self._copy(a, 0, self.sibling, self.sibling).wait_recv()
            for j, chip in enumerate(self.others):
                self._copy(a, 4 + j, (*chip, 1 - self.c), self.sibling).wait_recv()
        for cp in sent:
            cp.wait_send()


def _gathered_shapes(parts):
    return [jax.ShapeDtypeStruct((N_DEV,) + p.shape, p.dtype) for p in parts]


def _rs_sum_group(name, place, landed, parts, cross_parts, small_parts):
    nk, nx, ns = len(landed), len(cross_parts), len(small_parts)
    dims = [a.shape[1:] for a in landed]

    def body(place_ref, *refs):
        l_refs, p_refs = refs[:nk], refs[nk:2 * nk]
        x_refs, sp_refs = refs[2 * nk:2 * nk + nx], refs[2 * nk + nx:2 * nk + nx + ns]
        outs = refs[2 * nk + nx + ns:]
        o_refs, xl_refs, sbufs = outs[:nk], outs[nk:nk + nx], outs[nk + nx:nk + nx + ns]
        sems = outs[nk + nx + ns:]
        cross = _Cross(x_refs, xl_refs, sems[0], sems[1])
        small = _SmallGather(sp_refs, sbufs, sems[2], sems[3])
        t = pl.program_id(0)

        @pl.when(t == 0)
        def _():
            cross.start()
            small.start()

        me = place_ref[0]
        for l_ref, p_ref, o_ref in zip(l_refs, p_refs, o_refs):
            own = p_ref[0].astype(f32)
            acc = jnp.where(me == 0, own, l_ref[0].astype(f32))
            for j in range(1, N_CHIPS):
                acc = acc + jnp.where(me == j, own, l_ref[j].astype(f32))
            o_ref[...] = acc

        @pl.when(t == 1)
        def _():
            small.finish()
            cross.finish()

    def halves(h, c, lead, index):
        return pl.BlockSpec((lead, h // 2, c) if lead else (h // 2, c), index)

    in_specs = [halves(h, c, N_CHIPS, lambda t, pr: (0, t, 0)) for h, c in dims]
    in_specs += [halves(h, c, 1, lambda t, pr: (pr[0], t, 0)) for h, c in dims]
    in_specs += [ANY] * (nx + ns)
    out_specs = [halves(h, c, 0, lambda t, pr: (2 * pr[1] + t, 0)) for h, c in dims] + [ANY] * (nx + ns)
    out_shape = [jax.ShapeDtypeStruct((2 * h, c), f32) for h, c in dims]
    out_shape += [jax.ShapeDtypeStruct(a.shape, a.dtype) for a in cross_parts] + _gathered_shapes(small_parts)
    res = pl.pallas_call(
        body, name=f"rs_sum_{name}",
        grid_spec=pltpu.PrefetchScalarGridSpec(
            num_scalar_prefetch=1, grid=(2,), in_specs=in_specs, out_specs=out_specs,
            scratch_shapes=_Cross.scratch(nx) + _SmallGather.scratch(ns)),
        out_shape=out_shape, compiler_params=_cp("arbitrary"),
    )(place, *landed, *parts, *cross_parts, *small_parts)
    return res[:nk], res[nk:nk + nx], res[nk + nx:]


SMALL_PARAMS = ("g_ple_gate", "g_ple_post", "g_final", "g_ffn", "ln_g", "ln_b", "conv_b", "pool_scale", "pool_w",
                "conv_w", "g_mix")
SMALL_ROWS = {"g_ple_gate": (0, 0), "g_ple_post": (0, 1), "g_final": (0, 2), "g_ffn": (1, 0), "ln_g": (2, 0),
              "ln_b": (2, 1), "conv_b": (2, 2), "pool_scale": (2, 3), "g_mix": (5, 0)}
LOSS_ROW = (0, 3)


def _small_adam(place, gathered, parts, params):
    nb, names = len(parts), SMALL_PARAMS
    flat = [a for nm in names for a in params[nm]]

    def body(place_ref, *refs):
        b_refs, p_refs = refs[:nb], refs[nb:2 * nb]
        w_refs = refs[2 * nb:2 * nb + 3 * len(names)]
        outs = refs[2 * nb + 3 * len(names):]
        loss_ref, o_refs, cw_sum = outs[0], outs[1:1 + 4 * len(names)], outs[1 + 4 * len(names)]
        chip = place_ref[0]
        me = 2 * chip + place_ref[1]

        def total(blk, idx):
            own = p_refs[blk][idx]
            g = jnp.where(me == 0, own, b_refs[blk][(0,) + idx])
            for d in range(1, N_DEV):
                g = g + jnp.where(me == d, own, b_refs[blk][(d,) + idx])
            return g

        everything = (slice(None), slice(None))
        loss_ref[...] = total(LOSS_ROW[0], (pl.ds(LOSS_ROW[1], 1), pl.ds(0, 128)))
        d_cw = total(4, everything)
        mine = jnp.where(chip == 0, d_cw[:, 0:128], 0.0)
        for j in range(1, N_CHIPS):
            mine = mine + jnp.where(chip == j, d_cw[:, j * 128:(j + 1) * 128], 0.0)
        cw_sum[...] = mine
        for k, nm in enumerate(names):
            w_ref, m_ref, v_ref = w_refs[3 * k:3 * k + 3]
            g_ref, d_ref, nm_ref, nv_ref = o_refs[4 * k:4 * k + 4]
            if nm == "pool_w":
                g = total(3, everything + (slice(None),))
            elif nm == "conv_w":
                g = cw_sum[pl.ds(0, CONV_K), :]
            else:
                blk, row = SMALL_ROWS[nm]
                g = total(blk, (pl.ds(row, 1), slice(None)))
            g_ref[...] = g
            d_ref[...], nm_ref[...], nv_ref[...] = _adam_math(w_ref[...], g, m_ref[...], v_ref[...])

    whole = lambda a: pl.BlockSpec(a.shape, lambda t, pr: (0,) * a.ndim)
    out_shape = [jax.ShapeDtypeStruct((1, 128), f32)]
    out_shape += [jax.ShapeDtypeStruct(params[nm][0].shape, f32) for nm in names for _ in range(4)]
    res = pl.pallas_call(
        body, name="small_adam",
        grid_spec=pltpu.PrefetchScalarGridSpec(
            num_scalar_prefetch=1, grid=(1,),
            in_specs=[whole(a) for a in list(gathered) + list(parts) + flat],
            out_specs=[whole(s) for s in out_shape], scratch_shapes=[pltpu.VMEM((HALO, 128), f32)]),
        out_shape=out_shape, compiler_params=_cp("arbitrary"),
    )(place, *gathered, *parts, *flat)
    return res[0], {nm: res[1 + 4 * k:5 + 4 * k] for k, nm in enumerate(names)}


def _pad_rows(a, rows):
    return jnp.concatenate([a, jnp.zeros((rows - a.shape[0],) + a.shape[1:], a.dtype)], axis=0)


def kernel(x, p, g_mix, w_in, conv_w, conv_b, ln_g, ln_b, pool_w, pool_scale, w_out, g_ffn, w_gate_up, w_down, g_ple_gate, w_ple_gate, w_ple_up, g_ple_post, g_final, loss_target, m_g_mix, m_w_in, m_conv_w, m_conv_b, m_ln_g, m_ln_b, m_pool_w, m_pool_scale, m_w_out, m_g_ffn, m_w_gate_up, m_w_down, m_g_ple_gate, m_w_ple_gate, m_w_ple_up, m_g_ple_post, m_g_final, v_g_mix, v_w_in, v_conv_w, v_conv_b, v_ln_g, v_ln_b, v_pool_w, v_pool_scale, v_w_out, v_g_ffn, v_w_gate_up, v_w_down, v_g_ple_gate, v_w_ple_gate, v_w_ple_up, v_g_ple_post, v_g_final):
    seq = x.shape[1]
    me = 2 * lax.axis_index("x") + lax.axis_index("y")
    chip = me.astype(jnp.int32).reshape(1)
    core = lax.axis_index("c").astype(jnp.int32).reshape(1)
    place = jnp.concatenate([chip, core])
    xs, ps, ts = x.reshape(seq, D_MODEL), p.reshape(seq, D_PLE), loss_target.reshape(seq, D_MODEL)

    big = [w_in[0], w_gate_up[0], w_out[0], w_down[0], w_ple_gate[0], w_ple_up[0]]
    big_m = [m_w_in[0], m_w_gate_up[0], m_w_out[0], m_w_down[0], m_w_ple_gate[0], m_w_ple_up[0]]
    big_v = [v_w_in[0], v_w_gate_up[0], v_w_out[0], v_w_down[0], v_w_ple_gate[0], v_w_ple_up[0]]
    b_in, b_gu, b_out, b_down, b_pg, b_pu, b_cw = _cast_into_slots(
        chip, big + [_pad_rows(conv_w[0], HALO)], [bf16] * len(big) + [f32])
    xi, yi = lax.axis_index("x"), lax.axis_index("y")
    order = jnp.stack([me, 2 * (1 - xi) + yi, 2 * xi + 1 - yi, 2 * (1 - xi) + 1 - yi]).astype(jnp.int32)

    z, (w_in_g, cw_g, w_out_g) = _mix_in(xs, g_mix, order, [b_in, b_cw, b_out])
    conv_w_f = cw_g.transpose(1, 0, 2).reshape(HALO, C_CONV)
    w_out_f = w_out_g.reshape(D_MODEL, D_MODEL)
    (x1, mix, u1, pooled), (w_gu_g, w_down_g) = _conv_pool_out(z, xs, conv_w_f, conv_b, ln_g, ln_b, pool_w[0],
                                                                pool_scale, w_out_f, [b_gu, b_down])
    w_down_f = w_down_g.reshape(D_FF, D_MODEL)
    (x2, h2, gu, ffn_f), (w_pg_g, w_pu_g) = _ffn_fwd(x1, g_ffn, w_gu_g, w_down_f, [b_pg, b_pu])
    w_pg_f = w_pg_g.reshape(D_MODEL, D_MODEL)
    dx2, d_w_pg, d_w_pu, small_ple = _ple_loss(x2, ps, ts, g_ple_gate, g_ple_post, g_final.reshape(1, D_MODEL),
                                               w_pg_f, w_pu_g)
    d_w_down = _ffn_bwd_dw_down(ffn_f, dx2)
    parts_a = _pair_reduce("a", [d_w_pg.reshape(N_CHIPS, -1, D_MODEL), d_w_pu,
                                 d_w_down.reshape(N_CHIPS, -1, D_MODEL)])
    (dx1, dgu, small_ffn), landed_a = _ffn_bwd_dx(dx2, x1, gu, g_ffn, w_gu_g, w_down_f, parts_a)
    d_w_gu = _ffn_bwd_dw_gu(h2, dgu)
    du1, dpo, d_w_out, d_pool_w, small_mix = _mix_bwd_local(dx1, mix, u1, pooled, w_out_f, ln_g, ln_b, pool_w[0],
                                                             pool_scale)
    parts_b = _pair_reduce("b", [d_w_gu, d_w_out.reshape(N_CHIPS, -1, D_MODEL)])
    small_0 = [small_ple, small_ffn, small_mix, d_pool_w]
    (grad_x, d_w_in, d_conv_w, small_in), landed_b, small_all_0 = _in_bwd(du1, dpo, z, xs, dx1, conv_w_f, g_mix,
                                                                            w_in_g, parts_b, small_0)
    parts_c = _pair_reduce("c", [d_w_in])
    small_1 = [d_conv_w, small_in]
    (h_gu, h_out, h_down, h_pg, h_pu), landed_c, small_all_1 = _rs_sum_group(
        "ab", place, [landed_b[0], landed_b[1], landed_a[2], landed_a[0], landed_a[1]],
        [parts_b[0], parts_b[1], parts_a[2], parts_a[0], parts_a[1]], parts_c, small_1)
    h_in = _rs_sum_chips("w_in", place, landed_c[0], parts_c[0])
    big_g = _grad_pair([h_in, h_gu, h_out, h_down, h_pg, h_pu])
    big_upd = _adam(big, big_g, big_m, big_v)

    p3 = lambda w, m, v: (w, m, v)
    row = lambda a: a.reshape(1, D_MODEL)
    params = dict(
        g_ple_gate=p3(g_ple_gate, m_g_ple_gate, v_g_ple_gate), g_ple_post=p3(g_ple_post, m_g_ple_post, v_g_ple_post),
        g_final=p3(row(g_final), row(m_g_final), row(v_g_final)), g_ffn=p3(g_ffn, m_g_ffn, v_g_ffn),
        ln_g=p3(ln_g, m_ln_g, v_ln_g), ln_b=p3(ln_b, m_ln_b, v_ln_b), conv_b=p3(conv_b, m_conv_b, v_conv_b),
        pool_scale=p3(pool_scale, m_pool_scale, v_pool_scale), pool_w=p3(pool_w[0], m_pool_w[0], v_pool_w[0]),
        conv_w=p3(conv_w[0], m_conv_w[0], v_conv_w[0]), g_mix=p3(g_mix, m_g_mix, v_g_mix))
    loss, small = _small_adam(place, list(small_all_0) + list(small_all_1), small_0 + small_1, params)
    back = dict(g_final=lambda a: a.reshape(D_MODEL), pool_w=lambda a: a[None], conv_w=lambda a: a[None])

    names = ["g_mix", "w_in", "conv_w", "conv_b", "ln_g", "ln_b", "pool_w", "pool_scale", "w_out", "g_ffn",
             "w_gate_up", "w_down", "g_ple_gate", "w_ple_gate", "w_ple_up", "g_ple_post", "g_final"]
    big_at = {"w_in": 0, "w_gate_up": 1, "w_out": 2, "w_down": 3, "w_ple_gate": 4, "w_ple_up": 5}
    out = [loss[0, 0], grad_x.reshape(1, seq, D_MODEL)]
    for kind in range(4):
        for nm in names:
            if nm in big_at:
                k = big_at[nm]
                out.append((big_g[k] if kind == 0 else big_upd[k][kind - 1])[None])
            else:
                out.append(back.get(nm, lambda a: a)(small[nm][kind]))
    return tuple(out)
```

```python
import functools

import jax
import jax.numpy as jnp
from jax import lax
from jax.experimental import pallas as pl
from jax.experimental.pallas import tpu as pltpu

f32, bf16 = jnp.float32, jnp.bfloat16

EPS = 1e-6
D_MODEL = 1024
C_CONV = 512
C_POOL = 512
POOL_WINDOWS = (2, 4, 8, 16)
POOL_GROUP = 128
CONV_K = 31
D_FF = 2816
D_PLE = 256
N_CHIPS = 4
N_DEV = 8
W_IN_COLS = 2 * C_CONV + C_POOL
W_IN_CHUNK = W_IN_COLS // N_CHIPS
FF_CHUNK = 2 * D_FF // N_CHIPS
PLE_CHUNK = D_MODEL // N_CHIPS
HALO = 32
ROW_TILE = 512
CONV_ROWS = 64
CONV_COLS = (slice(0, 256), slice(256, 512))
SHIFT_PAD = 32
SHIFT_GROUPS = 16
FF_SUB = (0, 512, 1024, FF_CHUNK)
VMEM_LIMIT = 56 * 1024 * 1024

ADAM_LR = 0.001
ADAM_B1 = 0.9
ADAM_B2 = 0.999
ADAM_EPS = 1e-08
ADAM_WD = 0.01
ADAM_STEP = 10

MESH = pl.DeviceIdType.MESH
ANY = pl.BlockSpec(memory_space=pl.ANY)
VMEM = pl.BlockSpec(memory_space=pltpu.VMEM)


def _cp(*sem):
    return pltpu.CompilerParams(dimension_semantics=sem, vmem_limit_bytes=VMEM_LIMIT)


def _dot(a, b):
    return jnp.dot(a, b, preferred_element_type=f32)


def _dot_nt(a, b):
    return lax.dot_general(a, b, (((1,), (1,)), ((), ())), preferred_element_type=f32)


def _dot_tn(a, b):
    return lax.dot_general(a, b, (((0,), (0,)), ((), ())), preferred_element_type=f32)


def _sigmoid(v):
    return jax.nn.sigmoid(v)


def _rms_fwd(v, g):
    r = lax.rsqrt(jnp.mean(v * v, axis=-1, keepdims=True) + EPS)
    vh = v * r
    return vh * g, vh, r


def _rms_bwd(dy, vh, r, g):
    dvh = dy * g
    dv = r * (dvh - vh * jnp.mean(dvh * vh, axis=-1, keepdims=True))
    return dv, jnp.sum(dy * vh, axis=0, keepdims=True)


def _silu_grad(v, s):
    return s * (1.0 + v * (1.0 - s))


def _row(i, n):
    return pl.BlockSpec((n[0], n[1]), lambda *a: (a[i], 0))


def _full(shape):
    nd = len(shape)
    return pl.BlockSpec(shape, lambda *a: (0,) * nd)


def _place():
    x, y, c = lax.axis_index("x"), lax.axis_index("y"), lax.axis_index("c")
    others = [(1 - x, y), (x, 1 - y), (1 - x, 1 - y)]
    return x, y, c, 2 * x + y, others


def _remote(src, dst, send_sem, recv_sem, dev):
    return pltpu.make_async_remote_copy(src_ref=src, dst_ref=dst, send_sem=send_sem, recv_sem=recv_sem,
                                        device_id=dev, device_id_type=MESH)


SHARD_STEPS = 4


def _cast_into_slots(me, ws, dtypes):
    n = len(ws)

    def body(me_ref, *refs):
        for w_ref, o_ref, dtype in zip(refs[:n], refs[n:], dtypes):
            o_ref[0] = w_ref[...].astype(dtype)

    return pl.pallas_call(
        body, name="cast_shards",
        grid_spec=pltpu.PrefetchScalarGridSpec(
            num_scalar_prefetch=1, grid=(SHARD_STEPS,),
            in_specs=[pl.BlockSpec((w.shape[0] // SHARD_STEPS, w.shape[1]), lambda r, me_ref: (r, 0)) for w in ws],
            out_specs=[pl.BlockSpec((1, w.shape[0] // SHARD_STEPS, w.shape[1]), lambda r, me_ref: (me_ref[0], r, 0))
                       for w in ws]),
        out_shape=[jax.ShapeDtypeStruct((N_CHIPS,) + w.shape, dt) for w, dt in zip(ws, dtypes)],
        compiler_params=_cp("parallel"),
    )(me, *ws)


class _Gather:
    def __init__(self, bufs, send_sems, recv_sems):
        self.bufs, self.send_sems, self.recv_sems = bufs, send_sems, recv_sems
        self.x, self.y, self.c, self.me, self.others = _place()
        self.halves = [b.shape[1] // 2 for b in bufs]

    def _piece(self, k, chip, half):
        return self.bufs[k].at[chip, pl.ds(half * self.halves[k], self.halves[k]), :]

    def _ici(self, k, j, chip):
        ox, oy = self.others[j]
        piece = self._piece(k, chip, self.c)
        return _remote(piece, piece, self.send_sems.at[6 * k + j], self.recv_sems.at[6 * k + j], (ox, oy, self.c))

    def _relay(self, k):
        first = self.c == 0
        piece = self._piece(k, jnp.where(first, self.chip(0), self.chip(1)), self.c)
        to = (jnp.where(first, self.others[1][0], self.others[0][0]),
              jnp.where(first, self.others[1][1], self.others[0][1]), self.c)
        return _remote(piece, piece, self.send_sems.at[6 * k + 2], self.recv_sems.at[6 * k + 2], to)

    def _pair(self, k, j, half):
        ox, oy = self.others[j]
        piece = self._piece(k, 2 * ox + oy, half)
        return _remote(piece, piece, self.send_sems.at[6 * k + 3 + j], self.recv_sems.at[6 * k + 3 + j],
                       (self.x, self.y, 1 - self.c))

    def _each(self, ks=None):
        return [(k, j) for k in (range(len(self.bufs)) if ks is None else ks) for j in range(3)]

    def chip(self, j):
        ox, oy = self.others[j]
        return 2 * ox + oy

    def start(self):
        for k in range(len(self.bufs)):
            for j in range(2):
                self._ici(k, j, self.me).start()

    def forward(self, pairs=None):
        for k, j in self._each() if pairs is None else pairs:
            self._ici(k, j, self.chip(j)).wait_recv()
            self._pair(k, j, self.c).start()
            if j < 2:
                pl.when(self.c == j)(self._relay(k).start)

    def landed(self, pairs):
        for k, j in pairs:
            self._pair(k, j, 1 - self.c).wait_recv()

    def finish(self, ks=None):
        self.landed(self._each(ks))
        for k in range(len(self.bufs)):
            for j in range(2):
                self._ici(k, j, self.me).wait_send()
            self._relay(k).wait_send()
            for j in range(3):
                self._pair(k, j, self.c).wait_send()

    @staticmethod
    def scratch(n):
        return [pltpu.SemaphoreType.DMA((6 * n,)), pltpu.SemaphoreType.DMA((6 * n,))]


def _carried(bufs):
    n = len(bufs)
    return dict(in_specs=[ANY] * n, out_specs=[ANY] * n,
                out_shape=[jax.ShapeDtypeStruct(b.shape, b.dtype) for b in bufs], scratch=_Gather.scratch(n))


def _mix_in(x, g_mix, order, carry):
    s = x.shape[0]
    tm = min(2 * ROW_TILE, s)
    n = s // tm
    nc = len(carry)
    cs = _carried(carry)

    def body(order_ref, x_ref, g_ref, *refs):
        z_ref = refs[nc]
        bufs = refs[nc + 1:2 * nc + 1]
        h_ref, w_ref, w_sem = refs[2 * nc + 1:2 * nc + 4]
        gather = _Gather(bufs, *refs[2 * nc + 4:])
        q, i = pl.program_id(0), pl.program_id(1)
        first = i == 0
        pl.when(jnp.logical_and(q == 0, first))(gather.start)
        for j in range(3):

            @pl.when(jnp.logical_and(q == j + 1, first))
            def _():
                gather.forward([(0, j)])
                gather.landed([(0, j)])
                if j == 1:
                    gather.forward([(k, jj) for k in range(1, nc) for jj in range(2)])

        @pl.when(first)
        def _():
            load = pltpu.make_async_copy(bufs[0].at[order_ref[q]], w_ref, w_sem)
            load.start()
            load.wait()

        @pl.when(q == 0)
        def _():
            h, _, _ = _rms_fwd(x_ref[...], g_ref[...])
            h_ref[i] = h.astype(bf16)

        z_ref[...] = _dot(h_ref[i], w_ref[...])

        @pl.when(jnp.logical_and(q == N_CHIPS - 1, i == n - 1))
        def _():
            gather.forward([(k, 2) for k in range(1, nc)])
            gather.finish(range(1, nc))

    res = pl.pallas_call(
        body, name="mix_in",
        grid_spec=pltpu.PrefetchScalarGridSpec(
            num_scalar_prefetch=1, grid=(N_CHIPS, n),
            in_specs=[pl.BlockSpec((tm, D_MODEL), lambda q, i, order_ref: (jnp.where(q == 0, i, 0), 0)),
                      pl.BlockSpec((1, D_MODEL), lambda q, i, order_ref: (0, 0))] + cs["in_specs"],
            out_specs=[pl.BlockSpec((tm, W_IN_CHUNK), lambda q, i, order_ref: (i, order_ref[q]))] + cs["out_specs"],
            scratch_shapes=[pltpu.VMEM((n, tm, D_MODEL), bf16), pltpu.VMEM((D_MODEL, W_IN_CHUNK), bf16),
                            pltpu.SemaphoreType.DMA(())] + cs["scratch"]),
        out_shape=[jax.ShapeDtypeStruct((s, W_IN_COLS), f32)] + cs["out_shape"],
        input_output_aliases={3 + k: 1 + k for k in range(nc)},
        compiler_params=_cp("arbitrary", "arbitrary"),
    )(order, x, g_mix, *carry)
    return res[0], res[1:]


def _tap_offsets(lo, hi):
    groups = [[o for o in range(lo, hi + 1) if o % 8 == s] for s in range(8)]
    return [g for g in groups if g]


def _tap_sum(buf, w_ref, row0, cols, tap_of):
    acc = jnp.zeros((CONV_ROWS, cols.stop - cols.start), f32)
    for offs in _tap_offsets(0, CONV_K - 1):
        slab = buf[pl.ds(row0 + offs[0], offs[-1] - offs[0] + CONV_ROWS), cols]
        for o in offs:
            acc = acc + w_ref[pl.ds(tap_of(o), 1), cols] * slab[o - offs[0]:o - offs[0] + CONV_ROWS]
    return acc


def _pool_counts(tm, w, first_row):
    t1 = (lax.broadcasted_iota(jnp.int32, (tm, 1), 0) + first_row + 1).astype(f32)
    return jnp.minimum(t1, float(w))


def _conv_pool_out(z, x, conv_w, conv_b, ln_g, ln_b, pool_w, pool_scale, w_out, g_ffn, carry):
    s = x.shape[0]
    tm = min(ROW_TILE, s)
    n = s // tm
    hb = tm // HALO
    nc = len(carry)
    cs = _carried(carry)

    def body(z_ref, zp_ref, x_ref, cw_ref, cb_ref, lg_ref, lb_ref, pw_ref, ps_ref, wo_ref, gf_ref, *refs):
        x1_ref, mix_ref, u1_ref, pooled_ref, h2_ref = refs[nc:nc + 5]
        ubuf, vbuf = refs[2 * nc + 5:2 * nc + 7]
        gather = _Gather(refs[nc + 5:2 * nc + 5], *refs[2 * nc + 7:])
        i = pl.program_id(0)
        pl.when(i == 0)(gather.start)
        for k in range(nc):
            pl.when(i == min(n // 2 + 2 * k, n - 1))(functools.partial(gather.forward, [(k, 0), (k, 1)]))
        keep = (i > 0).astype(f32)
        zp = zp_ref[...] * keep
        ubuf[0:HALO, :] = zp[:, :C_CONV] * _sigmoid(zp[:, C_CONV:2 * C_CONV])
        vbuf[0:HALO, :] = zp[:, 2 * C_CONV:]
        ubuf[HALO:, :] = z_ref[:, :C_CONV] * _sigmoid(z_ref[:, C_CONV:2 * C_CONV])
        vbuf[HALO:, :] = z_ref[:, 2 * C_CONV:]
        off = HALO - (CONV_K - 1)
        for r0 in range(0, tm, CONV_ROWS):
            for cols in CONV_COLS:
                u1_ref[r0:r0 + CONV_ROWS, cols] = cb_ref[:, cols] + _tap_sum(ubuf, cw_ref, r0 + off, cols, lambda o: o)
        u1 = u1_ref[...]
        mu = jnp.mean(u1, axis=-1, keepdims=True)
        uc = u1 - mu
        rstd = lax.rsqrt(jnp.mean(uc * uc, axis=-1, keepdims=True) + EPS)
        u2 = uc * rstd * lg_ref[...] + lb_ref[...]
        mix_ref[:, :C_CONV] = (u2 * _sigmoid(u2)).astype(bf16)
        for g, w in enumerate(POOL_WINDOWS):
            cols = slice(g * POOL_GROUP, (g + 1) * POOL_GROUP)
            acc = vbuf[pl.ds(HALO, tm), cols]
            vg = acc
            for d in range(1, w):
                acc = acc + vbuf[pl.ds(HALO - d, tm), cols]
            pooled = (acc / _pool_counts(tm, w, i * tm) - vg).astype(bf16)
            pooled_ref[:, cols] = pooled
            mixed = _dot(pooled, pw_ref[g].astype(bf16))
            mix_ref[:, C_CONV + g * POOL_GROUP:C_CONV + (g + 1) * POOL_GROUP] = (mixed * ps_ref[:, cols]).astype(bf16)
        x1 = x_ref[...] + _dot(mix_ref[...], wo_ref[...])
        x1_ref[...] = x1
        h2_ref[...] = _rms_fwd(x1, gf_ref[...])[0].astype(bf16)
        @pl.when(i == n - 1)
        def _():
            gather.forward([(k, 2) for k in range(nc)])
            gather.finish()

    res = pl.pallas_call(
        body, name="conv_pool_out", grid=(n,),
        in_specs=[_row(0, (tm, W_IN_COLS)),
                  pl.BlockSpec((HALO, W_IN_COLS), lambda i: (jnp.maximum(i * hb - 1, 0), 0)),
                  _row(0, (tm, D_MODEL)), _full((HALO, C_CONV)), _full((1, C_CONV)), _full((1, C_CONV)),
                  _full((1, C_CONV)), _full((4, POOL_GROUP, POOL_GROUP)), _full((1, C_POOL)),
                  _full((D_MODEL, D_MODEL)), _full((1, D_MODEL))] + cs["in_specs"],
        out_specs=[_row(0, (tm, D_MODEL)), _row(0, (tm, D_MODEL)), _row(0, (tm, C_CONV)), _row(0, (tm, C_POOL)),
                   _row(0, (tm, D_MODEL))] + cs["out_specs"],
        out_shape=[jax.ShapeDtypeStruct((s, D_MODEL), f32), jax.ShapeDtypeStruct((s, D_MODEL), bf16),
                   jax.ShapeDtypeStruct((s, C_CONV), f32), jax.ShapeDtypeStruct((s, C_POOL), bf16),
                   jax.ShapeDtypeStruct((s, D_MODEL), bf16)] + cs["out_shape"],
        input_output_aliases={11 + k: 5 + k for k in range(nc)},
        scratch_shapes=[pltpu.VMEM((HALO + tm, C_CONV), f32), pltpu.VMEM((HALO + tm, C_POOL), f32)] + cs["scratch"],
        compiler_params=_cp("arbitrary"),
    )(z, z, x, conv_w, conv_b, ln_g, ln_b, pool_w, pool_scale, w_out, g_ffn, *carry)
    return res[:5], res[5:]


def _ffn_fwd(x1, h2, w_gu_g, w_down, carry):
    s = x1.shape[0]
    tm = min(ROW_TILE, s)
    n = s // tm
    nc = len(carry)
    cs = _carried(carry)

    def body(x1_ref, h2_ref, wg_ref, wu_ref, wd_ref, *refs):
        x2_ref, gu_ref, f_ref = refs[nc:nc + 3]
        acc_ref = refs[2 * nc + 3]
        gather = _Gather(refs[nc + 3:2 * nc + 3], *refs[2 * nc + 4:])
        i, c = pl.program_id(0), pl.program_id(1)
        pl.when(jnp.logical_and(i == 0, c == 0))(gather.start)
        direct = [(k, j) for k in range(nc) for j in range(2)]
        pl.when(jnp.logical_and(i == n // 2, c == 0))(functools.partial(gather.forward, direct))
        pl.when(jnp.logical_and(i == n - 1, c == 0))(functools.partial(gather.forward, [(k, 2) for k in range(nc)]))

        @pl.when(c == 0)
        def _():
            acc_ref[...] = jnp.zeros_like(acc_ref)

        h = h2_ref[...]
        for lo, hi in zip(FF_SUB[:-1], FF_SUB[1:]):
            gate = _dot(h, wg_ref[0, :, lo:hi])
            up = _dot(h, wu_ref[0, :, lo:hi])
            gu_ref[0, :, lo:hi] = gate.astype(bf16)
            gu_ref[1, :, lo:hi] = up.astype(bf16)
            f = (gate * _sigmoid(gate) * up).astype(bf16)
            f_ref[:, lo:hi] = f
            acc_ref[...] += _dot(f, wd_ref[lo:hi, :])

        @pl.when(c == 1)
        def _():
            x2_ref[...] = x1_ref[...] + acc_ref[...]

        pl.when(jnp.logical_and(i == n - 1, c == 1))(gather.finish)

    res = pl.pallas_call(
        body, name="ffn_fwd", grid=(n, 2),
        in_specs=[_row(0, (tm, D_MODEL)), _row(0, (tm, D_MODEL)),
                  pl.BlockSpec((1, D_MODEL, FF_CHUNK), lambda i, c: (c, 0, 0)),
                  pl.BlockSpec((1, D_MODEL, FF_CHUNK), lambda i, c: (2 + c, 0, 0)),
                  pl.BlockSpec((FF_CHUNK, D_MODEL), lambda i, c: (c, 0))] + cs["in_specs"],
        out_specs=[_row(0, (tm, D_MODEL)), pl.BlockSpec((2, tm, FF_CHUNK), lambda i, c: (0, i, c)),
                   pl.BlockSpec((tm, FF_CHUNK), lambda i, c: (i, c))] + cs["out_specs"],
        out_shape=[jax.ShapeDtypeStruct((s, D_MODEL), f32), jax.ShapeDtypeStruct((2, s, D_FF), bf16),
                   jax.ShapeDtypeStruct((s, D_FF), bf16)] + cs["out_shape"],
        input_output_aliases={5 + k: 3 + k for k in range(nc)},
        scratch_shapes=[pltpu.VMEM((tm, D_MODEL), f32)] + cs["scratch"],
        compiler_params=_cp("arbitrary", "arbitrary"),
    )(x1, h2, w_gu_g, w_gu_g, w_down, *carry)
    return res[:3], res[3:]


def _ple_loss(x2, p, target, g_pg, g_post, g_final, w_pg, w_pu_g):
    s = x2.shape[0]
    tm = min(ROW_TILE, s)
    n = s // tm

    def body(x2_ref, p_ref, t_ref, gpg_ref, gpo_ref, gf_ref, wpg_ref, wpu_ref,
             dx2_ref, dwpg_ref, dwpu_ref, small_ref, apg_ref, apu_ref):
        i = pl.program_id(0)

        @pl.when(i == 0)
        def _():
            apg_ref[...] = jnp.zeros_like(apg_ref)
            apu_ref[...] = jnp.zeros_like(apu_ref)
            small_ref[...] = jnp.zeros_like(small_ref)

        x2 = x2_ref[...]
        h3, x2h, r2 = _rms_fwd(x2, gpg_ref[...])
        h3b = h3.astype(bf16)
        gate = _sigmoid(_dot(h3b, wpg_ref[...]))
        pb = p_ref[...].astype(bf16)
        pe = jnp.concatenate([_dot(pb, wpu_ref[j]) for j in range(N_CHIPS)], axis=-1)
        e, peh, rp = _rms_fwd(pe, gpo_ref[...])
        x3 = x2 + gate * e
        y, x3h, r3 = _rms_fwd(x3, gf_ref[...])
        d = y - t_ref[...]
        loss = 0.5 * jnp.sum(jnp.sum(d * d, axis=-1, keepdims=True) * (1.0 / D_MODEL), axis=0, keepdims=True)
        dx3, dgf = _rms_bwd(d * (1.0 / D_MODEL), x3h, r3, gf_ref[...])
        dpe, dgpo = _rms_bwd(dx3 * gate, peh, rp, gpo_ref[...])
        dgl = (dx3 * e * gate * (1.0 - gate)).astype(bf16)
        apg_ref[...] += _dot_tn(h3b, dgl)
        apu_ref[...] += _dot_tn(pb, dpe.astype(bf16))
        dh3 = _dot_nt(dgl, wpg_ref[...])
        dx2b, dgpg = _rms_bwd(dh3, x2h, r2, gpg_ref[...])
        dx2_ref[...] = dx3 + dx2b
        small_ref[0:1, :] += dgpg
        small_ref[1:2, :] += dgpo
        small_ref[2:3, :] += dgf
        small_ref[3:4, :] += jnp.broadcast_to(loss, (1, D_MODEL))

        @pl.when(i == n - 1)
        def _():
            dwpg_ref[...] = apg_ref[...].astype(bf16)
            for j in range(N_CHIPS):
                dwpu_ref[j] = apu_ref[:, j * PLE_CHUNK:(j + 1) * PLE_CHUNK].astype(bf16)

    return pl.pallas_call(
        body, name="ple_loss", grid=(n,),
        in_specs=[_row(0, (tm, D_MODEL)), _row(0, (tm, D_PLE)), _row(0, (tm, D_MODEL)),
                  _full((1, D_MODEL)), _full((1, D_MODEL)), _full((1, D_MODEL)),
                  _full((D_MODEL, D_MODEL)), _full((N_CHIPS, D_PLE, PLE_CHUNK))],
        out_specs=[_row(0, (tm, D_MODEL)), _full((D_MODEL, D_MODEL)), _full((N_CHIPS, D_PLE, PLE_CHUNK)),
                   _full((8, D_MODEL))],
        out_shape=[jax.ShapeDtypeStruct((s, D_MODEL), f32), jax.ShapeDtypeStruct((D_MODEL, D_MODEL), bf16),
                   jax.ShapeDtypeStruct((N_CHIPS, D_PLE, PLE_CHUNK), bf16), jax.ShapeDtypeStruct((8, D_MODEL), f32)],
        scratch_shapes=[pltpu.VMEM((D_MODEL, D_MODEL), f32), pltpu.VMEM((D_PLE, D_MODEL), f32)],
        compiler_params=_cp("arbitrary"),
    )(x2, p, target, g_pg, g_post, g_final, w_pg, w_pu_g)


def _crossed(parts):
    n = len(parts)
    return dict(in_specs=[ANY] * n, out_specs=[ANY] * n,
                out_shape=[jax.ShapeDtypeStruct(a.shape, a.dtype) for a in parts], scratch=_Cross.scratch(n))


def _ffn_bwd_dx(dx2, x1, gu, g_ffn, w_gu_g, w_down, parts):
    s = x1.shape[0]
    tm = min(ROW_TILE, s)
    n = s // tm
    nc = len(parts)
    cs = _crossed(parts)

    def body(dx2_ref, x1_ref, gu_ref, g_ref, wg_ref, wu_ref, wd_ref, *refs):
        dx1_ref, dgu_ref, small_ref = refs[nc:nc + 3]
        acc_ref = refs[2 * nc + 3]
        cross = _Cross(refs[:nc], refs[nc + 3:2 * nc + 3], *refs[2 * nc + 4:])
        i, c = pl.program_id(0), pl.program_id(1)
        pl.when(jnp.logical_and(i == 0, c == 0))(cross.start)

        @pl.when(jnp.logical_and(i == 0, c == 0))
        def _():
            small_ref[...] = jnp.zeros_like(small_ref)

        @pl.when(c == 0)
        def _():
            acc_ref[...] = jnp.zeros_like(acc_ref)

        dyb = dx2_ref[...].astype(bf16)
        for lo, hi in zip(FF_SUB[:-1], FF_SUB[1:]):
            df = _dot_nt(dyb, wd_ref[lo:hi, :])
            gate = gu_ref[0, :, lo:hi].astype(f32)
            up = gu_ref[1, :, lo:hi].astype(f32)
            sg = _sigmoid(gate)
            dgate = (df * up * _silu_grad(gate, sg)).astype(bf16)
            dup = (df * gate * sg).astype(bf16)
            dgu_ref[0, :, lo:hi] = dgate
            dgu_ref[1, :, lo:hi] = dup
            acc_ref[...] += _dot_nt(dgate, wg_ref[0, :, lo:hi]) + _dot_nt(dup, wu_ref[0, :, lo:hi])

        @pl.when(c == 1)
        def _():
            _, x1h, r1 = _rms_fwd(x1_ref[...], g_ref[...])
            dx1b, dg = _rms_bwd(acc_ref[...], x1h, r1, g_ref[...])
            dx1_ref[...] = dx2_ref[...] + dx1b
            small_ref[0:1, :] += dg

        pl.when(jnp.logical_and(i == n - 1, c == 1))(cross.finish)

    res = pl.pallas_call(
        body, name="ffn_bwd_dx", grid=(n, 2),
        in_specs=[_row(0, (tm, D_MODEL)), _row(0, (tm, D_MODEL)),
                  pl.BlockSpec((2, tm, FF_CHUNK), lambda i, c: (0, i, c)), _full((1, D_MODEL)),
                  pl.BlockSpec((1, D_MODEL, FF_CHUNK), lambda i, c: (c, 0, 0)),
                  pl.BlockSpec((1, D_MODEL, FF_CHUNK), lambda i, c: (2 + c, 0, 0)),
                  pl.BlockSpec((FF_CHUNK, D_MODEL), lambda i, c: (c, 0))] + cs["in_specs"],
        out_specs=[_row(0, (tm, D_MODEL)), pl.BlockSpec((2, tm, FF_CHUNK), lambda i, c: (0, i, c)),
                   _full((8, D_MODEL))] + cs["out_specs"],
        out_shape=[jax.ShapeDtypeStruct((s, D_MODEL), f32), jax.ShapeDtypeStruct((2, s, D_FF), bf16),
                   jax.ShapeDtypeStruct((8, D_MODEL), f32)] + cs["out_shape"],
        scratch_shapes=[pltpu.VMEM((tm, D_MODEL), f32)] + cs["scratch"],
        compiler_params=_cp("arbitrary", "arbitrary"),
    )(dx2, x1, gu, g_ffn, w_gu_g, w_gu_g, w_down, *parts)
    return res[:3], res[3:]


def _ffn_bwd_dw_gu(h2, dgu):
    s = h2.shape[0]
    ts = min(2 * ROW_TILE, s)
    n = s // ts

    def body(h_ref, d_ref, o_ref, acc_ref):
        t = pl.program_id(1)

        @pl.when(t == 0)
        def _():
            acc_ref[...] = jnp.zeros_like(acc_ref)

        acc_ref[...] += _dot_tn(h_ref[...], d_ref[0])

        @pl.when(t == n - 1)
        def _():
            o_ref[0] = acc_ref[...].astype(bf16)

    return pl.pallas_call(
        body, name="ffn_bwd_dw_gu", grid=(N_CHIPS, n),
        in_specs=[pl.BlockSpec((ts, D_MODEL), lambda j, t: (t, 0)),
                  pl.BlockSpec((1, ts, FF_CHUNK), lambda j, t: (j // 2, t, j % 2))],
        out_specs=pl.BlockSpec((1, D_MODEL, FF_CHUNK), lambda j, t: (j, 0, 0)),
        out_shape=jax.ShapeDtypeStruct((N_CHIPS, D_MODEL, FF_CHUNK), bf16),
        scratch_shapes=[pltpu.VMEM((D_MODEL, FF_CHUNK), f32)],
        compiler_params=_cp("parallel", "arbitrary"),
    )(h2, dgu)


def _ffn_bwd_dw_down(f, dx2):
    s = dx2.shape[0]
    ts = min(2 * ROW_TILE, s)
    n = s // ts

    def body(f_ref, d_ref, o_ref, acc_ref):
        t = pl.program_id(1)

        @pl.when(t == 0)
        def _():
            acc_ref[...] = jnp.zeros_like(acc_ref)

        acc_ref[...] += _dot_tn(f_ref[...], d_ref[...].astype(bf16))

        @pl.when(t == n - 1)
        def _():
            o_ref[...] = acc_ref[...].astype(bf16)

    return pl.pallas_call(
        body, name="ffn_bwd_dw_down", grid=(2, n),
        in_specs=[pl.BlockSpec((ts, FF_CHUNK), lambda c, t: (t, c)),
                  pl.BlockSpec((ts, D_MODEL), lambda c, t: (t, 0))],
        out_specs=pl.BlockSpec((FF_CHUNK, D_MODEL), lambda c, t: (c, 0)),
        out_shape=jax.ShapeDtypeStruct((D_FF, D_MODEL), bf16),
        scratch_shapes=[pltpu.VMEM((FF_CHUNK, D_MODEL), f32)],
        compiler_params=_cp("parallel", "arbitrary"),
    )(f, dx2)


def _mix_bwd_local(dx1, mix, u1, pooled, w_out, ln_g, ln_b, pool_w, pool_scale):
    s = dx1.shape[0]
    tm = min(ROW_TILE, s)
    n = s // tm

    def body(dx1_ref, mix_ref, u1_ref, po_ref, wo_ref, lg_ref, lb_ref, pw_ref, ps_ref,
             du1_ref, dpo_ref, dwo_ref, dpw_ref, small_ref, awo_ref):
        i = pl.program_id(0)

        @pl.when(i == 0)
        def _():
            awo_ref[...] = jnp.zeros_like(awo_ref)
            dpw_ref[...] = jnp.zeros_like(dpw_ref)
            small_ref[...] = jnp.zeros_like(small_ref)

        dyb = dx1_ref[...].astype(bf16)
        dmix = _dot_nt(dyb, wo_ref[...])
        awo_ref[...] += _dot_tn(mix_ref[...], dyb)
        u1 = u1_ref[...]
        mu = jnp.mean(u1, axis=-1, keepdims=True)
        uc = u1 - mu
        rstd = lax.rsqrt(jnp.mean(uc * uc, axis=-1, keepdims=True) + EPS)
        uh = uc * rstd
        u2 = uh * lg_ref[...] + lb_ref[...]
        du2 = dmix[:, :C_CONV] * _silu_grad(u2, _sigmoid(u2))
        duh = du2 * lg_ref[...]
        du1 = rstd * (duh - jnp.mean(duh, axis=-1, keepdims=True) - uh * jnp.mean(duh * uh, axis=-1, keepdims=True))
        du1_ref[...] = du1
        small_ref[0:1, :] += jnp.sum(du2 * uh, axis=0, keepdims=True)
        small_ref[1:2, :] += jnp.sum(du2, axis=0, keepdims=True)
        small_ref[2:3, :] += jnp.sum(du1, axis=0, keepdims=True)
        for g in range(len(POOL_WINDOWS)):
            cols = slice(g * POOL_GROUP, (g + 1) * POOL_GROUP)
            dq = dmix[:, C_CONV + g * POOL_GROUP:C_CONV + (g + 1) * POOL_GROUP]
            pwb = pw_ref[g].astype(bf16)
            pg = po_ref[:, cols]
            mixed = _dot(pg, pwb)
            small_ref[3:4, cols] += jnp.sum(dq * mixed, axis=0, keepdims=True)
            dmixed = (dq * ps_ref[:, cols]).astype(bf16)
            dpw_ref[g] += _dot_tn(pg, dmixed)
            dpo_ref[:, cols] = _dot_nt(dmixed, pwb)

        @pl.when(i == n - 1)
        def _():
            dwo_ref[...] = awo_ref[...].astype(bf16)

    return pl.pallas_call(
        body, name="mix_bwd_local", grid=(n,),
        in_specs=[_row(0, (tm, D_MODEL)), _row(0, (tm, D_MODEL)), _row(0, (tm, C_CONV)), _row(0, (tm, C_POOL)),
                  _full((D_MODEL, D_MODEL)), _full((1, C_CONV)), _full((1, C_CONV)),
                  _full((4, POOL_GROUP, POOL_GROUP)), _full((1, C_POOL))],
        out_specs=[_row(0, (tm, C_CONV)), _row(0, (tm, C_POOL)), _full((D_MODEL, D_MODEL)),
                   _full((4, POOL_GROUP, POOL_GROUP)), _full((8, C_CONV))],
        out_shape=[jax.ShapeDtypeStruct((s, C_CONV), f32), jax.ShapeDtypeStruct((s, C_POOL), f32),
                   jax.ShapeDtypeStruct((D_MODEL, D_MODEL), bf16),
                   jax.ShapeDtypeStruct((4, POOL_GROUP, POOL_GROUP), f32), jax.ShapeDtypeStruct((8, C_CONV), f32)],
        scratch_shapes=[pltpu.VMEM((D_MODEL, D_MODEL), f32)],
        compiler_params=_cp("arbitrary"),
    )(dx1, mix, u1, pooled, w_out, ln_g, ln_b, pool_w, pool_scale)


def _in_bwd(du1, dpo, z, x, dx1, conv_w, g_mix, w_in_g, parts, small_parts):
    s = x.shape[0]
    tm = min(ROW_TILE, s)
    n = s // tm
    hb = tm // HALO
    last = s // HALO - 1
    nv = tm // 8
    assert nv >= SHIFT_PAD and nv % SHIFT_GROUPS == 0
    nc, ns = len(parts), len(small_parts)
    cs = _crossed(parts)

    def body(du_ref, dun_ref, dp_ref, dpn_ref, z_ref, zp_ref, x_ref, dx1_ref, cw_ref, g_ref, w_ref, *refs):
        outs = refs[nc + ns:]
        gx_ref, dw_ref, dcw_ref, small_ref = outs[:4]
        eu, ed, ep, ss, u0_ref, dz_ref, acc_ref, dcw_acc = outs[4 + nc + ns:12 + nc + ns]
        sems = outs[12 + nc + ns:]
        cross = _Cross(refs[:nc], outs[4:4 + nc], sems[0], sems[1])
        gather = _SmallGather(refs[nc:nc + ns], outs[4 + nc:4 + nc + ns], sems[2], sems[3])
        i = pl.program_id(0)
        pl.when(i == 0)(cross.start)
        pl.when(i == 0)(gather.start)

        @pl.when(i == 0)
        def _():
            acc_ref[...] = jnp.zeros_like(acc_ref)
            dcw_acc[...] = jnp.zeros_like(dcw_acc)
            small_ref[...] = jnp.zeros_like(small_ref)

        keep_prev = (i > 0).astype(f32)
        keep_next = (i < n - 1).astype(f32)
        zp = zp_ref[...] * keep_prev
        u0_prev = zp[:, :C_CONV] * _sigmoid(zp[:, C_CONV:2 * C_CONV])
        u0_ref[...] = z_ref[:, :C_CONV] * _sigmoid(z_ref[:, C_CONV:2 * C_CONV])
        du_next = dun_ref[...] * keep_next
        for c, w_pool in enumerate(POOL_WINDOWS):
            lanes = slice(c * 128, (c + 1) * 128)
            eu[c, pl.ds(0, SHIFT_PAD, stride=8), :] = u0_prev[:, lanes]
            ed[c, pl.ds(nv * 8 + 7, SHIFT_PAD, stride=8), :] = du_next[:, lanes]
            ep[c, pl.ds(nv * 8 + 7, SHIFT_PAD, stride=8), :] = (
                dpn_ref[:, lanes] * keep_next / _pool_counts(HALO, w_pool, (i + 1) * tm))
            for j in range(8):
                rows = slice(j * nv, (j + 1) * nv)
                eu[c, pl.ds(SHIFT_PAD * 8 + j, nv, stride=8), :] = u0_ref[rows, lanes]
                ed[c, pl.ds(j, nv, stride=8), :] = du_ref[rows, lanes]
                ep[c, pl.ds(j, nv, stride=8), :] = dp_ref[rows, lanes] / _pool_counts(nv, w_pool, i * tm + j * nv)
                if j >= 1:
                    eu[c, pl.ds(j, SHIFT_PAD, stride=8), :] = u0_ref[j * nv - SHIFT_PAD:j * nv, lanes]
                if j <= 6:
                    edge = slice((j + 1) * nv, (j + 1) * nv + SHIFT_PAD)
                    ed[c, pl.ds(nv * 8 + j, SHIFT_PAD, stride=8), :] = du_ref[edge, lanes]
                    ep[c, pl.ds(nv * 8 + j, SHIFT_PAD, stride=8), :] = (
                        dp_ref[edge, lanes] / _pool_counts(SHIFT_PAD, w_pool, i * tm + (j + 1) * nv))
        for c, w_pool in enumerate(POOL_WINDOWS):
            lanes = slice(c * 128, (c + 1) * 128)
            b_lanes = slice(C_CONV + c * 128, C_CONV + (c + 1) * 128)
            v_lanes = slice(2 * C_CONV + c * 128, 2 * C_CONV + (c + 1) * 128)
            for v0 in range(0, nv, SHIFT_GROUPS):
                span = SHIFT_GROUPS * 8
                acc = jnp.zeros((span, 128), f32)
                for k in range(CONV_K):
                    acc = acc + cw_ref[pl.ds(k, 1), lanes] * ed[c, pl.ds((v0 + CONV_K - 1 - k) * 8, span), :]
                ss[0, v0 * 8:v0 * 8 + span, :] = acc
                acc = ep[c, pl.ds(v0 * 8, span), :]
                for d in range(1, w_pool):
                    acc = acc + ep[c, pl.ds((v0 + d) * 8, span), :]
                ss[1, v0 * 8:v0 * 8 + span, :] = acc
                d1 = ed[c, pl.ds(v0 * 8, span), :]
                for k in range(CONV_K):
                    prod = d1 * eu[c, pl.ds((SHIFT_PAD - (CONV_K - 1) + v0 + k) * 8, span), :]
                    fold = prod[0:8]
                    for r in range(8, span, 8):
                        fold = fold + prod[r:r + 8]
                    dcw_acc[k, :, lanes] += fold
            for j in range(8):
                rows = slice(j * nv, (j + 1) * nv)
                du0 = ss[0, pl.ds(j, nv, stride=8), :]
                av, sv = z_ref[rows, lanes], _sigmoid(z_ref[rows, b_lanes])
                dz_ref[rows, lanes] = (du0 * sv).astype(bf16)
                dz_ref[rows, b_lanes] = (du0 * av * sv * (1.0 - sv)).astype(bf16)
                dz_ref[rows, v_lanes] = (ss[1, pl.ds(j, nv, stride=8), :] - dp_ref[rows, lanes]).astype(bf16)
        h, xh, r = _rms_fwd(x_ref[...], g_ref[...])
        dz = dz_ref[...]
        acc_ref[...] += _dot_tn(h.astype(bf16), dz)
        dh = _dot_nt(dz[:, 0:W_IN_CHUNK], w_ref[0])
        for j in range(1, N_CHIPS):
            dh = dh + _dot_nt(dz[:, j * W_IN_CHUNK:(j + 1) * W_IN_CHUNK], w_ref[j])
        dxb, dg = _rms_bwd(dh, xh, r, g_ref[...])
        gx_ref[...] = dx1_ref[...] + dxb
        small_ref[0:1, :] += dg

        @pl.when(i == n - 1)
        def _():
            for j in range(N_CHIPS):
                dw_ref[j] = acc_ref[:, j * W_IN_CHUNK:(j + 1) * W_IN_CHUNK].astype(bf16)
            dcw_ref[...] = jnp.sum(dcw_acc[...], axis=1)

        pl.when(i == n - 1)(gather.finish)
        pl.when(i == n - 1)(cross.finish)

    nxt = lambda i: (jnp.minimum((i + 1) * hb, last), 0)
    res = pl.pallas_call(
        body, name="in_bwd", grid=(n,),
        in_specs=[_row(0, (tm, C_CONV)), pl.BlockSpec((HALO, C_CONV), nxt),
                  _row(0, (tm, C_POOL)), pl.BlockSpec((HALO, C_POOL), nxt),
                  _row(0, (tm, W_IN_COLS)),
                  pl.BlockSpec((HALO, W_IN_COLS), lambda i: (jnp.maximum(i * hb - 1, 0), 0)),
                  _row(0, (tm, D_MODEL)), _row(0, (tm, D_MODEL)), _full((HALO, C_CONV)), _full((1, D_MODEL)),
                  _full((N_CHIPS, D_MODEL, W_IN_CHUNK))] + cs["in_specs"] + [ANY] * ns,
        out_specs=[_row(0, (tm, D_MODEL)), _full((N_CHIPS, D_MODEL, W_IN_CHUNK)), _full((HALO, C_CONV)),
                   _full((8, D_MODEL))] + cs["out_specs"] + [ANY] * ns,
        out_shape=[jax.ShapeDtypeStruct((s, D_MODEL), f32), jax.ShapeDtypeStruct((N_CHIPS, D_MODEL, W_IN_CHUNK), bf16),
                   jax.ShapeDtypeStruct((HALO, C_CONV), f32), jax.ShapeDtypeStruct((8, D_MODEL), f32)] + cs["out_shape"]
        + _gathered_shapes(small_parts),
        scratch_shapes=[pltpu.VMEM((4, (SHIFT_PAD + nv) * 8, 128), f32), pltpu.VMEM((4, (nv + SHIFT_PAD) * 8, 128), f32),
                        pltpu.VMEM((4, (nv + SHIFT_PAD) * 8, 128), f32), pltpu.VMEM((2, tm, 128), f32),
                        pltpu.VMEM((tm, C_CONV), f32), pltpu.VMEM((tm, W_IN_COLS), bf16),
                        pltpu.VMEM((D_MODEL, W_IN_COLS), f32), pltpu.VMEM((HALO, 8, C_CONV), f32)] + cs["scratch"]
        + _SmallGather.scratch(ns),
        compiler_params=_cp("arbitrary"),
    )(du1, du1, dpo, dpo, z, z, x, dx1, conv_w, g_mix, w_in_g, *parts, *small_parts)
    return res[:4], res[4:4 + nc], res[4 + nc:]


def _pair_reduce(name, grads):
    nk = len(grads)
    halves = [g.shape[1] // 2 for g in grads]

    def body(*refs):
        ins, outs, got = refs[:nk], refs[nk:2 * nk], refs[2 * nk:3 * nk]
        send_sems, recv_sems = refs[3 * nk:]
        x, y, c, _, _ = _place()

        def half(k, core):
            return pl.ds(pl.multiple_of(core * halves[k], 16), halves[k])

        cps = [_remote(ins[k].at[:, half(k, 1 - c), :], got[k], send_sems.at[k], recv_sems.at[k], (x, y, 1 - c))
               for k in range(nk)]
        for cp in cps:
            cp.start()
        for k, cp in enumerate(cps):
            cp.wait_recv()
            outs[k][...] = (ins[k][:, half(k, c), :].astype(f32) + got[k][...].astype(f32)).astype(bf16)
        for cp in cps:
            cp.wait_send()

    shapes = [(N_CHIPS, h, g.shape[2]) for g, h in zip(grads, halves)]
    return pl.pallas_call(
        body, name=f"pair_reduce_{name}", in_specs=[VMEM] * nk, out_specs=[VMEM] * nk,
        out_shape=[jax.ShapeDtypeStruct(s, bf16) for s in shapes],
        scratch_shapes=[pltpu.VMEM(s, bf16) for s in shapes]
        + [pltpu.SemaphoreType.DMA((nk,)), pltpu.SemaphoreType.DMA((nk,))],
        compiler_params=pltpu.CompilerParams(vmem_limit_bytes=VMEM_LIMIT),
    )(*grads)


class _Cross:
    def __init__(self, parts, landed, send_sems, recv_sems):
        self.parts, self.landed, self.send_sems, self.recv_sems = parts, landed, send_sems, recv_sems
        _, _, self.c, self.me, self.others = _place()

    def _copy(self, k, j, src_chunk, dst_slot):
        ox, oy = self.others[j]
        return _remote(self.parts[k].at[src_chunk], self.landed[k].at[dst_slot], self.send_sems.at[3 * k + j],
                       self.recv_sems.at[3 * k + j], (ox, oy, self.c))

    def _each(self):
        return [(k, j, 2 * self.others[j][0] + self.others[j][1]) for k in range(len(self.parts)) for j in range(3)]

    def start(self):
        for k, j, chip in self._each():
            self._copy(k, j, chip, self.me).start()

    def finish(self):
        for k, j, chip in self._each():
            self._copy(k, j, chip, chip).wait_recv()
        for k, j, chip in self._each():
            self._copy(k, j, chip, self.me).wait_send()

    @staticmethod
    def scratch(n):
        return [pltpu.SemaphoreType.DMA((3 * n,)), pltpu.SemaphoreType.DMA((3 * n,))]


def _rs_sum_chips(name, place, landed, part):
    _, half, cols = landed.shape

    def body(place_ref, l_ref, p_ref, o_ref):
        me = place_ref[0]
        own = p_ref[0].astype(f32)
        acc = jnp.where(me == 0, own, l_ref[0].astype(f32))
        for j in range(1, N_CHIPS):
            acc = acc + jnp.where(me == j, own, l_ref[j].astype(f32))
        o_ref[...] = acc

    return pl.pallas_call(
        body, name=f"rs_sum_chips_{name}",
        grid_spec=pltpu.PrefetchScalarGridSpec(
            num_scalar_prefetch=1, grid=(1,),
            in_specs=[pl.BlockSpec((N_CHIPS, half, cols), lambda t, place_ref: (0, 0, 0)),
                      pl.BlockSpec((1, half, cols), lambda t, place_ref: (place_ref[0], 0, 0))],
            out_specs=pl.BlockSpec((half, cols), lambda t, place_ref: (place_ref[1], 0))),
        out_shape=jax.ShapeDtypeStruct((2 * half, cols), f32),
        compiler_params=_cp("arbitrary"),
    )(place, landed, part)


def _grad_pair(shards):
    nk = len(shards)

    def body(*refs):
        outs = refs[nk:2 * nk]
        send_sems, recv_sems = refs[2 * nk:]
        x, y, c, _, _ = _place()

        def half(k, core):
            h = outs[k].shape[0] // 2
            return outs[k].at[pl.ds(core * h, h), :]

        cps = [_remote(half(k, c), half(k, c), send_sems.at[k], recv_sems.at[k], (x, y, 1 - c)) for k in range(nk)]
        for cp in cps:
            cp.start()
        for k in range(nk):
            _remote(half(k, 1 - c), half(k, 1 - c), send_sems.at[k], recv_sems.at[k], (x, y, 1 - c)).wait_recv()
        for cp in cps:
            cp.wait_send()

    return pl.pallas_call(
        body, name="grad_pair", in_specs=[ANY] * nk, out_specs=[ANY] * nk,
        out_shape=[jax.ShapeDtypeStruct(a.shape, f32) for a in shards],
        input_output_aliases={k: k for k in range(nk)},
        scratch_shapes=[pltpu.SemaphoreType.DMA((nk,)), pltpu.SemaphoreType.DMA((nk,))],
    )(*shards)


def _adam_math(w, g, m, v):
    m = ADAM_B1 * m + (1.0 - ADAM_B1) * g
    v = ADAM_B2 * v + (1.0 - ADAM_B2) * (g * g)
    m_hat = m / (1.0 - ADAM_B1 ** ADAM_STEP)
    v_hat = v / (1.0 - ADAM_B2 ** ADAM_STEP)
    delta = -ADAM_LR * (m_hat / (jnp.sqrt(v_hat) + ADAM_EPS) + ADAM_WD * w)
    return delta, m, v


def _adam(ws, gs, ms, vs):
    n = len(ws)
    steps = 2 * SHARD_STEPS

    def body(*refs):
        for k in range(n):
            w_ref, g_ref, m_ref, v_ref = (refs[j * n + k] for j in range(4))
            go_ref, d_ref, nm_ref, nv_ref = (refs[(4 + j) * n + k] for j in range(4))
            g = g_ref[...]
            go_ref[...] = g
            d_ref[...], nm_ref[...], nv_ref[...] = _adam_math(w_ref[...], g, m_ref[...], v_ref[...])

    specs = [_row(0, (w.shape[0] // steps, w.shape[1])) for w in ws]
    res = pl.pallas_call(
        body, name="adam_shards", grid=(steps,), in_specs=specs * 4, out_specs=specs * 4,
        out_shape=[jax.ShapeDtypeStruct(w.shape, f32) for w in ws] * 4,
        compiler_params=_cp("parallel"),
    )(*ws, *gs, *ms, *vs)
    return [tuple(res[j * n + k] for j in range(4)) for k in range(n)]


class _SmallGather:
    def __init__(self, parts, bufs, send_sems, recv_sems):
        self.parts, self.bufs, self.send_sems, self.recv_sems = parts, bufs, send_sems, recv_sems
        self.x, self.y, self.c, _, self.others = _place()
        self.sibling = (self.x, self.y, 1 - self.c)

    def _copy(self, a, k, block, to, src=None):
        slot = self.bufs[a].at[4 * block[0] + 2 * block[1] + block[2]]
        return _remote(slot if src is None else src, slot, self.send_sems.at[7 * a + k], self.recv_sems.at[7 * a + k],
                       to)

    def _first(self, a):
        me = (self.x, self.y, self.c)
        return [self._copy(a, 0, me, self.sibling, src=self.parts[a])] + [
            self._copy(a, 1 + j, me, (*chip, self.c), src=self.parts[a]) for j, chip in enumerate(self.others)]

    def _passed(self, a):
        return [self._copy(a, 4 + j, (*chip, self.c), self.sibling) for j, chip in enumerate(self.others)]

    @staticmethod
    def scratch(n):
        return [pltpu.SemaphoreType.DMA((7 * n,)), pltpu.SemaphoreType.DMA((7 * n,))]

    def start(self):
        for a in range(len(self.parts)):
            for cp in self._first(a):
                cp.start()

    def finish(self):
        sent = []
        for a in range(len(self.parts)):
            passed = self._passed(a)
            for j, chip in enumerate(self.others):
                self._copy(a, 1 + j, (*chip, self.c), self.sibling).wait_recv()
                passed[j].start()
            sent += self._first(a) + passed
        for a in range(len(self.parts)):
            self._copy(a, 0, self.sibling, self.sibling).wait_recv()
            for j, chip in enumerate(self.others):
                self._copy(a, 4 + j, (*chip, 1 - self.c), self.sibling).wait_recv()
        for cp in sent:
            cp.wait_send()


def _gathered_shapes(parts):
    return [jax.ShapeDtypeStruct((N_DEV,) + p.shape, p.dtype) for p in parts]


def _rs_sum_group(name, place, landed, parts, cross_parts, small_parts):
    nk, nx, ns = len(landed), len(cross_parts), len(small_parts)
    dims = [a.shape[1:] for a in landed]

    def body(place_ref, *refs):
        l_refs, p_refs = refs[:nk], refs[nk:2 * nk]
        x_refs, sp_refs = refs[2 * nk:2 * nk + nx], refs[2 * nk + nx:2 * nk + nx + ns]
        outs = refs[2 * nk + nx + ns:]
        o_refs, xl_refs, sbufs = outs[:nk], outs[nk:nk + nx], outs[nk + nx:nk + nx + ns]
        sems = outs[nk + nx + ns:]
        cross = _Cross(x_refs, xl_refs, sems[0], sems[1])
        small = _SmallGather(sp_refs, sbufs, sems[2], sems[3])
        t = pl.program_id(0)

        @pl.when(t == 0)
        def _():
            cross.start()
            small.start()

        me = place_ref[0]
        for l_ref, p_ref, o_ref in zip(l_refs, p_refs, o_refs):
            own = p_ref[0].astype(f32)
            acc = jnp.where(me == 0, own, l_ref[0].astype(f32))
            for j in range(1, N_CHIPS):
                acc = acc + jnp.where(me == j, own, l_ref[j].astype(f32))
            o_ref[...] = acc

        @pl.when(t == 1)
        def _():
            small.finish()
            cross.finish()

    def halves(h, c, lead, index):
        return pl.BlockSpec((lead, h // 2, c) if lead else (h // 2, c), index)

    in_specs = [halves(h, c, N_CHIPS, lambda t, pr: (0, t, 0)) for h, c in dims]
    in_specs += [halves(h, c, 1, lambda t, pr: (pr[0], t, 0)) for h, c in dims]
    in_specs += [ANY] * (nx + ns)
    out_specs = [halves(h, c, 0, lambda t, pr: (2 * pr[1] + t, 0)) for h, c in dims] + [ANY] * (nx + ns)
    out_shape = [jax.ShapeDtypeStruct((2 * h, c), f32) for h, c in dims]
    out_shape += [jax.ShapeDtypeStruct(a.shape, a.dtype) for a in cross_parts] + _gathered_shapes(small_parts)
    res = pl.pallas_call(
        body, name=f"rs_sum_{name}",
        grid_spec=pltpu.PrefetchScalarGridSpec(
            num_scalar_prefetch=1, grid=(2,), in_specs=in_specs, out_specs=out_specs,
            scratch_shapes=_Cross.scratch(nx) + _SmallGather.scratch(ns)),
        out_shape=out_shape, compiler_params=_cp("arbitrary"),
    )(place, *landed, *parts, *cross_parts, *small_parts)
    return res[:nk], res[nk:nk + nx], res[nk + nx:]


SMALL_PARAMS = ("g_ple_gate", "g_ple_post", "g_final", "g_ffn", "ln_g", "ln_b", "conv_b", "pool_scale", "pool_w",
                "conv_w", "g_mix")
SMALL_ROWS = {"g_ple_gate": (0, 0), "g_ple_post": (0, 1), "g_final": (0, 2), "g_ffn": (1, 0), "ln_g": (2, 0),
              "ln_b": (2, 1), "conv_b": (2, 2), "pool_scale": (2, 3), "g_mix": (5, 0)}
LOSS_ROW = (0, 3)


def _small_adam(place, gathered, parts, params):
    nb, names = len(parts), SMALL_PARAMS
    flat = [a for nm in names for a in params[nm]]

    def body(place_ref, *refs):
        b_refs, p_refs = refs[:nb], refs[nb:2 * nb]
        w_refs = refs[2 * nb:2 * nb + 3 * len(names)]
        outs = refs[2 * nb + 3 * len(names):]
        loss_ref, o_refs, cw_sum = outs[0], outs[1:1 + 4 * len(names)], outs[1 + 4 * len(names)]
        chip = place_ref[0]
        me = 2 * chip + place_ref[1]

        def total(blk, idx):
            own = p_refs[blk][idx]
            g = jnp.where(me == 0, own, b_refs[blk][(0,) + idx])
            for d in range(1, N_DEV):
                g = g + jnp.where(me == d, own, b_refs[blk][(d,) + idx])
            return g

        everything = (slice(None), slice(None))
        loss_ref[...] = total(LOSS_ROW[0], (pl.ds(LOSS_ROW[1], 1), pl.ds(0, 128)))
        d_cw = total(4, everything)
        mine = jnp.where(chip == 0, d_cw[:, 0:128], 0.0)
        for j in range(1, N_CHIPS):
            mine = mine + jnp.where(chip == j, d_cw[:, j * 128:(j + 1) * 128], 0.0)
        cw_sum[...] = mine
        for k, nm in enumerate(names):
            w_ref, m_ref, v_ref = w_refs[3 * k:3 * k + 3]
            g_ref, d_ref, nm_ref, nv_ref = o_refs[4 * k:4 * k + 4]
            if nm == "pool_w":
                g = total(3, everything + (slice(None),))
            elif nm == "conv_w":
                g = cw_sum[pl.ds(0, CONV_K), :]
            else:
                blk, row = SMALL_ROWS[nm]
                g = total(blk, (pl.ds(row, 1), slice(None)))
            g_ref[...] = g
            d_ref[...], nm_ref[...], nv_ref[...] = _adam_math(w_ref[...], g, m_ref[...], v_ref[...])

    whole = lambda a: pl.BlockSpec(a.shape, lambda t, pr: (0,) * a.ndim)
    out_shape = [jax.ShapeDtypeStruct((1, 128), f32)]
    out_shape += [jax.ShapeDtypeStruct(params[nm][0].shape, f32) for nm in names for _ in range(4)]
    res = pl.pallas_call(
        body, name="small_adam",
        grid_spec=pltpu.PrefetchScalarGridSpec(
            num_scalar_prefetch=1, grid=(1,),
            in_specs=[whole(a) for a in list(gathered) + list(parts) + flat],
            out_specs=[whole(s) for s in out_shape], scratch_shapes=[pltpu.VMEM((HALO, 128), f32)]),
        out_shape=out_shape, compiler_params=_cp("arbitrary"),
    )(place, *gathered, *parts, *flat)
    return res[0], {nm: res[1 + 4 * k:5 + 4 * k] for k, nm in enumerate(names)}


def _pad_rows(a, rows):
    return jnp.concatenate([a, jnp.zeros((rows - a.shape[0],) + a.shape[1:], a.dtype)], axis=0)


def kernel(x, p, g_mix, w_in, conv_w, conv_b, ln_g, ln_b, pool_w, pool_scale, w_out, g_ffn, w_gate_up, w_down, g_ple_gate, w_ple_gate, w_ple_up, g_ple_post, g_final, loss_target, m_g_mix, m_w_in, m_conv_w, m_conv_b, m_ln_g, m_ln_b, m_pool_w, m_pool_scale, m_w_out, m_g_ffn, m_w_gate_up, m_w_down, m_g_ple_gate, m_w_ple_gate, m_w_ple_up, m_g_ple_post, m_g_final, v_g_mix, v_w_in, v_conv_w, v_conv_b, v_ln_g, v_ln_b, v_pool_w, v_pool_scale, v_w_out, v_g_ffn, v_w_gate_up, v_w_down, v_g_ple_gate, v_w_ple_gate, v_w_ple_up, v_g_ple_post, v_g_final):
    seq = x.shape[1]
    me = 2 * lax.axis_index("x") + lax.axis_index("y")
    chip = me.astype(jnp.int32).reshape(1)
    core = lax.axis_index("c").astype(jnp.int32).reshape(1)
    place = jnp.concatenate([chip, core])
    xs, ps, ts = x.reshape(seq, D_MODEL), p.reshape(seq, D_PLE), loss_target.reshape(seq, D_MODEL)

    big = [w_in[0], w_gate_up[0], w_out[0], w_down[0], w_ple_gate[0], w_ple_up[0]]
    big_m = [m_w_in[0], m_w_gate_up[0], m_w_out[0], m_w_down[0], m_w_ple_gate[0], m_w_ple_up[0]]
    big_v = [v_w_in[0], v_w_gate_up[0], v_w_out[0], v_w_down[0], v_w_ple_gate[0], v_w_ple_up[0]]
    b_in, b_gu, b_out, b_down, b_pg, b_pu, b_cw = _cast_into_slots(
        chip, big + [_pad_rows(conv_w[0], HALO)], [bf16] * len(big) + [f32])
    xi, yi = lax.axis_index("x"), lax.axis_index("y")
    order = jnp.stack([me, 2 * (1 - xi) + yi, 2 * xi + 1 - yi, 2 * (1 - xi) + 1 - yi]).astype(jnp.int32)

    z, (w_in_g, cw_g, w_out_g) = _mix_in(xs, g_mix, order, [b_in, b_cw, b_out])
    conv_w_f = cw_g.transpose(1, 0, 2).reshape(HALO, C_CONV)
    w_out_f = w_out_g.reshape(D_MODEL, D_MODEL)
    (x1, mix, u1, pooled, h2), (w_gu_g, w_down_g) = _conv_pool_out(z, xs, conv_w_f, conv_b, ln_g, ln_b, pool_w[0],
                                                                    pool_scale, w_out_f, g_ffn, [b_gu, b_down])
    w_down_f = w_down_g.reshape(D_FF, D_MODEL)
    (x2, gu, ffn_f), (w_pg_g, w_pu_g) = _ffn_fwd(x1, h2, w_gu_g, w_down_f, [b_pg, b_pu])
    w_pg_f = w_pg_g.reshape(D_MODEL, D_MODEL)
    dx2, d_w_pg, d_w_pu, small_ple = _ple_loss(x2, ps, ts, g_ple_gate, g_ple_post, g_final.reshape(1, D_MODEL),
                                               w_pg_f, w_pu_g)
    d_w_down = _ffn_bwd_dw_down(ffn_f, dx2)
    parts_a = _pair_reduce("a", [d_w_pg.reshape(N_CHIPS, -1, D_MODEL), d_w_pu,
                                 d_w_down.reshape(N_CHIPS, -1, D_MODEL)])
    (dx1, dgu, small_ffn), landed_a = _ffn_bwd_dx(dx2, x1, gu, g_ffn, w_gu_g, w_down_f, parts_a)
    d_w_gu = _ffn_bwd_dw_gu(h2, dgu)
    du1, dpo, d_w_out, d_pool_w, small_mix = _mix_bwd_local(dx1, mix, u1, pooled, w_out_f, ln_g, ln_b, pool_w[0],
                                                             pool_scale)
    parts_b = _pair_reduce("b", [d_w_gu, d_w_out.reshape(N_CHIPS, -1, D_MODEL)])
    small_0 = [small_ple, small_ffn, small_mix, d_pool_w]
    (grad_x, d_w_in, d_conv_w, small_in), landed_b, small_all_0 = _in_bwd(du1, dpo, z, xs, dx1, conv_w_f, g_mix,
                                                                            w_in_g, parts_b, small_0)
    parts_c = _pair_reduce("c", [d_w_in])
    small_1 = [d_conv_w, small_in]
    (h_gu, h_out, h_down, h_pg, h_pu), landed_c, small_all_1 = _rs_sum_group(
        "ab", place, [landed_b[0], landed_b[1], landed_a[2], landed_a[0], landed_a[1]],
        [parts_b[0], parts_b[1], parts_a[2], parts_a[0], parts_a[1]], parts_c, small_1)
    h_in = _rs_sum_chips("w_in", place, landed_c[0], parts_c[0])
    big_g = _grad_pair([h_in, h_gu, h_out, h_down, h_pg, h_pu])
    big_upd = _adam(big, big_g, big_m, big_v)

    p3 = lambda w, m, v: (w, m, v)
    row = lambda a: a.reshape(1, D_MODEL)
    params = dict(
        g_ple_gate=p3(g_ple_gate, m_g_ple_gate, v_g_ple_gate), g_ple_post=p3(g_ple_post, m_g_ple_post, v_g_ple_post),
        g_final=p3(row(g_final), row(m_g_final), row(v_g_final)), g_ffn=p3(g_ffn, m_g_ffn, v_g_ffn),
        ln_g=p3(ln_g, m_ln_g, v_ln_g), ln_b=p3(ln_b, m_ln_b, v_ln_b), conv_b=p3(conv_b, m_conv_b, v_conv_b),
        pool_scale=p3(pool_scale, m_pool_scale, v_pool_scale), pool_w=p3(pool_w[0], m_pool_w[0], v_pool_w[0]),
        conv_w=p3(conv_w[0], m_conv_w[0], v_conv_w[0]), g_mix=p3(g_mix, m_g_mix, v_g_mix))
    loss, small = _small_adam(place, list(small_all_0) + list(small_all_1), small_0 + small_1, params)
    back = dict(g_final=lambda a: a.reshape(D_MODEL), pool_w=lambda a: a[None], conv_w=lambda a: a[None])

    names = ["g_mix", "w_in", "conv_w", "conv_b", "ln_g", "ln_b", "pool_w", "pool_scale", "w_out", "g_ffn",
             "w_gate_up", "w_down", "g_ple_gate", "w_ple_gate", "w_ple_up", "g_ple_post", "g_final"]
    big_at = {"w_in": 0, "w_gate_up": 1, "w_out": 2, "w_down": 3, "w_ple_gate": 4, "w_ple_up": 5}
    out = [loss[0, 0], grad_x.reshape(1, seq, D_MODEL)]
    for kind in range(4):
        for nm in names:
            if nm in big_at:
                out.append(big_upd[big_at[nm]][kind][None])
            else:
                out.append(back.get(nm, lambda a: a)(small[nm][kind]))
    return tuple(out)
```

```python
import functools

import jax
import jax.numpy as jnp
from jax import lax
from jax.experimental import pallas as pl
from jax.experimental.pallas import tpu as pltpu

f32, bf16 = jnp.float32, jnp.bfloat16

EPS = 1e-6
D_MODEL = 1024
C_CONV = 512
C_POOL = 512
POOL_WINDOWS = (2, 4, 8, 16)
POOL_GROUP = 128
CONV_K = 31
D_FF = 2816
D_PLE = 256
N_CHIPS = 4
N_DEV = 8
W_IN_COLS = 2 * C_CONV + C_POOL
W_IN_CHUNK = W_IN_COLS // N_CHIPS
FF_CHUNK = 2 * D_FF // N_CHIPS
PLE_CHUNK = D_MODEL // N_CHIPS
HALO = 32
ROW_TILE = 512
CONV_ROWS = 64
CONV_COLS = (slice(0, 256), slice(256, 512))
SHIFT_PAD = 32
SHIFT_GROUPS = 16
FF_SUB = (0, 512, 1024, FF_CHUNK)
VMEM_LIMIT = 56 * 1024 * 1024

ADAM_LR = 0.001
ADAM_B1 = 0.9
ADAM_B2 = 0.999
ADAM_EPS = 1e-08
ADAM_WD = 0.01
ADAM_STEP = 10

MESH = pl.DeviceIdType.MESH
ANY = pl.BlockSpec(memory_space=pl.ANY)
VMEM = pl.BlockSpec(memory_space=pltpu.VMEM)


def _cp(*sem):
    return pltpu.CompilerParams(dimension_semantics=sem, vmem_limit_bytes=VMEM_LIMIT)


def _dot(a, b):
    return jnp.dot(a, b, preferred_element_type=f32)


def _dot_nt(a, b):
    return lax.dot_general(a, b, (((1,), (1,)), ((), ())), preferred_element_type=f32)


def _dot_tn(a, b):
    return lax.dot_general(a, b, (((0,), (0,)), ((), ())), preferred_element_type=f32)


def _sigmoid(v):
    return jax.nn.sigmoid(v)


def _rms_fwd(v, g):
    r = lax.rsqrt(jnp.mean(v * v, axis=-1, keepdims=True) + EPS)
    vh = v * r
    return vh * g, vh, r


def _rms_bwd(dy, vh, r, g):
    dvh = dy * g
    dv = r * (dvh - vh * jnp.mean(dvh * vh, axis=-1, keepdims=True))
    return dv, jnp.sum(dy * vh, axis=0, keepdims=True)


def _silu_grad(v, s):
    return s * (1.0 + v * (1.0 - s))


def _row(i, n):
    return pl.BlockSpec((n[0], n[1]), lambda *a: (a[i], 0))


def _full(shape):
    nd = len(shape)
    return pl.BlockSpec(shape, lambda *a: (0,) * nd)


def _place():
    x, y, c = lax.axis_index("x"), lax.axis_index("y"), lax.axis_index("c")
    others = [(1 - x, y), (x, 1 - y), (1 - x, 1 - y)]
    return x, y, c, 2 * x + y, others


def _remote(src, dst, send_sem, recv_sem, dev):
    return pltpu.make_async_remote_copy(src_ref=src, dst_ref=dst, send_sem=send_sem, recv_sem=recv_sem,
                                        device_id=dev, device_id_type=MESH)


SHARD_STEPS = 4


def _cast_into_slots(me, ws, dtypes):
    n = len(ws)

    def body(me_ref, *refs):
        for w_ref, o_ref, dtype in zip(refs[:n], refs[n:], dtypes):
            o_ref[0] = w_ref[...].astype(dtype)

    return pl.pallas_call(
        body, name="cast_shards",
        grid_spec=pltpu.PrefetchScalarGridSpec(
            num_scalar_prefetch=1, grid=(SHARD_STEPS,),
            in_specs=[pl.BlockSpec((w.shape[0] // SHARD_STEPS, w.shape[1]), lambda r, me_ref: (r, 0)) for w in ws],
            out_specs=[pl.BlockSpec((1, w.shape[0] // SHARD_STEPS, w.shape[1]), lambda r, me_ref: (me_ref[0], r, 0))
                       for w in ws]),
        out_shape=[jax.ShapeDtypeStruct((N_CHIPS,) + w.shape, dt) for w, dt in zip(ws, dtypes)],
        compiler_params=_cp("parallel"),
    )(me, *ws)


class _Gather:
    def __init__(self, bufs, send_sems, recv_sems):
        self.bufs, self.send_sems, self.recv_sems = bufs, send_sems, recv_sems
        self.x, self.y, self.c, self.me, self.others = _place()
        self.halves = [b.shape[1] // 2 for b in bufs]

    def _piece(self, k, chip, half):
        return self.bufs[k].at[chip, pl.ds(half * self.halves[k], self.halves[k]), :]

    def _ici(self, k, j, chip):
        ox, oy = self.others[j]
        piece = self._piece(k, chip, self.c)
        return _remote(piece, piece, self.send_sems.at[6 * k + j], self.recv_sems.at[6 * k + j], (ox, oy, self.c))

    def _relay(self, k):
        first = self.c == 0
        piece = self._piece(k, jnp.where(first, self.chip(0), self.chip(1)), self.c)
        to = (jnp.where(first, self.others[1][0], self.others[0][0]),
              jnp.where(first, self.others[1][1], self.others[0][1]), self.c)
        return _remote(piece, piece, self.send_sems.at[6 * k + 2], self.recv_sems.at[6 * k + 2], to)

    def _pair(self, k, j, half):
        ox, oy = self.others[j]
        piece = self._piece(k, 2 * ox + oy, half)
        return _remote(piece, piece, self.send_sems.at[6 * k + 3 + j], self.recv_sems.at[6 * k + 3 + j],
                       (self.x, self.y, 1 - self.c))

    def _each(self, ks=None):
        return [(k, j) for k in (range(len(self.bufs)) if ks is None else ks) for j in range(3)]

    def chip(self, j):
        ox, oy = self.others[j]
        return 2 * ox + oy

    def start(self):
        for k in range(len(self.bufs)):
            for j in range(2):
                self._ici(k, j, self.me).start()

    def forward(self, pairs=None):
        for k, j in self._each() if pairs is None else pairs:
            self._ici(k, j, self.chip(j)).wait_recv()
            self._pair(k, j, self.c).start()
            if j < 2:
                pl.when(self.c == j)(self._relay(k).start)

    def landed(self, pairs):
        for k, j in pairs:
            self._pair(k, j, 1 - self.c).wait_recv()

    def finish(self, ks=None):
        self.landed(self._each(ks))
        for k in range(len(self.bufs)):
            for j in range(2):
                self._ici(k, j, self.me).wait_send()
            self._relay(k).wait_send()
            for j in range(3):
                self._pair(k, j, self.c).wait_send()

    @staticmethod
    def scratch(n):
        return [pltpu.SemaphoreType.DMA((6 * n,)), pltpu.SemaphoreType.DMA((6 * n,))]


def _carried(bufs):
    n = len(bufs)
    return dict(in_specs=[ANY] * n, out_specs=[ANY] * n,
                out_shape=[jax.ShapeDtypeStruct(b.shape, b.dtype) for b in bufs], scratch=_Gather.scratch(n))


def _mix_in(x, g_mix, order, carry):
    s = x.shape[0]
    tm = min(2 * ROW_TILE, s)
    n = s // tm
    nc = len(carry)
    cs = _carried(carry)

    def body(order_ref, x_ref, g_ref, *refs):
        z_ref = refs[nc]
        bufs = refs[nc + 1:2 * nc + 1]
        h_ref, w_ref, w_sem = refs[2 * nc + 1:2 * nc + 4]
        gather = _Gather(bufs, *refs[2 * nc + 4:])
        q, i = pl.program_id(0), pl.program_id(1)
        first = i == 0
        pl.when(jnp.logical_and(q == 0, first))(gather.start)
        for j in range(3):

            @pl.when(jnp.logical_and(q == j + 1, first))
            def _():
                gather.forward([(0, j)])
                gather.landed([(0, j)])
                if j == 1:
                    gather.forward([(k, jj) for k in range(1, nc) for jj in range(2)])

        @pl.when(first)
        def _():
            load = pltpu.make_async_copy(bufs[0].at[order_ref[q]], w_ref, w_sem)
            load.start()
            load.wait()

        @pl.when(q == 0)
        def _():
            h, _, _ = _rms_fwd(x_ref[...], g_ref[...])
            h_ref[i] = h.astype(bf16)

        z_ref[...] = _dot(h_ref[i], w_ref[...])

        @pl.when(jnp.logical_and(q == N_CHIPS - 1, i == n - 1))
        def _():
            gather.forward([(k, 2) for k in range(1, nc)])
            gather.finish(range(1, nc))

    res = pl.pallas_call(
        body, name="mix_in",
        grid_spec=pltpu.PrefetchScalarGridSpec(
            num_scalar_prefetch=1, grid=(N_CHIPS, n),
            in_specs=[pl.BlockSpec((tm, D_MODEL), lambda q, i, order_ref: (jnp.where(q == 0, i, 0), 0)),
                      pl.BlockSpec((1, D_MODEL), lambda q, i, order_ref: (0, 0))] + cs["in_specs"],
            out_specs=[pl.BlockSpec((tm, W_IN_CHUNK), lambda q, i, order_ref: (i, order_ref[q]))] + cs["out_specs"],
            scratch_shapes=[pltpu.VMEM((n, tm, D_MODEL), bf16), pltpu.VMEM((D_MODEL, W_IN_CHUNK), bf16),
                            pltpu.SemaphoreType.DMA(())] + cs["scratch"]),
        out_shape=[jax.ShapeDtypeStruct((s, W_IN_COLS), f32)] + cs["out_shape"],
        input_output_aliases={3 + k: 1 + k for k in range(nc)},
        compiler_params=_cp("arbitrary", "arbitrary"),
    )(order, x, g_mix, *carry)
    return res[0], res[1:]


def _tap_offsets(lo, hi):
    groups = [[o for o in range(lo, hi + 1) if o % 8 == s] for s in range(8)]
    return [g for g in groups if g]


def _tap_sum(buf, w_ref, row0, cols, tap_of):
    acc = jnp.zeros((CONV_ROWS, cols.stop - cols.start), f32)
    for offs in _tap_offsets(0, CONV_K - 1):
        slab = buf[pl.ds(row0 + offs[0], offs[-1] - offs[0] + CONV_ROWS), cols]
        for o in offs:
            acc = acc + w_ref[pl.ds(tap_of(o), 1), cols] * slab[o - offs[0]:o - offs[0] + CONV_ROWS]
    return acc


def _pool_counts(tm, w, first_row):
    t1 = (lax.broadcasted_iota(jnp.int32, (tm, 1), 0) + first_row + 1).astype(f32)
    return jnp.minimum(t1, float(w))


def _conv_pool_out(z, x, conv_w, conv_b, ln_g, ln_b, pool_w, pool_scale, w_out, g_ffn, carry):
    s = x.shape[0]
    tm = min(ROW_TILE, s)
    n = s // tm
    hb = tm // HALO
    nc = len(carry)
    cs = _carried(carry)

    def body(z_ref, zp_ref, x_ref, cw_ref, cb_ref, lg_ref, lb_ref, pw_ref, ps_ref, wo_ref, gf_ref, *refs):
        x1_ref, mix_ref, u1_ref, pooled_ref, h2_ref = refs[nc:nc + 5]
        ubuf, vbuf = refs[2 * nc + 5:2 * nc + 7]
        gather = _Gather(refs[nc + 5:2 * nc + 5], *refs[2 * nc + 7:])
        i = pl.program_id(0)
        pl.when(i == 0)(gather.start)
        for k in range(nc):
            pl.when(i == min(n // 2 + 2 * k, n - 1))(functools.partial(gather.forward, [(k, 0), (k, 1)]))
        keep = (i > 0).astype(f32)
        zp = zp_ref[...] * keep
        ubuf[0:HALO, :] = zp[:, :C_CONV] * _sigmoid(zp[:, C_CONV:2 * C_CONV])
        vbuf[0:HALO, :] = zp[:, 2 * C_CONV:]
        ubuf[HALO:, :] = z_ref[:, :C_CONV] * _sigmoid(z_ref[:, C_CONV:2 * C_CONV])
        vbuf[HALO:, :] = z_ref[:, 2 * C_CONV:]
        off = HALO - (CONV_K - 1)
        for r0 in range(0, tm, CONV_ROWS):
            for cols in CONV_COLS:
                u1_ref[r0:r0 + CONV_ROWS, cols] = cb_ref[:, cols] + _tap_sum(ubuf, cw_ref, r0 + off, cols, lambda o: o)
        u1 = u1_ref[...]
        mu = jnp.mean(u1, axis=-1, keepdims=True)
        uc = u1 - mu
        rstd = lax.rsqrt(jnp.mean(uc * uc, axis=-1, keepdims=True) + EPS)
        u2 = uc * rstd * lg_ref[...] + lb_ref[...]
        mix_ref[:, :C_CONV] = (u2 * _sigmoid(u2)).astype(bf16)
        for g, w in enumerate(POOL_WINDOWS):
            cols = slice(g * POOL_GROUP, (g + 1) * POOL_GROUP)
            acc = vbuf[pl.ds(HALO, tm), cols]
            vg = acc
            for d in range(1, w):
                acc = acc + vbuf[pl.ds(HALO - d, tm), cols]
            pooled = (acc / _pool_counts(tm, w, i * tm) - vg).astype(bf16)
            pooled_ref[:, cols] = pooled
            mixed = _dot(pooled, pw_ref[g].astype(bf16))
            mix_ref[:, C_CONV + g * POOL_GROUP:C_CONV + (g + 1) * POOL_GROUP] = (mixed * ps_ref[:, cols]).astype(bf16)
        x1 = x_ref[...] + _dot(mix_ref[...], wo_ref[...])
        x1_ref[...] = x1
        h2_ref[...] = _rms_fwd(x1, gf_ref[...])[0].astype(bf16)
        @pl.when(i == n - 1)
        def _():
            gather.forward([(k, 2) for k in range(nc)])
            gather.finish()

    res = pl.pallas_call(
        body, name="conv_pool_out", grid=(n,),
        in_specs=[_row(0, (tm, W_IN_COLS)),
                  pl.BlockSpec((HALO, W_IN_COLS), lambda i: (jnp.maximum(i * hb - 1, 0), 0)),
                  _row(0, (tm, D_MODEL)), _full((HALO, C_CONV)), _full((1, C_CONV)), _full((1, C_CONV)),
                  _full((1, C_CONV)), _full((4, POOL_GROUP, POOL_GROUP)), _full((1, C_POOL)),
                  _full((D_MODEL, D_MODEL)), _full((1, D_MODEL))] + cs["in_specs"],
        out_specs=[_row(0, (tm, D_MODEL)), _row(0, (tm, D_MODEL)), _row(0, (tm, C_CONV)), _row(0, (tm, C_POOL)),
                   _row(0, (tm, D_MODEL))] + cs["out_specs"],
        out_shape=[jax.ShapeDtypeStruct((s, D_MODEL), f32), jax.ShapeDtypeStruct((s, D_MODEL), bf16),
                   jax.ShapeDtypeStruct((s, C_CONV), f32), jax.ShapeDtypeStruct((s, C_POOL), bf16),
                   jax.ShapeDtypeStruct((s, D_MODEL), bf16)] + cs["out_shape"],
        input_output_aliases={11 + k: 5 + k for k in range(nc)},
        scratch_shapes=[pltpu.VMEM((HALO + tm, C_CONV), f32), pltpu.VMEM((HALO + tm, C_POOL), f32)] + cs["scratch"],
        compiler_params=_cp("arbitrary"),
    )(z, z, x, conv_w, conv_b, ln_g, ln_b, pool_w, pool_scale, w_out, g_ffn, *carry)
    return res[:5], res[5:]


def _ffn_fwd(x1, h2, w_gu_g, w_down, carry):
    s = x1.shape[0]
    tm = min(ROW_TILE, s)
    n = s // tm
    nc = len(carry)
    cs = _carried(carry)

    def body(x1_ref, h2_ref, wg_ref, wu_ref, wd_ref, *refs):
        x2_ref, gu_ref, f_ref = refs[nc:nc + 3]
        acc_ref = refs[2 * nc + 3]
        gather = _Gather(refs[nc + 3:2 * nc + 3], *refs[2 * nc + 4:])
        i, c = pl.program_id(0), pl.program_id(1)
        pl.when(jnp.logical_and(i == 0, c == 0))(gather.start)
        direct = [(k, j) for k in range(nc) for j in range(2)]
        pl.when(jnp.logical_and(i == n // 2, c == 0))(functools.partial(gather.forward, direct))
        pl.when(jnp.logical_and(i == n - 1, c == 0))(functools.partial(gather.forward, [(k, 2) for k in range(nc)]))

        @pl.when(c == 0)
        def _():
            acc_ref[...] = jnp.zeros_like(acc_ref)

        h = h2_ref[...]
        for lo, hi in zip(FF_SUB[:-1], FF_SUB[1:]):
            gate = _dot(h, wg_ref[0, :, lo:hi])
            up = _dot(h, wu_ref[0, :, lo:hi])
            gu_ref[0, :, lo:hi] = gate.astype(bf16)
            gu_ref[1, :, lo:hi] = up.astype(bf16)
            f = (gate * _sigmoid(gate) * up).astype(bf16)
            f_ref[:, lo:hi] = f
            acc_ref[...] += _dot(f, wd_ref[lo:hi, :])

        @pl.when(c == 1)
        def _():
            x2_ref[...] = x1_ref[...] + acc_ref[...]

        pl.when(jnp.logical_and(i == n - 1, c == 1))(gather.finish)

    res = pl.pallas_call(
        body, name="ffn_fwd", grid=(n, 2),
        in_specs=[_row(0, (tm, D_MODEL)), _row(0, (tm, D_MODEL)),
                  pl.BlockSpec((1, D_MODEL, FF_CHUNK), lambda i, c: (c, 0, 0)),
                  pl.BlockSpec((1, D_MODEL, FF_CHUNK), lambda i, c: (2 + c, 0, 0)),
                  pl.BlockSpec((FF_CHUNK, D_MODEL), lambda i, c: (c, 0))] + cs["in_specs"],
        out_specs=[_row(0, (tm, D_MODEL)), pl.BlockSpec((2, tm, FF_CHUNK), lambda i, c: (0, i, c)),
                   pl.BlockSpec((tm, FF_CHUNK), lambda i, c: (i, c))] + cs["out_specs"],
        out_shape=[jax.ShapeDtypeStruct((s, D_MODEL), f32), jax.ShapeDtypeStruct((2, s, D_FF), bf16),
                   jax.ShapeDtypeStruct((s, D_FF), bf16)] + cs["out_shape"],
        input_output_aliases={5 + k: 3 + k for k in range(nc)},
        scratch_shapes=[pltpu.VMEM((tm, D_MODEL), f32)] + cs["scratch"],
        compiler_params=_cp("arbitrary", "arbitrary"),
    )(x1, h2, w_gu_g, w_gu_g, w_down, *carry)
    return res[:3], res[3:]


def _ple_loss(x2, p, target, g_pg, g_post, g_final, w_pg, w_pu_g):
    s = x2.shape[0]
    tm = min(ROW_TILE, s)
    n = s // tm

    def body(x2_ref, p_ref, t_ref, gpg_ref, gpo_ref, gf_ref, wpg_ref, wpu_ref,
             dx2_ref, dwpg_ref, dwpu_ref, small_ref, apg_ref, apu_ref):
        i = pl.program_id(0)

        @pl.when(i == 0)
        def _():
            apg_ref[...] = jnp.zeros_like(apg_ref)
            apu_ref[...] = jnp.zeros_like(apu_ref)
            small_ref[...] = jnp.zeros_like(small_ref)

        x2 = x2_ref[...]
        h3, x2h, r2 = _rms_fwd(x2, gpg_ref[...])
        h3b = h3.astype(bf16)
        gate = _sigmoid(_dot(h3b, wpg_ref[...]))
        pb = p_ref[...].astype(bf16)
        pe = jnp.concatenate([_dot(pb, wpu_ref[j]) for j in range(N_CHIPS)], axis=-1)
        e, peh, rp = _rms_fwd(pe, gpo_ref[...])
        x3 = x2 + gate * e
        y, x3h, r3 = _rms_fwd(x3, gf_ref[...])
        d = y - t_ref[...]
        loss = 0.5 * jnp.sum(jnp.sum(d * d, axis=-1, keepdims=True) * (1.0 / D_MODEL), axis=0, keepdims=True)
        dx3, dgf = _rms_bwd(d * (1.0 / D_MODEL), x3h, r3, gf_ref[...])
        dpe, dgpo = _rms_bwd(dx3 * gate, peh, rp, gpo_ref[...])
        dgl = (dx3 * e * gate * (1.0 - gate)).astype(bf16)
        apg_ref[...] += _dot_tn(h3b, dgl)
        apu_ref[...] += _dot_tn(pb, dpe.astype(bf16))
        dh3 = _dot_nt(dgl, wpg_ref[...])
        dx2b, dgpg = _rms_bwd(dh3, x2h, r2, gpg_ref[...])
        dx2_ref[...] = dx3 + dx2b
        small_ref[0:1, :] += dgpg
        small_ref[1:2, :] += dgpo
        small_ref[2:3, :] += dgf
        small_ref[3:4, :] += jnp.broadcast_to(loss, (1, D_MODEL))

        @pl.when(i == n - 1)
        def _():
            dwpg_ref[...] = apg_ref[...].astype(bf16)
            for j in range(N_CHIPS):
                dwpu_ref[j] = apu_ref[:, j * PLE_CHUNK:(j + 1) * PLE_CHUNK].astype(bf16)

    return pl.pallas_call(
        body, name="ple_loss", grid=(n,),
        in_specs=[_row(0, (tm, D_MODEL)), _row(0, (tm, D_PLE)), _row(0, (tm, D_MODEL)),
                  _full((1, D_MODEL)), _full((1, D_MODEL)), _full((1, D_MODEL)),
                  _full((D_MODEL, D_MODEL)), _full((N_CHIPS, D_PLE, PLE_CHUNK))],
        out_specs=[_row(0, (tm, D_MODEL)), _full((D_MODEL, D_MODEL)), _full((N_CHIPS, D_PLE, PLE_CHUNK)),
                   _full((8, D_MODEL))],
        out_shape=[jax.ShapeDtypeStruct((s, D_MODEL), f32), jax.ShapeDtypeStruct((D_MODEL, D_MODEL), bf16),
                   jax.ShapeDtypeStruct((N_CHIPS, D_PLE, PLE_CHUNK), bf16), jax.ShapeDtypeStruct((8, D_MODEL), f32)],
        scratch_shapes=[pltpu.VMEM((D_MODEL, D_MODEL), f32), pltpu.VMEM((D_PLE, D_MODEL), f32)],
        compiler_params=_cp("arbitrary"),
    )(x2, p, target, g_pg, g_post, g_final, w_pg, w_pu_g)


def _crossed(parts):
    n = len(parts)
    return dict(in_specs=[ANY] * n, out_specs=[ANY] * n,
                out_shape=[jax.ShapeDtypeStruct(a.shape, a.dtype) for a in parts], scratch=_Cross.scratch(n))


def _ffn_bwd_dx(dx2, x1, gu, g_ffn, w_gu_g, w_down, grads):
    s = x1.shape[0]
    tm = min(ROW_TILE, s)
    n = s // tm
    nc = len(grads)
    cs = dict(in_specs=[ANY] * nc, out_specs=[ANY] * nc, scratch=_CrossAll.scratch(nc),
              out_shape=[jax.ShapeDtypeStruct((N_DEV, g.shape[1] // 2, g.shape[2]), g.dtype) for g in grads])

    def body(dx2_ref, x1_ref, gu_ref, g_ref, wg_ref, wu_ref, wd_ref, *refs):
        dx1_ref, dgu_ref, small_ref = refs[nc:nc + 3]
        acc_ref = refs[2 * nc + 3]
        cross = _CrossAll(refs[:nc], refs[nc + 3:2 * nc + 3], *refs[2 * nc + 4:])
        i, c = pl.program_id(0), pl.program_id(1)
        pl.when(jnp.logical_and(i == 0, c == 0))(cross.start)

        @pl.when(jnp.logical_and(i == 0, c == 0))
        def _():
            small_ref[...] = jnp.zeros_like(small_ref)

        @pl.when(c == 0)
        def _():
            acc_ref[...] = jnp.zeros_like(acc_ref)

        dyb = dx2_ref[...].astype(bf16)
        for lo, hi in zip(FF_SUB[:-1], FF_SUB[1:]):
            df = _dot_nt(dyb, wd_ref[lo:hi, :])
            gate = gu_ref[0, :, lo:hi].astype(f32)
            up = gu_ref[1, :, lo:hi].astype(f32)
            sg = _sigmoid(gate)
            dgate = (df * up * _silu_grad(gate, sg)).astype(bf16)
            dup = (df * gate * sg).astype(bf16)
            dgu_ref[0, :, lo:hi] = dgate
            dgu_ref[1, :, lo:hi] = dup
            acc_ref[...] += _dot_nt(dgate, wg_ref[0, :, lo:hi]) + _dot_nt(dup, wu_ref[0, :, lo:hi])

        @pl.when(c == 1)
        def _():
            _, x1h, r1 = _rms_fwd(x1_ref[...], g_ref[...])
            dx1b, dg = _rms_bwd(acc_ref[...], x1h, r1, g_ref[...])
            dx1_ref[...] = dx2_ref[...] + dx1b
            small_ref[0:1, :] += dg

        pl.when(jnp.logical_and(i == n - 1, c == 1))(cross.finish)

    res = pl.pallas_call(
        body, name="ffn_bwd_dx", grid=(n, 2),
        in_specs=[_row(0, (tm, D_MODEL)), _row(0, (tm, D_MODEL)),
                  pl.BlockSpec((2, tm, FF_CHUNK), lambda i, c: (0, i, c)), _full((1, D_MODEL)),
                  pl.BlockSpec((1, D_MODEL, FF_CHUNK), lambda i, c: (c, 0, 0)),
                  pl.BlockSpec((1, D_MODEL, FF_CHUNK), lambda i, c: (2 + c, 0, 0)),
                  pl.BlockSpec((FF_CHUNK, D_MODEL), lambda i, c: (c, 0))] + cs["in_specs"],
        out_specs=[_row(0, (tm, D_MODEL)), pl.BlockSpec((2, tm, FF_CHUNK), lambda i, c: (0, i, c)),
                   _full((8, D_MODEL))] + cs["out_specs"],
        out_shape=[jax.ShapeDtypeStruct((s, D_MODEL), f32), jax.ShapeDtypeStruct((2, s, D_FF), bf16),
                   jax.ShapeDtypeStruct((8, D_MODEL), f32)] + cs["out_shape"],
        scratch_shapes=[pltpu.VMEM((tm, D_MODEL), f32)] + cs["scratch"],
        compiler_params=_cp("arbitrary", "arbitrary"),
    )(dx2, x1, gu, g_ffn, w_gu_g, w_gu_g, w_down, *grads)
    return res[:3], res[3:]


def _ffn_bwd_dw_gu(h2, dgu):
    s = h2.shape[0]
    ts = min(2 * ROW_TILE, s)
    n = s // ts

    def body(h_ref, d_ref, o_ref, acc_ref):
        t = pl.program_id(1)

        @pl.when(t == 0)
        def _():
            acc_ref[...] = jnp.zeros_like(acc_ref)

        acc_ref[...] += _dot_tn(h_ref[...], d_ref[0])

        @pl.when(t == n - 1)
        def _():
            o_ref[0] = acc_ref[...].astype(bf16)

    return pl.pallas_call(
        body, name="ffn_bwd_dw_gu", grid=(N_CHIPS, n),
        in_specs=[pl.BlockSpec((ts, D_MODEL), lambda j, t: (t, 0)),
                  pl.BlockSpec((1, ts, FF_CHUNK), lambda j, t: (j // 2, t, j % 2))],
        out_specs=pl.BlockSpec((1, D_MODEL, FF_CHUNK), lambda j, t: (j, 0, 0)),
        out_shape=jax.ShapeDtypeStruct((N_CHIPS, D_MODEL, FF_CHUNK), bf16),
        scratch_shapes=[pltpu.VMEM((D_MODEL, FF_CHUNK), f32)],
        compiler_params=_cp("parallel", "arbitrary"),
    )(h2, dgu)


def _ffn_bwd_dw_down(f, dx2):
    s = dx2.shape[0]
    ts = min(2 * ROW_TILE, s)
    n = s // ts

    def body(f_ref, d_ref, o_ref, acc_ref):
        t = pl.program_id(1)

        @pl.when(t == 0)
        def _():
            acc_ref[...] = jnp.zeros_like(acc_ref)

        acc_ref[...] += _dot_tn(f_ref[...], d_ref[...].astype(bf16))

        @pl.when(t == n - 1)
        def _():
            o_ref[...] = acc_ref[...].astype(bf16)

    return pl.pallas_call(
        body, name="ffn_bwd_dw_down", grid=(2, n),
        in_specs=[pl.BlockSpec((ts, FF_CHUNK), lambda c, t: (t, c)),
                  pl.BlockSpec((ts, D_MODEL), lambda c, t: (t, 0))],
        out_specs=pl.BlockSpec((FF_CHUNK, D_MODEL), lambda c, t: (c, 0)),
        out_shape=jax.ShapeDtypeStruct((D_FF, D_MODEL), bf16),
        scratch_shapes=[pltpu.VMEM((FF_CHUNK, D_MODEL), f32)],
        compiler_params=_cp("parallel", "arbitrary"),
    )(f, dx2)


def _mix_bwd_local(dx1, mix, u1, pooled, w_out, ln_g, ln_b, pool_w, pool_scale):
    s = dx1.shape[0]
    tm = min(ROW_TILE, s)
    n = s // tm

    def body(dx1_ref, mix_ref, u1_ref, po_ref, wo_ref, lg_ref, lb_ref, pw_ref, ps_ref,
             du1_ref, dpo_ref, dwo_ref, dpw_ref, small_ref, awo_ref):
        i = pl.program_id(0)

        @pl.when(i == 0)
        def _():
            awo_ref[...] = jnp.zeros_like(awo_ref)
            dpw_ref[...] = jnp.zeros_like(dpw_ref)
            small_ref[...] = jnp.zeros_like(small_ref)

        dyb = dx1_ref[...].astype(bf16)
        dmix = _dot_nt(dyb, wo_ref[...])
        awo_ref[...] += _dot_tn(mix_ref[...], dyb)
        u1 = u1_ref[...]
        mu = jnp.mean(u1, axis=-1, keepdims=True)
        uc = u1 - mu
        rstd = lax.rsqrt(jnp.mean(uc * uc, axis=-1, keepdims=True) + EPS)
        uh = uc * rstd
        u2 = uh * lg_ref[...] + lb_ref[...]
        du2 = dmix[:, :C_CONV] * _silu_grad(u2, _sigmoid(u2))
        duh = du2 * lg_ref[...]
        du1 = rstd * (duh - jnp.mean(duh, axis=-1, keepdims=True) - uh * jnp.mean(duh * uh, axis=-1, keepdims=True))
        du1_ref[...] = du1
        small_ref[0:1, :] += jnp.sum(du2 * uh, axis=0, keepdims=True)
        small_ref[1:2, :] += jnp.sum(du2, axis=0, keepdims=True)
        small_ref[2:3, :] += jnp.sum(du1, axis=0, keepdims=True)
        for g in range(len(POOL_WINDOWS)):
            cols = slice(g * POOL_GROUP, (g + 1) * POOL_GROUP)
            dq = dmix[:, C_CONV + g * POOL_GROUP:C_CONV + (g + 1) * POOL_GROUP]
            pwb = pw_ref[g].astype(bf16)
            pg = po_ref[:, cols]
            mixed = _dot(pg, pwb)
            small_ref[3:4, cols] += jnp.sum(dq * mixed, axis=0, keepdims=True)
            dmixed = (dq * ps_ref[:, cols]).astype(bf16)
            dpw_ref[g] += _dot_tn(pg, dmixed)
            dpo_ref[:, cols] = _dot_nt(dmixed, pwb)

        @pl.when(i == n - 1)
        def _():
            dwo_ref[...] = awo_ref[...].astype(bf16)

    return pl.pallas_call(
        body, name="mix_bwd_local", grid=(n,),
        in_specs=[_row(0, (tm, D_MODEL)), _row(0, (tm, D_MODEL)), _row(0, (tm, C_CONV)), _row(0, (tm, C_POOL)),
                  _full((D_MODEL, D_MODEL)), _full((1, C_CONV)), _full((1, C_CONV)),
                  _full((4, POOL_GROUP, POOL_GROUP)), _full((1, C_POOL))],
        out_specs=[_row(0, (tm, C_CONV)), _row(0, (tm, C_POOL)), _full((D_MODEL, D_MODEL)),
                   _full((4, POOL_GROUP, POOL_GROUP)), _full((8, C_CONV))],
        out_shape=[jax.ShapeDtypeStruct((s, C_CONV), f32), jax.ShapeDtypeStruct((s, C_POOL), f32),
                   jax.ShapeDtypeStruct((D_MODEL, D_MODEL), bf16),
                   jax.ShapeDtypeStruct((4, POOL_GROUP, POOL_GROUP), f32), jax.ShapeDtypeStruct((8, C_CONV), f32)],
        scratch_shapes=[pltpu.VMEM((D_MODEL, D_MODEL), f32)],
        compiler_params=_cp("arbitrary"),
    )(dx1, mix, u1, pooled, w_out, ln_g, ln_b, pool_w, pool_scale)


def _in_bwd(du1, dpo, z, x, dx1, conv_w, g_mix, w_in_g, parts, small_parts):
    s = x.shape[0]
    tm = min(ROW_TILE, s)
    n = s // tm
    hb = tm // HALO
    last = s // HALO - 1
    nv = tm // 8
    assert nv >= SHIFT_PAD and nv % SHIFT_GROUPS == 0
    nc, ns = len(parts), len(small_parts)
    cs = _crossed(parts)

    def body(du_ref, dun_ref, dp_ref, dpn_ref, z_ref, zp_ref, x_ref, dx1_ref, cw_ref, g_ref, w_ref, *refs):
        outs = refs[nc + ns:]
        gx_ref, dw_ref, dcw_ref, small_ref = outs[:4]
        eu, ed, ep, ss, u0_ref, dz_ref, acc_ref, dcw_acc = outs[4 + nc + ns:12 + nc + ns]
        sems = outs[12 + nc + ns:]
        cross = _Cross(refs[:nc], outs[4:4 + nc], sems[0], sems[1])
        gather = _SmallGather(refs[nc:nc + ns], outs[4 + nc:4 + nc + ns], sems[2], sems[3])
        i = pl.program_id(0)
        pl.when(i == 0)(cross.start)
        pl.when(i == 0)(gather.start)

        @pl.when(i == 0)
        def _():
            acc_ref[...] = jnp.zeros_like(acc_ref)
            dcw_acc[...] = jnp.zeros_like(dcw_acc)
            small_ref[...] = jnp.zeros_like(small_ref)

        keep_prev = (i > 0).astype(f32)
        keep_next = (i < n - 1).astype(f32)
        zp = zp_ref[...] * keep_prev
        u0_prev = zp[:, :C_CONV] * _sigmoid(zp[:, C_CONV:2 * C_CONV])
        u0_ref[...] = z_ref[:, :C_CONV] * _sigmoid(z_ref[:, C_CONV:2 * C_CONV])
        du_next = dun_ref[...] * keep_next
        for c, w_pool in enumerate(POOL_WINDOWS):
            lanes = slice(c * 128, (c + 1) * 128)
            eu[c, pl.ds(0, SHIFT_PAD, stride=8), :] = u0_prev[:, lanes]
            ed[c, pl.ds(nv * 8 + 7, SHIFT_PAD, stride=8), :] = du_next[:, lanes]
            ep[c, pl.ds(nv * 8 + 7, SHIFT_PAD, stride=8), :] = (
                dpn_ref[:, lanes] * keep_next / _pool_counts(HALO, w_pool, (i + 1) * tm))
            for j in range(8):
                rows = slice(j * nv, (j + 1) * nv)
                eu[c, pl.ds(SHIFT_PAD * 8 + j, nv, stride=8), :] = u0_ref[rows, lanes]
                ed[c, pl.ds(j, nv, stride=8), :] = du_ref[rows, lanes]
                ep[c, pl.ds(j, nv, stride=8), :] = dp_ref[rows, lanes] / _pool_counts(nv, w_pool, i * tm + j * nv)
                if j >= 1:
                    eu[c, pl.ds(j, SHIFT_PAD, stride=8), :] = u0_ref[j * nv - SHIFT_PAD:j * nv, lanes]
                if j <= 6:
                    edge = slice((j + 1) * nv, (j + 1) * nv + SHIFT_PAD)
                    ed[c, pl.ds(nv * 8 + j, SHIFT_PAD, stride=8), :] = du_ref[edge, lanes]
                    ep[c, pl.ds(nv * 8 + j, SHIFT_PAD, stride=8), :] = (
                        dp_ref[edge, lanes] / _pool_counts(SHIFT_PAD, w_pool, i * tm + (j + 1) * nv))
        for c, w_pool in enumerate(POOL_WINDOWS):
            lanes = slice(c * 128, (c + 1) * 128)
            b_lanes = slice(C_CONV + c * 128, C_CONV + (c + 1) * 128)
            v_lanes = slice(2 * C_CONV + c * 128, 2 * C_CONV + (c + 1) * 128)
            for v0 in range(0, nv, SHIFT_GROUPS):
                span = SHIFT_GROUPS * 8
                acc = jnp.zeros((span, 128), f32)
                for k in range(CONV_K):
                    acc = acc + cw_ref[pl.ds(k, 1), lanes] * ed[c, pl.ds((v0 + CONV_K - 1 - k) * 8, span), :]
                ss[0, v0 * 8:v0 * 8 + span, :] = acc
                acc = ep[c, pl.ds(v0 * 8, span), :]
                for d in range(1, w_pool):
                    acc = acc + ep[c, pl.ds((v0 + d) * 8, span), :]
                ss[1, v0 * 8:v0 * 8 + span, :] = acc
                d1 = ed[c, pl.ds(v0 * 8, span), :]
                for k in range(CONV_K):
                    prod = d1 * eu[c, pl.ds((SHIFT_PAD - (CONV_K - 1) + v0 + k) * 8, span), :]
                    fold = prod[0:8]
                    for r in range(8, span, 8):
                        fold = fold + prod[r:r + 8]
                    dcw_acc[k, :, lanes] += fold
            for j in range(8):
                rows = slice(j * nv, (j + 1) * nv)
                du0 = ss[0, pl.ds(j, nv, stride=8), :]
                av, sv = z_ref[rows, lanes], _sigmoid(z_ref[rows, b_lanes])
                dz_ref[rows, lanes] = (du0 * sv).astype(bf16)
                dz_ref[rows, b_lanes] = (du0 * av * sv * (1.0 - sv)).astype(bf16)
                dz_ref[rows, v_lanes] = (ss[1, pl.ds(j, nv, stride=8), :] - dp_ref[rows, lanes]).astype(bf16)
        h, xh, r = _rms_fwd(x_ref[...], g_ref[...])
        dz = dz_ref[...]
        acc_ref[...] += _dot_tn(h.astype(bf16), dz)
        dh = _dot_nt(dz[:, 0:W_IN_CHUNK], w_ref[0])
        for j in range(1, N_CHIPS):
            dh = dh + _dot_nt(dz[:, j * W_IN_CHUNK:(j + 1) * W_IN_CHUNK], w_ref[j])
        dxb, dg = _rms_bwd(dh, xh, r, g_ref[...])
        gx_ref[...] = dx1_ref[...] + dxb
        small_ref[0:1, :] += dg

        @pl.when(i == n - 1)
        def _():
            for j in range(N_CHIPS):
                dw_ref[j] = acc_ref[:, j * W_IN_CHUNK:(j + 1) * W_IN_CHUNK].astype(bf16)
            dcw_ref[...] = jnp.sum(dcw_acc[...], axis=1)

        pl.when(i == n - 1)(gather.finish)
        pl.when(i == n - 1)(cross.finish)

    nxt = lambda i: (jnp.minimum((i + 1) * hb, last), 0)
    res = pl.pallas_call(
        body, name="in_bwd", grid=(n,),
        in_specs=[_row(0, (tm, C_CONV)), pl.BlockSpec((HALO, C_CONV), nxt),
                  _row(0, (tm, C_POOL)), pl.BlockSpec((HALO, C_POOL), nxt),
                  _row(0, (tm, W_IN_COLS)),
                  pl.BlockSpec((HALO, W_IN_COLS), lambda i: (jnp.maximum(i * hb - 1, 0), 0)),
                  _row(0, (tm, D_MODEL)), _row(0, (tm, D_MODEL)), _full((HALO, C_CONV)), _full((1, D_MODEL)),
                  _full((N_CHIPS, D_MODEL, W_IN_CHUNK))] + cs["in_specs"] + [ANY] * ns,
        out_specs=[_row(0, (tm, D_MODEL)), _full((N_CHIPS, D_MODEL, W_IN_CHUNK)), _full((HALO, C_CONV)),
                   _full((8, D_MODEL))] + cs["out_specs"] + [ANY] * ns,
        out_shape=[jax.ShapeDtypeStruct((s, D_MODEL), f32), jax.ShapeDtypeStruct((N_CHIPS, D_MODEL, W_IN_CHUNK), bf16),
                   jax.ShapeDtypeStruct((HALO, C_CONV), f32), jax.ShapeDtypeStruct((8, D_MODEL), f32)] + cs["out_shape"]
        + _gathered_shapes(small_parts),
        scratch_shapes=[pltpu.VMEM((4, (SHIFT_PAD + nv) * 8, 128), f32), pltpu.VMEM((4, (nv + SHIFT_PAD) * 8, 128), f32),
                        pltpu.VMEM((4, (nv + SHIFT_PAD) * 8, 128), f32), pltpu.VMEM((2, tm, 128), f32),
                        pltpu.VMEM((tm, C_CONV), f32), pltpu.VMEM((tm, W_IN_COLS), bf16),
                        pltpu.VMEM((D_MODEL, W_IN_COLS), f32), pltpu.VMEM((HALO, 8, C_CONV), f32)] + cs["scratch"]
        + _SmallGather.scratch(ns),
        compiler_params=_cp("arbitrary"),
    )(du1, du1, dpo, dpo, z, z, x, dx1, conv_w, g_mix, w_in_g, *parts, *small_parts)
    return res[:4], res[4:4 + nc], res[4 + nc:]


def _pair_reduce(name, grads):
    nk = len(grads)
    halves = [g.shape[1] // 2 for g in grads]

    def body(*refs):
        ins, outs, got = refs[:nk], refs[nk:2 * nk], refs[2 * nk:3 * nk]
        send_sems, recv_sems = refs[3 * nk:]
        x, y, c, _, _ = _place()

        def half(k, core):
            return pl.ds(pl.multiple_of(core * halves[k], 16), halves[k])

        cps = [_remote(ins[k].at[:, half(k, 1 - c), :], got[k], send_sems.at[k], recv_sems.at[k], (x, y, 1 - c))
               for k in range(nk)]
        for cp in cps:
            cp.start()
        for k, cp in enumerate(cps):
            cp.wait_recv()
            outs[k][...] = (ins[k][:, half(k, c), :].astype(f32) + got[k][...].astype(f32)).astype(bf16)
        for cp in cps:
            cp.wait_send()

    shapes = [(N_CHIPS, h, g.shape[2]) for g, h in zip(grads, halves)]
    return pl.pallas_call(
        body, name=f"pair_reduce_{name}", in_specs=[VMEM] * nk, out_specs=[VMEM] * nk,
        out_shape=[jax.ShapeDtypeStruct(s, bf16) for s in shapes],
        scratch_shapes=[pltpu.VMEM(s, bf16) for s in shapes]
        + [pltpu.SemaphoreType.DMA((nk,)), pltpu.SemaphoreType.DMA((nk,))],
        compiler_params=pltpu.CompilerParams(vmem_limit_bytes=VMEM_LIMIT),
    )(*grads)


class _Cross:
    def __init__(self, parts, landed, send_sems, recv_sems):
        self.parts, self.landed, self.send_sems, self.recv_sems = parts, landed, send_sems, recv_sems
        _, _, self.c, self.me, self.others = _place()

    def _copy(self, k, j, src_chunk, dst_slot):
        ox, oy = self.others[j]
        return _remote(self.parts[k].at[src_chunk], self.landed[k].at[dst_slot], self.send_sems.at[3 * k + j],
                       self.recv_sems.at[3 * k + j], (ox, oy, self.c))

    def _each(self):
        return [(k, j, 2 * self.others[j][0] + self.others[j][1]) for k in range(len(self.parts)) for j in range(3)]

    def start(self):
        for k, j, chip in self._each():
            self._copy(k, j, chip, self.me).start()

    def finish(self):
        for k, j, chip in self._each():
            self._copy(k, j, chip, chip).wait_recv()
        for k, j, chip in self._each():
            self._copy(k, j, chip, self.me).wait_send()

    @staticmethod
    def scratch(n):
        return [pltpu.SemaphoreType.DMA((3 * n,)), pltpu.SemaphoreType.DMA((3 * n,))]


class _CrossAll:
    def __init__(self, grads, landed, send_sems, recv_sems):
        self.grads, self.landed, self.send_sems, self.recv_sems = grads, landed, send_sems, recv_sems
        self.x, self.y, self.c, self.me, self.others = _place()
        self.dev = 2 * self.me + self.c

    def _piece(self, k, chip, half):
        rows = self.grads[k].shape[1] // 2
        return self.grads[k].at[chip, pl.ds(half * rows, rows), :]

    def _to_sibling(self, k):
        return _remote(self._piece(k, self.me, 1 - self.c), self.landed[k].at[self.dev], self.send_sems.at[7 * k],
                       self.recv_sems.at[7 * k], (self.x, self.y, 1 - self.c))

    def _to_chip(self, k, j, half):
        ox, oy = self.others[j]
        return _remote(self._piece(k, 2 * ox + oy, half), self.landed[k].at[self.dev],
                       self.send_sems.at[7 * k + 1 + 2 * j + half], self.recv_sems.at[7 * k + 1 + 2 * j + self.c],
                       (ox, oy, half))

    def _from(self, k, sem, slot):
        return _remote(self.landed[k].at[slot], self.landed[k].at[slot], self.send_sems.at[7 * k + sem],
                       self.recv_sems.at[7 * k + sem], (self.x, self.y, 1 - self.c))

    def start(self):
        for k in range(len(self.grads)):
            self._to_sibling(k).start()
            for j in range(3):
                for half in range(2):
                    self._to_chip(k, j, half).start()

    def finish(self):
        for k in range(len(self.grads)):
            self._from(k, 0, 2 * self.me + 1 - self.c).wait_recv()
            for j, (ox, oy) in enumerate(self.others):
                for core in range(2):
                    self._from(k, 1 + 2 * j + core, 4 * ox + 2 * oy + core).wait_recv()
        for k in range(len(self.grads)):
            self._to_sibling(k).wait_send()
            for j in range(3):
                for half in range(2):
                    self._to_chip(k, j, half).wait_send()

    @staticmethod
    def scratch(n):
        return [pltpu.SemaphoreType.DMA((7 * n,)), pltpu.SemaphoreType.DMA((7 * n,))]


def _rs_sum_chips(name, place, landed, part):
    _, half, cols = landed.shape

    def body(place_ref, l_ref, p_ref, o_ref):
        me = place_ref[0]
        own = p_ref[0].astype(f32)
        acc = jnp.where(me == 0, own, l_ref[0].astype(f32))
        for j in range(1, N_CHIPS):
            acc = acc + jnp.where(me == j, own, l_ref[j].astype(f32))
        o_ref[...] = acc

    return pl.pallas_call(
        body, name=f"rs_sum_chips_{name}",
        grid_spec=pltpu.PrefetchScalarGridSpec(
            num_scalar_prefetch=1, grid=(1,),
            in_specs=[pl.BlockSpec((N_CHIPS, half, cols), lambda t, place_ref: (0, 0, 0)),
                      pl.BlockSpec((1, half, cols), lambda t, place_ref: (place_ref[0], 0, 0))],
            out_specs=pl.BlockSpec((half, cols), lambda t, place_ref: (place_ref[1], 0))),
        out_shape=jax.ShapeDtypeStruct((2 * half, cols), f32),
        compiler_params=_cp("arbitrary"),
    )(place, landed, part)


def _grad_pair(shards):
    nk = len(shards)

    def body(*refs):
        outs = refs[nk:2 * nk]
        send_sems, recv_sems = refs[2 * nk:]
        x, y, c, _, _ = _place()

        def half(k, core):
            h = outs[k].shape[0] // 2
            return outs[k].at[pl.ds(core * h, h), :]

        cps = [_remote(half(k, c), half(k, c), send_sems.at[k], recv_sems.at[k], (x, y, 1 - c)) for k in range(nk)]
        for cp in cps:
            cp.start()
        for k in range(nk):
            _remote(half(k, 1 - c), half(k, 1 - c), send_sems.at[k], recv_sems.at[k], (x, y, 1 - c)).wait_recv()
        for cp in cps:
            cp.wait_send()

    return pl.pallas_call(
        body, name="grad_pair", in_specs=[ANY] * nk, out_specs=[ANY] * nk,
        out_shape=[jax.ShapeDtypeStruct(a.shape, f32) for a in shards],
        input_output_aliases={k: k for k in range(nk)},
        scratch_shapes=[pltpu.SemaphoreType.DMA((nk,)), pltpu.SemaphoreType.DMA((nk,))],
    )(*shards)


def _adam_math(w, g, m, v):
    m = ADAM_B1 * m + (1.0 - ADAM_B1) * g
    v = ADAM_B2 * v + (1.0 - ADAM_B2) * (g * g)
    m_hat = m / (1.0 - ADAM_B1 ** ADAM_STEP)
    v_hat = v / (1.0 - ADAM_B2 ** ADAM_STEP)
    delta = -ADAM_LR * (m_hat / (jnp.sqrt(v_hat) + ADAM_EPS) + ADAM_WD * w)
    return delta, m, v


def _adam(ws, gs, ms, vs):
    n = len(ws)
    steps = 2 * SHARD_STEPS

    def body(*refs):
        for k in range(n):
            w_ref, g_ref, m_ref, v_ref = (refs[j * n + k] for j in range(4))
            go_ref, d_ref, nm_ref, nv_ref = (refs[(4 + j) * n + k] for j in range(4))
            g = g_ref[...]
            go_ref[...] = g
            d_ref[...], nm_ref[...], nv_ref[...] = _adam_math(w_ref[...], g, m_ref[...], v_ref[...])

    specs = [_row(0, (w.shape[0] // steps, w.shape[1])) for w in ws]
    res = pl.pallas_call(
        body, name="adam_shards", grid=(steps,), in_specs=specs * 4, out_specs=specs * 4,
        out_shape=[jax.ShapeDtypeStruct(w.shape, f32) for w in ws] * 4,
        compiler_params=_cp("parallel"),
    )(*ws, *gs, *ms, *vs)
    return [tuple(res[j * n + k] for j in range(4)) for k in range(n)]


class _SmallGather:
    def __init__(self, parts, bufs, send_sems, recv_sems):
        self.parts, self.bufs, self.send_sems, self.recv_sems = parts, bufs, send_sems, recv_sems
        self.x, self.y, self.c, _, self.others = _place()
        self.sibling = (self.x, self.y, 1 - self.c)

    def _copy(self, a, k, block, to, src=None):
        slot = self.bufs[a].at[4 * block[0] + 2 * block[1] + block[2]]
        return _remote(slot if src is None else src, slot, self.send_sems.at[7 * a + k], self.recv_sems.at[7 * a + k],
                       to)

    def _first(self, a):
        me = (self.x, self.y, self.c)
        return [self._copy(a, 0, me, self.sibling, src=self.parts[a])] + [
            self._copy(a, 1 + j, me, (*chip, self.c), src=self.parts[a]) for j, chip in enumerate(self.others)]

    def _passed(self, a):
        return [self._copy(a, 4 + j, (*chip, self.c), self.sibling) for j, chip in enumerate(self.others)]

    @staticmethod
    def scratch(n):
        return [pltpu.SemaphoreType.DMA((7 * n,)), pltpu.SemaphoreType.DMA((7 * n,))]

    def start(self):
        for a in range(len(self.parts)):
            for cp in self._first(a):
                cp.start()

    def finish(self):
        sent = []
        for a in range(len(self.parts)):
            passed = self._passed(a)
            for j, chip in enumerate(self.others):
                self._copy(a, 1 + j, (*chip, self.c), self.sibling).wait_recv()
                passed[j].start()
            sent += self._first(a) + passed
        for a in range(len(self.parts)):
            self._copy(a, 0, self.sibling, self.sibling).wait_recv()
            for j, chip in enumerate(self.others):
                self._copy(a, 4 + j, (*chip, 1 - self.c), self.sibling).wait_recv()
        for cp in sent:
            cp.wait_send()


def _gathered_shapes(parts):
    return [jax.ShapeDtypeStruct((N_DEV,) + p.shape, p.dtype) for p in parts]


def _rs_sum_group(name, place, landed, parts, landed_all, grads_all, cross_parts, small_parts):
    nk, na, nx, ns = len(landed), len(landed_all), len(cross_parts), len(small_parts)
    dims = [a.shape[1:] for a in landed]
    dims_all = [a.shape[1:] for a in landed_all]

    def body(place_ref, *refs):
        take = iter(refs)
        l_refs, p_refs, la_refs, ga_refs, x_refs, sp_refs, o_refs, oa_refs, xl_refs, sbufs = (
            [next(take) for _ in range(cnt)] for cnt in (nk, nk, na, na, nx, ns, nk, na, nx, ns))
        sems = list(take)
        cross = _Cross(x_refs, xl_refs, sems[0], sems[1])
        small = _SmallGather(sp_refs, sbufs, sems[2], sems[3])
        t = pl.program_id(0)

        @pl.when(t == 0)
        def _():
            cross.start()
            small.start()

        me = place_ref[0]
        for l_ref, p_ref, o_ref in zip(l_refs, p_refs, o_refs):
            own = p_ref[0].astype(f32)
            acc = jnp.where(me == 0, own, l_ref[0].astype(f32))
            for j in range(1, N_CHIPS):
                acc = acc + jnp.where(me == j, own, l_ref[j].astype(f32))
            o_ref[...] = acc
        dev = 2 * me + place_ref[1]
        for l_ref, g_ref, o_ref in zip(la_refs, ga_refs, oa_refs):
            own = g_ref[0].astype(f32)
            acc = jnp.where(dev == 0, own, l_ref[0].astype(f32))
            for d in range(1, N_DEV):
                acc = acc + jnp.where(dev == d, own, l_ref[d].astype(f32))
            o_ref[...] = acc

        @pl.when(t == 1)
        def _():
            small.finish()
            cross.finish()

    def halves(h, c, lead, index):
        return pl.BlockSpec((lead, h // 2, c) if lead else (h // 2, c), index)

    in_specs = [halves(h, c, N_CHIPS, lambda t, pr: (0, t, 0)) for h, c in dims]
    in_specs += [halves(h, c, 1, lambda t, pr: (pr[0], t, 0)) for h, c in dims]
    in_specs += [halves(h, c, N_DEV, lambda t, pr: (0, t, 0)) for h, c in dims_all]
    in_specs += [halves(h, c, 1, lambda t, pr: (pr[0], 2 * pr[1] + t, 0)) for h, c in dims_all]
    in_specs += [ANY] * (nx + ns)
    out_specs = [halves(h, c, 0, lambda t, pr: (2 * pr[1] + t, 0)) for h, c in dims + dims_all] + [ANY] * (nx + ns)
    out_shape = [jax.ShapeDtypeStruct((2 * h, c), f32) for h, c in dims + dims_all]
    out_shape += [jax.ShapeDtypeStruct(a.shape, a.dtype) for a in cross_parts] + _gathered_shapes(small_parts)
    res = pl.pallas_call(
        body, name=f"rs_sum_{name}",
        grid_spec=pltpu.PrefetchScalarGridSpec(
            num_scalar_prefetch=1, grid=(2,), in_specs=in_specs, out_specs=out_specs,
            scratch_shapes=_Cross.scratch(nx) + _SmallGather.scratch(ns)),
        out_shape=out_shape, compiler_params=_cp("arbitrary"),
    )(place, *landed, *parts, *landed_all, *grads_all, *cross_parts, *small_parts)
    return res[:nk], res[nk:nk + na], res[nk + na:nk + na + nx], res[nk + na + nx:]


SMALL_PARAMS = ("g_ple_gate", "g_ple_post", "g_final", "g_ffn", "ln_g", "ln_b", "conv_b", "pool_scale", "pool_w",
                "conv_w", "g_mix")
SMALL_ROWS = {"g_ple_gate": (0, 0), "g_ple_post": (0, 1), "g_final": (0, 2), "g_ffn": (1, 0), "ln_g": (2, 0),
              "ln_b": (2, 1), "conv_b": (2, 2), "pool_scale": (2, 3), "g_mix": (5, 0)}
LOSS_ROW = (0, 3)


def _small_adam(place, gathered, parts, params):
    nb, names = len(parts), SMALL_PARAMS
    flat = [a for nm in names for a in params[nm]]

    def body(place_ref, *refs):
        b_refs, p_refs = refs[:nb], refs[nb:2 * nb]
        w_refs = refs[2 * nb:2 * nb + 3 * len(names)]
        outs = refs[2 * nb + 3 * len(names):]
        loss_ref, o_refs, cw_sum = outs[0], outs[1:1 + 4 * len(names)], outs[1 + 4 * len(names)]
        chip = place_ref[0]
        me = 2 * chip + place_ref[1]

        def total(blk, idx):
            own = p_refs[blk][idx]
            g = jnp.where(me == 0, own, b_refs[blk][(0,) + idx])
            for d in range(1, N_DEV):
                g = g + jnp.where(me == d, own, b_refs[blk][(d,) + idx])
            return g

        everything = (slice(None), slice(None))
        loss_ref[...] = total(LOSS_ROW[0], (pl.ds(LOSS_ROW[1], 1), pl.ds(0, 128)))
        d_cw = total(4, everything)
        mine = jnp.where(chip == 0, d_cw[:, 0:128], 0.0)
        for j in range(1, N_CHIPS):
            mine = mine + jnp.where(chip == j, d_cw[:, j * 128:(j + 1) * 128], 0.0)
        cw_sum[...] = mine
        for k, nm in enumerate(names):
            w_ref, m_ref, v_ref = w_refs[3 * k:3 * k + 3]
            g_ref, d_ref, nm_ref, nv_ref = o_refs[4 * k:4 * k + 4]
            if nm == "pool_w":
                g = total(3, everything + (slice(None),))
            elif nm == "conv_w":
                g = cw_sum[pl.ds(0, CONV_K), :]
            else:
                blk, row = SMALL_ROWS[nm]
                g = total(blk, (pl.ds(row, 1), slice(None)))
            g_ref[...] = g
            d_ref[...], nm_ref[...], nv_ref[...] = _adam_math(w_ref[...], g, m_ref[...], v_ref[...])

    whole = lambda a: pl.BlockSpec(a.shape, lambda t, pr: (0,) * a.ndim)
    out_shape = [jax.ShapeDtypeStruct((1, 128), f32)]
    out_shape += [jax.ShapeDtypeStruct(params[nm][0].shape, f32) for nm in names for _ in range(4)]
    res = pl.pallas_call(
        body, name="small_adam",
        grid_spec=pltpu.PrefetchScalarGridSpec(
            num_scalar_prefetch=1, grid=(1,),
            in_specs=[whole(a) for a in list(gathered) + list(parts) + flat],
            out_specs=[whole(s) for s in out_shape], scratch_shapes=[pltpu.VMEM((HALO, 128), f32)]),
        out_shape=out_shape, compiler_params=_cp("arbitrary"),
    )(place, *gathered, *parts, *flat)
    return res[0], {nm: res[1 + 4 * k:5 + 4 * k] for k, nm in enumerate(names)}


def _pad_rows(a, rows):
    return jnp.concatenate([a, jnp.zeros((rows - a.shape[0],) + a.shape[1:], a.dtype)], axis=0)


def kernel(x, p, g_mix, w_in, conv_w, conv_b, ln_g, ln_b, pool_w, pool_scale, w_out, g_ffn, w_gate_up, w_down, g_ple_gate, w_ple_gate, w_ple_up, g_ple_post, g_final, loss_target, m_g_mix, m_w_in, m_conv_w, m_conv_b, m_ln_g, m_ln_b, m_pool_w, m_pool_scale, m_w_out, m_g_ffn, m_w_gate_up, m_w_down, m_g_ple_gate, m_w_ple_gate, m_w_ple_up, m_g_ple_post, m_g_final, v_g_mix, v_w_in, v_conv_w, v_conv_b, v_ln_g, v_ln_b, v_pool_w, v_pool_scale, v_w_out, v_g_ffn, v_w_gate_up, v_w_down, v_g_ple_gate, v_w_ple_gate, v_w_ple_up, v_g_ple_post, v_g_final):
    seq = x.shape[1]
    me = 2 * lax.axis_index("x") + lax.axis_index("y")
    chip = me.astype(jnp.int32).reshape(1)
    core = lax.axis_index("c").astype(jnp.int32).reshape(1)
    place = jnp.concatenate([chip, core])
    xs, ps, ts = x.reshape(seq, D_MODEL), p.reshape(seq, D_PLE), loss_target.reshape(seq, D_MODEL)

    big = [w_in[0], w_gate_up[0], w_out[0], w_down[0], w_ple_gate[0], w_ple_up[0]]
    big_m = [m_w_in[0], m_w_gate_up[0], m_w_out[0], m_w_down[0], m_w_ple_gate[0], m_w_ple_up[0]]
    big_v = [v_w_in[0], v_w_gate_up[0], v_w_out[0], v_w_down[0], v_w_ple_gate[0], v_w_ple_up[0]]
    b_in, b_gu, b_out, b_down, b_pg, b_pu, b_cw = _cast_into_slots(
        chip, big + [_pad_rows(conv_w[0], HALO)], [bf16] * len(big) + [f32])
    xi, yi = lax.axis_index("x"), lax.axis_index("y")
    order = jnp.stack([me, 2 * (1 - xi) + yi, 2 * xi + 1 - yi, 2 * (1 - xi) + 1 - yi]).astype(jnp.int32)

    z, (w_in_g, cw_g, w_out_g) = _mix_in(xs, g_mix, order, [b_in, b_cw, b_out])
    conv_w_f = cw_g.transpose(1, 0, 2).reshape(HALO, C_CONV)
    w_out_f = w_out_g.reshape(D_MODEL, D_MODEL)
    (x1, mix, u1, pooled, h2), (w_gu_g, w_down_g) = _conv_pool_out(z, xs, conv_w_f, conv_b, ln_g, ln_b, pool_w[0],
                                                                    pool_scale, w_out_f, g_ffn, [b_gu, b_down])
    w_down_f = w_down_g.reshape(D_FF, D_MODEL)
    (x2, gu, ffn_f), (w_pg_g, w_pu_g) = _ffn_fwd(x1, h2, w_gu_g, w_down_f, [b_pg, b_pu])
    w_pg_f = w_pg_g.reshape(D_MODEL, D_MODEL)
    dx2, d_w_pg, d_w_pu, small_ple = _ple_loss(x2, ps, ts, g_ple_gate, g_ple_post, g_final.reshape(1, D_MODEL),
                                               w_pg_f, w_pu_g)
    d_w_down = _ffn_bwd_dw_down(ffn_f, dx2)
    grads_a = [d_w_down.reshape(N_CHIPS, -1, D_MODEL), d_w_pg.reshape(N_CHIPS, -1, D_MODEL), d_w_pu]
    (dx1, dgu, small_ffn), landed_a = _ffn_bwd_dx(dx2, x1, gu, g_ffn, w_gu_g, w_down_f, grads_a)
    d_w_gu = _ffn_bwd_dw_gu(h2, dgu)
    du1, dpo, d_w_out, d_pool_w, small_mix = _mix_bwd_local(dx1, mix, u1, pooled, w_out_f, ln_g, ln_b, pool_w[0],
                                                             pool_scale)
    parts_b = _pair_reduce("b", [d_w_gu, d_w_out.reshape(N_CHIPS, -1, D_MODEL)])
    small_0 = [small_ple, small_ffn, small_mix, d_pool_w]
    (grad_x, d_w_in, d_conv_w, small_in), landed_b, small_all_0 = _in_bwd(du1, dpo, z, xs, dx1, conv_w_f, g_mix,
                                                                            w_in_g, parts_b, small_0)
    parts_c = _pair_reduce("c", [d_w_in])
    small_1 = [d_conv_w, small_in]
    (h_gu, h_out), (h_down, h_pg, h_pu), landed_c, small_all_1 = _rs_sum_group(
        "ab", place, landed_b, parts_b, landed_a, grads_a, parts_c, small_1)
    h_in = _rs_sum_chips("w_in", place, landed_c[0], parts_c[0])
    big_g = _grad_pair([h_in, h_gu, h_out, h_down, h_pg, h_pu])
    big_upd = _adam(big, big_g, big_m, big_v)

    p3 = lambda w, m, v: (w, m, v)
    row = lambda a: a.reshape(1, D_MODEL)
    params = dict(
        g_ple_gate=p3(g_ple_gate, m_g_ple_gate, v_g_ple_gate), g_ple_post=p3(g_ple_post, m_g_ple_post, v_g_ple_post),
        g_final=p3(row(g_final), row(m_g_final), row(v_g_final)), g_ffn=p3(g_ffn, m_g_ffn, v_g_ffn),
        ln_g=p3(ln_g, m_ln_g, v_ln_g), ln_b=p3(ln_b, m_ln_b, v_ln_b), conv_b=p3(conv_b, m_conv_b, v_conv_b),
        pool_scale=p3(pool_scale, m_pool_scale, v_pool_scale), pool_w=p3(pool_w[0], m_pool_w[0], v_pool_w[0]),
        conv_w=p3(conv_w[0], m_conv_w[0], v_conv_w[0]), g_mix=p3(g_mix, m_g_mix, v_g_mix))
    loss, small = _small_adam(place, list(small_all_0) + list(small_all_1), small_0 + small_1, params)
    back = dict(g_final=lambda a: a.reshape(D_MODEL), pool_w=lambda a: a[None], conv_w=lambda a: a[None])

    names = ["g_mix", "w_in", "conv_w", "conv_b", "ln_g", "ln_b", "pool_w", "pool_scale", "w_out", "g_ffn",
             "w_gate_up", "w_down", "g_ple_gate", "w_ple_gate", "w_ple_up", "g_ple_post", "g_final"]
    big_at = {"w_in": 0, "w_gate_up": 1, "w_out": 2, "w_down": 3, "w_ple_gate": 4, "w_ple_up": 5}
    out = [loss[0, 0], grad_x.reshape(1, seq, D_MODEL)]
    for kind in range(4):
        for nm in names:
            if nm in big_at:
                out.append(big_upd[big_at[nm]][kind][None])
            else:
                out.append(back.get(nm, lambda a: a)(small[nm][kind]))
    return tuple(out)
```

```python
import functools

import jax
import jax.numpy as jnp
from jax import lax
from jax.experimental import pallas as pl
from jax.experimental.pallas import tpu as pltpu

f32, bf16 = jnp.float32, jnp.bfloat16

EPS = 1e-6
D_MODEL = 1024
C_CONV = 512
C_POOL = 512
POOL_WINDOWS = (2, 4, 8, 16)
POOL_GROUP = 128
CONV_K = 31
D_FF = 2816
D_PLE = 256
N_CHIPS = 4
N_DEV = 8
W_IN_COLS = 2 * C_CONV + C_POOL
W_IN_CHUNK = W_IN_COLS // N_CHIPS
FF_CHUNK = 2 * D_FF // N_CHIPS
PLE_CHUNK = D_MODEL // N_CHIPS
HALO = 32
ROW_TILE = 512
CONV_ROWS = 64
CONV_COLS = (slice(0, 256), slice(256, 512))
SHIFT_PAD = 32
SHIFT_GROUPS = 16
FF_SUB = (0, 512, 1024, FF_CHUNK)
VMEM_LIMIT = 56 * 1024 * 1024

ADAM_LR = 0.001
ADAM_B1 = 0.9
ADAM_B2 = 0.999
ADAM_EPS = 1e-08
ADAM_WD = 0.01
ADAM_STEP = 10

MESH = pl.DeviceIdType.MESH
ANY = pl.BlockSpec(memory_space=pl.ANY)
VMEM = pl.BlockSpec(memory_space=pltpu.VMEM)


def _cp(*sem):
    return pltpu.CompilerParams(dimension_semantics=sem, vmem_limit_bytes=VMEM_LIMIT)


def _dot(a, b):
    return jnp.dot(a, b, preferred_element_type=f32)


def _dot_nt(a, b):
    return lax.dot_general(a, b, (((1,), (1,)), ((), ())), preferred_element_type=f32)


def _dot_tn(a, b):
    return lax.dot_general(a, b, (((0,), (0,)), ((), ())), preferred_element_type=f32)


def _sigmoid(v):
    return jax.nn.sigmoid(v)


def _rms_fwd(v, g):
    r = lax.rsqrt(jnp.mean(v * v, axis=-1, keepdims=True) + EPS)
    vh = v * r
    return vh * g, vh, r


def _rms_bwd(dy, vh, r, g):
    dvh = dy * g
    dv = r * (dvh - vh * jnp.mean(dvh * vh, axis=-1, keepdims=True))
    return dv, jnp.sum(dy * vh, axis=0, keepdims=True)


def _silu_grad(v, s):
    return s * (1.0 + v * (1.0 - s))


def _row(i, n):
    return pl.BlockSpec((n[0], n[1]), lambda *a: (a[i], 0))


def _full(shape):
    nd = len(shape)
    return pl.BlockSpec(shape, lambda *a: (0,) * nd)


def _place():
    x, y, c = lax.axis_index("x"), lax.axis_index("y"), lax.axis_index("c")
    others = [(1 - x, y), (x, 1 - y), (1 - x, 1 - y)]
    return x, y, c, 2 * x + y, others


def _remote(src, dst, send_sem, recv_sem, dev):
    return pltpu.make_async_remote_copy(src_ref=src, dst_ref=dst, send_sem=send_sem, recv_sem=recv_sem,
                                        device_id=dev, device_id_type=MESH)


SHARD_STEPS = 4


def _cast_into_slots(me, ws, dtypes):
    n = len(ws)

    def body(me_ref, *refs):
        for w_ref, o_ref, dtype in zip(refs[:n], refs[n:], dtypes):
            o_ref[0] = w_ref[...].astype(dtype)

    return pl.pallas_call(
        body, name="cast_shards",
        grid_spec=pltpu.PrefetchScalarGridSpec(
            num_scalar_prefetch=1, grid=(SHARD_STEPS,),
            in_specs=[pl.BlockSpec((w.shape[0] // SHARD_STEPS, w.shape[1]), lambda r, me_ref: (r, 0)) for w in ws],
            out_specs=[pl.BlockSpec((1, w.shape[0] // SHARD_STEPS, w.shape[1]), lambda r, me_ref: (me_ref[0], r, 0))
                       for w in ws]),
        out_shape=[jax.ShapeDtypeStruct((N_CHIPS,) + w.shape, dt) for w, dt in zip(ws, dtypes)],
        compiler_params=_cp("parallel"),
    )(me, *ws)


class _Gather:
    def __init__(self, bufs, send_sems, recv_sems):
        self.bufs, self.send_sems, self.recv_sems = bufs, send_sems, recv_sems
        self.x, self.y, self.c, self.me, self.others = _place()
        self.halves = [b.shape[1] // 2 for b in bufs]

    def _piece(self, k, chip, half):
        return self.bufs[k].at[chip, pl.ds(half * self.halves[k], self.halves[k]), :]

    def _ici(self, k, j, chip):
        ox, oy = self.others[j]
        piece = self._piece(k, chip, self.c)
        return _remote(piece, piece, self.send_sems.at[6 * k + j], self.recv_sems.at[6 * k + j], (ox, oy, self.c))

    def _relay(self, k):
        first = self.c == 0
        piece = self._piece(k, jnp.where(first, self.chip(0), self.chip(1)), self.c)
        to = (jnp.where(first, self.others[1][0], self.others[0][0]),
              jnp.where(first, self.others[1][1], self.others[0][1]), self.c)
        return _remote(piece, piece, self.send_sems.at[6 * k + 2], self.recv_sems.at[6 * k + 2], to)

    def _pair(self, k, j, half):
        ox, oy = self.others[j]
        piece = self._piece(k, 2 * ox + oy, half)
        return _remote(piece, piece, self.send_sems.at[6 * k + 3 + j], self.recv_sems.at[6 * k + 3 + j],
                       (self.x, self.y, 1 - self.c))

    def _each(self, ks=None):
        return [(k, j) for k in (range(len(self.bufs)) if ks is None else ks) for j in range(3)]

    def chip(self, j):
        ox, oy = self.others[j]
        return 2 * ox + oy

    def start(self):
        for k in range(len(self.bufs)):
            for j in range(2):
                self._ici(k, j, self.me).start()

    def forward(self, pairs=None):
        for k, j in self._each() if pairs is None else pairs:
            self._ici(k, j, self.chip(j)).wait_recv()
            self._pair(k, j, self.c).start()
            if j < 2:
                pl.when(self.c == j)(self._relay(k).start)

    def landed(self, pairs):
        for k, j in pairs:
            self._pair(k, j, 1 - self.c).wait_recv()

    def finish(self, ks=None):
        self.landed(self._each(ks))
        for k in range(len(self.bufs)):
            for j in range(2):
                self._ici(k, j, self.me).wait_send()
            self._relay(k).wait_send()
            for j in range(3):
                self._pair(k, j, self.c).wait_send()

    @staticmethod
    def scratch(n):
        return [pltpu.SemaphoreType.DMA((6 * n,)), pltpu.SemaphoreType.DMA((6 * n,))]


def _carried(bufs):
    n = len(bufs)
    return dict(in_specs=[ANY] * n, out_specs=[ANY] * n,
                out_shape=[jax.ShapeDtypeStruct(b.shape, b.dtype) for b in bufs], scratch=_Gather.scratch(n))


def _mix_in(x, g_mix, order, carry):
    s = x.shape[0]
    tm = min(2 * ROW_TILE, s)
    n = s // tm
    nc = len(carry)
    cs = _carried(carry)

    def body(order_ref, x_ref, g_ref, *refs):
        z_ref = refs[nc]
        bufs = refs[nc + 1:2 * nc + 1]
        h_ref, w_ref, w_sem = refs[2 * nc + 1:2 * nc + 4]
        gather = _Gather(bufs, *refs[2 * nc + 4:])
        q, i = pl.program_id(0), pl.program_id(1)
        first = i == 0
        pl.when(jnp.logical_and(q == 0, first))(gather.start)
        for j in range(3):

            @pl.when(jnp.logical_and(q == j + 1, first))
            def _():
                gather.forward([(0, j)])
                gather.landed([(0, j)])
                if j == 1:
                    gather.forward([(k, jj) for k in range(1, nc) for jj in range(2)])

        @pl.when(first)
        def _():
            load = pltpu.make_async_copy(bufs[0].at[order_ref[q]], w_ref, w_sem)
            load.start()
            load.wait()

        @pl.when(q == 0)
        def _():
            h, _, _ = _rms_fwd(x_ref[...], g_ref[...])
            h_ref[i] = h.astype(bf16)

        z_ref[...] = _dot(h_ref[i], w_ref[...])

        @pl.when(jnp.logical_and(q == N_CHIPS - 1, i == n - 1))
        def _():
            gather.forward([(k, 2) for k in range(1, nc)])
            gather.finish(range(1, nc))

    res = pl.pallas_call(
        body, name="mix_in",
        grid_spec=pltpu.PrefetchScalarGridSpec(
            num_scalar_prefetch=1, grid=(N_CHIPS, n),
            in_specs=[pl.BlockSpec((tm, D_MODEL), lambda q, i, order_ref: (jnp.where(q == 0, i, 0), 0)),
                      pl.BlockSpec((1, D_MODEL), lambda q, i, order_ref: (0, 0))] + cs["in_specs"],
            out_specs=[pl.BlockSpec((tm, W_IN_CHUNK), lambda q, i, order_ref: (i, order_ref[q]))] + cs["out_specs"],
            scratch_shapes=[pltpu.VMEM((n, tm, D_MODEL), bf16), pltpu.VMEM((D_MODEL, W_IN_CHUNK), bf16),
                            pltpu.SemaphoreType.DMA(())] + cs["scratch"]),
        out_shape=[jax.ShapeDtypeStruct((s, W_IN_COLS), f32)] + cs["out_shape"],
        input_output_aliases={3 + k: 1 + k for k in range(nc)},
        compiler_params=_cp("arbitrary", "arbitrary"),
    )(order, x, g_mix, *carry)
    return res[0], res[1:]


def _tap_offsets(lo, hi):
    groups = [[o for o in range(lo, hi + 1) if o % 8 == s] for s in range(8)]
    return [g for g in groups if g]


def _tap_sum(buf, w_ref, row0, cols, tap_of):
    acc = jnp.zeros((CONV_ROWS, cols.stop - cols.start), f32)
    for offs in _tap_offsets(0, CONV_K - 1):
        slab = buf[pl.ds(row0 + offs[0], offs[-1] - offs[0] + CONV_ROWS), cols]
        for o in offs:
            acc = acc + w_ref[pl.ds(tap_of(o), 1), cols] * slab[o - offs[0]:o - offs[0] + CONV_ROWS]
    return acc


def _pool_counts(tm, w, first_row):
    t1 = (lax.broadcasted_iota(jnp.int32, (tm, 1), 0) + first_row + 1).astype(f32)
    return jnp.minimum(t1, float(w))


def _conv_pool_out(z, x, conv_w, conv_b, ln_g, ln_b, pool_w, pool_scale, w_out, g_ffn, carry):
    s = x.shape[0]
    tm = min(ROW_TILE, s)
    n = s // tm
    hb = tm // HALO
    nc = len(carry)
    cs = _carried(carry)

    def body(z_ref, zp_ref, x_ref, cw_ref, cb_ref, lg_ref, lb_ref, pw_ref, ps_ref, wo_ref, gf_ref, *refs):
        x1_ref, mix_ref, u1_ref, pooled_ref, h2_ref = refs[nc:nc + 5]
        ubuf, vbuf = refs[2 * nc + 5:2 * nc + 7]
        gather = _Gather(refs[nc + 5:2 * nc + 5], *refs[2 * nc + 7:])
        i = pl.program_id(0)
        pl.when(i == 0)(gather.start)
        for k in range(nc):
            pl.when(i == min(n // 2 + 2 * k, n - 1))(functools.partial(gather.forward, [(k, 0), (k, 1)]))
        keep = (i > 0).astype(f32)
        zp = zp_ref[...] * keep
        ubuf[0:HALO, :] = zp[:, :C_CONV] * _sigmoid(zp[:, C_CONV:2 * C_CONV])
        vbuf[0:HALO, :] = zp[:, 2 * C_CONV:]
        ubuf[HALO:, :] = z_ref[:, :C_CONV] * _sigmoid(z_ref[:, C_CONV:2 * C_CONV])
        vbuf[HALO:, :] = z_ref[:, 2 * C_CONV:]
        off = HALO - (CONV_K - 1)
        for r0 in range(0, tm, CONV_ROWS):
            for cols in CONV_COLS:
                u1_ref[r0:r0 + CONV_ROWS, cols] = cb_ref[:, cols] + _tap_sum(ubuf, cw_ref, r0 + off, cols, lambda o: o)
        u1 = u1_ref[...]
        mu = jnp.mean(u1, axis=-1, keepdims=True)
        uc = u1 - mu
        rstd = lax.rsqrt(jnp.mean(uc * uc, axis=-1, keepdims=True) + EPS)
        u2 = uc * rstd * lg_ref[...] + lb_ref[...]
        mix_ref[:, :C_CONV] = (u2 * _sigmoid(u2)).astype(bf16)
        for g, w in enumerate(POOL_WINDOWS):
            cols = slice(g * POOL_GROUP, (g + 1) * POOL_GROUP)
            acc = vbuf[pl.ds(HALO, tm), cols]
            vg = acc
            for d in range(1, w):
                acc = acc + vbuf[pl.ds(HALO - d, tm), cols]
            pooled = (acc / _pool_counts(tm, w, i * tm) - vg).astype(bf16)
            pooled_ref[:, cols] = pooled
            mixed = _dot(pooled, pw_ref[g].astype(bf16))
            mix_ref[:, C_CONV + g * POOL_GROUP:C_CONV + (g + 1) * POOL_GROUP] = (mixed * ps_ref[:, cols]).astype(bf16)
        x1 = x_ref[...] + _dot(mix_ref[...], wo_ref[...])
        x1_ref[...] = x1
        h2_ref[...] = _rms_fwd(x1, gf_ref[...])[0].astype(bf16)
        @pl.when(i == n - 1)
        def _():
            gather.forward([(k, 2) for k in range(nc)])
            gather.finish()

    res = pl.pallas_call(
        body, name="conv_pool_out", grid=(n,),
        in_specs=[_row(0, (tm, W_IN_COLS)),
                  pl.BlockSpec((HALO, W_IN_COLS), lambda i: (jnp.maximum(i * hb - 1, 0), 0)),
                  _row(0, (tm, D_MODEL)), _full((HALO, C_CONV)), _full((1, C_CONV)), _full((1, C_CONV)),
                  _full((1, C_CONV)), _full((4, POOL_GROUP, POOL_GROUP)), _full((1, C_POOL)),
                  _full((D_MODEL, D_MODEL)), _full((1, D_MODEL))] + cs["in_specs"],
        out_specs=[_row(0, (tm, D_MODEL)), _row(0, (tm, D_MODEL)), _row(0, (tm, C_CONV)), _row(0, (tm, C_POOL)),
                   _row(0, (tm, D_MODEL))] + cs["out_specs"],
        out_shape=[jax.ShapeDtypeStruct((s, D_MODEL), f32), jax.ShapeDtypeStruct((s, D_MODEL), bf16),
                   jax.ShapeDtypeStruct((s, C_CONV), f32), jax.ShapeDtypeStruct((s, C_POOL), bf16),
                   jax.ShapeDtypeStruct((s, D_MODEL), bf16)] + cs["out_shape"],
        input_output_aliases={11 + k: 5 + k for k in range(nc)},
        scratch_shapes=[pltpu.VMEM((HALO + tm, C_CONV), f32), pltpu.VMEM((HALO + tm, C_POOL), f32)] + cs["scratch"],
        compiler_params=_cp("arbitrary"),
    )(z, z, x, conv_w, conv_b, ln_g, ln_b, pool_w, pool_scale, w_out, g_ffn, *carry)
    return res[:5], res[5:]


def _ffn_fwd(x1, h2, w_gu_g, w_down, carry):
    s = x1.shape[0]
    tm = min(ROW_TILE, s)
    n = s // tm
    nc = len(carry)
    cs = _carried(carry)

    def body(x1_ref, h2_ref, wg_ref, wu_ref, wd_ref, *refs):
        x2_ref, gu_ref, f_ref = refs[nc:nc + 3]
        acc_ref = refs[2 * nc + 3]
        gather = _Gather(refs[nc + 3:2 * nc + 3], *refs[2 * nc + 4:])
        i, c = pl.program_id(0), pl.program_id(1)
        pl.when(jnp.logical_and(i == 0, c == 0))(gather.start)
        direct = [(k, j) for k in range(nc) for j in range(2)]
        pl.when(jnp.logical_and(i == n // 2, c == 0))(functools.partial(gather.forward, direct))
        pl.when(jnp.logical_and(i == n - 1, c == 0))(functools.partial(gather.forward, [(k, 2) for k in range(nc)]))

        @pl.when(c == 0)
        def _():
            acc_ref[...] = jnp.zeros_like(acc_ref)

        h = h2_ref[...]
        for lo, hi in zip(FF_SUB[:-1], FF_SUB[1:]):
            gate = _dot(h, wg_ref[0, :, lo:hi])
            up = _dot(h, wu_ref[0, :, lo:hi])
            gu_ref[0, :, lo:hi] = gate.astype(bf16)
            gu_ref[1, :, lo:hi] = up.astype(bf16)
            f = (gate * _sigmoid(gate) * up).astype(bf16)
            f_ref[:, lo:hi] = f
            acc_ref[...] += _dot(f, wd_ref[lo:hi, :])

        @pl.when(c == 1)
        def _():
            x2_ref[...] = x1_ref[...] + acc_ref[...]

        pl.when(jnp.logical_and(i == n - 1, c == 1))(gather.finish)

    res = pl.pallas_call(
        body, name="ffn_fwd", grid=(n, 2),
        in_specs=[_row(0, (tm, D_MODEL)), _row(0, (tm, D_MODEL)),
                  pl.BlockSpec((1, D_MODEL, FF_CHUNK), lambda i, c: (c, 0, 0)),
                  pl.BlockSpec((1, D_MODEL, FF_CHUNK), lambda i, c: (2 + c, 0, 0)),
                  pl.BlockSpec((FF_CHUNK, D_MODEL), lambda i, c: (c, 0))] + cs["in_specs"],
        out_specs=[_row(0, (tm, D_MODEL)), pl.BlockSpec((2, tm, FF_CHUNK), lambda i, c: (0, i, c)),
                   pl.BlockSpec((tm, FF_CHUNK), lambda i, c: (i, c))] + cs["out_specs"],
        out_shape=[jax.ShapeDtypeStruct((s, D_MODEL), f32), jax.ShapeDtypeStruct((2, s, D_FF), bf16),
                   jax.ShapeDtypeStruct((s, D_FF), bf16)] + cs["out_shape"],
        input_output_aliases={5 + k: 3 + k for k in range(nc)},
        scratch_shapes=[pltpu.VMEM((tm, D_MODEL), f32)] + cs["scratch"],
        compiler_params=_cp("arbitrary", "arbitrary"),
    )(x1, h2, w_gu_g, w_gu_g, w_down, *carry)
    return res[:3], res[3:]


def _ple_loss(x2, p, target, g_pg, g_post, g_final, w_pg, w_pu):
    s = x2.shape[0]
    tm = min(ROW_TILE, s)
    n = s // tm

    def body(x2_ref, p_ref, t_ref, gpg_ref, gpo_ref, gf_ref, wpg_ref, wpu_ref,
             dx2_ref, dwpg_ref, dwpu_ref, small_ref, apg_ref, apu_ref):
        i = pl.program_id(0)

        @pl.when(i == 0)
        def _():
            apg_ref[...] = jnp.zeros_like(apg_ref)
            apu_ref[...] = jnp.zeros_like(apu_ref)
            small_ref[...] = jnp.zeros_like(small_ref)

        x2 = x2_ref[...]
        h3, x2h, r2 = _rms_fwd(x2, gpg_ref[...])
        h3b = h3.astype(bf16)
        gate = _sigmoid(_dot(h3b, wpg_ref[...]))
        pb = p_ref[...].astype(bf16)
        pe = _dot(pb, wpu_ref[...])
        e, peh, rp = _rms_fwd(pe, gpo_ref[...])
        x3 = x2 + gate * e
        y, x3h, r3 = _rms_fwd(x3, gf_ref[...])
        d = y - t_ref[...]
        loss = 0.5 * jnp.sum(jnp.sum(d * d, axis=-1, keepdims=True) * (1.0 / D_MODEL), axis=0, keepdims=True)
        dx3, dgf = _rms_bwd(d * (1.0 / D_MODEL), x3h, r3, gf_ref[...])
        dpe, dgpo = _rms_bwd(dx3 * gate, peh, rp, gpo_ref[...])
        dgl = (dx3 * e * gate * (1.0 - gate)).astype(bf16)
        apg_ref[...] += _dot_tn(h3b, dgl)
        apu_ref[...] += _dot_tn(pb, dpe.astype(bf16))
        dh3 = _dot_nt(dgl, wpg_ref[...])
        dx2b, dgpg = _rms_bwd(dh3, x2h, r2, gpg_ref[...])
        dx2_ref[...] = dx3 + dx2b
        small_ref[0:1, :] += dgpg
        small_ref[1:2, :] += dgpo
        small_ref[2:3, :] += dgf
        small_ref[3:4, :] += jnp.broadcast_to(loss, (1, D_MODEL))

        @pl.when(i == n - 1)
        def _():
            dwpg_ref[...] = apg_ref[...].astype(bf16)
            for j in range(N_CHIPS):
                dwpu_ref[j] = apu_ref[:, j * PLE_CHUNK:(j + 1) * PLE_CHUNK].astype(bf16)

    return pl.pallas_call(
        body, name="ple_loss", grid=(n,),
        in_specs=[_row(0, (tm, D_MODEL)), _row(0, (tm, D_PLE)), _row(0, (tm, D_MODEL)),
                  _full((1, D_MODEL)), _full((1, D_MODEL)), _full((1, D_MODEL)),
                  _full((D_MODEL, D_MODEL)), _full((D_PLE, D_MODEL))],
        out_specs=[_row(0, (tm, D_MODEL)), _full((D_MODEL, D_MODEL)), _full((N_CHIPS, D_PLE, PLE_CHUNK)),
                   _full((8, D_MODEL))],
        out_shape=[jax.ShapeDtypeStruct((s, D_MODEL), f32), jax.ShapeDtypeStruct((D_MODEL, D_MODEL), bf16),
                   jax.ShapeDtypeStruct((N_CHIPS, D_PLE, PLE_CHUNK), bf16), jax.ShapeDtypeStruct((8, D_MODEL), f32)],
        scratch_shapes=[pltpu.VMEM((D_MODEL, D_MODEL), f32), pltpu.VMEM((D_PLE, D_MODEL), f32)],
        compiler_params=_cp("arbitrary"),
    )(x2, p, target, g_pg, g_post, g_final, w_pg, w_pu)


def _crossed(parts):
    n = len(parts)
    return dict(in_specs=[ANY] * n, out_specs=[ANY] * n,
                out_shape=[jax.ShapeDtypeStruct(a.shape, a.dtype) for a in parts], scratch=_Cross.scratch(n))


def _ffn_bwd_dx(dx2, x1, gu, g_ffn, w_gu_g, w_down, grads):
    s = x1.shape[0]
    tm = min(ROW_TILE, s)
    n = s // tm
    nc = len(grads)
    cs = dict(in_specs=[ANY] * nc, out_specs=[ANY] * nc, scratch=_CrossAll.scratch(nc),
              out_shape=[jax.ShapeDtypeStruct((N_DEV, g.shape[1] // 2, g.shape[2]), g.dtype) for g in grads])

    def body(dx2_ref, x1_ref, gu_ref, g_ref, wg_ref, wu_ref, wd_ref, *refs):
        dx1_ref, dgu_ref, small_ref = refs[nc:nc + 3]
        acc_ref = refs[2 * nc + 3]
        cross = _CrossAll(refs[:nc], refs[nc + 3:2 * nc + 3], *refs[2 * nc + 4:])
        i, c = pl.program_id(0), pl.program_id(1)
        pl.when(jnp.logical_and(i == 0, c == 0))(cross.start)

        @pl.when(jnp.logical_and(i == 0, c == 0))
        def _():
            small_ref[...] = jnp.zeros_like(small_ref)

        @pl.when(c == 0)
        def _():
            acc_ref[...] = jnp.zeros_like(acc_ref)

        dyb = dx2_ref[...].astype(bf16)
        for lo, hi in zip(FF_SUB[:-1], FF_SUB[1:]):
            df = _dot_nt(dyb, wd_ref[lo:hi, :])
            gate = gu_ref[0, :, lo:hi].astype(f32)
            up = gu_ref[1, :, lo:hi].astype(f32)
            sg = _sigmoid(gate)
            dgate = (df * up * _silu_grad(gate, sg)).astype(bf16)
            dup = (df * gate * sg).astype(bf16)
            dgu_ref[0, :, lo:hi] = dgate
            dgu_ref[1, :, lo:hi] = dup
            acc_ref[...] += _dot_nt(dgate, wg_ref[0, :, lo:hi]) + _dot_nt(dup, wu_ref[0, :, lo:hi])

        @pl.when(c == 1)
        def _():
            _, x1h, r1 = _rms_fwd(x1_ref[...], g_ref[...])
            dx1b, dg = _rms_bwd(acc_ref[...], x1h, r1, g_ref[...])
            dx1_ref[...] = dx2_ref[...] + dx1b
            small_ref[0:1, :] += dg

        pl.when(jnp.logical_and(i == n - 1, c == 1))(cross.finish)

    res = pl.pallas_call(
        body, name="ffn_bwd_dx", grid=(n, 2),
        in_specs=[_row(0, (tm, D_MODEL)), _row(0, (tm, D_MODEL)),
                  pl.BlockSpec((2, tm, FF_CHUNK), lambda i, c: (0, i, c)), _full((1, D_MODEL)),
                  pl.BlockSpec((1, D_MODEL, FF_CHUNK), lambda i, c: (c, 0, 0)),
                  pl.BlockSpec((1, D_MODEL, FF_CHUNK), lambda i, c: (2 + c, 0, 0)),
                  pl.BlockSpec((FF_CHUNK, D_MODEL), lambda i, c: (c, 0))] + cs["in_specs"],
        out_specs=[_row(0, (tm, D_MODEL)), pl.BlockSpec((2, tm, FF_CHUNK), lambda i, c: (0, i, c)),
                   _full((8, D_MODEL))] + cs["out_specs"],
        out_shape=[jax.ShapeDtypeStruct((s, D_MODEL), f32), jax.ShapeDtypeStruct((2, s, D_FF), bf16),
                   jax.ShapeDtypeStruct((8, D_MODEL), f32)] + cs["out_shape"],
        scratch_shapes=[pltpu.VMEM((tm, D_MODEL), f32)] + cs["scratch"],
        compiler_params=_cp("arbitrary", "arbitrary"),
    )(dx2, x1, gu, g_ffn, w_gu_g, w_gu_g, w_down, *grads)
    return res[:3], res[3:]


def _ffn_bwd_dw_gu(h2, dgu):
    s = h2.shape[0]
    ts = min(4 * ROW_TILE, s)
    n = s // ts

    def body(h_ref, d_ref, o_ref, acc_ref):
        t = pl.program_id(1)

        @pl.when(t == 0)
        def _():
            acc_ref[...] = jnp.zeros_like(acc_ref)

        acc_ref[...] += _dot_tn(h_ref[...], d_ref[0])

        @pl.when(t == n - 1)
        def _():
            o_ref[0] = acc_ref[...].astype(bf16)

    return pl.pallas_call(
        body, name="ffn_bwd_dw_gu", grid=(N_CHIPS, n),
        in_specs=[pl.BlockSpec((ts, D_MODEL), lambda j, t: (t, 0)),
                  pl.BlockSpec((1, ts, FF_CHUNK), lambda j, t: (j // 2, t, j % 2))],
        out_specs=pl.BlockSpec((1, D_MODEL, FF_CHUNK), lambda j, t: (j, 0, 0)),
        out_shape=jax.ShapeDtypeStruct((N_CHIPS, D_MODEL, FF_CHUNK), bf16),
        scratch_shapes=[pltpu.VMEM((D_MODEL, FF_CHUNK), f32)],
        compiler_params=_cp("parallel", "arbitrary"),
    )(h2, dgu)


def _ffn_bwd_dw_down(f, dx2):
    s = dx2.shape[0]
    ts = min(2 * ROW_TILE, s)
    n = s // ts

    def body(f_ref, d_ref, o_ref, acc_ref):
        t = pl.program_id(1)

        @pl.when(t == 0)
        def _():
            acc_ref[...] = jnp.zeros_like(acc_ref)

        acc_ref[...] += _dot_tn(f_ref[...], d_ref[...].astype(bf16))

        @pl.when(t == n - 1)
        def _():
            o_ref[...] = acc_ref[...].astype(bf16)

    return pl.pallas_call(
        body, name="ffn_bwd_dw_down", grid=(2, n),
        in_specs=[pl.BlockSpec((ts, FF_CHUNK), lambda c, t: (t, c)),
                  pl.BlockSpec((ts, D_MODEL), lambda c, t: (t, 0))],
        out_specs=pl.BlockSpec((FF_CHUNK, D_MODEL), lambda c, t: (c, 0)),
        out_shape=jax.ShapeDtypeStruct((D_FF, D_MODEL), bf16),
        scratch_shapes=[pltpu.VMEM((FF_CHUNK, D_MODEL), f32)],
        compiler_params=_cp("parallel", "arbitrary"),
    )(f, dx2)


def _mix_bwd_local(dx1, mix, u1, pooled, w_out, ln_g, ln_b, pool_w, pool_scale):
    s = dx1.shape[0]
    tm = min(2 * ROW_TILE, s)
    n = s // tm

    def body(dx1_ref, mix_ref, u1_ref, po_ref, wo_ref, lg_ref, lb_ref, pw_ref, ps_ref,
             du1_ref, dpo_ref, dwo_ref, dpw_ref, small_ref, awo_ref):
        i = pl.program_id(0)

        @pl.when(i == 0)
        def _():
            awo_ref[...] = jnp.zeros_like(awo_ref)
            dpw_ref[...] = jnp.zeros_like(dpw_ref)
            small_ref[...] = jnp.zeros_like(small_ref)

        dyb = dx1_ref[...].astype(bf16)
        dmix = _dot_nt(dyb, wo_ref[...])
        awo_ref[...] += _dot_tn(mix_ref[...], dyb)
        u1 = u1_ref[...]
        mu = jnp.mean(u1, axis=-1, keepdims=True)
        uc = u1 - mu
        rstd = lax.rsqrt(jnp.mean(uc * uc, axis=-1, keepdims=True) + EPS)
        uh = uc * rstd
        u2 = uh * lg_ref[...] + lb_ref[...]
        du2 = dmix[:, :C_CONV] * _silu_grad(u2, _sigmoid(u2))
        duh = du2 * lg_ref[...]
        du1 = rstd * (duh - jnp.mean(duh, axis=-1, keepdims=True) - uh * jnp.mean(duh * uh, axis=-1, keepdims=True))
        du1_ref[...] = du1
        small_ref[0:1, :] += jnp.sum(du2 * uh, axis=0, keepdims=True)
        small_ref[1:2, :] += jnp.sum(du2, axis=0, keepdims=True)
        small_ref[2:3, :] += jnp.sum(du1, axis=0, keepdims=True)
        for g in range(len(POOL_WINDOWS)):
            cols = slice(g * POOL_GROUP, (g + 1) * POOL_GROUP)
            dq = dmix[:, C_CONV + g * POOL_GROUP:C_CONV + (g + 1) * POOL_GROUP]
            pwb = pw_ref[g].astype(bf16)
            pg = po_ref[:, cols]
            mixed = _dot(pg, pwb)
            small_ref[3:4, cols] += jnp.sum(dq * mixed, axis=0, keepdims=True)
            dmixed = (dq * ps_ref[:, cols]).astype(bf16)
            dpw_ref[g] += _dot_tn(pg, dmixed)
            dpo_ref[:, cols] = _dot_nt(dmixed, pwb)

        @pl.when(i == n - 1)
        def _():
            dwo_ref[...] = awo_ref[...].astype(bf16)

    return pl.pallas_call(
        body, name="mix_bwd_local", grid=(n,),
        in_specs=[_row(0, (tm, D_MODEL)), _row(0, (tm, D_MODEL)), _row(0, (tm, C_CONV)), _row(0, (tm, C_POOL)),
                  _full((D_MODEL, D_MODEL)), _full((1, C_CONV)), _full((1, C_CONV)),
                  _full((4, POOL_GROUP, POOL_GROUP)), _full((1, C_POOL))],
        out_specs=[_row(0, (tm, C_CONV)), _row(0, (tm, C_POOL)), _full((D_MODEL, D_MODEL)),
                   _full((4, POOL_GROUP, POOL_GROUP)), _full((8, C_CONV))],
        out_shape=[jax.ShapeDtypeStruct((s, C_CONV), f32), jax.ShapeDtypeStruct((s, C_POOL), f32),
                   jax.ShapeDtypeStruct((D_MODEL, D_MODEL), bf16),
                   jax.ShapeDtypeStruct((4, POOL_GROUP, POOL_GROUP), f32), jax.ShapeDtypeStruct((8, C_CONV), f32)],
        scratch_shapes=[pltpu.VMEM((D_MODEL, D_MODEL), f32)],
        compiler_params=_cp("arbitrary"),
    )(dx1, mix, u1, pooled, w_out, ln_g, ln_b, pool_w, pool_scale)


def _in_bwd(du1, dpo, z, x, dx1, conv_w, g_mix, w_in_g, parts, small_parts):
    s = x.shape[0]
    tm = min(ROW_TILE, s)
    n = s // tm
    hb = tm // HALO
    last = s // HALO - 1
    nv = tm // 8
    assert nv >= SHIFT_PAD and nv % SHIFT_GROUPS == 0
    nc, ns = len(parts), len(small_parts)
    cs = _crossed(parts)

    def body(du_ref, dun_ref, dp_ref, dpn_ref, z_ref, zp_ref, x_ref, dx1_ref, cw_ref, g_ref, w_ref, *refs):
        outs = refs[nc + ns:]
        gx_ref, dw_ref, dcw_ref, small_ref = outs[:4]
        eu, ed, ep, ss, u0_ref, dz_ref, acc_ref, dcw_acc = outs[4 + nc + ns:12 + nc + ns]
        sems = outs[12 + nc + ns:]
        cross = _Cross(refs[:nc], outs[4:4 + nc], sems[0], sems[1])
        gather = _SmallGather(refs[nc:nc + ns], outs[4 + nc:4 + nc + ns], sems[2], sems[3])
        i = pl.program_id(0)
        pl.when(i == 0)(cross.start)
        pl.when(i == 0)(gather.start)

        @pl.when(i == 0)
        def _():
            acc_ref[...] = jnp.zeros_like(acc_ref)
            dcw_acc[...] = jnp.zeros_like(dcw_acc)
            small_ref[...] = jnp.zeros_like(small_ref)

        keep_prev = (i > 0).astype(f32)
        keep_next = (i < n - 1).astype(f32)
        zp = zp_ref[...] * keep_prev
        u0_prev = zp[:, :C_CONV] * _sigmoid(zp[:, C_CONV:2 * C_CONV])
        u0_ref[...] = z_ref[:, :C_CONV] * _sigmoid(z_ref[:, C_CONV:2 * C_CONV])
        du_next = dun_ref[...] * keep_next
        for c, w_pool in enumerate(POOL_WINDOWS):
            lanes = slice(c * 128, (c + 1) * 128)
            eu[c, pl.ds(0, SHIFT_PAD, stride=8), :] = u0_prev[:, lanes]
            ed[c, pl.ds(nv * 8 + 7, SHIFT_PAD, stride=8), :] = du_next[:, lanes]
            ep[c, pl.ds(nv * 8 + 7, SHIFT_PAD, stride=8), :] = (
                dpn_ref[:, lanes] * keep_next / _pool_counts(HALO, w_pool, (i + 1) * tm))
            for j in range(8):
                rows = slice(j * nv, (j + 1) * nv)
                eu[c, pl.ds(SHIFT_PAD * 8 + j, nv, stride=8), :] = u0_ref[rows, lanes]
                ed[c, pl.ds(j, nv, stride=8), :] = du_ref[rows, lanes]
                ep[c, pl.ds(j, nv, stride=8), :] = dp_ref[rows, lanes] / _pool_counts(nv, w_pool, i * tm + j * nv)
                if j >= 1:
                    eu[c, pl.ds(j, SHIFT_PAD, stride=8), :] = u0_ref[j * nv - SHIFT_PAD:j * nv, lanes]
                if j <= 6:
                    edge = slice((j + 1) * nv, (j + 1) * nv + SHIFT_PAD)
                    ed[c, pl.ds(nv * 8 + j, SHIFT_PAD, stride=8), :] = du_ref[edge, lanes]
                    ep[c, pl.ds(nv * 8 + j, SHIFT_PAD, stride=8), :] = (
                        dp_ref[edge, lanes] / _pool_counts(SHIFT_PAD, w_pool, i * tm + (j + 1) * nv))
        for c, w_pool in enumerate(POOL_WINDOWS):
            lanes = slice(c * 128, (c + 1) * 128)
            b_lanes = slice(C_CONV + c * 128, C_CONV + (c + 1) * 128)
            v_lanes = slice(2 * C_CONV + c * 128, 2 * C_CONV + (c + 1) * 128)
            for v0 in range(0, nv, SHIFT_GROUPS):
                span = SHIFT_GROUPS * 8
                acc = jnp.zeros((span, 128), f32)
                for k in range(CONV_K):
                    acc = acc + cw_ref[pl.ds(k, 1), lanes] * ed[c, pl.ds((v0 + CONV_K - 1 - k) * 8, span), :]
                ss[0, v0 * 8:v0 * 8 + span, :] = acc
                acc = ep[c, pl.ds(v0 * 8, span), :]
                for d in range(1, w_pool):
                    acc = acc + ep[c, pl.ds((v0 + d) * 8, span), :]
                ss[1, v0 * 8:v0 * 8 + span, :] = acc
                d1 = ed[c, pl.ds(v0 * 8, span), :]
                for k in range(CONV_K):
                    prod = d1 * eu[c, pl.ds((SHIFT_PAD - (CONV_K - 1) + v0 + k) * 8, span), :]
                    fold = prod[0:8]
                    for r in range(8, span, 8):
                        fold = fold + prod[r:r + 8]
                    dcw_acc[k, :, lanes] += fold
            for j in range(8):
                rows = slice(j * nv, (j + 1) * nv)
                du0 = ss[0, pl.ds(j, nv, stride=8), :]
                av, sv = z_ref[rows, lanes], _sigmoid(z_ref[rows, b_lanes])
                dz_ref[rows, lanes] = (du0 * sv).astype(bf16)
                dz_ref[rows, b_lanes] = (du0 * av * sv * (1.0 - sv)).astype(bf16)
                dz_ref[rows, v_lanes] = (ss[1, pl.ds(j, nv, stride=8), :] - dp_ref[rows, lanes]).astype(bf16)
        h, xh, r = _rms_fwd(x_ref[...], g_ref[...])
        dz = dz_ref[...]
        acc_ref[...] += _dot_tn(h.astype(bf16), dz)
        dh = _dot_nt(dz[:, 0:W_IN_CHUNK], w_ref[0])
        for j in range(1, N_CHIPS):
            dh = dh + _dot_nt(dz[:, j * W_IN_CHUNK:(j + 1) * W_IN_CHUNK], w_ref[j])
        dxb, dg = _rms_bwd(dh, xh, r, g_ref[...])
        gx_ref[...] = dx1_ref[...] + dxb
        small_ref[0:1, :] += dg

        @pl.when(i == n - 1)
        def _():
            for j in range(N_CHIPS):
                dw_ref[j] = acc_ref[:, j * W_IN_CHUNK:(j + 1) * W_IN_CHUNK].astype(bf16)
            dcw_ref[...] = jnp.sum(dcw_acc[...], axis=1)

        pl.when(i == n - 1)(gather.finish)
        pl.when(i == n - 1)(cross.finish)

    nxt = lambda i: (jnp.minimum((i + 1) * hb, last), 0)
    res = pl.pallas_call(
        body, name="in_bwd", grid=(n,),
        in_specs=[_row(0, (tm, C_CONV)), pl.BlockSpec((HALO, C_CONV), nxt),
                  _row(0, (tm, C_POOL)), pl.BlockSpec((HALO, C_POOL), nxt),
                  _row(0, (tm, W_IN_COLS)),
                  pl.BlockSpec((HALO, W_IN_COLS), lambda i: (jnp.maximum(i * hb - 1, 0), 0)),
                  _row(0, (tm, D_MODEL)), _row(0, (tm, D_MODEL)), _full((HALO, C_CONV)), _full((1, D_MODEL)),
                  _full((N_CHIPS, D_MODEL, W_IN_CHUNK))] + cs["in_specs"] + [ANY] * ns,
        out_specs=[_row(0, (tm, D_MODEL)), _full((N_CHIPS, D_MODEL, W_IN_CHUNK)), _full((HALO, C_CONV)),
                   _full((8, D_MODEL))] + cs["out_specs"] + [ANY] * ns,
        out_shape=[jax.ShapeDtypeStruct((s, D_MODEL), f32), jax.ShapeDtypeStruct((N_CHIPS, D_MODEL, W_IN_CHUNK), bf16),
                   jax.ShapeDtypeStruct((HALO, C_CONV), f32), jax.ShapeDtypeStruct((8, D_MODEL), f32)] + cs["out_shape"]
        + _gathered_shapes(small_parts),
        scratch_shapes=[pltpu.VMEM((4, (SHIFT_PAD + nv) * 8, 128), f32), pltpu.VMEM((4, (nv + SHIFT_PAD) * 8, 128), f32),
                        pltpu.VMEM((4, (nv + SHIFT_PAD) * 8, 128), f32), pltpu.VMEM((2, tm, 128), f32),
                        pltpu.VMEM((tm, C_CONV), f32), pltpu.VMEM((tm, W_IN_COLS), bf16),
                        pltpu.VMEM((D_MODEL, W_IN_COLS), f32), pltpu.VMEM((HALO, 8, C_CONV), f32)] + cs["scratch"]
        + _SmallGather.scratch(ns),
        compiler_params=_cp("arbitrary"),
    )(du1, du1, dpo, dpo, z, z, x, dx1, conv_w, g_mix, w_in_g, *parts, *small_parts)
    return res[:4], res[4:4 + nc], res[4 + nc:]


def _pair_reduce(name, grads):
    nk = len(grads)
    halves = [g.shape[1] // 2 for g in grads]

    def body(*refs):
        ins, outs, got = refs[:nk], refs[nk:2 * nk], refs[2 * nk:3 * nk]
        send_sems, recv_sems = refs[3 * nk:]
        x, y, c, _, _ = _place()

        def half(k, core):
            return pl.ds(pl.multiple_of(core * halves[k], 16), halves[k])

        cps = [_remote(ins[k].at[:, half(k, 1 - c), :], got[k], send_sems.at[k], recv_sems.at[k], (x, y, 1 - c))
               for k in range(nk)]
        for cp in cps:
            cp.start()
        for k, cp in enumerate(cps):
            cp.wait_recv()
            outs[k][...] = (ins[k][:, half(k, c), :].astype(f32) + got[k][...].astype(f32)).astype(bf16)
        for cp in cps:
            cp.wait_send()

    shapes = [(N_CHIPS, h, g.shape[2]) for g, h in zip(grads, halves)]
    return pl.pallas_call(
        body, name=f"pair_reduce_{name}", in_specs=[VMEM] * nk, out_specs=[VMEM] * nk,
        out_shape=[jax.ShapeDtypeStruct(s, bf16) for s in shapes],
        scratch_shapes=[pltpu.VMEM(s, bf16) for s in shapes]
        + [pltpu.SemaphoreType.DMA((nk,)), pltpu.SemaphoreType.DMA((nk,))],
        compiler_params=pltpu.CompilerParams(vmem_limit_bytes=VMEM_LIMIT),
    )(*grads)


class _Cross:
    def __init__(self, parts, landed, send_sems, recv_sems):
        self.parts, self.landed, self.send_sems, self.recv_sems = parts, landed, send_sems, recv_sems
        _, _, self.c, self.me, self.others = _place()

    def _copy(self, k, j, src_chunk, dst_slot):
        ox, oy = self.others[j]
        return _remote(self.parts[k].at[src_chunk], self.landed[k].at[dst_slot], self.send_sems.at[3 * k + j],
                       self.recv_sems.at[3 * k + j], (ox, oy, self.c))

    def _each(self):
        return [(k, j, 2 * self.others[j][0] + self.others[j][1]) for k in range(len(self.parts)) for j in range(3)]

    def start(self):
        for k, j, chip in self._each():
            self._copy(k, j, chip, self.me).start()

    def finish(self):
        for k, j, chip in self._each():
            self._copy(k, j, chip, chip).wait_recv()
        for k, j, chip in self._each():
            self._copy(k, j, chip, self.me).wait_send()

    @staticmethod
    def scratch(n):
        return [pltpu.SemaphoreType.DMA((3 * n,)), pltpu.SemaphoreType.DMA((3 * n,))]


class _CrossAll:
    def __init__(self, grads, landed, send_sems, recv_sems):
        self.grads, self.landed, self.send_sems, self.recv_sems = grads, landed, send_sems, recv_sems
        self.x, self.y, self.c, self.me, self.others = _place()
        self.dev = 2 * self.me + self.c

    def _piece(self, k, chip, half):
        rows = self.grads[k].shape[1] // 2
        return self.grads[k].at[chip, pl.ds(half * rows, rows), :]

    def _to_sibling(self, k):
        return _remote(self._piece(k, self.me, 1 - self.c), self.landed[k].at[self.dev], self.send_sems.at[7 * k],
                       self.recv_sems.at[7 * k], (self.x, self.y, 1 - self.c))

    def _to_chip(self, k, j, half):
        ox, oy = self.others[j]
        return _remote(self._piece(k, 2 * ox + oy, half), self.landed[k].at[self.dev],
                       self.send_sems.at[7 * k + 1 + 2 * j + half], self.recv_sems.at[7 * k + 1 + 2 * j + self.c],
                       (ox, oy, half))

    def _from(self, k, sem, slot):
        return _remote(self.landed[k].at[slot], self.landed[k].at[slot], self.send_sems.at[7 * k + sem],
                       self.recv_sems.at[7 * k + sem], (self.x, self.y, 1 - self.c))

    def start(self):
        for k in range(len(self.grads)):
            self._to_sibling(k).start()
            for j in range(3):
                for half in range(2):
                    self._to_chip(k, j, half).start()

    def finish(self):
        for k in range(len(self.grads)):
            self._from(k, 0, 2 * self.me + 1 - self.c).wait_recv()
            for j, (ox, oy) in enumerate(self.others):
                for core in range(2):
                    self._from(k, 1 + 2 * j + core, 4 * ox + 2 * oy + core).wait_recv()
        for k in range(len(self.grads)):
            self._to_sibling(k).wait_send()
            for j in range(3):
                for half in range(2):
                    self._to_chip(k, j, half).wait_send()

    @staticmethod
    def scratch(n):
        return [pltpu.SemaphoreType.DMA((7 * n,)), pltpu.SemaphoreType.DMA((7 * n,))]


def _rs_sum_chips(name, place, landed, part):
    _, half, cols = landed.shape

    def body(place_ref, l_ref, p_ref, o_ref):
        me = place_ref[0]
        own = p_ref[0].astype(f32)
        acc = jnp.where(me == 0, own, l_ref[0].astype(f32))
        for j in range(1, N_CHIPS):
            acc = acc + jnp.where(me == j, own, l_ref[j].astype(f32))
        o_ref[...] = acc

    return pl.pallas_call(
        body, name=f"rs_sum_chips_{name}",
        grid_spec=pltpu.PrefetchScalarGridSpec(
            num_scalar_prefetch=1, grid=(1,),
            in_specs=[pl.BlockSpec((N_CHIPS, half, cols), lambda t, place_ref: (0, 0, 0)),
                      pl.BlockSpec((1, half, cols), lambda t, place_ref: (place_ref[0], 0, 0))],
            out_specs=pl.BlockSpec((half, cols), lambda t, place_ref: (place_ref[1], 0))),
        out_shape=jax.ShapeDtypeStruct((2 * half, cols), f32),
        compiler_params=_cp("arbitrary"),
    )(place, landed, part)


def _grad_pair(shards):
    nk = len(shards)

    def body(*refs):
        outs = refs[nk:2 * nk]
        send_sems, recv_sems = refs[2 * nk:]
        x, y, c, _, _ = _place()

        def half(k, core):
            h = outs[k].shape[0] // 2
            return outs[k].at[pl.ds(core * h, h), :]

        cps = [_remote(half(k, c), half(k, c), send_sems.at[k], recv_sems.at[k], (x, y, 1 - c)) for k in range(nk)]
        for cp in cps:
            cp.start()
        for k in range(nk):
            _remote(half(k, 1 - c), half(k, 1 - c), send_sems.at[k], recv_sems.at[k], (x, y, 1 - c)).wait_recv()
        for cp in cps:
            cp.wait_send()

    return pl.pallas_call(
        body, name="grad_pair", in_specs=[ANY] * nk, out_specs=[ANY] * nk,
        out_shape=[jax.ShapeDtypeStruct(a.shape, f32) for a in shards],
        input_output_aliases={k: k for k in range(nk)},
        scratch_shapes=[pltpu.SemaphoreType.DMA((nk,)), pltpu.SemaphoreType.DMA((nk,))],
    )(*shards)


def _adam_math(w, g, m, v):
    m = ADAM_B1 * m + (1.0 - ADAM_B1) * g
    v = ADAM_B2 * v + (1.0 - ADAM_B2) * (g * g)
    m_hat = m / (1.0 - ADAM_B1 ** ADAM_STEP)
    v_hat = v / (1.0 - ADAM_B2 ** ADAM_STEP)
    delta = -ADAM_LR * (m_hat / (jnp.sqrt(v_hat) + ADAM_EPS) + ADAM_WD * w)
    return delta, m, v


def _adam(ws, gs, ms, vs):
    n = len(ws)
    steps = 2 * SHARD_STEPS

    def body(*refs):
        for k in range(n):
            w_ref, g_ref, m_ref, v_ref = (refs[j * n + k] for j in range(4))
            go_ref, d_ref, nm_ref, nv_ref = (refs[(4 + j) * n + k] for j in range(4))
            g = g_ref[...]
            go_ref[...] = g
            d_ref[...], nm_ref[...], nv_ref[...] = _adam_math(w_ref[...], g, m_ref[...], v_ref[...])

    specs = [_row(0, (w.shape[0] // steps, w.shape[1])) for w in ws]
    res = pl.pallas_call(
        body, name="adam_shards", grid=(steps,), in_specs=specs * 4, out_specs=specs * 4,
        out_shape=[jax.ShapeDtypeStruct(w.shape, f32) for w in ws] * 4,
        compiler_params=_cp("parallel"),
    )(*ws, *gs, *ms, *vs)
    return [tuple(res[j * n + k] for j in range(4)) for k in range(n)]


class _SmallGather:
    def __init__(self, parts, bufs, send_sems, recv_sems):
        self.parts, self.bufs, self.send_sems, self.recv_sems = parts, bufs, send_sems, recv_sems
        self.x, self.y, self.c, _, self.others = _place()
        self.sibling = (self.x, self.y, 1 - self.c)

    def _copy(self, a, k, block, to, src=None):
        slot = self.bufs[a].at[4 * block[0] + 2 * block[1] + block[2]]
        return _remote(slot if src is None else src, slot, self.send_sems.at[7 * a + k], self.recv_sems.at[7 * a + k],
                       to)

    def _first(self, a):
        me = (self.x, self.y, self.c)
        return [self._copy(a, 0, me, self.sibling, src=self.parts[a])] + [
            self._copy(a, 1 + j, me, (*chip, self.c), src=self.parts[a]) for j, chip in enumerate(self.others)]

    def _passed(self, a):
        return [self._copy(a, 4 + j, (*chip, self.c), self.sibling) for j, chip in enumerate(self.others)]

    @staticmethod
    def scratch(n):
        return [pltpu.SemaphoreType.DMA((7 * n,)), pltpu.SemaphoreType.DMA((7 * n,))]

    def start(self):
        for a in range(len(self.parts)):
            for cp in self._first(a):
                cp.start()

    def finish(self):
        sent = []
        for a in range(len(self.parts)):
            passed = self._passed(a)
            for j, chip in enumerate(self.others):
                self._copy(a, 1 + j, (*chip, self.c), self.sibling).wait_recv()
                passed[j].start()
            sent += self._first(a) + passed
        for a in range(len(self.parts)):
            self._copy(a, 0, self.sibling, self.sibling).wait_recv()
            for j, chip in enumerate(self.others):
                self._copy(a, 4 + j, (*chip, 1 - self.c), self.sibling).wait_recv()
        for cp in sent:
            cp.wait_send()


def _gathered_shapes(parts):
    return [jax.ShapeDtypeStruct((N_DEV,) + p.shape, p.dtype) for p in parts]


def _rs_sum_group(name, place, landed, parts, landed_all, grads_all, cross_parts, small_parts):
    nk, na, nx, ns = len(landed), len(landed_all), len(cross_parts), len(small_parts)
    dims = [a.shape[1:] for a in landed]
    dims_all = [a.shape[1:] for a in landed_all]

    def body(place_ref, *refs):
        take = iter(refs)
        l_refs, p_refs, la_refs, ga_refs, x_refs, sp_refs, o_refs, oa_refs, xl_refs, sbufs = (
            [next(take) for _ in range(cnt)] for cnt in (nk, nk, na, na, nx, ns, nk, na, nx, ns))
        sems = list(take)
        cross = _Cross(x_refs, xl_refs, sems[0], sems[1])
        small = _SmallGather(sp_refs, sbufs, sems[2], sems[3])
        t = pl.program_id(0)

        @pl.when(t == 0)
        def _():
            cross.start()
            small.start()

        me = place_ref[0]
        for l_ref, p_ref, o_ref in zip(l_refs, p_refs, o_refs):
            own = p_ref[0].astype(f32)
            acc = jnp.where(me == 0, own, l_ref[0].astype(f32))
            for j in range(1, N_CHIPS):
                acc = acc + jnp.where(me == j, own, l_ref[j].astype(f32))
            o_ref[...] = acc
        dev = 2 * me + place_ref[1]
        for l_ref, g_ref, o_ref in zip(la_refs, ga_refs, oa_refs):
            own = g_ref[0].astype(f32)
            acc = jnp.where(dev == 0, own, l_ref[0].astype(f32))
            for d in range(1, N_DEV):
                acc = acc + jnp.where(dev == d, own, l_ref[d].astype(f32))
            o_ref[...] = acc

        @pl.when(t == 1)
        def _():
            small.finish()
            cross.finish()

    def halves(h, c, lead, index):
        return pl.BlockSpec((lead, h // 2, c) if lead else (h // 2, c), index)

    in_specs = [halves(h, c, N_CHIPS, lambda t, pr: (0, t, 0)) for h, c in dims]
    in_specs += [halves(h, c, 1, lambda t, pr: (pr[0], t, 0)) for h, c in dims]
    in_specs += [halves(h, c, N_DEV, lambda t, pr: (0, t, 0)) for h, c in dims_all]
    in_specs += [halves(h, c, 1, lambda t, pr: (pr[0], 2 * pr[1] + t, 0)) for h, c in dims_all]
    in_specs += [ANY] * (nx + ns)
    out_specs = [halves(h, c, 0, lambda t, pr: (2 * pr[1] + t, 0)) for h, c in dims + dims_all] + [ANY] * (nx + ns)
    out_shape = [jax.ShapeDtypeStruct((2 * h, c), f32) for h, c in dims + dims_all]
    out_shape += [jax.ShapeDtypeStruct(a.shape, a.dtype) for a in cross_parts] + _gathered_shapes(small_parts)
    res = pl.pallas_call(
        body, name=f"rs_sum_{name}",
        grid_spec=pltpu.PrefetchScalarGridSpec(
            num_scalar_prefetch=1, grid=(2,), in_specs=in_specs, out_specs=out_specs,
            scratch_shapes=_Cross.scratch(nx) + _SmallGather.scratch(ns)),
        out_shape=out_shape, compiler_params=_cp("arbitrary"),
    )(place, *landed, *parts, *landed_all, *grads_all, *cross_parts, *small_parts)
    return res[:nk], res[nk:nk + na], res[nk + na:nk + na + nx], res[nk + na + nx:]


SMALL_PARAMS = ("g_ple_gate", "g_ple_post", "g_final", "g_ffn", "ln_g", "ln_b", "conv_b", "pool_scale", "pool_w",
                "conv_w", "g_mix")
SMALL_ROWS = {"g_ple_gate": (0, 0), "g_ple_post": (0, 1), "g_final": (0, 2), "g_ffn": (1, 0), "ln_g": (2, 0),
              "ln_b": (2, 1), "conv_b": (2, 2), "pool_scale": (2, 3), "g_mix": (5, 0)}
LOSS_ROW = (0, 3)


def _small_adam(place, gathered, parts, params):
    nb, names = len(parts), SMALL_PARAMS
    flat = [a for nm in names for a in params[nm]]

    def body(place_ref, *refs):
        b_refs, p_refs = refs[:nb], refs[nb:2 * nb]
        w_refs = refs[2 * nb:2 * nb + 3 * len(names)]
        outs = refs[2 * nb + 3 * len(names):]
        loss_ref, o_refs, cw_sum = outs[0], outs[1:1 + 4 * len(names)], outs[1 + 4 * len(names)]
        chip = place_ref[0]
        me = 2 * chip + place_ref[1]

        def total(blk, idx):
            own = p_refs[blk][idx]
            g = jnp.where(me == 0, own, b_refs[blk][(0,) + idx])
            for d in range(1, N_DEV):
                g = g + jnp.where(me == d, own, b_refs[blk][(d,) + idx])
            return g

        everything = (slice(None), slice(None))
        loss_ref[...] = total(LOSS_ROW[0], (pl.ds(LOSS_ROW[1], 1), pl.ds(0, 128)))
        d_cw = total(4, everything)
        mine = jnp.where(chip == 0, d_cw[:, 0:128], 0.0)
        for j in range(1, N_CHIPS):
            mine = mine + jnp.where(chip == j, d_cw[:, j * 128:(j + 1) * 128], 0.0)
        cw_sum[...] = mine
        for k, nm in enumerate(names):
            w_ref, m_ref, v_ref = w_refs[3 * k:3 * k + 3]
            g_ref, d_ref, nm_ref, nv_ref = o_refs[4 * k:4 * k + 4]
            if nm == "pool_w":
                g = total(3, everything + (slice(None),))
            elif nm == "conv_w":
                g = cw_sum[pl.ds(0, CONV_K), :]
            else:
                blk, row = SMALL_ROWS[nm]
                g = total(blk, (pl.ds(row, 1), slice(None)))
            g_ref[...] = g
            d_ref[...], nm_ref[...], nv_ref[...] = _adam_math(w_ref[...], g, m_ref[...], v_ref[...])

    whole = lambda a: pl.BlockSpec(a.shape, lambda t, pr: (0,) * a.ndim)
    out_shape = [jax.ShapeDtypeStruct((1, 128), f32)]
    out_shape += [jax.ShapeDtypeStruct(params[nm][0].shape, f32) for nm in names for _ in range(4)]
    res = pl.pallas_call(
        body, name="small_adam",
        grid_spec=pltpu.PrefetchScalarGridSpec(
            num_scalar_prefetch=1, grid=(1,),
            in_specs=[whole(a) for a in list(gathered) + list(parts) + flat],
            out_specs=[whole(s) for s in out_shape], scratch_shapes=[pltpu.VMEM((HALO, 128), f32)]),
        out_shape=out_shape, compiler_params=_cp("arbitrary"),
    )(place, *gathered, *parts, *flat)
    return res[0], {nm: res[1 + 4 * k:5 + 4 * k] for k, nm in enumerate(names)}


def _pad_rows(a, rows):
    return jnp.concatenate([a, jnp.zeros((rows - a.shape[0],) + a.shape[1:], a.dtype)], axis=0)


def kernel(x, p, g_mix, w_in, conv_w, conv_b, ln_g, ln_b, pool_w, pool_scale, w_out, g_ffn, w_gate_up, w_down, g_ple_gate, w_ple_gate, w_ple_up, g_ple_post, g_final, loss_target, m_g_mix, m_w_in, m_conv_w, m_conv_b, m_ln_g, m_ln_b, m_pool_w, m_pool_scale, m_w_out, m_g_ffn, m_w_gate_up, m_w_down, m_g_ple_gate, m_w_ple_gate, m_w_ple_up, m_g_ple_post, m_g_final, v_g_mix, v_w_in, v_conv_w, v_conv_b, v_ln_g, v_ln_b, v_pool_w, v_pool_scale, v_w_out, v_g_ffn, v_w_gate_up, v_w_down, v_g_ple_gate, v_w_ple_gate, v_w_ple_up, v_g_ple_post, v_g_final):
    seq = x.shape[1]
    me = 2 * lax.axis_index("x") + lax.axis_index("y")
    chip = me.astype(jnp.int32).reshape(1)
    core = lax.axis_index("c").astype(jnp.int32).reshape(1)
    place = jnp.concatenate([chip, core])
    xs, ps, ts = x.reshape(seq, D_MODEL), p.reshape(seq, D_PLE), loss_target.reshape(seq, D_MODEL)

    big = [w_in[0], w_gate_up[0], w_out[0], w_down[0], w_ple_gate[0], w_ple_up[0]]
    big_m = [m_w_in[0], m_w_gate_up[0], m_w_out[0], m_w_down[0], m_w_ple_gate[0], m_w_ple_up[0]]
    big_v = [v_w_in[0], v_w_gate_up[0], v_w_out[0], v_w_down[0], v_w_ple_gate[0], v_w_ple_up[0]]
    b_in, b_gu, b_out, b_down, b_pg, b_pu, b_cw = _cast_into_slots(
        chip, big + [_pad_rows(conv_w[0], HALO)], [bf16] * len(big) + [f32])
    xi, yi = lax.axis_index("x"), lax.axis_index("y")
    order = jnp.stack([me, 2 * (1 - xi) + yi, 2 * xi + 1 - yi, 2 * (1 - xi) + 1 - yi]).astype(jnp.int32)

    z, (w_in_g, cw_g, w_out_g) = _mix_in(xs, g_mix, order, [b_in, b_cw, b_out])
    conv_w_f = cw_g.transpose(1, 0, 2).reshape(HALO, C_CONV)
    w_out_f = w_out_g.reshape(D_MODEL, D_MODEL)
    (x1, mix, u1, pooled, h2), (w_gu_g, w_down_g) = _conv_pool_out(z, xs, conv_w_f, conv_b, ln_g, ln_b, pool_w[0],
                                                                    pool_scale, w_out_f, g_ffn, [b_gu, b_down])
    w_down_f = w_down_g.reshape(D_FF, D_MODEL)
    (x2, gu, ffn_f), (w_pg_g, w_pu_g) = _ffn_fwd(x1, h2, w_gu_g, w_down_f, [b_pg, b_pu])
    w_pg_f = w_pg_g.reshape(D_MODEL, D_MODEL)
    dx2, d_w_pg, d_w_pu, small_ple = _ple_loss(x2, ps, ts, g_ple_gate, g_ple_post, g_final.reshape(1, D_MODEL),
                                               w_pg_f, w_pu_g.transpose(1, 0, 2).reshape(D_PLE, D_MODEL))
    d_w_down = _ffn_bwd_dw_down(ffn_f, dx2)
    grads_a = [d_w_down.reshape(N_CHIPS, -1, D_MODEL), d_w_pg.reshape(N_CHIPS, -1, D_MODEL), d_w_pu]
    (dx1, dgu, small_ffn), landed_a = _ffn_bwd_dx(dx2, x1, gu, g_ffn, w_gu_g, w_down_f, grads_a)
    d_w_gu = _ffn_bwd_dw_gu(h2, dgu)
    du1, dpo, d_w_out, d_pool_w, small_mix = _mix_bwd_local(dx1, mix, u1, pooled, w_out_f, ln_g, ln_b, pool_w[0],
                                                             pool_scale)
    parts_b = _pair_reduce("b", [d_w_gu, d_w_out.reshape(N_CHIPS, -1, D_MODEL)])
    small_0 = [small_ple, small_ffn, small_mix, d_pool_w]
    (grad_x, d_w_in, d_conv_w, small_in), landed_b, small_all_0 = _in_bwd(du1, dpo, z, xs, dx1, conv_w_f, g_mix,
                                                                            w_in_g, parts_b, small_0)
    parts_c = _pair_reduce("c", [d_w_in])
    small_1 = [d_conv_w, small_in]
    (h_gu, h_out), (h_down, h_pg, h_pu), landed_c, small_all_1 = _rs_sum_group(
        "ab", place, landed_b, parts_b, landed_a, grads_a, parts_c, small_1)
    h_in = _rs_sum_chips("w_in", place, landed_c[0], parts_c[0])
    big_g = _grad_pair([h_in, h_gu, h_out, h_down, h_pg, h_pu])
    big_upd = _adam(big, big_g, big_m, big_v)

    p3 = lambda w, m, v: (w, m, v)
    row = lambda a: a.reshape(1, D_MODEL)
    params = dict(
        g_ple_gate=p3(g_ple_gate, m_g_ple_gate, v_g_ple_gate), g_ple_post=p3(g_ple_post, m_g_ple_post, v_g_ple_post),
        g_final=p3(row(g_final), row(m_g_final), row(v_g_final)), g_ffn=p3(g_ffn, m_g_ffn, v_g_ffn),
        ln_g=p3(ln_g, m_ln_g, v_ln_g), ln_b=p3(ln_b, m_ln_b, v_ln_b), conv_b=p3(conv_b, m_conv_b, v_conv_b),
        pool_scale=p3(pool_scale, m_pool_scale, v_pool_scale), pool_w=p3(pool_w[0], m_pool_w[0], v_pool_w[0]),
        conv_w=p3(conv_w[0], m_conv_w[0], v_conv_w[0]), g_mix=p3(g_mix, m_g_mix, v_g_mix))
    loss, small = _small_adam(place, list(small_all_0) + list(small_all_1), small_0 + small_1, params)
    back = dict(g_final=lambda a: a.reshape(D_MODEL), pool_w=lambda a: a[None], conv_w=lambda a: a[None])

    names = ["g_mix", "w_in", "conv_w", "conv_b", "ln_g", "ln_b", "pool_w", "pool_scale", "w_out", "g_ffn",
             "w_gate_up", "w_down", "g_ple_gate", "w_ple_gate", "w_ple_up", "g_ple_post", "g_final"]
    big_at = {"w_in": 0, "w_gate_up": 1, "w_out": 2, "w_down": 3, "w_ple_gate": 4, "w_ple_up": 5}
    out = [loss[0, 0], grad_x.reshape(1, seq, D_MODEL)]
    for kind in range(4):
        for nm in names:
            if nm in big_at:
                out.append(big_upd[big_at[nm]][kind][None])
            else:
                out.append(back.get(nm, lambda a: a)(small[nm][kind]))
    return tuple(out)
```

```python
import functools

import jax
import jax.numpy as jnp
from jax import lax
from jax.experimental import pallas as pl
from jax.experimental.pallas import tpu as pltpu

f32, bf16 = jnp.float32, jnp.bfloat16

EPS = 1e-6
D_MODEL = 1024
C_CONV = 512
C_POOL = 512
POOL_WINDOWS = (2, 4, 8, 16)
POOL_GROUP = 128
CONV_K = 31
D_FF = 2816
D_PLE = 256
N_CHIPS = 4
N_DEV = 8
W_IN_COLS = 2 * C_CONV + C_POOL
W_IN_CHUNK = W_IN_COLS // N_CHIPS
FF_CHUNK = 2 * D_FF // N_CHIPS
PLE_CHUNK = D_MODEL // N_CHIPS
HALO = 32
ROW_TILE = 512
CONV_ROWS = 64
CONV_COLS = (slice(0, 256), slice(256, 512))
SHIFT_PAD = 32
SHIFT_GROUPS = 16
FF_SUB = (0, 512, 1024, FF_CHUNK)
VMEM_LIMIT = 56 * 1024 * 1024

ADAM_LR = 0.001
ADAM_B1 = 0.9
ADAM_B2 = 0.999
ADAM_EPS = 1e-08
ADAM_WD = 0.01
ADAM_STEP = 10

MESH = pl.DeviceIdType.MESH
ANY = pl.BlockSpec(memory_space=pl.ANY)
VMEM = pl.BlockSpec(memory_space=pltpu.VMEM)


def _cp(*sem):
    return pltpu.CompilerParams(dimension_semantics=sem, vmem_limit_bytes=VMEM_LIMIT)


def _dot(a, b):
    return jnp.dot(a, b, preferred_element_type=f32)


def _dot_nt(a, b):
    return lax.dot_general(a, b, (((1,), (1,)), ((), ())), preferred_element_type=f32)


def _dot_tn(a, b):
    return lax.dot_general(a, b, (((0,), (0,)), ((), ())), preferred_element_type=f32)


def _sigmoid(v):
    return jax.nn.sigmoid(v)


def _rms_fwd(v, g):
    r = lax.rsqrt(jnp.mean(v * v, axis=-1, keepdims=True) + EPS)
    vh = v * r
    return vh * g, vh, r


def _rms_bwd(dy, vh, r, g):
    dvh = dy * g
    dv = r * (dvh - vh * jnp.mean(dvh * vh, axis=-1, keepdims=True))
    return dv, jnp.sum(dy * vh, axis=0, keepdims=True)


def _silu_grad(v, s):
    return s * (1.0 + v * (1.0 - s))


def _row(i, n):
    return pl.BlockSpec((n[0], n[1]), lambda *a: (a[i], 0))


def _full(shape):
    nd = len(shape)
    return pl.BlockSpec(shape, lambda *a: (0,) * nd)


def _place():
    x, y, c = lax.axis_index("x"), lax.axis_index("y"), lax.axis_index("c")
    others = [(1 - x, y), (x, 1 - y), (1 - x, 1 - y)]
    return x, y, c, 2 * x + y, others


def _remote(src, dst, send_sem, recv_sem, dev):
    return pltpu.make_async_remote_copy(src_ref=src, dst_ref=dst, send_sem=send_sem, recv_sem=recv_sem,
                                        device_id=dev, device_id_type=MESH)


SHARD_STEPS = 4


def _cast_into_slots(me, ws, dtypes):
    n = len(ws)

    def body(me_ref, *refs):
        for w_ref, o_ref, dtype in zip(refs[:n], refs[n:], dtypes):
            o_ref[0] = w_ref[...].astype(dtype)

    return pl.pallas_call(
        body, name="cast_shards",
        grid_spec=pltpu.PrefetchScalarGridSpec(
            num_scalar_prefetch=1, grid=(SHARD_STEPS,),
            in_specs=[pl.BlockSpec((w.shape[0] // SHARD_STEPS, w.shape[1]), lambda r, me_ref: (r, 0)) for w in ws],
            out_specs=[pl.BlockSpec((1, w.shape[0] // SHARD_STEPS, w.shape[1]), lambda r, me_ref: (me_ref[0], r, 0))
                       for w in ws]),
        out_shape=[jax.ShapeDtypeStruct((N_CHIPS,) + w.shape, dt) for w, dt in zip(ws, dtypes)],
        compiler_params=_cp("parallel"),
    )(me, *ws)


class _Gather:
    def __init__(self, bufs, send_sems, recv_sems):
        self.bufs, self.send_sems, self.recv_sems = bufs, send_sems, recv_sems
        self.x, self.y, self.c, self.me, self.others = _place()
        self.halves = [b.shape[1] // 2 for b in bufs]

    def _piece(self, k, chip, half):
        return self.bufs[k].at[chip, pl.ds(half * self.halves[k], self.halves[k]), :]

    def _ici(self, k, j, chip):
        ox, oy = self.others[j]
        piece = self._piece(k, chip, self.c)
        return _remote(piece, piece, self.send_sems.at[6 * k + j], self.recv_sems.at[6 * k + j], (ox, oy, self.c))

    def _relay(self, k):
        first = self.c == 0
        piece = self._piece(k, jnp.where(first, self.chip(0), self.chip(1)), self.c)
        to = (jnp.where(first, self.others[1][0], self.others[0][0]),
              jnp.where(first, self.others[1][1], self.others[0][1]), self.c)
        return _remote(piece, piece, self.send_sems.at[6 * k + 2], self.recv_sems.at[6 * k + 2], to)

    def _pair(self, k, j, half):
        ox, oy = self.others[j]
        piece = self._piece(k, 2 * ox + oy, half)
        return _remote(piece, piece, self.send_sems.at[6 * k + 3 + j], self.recv_sems.at[6 * k + 3 + j],
                       (self.x, self.y, 1 - self.c))

    def _each(self, ks=None):
        return [(k, j) for k in (range(len(self.bufs)) if ks is None else ks) for j in range(3)]

    def chip(self, j):
        ox, oy = self.others[j]
        return 2 * ox + oy

    def start(self):
        for k in range(len(self.bufs)):
            for j in range(2):
                self._ici(k, j, self.me).start()

    def forward(self, pairs=None):
        for k, j in self._each() if pairs is None else pairs:
            self._ici(k, j, self.chip(j)).wait_recv()
            self._pair(k, j, self.c).start()
            if j < 2:
                pl.when(self.c == j)(self._relay(k).start)

    def landed(self, pairs):
        for k, j in pairs:
            self._pair(k, j, 1 - self.c).wait_recv()

    def finish(self, ks=None):
        self.landed(self._each(ks))
        for k in range(len(self.bufs)):
            for j in range(2):
                self._ici(k, j, self.me).wait_send()
            self._relay(k).wait_send()
            for j in range(3):
                self._pair(k, j, self.c).wait_send()

    @staticmethod
    def scratch(n):
        return [pltpu.SemaphoreType.DMA((6 * n,)), pltpu.SemaphoreType.DMA((6 * n,))]


def _carried(bufs):
    n = len(bufs)
    return dict(in_specs=[ANY] * n, out_specs=[ANY] * n,
                out_shape=[jax.ShapeDtypeStruct(b.shape, b.dtype) for b in bufs], scratch=_Gather.scratch(n))


def _mix_in(x, g_mix, order, carry):
    s = x.shape[0]
    tm = min(2 * ROW_TILE, s)
    n = s // tm
    nc = len(carry)
    cs = _carried(carry)

    def body(order_ref, x_ref, g_ref, *refs):
        z_ref = refs[nc]
        bufs = refs[nc + 1:2 * nc + 1]
        h_ref, w_ref, w_sem = refs[2 * nc + 1:2 * nc + 4]
        gather = _Gather(bufs, *refs[2 * nc + 4:])
        q, i = pl.program_id(0), pl.program_id(1)
        first = i == 0
        pl.when(jnp.logical_and(q == 0, first))(gather.start)
        for j in range(3):

            @pl.when(jnp.logical_and(q == j + 1, first))
            def _():
                gather.forward([(0, j)])
                gather.landed([(0, j)])
                if j == 1:
                    gather.forward([(k, jj) for k in range(1, nc) for jj in range(2)])

        @pl.when(first)
        def _():
            load = pltpu.make_async_copy(bufs[0].at[order_ref[q]], w_ref, w_sem)
            load.start()
            load.wait()

        @pl.when(q == 0)
        def _():
            h, _, _ = _rms_fwd(x_ref[...], g_ref[...])
            h_ref[i] = h.astype(bf16)

        z_ref[...] = _dot(h_ref[i], w_ref[...])

        @pl.when(jnp.logical_and(q == N_CHIPS - 1, i == n - 1))
        def _():
            gather.forward([(k, 2) for k in range(1, nc)])
            gather.finish(range(1, nc))

    res = pl.pallas_call(
        body, name="mix_in",
        grid_spec=pltpu.PrefetchScalarGridSpec(
            num_scalar_prefetch=1, grid=(N_CHIPS, n),
            in_specs=[pl.BlockSpec((tm, D_MODEL), lambda q, i, order_ref: (jnp.where(q == 0, i, 0), 0)),
                      pl.BlockSpec((1, D_MODEL), lambda q, i, order_ref: (0, 0))] + cs["in_specs"],
            out_specs=[pl.BlockSpec((tm, W_IN_CHUNK), lambda q, i, order_ref: (i, order_ref[q]))] + cs["out_specs"],
            scratch_shapes=[pltpu.VMEM((n, tm, D_MODEL), bf16), pltpu.VMEM((D_MODEL, W_IN_CHUNK), bf16),
                            pltpu.SemaphoreType.DMA(())] + cs["scratch"]),
        out_shape=[jax.ShapeDtypeStruct((s, W_IN_COLS), f32)] + cs["out_shape"],
        input_output_aliases={3 + k: 1 + k for k in range(nc)},
        compiler_params=_cp("arbitrary", "arbitrary"),
    )(order, x, g_mix, *carry)
    return res[0], res[1:]


def _tap_offsets(lo, hi):
    groups = [[o for o in range(lo, hi + 1) if o % 8 == s] for s in range(8)]
    return [g for g in groups if g]


def _tap_sum(buf, w_ref, row0, cols, tap_of):
    acc = jnp.zeros((CONV_ROWS, cols.stop - cols.start), f32)
    for offs in _tap_offsets(0, CONV_K - 1):
        slab = buf[pl.ds(row0 + offs[0], offs[-1] - offs[0] + CONV_ROWS), cols]
        for o in offs:
            acc = acc + w_ref[pl.ds(tap_of(o), 1), cols] * slab[o - offs[0]:o - offs[0] + CONV_ROWS]
    return acc


def _pool_counts(tm, w, first_row):
    t1 = (lax.broadcasted_iota(jnp.int32, (tm, 1), 0) + first_row + 1).astype(f32)
    return jnp.minimum(t1, float(w))


def _conv_pool_out(z, x, conv_w, conv_b, ln_g, ln_b, pool_w, pool_scale, w_out, g_ffn, carry):
    s = x.shape[0]
    tm = min(ROW_TILE, s)
    n = s // tm
    hb = tm // HALO
    nc = len(carry)
    cs = _carried(carry)

    def body(z_ref, zp_ref, x_ref, cw_ref, cb_ref, lg_ref, lb_ref, pw_ref, ps_ref, wo_ref, gf_ref, *refs):
        x1_ref, mix_ref, u1_ref, pooled_ref, h2_ref = refs[nc:nc + 5]
        ubuf, vbuf = refs[2 * nc + 5:2 * nc + 7]
        gather = _Gather(refs[nc + 5:2 * nc + 5], *refs[2 * nc + 7:])
        i = pl.program_id(0)
        pl.when(i == 0)(gather.start)
        for k in range(nc):
            pl.when(i == min(n // 2 + 2 * k, n - 1))(functools.partial(gather.forward, [(k, 0), (k, 1)]))
        keep = (i > 0).astype(f32)
        zp = zp_ref[...] * keep
        ubuf[0:HALO, :] = zp[:, :C_CONV] * _sigmoid(zp[:, C_CONV:2 * C_CONV])
        vbuf[0:HALO, :] = zp[:, 2 * C_CONV:]
        ubuf[HALO:, :] = z_ref[:, :C_CONV] * _sigmoid(z_ref[:, C_CONV:2 * C_CONV])
        vbuf[HALO:, :] = z_ref[:, 2 * C_CONV:]
        off = HALO - (CONV_K - 1)
        for r0 in range(0, tm, CONV_ROWS):
            for cols in CONV_COLS:
                u1_ref[r0:r0 + CONV_ROWS, cols] = cb_ref[:, cols] + _tap_sum(ubuf, cw_ref, r0 + off, cols, lambda o: o)
        u1 = u1_ref[...]
        mu = jnp.mean(u1, axis=-1, keepdims=True)
        uc = u1 - mu
        rstd = lax.rsqrt(jnp.mean(uc * uc, axis=-1, keepdims=True) + EPS)
        u2 = uc * rstd * lg_ref[...] + lb_ref[...]
        mix_ref[:, :C_CONV] = (u2 * _sigmoid(u2)).astype(bf16)
        for g, w in enumerate(POOL_WINDOWS):
            cols = slice(g * POOL_GROUP, (g + 1) * POOL_GROUP)
            acc = vbuf[pl.ds(HALO, tm), cols]
            vg = acc
            for d in range(1, w):
                acc = acc + vbuf[pl.ds(HALO - d, tm), cols]
            pooled = (acc / _pool_counts(tm, w, i * tm) - vg).astype(bf16)
            pooled_ref[:, cols] = pooled
            mixed = _dot(pooled, pw_ref[g].astype(bf16))
            mix_ref[:, C_CONV + g * POOL_GROUP:C_CONV + (g + 1) * POOL_GROUP] = (mixed * ps_ref[:, cols]).astype(bf16)
        x1 = x_ref[...] + _dot(mix_ref[...], wo_ref[...])
        x1_ref[...] = x1
        h2_ref[...] = _rms_fwd(x1, gf_ref[...])[0].astype(bf16)
        @pl.when(i == n - 1)
        def _():
            gather.forward([(k, 2) for k in range(nc)])
            gather.finish()

    res = pl.pallas_call(
        body, name="conv_pool_out", grid=(n,),
        in_specs=[_row(0, (tm, W_IN_COLS)),
                  pl.BlockSpec((HALO, W_IN_COLS), lambda i: (jnp.maximum(i * hb - 1, 0), 0)),
                  _row(0, (tm, D_MODEL)), _full((HALO, C_CONV)), _full((1, C_CONV)), _full((1, C_CONV)),
                  _full((1, C_CONV)), _full((4, POOL_GROUP, POOL_GROUP)), _full((1, C_POOL)),
                  _full((D_MODEL, D_MODEL)), _full((1, D_MODEL))] + cs["in_specs"],
        out_specs=[_row(0, (tm, D_MODEL)), _row(0, (tm, D_MODEL)), _row(0, (tm, C_CONV)), _row(0, (tm, C_POOL)),
                   _row(0, (tm, D_MODEL))] + cs["out_specs"],
        out_shape=[jax.ShapeDtypeStruct((s, D_MODEL), f32), jax.ShapeDtypeStruct((s, D_MODEL), bf16),
                   jax.ShapeDtypeStruct((s, C_CONV), f32), jax.ShapeDtypeStruct((s, C_POOL), bf16),
                   jax.ShapeDtypeStruct((s, D_MODEL), bf16)] + cs["out_shape"],
        input_output_aliases={11 + k: 5 + k for k in range(nc)},
        scratch_shapes=[pltpu.VMEM((HALO + tm, C_CONV), f32), pltpu.VMEM((HALO + tm, C_POOL), f32)] + cs["scratch"],
        compiler_params=_cp("arbitrary"),
    )(z, z, x, conv_w, conv_b, ln_g, ln_b, pool_w, pool_scale, w_out, g_ffn, *carry)
    return res[:5], res[5:]


def _ffn_up(h2, w_gu_g, carry):
    s = h2.shape[0]
    tm = min(ROW_TILE, s)
    n = s // tm
    nc = len(carry)
    cs = _carried(carry)

    def body(h2_ref, wg_ref, wu_ref, *refs):
        gu_ref, f_ref = refs[nc:nc + 2]
        gather = _Gather(refs[nc + 2:2 * nc + 2], *refs[2 * nc + 2:])
        i, c = pl.program_id(0), pl.program_id(1)
        pl.when(jnp.logical_and(i == 0, c == 0))(gather.start)
        direct = [(k, j) for k in range(nc) for j in range(2)]
        pl.when(jnp.logical_and(i == n // 2, c == 0))(functools.partial(gather.forward, direct))
        pl.when(jnp.logical_and(i == n - 1, c == 0))(functools.partial(gather.forward, [(k, 2) for k in range(nc)]))
        h = h2_ref[...]
        for lo, hi in zip(FF_SUB[:-1], FF_SUB[1:]):
            gate = _dot(h, wg_ref[0, :, lo:hi])
            up = _dot(h, wu_ref[0, :, lo:hi])
            gu_ref[0, :, lo:hi] = gate.astype(bf16)
            gu_ref[1, :, lo:hi] = up.astype(bf16)
            f_ref[:, lo:hi] = (gate * _sigmoid(gate) * up).astype(bf16)
        pl.when(jnp.logical_and(i == n - 1, c == 1))(gather.finish)

    res = pl.pallas_call(
        body, name="ffn_up", grid=(n, 2),
        in_specs=[_row(0, (tm, D_MODEL)),
                  pl.BlockSpec((1, D_MODEL, FF_CHUNK), lambda i, c: (c, 0, 0)),
                  pl.BlockSpec((1, D_MODEL, FF_CHUNK), lambda i, c: (2 + c, 0, 0))] + cs["in_specs"],
        out_specs=[pl.BlockSpec((2, tm, FF_CHUNK), lambda i, c: (0, i, c)),
                   pl.BlockSpec((tm, FF_CHUNK), lambda i, c: (i, c))] + cs["out_specs"],
        out_shape=[jax.ShapeDtypeStruct((2, s, D_FF), bf16), jax.ShapeDtypeStruct((s, D_FF), bf16)] + cs["out_shape"],
        input_output_aliases={3 + k: 2 + k for k in range(nc)},
        scratch_shapes=cs["scratch"],
        compiler_params=_cp("arbitrary", "arbitrary"),
    )(h2, w_gu_g, w_gu_g, *carry)
    return res[:2], res[2:]


def _ffn_down(x1, f, w_down):
    s = x1.shape[0]
    tm = min(ROW_TILE, s)

    def body(x1_ref, f_ref, wd_ref, x2_ref):
        x2_ref[...] = x1_ref[...] + _dot(f_ref[...], wd_ref[...])

    return pl.pallas_call(
        body, name="ffn_down", grid=(s // tm,),
        in_specs=[_row(0, (tm, D_MODEL)), _row(0, (tm, D_FF)), _full((D_FF, D_MODEL))],
        out_specs=_row(0, (tm, D_MODEL)), out_shape=jax.ShapeDtypeStruct((s, D_MODEL), f32),
        compiler_params=_cp("parallel"),
    )(x1, f, w_down)


def _ple_loss(x2, p, target, g_pg, g_post, g_final, w_pg, w_pu_g):
    s = x2.shape[0]
    tm = min(ROW_TILE, s)
    n = s // tm

    def body(x2_ref, p_ref, t_ref, gpg_ref, gpo_ref, gf_ref, wpg_ref, wpu_ref,
             dx2_ref, dwpg_ref, dwpu_ref, small_ref, apg_ref, apu_ref):
        i = pl.program_id(0)

        @pl.when(i == 0)
        def _():
            apg_ref[...] = jnp.zeros_like(apg_ref)
            apu_ref[...] = jnp.zeros_like(apu_ref)
            small_ref[...] = jnp.zeros_like(small_ref)

        x2 = x2_ref[...]
        h3, x2h, r2 = _rms_fwd(x2, gpg_ref[...])
        h3b = h3.astype(bf16)
        gate = _sigmoid(_dot(h3b, wpg_ref[...]))
        pb = p_ref[...].astype(bf16)
        pe = jnp.concatenate([_dot(pb, wpu_ref[j]) for j in range(N_CHIPS)], axis=-1)
        e, peh, rp = _rms_fwd(pe, gpo_ref[...])
        x3 = x2 + gate * e
        y, x3h, r3 = _rms_fwd(x3, gf_ref[...])
        d = y - t_ref[...]
        loss = 0.5 * jnp.sum(jnp.sum(d * d, axis=-1, keepdims=True) * (1.0 / D_MODEL), axis=0, keepdims=True)
        dx3, dgf = _rms_bwd(d * (1.0 / D_MODEL), x3h, r3, gf_ref[...])
        dpe, dgpo = _rms_bwd(dx3 * gate, peh, rp, gpo_ref[...])
        dgl = (dx3 * e * gate * (1.0 - gate)).astype(bf16)
        apg_ref[...] += _dot_tn(h3b, dgl)
        apu_ref[...] += _dot_tn(pb, dpe.astype(bf16))
        dh3 = _dot_nt(dgl, wpg_ref[...])
        dx2b, dgpg = _rms_bwd(dh3, x2h, r2, gpg_ref[...])
        dx2_ref[...] = dx3 + dx2b
        small_ref[0:1, :] += dgpg
        small_ref[1:2, :] += dgpo
        small_ref[2:3, :] += dgf
        small_ref[3:4, :] += jnp.broadcast_to(loss, (1, D_MODEL))

        @pl.when(i == n - 1)
        def _():
            dwpg_ref[...] = apg_ref[...].astype(bf16)
            for j in range(N_CHIPS):
                dwpu_ref[j] = apu_ref[:, j * PLE_CHUNK:(j + 1) * PLE_CHUNK].astype(bf16)

    return pl.pallas_call(
        body, name="ple_loss", grid=(n,),
        in_specs=[_row(0, (tm, D_MODEL)), _row(0, (tm, D_PLE)), _row(0, (tm, D_MODEL)),
                  _full((1, D_MODEL)), _full((1, D_MODEL)), _full((1, D_MODEL)),
                  _full((D_MODEL, D_MODEL)), _full((N_CHIPS, D_PLE, PLE_CHUNK))],
        out_specs=[_row(0, (tm, D_MODEL)), _full((D_MODEL, D_MODEL)), _full((N_CHIPS, D_PLE, PLE_CHUNK)),
                   _full((8, D_MODEL))],
        out_shape=[jax.ShapeDtypeStruct((s, D_MODEL), f32), jax.ShapeDtypeStruct((D_MODEL, D_MODEL), bf16),
                   jax.ShapeDtypeStruct((N_CHIPS, D_PLE, PLE_CHUNK), bf16), jax.ShapeDtypeStruct((8, D_MODEL), f32)],
        scratch_shapes=[pltpu.VMEM((D_MODEL, D_MODEL), f32), pltpu.VMEM((D_PLE, D_MODEL), f32)],
        compiler_params=_cp("arbitrary"),
    )(x2, p, target, g_pg, g_post, g_final, w_pg, w_pu_g)


def _crossed(parts):
    n = len(parts)
    return dict(in_specs=[ANY] * n, out_specs=[ANY] * n,
                out_shape=[jax.ShapeDtypeStruct(a.shape, a.dtype) for a in parts], scratch=_Cross.scratch(n))


def _ffn_bwd_dx(dx2, x1, gu, g_ffn, w_gu_g, w_down, grads):
    s = x1.shape[0]
    tm = min(ROW_TILE, s)
    n = s // tm
    nc = len(grads)
    cs = dict(in_specs=[ANY] * nc, out_specs=[ANY] * nc, scratch=_CrossAll.scratch(nc),
              out_shape=[jax.ShapeDtypeStruct((N_DEV, g.shape[1] // 2, g.shape[2]), g.dtype) for g in grads])

    def body(dx2_ref, x1_ref, gu_ref, g_ref, wg_ref, wu_ref, wd_ref, *refs):
        dx1_ref, dgu_ref, small_ref = refs[nc:nc + 3]
        acc_ref = refs[2 * nc + 3]
        cross = _CrossAll(refs[:nc], refs[nc + 3:2 * nc + 3], *refs[2 * nc + 4:])
        i, c = pl.program_id(0), pl.program_id(1)
        pl.when(jnp.logical_and(i == 0, c == 0))(cross.start)

        @pl.when(jnp.logical_and(i == 0, c == 0))
        def _():
            small_ref[...] = jnp.zeros_like(small_ref)

        @pl.when(c == 0)
        def _():
            acc_ref[...] = jnp.zeros_like(acc_ref)

        dyb = dx2_ref[...].astype(bf16)
        for lo, hi in zip(FF_SUB[:-1], FF_SUB[1:]):
            df = _dot_nt(dyb, wd_ref[lo:hi, :])
            gate = gu_ref[0, :, lo:hi].astype(f32)
            up = gu_ref[1, :, lo:hi].astype(f32)
            sg = _sigmoid(gate)
            dgate = (df * up * _silu_grad(gate, sg)).astype(bf16)
            dup = (df * gate * sg).astype(bf16)
            dgu_ref[0, :, lo:hi] = dgate
            dgu_ref[1, :, lo:hi] = dup
            acc_ref[...] += _dot_nt(dgate, wg_ref[0, :, lo:hi]) + _dot_nt(dup, wu_ref[0, :, lo:hi])

        @pl.when(c == 1)
        def _():
            _, x1h, r1 = _rms_fwd(x1_ref[...], g_ref[...])
            dx1b, dg = _rms_bwd(acc_ref[...], x1h, r1, g_ref[...])
            dx1_ref[...] = dx2_ref[...] + dx1b
            small_ref[0:1, :] += dg

        pl.when(jnp.logical_and(i == n - 1, c == 1))(cross.finish)

    res = pl.pallas_call(
        body, name="ffn_bwd_dx", grid=(n, 2),
        in_specs=[_row(0, (tm, D_MODEL)), _row(0, (tm, D_MODEL)),
                  pl.BlockSpec((2, tm, FF_CHUNK), lambda i, c: (0, i, c)), _full((1, D_MODEL)),
                  pl.BlockSpec((1, D_MODEL, FF_CHUNK), lambda i, c: (c, 0, 0)),
                  pl.BlockSpec((1, D_MODEL, FF_CHUNK), lambda i, c: (2 + c, 0, 0)),
                  pl.BlockSpec((FF_CHUNK, D_MODEL), lambda i, c: (c, 0))] + cs["in_specs"],
        out_specs=[_row(0, (tm, D_MODEL)), pl.BlockSpec((2, tm, FF_CHUNK), lambda i, c: (0, i, c)),
                   _full((8, D_MODEL))] + cs["out_specs"],
        out_shape=[jax.ShapeDtypeStruct((s, D_MODEL), f32), jax.ShapeDtypeStruct((2, s, D_FF), bf16),
                   jax.ShapeDtypeStruct((8, D_MODEL), f32)] + cs["out_shape"],
        scratch_shapes=[pltpu.VMEM((tm, D_MODEL), f32)] + cs["scratch"],
        compiler_params=_cp("arbitrary", "arbitrary"),
    )(dx2, x1, gu, g_ffn, w_gu_g, w_gu_g, w_down, *grads)
    return res[:3], res[3:]


def _ffn_bwd_dw_gu(h2, dgu):
    s = h2.shape[0]
    ts = min(2 * ROW_TILE, s)
    n = s // ts

    def body(h_ref, d_ref, o_ref, acc_ref):
        t = pl.program_id(1)

        @pl.when(t == 0)
        def _():
            acc_ref[...] = jnp.zeros_like(acc_ref)

        acc_ref[...] += _dot_tn(h_ref[...], d_ref[0])

        @pl.when(t == n - 1)
        def _():
            o_ref[0] = acc_ref[...].astype(bf16)

    return pl.pallas_call(
        body, name="ffn_bwd_dw_gu", grid=(N_CHIPS, n),
        in_specs=[pl.BlockSpec((ts, D_MODEL), lambda j, t: (t, 0)),
                  pl.BlockSpec((1, ts, FF_CHUNK), lambda j, t: (j // 2, t, j % 2))],
        out_specs=pl.BlockSpec((1, D_MODEL, FF_CHUNK), lambda j, t: (j, 0, 0)),
        out_shape=jax.ShapeDtypeStruct((N_CHIPS, D_MODEL, FF_CHUNK), bf16),
        scratch_shapes=[pltpu.VMEM((D_MODEL, FF_CHUNK), f32)],
        compiler_params=_cp("parallel", "arbitrary"),
    )(h2, dgu)


def _ffn_bwd_dw_down(f, dx2):
    s = dx2.shape[0]
    ts = min(2 * ROW_TILE, s)
    n = s // ts

    def body(f_ref, d_ref, o_ref, acc_ref):
        t = pl.program_id(1)

        @pl.when(t == 0)
        def _():
            acc_ref[...] = jnp.zeros_like(acc_ref)

        acc_ref[...] += _dot_tn(f_ref[...], d_ref[...].astype(bf16))

        @pl.when(t == n - 1)
        def _():
            o_ref[...] = acc_ref[...].astype(bf16)

    return pl.pallas_call(
        body, name="ffn_bwd_dw_down", grid=(2, n),
        in_specs=[pl.BlockSpec((ts, FF_CHUNK), lambda c, t: (t, c)),
                  pl.BlockSpec((ts, D_MODEL), lambda c, t: (t, 0))],
        out_specs=pl.BlockSpec((FF_CHUNK, D_MODEL), lambda c, t: (c, 0)),
        out_shape=jax.ShapeDtypeStruct((D_FF, D_MODEL), bf16),
        scratch_shapes=[pltpu.VMEM((FF_CHUNK, D_MODEL), f32)],
        compiler_params=_cp("parallel", "arbitrary"),
    )(f, dx2)


def _mix_bwd_local(dx1, mix, u1, pooled, w_out, ln_g, ln_b, pool_w, pool_scale):
    s = dx1.shape[0]
    tm = min(ROW_TILE, s)
    n = s // tm

    def body(dx1_ref, mix_ref, u1_ref, po_ref, wo_ref, lg_ref, lb_ref, pw_ref, ps_ref,
             du1_ref, dpo_ref, dwo_ref, dpw_ref, small_ref, awo_ref):
        i = pl.program_id(0)

        @pl.when(i == 0)
        def _():
            awo_ref[...] = jnp.zeros_like(awo_ref)
            dpw_ref[...] = jnp.zeros_like(dpw_ref)
            small_ref[...] = jnp.zeros_like(small_ref)

        dyb = dx1_ref[...].astype(bf16)
        dmix = _dot_nt(dyb, wo_ref[...])
        awo_ref[...] += _dot_tn(mix_ref[...], dyb)
        u1 = u1_ref[...]
        mu = jnp.mean(u1, axis=-1, keepdims=True)
        uc = u1 - mu
        rstd = lax.rsqrt(jnp.mean(uc * uc, axis=-1, keepdims=True) + EPS)
        uh = uc * rstd
        u2 = uh * lg_ref[...] + lb_ref[...]
        du2 = dmix[:, :C_CONV] * _silu_grad(u2, _sigmoid(u2))
        duh = du2 * lg_ref[...]
        du1 = rstd * (duh - jnp.mean(duh, axis=-1, keepdims=True) - uh * jnp.mean(duh * uh, axis=-1, keepdims=True))
        du1_ref[...] = du1
        small_ref[0:1, :] += jnp.sum(du2 * uh, axis=0, keepdims=True)
        small_ref[1:2, :] += jnp.sum(du2, axis=0, keepdims=True)
        small_ref[2:3, :] += jnp.sum(du1, axis=0, keepdims=True)
        for g in range(len(POOL_WINDOWS)):
            cols = slice(g * POOL_GROUP, (g + 1) * POOL_GROUP)
            dq = dmix[:, C_CONV + g * POOL_GROUP:C_CONV + (g + 1) * POOL_GROUP]
            pwb = pw_ref[g].astype(bf16)
            pg = po_ref[:, cols]
            mixed = _dot(pg, pwb)
            small_ref[3:4, cols] += jnp.sum(dq * mixed, axis=0, keepdims=True)
            dmixed = (dq * ps_ref[:, cols]).astype(bf16)
            dpw_ref[g] += _dot_tn(pg, dmixed)
            dpo_ref[:, cols] = _dot_nt(dmixed, pwb)

        @pl.when(i == n - 1)
        def _():
            dwo_ref[...] = awo_ref[...].astype(bf16)

    return pl.pallas_call(
        body, name="mix_bwd_local", grid=(n,),
        in_specs=[_row(0, (tm, D_MODEL)), _row(0, (tm, D_MODEL)), _row(0, (tm, C_CONV)), _row(0, (tm, C_POOL)),
                  _full((D_MODEL, D_MODEL)), _full((1, C_CONV)), _full((1, C_CONV)),
                  _full((4, POOL_GROUP, POOL_GROUP)), _full((1, C_POOL))],
        out_specs=[_row(0, (tm, C_CONV)), _row(0, (tm, C_POOL)), _full((D_MODEL, D_MODEL)),
                   _full((4, POOL_GROUP, POOL_GROUP)), _full((8, C_CONV))],
        out_shape=[jax.ShapeDtypeStruct((s, C_CONV), f32), jax.ShapeDtypeStruct((s, C_POOL), f32),
                   jax.ShapeDtypeStruct((D_MODEL, D_MODEL), bf16),
                   jax.ShapeDtypeStruct((4, POOL_GROUP, POOL_GROUP), f32), jax.ShapeDtypeStruct((8, C_CONV), f32)],
        scratch_shapes=[pltpu.VMEM((D_MODEL, D_MODEL), f32)],
        compiler_params=_cp("arbitrary"),
    )(dx1, mix, u1, pooled, w_out, ln_g, ln_b, pool_w, pool_scale)


def _in_bwd(du1, dpo, z, x, dx1, conv_w, g_mix, w_in_g, parts, small_parts):
    s = x.shape[0]
    tm = min(ROW_TILE, s)
    n = s // tm
    hb = tm // HALO
    last = s // HALO - 1
    nv = tm // 8
    assert nv >= SHIFT_PAD and nv % SHIFT_GROUPS == 0
    nc, ns = len(parts), len(small_parts)
    cs = _crossed(parts)

    def body(du_ref, dun_ref, dp_ref, dpn_ref, z_ref, zp_ref, x_ref, dx1_ref, cw_ref, g_ref, w_ref, *refs):
        outs = refs[nc + ns:]
        gx_ref, dw_ref, dcw_ref, small_ref = outs[:4]
        eu, ed, ep, ss, u0_ref, dz_ref, acc_ref, dcw_acc = outs[4 + nc + ns:12 + nc + ns]
        sems = outs[12 + nc + ns:]
        cross = _Cross(refs[:nc], outs[4:4 + nc], sems[0], sems[1])
        gather = _SmallGather(refs[nc:nc + ns], outs[4 + nc:4 + nc + ns], sems[2], sems[3])
        i = pl.program_id(0)
        pl.when(i == 0)(cross.start)
        pl.when(i == 0)(gather.start)

        @pl.when(i == 0)
        def _():
            acc_ref[...] = jnp.zeros_like(acc_ref)
            dcw_acc[...] = jnp.zeros_like(dcw_acc)
            small_ref[...] = jnp.zeros_like(small_ref)

        keep_prev = (i > 0).astype(f32)
        keep_next = (i < n - 1).astype(f32)
        zp = zp_ref[...] * keep_prev
        u0_prev = zp[:, :C_CONV] * _sigmoid(zp[:, C_CONV:2 * C_CONV])
        u0_ref[...] = z_ref[:, :C_CONV] * _sigmoid(z_ref[:, C_CONV:2 * C_CONV])
        du_next = dun_ref[...] * keep_next
        for c, w_pool in enumerate(POOL_WINDOWS):
            lanes = slice(c * 128, (c + 1) * 128)
            eu[c, pl.ds(0, SHIFT_PAD, stride=8), :] = u0_prev[:, lanes]
            ed[c, pl.ds(nv * 8 + 7, SHIFT_PAD, stride=8), :] = du_next[:, lanes]
            ep[c, pl.ds(nv * 8 + 7, SHIFT_PAD, stride=8), :] = (
                dpn_ref[:, lanes] * keep_next / _pool_counts(HALO, w_pool, (i + 1) * tm))
            for j in range(8):
                rows = slice(j * nv, (j + 1) * nv)
                eu[c, pl.ds(SHIFT_PAD * 8 + j, nv, stride=8), :] = u0_ref[rows, lanes]
                ed[c, pl.ds(j, nv, stride=8), :] = du_ref[rows, lanes]
                ep[c, pl.ds(j, nv, stride=8), :] = dp_ref[rows, lanes] / _pool_counts(nv, w_pool, i * tm + j * nv)
                if j >= 1:
                    eu[c, pl.ds(j, SHIFT_PAD, stride=8), :] = u0_ref[j * nv - SHIFT_PAD:j * nv, lanes]
                if j <= 6:
                    edge = slice((j + 1) * nv, (j + 1) * nv + SHIFT_PAD)
                    ed[c, pl.ds(nv * 8 + j, SHIFT_PAD, stride=8), :] = du_ref[edge, lanes]
                    ep[c, pl.ds(nv * 8 + j, SHIFT_PAD, stride=8), :] = (
                        dp_ref[edge, lanes] / _pool_counts(SHIFT_PAD, w_pool, i * tm + (j + 1) * nv))
        for c, w_pool in enumerate(POOL_WINDOWS):
            lanes = slice(c * 128, (c + 1) * 128)
            b_lanes = slice(C_CONV + c * 128, C_CONV + (c + 1) * 128)
            v_lanes = slice(2 * C_CONV + c * 128, 2 * C_CONV + (c + 1) * 128)
            for v0 in range(0, nv, SHIFT_GROUPS):
                span = SHIFT_GROUPS * 8
                acc = jnp.zeros((span, 128), f32)
                for k in range(CONV_K):
                    acc = acc + cw_ref[pl.ds(k, 1), lanes] * ed[c, pl.ds((v0 + CONV_K - 1 - k) * 8, span), :]
                ss[0, v0 * 8:v0 * 8 + span, :] = acc
                acc = ep[c, pl.ds(v0 * 8, span), :]
                for d in range(1, w_pool):
                    acc = acc + ep[c, pl.ds((v0 + d) * 8, span), :]
                ss[1, v0 * 8:v0 * 8 + span, :] = acc
                d1 = ed[c, pl.ds(v0 * 8, span), :]
                for k in range(CONV_K):
                    prod = d1 * eu[c, pl.ds((SHIFT_PAD - (CONV_K - 1) + v0 + k) * 8, span), :]
                    fold = prod[0:8]
                    for r in range(8, span, 8):
                        fold = fold + prod[r:r + 8]
                    dcw_acc[k, :, lanes] += fold
            for j in range(8):
                rows = slice(j * nv, (j + 1) * nv)
                du0 = ss[0, pl.ds(j, nv, stride=8), :]
                av, sv = z_ref[rows, lanes], _sigmoid(z_ref[rows, b_lanes])
                dz_ref[rows, lanes] = (du0 * sv).astype(bf16)
                dz_ref[rows, b_lanes] = (du0 * av * sv * (1.0 - sv)).astype(bf16)
                dz_ref[rows, v_lanes] = (ss[1, pl.ds(j, nv, stride=8), :] - dp_ref[rows, lanes]).astype(bf16)
        h, xh, r = _rms_fwd(x_ref[...], g_ref[...])
        dz = dz_ref[...]
        acc_ref[...] += _dot_tn(h.astype(bf16), dz)
        dh = _dot_nt(dz[:, 0:W_IN_CHUNK], w_ref[0])
        for j in range(1, N_CHIPS):
            dh = dh + _dot_nt(dz[:, j * W_IN_CHUNK:(j + 1) * W_IN_CHUNK], w_ref[j])
        dxb, dg = _rms_bwd(dh, xh, r, g_ref[...])
        gx_ref[...] = dx1_ref[...] + dxb
        small_ref[0:1, :] += dg

        @pl.when(i == n - 1)
        def _():
            for j in range(N_CHIPS):
                dw_ref[j] = acc_ref[:, j * W_IN_CHUNK:(j + 1) * W_IN_CHUNK].astype(bf16)
            dcw_ref[...] = jnp.sum(dcw_acc[...], axis=1)

        pl.when(i == n - 1)(gather.finish)
        pl.when(i == n - 1)(cross.finish)

    nxt = lambda i: (jnp.minimum((i + 1) * hb, last), 0)
    res = pl.pallas_call(
        body, name="in_bwd", grid=(n,),
        in_specs=[_row(0, (tm, C_CONV)), pl.BlockSpec((HALO, C_CONV), nxt),
                  _row(0, (tm, C_POOL)), pl.BlockSpec((HALO, C_POOL), nxt),
                  _row(0, (tm, W_IN_COLS)),
                  pl.BlockSpec((HALO, W_IN_COLS), lambda i: (jnp.maximum(i * hb - 1, 0), 0)),
                  _row(0, (tm, D_MODEL)), _row(0, (tm, D_MODEL)), _full((HALO, C_CONV)), _full((1, D_MODEL)),
                  _full((N_CHIPS, D_MODEL, W_IN_CHUNK))] + cs["in_specs"] + [ANY] * ns,
        out_specs=[_row(0, (tm, D_MODEL)), _full((N_CHIPS, D_MODEL, W_IN_CHUNK)), _full((HALO, C_CONV)),
                   _full((8, D_MODEL))] + cs["out_specs"] + [ANY] * ns,
        out_shape=[jax.ShapeDtypeStruct((s, D_MODEL), f32), jax.ShapeDtypeStruct((N_CHIPS, D_MODEL, W_IN_CHUNK), bf16),
                   jax.ShapeDtypeStruct((HALO, C_CONV), f32), jax.ShapeDtypeStruct((8, D_MODEL), f32)] + cs["out_shape"]
        + _gathered_shapes(small_parts),
        scratch_shapes=[pltpu.VMEM((4, (SHIFT_PAD + nv) * 8, 128), f32), pltpu.VMEM((4, (nv + SHIFT_PAD) * 8, 128), f32),
                        pltpu.VMEM((4, (nv + SHIFT_PAD) * 8, 128), f32), pltpu.VMEM((2, tm, 128), f32),
                        pltpu.VMEM((tm, C_CONV), f32), pltpu.VMEM((tm, W_IN_COLS), bf16),
                        pltpu.VMEM((D_MODEL, W_IN_COLS), f32), pltpu.VMEM((HALO, 8, C_CONV), f32)] + cs["scratch"]
        + _SmallGather.scratch(ns),
        compiler_params=_cp("arbitrary"),
    )(du1, du1, dpo, dpo, z, z, x, dx1, conv_w, g_mix, w_in_g, *parts, *small_parts)
    return res[:4], res[4:4 + nc], res[4 + nc:]


def _pair_reduce(name, grads):
    nk = len(grads)
    halves = [g.shape[1] // 2 for g in grads]

    def body(*refs):
        ins, outs, got = refs[:nk], refs[nk:2 * nk], refs[2 * nk:3 * nk]
        send_sems, recv_sems = refs[3 * nk:]
        x, y, c, _, _ = _place()

        def half(k, core):
            return pl.ds(pl.multiple_of(core * halves[k], 16), halves[k])

        cps = [_remote(ins[k].at[:, half(k, 1 - c), :], got[k], send_sems.at[k], recv_sems.at[k], (x, y, 1 - c))
               for k in range(nk)]
        for cp in cps:
            cp.start()
        for k, cp in enumerate(cps):
            cp.wait_recv()
            outs[k][...] = (ins[k][:, half(k, c), :].astype(f32) + got[k][...].astype(f32)).astype(bf16)
        for cp in cps:
            cp.wait_send()

    shapes = [(N_CHIPS, h, g.shape[2]) for g, h in zip(grads, halves)]
    return pl.pallas_call(
        body, name=f"pair_reduce_{name}", in_specs=[VMEM] * nk, out_specs=[VMEM] * nk,
        out_shape=[jax.ShapeDtypeStruct(s, bf16) for s in shapes],
        scratch_shapes=[pltpu.VMEM(s, bf16) for s in shapes]
        + [pltpu.SemaphoreType.DMA((nk,)), pltpu.SemaphoreType.DMA((nk,))],
        compiler_params=pltpu.CompilerParams(vmem_limit_bytes=VMEM_LIMIT),
    )(*grads)


class _Cross:
    def __init__(self, parts, landed, send_sems, recv_sems):
        self.parts, self.landed, self.send_sems, self.recv_sems = parts, landed, send_sems, recv_sems
        _, _, self.c, self.me, self.others = _place()

    def _copy(self, k, j, src_chunk, dst_slot):
        ox, oy = self.others[j]
        return _remote(self.parts[k].at[src_chunk], self.landed[k].at[dst_slot], self.send_sems.at[3 * k + j],
                       self.recv_sems.at[3 * k + j], (ox, oy, self.c))

    def _each(self):
        return [(k, j, 2 * self.others[j][0] + self.others[j][1]) for k in range(len(self.parts)) for j in range(3)]

    def start(self):
        for k, j, chip in self._each():
            self._copy(k, j, chip, self.me).start()

    def finish(self):
        for k, j, chip in self._each():
            self._copy(k, j, chip, chip).wait_recv()
        for k, j, chip in self._each():
            self._copy(k, j, chip, self.me).wait_send()

    @staticmethod
    def scratch(n):
        return [pltpu.SemaphoreType.DMA((3 * n,)), pltpu.SemaphoreType.DMA((3 * n,))]


class _CrossAll:
    def __init__(self, grads, landed, send_sems, recv_sems):
        self.grads, self.landed, self.send_sems, self.recv_sems = grads, landed, send_sems, recv_sems
        self.x, self.y, self.c, self.me, self.others = _place()
        self.dev = 2 * self.me + self.c

    def _piece(self, k, chip, half):
        rows = self.grads[k].shape[1] // 2
        return self.grads[k].at[chip, pl.ds(half * rows, rows), :]

    def _to_sibling(self, k):
        return _remote(self._piece(k, self.me, 1 - self.c), self.landed[k].at[self.dev], self.send_sems.at[7 * k],
                       self.recv_sems.at[7 * k], (self.x, self.y, 1 - self.c))

    def _to_chip(self, k, j, half):
        ox, oy = self.others[j]
        return _remote(self._piece(k, 2 * ox + oy, half), self.landed[k].at[self.dev],
                       self.send_sems.at[7 * k + 1 + 2 * j + half], self.recv_sems.at[7 * k + 1 + 2 * j + self.c],
                       (ox, oy, half))

    def _from(self, k, sem, slot):
        return _remote(self.landed[k].at[slot], self.landed[k].at[slot], self.send_sems.at[7 * k + sem],
                       self.recv_sems.at[7 * k + sem], (self.x, self.y, 1 - self.c))

    def start(self):
        for k in range(len(self.grads)):
            self._to_sibling(k).start()
            for j in range(3):
                for half in range(2):
                    self._to_chip(k, j, half).start()

    def finish(self):
        for k in range(len(self.grads)):
            self._from(k, 0, 2 * self.me + 1 - self.c).wait_recv()
            for j, (ox, oy) in enumerate(self.others):
                for core in range(2):
                    self._from(k, 1 + 2 * j + core, 4 * ox + 2 * oy + core).wait_recv()
        for k in range(len(self.grads)):
            self._to_sibling(k).wait_send()
            for j in range(3):
                for half in range(2):
                    self._to_chip(k, j, half).wait_send()

    @staticmethod
    def scratch(n):
        return [pltpu.SemaphoreType.DMA((7 * n,)), pltpu.SemaphoreType.DMA((7 * n,))]


def _rs_sum_chips(name, place, landed, part):
    _, half, cols = landed.shape

    def body(place_ref, l_ref, p_ref, o_ref):
        me = place_ref[0]
        own = p_ref[0].astype(f32)
        acc = jnp.where(me == 0, own, l_ref[0].astype(f32))
        for j in range(1, N_CHIPS):
            acc = acc + jnp.where(me == j, own, l_ref[j].astype(f32))
        o_ref[...] = acc

    return pl.pallas_call(
        body, name=f"rs_sum_chips_{name}",
        grid_spec=pltpu.PrefetchScalarGridSpec(
            num_scalar_prefetch=1, grid=(1,),
            in_specs=[pl.BlockSpec((N_CHIPS, half, cols), lambda t, place_ref: (0, 0, 0)),
                      pl.BlockSpec((1, half, cols), lambda t, place_ref: (place_ref[0], 0, 0))],
            out_specs=pl.BlockSpec((half, cols), lambda t, place_ref: (place_ref[1], 0))),
        out_shape=jax.ShapeDtypeStruct((2 * half, cols), f32),
        compiler_params=_cp("arbitrary"),
    )(place, landed, part)


def _grad_pair(shards):
    nk = len(shards)

    def body(*refs):
        outs = refs[nk:2 * nk]
        send_sems, recv_sems = refs[2 * nk:]
        x, y, c, _, _ = _place()

        def half(k, core):
            h = outs[k].shape[0] // 2
            return outs[k].at[pl.ds(core * h, h), :]

        cps = [_remote(half(k, c), half(k, c), send_sems.at[k], recv_sems.at[k], (x, y, 1 - c)) for k in range(nk)]
        for cp in cps:
            cp.start()
        for k in range(nk):
            _remote(half(k, 1 - c), half(k, 1 - c), send_sems.at[k], recv_sems.at[k], (x, y, 1 - c)).wait_recv()
        for cp in cps:
            cp.wait_send()

    return pl.pallas_call(
        body, name="grad_pair", in_specs=[ANY] * nk, out_specs=[ANY] * nk,
        out_shape=[jax.ShapeDtypeStruct(a.shape, f32) for a in shards],
        input_output_aliases={k: k for k in range(nk)},
        scratch_shapes=[pltpu.SemaphoreType.DMA((nk,)), pltpu.SemaphoreType.DMA((nk,))],
    )(*shards)


def _adam_math(w, g, m, v):
    m = ADAM_B1 * m + (1.0 - ADAM_B1) * g
    v = ADAM_B2 * v + (1.0 - ADAM_B2) * (g * g)
    m_hat = m / (1.0 - ADAM_B1 ** ADAM_STEP)
    v_hat = v / (1.0 - ADAM_B2 ** ADAM_STEP)
    delta = -ADAM_LR * (m_hat / (jnp.sqrt(v_hat) + ADAM_EPS) + ADAM_WD * w)
    return delta, m, v


def _adam(ws, gs, ms, vs):
    n = len(ws)
    steps = 2 * SHARD_STEPS

    def body(*refs):
        for k in range(n):
            w_ref, g_ref, m_ref, v_ref = (refs[j * n + k] for j in range(4))
            go_ref, d_ref, nm_ref, nv_ref = (refs[(4 + j) * n + k] for j in range(4))
            g = g_ref[...]
            go_ref[...] = g
            d_ref[...], nm_ref[...], nv_ref[...] = _adam_math(w_ref[...], g, m_ref[...], v_ref[...])

    specs = [_row(0, (w.shape[0] // steps, w.shape[1])) for w in ws]
    res = pl.pallas_call(
        body, name="adam_shards", grid=(steps,), in_specs=specs * 4, out_specs=specs * 4,
        out_shape=[jax.ShapeDtypeStruct(w.shape, f32) for w in ws] * 4,
        compiler_params=_cp("parallel"),
    )(*ws, *gs, *ms, *vs)
    return [tuple(res[j * n + k] for j in range(4)) for k in range(n)]


class _SmallGather:
    def __init__(self, parts, bufs, send_sems, recv_sems):
        self.parts, self.bufs, self.send_sems, self.recv_sems = parts, bufs, send_sems, recv_sems
        self.x, self.y, self.c, _, self.others = _place()
        self.sibling = (self.x, self.y, 1 - self.c)

    def _copy(self, a, k, block, to, src=None):
        slot = self.bufs[a].at[4 * block[0] + 2 * block[1] + block[2]]
        return _remote(slot if src is None else src, slot, self.send_sems.at[7 * a + k], self.recv_sems.at[7 * a + k],
                       to)

    def _first(self, a):
        me = (self.x, self.y, self.c)
        return [self._copy(a, 0, me, self.sibling, src=self.parts[a])] + [
            self._copy(a, 1 + j, me, (*chip, self.c), src=self.parts[a]) for j, chip in enumerate(self.others)]

    def _passed(self, a):
        return [self._copy(a, 4 + j, (*chip, self.c), self.sibling) for j, chip in enumerate(self.others)]

    @staticmethod
    def scratch(n):
        return [pltpu.SemaphoreType.DMA((7 * n,)), pltpu.SemaphoreType.DMA((7 * n,))]

    def start(self):
        for a in range(len(self.parts)):
            for cp in self._first(a):
                cp.start()

    def finish(self):
        sent = []
        for a in range(len(self.parts)):
            passed = self._passed(a)
            for j, chip in enumerate(self.others):
                self._copy(a, 1 + j, (*chip, self.c), self.sibling).wait_recv()
                passed[j].start()
            sent += self._first(a) + passed
        for a in range(len(self.parts)):
            self._copy(a, 0, self.sibling, self.sibling).wait_recv()
            for j, chip in enumerate(self.others):
                self._copy(a, 4 + j, (*chip, 1 - self.c), self.sibling).wait_recv()
        for cp in sent:
            cp.wait_send()


def _gathered_shapes(parts):
    return [jax.ShapeDtypeStruct((N_DEV,) + p.shape, p.dtype) for p in parts]


def _rs_sum_group(name, place, landed, parts, landed_all, grads_all, cross_parts, small_parts):
    nk, na, nx, ns = len(landed), len(landed_all), len(cross_parts), len(small_parts)
    dims = [a.shape[1:] for a in landed]
    dims_all = [a.shape[1:] for a in landed_all]

    def body(place_ref, *refs):
        take = iter(refs)
        l_refs, p_refs, la_refs, ga_refs, x_refs, sp_refs, o_refs, oa_refs, xl_refs, sbufs = (
            [next(take) for _ in range(cnt)] for cnt in (nk, nk, na, na, nx, ns, nk, na, nx, ns))
        sems = list(take)
        cross = _Cross(x_refs, xl_refs, sems[0], sems[1])
        small = _SmallGather(sp_refs, sbufs, sems[2], sems[3])
        t = pl.program_id(0)

        @pl.when(t == 0)
        def _():
            cross.start()
            small.start()

        me = place_ref[0]
        for l_ref, p_ref, o_ref in zip(l_refs, p_refs, o_refs):
            own = p_ref[0].astype(f32)
            acc = jnp.where(me == 0, own, l_ref[0].astype(f32))
            for j in range(1, N_CHIPS):
                acc = acc + jnp.where(me == j, own, l_ref[j].astype(f32))
            o_ref[...] = acc
        dev = 2 * me + place_ref[1]
        for l_ref, g_ref, o_ref in zip(la_refs, ga_refs, oa_refs):
            own = g_ref[0].astype(f32)
            acc = jnp.where(dev == 0, own, l_ref[0].astype(f32))
            for d in range(1, N_DEV):
                acc = acc + jnp.where(dev == d, own, l_ref[d].astype(f32))
            o_ref[...] = acc

        @pl.when(t == 1)
        def _():
            small.finish()
            cross.finish()

    def halves(h, c, lead, index):
        return pl.BlockSpec((lead, h // 2, c) if lead else (h // 2, c), index)

    in_specs = [halves(h, c, N_CHIPS, lambda t, pr: (0, t, 0)) for h, c in dims]
    in_specs += [halves(h, c, 1, lambda t, pr: (pr[0], t, 0)) for h, c in dims]
    in_specs += [halves(h, c, N_DEV, lambda t, pr: (0, t, 0)) for h, c in dims_all]
    in_specs += [halves(h, c, 1, lambda t, pr: (pr[0], 2 * pr[1] + t, 0)) for h, c in dims_all]
    in_specs += [ANY] * (nx + ns)
    out_specs = [halves(h, c, 0, lambda t, pr: (2 * pr[1] + t, 0)) for h, c in dims + dims_all] + [ANY] * (nx + ns)
    out_shape = [jax.ShapeDtypeStruct((2 * h, c), f32) for h, c in dims + dims_all]
    out_shape += [jax.ShapeDtypeStruct(a.shape, a.dtype) for a in cross_parts] + _gathered_shapes(small_parts)
    res = pl.pallas_call(
        body, name=f"rs_sum_{name}",
        grid_spec=pltpu.PrefetchScalarGridSpec(
            num_scalar_prefetch=1, grid=(2,), in_specs=in_specs, out_specs=out_specs,
            scratch_shapes=_Cross.scratch(nx) + _SmallGather.scratch(ns)),
        out_shape=out_shape, compiler_params=_cp("arbitrary"),
    )(place, *landed, *parts, *landed_all, *grads_all, *cross_parts, *small_parts)
    return res[:nk], res[nk:nk + na], res[nk + na:nk + na + nx], res[nk + na + nx:]


SMALL_PARAMS = ("g_ple_gate", "g_ple_post", "g_final", "g_ffn", "ln_g", "ln_b", "conv_b", "pool_scale", "pool_w",
                "conv_w", "g_mix")
SMALL_ROWS = {"g_ple_gate": (0, 0), "g_ple_post": (0, 1), "g_final": (0, 2), "g_ffn": (1, 0), "ln_g": (2, 0),
              "ln_b": (2, 1), "conv_b": (2, 2), "pool_scale": (2, 3), "g_mix": (5, 0)}
LOSS_ROW = (0, 3)


def _small_adam(place, gathered, parts, params):
    nb, names = len(parts), SMALL_PARAMS
    flat = [a for nm in names for a in params[nm]]

    def body(place_ref, *refs):
        b_refs, p_refs = refs[:nb], refs[nb:2 * nb]
        w_refs = refs[2 * nb:2 * nb + 3 * len(names)]
        outs = refs[2 * nb + 3 * len(names):]
        loss_ref, o_refs, cw_sum = outs[0], outs[1:1 + 4 * len(names)], outs[1 + 4 * len(names)]
        chip = place_ref[0]
        me = 2 * chip + place_ref[1]

        def total(blk, idx):
            own = p_refs[blk][idx]
            g = jnp.where(me == 0, own, b_refs[blk][(0,) + idx])
            for d in range(1, N_DEV):
                g = g + jnp.where(me == d, own, b_refs[blk][(d,) + idx])
            return g

        everything = (slice(None), slice(None))
        loss_ref[...] = total(LOSS_ROW[0], (pl.ds(LOSS_ROW[1], 1), pl.ds(0, 128)))
        d_cw = total(4, everything)
        mine = jnp.where(chip == 0, d_cw[:, 0:128], 0.0)
        for j in range(1, N_CHIPS):
            mine = mine + jnp.where(chip == j, d_cw[:, j * 128:(j + 1) * 128], 0.0)
        cw_sum[...] = mine
        for k, nm in enumerate(names):
            w_ref, m_ref, v_ref = w_refs[3 * k:3 * k + 3]
            g_ref, d_ref, nm_ref, nv_ref = o_refs[4 * k:4 * k + 4]
            if nm == "pool_w":
                g = total(3, everything + (slice(None),))
            elif nm == "conv_w":
                g = cw_sum[pl.ds(0, CONV_K), :]
            else:
                blk, row = SMALL_ROWS[nm]
                g = total(blk, (pl.ds(row, 1), slice(None)))
            g_ref[...] = g
            d_ref[...], nm_ref[...], nv_ref[...] = _adam_math(w_ref[...], g, m_ref[...], v_ref[...])

    whole = lambda a: pl.BlockSpec(a.shape, lambda t, pr: (0,) * a.ndim)
    out_shape = [jax.ShapeDtypeStruct((1, 128), f32)]
    out_shape += [jax.ShapeDtypeStruct(params[nm][0].shape, f32) for nm in names for _ in range(4)]
    res = pl.pallas_call(
        body, name="small_adam",
        grid_spec=pltpu.PrefetchScalarGridSpec(
            num_scalar_prefetch=1, grid=(1,),
            in_specs=[whole(a) for a in list(gathered) + list(parts) + flat],
            out_specs=[whole(s) for s in out_shape], scratch_shapes=[pltpu.VMEM((HALO, 128), f32)]),
        out_shape=out_shape, compiler_params=_cp("arbitrary"),
    )(place, *gathered, *parts, *flat)
    return res[0], {nm: res[1 + 4 * k:5 + 4 * k] for k, nm in enumerate(names)}


def _pad_rows(a, rows):
    return jnp.concatenate([a, jnp.zeros((rows - a.shape[0],) + a.shape[1:], a.dtype)], axis=0)


def kernel(x, p, g_mix, w_in, conv_w, conv_b, ln_g, ln_b, pool_w, pool_scale, w_out, g_ffn, w_gate_up, w_down, g_ple_gate, w_ple_gate, w_ple_up, g_ple_post, g_final, loss_target, m_g_mix, m_w_in, m_conv_w, m_conv_b, m_ln_g, m_ln_b, m_pool_w, m_pool_scale, m_w_out, m_g_ffn, m_w_gate_up, m_w_down, m_g_ple_gate, m_w_ple_gate, m_w_ple_up, m_g_ple_post, m_g_final, v_g_mix, v_w_in, v_conv_w, v_conv_b, v_ln_g, v_ln_b, v_pool_w, v_pool_scale, v_w_out, v_g_ffn, v_w_gate_up, v_w_down, v_g_ple_gate, v_w_ple_gate, v_w_ple_up, v_g_ple_post, v_g_final):
    seq = x.shape[1]
    me = 2 * lax.axis_index("x") + lax.axis_index("y")
    chip = me.astype(jnp.int32).reshape(1)
    core = lax.axis_index("c").astype(jnp.int32).reshape(1)
    place = jnp.concatenate([chip, core])
    xs, ps, ts = x.reshape(seq, D_MODEL), p.reshape(seq, D_PLE), loss_target.reshape(seq, D_MODEL)

    big = [w_in[0], w_gate_up[0], w_out[0], w_down[0], w_ple_gate[0], w_ple_up[0]]
    big_m = [m_w_in[0], m_w_gate_up[0], m_w_out[0], m_w_down[0], m_w_ple_gate[0], m_w_ple_up[0]]
    big_v = [v_w_in[0], v_w_gate_up[0], v_w_out[0], v_w_down[0], v_w_ple_gate[0], v_w_ple_up[0]]
    b_in, b_gu, b_out, b_down, b_pg, b_pu, b_cw = _cast_into_slots(
        chip, big + [_pad_rows(conv_w[0], HALO)], [bf16] * len(big) + [f32])
    xi, yi = lax.axis_index("x"), lax.axis_index("y")
    order = jnp.stack([me, 2 * (1 - xi) + yi, 2 * xi + 1 - yi, 2 * (1 - xi) + 1 - yi]).astype(jnp.int32)

    z, (w_in_g, cw_g, w_out_g) = _mix_in(xs, g_mix, order, [b_in, b_cw, b_out])
    conv_w_f = cw_g.transpose(1, 0, 2).reshape(HALO, C_CONV)
    w_out_f = w_out_g.reshape(D_MODEL, D_MODEL)
    (x1, mix, u1, pooled, h2), (w_gu_g,) = _conv_pool_out(z, xs, conv_w_f, conv_b, ln_g, ln_b, pool_w[0], pool_scale,
                                                          w_out_f, g_ffn, [b_gu])
    (gu, ffn_f), (w_down_g, w_pg_g, w_pu_g) = _ffn_up(h2, w_gu_g, [b_down, b_pg, b_pu])
    w_down_f = w_down_g.reshape(D_FF, D_MODEL)
    x2 = _ffn_down(x1, ffn_f, w_down_f)
    w_pg_f = w_pg_g.reshape(D_MODEL, D_MODEL)
    dx2, d_w_pg, d_w_pu, small_ple = _ple_loss(x2, ps, ts, g_ple_gate, g_ple_post, g_final.reshape(1, D_MODEL),
                                               w_pg_f, w_pu_g)
    d_w_down = _ffn_bwd_dw_down(ffn_f, dx2)
    grads_a = [d_w_down.reshape(N_CHIPS, -1, D_MODEL), d_w_pg.reshape(N_CHIPS, -1, D_MODEL), d_w_pu]
    (dx1, dgu, small_ffn), landed_a = _ffn_bwd_dx(dx2, x1, gu, g_ffn, w_gu_g, w_down_f, grads_a)
    d_w_gu = _ffn_bwd_dw_gu(h2, dgu)
    du1, dpo, d_w_out, d_pool_w, small_mix = _mix_bwd_local(dx1, mix, u1, pooled, w_out_f, ln_g, ln_b, pool_w[0],
                                                             pool_scale)
    parts_b = _pair_reduce("b", [d_w_gu, d_w_out.reshape(N_CHIPS, -1, D_MODEL)])
    small_0 = [small_ple, small_ffn, small_mix, d_pool_w]
    (grad_x, d_w_in, d_conv_w, small_in), landed_b, small_all_0 = _in_bwd(du1, dpo, z, xs, dx1, conv_w_f, g_mix,
                                                                            w_in_g, parts_b, small_0)
    parts_c = _pair_reduce("c", [d_w_in])
    small_1 = [d_conv_w, small_in]
    (h_gu, h_out), (h_down, h_pg, h_pu), landed_c, small_all_1 = _rs_sum_group(
        "ab", place, landed_b, parts_b, landed_a, grads_a, parts_c, small_1)
    h_in = _rs_sum_chips("w_in", place, landed_c[0], parts_c[0])
    big_g = _grad_pair([h_in, h_gu, h_out, h_down, h_pg, h_pu])
    big_upd = _adam(big, big_g, big_m, big_v)

    p3 = lambda w, m, v: (w, m, v)
    row = lambda a: a.reshape(1, D_MODEL)
    params = dict(
        g_ple_gate=p3(g_ple_gate, m_g_ple_gate, v_g_ple_gate), g_ple_post=p3(g_ple_post, m_g_ple_post, v_g_ple_post),
        g_final=p3(row(g_final), row(m_g_final), row(v_g_final)), g_ffn=p3(g_ffn, m_g_ffn, v_g_ffn),
        ln_g=p3(ln_g, m_ln_g, v_ln_g), ln_b=p3(ln_b, m_ln_b, v_ln_b), conv_b=p3(conv_b, m_conv_b, v_conv_b),
        pool_scale=p3(pool_scale, m_pool_scale, v_pool_scale), pool_w=p3(pool_w[0], m_pool_w[0], v_pool_w[0]),
        conv_w=p3(conv_w[0], m_conv_w[0], v_conv_w[0]), g_mix=p3(g_mix, m_g_mix, v_g_mix))
    loss, small = _small_adam(place, list(small_all_0) + list(small_all_1), small_0 + small_1, params)
    back = dict(g_final=lambda a: a.reshape(D_MODEL), pool_w=lambda a: a[None], conv_w=lambda a: a[None])

    names = ["g_mix", "w_in", "conv_w", "conv_b", "ln_g", "ln_b", "pool_w", "pool_scale", "w_out", "g_ffn",
             "w_gate_up", "w_down", "g_ple_gate", "w_ple_gate", "w_ple_up", "g_ple_post", "g_final"]
    big_at = {"w_in": 0, "w_gate_up": 1, "w_out": 2, "w_down": 3, "w_ple_gate": 4, "w_ple_up": 5}
    out = [loss[0, 0], grad_x.reshape(1, seq, D_MODEL)]
    for kind in range(4):
        for nm in names:
            if nm in big_at:
                out.append(big_upd[big_at[nm]][kind][None])
            else:
                out.append(back.get(nm, lambda a: a)(small[nm][kind]))
    return tuple(out)
```

```python
import functools

import jax
import jax.numpy as jnp
from jax import lax
from jax.experimental import pallas as pl
from jax.experimental.pallas import tpu as pltpu

f32, bf16 = jnp.float32, jnp.bfloat16

EPS = 1e-6
D_MODEL = 1024
C_CONV = 512
C_POOL = 512
POOL_WINDOWS = (2, 4, 8, 16)
POOL_GROUP = 128
CONV_K = 31
D_FF = 2816
D_PLE = 256
N_CHIPS = 4
N_DEV = 8
W_IN_COLS = 2 * C_CONV + C_POOL
W_IN_CHUNK = W_IN_COLS // N_CHIPS
FF_CHUNK = 2 * D_FF // N_CHIPS
PLE_CHUNK = D_MODEL // N_CHIPS
HALO = 32
ROW_TILE = 512
SHIFT_PAD = 32
SHIFT_GROUPS = 16
FF_SUB = (0, 512, 1024, FF_CHUNK)
VMEM_LIMIT = 56 * 1024 * 1024

ADAM_LR = 0.001
ADAM_B1 = 0.9
ADAM_B2 = 0.999
ADAM_EPS = 1e-08
ADAM_WD = 0.01
ADAM_STEP = 10

MESH = pl.DeviceIdType.MESH
ANY = pl.BlockSpec(memory_space=pl.ANY)
VMEM = pl.BlockSpec(memory_space=pltpu.VMEM)


def _cp(*sem):
    return pltpu.CompilerParams(dimension_semantics=sem, vmem_limit_bytes=VMEM_LIMIT)


def _dot(a, b):
    return jnp.dot(a, b, preferred_element_type=f32)


def _dot_nt(a, b):
    return lax.dot_general(a, b, (((1,), (1,)), ((), ())), preferred_element_type=f32)


def _dot_tn(a, b):
    return lax.dot_general(a, b, (((0,), (0,)), ((), ())), preferred_element_type=f32)


def _sigmoid(v):
    return jax.nn.sigmoid(v)


def _rms_fwd(v, g):
    r = lax.rsqrt(jnp.mean(v * v, axis=-1, keepdims=True) + EPS)
    vh = v * r
    return vh * g, vh, r


def _rms_bwd(dy, vh, r, g):
    dvh = dy * g
    dv = r * (dvh - vh * jnp.mean(dvh * vh, axis=-1, keepdims=True))
    return dv, jnp.sum(dy * vh, axis=0, keepdims=True)


def _silu_grad(v, s):
    return s * (1.0 + v * (1.0 - s))


def _row(i, n):
    return pl.BlockSpec((n[0], n[1]), lambda *a: (a[i], 0))


def _full(shape):
    nd = len(shape)
    return pl.BlockSpec(shape, lambda *a: (0,) * nd)


def _place():
    x, y, c = lax.axis_index("x"), lax.axis_index("y"), lax.axis_index("c")
    others = [(1 - x, y), (x, 1 - y), (1 - x, 1 - y)]
    return x, y, c, 2 * x + y, others


def _remote(src, dst, send_sem, recv_sem, dev):
    return pltpu.make_async_remote_copy(src_ref=src, dst_ref=dst, send_sem=send_sem, recv_sem=recv_sem,
                                        device_id=dev, device_id_type=MESH)


SHARD_STEPS = 4


def _cast_into_slots(me, ws, dtypes):
    n = len(ws)

    def body(me_ref, *refs):
        for w_ref, o_ref, dtype in zip(refs[:n], refs[n:], dtypes):
            o_ref[0] = w_ref[...].astype(dtype)

    return pl.pallas_call(
        body, name="cast_shards",
        grid_spec=pltpu.PrefetchScalarGridSpec(
            num_scalar_prefetch=1, grid=(SHARD_STEPS,),
            in_specs=[pl.BlockSpec((w.shape[0] // SHARD_STEPS, w.shape[1]), lambda r, me_ref: (r, 0)) for w in ws],
            out_specs=[pl.BlockSpec((1, w.shape[0] // SHARD_STEPS, w.shape[1]), lambda r, me_ref: (me_ref[0], r, 0))
                       for w in ws]),
        out_shape=[jax.ShapeDtypeStruct((N_CHIPS,) + w.shape, dt) for w, dt in zip(ws, dtypes)],
        compiler_params=_cp("parallel"),
    )(me, *ws)


class _Gather:
    def __init__(self, bufs, send_sems, recv_sems):
        self.bufs, self.send_sems, self.recv_sems = bufs, send_sems, recv_sems
        self.x, self.y, self.c, self.me, self.others = _place()
        self.halves = [b.shape[1] // 2 for b in bufs]

    def _piece(self, k, chip, half):
        return self.bufs[k].at[chip, pl.ds(half * self.halves[k], self.halves[k]), :]

    def _ici(self, k, j, chip):
        ox, oy = self.others[j]
        piece = self._piece(k, chip, self.c)
        return _remote(piece, piece, self.send_sems.at[6 * k + j], self.recv_sems.at[6 * k + j], (ox, oy, self.c))

    def _relay(self, k):
        first = self.c == 0
        piece = self._piece(k, jnp.where(first, self.chip(0), self.chip(1)), self.c)
        to = (jnp.where(first, self.others[1][0], self.others[0][0]),
              jnp.where(first, self.others[1][1], self.others[0][1]), self.c)
        return _remote(piece, piece, self.send_sems.at[6 * k + 2], self.recv_sems.at[6 * k + 2], to)

    def _pair(self, k, j, half):
        ox, oy = self.others[j]
        piece = self._piece(k, 2 * ox + oy, half)
        return _remote(piece, piece, self.send_sems.at[6 * k + 3 + j], self.recv_sems.at[6 * k + 3 + j],
                       (self.x, self.y, 1 - self.c))

    def _each(self, ks=None):
        return [(k, j) for k in (range(len(self.bufs)) if ks is None else ks) for j in range(3)]

    def chip(self, j):
        ox, oy = self.others[j]
        return 2 * ox + oy

    def start(self):
        for k in range(len(self.bufs)):
            for j in range(2):
                self._ici(k, j, self.me).start()

    def forward(self, pairs=None):
        for k, j in self._each() if pairs is None else pairs:
            self._ici(k, j, self.chip(j)).wait_recv()
            self._pair(k, j, self.c).start()
            if j < 2:
                pl.when(self.c == j)(self._relay(k).start)

    def landed(self, pairs):
        for k, j in pairs:
            self._pair(k, j, 1 - self.c).wait_recv()

    def finish(self, ks=None):
        self.landed(self._each(ks))
        for k in range(len(self.bufs)):
            for j in range(2):
                self._ici(k, j, self.me).wait_send()
            self._relay(k).wait_send()
            for j in range(3):
                self._pair(k, j, self.c).wait_send()

    @staticmethod
    def scratch(n):
        return [pltpu.SemaphoreType.DMA((6 * n,)), pltpu.SemaphoreType.DMA((6 * n,))]


def _carried(bufs):
    n = len(bufs)
    return dict(in_specs=[ANY] * n, out_specs=[ANY] * n,
                out_shape=[jax.ShapeDtypeStruct(b.shape, b.dtype) for b in bufs], scratch=_Gather.scratch(n))


def _mix_in(x, g_mix, order, carry):
    s = x.shape[0]
    tm = min(2 * ROW_TILE, s)
    n = s // tm
    nc = len(carry)
    cs = _carried(carry)

    def body(order_ref, x_ref, g_ref, *refs):
        z_ref = refs[nc]
        bufs = refs[nc + 1:2 * nc + 1]
        h_ref, w_ref, w_sem = refs[2 * nc + 1:2 * nc + 4]
        gather = _Gather(bufs, *refs[2 * nc + 4:])
        q, i = pl.program_id(0), pl.program_id(1)
        first = i == 0
        pl.when(jnp.logical_and(q == 0, first))(gather.start)
        for j in range(3):

            @pl.when(jnp.logical_and(q == j + 1, first))
            def _():
                gather.forward([(0, j)])
                gather.landed([(0, j)])
                if j == 1:
                    gather.forward([(k, jj) for k in range(1, nc) for jj in range(2)])

        @pl.when(first)
        def _():
            load = pltpu.make_async_copy(bufs[0].at[order_ref[q]], w_ref, w_sem)
            load.start()
            load.wait()

        @pl.when(q == 0)
        def _():
            h, _, _ = _rms_fwd(x_ref[...], g_ref[...])
            h_ref[i] = h.astype(bf16)

        z_ref[...] = _dot(h_ref[i], w_ref[...])

        @pl.when(jnp.logical_and(q == N_CHIPS - 1, i == n - 1))
        def _():
            gather.forward([(k, 2) for k in range(1, nc)])
            gather.finish(range(1, nc))

    res = pl.pallas_call(
        body, name="mix_in",
        grid_spec=pltpu.PrefetchScalarGridSpec(
            num_scalar_prefetch=1, grid=(N_CHIPS, n),
            in_specs=[pl.BlockSpec((tm, D_MODEL), lambda q, i, order_ref: (jnp.where(q == 0, i, 0), 0)),
                      pl.BlockSpec((1, D_MODEL), lambda q, i, order_ref: (0, 0))] + cs["in_specs"],
            out_specs=[pl.BlockSpec((tm, W_IN_CHUNK), lambda q, i, order_ref: (i, order_ref[q]))] + cs["out_specs"],
            scratch_shapes=[pltpu.VMEM((n, tm, D_MODEL), bf16), pltpu.VMEM((D_MODEL, W_IN_CHUNK), bf16),
                            pltpu.SemaphoreType.DMA(())] + cs["scratch"]),
        out_shape=[jax.ShapeDtypeStruct((s, W_IN_COLS), f32)] + cs["out_shape"],
        input_output_aliases={3 + k: 1 + k for k in range(nc)},
        compiler_params=_cp("arbitrary", "arbitrary"),
    )(order, x, g_mix, *carry)
    return res[0], res[1:]


def _pool_counts(tm, w, first_row):
    t1 = (lax.broadcasted_iota(jnp.int32, (tm, 1), 0) + first_row + 1).astype(f32)
    return jnp.minimum(t1, float(w))


def _conv_pool_out(z, x, conv_w, conv_b, ln_g, ln_b, pool_w, pool_scale, w_out, g_ffn, carry):
    s = x.shape[0]
    tm = min(ROW_TILE, s)
    n = s // tm
    hb = tm // HALO
    nv = tm // 8
    assert nv >= SHIFT_PAD and nv % SHIFT_GROUPS == 0
    nc = len(carry)
    cs = _carried(carry)

    def body(z_ref, zp_ref, x_ref, cw_ref, cb_ref, lg_ref, lb_ref, pw_ref, ps_ref, wo_ref, gf_ref, *refs):
        x1_ref, mix_ref, u1_ref, pooled_ref, h2_ref = refs[nc:nc + 5]
        eu, ev, ss, u0_ref = refs[2 * nc + 5:2 * nc + 9]
        gather = _Gather(refs[nc + 5:2 * nc + 5], *refs[2 * nc + 9:])
        i = pl.program_id(0)
        pl.when(i == 0)(gather.start)
        for k in range(nc):
            pl.when(i == min(n // 2 + 2 * k, n - 1))(functools.partial(gather.forward, [(k, 0), (k, 1)]))
        keep = (i > 0).astype(f32)
        zp = zp_ref[...] * keep
        u0_prev = zp[:, :C_CONV] * _sigmoid(zp[:, C_CONV:2 * C_CONV])
        u0_ref[...] = z_ref[:, :C_CONV] * _sigmoid(z_ref[:, C_CONV:2 * C_CONV])
        for c, w_pool in enumerate(POOL_WINDOWS):
            lanes = slice(c * 128, (c + 1) * 128)
            v_lanes = slice(2 * C_CONV + c * 128, 2 * C_CONV + (c + 1) * 128)
            eu[c, pl.ds(0, SHIFT_PAD, stride=8), :] = u0_prev[:, lanes]
            ev[c, pl.ds(0, SHIFT_PAD, stride=8), :] = zp[:, v_lanes]
            for j in range(8):
                rows = slice(j * nv, (j + 1) * nv)
                eu[c, pl.ds(SHIFT_PAD * 8 + j, nv, stride=8), :] = u0_ref[rows, lanes]
                ev[c, pl.ds(SHIFT_PAD * 8 + j, nv, stride=8), :] = z_ref[rows, v_lanes]
                if j >= 1:
                    edge = slice(j * nv - SHIFT_PAD, j * nv)
                    eu[c, pl.ds(j, SHIFT_PAD, stride=8), :] = u0_ref[edge, lanes]
                    ev[c, pl.ds(j, SHIFT_PAD, stride=8), :] = z_ref[edge, v_lanes]
            for v0 in range(0, nv, SHIFT_GROUPS):
                span = SHIFT_GROUPS * 8
                acc = jnp.zeros((span, 128), f32) + cb_ref[:, lanes]
                for k in range(CONV_K):
                    acc = acc + cw_ref[pl.ds(k, 1), lanes] * eu[c, pl.ds((SHIFT_PAD - (CONV_K - 1) + v0 + k) * 8, span), :]
                ss[0, v0 * 8:v0 * 8 + span, :] = acc
                acc = ev[c, pl.ds((SHIFT_PAD + v0) * 8, span), :]
                for d in range(1, w_pool):
                    acc = acc + ev[c, pl.ds((SHIFT_PAD + v0 - d) * 8, span), :]
                ss[1, v0 * 8:v0 * 8 + span, :] = acc
            for j in range(8):
                rows = slice(j * nv, (j + 1) * nv)
                u1_ref[rows, lanes] = ss[0, pl.ds(j, nv, stride=8), :]
                mean = ss[1, pl.ds(j, nv, stride=8), :] / _pool_counts(nv, w_pool, i * tm + j * nv)
                pooled_ref[rows, lanes] = (mean - z_ref[rows, v_lanes]).astype(bf16)
        u1 = u1_ref[...]
        mu = jnp.mean(u1, axis=-1, keepdims=True)
        uc = u1 - mu
        rstd = lax.rsqrt(jnp.mean(uc * uc, axis=-1, keepdims=True) + EPS)
        u2 = uc * rstd * lg_ref[...] + lb_ref[...]
        mix_ref[:, :C_CONV] = (u2 * _sigmoid(u2)).astype(bf16)
        for g in range(len(POOL_WINDOWS)):
            cols = slice(g * POOL_GROUP, (g + 1) * POOL_GROUP)
            mixed = _dot(pooled_ref[:, cols], pw_ref[g].astype(bf16))
            mix_ref[:, C_CONV + g * POOL_GROUP:C_CONV + (g + 1) * POOL_GROUP] = (mixed * ps_ref[:, cols]).astype(bf16)
        x1 = x_ref[...] + _dot(mix_ref[...], wo_ref[...])
        x1_ref[...] = x1
        h2_ref[...] = _rms_fwd(x1, gf_ref[...])[0].astype(bf16)
        @pl.when(i == n - 1)
        def _():
            gather.forward([(k, 2) for k in range(nc)])
            gather.finish()

    res = pl.pallas_call(
        body, name="conv_pool_out", grid=(n,),
        in_specs=[_row(0, (tm, W_IN_COLS)),
                  pl.BlockSpec((HALO, W_IN_COLS), lambda i: (jnp.maximum(i * hb - 1, 0), 0)),
                  _row(0, (tm, D_MODEL)), _full((HALO, C_CONV)), _full((1, C_CONV)), _full((1, C_CONV)),
                  _full((1, C_CONV)), _full((4, POOL_GROUP, POOL_GROUP)), _full((1, C_POOL)),
                  _full((D_MODEL, D_MODEL)), _full((1, D_MODEL))] + cs["in_specs"],
        out_specs=[_row(0, (tm, D_MODEL)), _row(0, (tm, D_MODEL)), _row(0, (tm, C_CONV)), _row(0, (tm, C_POOL)),
                   _row(0, (tm, D_MODEL))] + cs["out_specs"],
        out_shape=[jax.ShapeDtypeStruct((s, D_MODEL), f32), jax.ShapeDtypeStruct((s, D_MODEL), bf16),
                   jax.ShapeDtypeStruct((s, C_CONV), f32), jax.ShapeDtypeStruct((s, C_POOL), bf16),
                   jax.ShapeDtypeStruct((s, D_MODEL), bf16)] + cs["out_shape"],
        input_output_aliases={11 + k: 5 + k for k in range(nc)},
        scratch_shapes=[pltpu.VMEM((4, (SHIFT_PAD + nv) * 8, 128), f32), pltpu.VMEM((4, (SHIFT_PAD + nv) * 8, 128), f32),
                        pltpu.VMEM((2, tm, 128), f32), pltpu.VMEM((tm, C_CONV), f32)] + cs["scratch"],
        compiler_params=_cp("arbitrary"),
    )(z, z, x, conv_w, conv_b, ln_g, ln_b, pool_w, pool_scale, w_out, g_ffn, *carry)
    return res[:5], res[5:]


def _ffn_up(h2, w_gu_g, carry):
    s = h2.shape[0]
    tm = min(ROW_TILE, s)
    n = s // tm
    nc = len(carry)
    cs = _carried(carry)

    def body(h2_ref, wg_ref, wu_ref, *refs):
        gu_ref, f_ref = refs[nc:nc + 2]
        gather = _Gather(refs[nc + 2:2 * nc + 2], *refs[2 * nc + 2:])
        i, c = pl.program_id(0), pl.program_id(1)
        pl.when(jnp.logical_and(i == 0, c == 0))(gather.start)
        direct = [(k, j) for k in range(nc) for j in range(2)]
        pl.when(jnp.logical_and(i == n // 2, c == 0))(functools.partial(gather.forward, direct))
        pl.when(jnp.logical_and(i == n - 1, c == 0))(functools.partial(gather.forward, [(k, 2) for k in range(nc)]))
        h = h2_ref[...]
        for lo, hi in zip(FF_SUB[:-1], FF_SUB[1:]):
            gate = _dot(h, wg_ref[0, :, lo:hi])
            up = _dot(h, wu_ref[0, :, lo:hi])
            gu_ref[0, :, lo:hi] = gate.astype(bf16)
            gu_ref[1, :, lo:hi] = up.astype(bf16)
            f_ref[:, lo:hi] = (gate * _sigmoid(gate) * up).astype(bf16)
        pl.when(jnp.logical_and(i == n - 1, c == 1))(gather.finish)

    res = pl.pallas_call(
        body, name="ffn_up", grid=(n, 2),
        in_specs=[_row(0, (tm, D_MODEL)),
                  pl.BlockSpec((1, D_MODEL, FF_CHUNK), lambda i, c: (c, 0, 0)),
                  pl.BlockSpec((1, D_MODEL, FF_CHUNK), lambda i, c: (2 + c, 0, 0))] + cs["in_specs"],
        out_specs=[pl.BlockSpec((2, tm, FF_CHUNK), lambda i, c: (0, i, c)),
                   pl.BlockSpec((tm, FF_CHUNK), lambda i, c: (i, c))] + cs["out_specs"],
        out_shape=[jax.ShapeDtypeStruct((2, s, D_FF), bf16), jax.ShapeDtypeStruct((s, D_FF), bf16)] + cs["out_shape"],
        input_output_aliases={3 + k: 2 + k for k in range(nc)},
        scratch_shapes=cs["scratch"],
        compiler_params=_cp("arbitrary", "arbitrary"),
    )(h2, w_gu_g, w_gu_g, *carry)
    return res[:2], res[2:]


def _ffn_down(x1, f, w_down):
    s = x1.shape[0]
    tm = min(ROW_TILE, s)

    def body(x1_ref, f_ref, wd_ref, x2_ref):
        x2_ref[...] = x1_ref[...] + _dot(f_ref[...], wd_ref[...])

    return pl.pallas_call(
        body, name="ffn_down", grid=(s // tm,),
        in_specs=[_row(0, (tm, D_MODEL)), _row(0, (tm, D_FF)), _full((D_FF, D_MODEL))],
        out_specs=_row(0, (tm, D_MODEL)), out_shape=jax.ShapeDtypeStruct((s, D_MODEL), f32),
        compiler_params=_cp("parallel"),
    )(x1, f, w_down)


def _ple_loss(x2, p, target, g_pg, g_post, g_final, w_pg, w_pu_g):
    s = x2.shape[0]
    tm = min(ROW_TILE, s)
    n = s // tm

    def body(x2_ref, p_ref, t_ref, gpg_ref, gpo_ref, gf_ref, wpg_ref, wpu_ref,
             dx2_ref, dwpg_ref, dwpu_ref, small_ref, apg_ref, apu_ref):
        i = pl.program_id(0)

        @pl.when(i == 0)
        def _():
            apg_ref[...] = jnp.zeros_like(apg_ref)
            apu_ref[...] = jnp.zeros_like(apu_ref)
            small_ref[...] = jnp.zeros_like(small_ref)

        x2 = x2_ref[...]
        h3, x2h, r2 = _rms_fwd(x2, gpg_ref[...])
        h3b = h3.astype(bf16)
        gate = _sigmoid(_dot(h3b, wpg_ref[...]))
        pb = p_ref[...].astype(bf16)
        pe = jnp.concatenate([_dot(pb, wpu_ref[j]) for j in range(N_CHIPS)], axis=-1)
        e, peh, rp = _rms_fwd(pe, gpo_ref[...])
        x3 = x2 + gate * e
        y, x3h, r3 = _rms_fwd(x3, gf_ref[...])
        d = y - t_ref[...]
        loss = 0.5 * jnp.sum(jnp.sum(d * d, axis=-1, keepdims=True) * (1.0 / D_MODEL), axis=0, keepdims=True)
        dx3, dgf = _rms_bwd(d * (1.0 / D_MODEL), x3h, r3, gf_ref[...])
        dpe, dgpo = _rms_bwd(dx3 * gate, peh, rp, gpo_ref[...])
        dgl = (dx3 * e * gate * (1.0 - gate)).astype(bf16)
        apg_ref[...] += _dot_tn(h3b, dgl)
        apu_ref[...] += _dot_tn(pb, dpe.astype(bf16))
        dh3 = _dot_nt(dgl, wpg_ref[...])
        dx2b, dgpg = _rms_bwd(dh3, x2h, r2, gpg_ref[...])
        dx2_ref[...] = dx3 + dx2b
        small_ref[0:1, :] += dgpg
        small_ref[1:2, :] += dgpo
        small_ref[2:3, :] += dgf
        small_ref[3:4, :] += jnp.broadcast_to(loss, (1, D_MODEL))

        @pl.when(i == n - 1)
        def _():
            dwpg_ref[...] = apg_ref[...].astype(bf16)
            for j in range(N_CHIPS):
                dwpu_ref[j] = apu_ref[:, j * PLE_CHUNK:(j + 1) * PLE_CHUNK].astype(bf16)

    return pl.pallas_call(
        body, name="ple_loss", grid=(n,),
        in_specs=[_row(0, (tm, D_MODEL)), _row(0, (tm, D_PLE)), _row(0, (tm, D_MODEL)),
                  _full((1, D_MODEL)), _full((1, D_MODEL)), _full((1, D_MODEL)),
                  _full((D_MODEL, D_MODEL)), _full((N_CHIPS, D_PLE, PLE_CHUNK))],
        out_specs=[_row(0, (tm, D_MODEL)), _full((D_MODEL, D_MODEL)), _full((N_CHIPS, D_PLE, PLE_CHUNK)),
                   _full((8, D_MODEL))],
        out_shape=[jax.ShapeDtypeStruct((s, D_MODEL), f32), jax.ShapeDtypeStruct((D_MODEL, D_MODEL), bf16),
                   jax.ShapeDtypeStruct((N_CHIPS, D_PLE, PLE_CHUNK), bf16), jax.ShapeDtypeStruct((8, D_MODEL), f32)],
        scratch_shapes=[pltpu.VMEM((D_MODEL, D_MODEL), f32), pltpu.VMEM((D_PLE, D_MODEL), f32)],
        compiler_params=_cp("arbitrary"),
    )(x2, p, target, g_pg, g_post, g_final, w_pg, w_pu_g)


def _crossed(parts):
    n = len(parts)
    return dict(in_specs=[ANY] * n, out_specs=[ANY] * n,
                out_shape=[jax.ShapeDtypeStruct(a.shape, a.dtype) for a in parts], scratch=_Cross.scratch(n))


def _ffn_bwd_dx(dx2, x1, gu, g_ffn, w_gu_g, w_down, grads):
    s = x1.shape[0]
    tm = min(ROW_TILE, s)
    n = s // tm
    nc = len(grads)
    cs = dict(in_specs=[ANY] * nc, out_specs=[ANY] * nc, scratch=_CrossAll.scratch(nc),
              out_shape=[jax.ShapeDtypeStruct((N_DEV, g.shape[1] // 2, g.shape[2]), g.dtype) for g in grads])

    def body(dx2_ref, x1_ref, gu_ref, g_ref, wg_ref, wu_ref, wd_ref, *refs):
        dx1_ref, dgu_ref, small_ref = refs[nc:nc + 3]
        acc_ref = refs[2 * nc + 3]
        cross = _CrossAll(refs[:nc], refs[nc + 3:2 * nc + 3], *refs[2 * nc + 4:])
        i, c = pl.program_id(0), pl.program_id(1)
        pl.when(jnp.logical_and(i == 0, c == 0))(cross.start)

        @pl.when(jnp.logical_and(i == 0, c == 0))
        def _():
            small_ref[...] = jnp.zeros_like(small_ref)

        @pl.when(c == 0)
        def _():
            acc_ref[...] = jnp.zeros_like(acc_ref)

        dyb = dx2_ref[...].astype(bf16)
        for lo, hi in zip(FF_SUB[:-1], FF_SUB[1:]):
            df = _dot_nt(dyb, wd_ref[lo:hi, :])
            gate = gu_ref[0, :, lo:hi].astype(f32)
            up = gu_ref[1, :, lo:hi].astype(f32)
            sg = _sigmoid(gate)
            dgate = (df * up * _silu_grad(gate, sg)).astype(bf16)
            dup = (df * gate * sg).astype(bf16)
            dgu_ref[0, :, lo:hi] = dgate
            dgu_ref[1, :, lo:hi] = dup
            acc_ref[...] += _dot_nt(dgate, wg_ref[0, :, lo:hi]) + _dot_nt(dup, wu_ref[0, :, lo:hi])

        @pl.when(c == 1)
        def _():
            _, x1h, r1 = _rms_fwd(x1_ref[...], g_ref[...])
            dx1b, dg = _rms_bwd(acc_ref[...], x1h, r1, g_ref[...])
            dx1_ref[...] = dx2_ref[...] + dx1b
            small_ref[0:1, :] += dg

        pl.when(jnp.logical_and(i == n - 1, c == 1))(cross.finish)

    res = pl.pallas_call(
        body, name="ffn_bwd_dx", grid=(n, 2),
        in_specs=[_row(0, (tm, D_MODEL)), _row(0, (tm, D_MODEL)),
                  pl.BlockSpec((2, tm, FF_CHUNK), lambda i, c: (0, i, c)), _full((1, D_MODEL)),
                  pl.BlockSpec((1, D_MODEL, FF_CHUNK), lambda i, c: (c, 0, 0)),
                  pl.BlockSpec((1, D_MODEL, FF_CHUNK), lambda i, c: (2 + c, 0, 0)),
                  pl.BlockSpec((FF_CHUNK, D_MODEL), lambda i, c: (c, 0))] + cs["in_specs"],
        out_specs=[_row(0, (tm, D_MODEL)), pl.BlockSpec((2, tm, FF_CHUNK), lambda i, c: (0, i, c)),
                   _full((8, D_MODEL))] + cs["out_specs"],
        out_shape=[jax.ShapeDtypeStruct((s, D_MODEL), f32), jax.ShapeDtypeStruct((2, s, D_FF), bf16),
                   jax.ShapeDtypeStruct((8, D_MODEL), f32)] + cs["out_shape"],
        scratch_shapes=[pltpu.VMEM((tm, D_MODEL), f32)] + cs["scratch"],
        compiler_params=_cp("arbitrary", "arbitrary"),
    )(dx2, x1, gu, g_ffn, w_gu_g, w_gu_g, w_down, *grads)
    return res[:3], res[3:]


def _ffn_bwd_dw_gu(h2, dgu):
    s = h2.shape[0]
    ts = min(2 * ROW_TILE, s)
    n = s // ts

    def body(h_ref, d_ref, o_ref, acc_ref):
        t = pl.program_id(1)

        @pl.when(t == 0)
        def _():
            acc_ref[...] = jnp.zeros_like(acc_ref)

        acc_ref[...] += _dot_tn(h_ref[...], d_ref[0])

        @pl.when(t == n - 1)
        def _():
            o_ref[0] = acc_ref[...].astype(bf16)

    return pl.pallas_call(
        body, name="ffn_bwd_dw_gu", grid=(N_CHIPS, n),
        in_specs=[pl.BlockSpec((ts, D_MODEL), lambda j, t: (t, 0)),
                  pl.BlockSpec((1, ts, FF_CHUNK), lambda j, t: (j // 2, t, j % 2))],
        out_specs=pl.BlockSpec((1, D_MODEL, FF_CHUNK), lambda j, t: (j, 0, 0)),
        out_shape=jax.ShapeDtypeStruct((N_CHIPS, D_MODEL, FF_CHUNK), bf16),
        scratch_shapes=[pltpu.VMEM((D_MODEL, FF_CHUNK), f32)],
        compiler_params=_cp("parallel", "arbitrary"),
    )(h2, dgu)


def _ffn_bwd_dw_down(f, dx2):
    s = dx2.shape[0]
    ts = min(2 * ROW_TILE, s)
    n = s // ts

    def body(f_ref, d_ref, o_ref, acc_ref):
        t = pl.program_id(1)

        @pl.when(t == 0)
        def _():
            acc_ref[...] = jnp.zeros_like(acc_ref)

        acc_ref[...] += _dot_tn(f_ref[...], d_ref[...].astype(bf16))

        @pl.when(t == n - 1)
        def _():
            o_ref[...] = acc_ref[...].astype(bf16)

    return pl.pallas_call(
        body, name="ffn_bwd_dw_down", grid=(2, n),
        in_specs=[pl.BlockSpec((ts, FF_CHUNK), lambda c, t: (t, c)),
                  pl.BlockSpec((ts, D_MODEL), lambda c, t: (t, 0))],
        out_specs=pl.BlockSpec((FF_CHUNK, D_MODEL), lambda c, t: (c, 0)),
        out_shape=jax.ShapeDtypeStruct((D_FF, D_MODEL), bf16),
        scratch_shapes=[pltpu.VMEM((FF_CHUNK, D_MODEL), f32)],
        compiler_params=_cp("parallel", "arbitrary"),
    )(f, dx2)


def _mix_bwd_local(dx1, mix, u1, pooled, w_out, ln_g, ln_b, pool_w, pool_scale):
    s = dx1.shape[0]
    tm = min(ROW_TILE, s)
    n = s // tm

    def body(dx1_ref, mix_ref, u1_ref, po_ref, wo_ref, lg_ref, lb_ref, pw_ref, ps_ref,
             du1_ref, dpo_ref, dwo_ref, dpw_ref, small_ref, awo_ref):
        i = pl.program_id(0)

        @pl.when(i == 0)
        def _():
            awo_ref[...] = jnp.zeros_like(awo_ref)
            dpw_ref[...] = jnp.zeros_like(dpw_ref)
            small_ref[...] = jnp.zeros_like(small_ref)

        dyb = dx1_ref[...].astype(bf16)
        dmix = _dot_nt(dyb, wo_ref[...])
        awo_ref[...] += _dot_tn(mix_ref[...], dyb)
        u1 = u1_ref[...]
        mu = jnp.mean(u1, axis=-1, keepdims=True)
        uc = u1 - mu
        rstd = lax.rsqrt(jnp.mean(uc * uc, axis=-1, keepdims=True) + EPS)
        uh = uc * rstd
        u2 = uh * lg_ref[...] + lb_ref[...]
        du2 = dmix[:, :C_CONV] * _silu_grad(u2, _sigmoid(u2))
        duh = du2 * lg_ref[...]
        du1 = rstd * (duh - jnp.mean(duh, axis=-1, keepdims=True) - uh * jnp.mean(duh * uh, axis=-1, keepdims=True))
        du1_ref[...] = du1
        small_ref[0:1, :] += jnp.sum(du2 * uh, axis=0, keepdims=True)
        small_ref[1:2, :] += jnp.sum(du2, axis=0, keepdims=True)
        small_ref[2:3, :] += jnp.sum(du1, axis=0, keepdims=True)
        for g in range(len(POOL_WINDOWS)):
            cols = slice(g * POOL_GROUP, (g + 1) * POOL_GROUP)
            dq = dmix[:, C_CONV + g * POOL_GROUP:C_CONV + (g + 1) * POOL_GROUP]
            pwb = pw_ref[g].astype(bf16)
            pg = po_ref[:, cols]
            mixed = _dot(pg, pwb)
            small_ref[3:4, cols] += jnp.sum(dq * mixed, axis=0, keepdims=True)
            dmixed = (dq * ps_ref[:, cols]).astype(bf16)
            dpw_ref[g] += _dot_tn(pg, dmixed)
            dpo_ref[:, cols] = _dot_nt(dmixed, pwb)

        @pl.when(i == n - 1)
        def _():
            dwo_ref[...] = awo_ref[...].astype(bf16)

    return pl.pallas_call(
        body, name="mix_bwd_local", grid=(n,),
        in_specs=[_row(0, (tm, D_MODEL)), _row(0, (tm, D_MODEL)), _row(0, (tm, C_CONV)), _row(0, (tm, C_POOL)),
                  _full((D_MODEL, D_MODEL)), _full((1, C_CONV)), _full((1, C_CONV)),
                  _full((4, POOL_GROUP, POOL_GROUP)), _full((1, C_POOL))],
        out_specs=[_row(0, (tm, C_CONV)), _row(0, (tm, C_POOL)), _full((D_MODEL, D_MODEL)),
                   _full((4, POOL_GROUP, POOL_GROUP)), _full((8, C_CONV))],
        out_shape=[jax.ShapeDtypeStruct((s, C_CONV), f32), jax.ShapeDtypeStruct((s, C_POOL), f32),
                   jax.ShapeDtypeStruct((D_MODEL, D_MODEL), bf16),
                   jax.ShapeDtypeStruct((4, POOL_GROUP, POOL_GROUP), f32), jax.ShapeDtypeStruct((8, C_CONV), f32)],
        scratch_shapes=[pltpu.VMEM((D_MODEL, D_MODEL), f32)],
        compiler_params=_cp("arbitrary"),
    )(dx1, mix, u1, pooled, w_out, ln_g, ln_b, pool_w, pool_scale)


def _in_bwd(du1, dpo, z, x, dx1, conv_w, g_mix, w_in_g, parts, small_parts):
    s = x.shape[0]
    tm = min(ROW_TILE, s)
    n = s // tm
    hb = tm // HALO
    last = s // HALO - 1
    nv = tm // 8
    assert nv >= SHIFT_PAD and nv % SHIFT_GROUPS == 0
    nc, ns = len(parts), len(small_parts)
    cs = _crossed(parts)

    def body(du_ref, dun_ref, dp_ref, dpn_ref, z_ref, zp_ref, x_ref, dx1_ref, cw_ref, g_ref, w_ref, *refs):
        outs = refs[nc + ns:]
        gx_ref, dw_ref, dcw_ref, small_ref = outs[:4]
        eu, ed, ep, ss, u0_ref, dz_ref, acc_ref, dcw_acc = outs[4 + nc + ns:12 + nc + ns]
        sems = outs[12 + nc + ns:]
        cross = _Cross(refs[:nc], outs[4:4 + nc], sems[0], sems[1])
        gather = _SmallGather(refs[nc:nc + ns], outs[4 + nc:4 + nc + ns], sems[2], sems[3])
        i = pl.program_id(0)
        pl.when(i == 0)(cross.start)
        pl.when(i == 0)(gather.start)

        @pl.when(i == 0)
        def _():
            acc_ref[...] = jnp.zeros_like(acc_ref)
            dcw_acc[...] = jnp.zeros_like(dcw_acc)
            small_ref[...] = jnp.zeros_like(small_ref)

        keep_prev = (i > 0).astype(f32)
        keep_next = (i < n - 1).astype(f32)
        zp = zp_ref[...] * keep_prev
        u0_prev = zp[:, :C_CONV] * _sigmoid(zp[:, C_CONV:2 * C_CONV])
        u0_ref[...] = z_ref[:, :C_CONV] * _sigmoid(z_ref[:, C_CONV:2 * C_CONV])
        du_next = dun_ref[...] * keep_next
        for c, w_pool in enumerate(POOL_WINDOWS):
            lanes = slice(c * 128, (c + 1) * 128)
            eu[c, pl.ds(0, SHIFT_PAD, stride=8), :] = u0_prev[:, lanes]
            ed[c, pl.ds(nv * 8 + 7, SHIFT_PAD, stride=8), :] = du_next[:, lanes]
            ep[c, pl.ds(nv * 8 + 7, SHIFT_PAD, stride=8), :] = (
                dpn_ref[:, lanes] * keep_next / _pool_counts(HALO, w_pool, (i + 1) * tm))
            for j in range(8):
                rows = slice(j * nv, (j + 1) * nv)
                eu[c, pl.ds(SHIFT_PAD * 8 + j, nv, stride=8), :] = u0_ref[rows, lanes]
                ed[c, pl.ds(j, nv, stride=8), :] = du_ref[rows, lanes]
                ep[c, pl.ds(j, nv, stride=8), :] = dp_ref[rows, lanes] / _pool_counts(nv, w_pool, i * tm + j * nv)
                if j >= 1:
                    eu[c, pl.ds(j, SHIFT_PAD, stride=8), :] = u0_ref[j * nv - SHIFT_PAD:j * nv, lanes]
                if j <= 6:
                    edge = slice((j + 1) * nv, (j + 1) * nv + SHIFT_PAD)
                    ed[c, pl.ds(nv * 8 + j, SHIFT_PAD, stride=8), :] = du_ref[edge, lanes]
                    ep[c, pl.ds(nv * 8 + j, SHIFT_PAD, stride=8), :] = (
                        dp_ref[edge, lanes] / _pool_counts(SHIFT_PAD, w_pool, i * tm + (j + 1) * nv))
        for c, w_pool in enumerate(POOL_WINDOWS):
            lanes = slice(c * 128, (c + 1) * 128)
            b_lanes = slice(C_CONV + c * 128, C_CONV + (c + 1) * 128)
            v_lanes = slice(2 * C_CONV + c * 128, 2 * C_CONV + (c + 1) * 128)
            for v0 in range(0, nv, SHIFT_GROUPS):
                span = SHIFT_GROUPS * 8
                acc = jnp.zeros((span, 128), f32)
                for k in range(CONV_K):
                    acc = acc + cw_ref[pl.ds(k, 1), lanes] * ed[c, pl.ds((v0 + CONV_K - 1 - k) * 8, span), :]
                ss[0, v0 * 8:v0 * 8 + span, :] = acc
                acc = ep[c, pl.ds(v0 * 8, span), :]
                for d in range(1, w_pool):
                    acc = acc + ep[c, pl.ds((v0 + d) * 8, span), :]
                ss[1, v0 * 8:v0 * 8 + span, :] = acc
                d1 = ed[c, pl.ds(v0 * 8, span), :]
                for k in range(CONV_K):
                    prod = d1 * eu[c, pl.ds((SHIFT_PAD - (CONV_K - 1) + v0 + k) * 8, span), :]
                    fold = prod[0:8]
                    for r in range(8, span, 8):
                        fold = fold + prod[r:r + 8]
                    dcw_acc[k, :, lanes] += fold
            for j in range(8):
                rows = slice(j * nv, (j + 1) * nv)
                du0 = ss[0, pl.ds(j, nv, stride=8), :]
                av, sv = z_ref[rows, lanes], _sigmoid(z_ref[rows, b_lanes])
                dz_ref[rows, lanes] = (du0 * sv).astype(bf16)
                dz_ref[rows, b_lanes] = (du0 * av * sv * (1.0 - sv)).astype(bf16)
                dz_ref[rows, v_lanes] = (ss[1, pl.ds(j, nv, stride=8), :] - dp_ref[rows, lanes]).astype(bf16)
        h, xh, r = _rms_fwd(x_ref[...], g_ref[...])
        dz = dz_ref[...]
        acc_ref[...] += _dot_tn(h.astype(bf16), dz)
        dh = _dot_nt(dz[:, 0:W_IN_CHUNK], w_ref[0])
        for j in range(1, N_CHIPS):
            dh = dh + _dot_nt(dz[:, j * W_IN_CHUNK:(j + 1) * W_IN_CHUNK], w_ref[j])
        dxb, dg = _rms_bwd(dh, xh, r, g_ref[...])
        gx_ref[...] = dx1_ref[...] + dxb
        small_ref[0:1, :] += dg

        @pl.when(i == n - 1)
        def _():
            for j in range(N_CHIPS):
                dw_ref[j] = acc_ref[:, j * W_IN_CHUNK:(j + 1) * W_IN_CHUNK].astype(bf16)
            dcw_ref[...] = jnp.sum(dcw_acc[...], axis=1)

        pl.when(i == n - 1)(gather.finish)
        pl.when(i == n - 1)(cross.finish)

    nxt = lambda i: (jnp.minimum((i + 1) * hb, last), 0)
    res = pl.pallas_call(
        body, name="in_bwd", grid=(n,),
        in_specs=[_row(0, (tm, C_CONV)), pl.BlockSpec((HALO, C_CONV), nxt),
                  _row(0, (tm, C_POOL)), pl.BlockSpec((HALO, C_POOL), nxt),
                  _row(0, (tm, W_IN_COLS)),
                  pl.BlockSpec((HALO, W_IN_COLS), lambda i: (jnp.maximum(i * hb - 1, 0), 0)),
                  _row(0, (tm, D_MODEL)), _row(0, (tm, D_MODEL)), _full((HALO, C_CONV)), _full((1, D_MODEL)),
                  _full((N_CHIPS, D_MODEL, W_IN_CHUNK))] + cs["in_specs"] + [ANY] * ns,
        out_specs=[_row(0, (tm, D_MODEL)), _full((N_CHIPS, D_MODEL, W_IN_CHUNK)), _full((HALO, C_CONV)),
                   _full((8, D_MODEL))] + cs["out_specs"] + [ANY] * ns,
        out_shape=[jax.ShapeDtypeStruct((s, D_MODEL), f32), jax.ShapeDtypeStruct((N_CHIPS, D_MODEL, W_IN_CHUNK), bf16),
                   jax.ShapeDtypeStruct((HALO, C_CONV), f32), jax.ShapeDtypeStruct((8, D_MODEL), f32)] + cs["out_shape"]
        + _gathered_shapes(small_parts),
        scratch_shapes=[pltpu.VMEM((4, (SHIFT_PAD + nv) * 8, 128), f32), pltpu.VMEM((4, (nv + SHIFT_PAD) * 8, 128), f32),
                        pltpu.VMEM((4, (nv + SHIFT_PAD) * 8, 128), f32), pltpu.VMEM((2, tm, 128), f32),
                        pltpu.VMEM((tm, C_CONV), f32), pltpu.VMEM((tm, W_IN_COLS), bf16),
                        pltpu.VMEM((D_MODEL, W_IN_COLS), f32), pltpu.VMEM((HALO, 8, C_CONV), f32)] + cs["scratch"]
        + _SmallGather.scratch(ns),
        compiler_params=_cp("arbitrary"),
    )(du1, du1, dpo, dpo, z, z, x, dx1, conv_w, g_mix, w_in_g, *parts, *small_parts)
    return res[:4], res[4:4 + nc], res[4 + nc:]


def _pair_reduce(name, grads):
    nk = len(grads)
    halves = [g.shape[1] // 2 for g in grads]

    def body(*refs):
        ins, outs, got = refs[:nk], refs[nk:2 * nk], refs[2 * nk:3 * nk]
        send_sems, recv_sems = refs[3 * nk:]
        x, y, c, _, _ = _place()

        def half(k, core):
            return pl.ds(pl.multiple_of(core * halves[k], 16), halves[k])

        cps = [_remote(ins[k].at[:, half(k, 1 - c), :], got[k], send_sems.at[k], recv_sems.at[k], (x, y, 1 - c))
               for k in range(nk)]
        for cp in cps:
            cp.start()
        for k, cp in enumerate(cps):
            cp.wait_recv()
            outs[k][...] = (ins[k][:, half(k, c), :].astype(f32) + got[k][...].astype(f32)).astype(bf16)
        for cp in cps:
            cp.wait_send()

    shapes = [(N_CHIPS, h, g.shape[2]) for g, h in zip(grads, halves)]
    return pl.pallas_call(
        body, name=f"pair_reduce_{name}", in_specs=[VMEM] * nk, out_specs=[VMEM] * nk,
        out_shape=[jax.ShapeDtypeStruct(s, bf16) for s in shapes],
        scratch_shapes=[pltpu.VMEM(s, bf16) for s in shapes]
        + [pltpu.SemaphoreType.DMA((nk,)), pltpu.SemaphoreType.DMA((nk,))],
        compiler_params=pltpu.CompilerParams(vmem_limit_bytes=VMEM_LIMIT),
    )(*grads)


class _Cross:
    def __init__(self, parts, landed, send_sems, recv_sems):
        self.parts, self.landed, self.send_sems, self.recv_sems = parts, landed, send_sems, recv_sems
        _, _, self.c, self.me, self.others = _place()

    def _copy(self, k, j, src_chunk, dst_slot):
        ox, oy = self.others[j]
        return _remote(self.parts[k].at[src_chunk], self.landed[k].at[dst_slot], self.send_sems.at[3 * k + j],
                       self.recv_sems.at[3 * k + j], (ox, oy, self.c))

    def _each(self):
        return [(k, j, 2 * self.others[j][0] + self.others[j][1]) for k in range(len(self.parts)) for j in range(3)]

    def start(self):
        for k, j, chip in self._each():
            self._copy(k, j, chip, self.me).start()

    def finish(self):
        for k, j, chip in self._each():
            self._copy(k, j, chip, chip).wait_recv()
        for k, j, chip in self._each():
            self._copy(k, j, chip, self.me).wait_send()

    @staticmethod
    def scratch(n):
        return [pltpu.SemaphoreType.DMA((3 * n,)), pltpu.SemaphoreType.DMA((3 * n,))]


class _CrossAll:
    def __init__(self, grads, landed, send_sems, recv_sems):
        self.grads, self.landed, self.send_sems, self.recv_sems = grads, landed, send_sems, recv_sems
        self.x, self.y, self.c, self.me, self.others = _place()
        self.dev = 2 * self.me + self.c

    def _piece(self, k, chip, half):
        rows = self.grads[k].shape[1] // 2
        return self.grads[k].at[chip, pl.ds(half * rows, rows), :]

    def _to_sibling(self, k):
        return _remote(self._piece(k, self.me, 1 - self.c), self.landed[k].at[self.dev], self.send_sems.at[7 * k],
                       self.recv_sems.at[7 * k], (self.x, self.y, 1 - self.c))

    def _to_chip(self, k, j, half):
        ox, oy = self.others[j]
        return _remote(self._piece(k, 2 * ox + oy, half), self.landed[k].at[self.dev],
                       self.send_sems.at[7 * k + 1 + 2 * j + half], self.recv_sems.at[7 * k + 1 + 2 * j + self.c],
                       (ox, oy, half))

    def _from(self, k, sem, slot):
        return _remote(self.landed[k].at[slot], self.landed[k].at[slot], self.send_sems.at[7 * k + sem],
                       self.recv_sems.at[7 * k + sem], (self.x, self.y, 1 - self.c))

    def start(self):
        for k in range(len(self.grads)):
            self._to_sibling(k).start()
            for j in range(3):
                for half in range(2):
                    self._to_chip(k, j, half).start()

    def finish(self):
        for k in range(len(self.grads)):
            self._from(k, 0, 2 * self.me + 1 - self.c).wait_recv()
            for j, (ox, oy) in enumerate(self.others):
                for core in range(2):
                    self._from(k, 1 + 2 * j + core, 4 * ox + 2 * oy + core).wait_recv()
        for k in range(len(self.grads)):
            self._to_sibling(k).wait_send()
            for j in range(3):
                for half in range(2):
                    self._to_chip(k, j, half).wait_send()

    @staticmethod
    def scratch(n):
        return [pltpu.SemaphoreType.DMA((7 * n,)), pltpu.SemaphoreType.DMA((7 * n,))]


def _rs_sum_chips(name, place, landed, part):
    _, half, cols = landed.shape

    def body(place_ref, l_ref, p_ref, o_ref):
        me = place_ref[0]
        own = p_ref[0].astype(f32)
        acc = jnp.where(me == 0, own, l_ref[0].astype(f32))
        for j in range(1, N_CHIPS):
            acc = acc + jnp.where(me == j, own, l_ref[j].astype(f32))
        o_ref[...] = acc

    return pl.pallas_call(
        body, name=f"rs_sum_chips_{name}",
        grid_spec=pltpu.PrefetchScalarGridSpec(
            num_scalar_prefetch=1, grid=(1,),
            in_specs=[pl.BlockSpec((N_CHIPS, half, cols), lambda t, place_ref: (0, 0, 0)),
                      pl.BlockSpec((1, half, cols), lambda t, place_ref: (place_ref[0], 0, 0))],
            out_specs=pl.BlockSpec((half, cols), lambda t, place_ref: (place_ref[1], 0))),
        out_shape=jax.ShapeDtypeStruct((2 * half, cols), f32),
        compiler_params=_cp("arbitrary"),
    )(place, landed, part)


def _grad_pair(shards):
    nk = len(shards)

    def body(*refs):
        outs = refs[nk:2 * nk]
        send_sems, recv_sems = refs[2 * nk:]
        x, y, c, _, _ = _place()

        def half(k, core):
            h = outs[k].shape[0] // 2
            return outs[k].at[pl.ds(core * h, h), :]

        cps = [_remote(half(k, c), half(k, c), send_sems.at[k], recv_sems.at[k], (x, y, 1 - c)) for k in range(nk)]
        for cp in cps:
            cp.start()
        for k in range(nk):
            _remote(half(k, 1 - c), half(k, 1 - c), send_sems.at[k], recv_sems.at[k], (x, y, 1 - c)).wait_recv()
        for cp in cps:
            cp.wait_send()

    return pl.pallas_call(
        body, name="grad_pair", in_specs=[ANY] * nk, out_specs=[ANY] * nk,
        out_shape=[jax.ShapeDtypeStruct(a.shape, f32) for a in shards],
        input_output_aliases={k: k for k in range(nk)},
        scratch_shapes=[pltpu.SemaphoreType.DMA((nk,)), pltpu.SemaphoreType.DMA((nk,))],
    )(*shards)


def _adam_math(w, g, m, v):
    m = ADAM_B1 * m + (1.0 - ADAM_B1) * g
    v = ADAM_B2 * v + (1.0 - ADAM_B2) * (g * g)
    m_hat = m / (1.0 - ADAM_B1 ** ADAM_STEP)
    v_hat = v / (1.0 - ADAM_B2 ** ADAM_STEP)
    delta = -ADAM_LR * (m_hat / (jnp.sqrt(v_hat) + ADAM_EPS) + ADAM_WD * w)
    return delta, m, v


def _adam(ws, gs, ms, vs):
    n = len(ws)
    steps = 2 * SHARD_STEPS

    def body(*refs):
        for k in range(n):
            w_ref, g_ref, m_ref, v_ref = (refs[j * n + k] for j in range(4))
            go_ref, d_ref, nm_ref, nv_ref = (refs[(4 + j) * n + k] for j in range(4))
            g = g_ref[...]
            go_ref[...] = g
            d_ref[...], nm_ref[...], nv_ref[...] = _adam_math(w_ref[...], g, m_ref[...], v_ref[...])

    specs = [_row(0, (w.shape[0] // steps, w.shape[1])) for w in ws]
    res = pl.pallas_call(
        body, name="adam_shards", grid=(steps,), in_specs=specs * 4, out_specs=specs * 4,
        out_shape=[jax.ShapeDtypeStruct(w.shape, f32) for w in ws] * 4,
        compiler_params=_cp("parallel"),
    )(*ws, *gs, *ms, *vs)
    return [tuple(res[j * n + k] for j in range(4)) for k in range(n)]


class _SmallGather:
    def __init__(self, parts, bufs, send_sems, recv_sems):
        self.parts, self.bufs, self.send_sems, self.recv_sems = parts, bufs, send_sems, recv_sems
        self.x, self.y, self.c, _, self.others = _place()
        self.sibling = (self.x, self.y, 1 - self.c)

    def _copy(self, a, k, block, to, src=None):
        slot = self.bufs[a].at[4 * block[0] + 2 * block[1] + block[2]]
        return _remote(slot if src is None else src, slot, self.send_sems.at[7 * a + k], self.recv_sems.at[7 * a + k],
                       to)

    def _first(self, a):
        me = (self.x, self.y, self.c)
        return [self._copy(a, 0, me, self.sibling, src=self.parts[a])] + [
            self._copy(a, 1 + j, me, (*chip, self.c), src=self.parts[a]) for j, chip in enumerate(self.others)]

    def _passed(self, a):
        return [self._copy(a, 4 + j, (*chip, self.c), self.sibling) for j, chip in enumerate(self.others)]

    @staticmethod
    def scratch(n):
        return [pltpu.SemaphoreType.DMA((7 * n,)), pltpu.SemaphoreType.DMA((7 * n,))]

    def start(self):
        for a in range(len(self.parts)):
            for cp in self._first(a):
                cp.start()

    def finish(self):
        sent = []
        for a in range(len(self.parts)):
            passed = self._passed(a)
            for j, chip in enumerate(self.others):
                self._copy(a, 1 + j, (*chip, self.c), self.sibling).wait_recv()
                passed[j].start()
            sent += self._first(a) + passed
        for a in range(len(self.parts)):
            self._copy(a, 0, self.sibling, self.sibling).wait_recv()
            for j, chip in enumerate(self.others):
                self._copy(a, 4 + j, (*chip, 1 - self.c), self.sibling).wait_recv()
        for cp in sent:
            cp.wait_send()


def _gathered_shapes(parts):
    return [jax.ShapeDtypeStruct((N_DEV,) + p.shape, p.dtype) for p in parts]


def _rs_sum_group(name, place, landed, parts, landed_all, grads_all, cross_parts, small_parts):
    nk, na, nx, ns = len(landed), len(landed_all), len(cross_parts), len(small_parts)
    dims = [a.shape[1:] for a in landed]
    dims_all = [a.shape[1:] for a in landed_all]

    def body(place_ref, *refs):
        take = iter(refs)
        l_refs, p_refs, la_refs, ga_refs, x_refs, sp_refs, o_refs, oa_refs, xl_refs, sbufs = (
            [next(take) for _ in range(cnt)] for cnt in (nk, nk, na, na, nx, ns, nk, na, nx, ns))
        sems = list(take)
        cross = _Cross(x_refs, xl_refs, sems[0], sems[1])
        small = _SmallGather(sp_refs, sbufs, sems[2], sems[3])
        t = pl.program_id(0)

        @pl.when(t == 0)
        def _():
            cross.start()
            small.start()

        me = place_ref[0]
        for l_ref, p_ref, o_ref in zip(l_refs, p_refs, o_refs):
            own = p_ref[0].astype(f32)
            acc = jnp.where(me == 0, own, l_ref[0].astype(f32))
            for j in range(1, N_CHIPS):
                acc = acc + jnp.where(me == j, own, l_ref[j].astype(f32))
            o_ref[...] = acc
        dev = 2 * me + place_ref[1]
        for l_ref, g_ref, o_ref in zip(la_refs, ga_refs, oa_refs):
            own = g_ref[0].astype(f32)
            acc = jnp.where(dev == 0, own, l_ref[0].astype(f32))
            for d in range(1, N_DEV):
                acc = acc + jnp.where(dev == d, own, l_ref[d].astype(f32))
            o_ref[...] = acc

        @pl.when(t == 1)
        def _():
            small.finish()
            cross.finish()

    def halves(h, c, lead, index):
        return pl.BlockSpec((lead, h // 2, c) if lead else (h // 2, c), index)

    in_specs = [halves(h, c, N_CHIPS, lambda t, pr: (0, t, 0)) for h, c in dims]
    in_specs += [halves(h, c, 1, lambda t, pr: (pr[0], t, 0)) for h, c in dims]
    in_specs += [halves(h, c, N_DEV, lambda t, pr: (0, t, 0)) for h, c in dims_all]
    in_specs += [halves(h, c, 1, lambda t, pr: (pr[0], 2 * pr[1] + t, 0)) for h, c in dims_all]
    in_specs += [ANY] * (nx + ns)
    out_specs = [halves(h, c, 0, lambda t, pr: (2 * pr[1] + t, 0)) for h, c in dims + dims_all] + [ANY] * (nx + ns)
    out_shape = [jax.ShapeDtypeStruct((2 * h, c), f32) for h, c in dims + dims_all]
    out_shape += [jax.ShapeDtypeStruct(a.shape, a.dtype) for a in cross_parts] + _gathered_shapes(small_parts)
    res = pl.pallas_call(
        body, name=f"rs_sum_{name}",
        grid_spec=pltpu.PrefetchScalarGridSpec(
            num_scalar_prefetch=1, grid=(2,), in_specs=in_specs, out_specs=out_specs,
            scratch_shapes=_Cross.scratch(nx) + _SmallGather.scratch(ns)),
        out_shape=out_shape, compiler_params=_cp("arbitrary"),
    )(place, *landed, *parts, *landed_all, *grads_all, *cross_parts, *small_parts)
    return res[:nk], res[nk:nk + na], res[nk + na:nk + na + nx], res[nk + na + nx:]


SMALL_PARAMS = ("g_ple_gate", "g_ple_post", "g_final", "g_ffn", "ln_g", "ln_b", "conv_b", "pool_scale", "pool_w",
                "conv_w", "g_mix")
SMALL_ROWS = {"g_ple_gate": (0, 0), "g_ple_post": (0, 1), "g_final": (0, 2), "g_ffn": (1, 0), "ln_g": (2, 0),
              "ln_b": (2, 1), "conv_b": (2, 2), "pool_scale": (2, 3), "g_mix": (5, 0)}
LOSS_ROW = (0, 3)


def _small_adam(place, gathered, parts, params):
    nb, names = len(parts), SMALL_PARAMS
    flat = [a for nm in names for a in params[nm]]

    def body(place_ref, *refs):
        b_refs, p_refs = refs[:nb], refs[nb:2 * nb]
        w_refs = refs[2 * nb:2 * nb + 3 * len(names)]
        outs = refs[2 * nb + 3 * len(names):]
        loss_ref, o_refs, cw_sum = outs[0], outs[1:1 + 4 * len(names)], outs[1 + 4 * len(names)]
        chip = place_ref[0]
        me = 2 * chip + place_ref[1]

        def total(blk, idx):
            own = p_refs[blk][idx]
            g = jnp.where(me == 0, own, b_refs[blk][(0,) + idx])
            for d in range(1, N_DEV):
                g = g + jnp.where(me == d, own, b_refs[blk][(d,) + idx])
            return g

        everything = (slice(None), slice(None))
        loss_ref[...] = total(LOSS_ROW[0], (pl.ds(LOSS_ROW[1], 1), pl.ds(0, 128)))
        d_cw = total(4, everything)
        mine = jnp.where(chip == 0, d_cw[:, 0:128], 0.0)
        for j in range(1, N_CHIPS):
            mine = mine + jnp.where(chip == j, d_cw[:, j * 128:(j + 1) * 128], 0.0)
        cw_sum[...] = mine
        for k, nm in enumerate(names):
            w_ref, m_ref, v_ref = w_refs[3 * k:3 * k + 3]
            g_ref, d_ref, nm_ref, nv_ref = o_refs[4 * k:4 * k + 4]
            if nm == "pool_w":
                g = total(3, everything + (slice(None),))
            elif nm == "conv_w":
                g = cw_sum[pl.ds(0, CONV_K), :]
            else:
                blk, row = SMALL_ROWS[nm]
                g = total(blk, (pl.ds(row, 1), slice(None)))
            g_ref[...] = g
            d_ref[...], nm_ref[...], nv_ref[...] = _adam_math(w_ref[...], g, m_ref[...], v_ref[...])

    whole = lambda a: pl.BlockSpec(a.shape, lambda t, pr: (0,) * a.ndim)
    out_shape = [jax.ShapeDtypeStruct((1, 128), f32)]
    out_shape += [jax.ShapeDtypeStruct(params[nm][0].shape, f32) for nm in names for _ in range(4)]
    res = pl.pallas_call(
        body, name="small_adam",
        grid_spec=pltpu.PrefetchScalarGridSpec(
            num_scalar_prefetch=1, grid=(1,),
            in_specs=[whole(a) for a in list(gathered) + list(parts) + flat],
            out_specs=[whole(s) for s in out_shape], scratch_shapes=[pltpu.VMEM((HALO, 128), f32)]),
        out_shape=out_shape, compiler_params=_cp("arbitrary"),
    )(place, *gathered, *parts, *flat)
    return res[0], {nm: res[1 + 4 * k:5 + 4 * k] for k, nm in enumerate(names)}


def _pad_rows(a, rows):
    return jnp.concatenate([a, jnp.zeros((rows - a.shape[0],) + a.shape[1:], a.dtype)], axis=0)


def kernel(x, p, g_mix, w_in, conv_w, conv_b, ln_g, ln_b, pool_w, pool_scale, w_out, g_ffn, w_gate_up, w_down, g_ple_gate, w_ple_gate, w_ple_up, g_ple_post, g_final, loss_target, m_g_mix, m_w_in, m_conv_w, m_conv_b, m_ln_g, m_ln_b, m_pool_w, m_pool_scale, m_w_out, m_g_ffn, m_w_gate_up, m_w_down, m_g_ple_gate, m_w_ple_gate, m_w_ple_up, m_g_ple_post, m_g_final, v_g_mix, v_w_in, v_conv_w, v_conv_b, v_ln_g, v_ln_b, v_pool_w, v_pool_scale, v_w_out, v_g_ffn, v_w_gate_up, v_w_down, v_g_ple_gate, v_w_ple_gate, v_w_ple_up, v_g_ple_post, v_g_final):
    seq = x.shape[1]
    me = 2 * lax.axis_index("x") + lax.axis_index("y")
    chip = me.astype(jnp.int32).reshape(1)
    core = lax.axis_index("c").astype(jnp.int32).reshape(1)
    place = jnp.concatenate([chip, core])
    xs, ps, ts = x.reshape(seq, D_MODEL), p.reshape(seq, D_PLE), loss_target.reshape(seq, D_MODEL)

    big = [w_in[0], w_gate_up[0], w_out[0], w_down[0], w_ple_gate[0], w_ple_up[0]]
    big_m = [m_w_in[0], m_w_gate_up[0], m_w_out[0], m_w_down[0], m_w_ple_gate[0], m_w_ple_up[0]]
    big_v = [v_w_in[0], v_w_gate_up[0], v_w_out[0], v_w_down[0], v_w_ple_gate[0], v_w_ple_up[0]]
    b_in, b_gu, b_out, b_down, b_pg, b_pu, b_cw = _cast_into_slots(
        chip, big + [_pad_rows(conv_w[0], HALO)], [bf16] * len(big) + [f32])
    xi, yi = lax.axis_index("x"), lax.axis_index("y")
    order = jnp.stack([me, 2 * (1 - xi) + yi, 2 * xi + 1 - yi, 2 * (1 - xi) + 1 - yi]).astype(jnp.int32)

    z, (w_in_g, cw_g, w_out_g) = _mix_in(xs, g_mix, order, [b_in, b_cw, b_out])
    conv_w_f = cw_g.transpose(1, 0, 2).reshape(HALO, C_CONV)
    w_out_f = w_out_g.reshape(D_MODEL, D_MODEL)
    (x1, mix, u1, pooled, h2), (w_gu_g,) = _conv_pool_out(z, xs, conv_w_f, conv_b, ln_g, ln_b, pool_w[0], pool_scale,
                                                          w_out_f, g_ffn, [b_gu])
    (gu, ffn_f), (w_down_g, w_pg_g, w_pu_g) = _ffn_up(h2, w_gu_g, [b_down, b_pg, b_pu])
    w_down_f = w_down_g.reshape(D_FF, D_MODEL)
    x2 = _ffn_down(x1, ffn_f, w_down_f)
    w_pg_f = w_pg_g.reshape(D_MODEL, D_MODEL)
    dx2, d_w_pg, d_w_pu, small_ple = _ple_loss(x2, ps, ts, g_ple_gate, g_ple_post, g_final.reshape(1, D_MODEL),
                                               w_pg_f, w_pu_g)
    d_w_down = _ffn_bwd_dw_down(ffn_f, dx2)
    grads_a = [d_w_down.reshape(N_CHIPS, -1, D_MODEL), d_w_pg.reshape(N_CHIPS, -1, D_MODEL), d_w_pu]
    (dx1, dgu, small_ffn), landed_a = _ffn_bwd_dx(dx2, x1, gu, g_ffn, w_gu_g, w_down_f, grads_a)
    d_w_gu = _ffn_bwd_dw_gu(h2, dgu)
    du1, dpo, d_w_out, d_pool_w, small_mix = _mix_bwd_local(dx1, mix, u1, pooled, w_out_f, ln_g, ln_b, pool_w[0],
                                                             pool_scale)
    parts_b = _pair_reduce("b", [d_w_gu, d_w_out.reshape(N_CHIPS, -1, D_MODEL)])
    small_0 = [small_ple, small_ffn, small_mix, d_pool_w]
    (grad_x, d_w_in, d_conv_w, small_in), landed_b, small_all_0 = _in_bwd(du1, dpo, z, xs, dx1, conv_w_f, g_mix,
                                                                            w_in_g, parts_b, small_0)
    parts_c = _pair_reduce("c", [d_w_in])
    small_1 = [d_conv_w, small_in]
    (h_gu, h_out), (h_down, h_pg, h_pu), landed_c, small_all_1 = _rs_sum_group(
        "ab", place, landed_b, parts_b, landed_a, grads_a, parts_c, small_1)
    h_in = _rs_sum_chips("w_in", place, landed_c[0], parts_c[0])
    big_g = _grad_pair([h_in, h_gu, h_out, h_down, h_pg, h_pu])
    big_upd = _adam(big, big_g, big_m, big_v)

    p3 = lambda w, m, v: (w, m, v)
    row = lambda a: a.reshape(1, D_MODEL)
    params = dict(
        g_ple_gate=p3(g_ple_gate, m_g_ple_gate, v_g_ple_gate), g_ple_post=p3(g_ple_post, m_g_ple_post, v_g_ple_post),
        g_final=p3(row(g_final), row(m_g_final), row(v_g_final)), g_ffn=p3(g_ffn, m_g_ffn, v_g_ffn),
        ln_g=p3(ln_g, m_ln_g, v_ln_g), ln_b=p3(ln_b, m_ln_b, v_ln_b), conv_b=p3(conv_b, m_conv_b, v_conv_b),
        pool_scale=p3(pool_scale, m_pool_scale, v_pool_scale), pool_w=p3(pool_w[0], m_pool_w[0], v_pool_w[0]),
        conv_w=p3(conv_w[0], m_conv_w[0], v_conv_w[0]), g_mix=p3(g_mix, m_g_mix, v_g_mix))
    loss, small = _small_adam(place, list(small_all_0) + list(small_all_1), small_0 + small_1, params)
    back = dict(g_final=lambda a: a.reshape(D_MODEL), pool_w=lambda a: a[None], conv_w=lambda a: a[None])

    names = ["g_mix", "w_in", "conv_w", "conv_b", "ln_g", "ln_b", "pool_w", "pool_scale", "w_out", "g_ffn",
             "w_gate_up", "w_down", "g_ple_gate", "w_ple_gate", "w_ple_up", "g_ple_post", "g_final"]
    big_at = {"w_in": 0, "w_gate_up": 1, "w_out": 2, "w_down": 3, "w_ple_gate": 4, "w_ple_up": 5}
    out = [loss[0, 0], grad_x.reshape(1, seq, D_MODEL)]
    for kind in range(4):
        for nm in names:
            if nm in big_at:
                out.append(big_upd[big_at[nm]][kind][None])
            else:
                out.append(back.get(nm, lambda a: a)(small[nm][kind]))
    return tuple(out)
```

```python
import functools

import jax
import jax.numpy as jnp
from jax import lax
from jax.experimental import pallas as pl
from jax.experimental.pallas import tpu as pltpu

f32, bf16 = jnp.float32, jnp.bfloat16

EPS = 1e-6
D_MODEL = 1024
C_CONV = 512
C_POOL = 512
POOL_WINDOWS = (2, 4, 8, 16)
POOL_GROUP = 128
CONV_K = 31
D_FF = 2816
D_PLE = 256
N_CHIPS = 4
N_DEV = 8
W_IN_COLS = 2 * C_CONV + C_POOL
W_IN_CHUNK = W_IN_COLS // N_CHIPS
FF_CHUNK = 2 * D_FF // N_CHIPS
PLE_CHUNK = D_MODEL // N_CHIPS
HALO = 32
ROW_TILE = 512
SHIFT_PAD = 32
SHIFT_GROUPS = 16
FF_SUB = (0, 512, 1024, FF_CHUNK)
VMEM_LIMIT = 56 * 1024 * 1024

ADAM_LR = 0.001
ADAM_B1 = 0.9
ADAM_B2 = 0.999
ADAM_EPS = 1e-08
ADAM_WD = 0.01
ADAM_STEP = 10

MESH = pl.DeviceIdType.MESH
ANY = pl.BlockSpec(memory_space=pl.ANY)
VMEM = pl.BlockSpec(memory_space=pltpu.VMEM)


def _cp(*sem):
    return pltpu.CompilerParams(dimension_semantics=sem, vmem_limit_bytes=VMEM_LIMIT)


def _dot(a, b):
    return jnp.dot(a, b, preferred_element_type=f32)


def _dot_nt(a, b):
    return lax.dot_general(a, b, (((1,), (1,)), ((), ())), preferred_element_type=f32)


def _dot_tn(a, b):
    return lax.dot_general(a, b, (((0,), (0,)), ((), ())), preferred_element_type=f32)


def _sigmoid(v):
    return jax.nn.sigmoid(v)


def _rms_fwd(v, g):
    r = lax.rsqrt(jnp.mean(v * v, axis=-1, keepdims=True) + EPS)
    vh = v * r
    return vh * g, vh, r


def _rms_bwd(dy, vh, r, g):
    dvh = dy * g
    dv = r * (dvh - vh * jnp.mean(dvh * vh, axis=-1, keepdims=True))
    return dv, jnp.sum(dy * vh, axis=0, keepdims=True)


def _silu_grad(v, s):
    return s * (1.0 + v * (1.0 - s))


def _row(i, n):
    return pl.BlockSpec((n[0], n[1]), lambda *a: (a[i], 0))


def _full(shape):
    nd = len(shape)
    return pl.BlockSpec(shape, lambda *a: (0,) * nd)


def _place():
    x, y, c = lax.axis_index("x"), lax.axis_index("y"), lax.axis_index("c")
    others = [(1 - x, y), (x, 1 - y), (1 - x, 1 - y)]
    return x, y, c, 2 * x + y, others


def _remote(src, dst, send_sem, recv_sem, dev):
    return pltpu.make_async_remote_copy(src_ref=src, dst_ref=dst, send_sem=send_sem, recv_sem=recv_sem,
                                        device_id=dev, device_id_type=MESH)


SHARD_STEPS = 4


def _cast_into_slots(me, ws, dtypes):
    n = len(ws)

    def body(me_ref, *refs):
        for w_ref, o_ref, dtype in zip(refs[:n], refs[n:], dtypes):
            o_ref[0] = w_ref[...].astype(dtype)

    return pl.pallas_call(
        body, name="cast_shards",
        grid_spec=pltpu.PrefetchScalarGridSpec(
            num_scalar_prefetch=1, grid=(SHARD_STEPS,),
            in_specs=[pl.BlockSpec((w.shape[0] // SHARD_STEPS, w.shape[1]), lambda r, me_ref: (r, 0)) for w in ws],
            out_specs=[pl.BlockSpec((1, w.shape[0] // SHARD_STEPS, w.shape[1]), lambda r, me_ref: (me_ref[0], r, 0))
                       for w in ws]),
        out_shape=[jax.ShapeDtypeStruct((N_CHIPS,) + w.shape, dt) for w, dt in zip(ws, dtypes)],
        compiler_params=_cp("parallel"),
    )(me, *ws)


class _Gather:
    def __init__(self, bufs, send_sems, recv_sems):
        self.bufs, self.send_sems, self.recv_sems = bufs, send_sems, recv_sems
        self.x, self.y, self.c, self.me, self.others = _place()
        self.halves = [b.shape[1] // 2 for b in bufs]

    def _piece(self, k, chip, half):
        return self.bufs[k].at[chip, pl.ds(half * self.halves[k], self.halves[k]), :]

    def _ici(self, k, j, chip):
        ox, oy = self.others[j]
        piece = self._piece(k, chip, self.c)
        return _remote(piece, piece, self.send_sems.at[6 * k + j], self.recv_sems.at[6 * k + j], (ox, oy, self.c))

    def _relay(self, k):
        first = self.c == 0
        piece = self._piece(k, jnp.where(first, self.chip(0), self.chip(1)), self.c)
        to = (jnp.where(first, self.others[1][0], self.others[0][0]),
              jnp.where(first, self.others[1][1], self.others[0][1]), self.c)
        return _remote(piece, piece, self.send_sems.at[6 * k + 2], self.recv_sems.at[6 * k + 2], to)

    def _pair(self, k, j, half):
        ox, oy = self.others[j]
        piece = self._piece(k, 2 * ox + oy, half)
        return _remote(piece, piece, self.send_sems.at[6 * k + 3 + j], self.recv_sems.at[6 * k + 3 + j],
                       (self.x, self.y, 1 - self.c))

    def _each(self, ks=None):
        return [(k, j) for k in (range(len(self.bufs)) if ks is None else ks) for j in range(3)]

    def chip(self, j):
        ox, oy = self.others[j]
        return 2 * ox + oy

    def start(self):
        for k in range(len(self.bufs)):
            for j in range(2):
                self._ici(k, j, self.me).start()

    def forward(self, pairs=None):
        for k, j in self._each() if pairs is None else pairs:
            self._ici(k, j, self.chip(j)).wait_recv()
            self._pair(k, j, self.c).start()
            if j < 2:
                pl.when(self.c == j)(self._relay(k).start)

    def landed(self, pairs):
        for k, j in pairs:
            self._pair(k, j, 1 - self.c).wait_recv()

    def finish(self, ks=None):
        self.landed(self._each(ks))
        for k in range(len(self.bufs)):
            for j in range(2):
                self._ici(k, j, self.me).wait_send()
            self._relay(k).wait_send()
            for j in range(3):
                self._pair(k, j, self.c).wait_send()

    @staticmethod
    def scratch(n):
        return [pltpu.SemaphoreType.DMA((6 * n,)), pltpu.SemaphoreType.DMA((6 * n,))]


def _carried(bufs):
    n = len(bufs)
    return dict(in_specs=[ANY] * n, out_specs=[ANY] * n,
                out_shape=[jax.ShapeDtypeStruct(b.shape, b.dtype) for b in bufs], scratch=_Gather.scratch(n))


def _mix_in(x, g_mix, order, carry):
    s = x.shape[0]
    tm = min(2 * ROW_TILE, s)
    n = s // tm
    nc = len(carry)
    cs = _carried(carry)

    def body(order_ref, x_ref, g_ref, *refs):
        z_ref = refs[nc]
        bufs = refs[nc + 1:2 * nc + 1]
        h_ref, w_ref, w_sem = refs[2 * nc + 1:2 * nc + 4]
        gather = _Gather(bufs, *refs[2 * nc + 4:])
        q, i = pl.program_id(0), pl.program_id(1)
        first = i == 0
        pl.when(jnp.logical_and(q == 0, first))(gather.start)
        for j in range(3):

            @pl.when(jnp.logical_and(q == j + 1, first))
            def _():
                gather.forward([(0, j)])
                gather.landed([(0, j)])
                if j == 1:
                    gather.forward([(k, jj) for k in range(1, nc) for jj in range(2)])

        @pl.when(first)
        def _():
            load = pltpu.make_async_copy(bufs[0].at[order_ref[q]], w_ref, w_sem)
            load.start()
            load.wait()

        @pl.when(q == 0)
        def _():
            h, _, _ = _rms_fwd(x_ref[...], g_ref[...])
            h_ref[i] = h.astype(bf16)

        z_ref[...] = _dot(h_ref[i], w_ref[...])

        @pl.when(jnp.logical_and(q == N_CHIPS - 1, i == n - 1))
        def _():
            gather.forward([(k, 2) for k in range(1, nc)])
            gather.finish(range(1, nc))

    res = pl.pallas_call(
        body, name="mix_in",
        grid_spec=pltpu.PrefetchScalarGridSpec(
            num_scalar_prefetch=1, grid=(N_CHIPS, n),
            in_specs=[pl.BlockSpec((tm, D_MODEL), lambda q, i, order_ref: (jnp.where(q == 0, i, 0), 0)),
                      pl.BlockSpec((1, D_MODEL), lambda q, i, order_ref: (0, 0))] + cs["in_specs"],
            out_specs=[pl.BlockSpec((tm, W_IN_CHUNK), lambda q, i, order_ref: (i, order_ref[q]))] + cs["out_specs"],
            scratch_shapes=[pltpu.VMEM((n, tm, D_MODEL), bf16), pltpu.VMEM((D_MODEL, W_IN_CHUNK), bf16),
                            pltpu.SemaphoreType.DMA(())] + cs["scratch"]),
        out_shape=[jax.ShapeDtypeStruct((s, W_IN_COLS), f32)] + cs["out_shape"],
        input_output_aliases={3 + k: 1 + k for k in range(nc)},
        compiler_params=_cp("arbitrary", "arbitrary"),
    )(order, x, g_mix, *carry)
    return res[0], res[1:]


def _pool_counts(tm, w, first_row):
    t1 = (lax.broadcasted_iota(jnp.int32, (tm, 1), 0) + first_row + 1).astype(f32)
    return jnp.minimum(t1, float(w))


def _conv_pool_out(z, x, conv_w, conv_b, ln_g, ln_b, pool_w, pool_scale, w_out, g_ffn, carry):
    s = x.shape[0]
    tm = min(ROW_TILE, s)
    n = s // tm
    hb = tm // HALO
    nv = tm // 8
    assert nv >= SHIFT_PAD and nv % SHIFT_GROUPS == 0
    nc = len(carry)
    cs = _carried(carry)

    def body(z_ref, zp_ref, x_ref, cw_ref, cb_ref, lg_ref, lb_ref, pw_ref, ps_ref, wo_ref, gf_ref, *refs):
        x1_ref, mix_ref, u1_ref, pooled_ref, h2_ref = refs[nc:nc + 5]
        eu, ev, ss, u0_ref = refs[2 * nc + 5:2 * nc + 9]
        gather = _Gather(refs[nc + 5:2 * nc + 5], *refs[2 * nc + 9:])
        i = pl.program_id(0)
        pl.when(i == 0)(gather.start)
        for k in range(nc):
            pl.when(i == min(n // 2 + 2 * k, n - 1))(functools.partial(gather.forward, [(k, 0), (k, 1)]))
        keep = (i > 0).astype(f32)
        zp = zp_ref[...] * keep
        u0_prev = zp[:, :C_CONV] * _sigmoid(zp[:, C_CONV:2 * C_CONV])
        u0_ref[...] = z_ref[:, :C_CONV] * _sigmoid(z_ref[:, C_CONV:2 * C_CONV])
        for c, w_pool in enumerate(POOL_WINDOWS):
            lanes = slice(c * 128, (c + 1) * 128)
            v_lanes = slice(2 * C_CONV + c * 128, 2 * C_CONV + (c + 1) * 128)
            eu[c, pl.ds(0, SHIFT_PAD, stride=8), :] = u0_prev[:, lanes]
            ev[c, pl.ds(0, SHIFT_PAD, stride=8), :] = zp[:, v_lanes]
            for j in range(8):
                rows = slice(j * nv, (j + 1) * nv)
                eu[c, pl.ds(SHIFT_PAD * 8 + j, nv, stride=8), :] = u0_ref[rows, lanes]
                ev[c, pl.ds(SHIFT_PAD * 8 + j, nv, stride=8), :] = z_ref[rows, v_lanes]
                if j >= 1:
                    edge = slice(j * nv - SHIFT_PAD, j * nv)
                    eu[c, pl.ds(j, SHIFT_PAD, stride=8), :] = u0_ref[edge, lanes]
                    ev[c, pl.ds(j, SHIFT_PAD, stride=8), :] = z_ref[edge, v_lanes]
            for v0 in range(0, nv, SHIFT_GROUPS):
                span = SHIFT_GROUPS * 8
                acc = jnp.zeros((span, 128), f32) + cb_ref[:, lanes]
                for k in range(CONV_K):
                    acc = acc + cw_ref[pl.ds(k, 1), lanes] * eu[c, pl.ds((SHIFT_PAD - (CONV_K - 1) + v0 + k) * 8, span), :]
                ss[0, v0 * 8:v0 * 8 + span, :] = acc
                acc = ev[c, pl.ds((SHIFT_PAD + v0) * 8, span), :]
                for d in range(1, w_pool):
                    acc = acc + ev[c, pl.ds((SHIFT_PAD + v0 - d) * 8, span), :]
                ss[1, v0 * 8:v0 * 8 + span, :] = acc
            for j in range(8):
                rows = slice(j * nv, (j + 1) * nv)
                u1_ref[rows, lanes] = ss[0, pl.ds(j, nv, stride=8), :]
                mean = ss[1, pl.ds(j, nv, stride=8), :] / _pool_counts(nv, w_pool, i * tm + j * nv)
                pooled_ref[rows, lanes] = (mean - z_ref[rows, v_lanes]).astype(bf16)
        u1 = u1_ref[...]
        mu = jnp.mean(u1, axis=-1, keepdims=True)
        uc = u1 - mu
        rstd = lax.rsqrt(jnp.mean(uc * uc, axis=-1, keepdims=True) + EPS)
        u2 = uc * rstd * lg_ref[...] + lb_ref[...]
        mix_ref[:, :C_CONV] = (u2 * _sigmoid(u2)).astype(bf16)
        for g in range(len(POOL_WINDOWS)):
            cols = slice(g * POOL_GROUP, (g + 1) * POOL_GROUP)
            mixed = _dot(pooled_ref[:, cols], pw_ref[g].astype(bf16))
            mix_ref[:, C_CONV + g * POOL_GROUP:C_CONV + (g + 1) * POOL_GROUP] = (mixed * ps_ref[:, cols]).astype(bf16)
        x1 = x_ref[...] + _dot(mix_ref[...], wo_ref[...])
        x1_ref[...] = x1
        h2_ref[...] = _rms_fwd(x1, gf_ref[...])[0].astype(bf16)
        @pl.when(i == n - 1)
        def _():
            gather.forward([(k, 2) for k in range(nc)])
            gather.finish()

    res = pl.pallas_call(
        body, name="conv_pool_out", grid=(n,),
        in_specs=[_row(0, (tm, W_IN_COLS)),
                  pl.BlockSpec((HALO, W_IN_COLS), lambda i: (jnp.maximum(i * hb - 1, 0), 0)),
                  _row(0, (tm, D_MODEL)), _full((HALO, C_CONV)), _full((1, C_CONV)), _full((1, C_CONV)),
                  _full((1, C_CONV)), _full((4, POOL_GROUP, POOL_GROUP)), _full((1, C_POOL)),
                  _full((D_MODEL, D_MODEL)), _full((1, D_MODEL))] + cs["in_specs"],
        out_specs=[_row(0, (tm, D_MODEL)), _row(0, (tm, D_MODEL)), _row(0, (tm, C_CONV)), _row(0, (tm, C_POOL)),
                   _row(0, (tm, D_MODEL))] + cs["out_specs"],
        out_shape=[jax.ShapeDtypeStruct((s, D_MODEL), f32), jax.ShapeDtypeStruct((s, D_MODEL), bf16),
                   jax.ShapeDtypeStruct((s, C_CONV), f32), jax.ShapeDtypeStruct((s, C_POOL), bf16),
                   jax.ShapeDtypeStruct((s, D_MODEL), bf16)] + cs["out_shape"],
        input_output_aliases={11 + k: 5 + k for k in range(nc)},
        scratch_shapes=[pltpu.VMEM((4, (SHIFT_PAD + nv) * 8, 128), f32), pltpu.VMEM((4, (SHIFT_PAD + nv) * 8, 128), f32),
                        pltpu.VMEM((2, tm, 128), f32), pltpu.VMEM((tm, C_CONV), f32)] + cs["scratch"],
        compiler_params=_cp("arbitrary"),
    )(z, z, x, conv_w, conv_b, ln_g, ln_b, pool_w, pool_scale, w_out, g_ffn, *carry)
    return res[:5], res[5:]


def _ffn_up(h2, w_gu_g, carry):
    s = h2.shape[0]
    tm = min(ROW_TILE, s)
    n = s // tm
    nc = len(carry)
    cs = _carried(carry)

    def body(h2_ref, wg_ref, wu_ref, *refs):
        gu_ref, f_ref = refs[nc:nc + 2]
        gather = _Gather(refs[nc + 2:2 * nc + 2], *refs[2 * nc + 2:])
        c, i = pl.program_id(0), pl.program_id(1)
        pl.when(jnp.logical_and(i == 0, c == 0))(gather.start)
        direct = [(k, j) for k in range(nc) for j in range(2)]
        pl.when(jnp.logical_and(i == n - 1, c == 0))(functools.partial(gather.forward, direct))
        pl.when(jnp.logical_and(i == n // 2, c == 1))(functools.partial(gather.forward, [(k, 2) for k in range(nc)]))
        h = h2_ref[...]
        for lo, hi in zip(FF_SUB[:-1], FF_SUB[1:]):
            gate = _dot(h, wg_ref[0, :, lo:hi])
            up = _dot(h, wu_ref[0, :, lo:hi])
            gu_ref[0, :, lo:hi] = gate.astype(bf16)
            gu_ref[1, :, lo:hi] = up.astype(bf16)
            f_ref[:, lo:hi] = (gate * _sigmoid(gate) * up).astype(bf16)
        pl.when(jnp.logical_and(i == n - 1, c == 1))(gather.finish)

    res = pl.pallas_call(
        body, name="ffn_up", grid=(2, n),
        in_specs=[_row(1, (tm, D_MODEL)),
                  pl.BlockSpec((1, D_MODEL, FF_CHUNK), lambda c, i: (c, 0, 0)),
                  pl.BlockSpec((1, D_MODEL, FF_CHUNK), lambda c, i: (2 + c, 0, 0))] + cs["in_specs"],
        out_specs=[pl.BlockSpec((2, tm, FF_CHUNK), lambda c, i: (0, i, c)),
                   pl.BlockSpec((tm, FF_CHUNK), lambda c, i: (i, c))] + cs["out_specs"],
        out_shape=[jax.ShapeDtypeStruct((2, s, D_FF), bf16), jax.ShapeDtypeStruct((s, D_FF), bf16)] + cs["out_shape"],
        input_output_aliases={3 + k: 2 + k for k in range(nc)},
        scratch_shapes=cs["scratch"],
        compiler_params=_cp("arbitrary", "arbitrary"),
    )(h2, w_gu_g, w_gu_g, *carry)
    return res[:2], res[2:]


def _ffn_down(x1, f, w_down):
    s = x1.shape[0]
    tm = min(ROW_TILE, s)

    def body(x1_ref, f_ref, wd_ref, x2_ref):
        x2_ref[...] = x1_ref[...] + _dot(f_ref[...], wd_ref[...])

    return pl.pallas_call(
        body, name="ffn_down", grid=(s // tm,),
        in_specs=[_row(0, (tm, D_MODEL)), _row(0, (tm, D_FF)), _full((D_FF, D_MODEL))],
        out_specs=_row(0, (tm, D_MODEL)), out_shape=jax.ShapeDtypeStruct((s, D_MODEL), f32),
        compiler_params=_cp("parallel"),
    )(x1, f, w_down)


def _ple_loss(x2, p, target, g_pg, g_post, g_final, w_pg, w_pu_g):
    s = x2.shape[0]
    tm = min(ROW_TILE, s)
    n = s // tm

    def body(x2_ref, p_ref, t_ref, gpg_ref, gpo_ref, gf_ref, wpg_ref, wpu_ref,
             dx2_ref, dwpg_ref, dwpu_ref, small_ref, apg_ref, apu_ref):
        i = pl.program_id(0)

        @pl.when(i == 0)
        def _():
            apg_ref[...] = jnp.zeros_like(apg_ref)
            apu_ref[...] = jnp.zeros_like(apu_ref)
            small_ref[...] = jnp.zeros_like(small_ref)

        x2 = x2_ref[...]
        h3, x2h, r2 = _rms_fwd(x2, gpg_ref[...])
        h3b = h3.astype(bf16)
        gate = _sigmoid(_dot(h3b, wpg_ref[...]))
        pb = p_ref[...].astype(bf16)
        pe = jnp.concatenate([_dot(pb, wpu_ref[j]) for j in range(N_CHIPS)], axis=-1)
        e, peh, rp = _rms_fwd(pe, gpo_ref[...])
        x3 = x2 + gate * e
        y, x3h, r3 = _rms_fwd(x3, gf_ref[...])
        d = y - t_ref[...]
        loss = 0.5 * jnp.sum(jnp.sum(d * d, axis=-1, keepdims=True) * (1.0 / D_MODEL), axis=0, keepdims=True)
        dx3, dgf = _rms_bwd(d * (1.0 / D_MODEL), x3h, r3, gf_ref[...])
        dpe, dgpo = _rms_bwd(dx3 * gate, peh, rp, gpo_ref[...])
        dgl = (dx3 * e * gate * (1.0 - gate)).astype(bf16)
        apg_ref[...] += _dot_tn(h3b, dgl)
        apu_ref[...] += _dot_tn(pb, dpe.astype(bf16))
        dh3 = _dot_nt(dgl, wpg_ref[...])
        dx2b, dgpg = _rms_bwd(dh3, x2h, r2, gpg_ref[...])
        dx2_ref[...] = dx3 + dx2b
        small_ref[0:1, :] += dgpg
        small_ref[1:2, :] += dgpo
        small_ref[2:3, :] += dgf
        small_ref[3:4, :] += jnp.broadcast_to(loss, (1, D_MODEL))

        @pl.when(i == n - 1)
        def _():
            dwpg_ref[...] = apg_ref[...].astype(bf16)
            for j in range(N_CHIPS):
                dwpu_ref[j] = apu_ref[:, j * PLE_CHUNK:(j + 1) * PLE_CHUNK].astype(bf16)

    return pl.pallas_call(
        body, name="ple_loss", grid=(n,),
        in_specs=[_row(0, (tm, D_MODEL)), _row(0, (tm, D_PLE)), _row(0, (tm, D_MODEL)),
                  _full((1, D_MODEL)), _full((1, D_MODEL)), _full((1, D_MODEL)),
                  _full((D_MODEL, D_MODEL)), _full((N_CHIPS, D_PLE, PLE_CHUNK))],
        out_specs=[_row(0, (tm, D_MODEL)), _full((D_MODEL, D_MODEL)), _full((N_CHIPS, D_PLE, PLE_CHUNK)),
                   _full((8, D_MODEL))],
        out_shape=[jax.ShapeDtypeStruct((s, D_MODEL), f32), jax.ShapeDtypeStruct((D_MODEL, D_MODEL), bf16),
                   jax.ShapeDtypeStruct((N_CHIPS, D_PLE, PLE_CHUNK), bf16), jax.ShapeDtypeStruct((8, D_MODEL), f32)],
        scratch_shapes=[pltpu.VMEM((D_MODEL, D_MODEL), f32), pltpu.VMEM((D_PLE, D_MODEL), f32)],
        compiler_params=_cp("arbitrary"),
    )(x2, p, target, g_pg, g_post, g_final, w_pg, w_pu_g)


def _crossed(parts):
    n = len(parts)
    return dict(in_specs=[ANY] * n, out_specs=[ANY] * n,
                out_shape=[jax.ShapeDtypeStruct(a.shape, a.dtype) for a in parts], scratch=_Cross.scratch(n))


def _ffn_bwd_dx(dx2, x1, gu, g_ffn, w_gu_g, w_down, grads):
    s = x1.shape[0]
    tm = min(ROW_TILE, s)
    n = s // tm
    nc = len(grads)
    cs = dict(in_specs=[ANY] * nc, out_specs=[ANY] * nc, scratch=_CrossAll.scratch(nc),
              out_shape=[jax.ShapeDtypeStruct((N_DEV, g.shape[1] // 2, g.shape[2]), g.dtype) for g in grads])

    def body(dx2_ref, x1_ref, gu_ref, g_ref, wg_ref, wu_ref, wd_ref, *refs):
        dx1_ref, dgu_ref, small_ref = refs[nc:nc + 3]
        acc_ref = refs[2 * nc + 3]
        cross = _CrossAll(refs[:nc], refs[nc + 3:2 * nc + 3], *refs[2 * nc + 4:])
        i, c = pl.program_id(0), pl.program_id(1)
        pl.when(jnp.logical_and(i == 0, c == 0))(cross.start)

        @pl.when(jnp.logical_and(i == 0, c == 0))
        def _():
            small_ref[...] = jnp.zeros_like(small_ref)

        @pl.when(c == 0)
        def _():
            acc_ref[...] = jnp.zeros_like(acc_ref)

        dyb = dx2_ref[...].astype(bf16)
        for lo, hi in zip(FF_SUB[:-1], FF_SUB[1:]):
            df = _dot_nt(dyb, wd_ref[lo:hi, :])
            gate = gu_ref[0, :, lo:hi].astype(f32)
            up = gu_ref[1, :, lo:hi].astype(f32)
            sg = _sigmoid(gate)
            dgate = (df * up * _silu_grad(gate, sg)).astype(bf16)
            dup = (df * gate * sg).astype(bf16)
            dgu_ref[0, :, lo:hi] = dgate
            dgu_ref[1, :, lo:hi] = dup
            acc_ref[...] += _dot_nt(dgate, wg_ref[0, :, lo:hi]) + _dot_nt(dup, wu_ref[0, :, lo:hi])

        @pl.when(c == 1)
        def _():
            _, x1h, r1 = _rms_fwd(x1_ref[...], g_ref[...])
            dx1b, dg = _rms_bwd(acc_ref[...], x1h, r1, g_ref[...])
            dx1_ref[...] = dx2_ref[...] + dx1b
            small_ref[0:1, :] += dg

        pl.when(jnp.logical_and(i == n - 1, c == 1))(cross.finish)

    res = pl.pallas_call(
        body, name="ffn_bwd_dx", grid=(n, 2),
        in_specs=[_row(0, (tm, D_MODEL)), _row(0, (tm, D_MODEL)),
                  pl.BlockSpec((2, tm, FF_CHUNK), lambda i, c: (0, i, c)), _full((1, D_MODEL)),
                  pl.BlockSpec((1, D_MODEL, FF_CHUNK), lambda i, c: (c, 0, 0)),
                  pl.BlockSpec((1, D_MODEL, FF_CHUNK), lambda i, c: (2 + c, 0, 0)),
                  pl.BlockSpec((FF_CHUNK, D_MODEL), lambda i, c: (c, 0))] + cs["in_specs"],
        out_specs=[_row(0, (tm, D_MODEL)), pl.BlockSpec((2, tm, FF_CHUNK), lambda i, c: (0, i, c)),
                   _full((8, D_MODEL))] + cs["out_specs"],
        out_shape=[jax.ShapeDtypeStruct((s, D_MODEL), f32), jax.ShapeDtypeStruct((2, s, D_FF), bf16),
                   jax.ShapeDtypeStruct((8, D_MODEL), f32)] + cs["out_shape"],
        scratch_shapes=[pltpu.VMEM((tm, D_MODEL), f32)] + cs["scratch"],
        compiler_params=_cp("arbitrary", "arbitrary"),
    )(dx2, x1, gu, g_ffn, w_gu_g, w_gu_g, w_down, *grads)
    return res[:3], res[3:]


def _ffn_bwd_dw_gu(h2, dgu):
    s = h2.shape[0]
    ts = min(2 * ROW_TILE, s)
    n = s // ts

    def body(h_ref, d_ref, o_ref, acc_ref):
        t = pl.program_id(1)

        @pl.when(t == 0)
        def _():
            acc_ref[...] = jnp.zeros_like(acc_ref)

        acc_ref[...] += _dot_tn(h_ref[...], d_ref[0])

        @pl.when(t == n - 1)
        def _():
            o_ref[0] = acc_ref[...].astype(bf16)

    return pl.pallas_call(
        body, name="ffn_bwd_dw_gu", grid=(N_CHIPS, n),
        in_specs=[pl.BlockSpec((ts, D_MODEL), lambda j, t: (t, 0)),
                  pl.BlockSpec((1, ts, FF_CHUNK), lambda j, t: (j // 2, t, j % 2))],
        out_specs=pl.BlockSpec((1, D_MODEL, FF_CHUNK), lambda j, t: (j, 0, 0)),
        out_shape=jax.ShapeDtypeStruct((N_CHIPS, D_MODEL, FF_CHUNK), bf16),
        scratch_shapes=[pltpu.VMEM((D_MODEL, FF_CHUNK), f32)],
        compiler_params=_cp("parallel", "arbitrary"),
    )(h2, dgu)


def _ffn_bwd_dw_down(f, dx2):
    s = dx2.shape[0]
    ts = min(2 * ROW_TILE, s)
    n = s // ts

    def body(f_ref, d_ref, o_ref, acc_ref):
        t = pl.program_id(1)

        @pl.when(t == 0)
        def _():
            acc_ref[...] = jnp.zeros_like(acc_ref)

        acc_ref[...] += _dot_tn(f_ref[...], d_ref[...].astype(bf16))

        @pl.when(t == n - 1)
        def _():
            o_ref[...] = acc_ref[...].astype(bf16)

    return pl.pallas_call(
        body, name="ffn_bwd_dw_down", grid=(2, n),
        in_specs=[pl.BlockSpec((ts, FF_CHUNK), lambda c, t: (t, c)),
                  pl.BlockSpec((ts, D_MODEL), lambda c, t: (t, 0))],
        out_specs=pl.BlockSpec((FF_CHUNK, D_MODEL), lambda c, t: (c, 0)),
        out_shape=jax.ShapeDtypeStruct((D_FF, D_MODEL), bf16),
        scratch_shapes=[pltpu.VMEM((FF_CHUNK, D_MODEL), f32)],
        compiler_params=_cp("parallel", "arbitrary"),
    )(f, dx2)


def _mix_bwd_local(dx1, mix, u1, pooled, w_out, ln_g, ln_b, pool_w, pool_scale):
    s = dx1.shape[0]
    tm = min(ROW_TILE, s)
    n = s // tm

    def body(dx1_ref, mix_ref, u1_ref, po_ref, wo_ref, lg_ref, lb_ref, pw_ref, ps_ref,
             du1_ref, dpo_ref, dwo_ref, dpw_ref, small_ref, awo_ref):
        i = pl.program_id(0)

        @pl.when(i == 0)
        def _():
            awo_ref[...] = jnp.zeros_like(awo_ref)
            dpw_ref[...] = jnp.zeros_like(dpw_ref)
            small_ref[...] = jnp.zeros_like(small_ref)

        dyb = dx1_ref[...].astype(bf16)
        dmix = _dot_nt(dyb, wo_ref[...])
        awo_ref[...] += _dot_tn(mix_ref[...], dyb)
        u1 = u1_ref[...]
        mu = jnp.mean(u1, axis=-1, keepdims=True)
        uc = u1 - mu
        rstd = lax.rsqrt(jnp.mean(uc * uc, axis=-1, keepdims=True) + EPS)
        uh = uc * rstd
        u2 = uh * lg_ref[...] + lb_ref[...]
        du2 = dmix[:, :C_CONV] * _silu_grad(u2, _sigmoid(u2))
        duh = du2 * lg_ref[...]
        du1 = rstd * (duh - jnp.mean(duh, axis=-1, keepdims=True) - uh * jnp.mean(duh * uh, axis=-1, keepdims=True))
        du1_ref[...] = du1
        small_ref[0:1, :] += jnp.sum(du2 * uh, axis=0, keepdims=True)
        small_ref[1:2, :] += jnp.sum(du2, axis=0, keepdims=True)
        small_ref[2:3, :] += jnp.sum(du1, axis=0, keepdims=True)
        for g in range(len(POOL_WINDOWS)):
            cols = slice(g * POOL_GROUP, (g + 1) * POOL_GROUP)
            dq = dmix[:, C_CONV + g * POOL_GROUP:C_CONV + (g + 1) * POOL_GROUP]
            pwb = pw_ref[g].astype(bf16)
            pg = po_ref[:, cols]
            mixed = _dot(pg, pwb)
            small_ref[3:4, cols] += jnp.sum(dq * mixed, axis=0, keepdims=True)
            dmixed = (dq * ps_ref[:, cols]).astype(bf16)
            dpw_ref[g] += _dot_tn(pg, dmixed)
            dpo_ref[:, cols] = _dot_nt(dmixed, pwb)

        @pl.when(i == n - 1)
        def _():
            dwo_ref[...] = awo_ref[...].astype(bf16)

    return pl.pallas_call(
        body, name="mix_bwd_local", grid=(n,),
        in_specs=[_row(0, (tm, D_MODEL)), _row(0, (tm, D_MODEL)), _row(0, (tm, C_CONV)), _row(0, (tm, C_POOL)),
                  _full((D_MODEL, D_MODEL)), _full((1, C_CONV)), _full((1, C_CONV)),
                  _full((4, POOL_GROUP, POOL_GROUP)), _full((1, C_POOL))],
        out_specs=[_row(0, (tm, C_CONV)), _row(0, (tm, C_POOL)), _full((D_MODEL, D_MODEL)),
                   _full((4, POOL_GROUP, POOL_GROUP)), _full((8, C_CONV))],
        out_shape=[jax.ShapeDtypeStruct((s, C_CONV), f32), jax.ShapeDtypeStruct((s, C_POOL), f32),
                   jax.ShapeDtypeStruct((D_MODEL, D_MODEL), bf16),
                   jax.ShapeDtypeStruct((4, POOL_GROUP, POOL_GROUP), f32), jax.ShapeDtypeStruct((8, C_CONV), f32)],
        scratch_shapes=[pltpu.VMEM((D_MODEL, D_MODEL), f32)],
        compiler_params=_cp("arbitrary"),
    )(dx1, mix, u1, pooled, w_out, ln_g, ln_b, pool_w, pool_scale)


def _in_bwd(du1, dpo, z, x, dx1, conv_w, g_mix, w_in_g, parts, small_parts):
    s = x.shape[0]
    tm = min(ROW_TILE, s)
    n = s // tm
    hb = tm // HALO
    last = s // HALO - 1
    nv = tm // 8
    assert nv >= SHIFT_PAD and nv % SHIFT_GROUPS == 0
    nc, ns = len(parts), len(small_parts)
    cs = _crossed(parts)

    def body(du_ref, dun_ref, dp_ref, dpn_ref, z_ref, zp_ref, x_ref, dx1_ref, cw_ref, g_ref, w_ref, *refs):
        outs = refs[nc + ns:]
        gx_ref, dw_ref, dcw_ref, small_ref = outs[:4]
        eu, ed, ep, ss, u0_ref, dz_ref, acc_ref, dcw_acc = outs[4 + nc + ns:12 + nc + ns]
        sems = outs[12 + nc + ns:]
        cross = _Cross(refs[:nc], outs[4:4 + nc], sems[0], sems[1])
        gather = _SmallGather(refs[nc:nc + ns], outs[4 + nc:4 + nc + ns], sems[2], sems[3])
        i = pl.program_id(0)
        pl.when(i == 0)(cross.start)
        pl.when(i == 0)(gather.start)

        @pl.when(i == 0)
        def _():
            acc_ref[...] = jnp.zeros_like(acc_ref)
            dcw_acc[...] = jnp.zeros_like(dcw_acc)
            small_ref[...] = jnp.zeros_like(small_ref)

        keep_prev = (i > 0).astype(f32)
        keep_next = (i < n - 1).astype(f32)
        zp = zp_ref[...] * keep_prev
        u0_prev = zp[:, :C_CONV] * _sigmoid(zp[:, C_CONV:2 * C_CONV])
        u0_ref[...] = z_ref[:, :C_CONV] * _sigmoid(z_ref[:, C_CONV:2 * C_CONV])
        du_next = dun_ref[...] * keep_next
        for c, w_pool in enumerate(POOL_WINDOWS):
            lanes = slice(c * 128, (c + 1) * 128)
            eu[c, pl.ds(0, SHIFT_PAD, stride=8), :] = u0_prev[:, lanes]
            ed[c, pl.ds(nv * 8 + 7, SHIFT_PAD, stride=8), :] = du_next[:, lanes]
            ep[c, pl.ds(nv * 8 + 7, SHIFT_PAD, stride=8), :] = (
                dpn_ref[:, lanes] * keep_next / _pool_counts(HALO, w_pool, (i + 1) * tm))
            for j in range(8):
                rows = slice(j * nv, (j + 1) * nv)
                eu[c, pl.ds(SHIFT_PAD * 8 + j, nv, stride=8), :] = u0_ref[rows, lanes]
                ed[c, pl.ds(j, nv, stride=8), :] = du_ref[rows, lanes]
                ep[c, pl.ds(j, nv, stride=8), :] = dp_ref[rows, lanes] / _pool_counts(nv, w_pool, i * tm + j * nv)
                if j >= 1:
                    eu[c, pl.ds(j, SHIFT_PAD, stride=8), :] = u0_ref[j * nv - SHIFT_PAD:j * nv, lanes]
                if j <= 6:
                    edge = slice((j + 1) * nv, (j + 1) * nv + SHIFT_PAD)
                    ed[c, pl.ds(nv * 8 + j, SHIFT_PAD, stride=8), :] = du_ref[edge, lanes]
                    ep[c, pl.ds(nv * 8 + j, SHIFT_PAD, stride=8), :] = (
                        dp_ref[edge, lanes] / _pool_counts(SHIFT_PAD, w_pool, i * tm + (j + 1) * nv))
        for c, w_pool in enumerate(POOL_WINDOWS):
            lanes = slice(c * 128, (c + 1) * 128)
            b_lanes = slice(C_CONV + c * 128, C_CONV + (c + 1) * 128)
            v_lanes = slice(2 * C_CONV + c * 128, 2 * C_CONV + (c + 1) * 128)
            for v0 in range(0, nv, SHIFT_GROUPS):
                span = SHIFT_GROUPS * 8
                acc = jnp.zeros((span, 128), f32)
                for k in range(CONV_K):
                    acc = acc + cw_ref[pl.ds(k, 1), lanes] * ed[c, pl.ds((v0 + CONV_K - 1 - k) * 8, span), :]
                ss[0, v0 * 8:v0 * 8 + span, :] = acc
                acc = ep[c, pl.ds(v0 * 8, span), :]
                for d in range(1, w_pool):
                    acc = acc + ep[c, pl.ds((v0 + d) * 8, span), :]
                ss[1, v0 * 8:v0 * 8 + span, :] = acc
                d1 = ed[c, pl.ds(v0 * 8, span), :]
                for k in range(CONV_K):
                    prod = d1 * eu[c, pl.ds((SHIFT_PAD - (CONV_K - 1) + v0 + k) * 8, span), :]
                    fold = prod[0:8]
                    for r in range(8, span, 8):
                        fold = fold + prod[r:r + 8]
                    dcw_acc[k, :, lanes] += fold
            for j in range(8):
                rows = slice(j * nv, (j + 1) * nv)
                du0 = ss[0, pl.ds(j, nv, stride=8), :]
                av, sv = z_ref[rows, lanes], _sigmoid(z_ref[rows, b_lanes])
                dz_ref[rows, lanes] = (du0 * sv).astype(bf16)
                dz_ref[rows, b_lanes] = (du0 * av * sv * (1.0 - sv)).astype(bf16)
                dz_ref[rows, v_lanes] = (ss[1, pl.ds(j, nv, stride=8), :] - dp_ref[rows, lanes]).astype(bf16)
        h, xh, r = _rms_fwd(x_ref[...], g_ref[...])
        dz = dz_ref[...]
        acc_ref[...] += _dot_tn(h.astype(bf16), dz)
        dh = _dot_nt(dz[:, 0:W_IN_CHUNK], w_ref[0])
        for j in range(1, N_CHIPS):
            dh = dh + _dot_nt(dz[:, j * W_IN_CHUNK:(j + 1) * W_IN_CHUNK], w_ref[j])
        dxb, dg = _rms_bwd(dh, xh, r, g_ref[...])
        gx_ref[...] = dx1_ref[...] + dxb
        small_ref[0:1, :] += dg

        @pl.when(i == n - 1)
        def _():
            for j in range(N_CHIPS):
                dw_ref[j] = acc_ref[:, j * W_IN_CHUNK:(j + 1) * W_IN_CHUNK].astype(bf16)
            dcw_ref[...] = jnp.sum(dcw_acc[...], axis=1)

        pl.when(i == n - 1)(gather.finish)
        pl.when(i == n - 1)(cross.finish)

    nxt = lambda i: (jnp.minimum((i + 1) * hb, last), 0)
    res = pl.pallas_call(
        body, name="in_bwd", grid=(n,),
        in_specs=[_row(0, (tm, C_CONV)), pl.BlockSpec((HALO, C_CONV), nxt),
                  _row(0, (tm, C_POOL)), pl.BlockSpec((HALO, C_POOL), nxt),
                  _row(0, (tm, W_IN_COLS)),
                  pl.BlockSpec((HALO, W_IN_COLS), lambda i: (jnp.maximum(i * hb - 1, 0), 0)),
                  _row(0, (tm, D_MODEL)), _row(0, (tm, D_MODEL)), _full((HALO, C_CONV)), _full((1, D_MODEL)),
                  _full((N_CHIPS, D_MODEL, W_IN_CHUNK))] + cs["in_specs"] + [ANY] * ns,
        out_specs=[_row(0, (tm, D_MODEL)), _full((N_CHIPS, D_MODEL, W_IN_CHUNK)), _full((HALO, C_CONV)),
                   _full((8, D_MODEL))] + cs["out_specs"] + [ANY] * ns,
        out_shape=[jax.ShapeDtypeStruct((s, D_MODEL), f32), jax.ShapeDtypeStruct((N_CHIPS, D_MODEL, W_IN_CHUNK), bf16),
                   jax.ShapeDtypeStruct((HALO, C_CONV), f32), jax.ShapeDtypeStruct((8, D_MODEL), f32)] + cs["out_shape"]
        + _gathered_shapes(small_parts),
        scratch_shapes=[pltpu.VMEM((4, (SHIFT_PAD + nv) * 8, 128), f32), pltpu.VMEM((4, (nv + SHIFT_PAD) * 8, 128), f32),
                        pltpu.VMEM((4, (nv + SHIFT_PAD) * 8, 128), f32), pltpu.VMEM((2, tm, 128), f32),
                        pltpu.VMEM((tm, C_CONV), f32), pltpu.VMEM((tm, W_IN_COLS), bf16),
                        pltpu.VMEM((D_MODEL, W_IN_COLS), f32), pltpu.VMEM((HALO, 8, C_CONV), f32)] + cs["scratch"]
        + _SmallGather.scratch(ns),
        compiler_params=_cp("arbitrary"),
    )(du1, du1, dpo, dpo, z, z, x, dx1, conv_w, g_mix, w_in_g, *parts, *small_parts)
    return res[:4], res[4:4 + nc], res[4 + nc:]


def _pair_reduce(name, grads):
    nk = len(grads)
    halves = [g.shape[1] // 2 for g in grads]

    def body(*refs):
        ins, outs, got = refs[:nk], refs[nk:2 * nk], refs[2 * nk:3 * nk]
        send_sems, recv_sems = refs[3 * nk:]
        x, y, c, _, _ = _place()

        def half(k, core):
            return pl.ds(pl.multiple_of(core * halves[k], 16), halves[k])

        cps = [_remote(ins[k].at[:, half(k, 1 - c), :], got[k], send_sems.at[k], recv_sems.at[k], (x, y, 1 - c))
               for k in range(nk)]
        for cp in cps:
            cp.start()
        for k, cp in enumerate(cps):
            cp.wait_recv()
            outs[k][...] = (ins[k][:, half(k, c), :].astype(f32) + got[k][...].astype(f32)).astype(bf16)
        for cp in cps:
            cp.wait_send()

    shapes = [(N_CHIPS, h, g.shape[2]) for g, h in zip(grads, halves)]
    return pl.pallas_call(
        body, name=f"pair_reduce_{name}", in_specs=[VMEM] * nk, out_specs=[VMEM] * nk,
        out_shape=[jax.ShapeDtypeStruct(s, bf16) for s in shapes],
        scratch_shapes=[pltpu.VMEM(s, bf16) for s in shapes]
        + [pltpu.SemaphoreType.DMA((nk,)), pltpu.SemaphoreType.DMA((nk,))],
        compiler_params=pltpu.CompilerParams(vmem_limit_bytes=VMEM_LIMIT),
    )(*grads)


class _Cross:
    def __init__(self, parts, landed, send_sems, recv_sems):
        self.parts, self.landed, self.send_sems, self.recv_sems = parts, landed, send_sems, recv_sems
        _, _, self.c, self.me, self.others = _place()

    def _copy(self, k, j, src_chunk, dst_slot):
        ox, oy = self.others[j]
        return _remote(self.parts[k].at[src_chunk], self.landed[k].at[dst_slot], self.send_sems.at[3 * k + j],
                       self.recv_sems.at[3 * k + j], (ox, oy, self.c))

    def _each(self):
        return [(k, j, 2 * self.others[j][0] + self.others[j][1]) for k in range(len(self.parts)) for j in range(3)]

    def start(self):
        for k, j, chip in self._each():
            self._copy(k, j, chip, self.me).start()

    def finish(self):
        for k, j, chip in self._each():
            self._copy(k, j, chip, chip).wait_recv()
        for k, j, chip in self._each():
            self._copy(k, j, chip, self.me).wait_send()

    @staticmethod
    def scratch(n):
        return [pltpu.SemaphoreType.DMA((3 * n,)), pltpu.SemaphoreType.DMA((3 * n,))]


class _CrossAll:
    def __init__(self, grads, landed, send_sems, recv_sems):
        self.grads, self.landed, self.send_sems, self.recv_sems = grads, landed, send_sems, recv_sems
        self.x, self.y, self.c, self.me, self.others = _place()
        self.dev = 2 * self.me + self.c

    def _piece(self, k, chip, half):
        rows = self.grads[k].shape[1] // 2
        return self.grads[k].at[chip, pl.ds(half * rows, rows), :]

    def _to_sibling(self, k):
        return _remote(self._piece(k, self.me, 1 - self.c), self.landed[k].at[self.dev], self.send_sems.at[7 * k],
                       self.recv_sems.at[7 * k], (self.x, self.y, 1 - self.c))

    def _to_chip(self, k, j, half):
        ox, oy = self.others[j]
        return _remote(self._piece(k, 2 * ox + oy, half), self.landed[k].at[self.dev],
                       self.send_sems.at[7 * k + 1 + 2 * j + half], self.recv_sems.at[7 * k + 1 + 2 * j + self.c],
                       (ox, oy, half))

    def _from(self, k, sem, slot):
        return _remote(self.landed[k].at[slot], self.landed[k].at[slot], self.send_sems.at[7 * k + sem],
                       self.recv_sems.at[7 * k + sem], (self.x, self.y, 1 - self.c))

    def start(self):
        for k in range(len(self.grads)):
            self._to_sibling(k).start()
            for j in range(3):
                for half in range(2):
                    self._to_chip(k, j, half).start()

    def finish(self):
        for k in range(len(self.grads)):
            self._from(k, 0, 2 * self.me + 1 - self.c).wait_recv()
            for j, (ox, oy) in enumerate(self.others):
                for core in range(2):
                    self._from(k, 1 + 2 * j + core, 4 * ox + 2 * oy + core).wait_recv()
        for k in range(len(self.grads)):
            self._to_sibling(k).wait_send()
            for j in range(3):
                for half in range(2):
                    self._to_chip(k, j, half).wait_send()

    @staticmethod
    def scratch(n):
        return [pltpu.SemaphoreType.DMA((7 * n,)), pltpu.SemaphoreType.DMA((7 * n,))]


def _rs_sum_chips(name, place, landed, part):
    _, half, cols = landed.shape

    def body(place_ref, l_ref, p_ref, o_ref):
        me = place_ref[0]
        own = p_ref[0].astype(f32)
        acc = jnp.where(me == 0, own, l_ref[0].astype(f32))
        for j in range(1, N_CHIPS):
            acc = acc + jnp.where(me == j, own, l_ref[j].astype(f32))
        o_ref[...] = acc

    return pl.pallas_call(
        body, name=f"rs_sum_chips_{name}",
        grid_spec=pltpu.PrefetchScalarGridSpec(
            num_scalar_prefetch=1, grid=(1,),
            in_specs=[pl.BlockSpec((N_CHIPS, half, cols), lambda t, place_ref: (0, 0, 0)),
                      pl.BlockSpec((1, half, cols), lambda t, place_ref: (place_ref[0], 0, 0))],
            out_specs=pl.BlockSpec((half, cols), lambda t, place_ref: (place_ref[1], 0))),
        out_shape=jax.ShapeDtypeStruct((2 * half, cols), f32),
        compiler_params=_cp("arbitrary"),
    )(place, landed, part)


def _grad_pair(shards):
    nk = len(shards)

    def body(*refs):
        outs = refs[nk:2 * nk]
        send_sems, recv_sems = refs[2 * nk:]
        x, y, c, _, _ = _place()

        def half(k, core):
            h = outs[k].shape[0] // 2
            return outs[k].at[pl.ds(core * h, h), :]

        cps = [_remote(half(k, c), half(k, c), send_sems.at[k], recv_sems.at[k], (x, y, 1 - c)) for k in range(nk)]
        for cp in cps:
            cp.start()
        for k in range(nk):
            _remote(half(k, 1 - c), half(k, 1 - c), send_sems.at[k], recv_sems.at[k], (x, y, 1 - c)).wait_recv()
        for cp in cps:
            cp.wait_send()

    return pl.pallas_call(
        body, name="grad_pair", in_specs=[ANY] * nk, out_specs=[ANY] * nk,
        out_shape=[jax.ShapeDtypeStruct(a.shape, f32) for a in shards],
        input_output_aliases={k: k for k in range(nk)},
        scratch_shapes=[pltpu.SemaphoreType.DMA((nk,)), pltpu.SemaphoreType.DMA((nk,))],
    )(*shards)


def _adam_math(w, g, m, v):
    m = ADAM_B1 * m + (1.0 - ADAM_B1) * g
    v = ADAM_B2 * v + (1.0 - ADAM_B2) * (g * g)
    m_hat = m / (1.0 - ADAM_B1 ** ADAM_STEP)
    v_hat = v / (1.0 - ADAM_B2 ** ADAM_STEP)
    delta = -ADAM_LR * (m_hat / (jnp.sqrt(v_hat) + ADAM_EPS) + ADAM_WD * w)
    return delta, m, v


def _adam(ws, gs, ms, vs):
    n = len(ws)
    steps = 2 * SHARD_STEPS

    def body(*refs):
        for k in range(n):
            w_ref, g_ref, m_ref, v_ref = (refs[j * n + k] for j in range(4))
            go_ref, d_ref, nm_ref, nv_ref = (refs[(4 + j) * n + k] for j in range(4))
            g = g_ref[...]
            go_ref[...] = g
            d_ref[...], nm_ref[...], nv_ref[...] = _adam_math(w_ref[...], g, m_ref[...], v_ref[...])

    specs = [_row(0, (w.shape[0] // steps, w.shape[1])) for w in ws]
    res = pl.pallas_call(
        body, name="adam_shards", grid=(steps,), in_specs=specs * 4, out_specs=specs * 4,
        out_shape=[jax.ShapeDtypeStruct(w.shape, f32) for w in ws] * 4,
        compiler_params=_cp("parallel"),
    )(*ws, *gs, *ms, *vs)
    return [tuple(res[j * n + k] for j in range(4)) for k in range(n)]


class _SmallGather:
    def __init__(self, parts, bufs, send_sems, recv_sems):
        self.parts, self.bufs, self.send_sems, self.recv_sems = parts, bufs, send_sems, recv_sems
        self.x, self.y, self.c, _, self.others = _place()
        self.sibling = (self.x, self.y, 1 - self.c)

    def _copy(self, a, k, block, to, src=None):
        slot = self.bufs[a].at[4 * block[0] + 2 * block[1] + block[2]]
        return _remote(slot if src is None else src, slot, self.send_sems.at[7 * a + k], self.recv_sems.at[7 * a + k],
                       to)

    def _first(self, a):
        me = (self.x, self.y, self.c)
        return [self._copy(a, 0, me, self.sibling, src=self.parts[a])] + [
            self._copy(a, 1 + j, me, (*chip, self.c), src=self.parts[a]) for j, chip in enumerate(self.others)]

    def _passed(self, a):
        return [self._copy(a, 4 + j, (*chip, self.c), self.sibling) for j, chip in enumerate(self.others)]

    @staticmethod
    def scratch(n):
        return [pltpu.SemaphoreType.DMA((7 * n,)), pltpu.SemaphoreType.DMA((7 * n,))]

    def start(self):
        for a in range(len(self.parts)):
            for cp in self._first(a):
                cp.start()

    def finish(self):
        sent = []
        for a in range(len(self.parts)):
            passed = self._passed(a)
            for j, chip in enumerate(self.others):
                self._copy(a, 1 + j, (*chip, self.c), self.sibling).wait_recv()
                passed[j].start()
            sent += self._first(a) + passed
        for a in range(len(self.parts)):
            self._copy(a, 0, self.sibling, self.sibling).wait_recv()
            for j, chip in enumerate(self.others):
                self._copy(a, 4 + j, (*chip, 1 - self.c), self.sibling).wait_recv()
        for cp in sent:
            cp.wait_send()


def _gathered_shapes(parts):
    return [jax.ShapeDtypeStruct((N_DEV,) + p.shape, p.dtype) for p in parts]


def _rs_sum_group(name, place, landed, parts, landed_all, grads_all, cross_parts, small_parts):
    nk, na, nx, ns = len(landed), len(landed_all), len(cross_parts), len(small_parts)
    dims = [a.shape[1:] for a in landed]
    dims_all = [a.shape[1:] for a in landed_all]

    def body(place_ref, *refs):
        take = iter(refs)
        l_refs, p_refs, la_refs, ga_refs, x_refs, sp_refs, o_refs, oa_refs, xl_refs, sbufs = (
            [next(take) for _ in range(cnt)] for cnt in (nk, nk, na, na, nx, ns, nk, na, nx, ns))
        sems = list(take)
        cross = _Cross(x_refs, xl_refs, sems[0], sems[1])
        small = _SmallGather(sp_refs, sbufs, sems[2], sems[3])
        t = pl.program_id(0)

        @pl.when(t == 0)
        def _():
            cross.start()
            small.start()

        me = place_ref[0]
        for l_ref, p_ref, o_ref in zip(l_refs, p_refs, o_refs):
            own = p_ref[0].astype(f32)
            acc = jnp.where(me == 0, own, l_ref[0].astype(f32))
            for j in range(1, N_CHIPS):
                acc = acc + jnp.where(me == j, own, l_ref[j].astype(f32))
            o_ref[...] = acc
        dev = 2 * me + place_ref[1]
        for l_ref, g_ref, o_ref in zip(la_refs, ga_refs, oa_refs):
            own = g_ref[0].astype(f32)
            acc = jnp.where(dev == 0, own, l_ref[0].astype(f32))
            for d in range(1, N_DEV):
                acc = acc + jnp.where(dev == d, own, l_ref[d].astype(f32))
            o_ref[...] = acc

        @pl.when(t == 1)
        def _():
            small.finish()
            cross.finish()

    def halves(h, c, lead, index):
        return pl.BlockSpec((lead, h // 2, c) if lead else (h // 2, c), index)

    in_specs = [halves(h, c, N_CHIPS, lambda t, pr: (0, t, 0)) for h, c in dims]
    in_specs += [halves(h, c, 1, lambda t, pr: (pr[0], t, 0)) for h, c in dims]
    in_specs += [halves(h, c, N_DEV, lambda t, pr: (0, t, 0)) for h, c in dims_all]
    in_specs += [halves(h, c, 1, lambda t, pr: (pr[0], 2 * pr[1] + t, 0)) for h, c in dims_all]
    in_specs += [ANY] * (nx + ns)
    out_specs = [halves(h, c, 0, lambda t, pr: (2 * pr[1] + t, 0)) for h, c in dims + dims_all] + [ANY] * (nx + ns)
    out_shape = [jax.ShapeDtypeStruct((2 * h, c), f32) for h, c in dims + dims_all]
    out_shape += [jax.ShapeDtypeStruct(a.shape, a.dtype) for a in cross_parts] + _gathered_shapes(small_parts)
    res = pl.pallas_call(
        body, name=f"rs_sum_{name}",
        grid_spec=pltpu.PrefetchScalarGridSpec(
            num_scalar_prefetch=1, grid=(2,), in_specs=in_specs, out_specs=out_specs,
            scratch_shapes=_Cross.scratch(nx) + _SmallGather.scratch(ns)),
        out_shape=out_shape, compiler_params=_cp("arbitrary"),
    )(place, *landed, *parts, *landed_all, *grads_all, *cross_parts, *small_parts)
    return res[:nk], res[nk:nk + na], res[nk + na:nk + na + nx], res[nk + na + nx:]


SMALL_PARAMS = ("g_ple_gate", "g_ple_post", "g_final", "g_ffn", "ln_g", "ln_b", "conv_b", "pool_scale", "pool_w",
                "conv_w", "g_mix")
SMALL_ROWS = {"g_ple_gate": (0, 0), "g_ple_post": (0, 1), "g_final": (0, 2), "g_ffn": (1, 0), "ln_g": (2, 0),
              "ln_b": (2, 1), "conv_b": (2, 2), "pool_scale": (2, 3), "g_mix": (5, 0)}
LOSS_ROW = (0, 3)


def _small_adam(place, gathered, parts, params):
    nb, names = len(parts), SMALL_PARAMS
    flat = [a for nm in names for a in params[nm]]

    def body(place_ref, *refs):
        b_refs, p_refs = refs[:nb], refs[nb:2 * nb]
        w_refs = refs[2 * nb:2 * nb + 3 * len(names)]
        outs = refs[2 * nb + 3 * len(names):]
        loss_ref, o_refs, cw_sum = outs[0], outs[1:1 + 4 * len(names)], outs[1 + 4 * len(names)]
        chip = place_ref[0]
        me = 2 * chip + place_ref[1]

        def total(blk, idx):
            own = p_refs[blk][idx]
            g = jnp.where(me == 0, own, b_refs[blk][(0,) + idx])
            for d in range(1, N_DEV):
                g = g + jnp.where(me == d, own, b_refs[blk][(d,) + idx])
            return g

        everything = (slice(None), slice(None))
        loss_ref[...] = total(LOSS_ROW[0], (pl.ds(LOSS_ROW[1], 1), pl.ds(0, 128)))
        d_cw = total(4, everything)
        mine = jnp.where(chip == 0, d_cw[:, 0:128], 0.0)
        for j in range(1, N_CHIPS):
            mine = mine + jnp.where(chip == j, d_cw[:, j * 128:(j + 1) * 128], 0.0)
        cw_sum[...] = mine
        for k, nm in enumerate(names):
            w_ref, m_ref, v_ref = w_refs[3 * k:3 * k + 3]
            g_ref, d_ref, nm_ref, nv_ref = o_refs[4 * k:4 * k + 4]
            if nm == "pool_w":
                g = total(3, everything + (slice(None),))
            elif nm == "conv_w":
                g = cw_sum[pl.ds(0, CONV_K), :]
            else:
                blk, row = SMALL_ROWS[nm]
                g = total(blk, (pl.ds(row, 1), slice(None)))
            g_ref[...] = g
            d_ref[...], nm_ref[...], nv_ref[...] = _adam_math(w_ref[...], g, m_ref[...], v_ref[...])

    whole = lambda a: pl.BlockSpec(a.shape, lambda t, pr: (0,) * a.ndim)
    out_shape = [jax.ShapeDtypeStruct((1, 128), f32)]
    out_shape += [jax.ShapeDtypeStruct(params[nm][0].shape, f32) for nm in names for _ in range(4)]
    res = pl.pallas_call(
        body, name="small_adam",
        grid_spec=pltpu.PrefetchScalarGridSpec(
            num_scalar_prefetch=1, grid=(1,),
            in_specs=[whole(a) for a in list(gathered) + list(parts) + flat],
            out_specs=[whole(s) for s in out_shape], scratch_shapes=[pltpu.VMEM((HALO, 128), f32)]),
        out_shape=out_shape, compiler_params=_cp("arbitrary"),
    )(place, *gathered, *parts, *flat)
    return res[0], {nm: res[1 + 4 * k:5 + 4 * k] for k, nm in enumerate(names)}


def _pad_rows(a, rows):
    return jnp.concatenate([a, jnp.zeros((rows - a.shape[0],) + a.shape[1:], a.dtype)], axis=0)


def kernel(x, p, g_mix, w_in, conv_w, conv_b, ln_g, ln_b, pool_w, pool_scale, w_out, g_ffn, w_gate_up, w_down, g_ple_gate, w_ple_gate, w_ple_up, g_ple_post, g_final, loss_target, m_g_mix, m_w_in, m_conv_w, m_conv_b, m_ln_g, m_ln_b, m_pool_w, m_pool_scale, m_w_out, m_g_ffn, m_w_gate_up, m_w_down, m_g_ple_gate, m_w_ple_gate, m_w_ple_up, m_g_ple_post, m_g_final, v_g_mix, v_w_in, v_conv_w, v_conv_b, v_ln_g, v_ln_b, v_pool_w, v_pool_scale, v_w_out, v_g_ffn, v_w_gate_up, v_w_down, v_g_ple_gate, v_w_ple_gate, v_w_ple_up, v_g_ple_post, v_g_final):
    seq = x.shape[1]
    me = 2 * lax.axis_index("x") + lax.axis_index("y")
    chip = me.astype(jnp.int32).reshape(1)
    core = lax.axis_index("c").astype(jnp.int32).reshape(1)
    place = jnp.concatenate([chip, core])
    xs, ps, ts = x.reshape(seq, D_MODEL), p.reshape(seq, D_PLE), loss_target.reshape(seq, D_MODEL)

    big = [w_in[0], w_gate_up[0], w_out[0], w_down[0], w_ple_gate[0], w_ple_up[0]]
    big_m = [m_w_in[0], m_w_gate_up[0], m_w_out[0], m_w_down[0], m_w_ple_gate[0], m_w_ple_up[0]]
    big_v = [v_w_in[0], v_w_gate_up[0], v_w_out[0], v_w_down[0], v_w_ple_gate[0], v_w_ple_up[0]]
    b_in, b_gu, b_out, b_down, b_pg, b_pu, b_cw = _cast_into_slots(
        chip, big + [_pad_rows(conv_w[0], HALO)], [bf16] * len(big) + [f32])
    xi, yi = lax.axis_index("x"), lax.axis_index("y")
    order = jnp.stack([me, 2 * (1 - xi) + yi, 2 * xi + 1 - yi, 2 * (1 - xi) + 1 - yi]).astype(jnp.int32)

    z, (w_in_g, cw_g, w_out_g) = _mix_in(xs, g_mix, order, [b_in, b_cw, b_out])
    conv_w_f = cw_g.transpose(1, 0, 2).reshape(HALO, C_CONV)
    w_out_f = w_out_g.reshape(D_MODEL, D_MODEL)
    (x1, mix, u1, pooled, h2), (w_gu_g,) = _conv_pool_out(z, xs, conv_w_f, conv_b, ln_g, ln_b, pool_w[0], pool_scale,
                                                          w_out_f, g_ffn, [b_gu])
    (gu, ffn_f), (w_down_g, w_pg_g, w_pu_g) = _ffn_up(h2, w_gu_g, [b_down, b_pg, b_pu])
    w_down_f = w_down_g.reshape(D_FF, D_MODEL)
    x2 = _ffn_down(x1, ffn_f, w_down_f)
    w_pg_f = w_pg_g.reshape(D_MODEL, D_MODEL)
    dx2, d_w_pg, d_w_pu, small_ple = _ple_loss(x2, ps, ts, g_ple_gate, g_ple_post, g_final.reshape(1, D_MODEL),
                                               w_pg_f, w_pu_g)
    d_w_down = _ffn_bwd_dw_down(ffn_f, dx2)
    grads_a = [d_w_down.reshape(N_CHIPS, -1, D_MODEL), d_w_pg.reshape(N_CHIPS, -1, D_MODEL), d_w_pu]
    (dx1, dgu, small_ffn), landed_a = _ffn_bwd_dx(dx2, x1, gu, g_ffn, w_gu_g, w_down_f, grads_a)
    d_w_gu = _ffn_bwd_dw_gu(h2, dgu)
    du1, dpo, d_w_out, d_pool_w, small_mix = _mix_bwd_local(dx1, mix, u1, pooled, w_out_f, ln_g, ln_b, pool_w[0],
                                                             pool_scale)
    parts_b = _pair_reduce("b", [d_w_gu, d_w_out.reshape(N_CHIPS, -1, D_MODEL)])
    small_0 = [small_ple, small_ffn, small_mix, d_pool_w]
    (grad_x, d_w_in, d_conv_w, small_in), landed_b, small_all_0 = _in_bwd(du1, dpo, z, xs, dx1, conv_w_f, g_mix,
                                                                            w_in_g, parts_b, small_0)
    parts_c = _pair_reduce("c", [d_w_in])
    small_1 = [d_conv_w, small_in]
    (h_gu, h_out), (h_down, h_pg, h_pu), landed_c, small_all_1 = _rs_sum_group(
        "ab", place, landed_b, parts_b, landed_a, grads_a, parts_c, small_1)
    h_in = _rs_sum_chips("w_in", place, landed_c[0], parts_c[0])
    big_g = _grad_pair([h_in, h_gu, h_out, h_down, h_pg, h_pu])
    big_upd = _adam(big, big_g, big_m, big_v)

    p3 = lambda w, m, v: (w, m, v)
    row = lambda a: a.reshape(1, D_MODEL)
    params = dict(
        g_ple_gate=p3(g_ple_gate, m_g_ple_gate, v_g_ple_gate), g_ple_post=p3(g_ple_post, m_g_ple_post, v_g_ple_post),
        g_final=p3(row(g_final), row(m_g_final), row(v_g_final)), g_ffn=p3(g_ffn, m_g_ffn, v_g_ffn),
        ln_g=p3(ln_g, m_ln_g, v_ln_g), ln_b=p3(ln_b, m_ln_b, v_ln_b), conv_b=p3(conv_b, m_conv_b, v_conv_b),
        pool_scale=p3(pool_scale, m_pool_scale, v_pool_scale), pool_w=p3(pool_w[0], m_pool_w[0], v_pool_w[0]),
        conv_w=p3(conv_w[0], m_conv_w[0], v_conv_w[0]), g_mix=p3(g_mix, m_g_mix, v_g_mix))
    loss, small = _small_adam(place, list(small_all_0) + list(small_all_1), small_0 + small_1, params)
    back = dict(g_final=lambda a: a.reshape(D_MODEL), pool_w=lambda a: a[None], conv_w=lambda a: a[None])

    names = ["g_mix", "w_in", "conv_w", "conv_b", "ln_g", "ln_b", "pool_w", "pool_scale", "w_out", "g_ffn",
             "w_gate_up", "w_down", "g_ple_gate", "w_ple_gate", "w_ple_up", "g_ple_post", "g_final"]
    big_at = {"w_in": 0, "w_gate_up": 1, "w_out": 2, "w_down": 3, "w_ple_gate": 4, "w_ple_up": 5}
    out = [loss[0, 0], grad_x.reshape(1, seq, D_MODEL)]
    for kind in range(4):
        for nm in names:
            if nm in big_at:
                out.append(big_upd[big_at[nm]][kind][None])
            else:
                out.append(back.get(nm, lambda a: a)(small[nm][kind]))
    return tuple(out)
```

```python
import functools

import jax
import jax.numpy as jnp
from jax import lax
from jax.experimental import pallas as pl
from jax.experimental.pallas import tpu as pltpu

f32, bf16 = jnp.float32, jnp.bfloat16

EPS = 1e-6
D_MODEL = 1024
C_CONV = 512
C_POOL = 512
POOL_WINDOWS = (2, 4, 8, 16)
POOL_GROUP = 128
CONV_K = 31
D_FF = 2816
D_PLE = 256
N_CHIPS = 4
N_DEV = 8
W_IN_COLS = 2 * C_CONV + C_POOL
W_IN_CHUNK = W_IN_COLS // N_CHIPS
FF_CHUNK = 2 * D_FF // N_CHIPS
PLE_CHUNK = D_MODEL // N_CHIPS
HALO = 32
ROW_TILE = 512
SHIFT_PAD = 32
SHIFT_GROUPS = 16
FF_SUB = (0, 512, 1024, FF_CHUNK)
VMEM_LIMIT = 56 * 1024 * 1024

ADAM_LR = 0.001
ADAM_B1 = 0.9
ADAM_B2 = 0.999
ADAM_EPS = 1e-08
ADAM_WD = 0.01
ADAM_STEP = 10

MESH = pl.DeviceIdType.MESH
ANY = pl.BlockSpec(memory_space=pl.ANY)
VMEM = pl.BlockSpec(memory_space=pltpu.VMEM)


def _cp(*sem):
    return pltpu.CompilerParams(dimension_semantics=sem, vmem_limit_bytes=VMEM_LIMIT)


def _dot(a, b):
    return jnp.dot(a, b, preferred_element_type=f32)


def _dot_nt(a, b):
    return lax.dot_general(a, b, (((1,), (1,)), ((), ())), preferred_element_type=f32)


def _dot_tn(a, b):
    return lax.dot_general(a, b, (((0,), (0,)), ((), ())), preferred_element_type=f32)


def _sigmoid(v):
    return jax.nn.sigmoid(v)


def _rms_fwd(v, g):
    r = lax.rsqrt(jnp.mean(v * v, axis=-1, keepdims=True) + EPS)
    vh = v * r
    return vh * g, vh, r


def _rms_bwd(dy, vh, r, g):
    dvh = dy * g
    dv = r * (dvh - vh * jnp.mean(dvh * vh, axis=-1, keepdims=True))
    return dv, jnp.sum(dy * vh, axis=0, keepdims=True)


def _silu_grad(v, s):
    return s * (1.0 + v * (1.0 - s))


def _row(i, n):
    return pl.BlockSpec((n[0], n[1]), lambda *a: (a[i], 0))


def _full(shape):
    nd = len(shape)
    return pl.BlockSpec(shape, lambda *a: (0,) * nd)


def _place():
    x, y, c = lax.axis_index("x"), lax.axis_index("y"), lax.axis_index("c")
    others = [(1 - x, y), (x, 1 - y), (1 - x, 1 - y)]
    return x, y, c, 2 * x + y, others


def _remote(src, dst, send_sem, recv_sem, dev):
    return pltpu.make_async_remote_copy(src_ref=src, dst_ref=dst, send_sem=send_sem, recv_sem=recv_sem,
                                        device_id=dev, device_id_type=MESH)


SHARD_STEPS = 4


def _cast_into_slots(me, ws, dtypes):
    n = len(ws)

    def body(me_ref, *refs):
        for w_ref, o_ref, dtype in zip(refs[:n], refs[n:], dtypes):
            o_ref[0] = w_ref[...].astype(dtype)

    return pl.pallas_call(
        body, name="cast_shards",
        grid_spec=pltpu.PrefetchScalarGridSpec(
            num_scalar_prefetch=1, grid=(SHARD_STEPS,),
            in_specs=[pl.BlockSpec((w.shape[0] // SHARD_STEPS, w.shape[1]), lambda r, me_ref: (r, 0)) for w in ws],
            out_specs=[pl.BlockSpec((1, w.shape[0] // SHARD_STEPS, w.shape[1]), lambda r, me_ref: (me_ref[0], r, 0))
                       for w in ws]),
        out_shape=[jax.ShapeDtypeStruct((N_CHIPS,) + w.shape, dt) for w, dt in zip(ws, dtypes)],
        compiler_params=_cp("parallel"),
    )(me, *ws)


class _Gather:
    def __init__(self, bufs, send_sems, recv_sems):
        self.bufs, self.send_sems, self.recv_sems = bufs, send_sems, recv_sems
        self.x, self.y, self.c, self.me, self.others = _place()
        self.halves = [b.shape[1] // 2 for b in bufs]

    def _piece(self, k, chip, half):
        return self.bufs[k].at[chip, pl.ds(half * self.halves[k], self.halves[k]), :]

    def _ici(self, k, j, chip):
        ox, oy = self.others[j]
        piece = self._piece(k, chip, self.c)
        return _remote(piece, piece, self.send_sems.at[6 * k + j], self.recv_sems.at[6 * k + j], (ox, oy, self.c))

    def _relay(self, k):
        first = self.c == 0
        piece = self._piece(k, jnp.where(first, self.chip(0), self.chip(1)), self.c)
        to = (jnp.where(first, self.others[1][0], self.others[0][0]),
              jnp.where(first, self.others[1][1], self.others[0][1]), self.c)
        return _remote(piece, piece, self.send_sems.at[6 * k + 2], self.recv_sems.at[6 * k + 2], to)

    def _pair(self, k, j, half):
        ox, oy = self.others[j]
        piece = self._piece(k, 2 * ox + oy, half)
        return _remote(piece, piece, self.send_sems.at[6 * k + 3 + j], self.recv_sems.at[6 * k + 3 + j],
                       (self.x, self.y, 1 - self.c))

    def _each(self, ks=None):
        return [(k, j) for k in (range(len(self.bufs)) if ks is None else ks) for j in range(3)]

    def chip(self, j):
        ox, oy = self.others[j]
        return 2 * ox + oy

    def start(self):
        for k in range(len(self.bufs)):
            for j in range(2):
                self._ici(k, j, self.me).start()

    def forward(self, pairs=None):
        for k, j in self._each() if pairs is None else pairs:
            self._ici(k, j, self.chip(j)).wait_recv()
            self._pair(k, j, self.c).start()
            if j < 2:
                pl.when(self.c == j)(self._relay(k).start)

    def landed(self, pairs):
        for k, j in pairs:
            self._pair(k, j, 1 - self.c).wait_recv()

    def finish(self, ks=None):
        self.landed(self._each(ks))
        for k in range(len(self.bufs)):
            for j in range(2):
                self._ici(k, j, self.me).wait_send()
            self._relay(k).wait_send()
            for j in range(3):
                self._pair(k, j, self.c).wait_send()

    @staticmethod
    def scratch(n):
        return [pltpu.SemaphoreType.DMA((6 * n,)), pltpu.SemaphoreType.DMA((6 * n,))]


def _carried(bufs):
    n = len(bufs)
    return dict(in_specs=[ANY] * n, out_specs=[ANY] * n,
                out_shape=[jax.ShapeDtypeStruct(b.shape, b.dtype) for b in bufs], scratch=_Gather.scratch(n))


def _mix_in(x, g_mix, order, carry):
    s = x.shape[0]
    tm = min(2 * ROW_TILE, s)
    n = s // tm
    nc = len(carry)
    cs = _carried(carry)

    def body(order_ref, x_ref, g_ref, *refs):
        z_ref = refs[nc]
        bufs = refs[nc + 1:2 * nc + 1]
        h_ref, w_ref, w_sem = refs[2 * nc + 1:2 * nc + 4]
        gather = _Gather(bufs, *refs[2 * nc + 4:])
        q, i = pl.program_id(0), pl.program_id(1)
        first = i == 0
        pl.when(jnp.logical_and(q == 0, first))(gather.start)
        for j in range(3):

            @pl.when(jnp.logical_and(q == j + 1, first))
            def _():
                gather.forward([(0, j)])
                gather.landed([(0, j)])
                if j == 1:
                    gather.forward([(k, jj) for k in range(1, nc) for jj in range(2)])

        @pl.when(first)
        def _():
            load = pltpu.make_async_copy(bufs[0].at[order_ref[q]], w_ref, w_sem)
            load.start()
            load.wait()

        @pl.when(q == 0)
        def _():
            h, _, _ = _rms_fwd(x_ref[...], g_ref[...])
            h_ref[i] = h.astype(bf16)

        z_ref[...] = _dot(h_ref[i], w_ref[...])

        @pl.when(jnp.logical_and(q == N_CHIPS - 1, i == n - 1))
        def _():
            gather.forward([(k, 2) for k in range(1, nc)])
            gather.finish(range(1, nc))

    res = pl.pallas_call(
        body, name="mix_in",
        grid_spec=pltpu.PrefetchScalarGridSpec(
            num_scalar_prefetch=1, grid=(N_CHIPS, n),
            in_specs=[pl.BlockSpec((tm, D_MODEL), lambda q, i, order_ref: (jnp.where(q == 0, i, 0), 0)),
                      pl.BlockSpec((1, D_MODEL), lambda q, i, order_ref: (0, 0))] + cs["in_specs"],
            out_specs=[pl.BlockSpec((tm, W_IN_CHUNK), lambda q, i, order_ref: (i, order_ref[q]))] + cs["out_specs"],
            scratch_shapes=[pltpu.VMEM((n, tm, D_MODEL), bf16), pltpu.VMEM((D_MODEL, W_IN_CHUNK), bf16),
                            pltpu.SemaphoreType.DMA(())] + cs["scratch"]),
        out_shape=[jax.ShapeDtypeStruct((s, W_IN_COLS), f32)] + cs["out_shape"],
        input_output_aliases={3 + k: 1 + k for k in range(nc)},
        compiler_params=_cp("arbitrary", "arbitrary"),
    )(order, x, g_mix, *carry)
    return res[0], res[1:]


def _pool_counts(tm, w, first_row):
    t1 = (lax.broadcasted_iota(jnp.int32, (tm, 1), 0) + first_row + 1).astype(f32)
    return jnp.minimum(t1, float(w))


def _conv_pool_out(z, x, conv_w, conv_b, ln_g, ln_b, pool_w, pool_scale, w_out, g_ffn, carry):
    s = x.shape[0]
    tm = min(ROW_TILE, s)
    n = s // tm
    hb = tm // HALO
    nv = tm // 8
    assert nv >= SHIFT_PAD and nv % SHIFT_GROUPS == 0
    nc = len(carry)
    cs = _carried(carry)

    def body(z_ref, zp_ref, x_ref, cw_ref, cb_ref, lg_ref, lb_ref, pw_ref, ps_ref, wo_ref, gf_ref, *refs):
        x1_ref, mix_ref, u1_ref, pooled_ref, h2_ref = refs[nc:nc + 5]
        eu, ev, ss, u0_ref = refs[2 * nc + 5:2 * nc + 9]
        gather = _Gather(refs[nc + 5:2 * nc + 5], *refs[2 * nc + 9:])
        i = pl.program_id(0)
        pl.when(i == 0)(gather.start)
        for k in range(nc):
            pl.when(i == min(n // 2 + 2 * k, n - 1))(functools.partial(gather.forward, [(k, 0), (k, 1)]))
        keep = (i > 0).astype(f32)
        zp = zp_ref[...] * keep
        u0_prev = zp[:, :C_CONV] * _sigmoid(zp[:, C_CONV:2 * C_CONV])
        u0_ref[...] = z_ref[:, :C_CONV] * _sigmoid(z_ref[:, C_CONV:2 * C_CONV])
        for c, w_pool in enumerate(POOL_WINDOWS):
            lanes = slice(c * 128, (c + 1) * 128)
            v_lanes = slice(2 * C_CONV + c * 128, 2 * C_CONV + (c + 1) * 128)
            eu[c, pl.ds(0, SHIFT_PAD, stride=8), :] = u0_prev[:, lanes]
            ev[c, pl.ds(0, SHIFT_PAD, stride=8), :] = zp[:, v_lanes]
            for j in range(8):
                rows = slice(j * nv, (j + 1) * nv)
                eu[c, pl.ds(SHIFT_PAD * 8 + j, nv, stride=8), :] = u0_ref[rows, lanes]
                ev[c, pl.ds(SHIFT_PAD * 8 + j, nv, stride=8), :] = z_ref[rows, v_lanes]
                if j >= 1:
                    edge = slice(j * nv - SHIFT_PAD, j * nv)
                    eu[c, pl.ds(j, SHIFT_PAD, stride=8), :] = u0_ref[edge, lanes]
                    ev[c, pl.ds(j, SHIFT_PAD, stride=8), :] = z_ref[edge, v_lanes]
            for v0 in range(0, nv, SHIFT_GROUPS):
                span = SHIFT_GROUPS * 8
                acc = jnp.zeros((span, 128), f32) + cb_ref[:, lanes]
                for k in range(CONV_K):
                    acc = acc + cw_ref[pl.ds(k, 1), lanes] * eu[c, pl.ds((SHIFT_PAD - (CONV_K - 1) + v0 + k) * 8, span), :]
                ss[0, v0 * 8:v0 * 8 + span, :] = acc
                acc = ev[c, pl.ds((SHIFT_PAD + v0) * 8, span), :]
                for d in range(1, w_pool):
                    acc = acc + ev[c, pl.ds((SHIFT_PAD + v0 - d) * 8, span), :]
                ss[1, v0 * 8:v0 * 8 + span, :] = acc
            for j in range(8):
                rows = slice(j * nv, (j + 1) * nv)
                u1_ref[rows, lanes] = ss[0, pl.ds(j, nv, stride=8), :]
                mean = ss[1, pl.ds(j, nv, stride=8), :] / _pool_counts(nv, w_pool, i * tm + j * nv)
                pooled_ref[rows, lanes] = (mean - z_ref[rows, v_lanes]).astype(bf16)
        u1 = u1_ref[...]
        mu = jnp.mean(u1, axis=-1, keepdims=True)
        uc = u1 - mu
        rstd = lax.rsqrt(jnp.mean(uc * uc, axis=-1, keepdims=True) + EPS)
        u2 = uc * rstd * lg_ref[...] + lb_ref[...]
        mix_ref[:, :C_CONV] = (u2 * _sigmoid(u2)).astype(bf16)
        for g in range(len(POOL_WINDOWS)):
            cols = slice(g * POOL_GROUP, (g + 1) * POOL_GROUP)
            mixed = _dot(pooled_ref[:, cols], pw_ref[g].astype(bf16))
            mix_ref[:, C_CONV + g * POOL_GROUP:C_CONV + (g + 1) * POOL_GROUP] = (mixed * ps_ref[:, cols]).astype(bf16)
        x1 = x_ref[...] + _dot(mix_ref[...], wo_ref[...])
        x1_ref[...] = x1
        h2_ref[...] = _rms_fwd(x1, gf_ref[...])[0].astype(bf16)
        @pl.when(i == n - 1)
        def _():
            gather.forward([(k, 2) for k in range(nc)])
            gather.finish()

    res = pl.pallas_call(
        body, name="conv_pool_out", grid=(n,),
        in_specs=[_row(0, (tm, W_IN_COLS)),
                  pl.BlockSpec((HALO, W_IN_COLS), lambda i: (jnp.maximum(i * hb - 1, 0), 0)),
                  _row(0, (tm, D_MODEL)), _full((HALO, C_CONV)), _full((1, C_CONV)), _full((1, C_CONV)),
                  _full((1, C_CONV)), _full((4, POOL_GROUP, POOL_GROUP)), _full((1, C_POOL)),
                  _full((D_MODEL, D_MODEL)), _full((1, D_MODEL))] + cs["in_specs"],
        out_specs=[_row(0, (tm, D_MODEL)), _row(0, (tm, D_MODEL)), _row(0, (tm, C_CONV)), _row(0, (tm, C_POOL)),
                   _row(0, (tm, D_MODEL))] + cs["out_specs"],
        out_shape=[jax.ShapeDtypeStruct((s, D_MODEL), f32), jax.ShapeDtypeStruct((s, D_MODEL), bf16),
                   jax.ShapeDtypeStruct((s, C_CONV), f32), jax.ShapeDtypeStruct((s, C_POOL), bf16),
                   jax.ShapeDtypeStruct((s, D_MODEL), bf16)] + cs["out_shape"],
        input_output_aliases={11 + k: 5 + k for k in range(nc)},
        scratch_shapes=[pltpu.VMEM((4, (SHIFT_PAD + nv) * 8, 128), f32), pltpu.VMEM((4, (SHIFT_PAD + nv) * 8, 128), f32),
                        pltpu.VMEM((2, tm, 128), f32), pltpu.VMEM((tm, C_CONV), f32)] + cs["scratch"],
        compiler_params=_cp("arbitrary"),
    )(z, z, x, conv_w, conv_b, ln_g, ln_b, pool_w, pool_scale, w_out, g_ffn, *carry)
    return res[:5], res[5:]


def _ffn_up(h2, w_gu_g, carry):
    s = h2.shape[0]
    tm = min(2 * ROW_TILE, s)
    n = s // tm
    nc = len(carry)
    cs = _carried(carry)

    def body(h2_ref, wg_ref, wu_ref, *refs):
        gu_ref, f_ref = refs[nc:nc + 2]
        gather = _Gather(refs[nc + 2:2 * nc + 2], *refs[2 * nc + 2:])
        c, i = pl.program_id(0), pl.program_id(1)
        pl.when(jnp.logical_and(i == 0, c == 0))(gather.start)
        direct = [(k, j) for k in range(nc) for j in range(2)]
        pl.when(jnp.logical_and(i == n - 1, c == 0))(functools.partial(gather.forward, direct))
        pl.when(jnp.logical_and(i == n // 2, c == 1))(functools.partial(gather.forward, [(k, 2) for k in range(nc)]))
        h = h2_ref[...]
        for lo, hi in zip(FF_SUB[:-1], FF_SUB[1:]):
            gate = _dot(h, wg_ref[0, :, lo:hi])
            up = _dot(h, wu_ref[0, :, lo:hi])
            gu_ref[0, :, lo:hi] = gate.astype(bf16)
            gu_ref[1, :, lo:hi] = up.astype(bf16)
            f_ref[:, lo:hi] = (gate * _sigmoid(gate) * up).astype(bf16)
        pl.when(jnp.logical_and(i == n - 1, c == 1))(gather.finish)

    res = pl.pallas_call(
        body, name="ffn_up", grid=(2, n),
        in_specs=[_row(1, (tm, D_MODEL)),
                  pl.BlockSpec((1, D_MODEL, FF_CHUNK), lambda c, i: (c, 0, 0)),
                  pl.BlockSpec((1, D_MODEL, FF_CHUNK), lambda c, i: (2 + c, 0, 0))] + cs["in_specs"],
        out_specs=[pl.BlockSpec((2, tm, FF_CHUNK), lambda c, i: (0, i, c)),
                   pl.BlockSpec((tm, FF_CHUNK), lambda c, i: (i, c))] + cs["out_specs"],
        out_shape=[jax.ShapeDtypeStruct((2, s, D_FF), bf16), jax.ShapeDtypeStruct((s, D_FF), bf16)] + cs["out_shape"],
        input_output_aliases={3 + k: 2 + k for k in range(nc)},
        scratch_shapes=cs["scratch"],
        compiler_params=_cp("arbitrary", "arbitrary"),
    )(h2, w_gu_g, w_gu_g, *carry)
    return res[:2], res[2:]


def _ffn_down(x1, f, w_down):
    s = x1.shape[0]
    tm = min(2 * ROW_TILE, s)

    def body(x1_ref, f_ref, wd_ref, x2_ref):
        x2_ref[...] = x1_ref[...] + _dot(f_ref[...], wd_ref[...])

    return pl.pallas_call(
        body, name="ffn_down", grid=(s // tm,),
        in_specs=[_row(0, (tm, D_MODEL)), _row(0, (tm, D_FF)), _full((D_FF, D_MODEL))],
        out_specs=_row(0, (tm, D_MODEL)), out_shape=jax.ShapeDtypeStruct((s, D_MODEL), f32),
        compiler_params=_cp("parallel"),
    )(x1, f, w_down)


def _ple_loss(x2, p, target, g_pg, g_post, g_final, w_pg, w_pu_g):
    s = x2.shape[0]
    tm = min(ROW_TILE, s)
    n = s // tm

    def body(x2_ref, p_ref, t_ref, gpg_ref, gpo_ref, gf_ref, wpg_ref, wpu_ref,
             dx2_ref, dwpg_ref, dwpu_ref, small_ref, apg_ref, apu_ref):
        i = pl.program_id(0)

        @pl.when(i == 0)
        def _():
            apg_ref[...] = jnp.zeros_like(apg_ref)
            apu_ref[...] = jnp.zeros_like(apu_ref)
            small_ref[...] = jnp.zeros_like(small_ref)

        x2 = x2_ref[...]
        h3, x2h, r2 = _rms_fwd(x2, gpg_ref[...])
        h3b = h3.astype(bf16)
        gate = _sigmoid(_dot(h3b, wpg_ref[...]))
        pb = p_ref[...].astype(bf16)
        pe = jnp.concatenate([_dot(pb, wpu_ref[j]) for j in range(N_CHIPS)], axis=-1)
        e, peh, rp = _rms_fwd(pe, gpo_ref[...])
        x3 = x2 + gate * e
        y, x3h, r3 = _rms_fwd(x3, gf_ref[...])
        d = y - t_ref[...]
        loss = 0.5 * jnp.sum(jnp.sum(d * d, axis=-1, keepdims=True) * (1.0 / D_MODEL), axis=0, keepdims=True)
        dx3, dgf = _rms_bwd(d * (1.0 / D_MODEL), x3h, r3, gf_ref[...])
        dpe, dgpo = _rms_bwd(dx3 * gate, peh, rp, gpo_ref[...])
        dgl = (dx3 * e * gate * (1.0 - gate)).astype(bf16)
        apg_ref[...] += _dot_tn(h3b, dgl)
        apu_ref[...] += _dot_tn(pb, dpe.astype(bf16))
        dh3 = _dot_nt(dgl, wpg_ref[...])
        dx2b, dgpg = _rms_bwd(dh3, x2h, r2, gpg_ref[...])
        dx2_ref[...] = dx3 + dx2b
        small_ref[0:1, :] += dgpg
        small_ref[1:2, :] += dgpo
        small_ref[2:3, :] += dgf
        small_ref[3:4, :] += jnp.broadcast_to(loss, (1, D_MODEL))

        @pl.when(i == n - 1)
        def _():
            dwpg_ref[...] = apg_ref[...].astype(bf16)
            for j in range(N_CHIPS):
                dwpu_ref[j] = apu_ref[:, j * PLE_CHUNK:(j + 1) * PLE_CHUNK].astype(bf16)

    return pl.pallas_call(
        body, name="ple_loss", grid=(n,),
        in_specs=[_row(0, (tm, D_MODEL)), _row(0, (tm, D_PLE)), _row(0, (tm, D_MODEL)),
                  _full((1, D_MODEL)), _full((1, D_MODEL)), _full((1, D_MODEL)),
                  _full((D_MODEL, D_MODEL)), _full((N_CHIPS, D_PLE, PLE_CHUNK))],
        out_specs=[_row(0, (tm, D_MODEL)), _full((D_MODEL, D_MODEL)), _full((N_CHIPS, D_PLE, PLE_CHUNK)),
                   _full((8, D_MODEL))],
        out_shape=[jax.ShapeDtypeStruct((s, D_MODEL), f32), jax.ShapeDtypeStruct((D_MODEL, D_MODEL), bf16),
                   jax.ShapeDtypeStruct((N_CHIPS, D_PLE, PLE_CHUNK), bf16), jax.ShapeDtypeStruct((8, D_MODEL), f32)],
        scratch_shapes=[pltpu.VMEM((D_MODEL, D_MODEL), f32), pltpu.VMEM((D_PLE, D_MODEL), f32)],
        compiler_params=_cp("arbitrary"),
    )(x2, p, target, g_pg, g_post, g_final, w_pg, w_pu_g)


def _crossed(parts):
    n = len(parts)
    return dict(in_specs=[ANY] * n, out_specs=[ANY] * n,
                out_shape=[jax.ShapeDtypeStruct(a.shape, a.dtype) for a in parts], scratch=_Cross.scratch(n))


def _ffn_bwd_dx(dx2, x1, gu, g_ffn, w_gu_g, w_down, grads):
    s = x1.shape[0]
    tm = min(ROW_TILE, s)
    n = s // tm
    nc = len(grads)
    cs = dict(in_specs=[ANY] * nc, out_specs=[ANY] * nc, scratch=_CrossAll.scratch(nc),
              out_shape=[jax.ShapeDtypeStruct((N_DEV, g.shape[1] // 2, g.shape[2]), g.dtype) for g in grads])

    def body(dx2_ref, x1_ref, gu_ref, g_ref, wg_ref, wu_ref, wd_ref, *refs):
        dx1_ref, dgu_ref, small_ref = refs[nc:nc + 3]
        acc_ref = refs[2 * nc + 3]
        cross = _CrossAll(refs[:nc], refs[nc + 3:2 * nc + 3], *refs[2 * nc + 4:])
        i, c = pl.program_id(0), pl.program_id(1)
        pl.when(jnp.logical_and(i == 0, c == 0))(cross.start)

        @pl.when(jnp.logical_and(i == 0, c == 0))
        def _():
            small_ref[...] = jnp.zeros_like(small_ref)

        @pl.when(c == 0)
        def _():
            acc_ref[...] = jnp.zeros_like(acc_ref)

        dyb = dx2_ref[...].astype(bf16)
        for lo, hi in zip(FF_SUB[:-1], FF_SUB[1:]):
            df = _dot_nt(dyb, wd_ref[lo:hi, :])
            gate = gu_ref[0, :, lo:hi].astype(f32)
            up = gu_ref[1, :, lo:hi].astype(f32)
            sg = _sigmoid(gate)
            dgate = (df * up * _silu_grad(gate, sg)).astype(bf16)
            dup = (df * gate * sg).astype(bf16)
            dgu_ref[0, :, lo:hi] = dgate
            dgu_ref[1, :, lo:hi] = dup
            acc_ref[...] += _dot_nt(dgate, wg_ref[0, :, lo:hi]) + _dot_nt(dup, wu_ref[0, :, lo:hi])

        @pl.when(c == 1)
        def _():
            _, x1h, r1 = _rms_fwd(x1_ref[...], g_ref[...])
            dx1b, dg = _rms_bwd(acc_ref[...], x1h, r1, g_ref[...])
            dx1_ref[...] = dx2_ref[...] + dx1b
            small_ref[0:1, :] += dg

        pl.when(jnp.logical_and(i == n - 1, c == 1))(cross.finish)

    res = pl.pallas_call(
        body, name="ffn_bwd_dx", grid=(n, 2),
        in_specs=[_row(0, (tm, D_MODEL)), _row(0, (tm, D_MODEL)),
                  pl.BlockSpec((2, tm, FF_CHUNK), lambda i, c: (0, i, c)), _full((1, D_MODEL)),
                  pl.BlockSpec((1, D_MODEL, FF_CHUNK), lambda i, c: (c, 0, 0)),
                  pl.BlockSpec((1, D_MODEL, FF_CHUNK), lambda i, c: (2 + c, 0, 0)),
                  pl.BlockSpec((FF_CHUNK, D_MODEL), lambda i, c: (c, 0))] + cs["in_specs"],
        out_specs=[_row(0, (tm, D_MODEL)), pl.BlockSpec((2, tm, FF_CHUNK), lambda i, c: (0, i, c)),
                   _full((8, D_MODEL))] + cs["out_specs"],
        out_shape=[jax.ShapeDtypeStruct((s, D_MODEL), f32), jax.ShapeDtypeStruct((2, s, D_FF), bf16),
                   jax.ShapeDtypeStruct((8, D_MODEL), f32)] + cs["out_shape"],
        scratch_shapes=[pltpu.VMEM((tm, D_MODEL), f32)] + cs["scratch"],
        compiler_params=_cp("arbitrary", "arbitrary"),
    )(dx2, x1, gu, g_ffn, w_gu_g, w_gu_g, w_down, *grads)
    return res[:3], res[3:]


def _ffn_bwd_dw_gu(h2, dgu):
    s = h2.shape[0]
    ts = min(2 * ROW_TILE, s)
    n = s // ts

    def body(h_ref, d_ref, o_ref, acc_ref):
        t = pl.program_id(1)

        @pl.when(t == 0)
        def _():
            acc_ref[...] = jnp.zeros_like(acc_ref)

        acc_ref[...] += _dot_tn(h_ref[...], d_ref[0])

        @pl.when(t == n - 1)
        def _():
            o_ref[0] = acc_ref[...].astype(bf16)

    return pl.pallas_call(
        body, name="ffn_bwd_dw_gu", grid=(N_CHIPS, n),
        in_specs=[pl.BlockSpec((ts, D_MODEL), lambda j, t: (t, 0)),
                  pl.BlockSpec((1, ts, FF_CHUNK), lambda j, t: (j // 2, t, j % 2))],
        out_specs=pl.BlockSpec((1, D_MODEL, FF_CHUNK), lambda j, t: (j, 0, 0)),
        out_shape=jax.ShapeDtypeStruct((N_CHIPS, D_MODEL, FF_CHUNK), bf16),
        scratch_shapes=[pltpu.VMEM((D_MODEL, FF_CHUNK), f32)],
        compiler_params=_cp("parallel", "arbitrary"),
    )(h2, dgu)


def _ffn_bwd_dw_down(f, dx2):
    s = dx2.shape[0]
    ts = min(2 * ROW_TILE, s)
    n = s // ts

    def body(f_ref, d_ref, o_ref, acc_ref):
        t = pl.program_id(1)

        @pl.when(t == 0)
        def _():
            acc_ref[...] = jnp.zeros_like(acc_ref)

        acc_ref[...] += _dot_tn(f_ref[...], d_ref[...].astype(bf16))

        @pl.when(t == n - 1)
        def _():
            o_ref[...] = acc_ref[...].astype(bf16)

    return pl.pallas_call(
        body, name="ffn_bwd_dw_down", grid=(2, n),
        in_specs=[pl.BlockSpec((ts, FF_CHUNK), lambda c, t: (t, c)),
                  pl.BlockSpec((ts, D_MODEL), lambda c, t: (t, 0))],
        out_specs=pl.BlockSpec((FF_CHUNK, D_MODEL), lambda c, t: (c, 0)),
        out_shape=jax.ShapeDtypeStruct((D_FF, D_MODEL), bf16),
        scratch_shapes=[pltpu.VMEM((FF_CHUNK, D_MODEL), f32)],
        compiler_params=_cp("parallel", "arbitrary"),
    )(f, dx2)


def _mix_bwd_local(dx1, mix, u1, pooled, w_out, ln_g, ln_b, pool_w, pool_scale):
    s = dx1.shape[0]
    tm = min(ROW_TILE, s)
    n = s // tm

    def body(dx1_ref, mix_ref, u1_ref, po_ref, wo_ref, lg_ref, lb_ref, pw_ref, ps_ref,
             du1_ref, dpo_ref, dwo_ref, dpw_ref, small_ref, awo_ref):
        i = pl.program_id(0)

        @pl.when(i == 0)
        def _():
            awo_ref[...] = jnp.zeros_like(awo_ref)
            dpw_ref[...] = jnp.zeros_like(dpw_ref)
            small_ref[...] = jnp.zeros_like(small_ref)

        dyb = dx1_ref[...].astype(bf16)
        dmix = _dot_nt(dyb, wo_ref[...])
        awo_ref[...] += _dot_tn(mix_ref[...], dyb)
        u1 = u1_ref[...]
        mu = jnp.mean(u1, axis=-1, keepdims=True)
        uc = u1 - mu
        rstd = lax.rsqrt(jnp.mean(uc * uc, axis=-1, keepdims=True) + EPS)
        uh = uc * rstd
        u2 = uh * lg_ref[...] + lb_ref[...]
        du2 = dmix[:, :C_CONV] * _silu_grad(u2, _sigmoid(u2))
        duh = du2 * lg_ref[...]
        du1 = rstd * (duh - jnp.mean(duh, axis=-1, keepdims=True) - uh * jnp.mean(duh * uh, axis=-1, keepdims=True))
        du1_ref[...] = du1
        small_ref[0:1, :] += jnp.sum(du2 * uh, axis=0, keepdims=True)
        small_ref[1:2, :] += jnp.sum(du2, axis=0, keepdims=True)
        small_ref[2:3, :] += jnp.sum(du1, axis=0, keepdims=True)
        for g in range(len(POOL_WINDOWS)):
            cols = slice(g * POOL_GROUP, (g + 1) * POOL_GROUP)
            dq = dmix[:, C_CONV + g * POOL_GROUP:C_CONV + (g + 1) * POOL_GROUP]
            pwb = pw_ref[g].astype(bf16)
            pg = po_ref[:, cols]
            mixed = _dot(pg, pwb)
            small_ref[3:4, cols] += jnp.sum(dq * mixed, axis=0, keepdims=True)
            dmixed = (dq * ps_ref[:, cols]).astype(bf16)
            dpw_ref[g] += _dot_tn(pg, dmixed)
            dpo_ref[:, cols] = _dot_nt(dmixed, pwb)

        @pl.when(i == n - 1)
        def _():
            dwo_ref[...] = awo_ref[...].astype(bf16)

    return pl.pallas_call(
        body, name="mix_bwd_local", grid=(n,),
        in_specs=[_row(0, (tm, D_MODEL)), _row(0, (tm, D_MODEL)), _row(0, (tm, C_CONV)), _row(0, (tm, C_POOL)),
                  _full((D_MODEL, D_MODEL)), _full((1, C_CONV)), _full((1, C_CONV)),
                  _full((4, POOL_GROUP, POOL_GROUP)), _full((1, C_POOL))],
        out_specs=[_row(0, (tm, C_CONV)), _row(0, (tm, C_POOL)), _full((D_MODEL, D_MODEL)),
                   _full((4, POOL_GROUP, POOL_GROUP)), _full((8, C_CONV))],
        out_shape=[jax.ShapeDtypeStruct((s, C_CONV), f32), jax.ShapeDtypeStruct((s, C_POOL), f32),
                   jax.ShapeDtypeStruct((D_MODEL, D_MODEL), bf16),
                   jax.ShapeDtypeStruct((4, POOL_GROUP, POOL_GROUP), f32), jax.ShapeDtypeStruct((8, C_CONV), f32)],
        scratch_shapes=[pltpu.VMEM((D_MODEL, D_MODEL), f32)],
        compiler_params=_cp("arbitrary"),
    )(dx1, mix, u1, pooled, w_out, ln_g, ln_b, pool_w, pool_scale)


def _in_bwd(du1, dpo, z, x, dx1, conv_w, g_mix, w_in_g, parts, small_parts):
    s = x.shape[0]
    tm = min(ROW_TILE, s)
    n = s // tm
    hb = tm // HALO
    last = s // HALO - 1
    nv = tm // 8
    assert nv >= SHIFT_PAD and nv % SHIFT_GROUPS == 0
    nc, ns = len(parts), len(small_parts)
    cs = _crossed(parts)

    def body(du_ref, dun_ref, dp_ref, dpn_ref, z_ref, zp_ref, x_ref, dx1_ref, cw_ref, g_ref, w_ref, *refs):
        outs = refs[nc + ns:]
        gx_ref, dw_ref, dcw_ref, small_ref = outs[:4]
        eu, ed, ep, ss, u0_ref, dz_ref, acc_ref, dcw_acc = outs[4 + nc + ns:12 + nc + ns]
        sems = outs[12 + nc + ns:]
        cross = _Cross(refs[:nc], outs[4:4 + nc], sems[0], sems[1])
        gather = _SmallGather(refs[nc:nc + ns], outs[4 + nc:4 + nc + ns], sems[2], sems[3])
        i = pl.program_id(0)
        pl.when(i == 0)(cross.start)
        pl.when(i == 0)(gather.start)

        @pl.when(i == 0)
        def _():
            acc_ref[...] = jnp.zeros_like(acc_ref)
            dcw_acc[...] = jnp.zeros_like(dcw_acc)
            small_ref[...] = jnp.zeros_like(small_ref)

        keep_prev = (i > 0).astype(f32)
        keep_next = (i < n - 1).astype(f32)
        zp = zp_ref[...] * keep_prev
        u0_prev = zp[:, :C_CONV] * _sigmoid(zp[:, C_CONV:2 * C_CONV])
        u0_ref[...] = z_ref[:, :C_CONV] * _sigmoid(z_ref[:, C_CONV:2 * C_CONV])
        du_next = dun_ref[...] * keep_next
        for c, w_pool in enumerate(POOL_WINDOWS):
            lanes = slice(c * 128, (c + 1) * 128)
            eu[c, pl.ds(0, SHIFT_PAD, stride=8), :] = u0_prev[:, lanes]
            ed[c, pl.ds(nv * 8 + 7, SHIFT_PAD, stride=8), :] = du_next[:, lanes]
            ep[c, pl.ds(nv * 8 + 7, SHIFT_PAD, stride=8), :] = (
                dpn_ref[:, lanes] * keep_next / _pool_counts(HALO, w_pool, (i + 1) * tm))
            for j in range(8):
                rows = slice(j * nv, (j + 1) * nv)
                eu[c, pl.ds(SHIFT_PAD * 8 + j, nv, stride=8), :] = u0_ref[rows, lanes]
                ed[c, pl.ds(j, nv, stride=8), :] = du_ref[rows, lanes]
                ep[c, pl.ds(j, nv, stride=8), :] = dp_ref[rows, lanes] / _pool_counts(nv, w_pool, i * tm + j * nv)
                if j >= 1:
                    eu[c, pl.ds(j, SHIFT_PAD, stride=8), :] = u0_ref[j * nv - SHIFT_PAD:j * nv, lanes]
                if j <= 6:
                    edge = slice((j + 1) * nv, (j + 1) * nv + SHIFT_PAD)
                    ed[c, pl.ds(nv * 8 + j, SHIFT_PAD, stride=8), :] = du_ref[edge, lanes]
                    ep[c, pl.ds(nv * 8 + j, SHIFT_PAD, stride=8), :] = (
                        dp_ref[edge, lanes] / _pool_counts(SHIFT_PAD, w_pool, i * tm + (j + 1) * nv))
        for c, w_pool in enumerate(POOL_WINDOWS):
            lanes = slice(c * 128, (c + 1) * 128)
            b_lanes = slice(C_CONV + c * 128, C_CONV + (c + 1) * 128)
            v_lanes = slice(2 * C_CONV + c * 128, 2 * C_CONV + (c + 1) * 128)
            for v0 in range(0, nv, SHIFT_GROUPS):
                span = SHIFT_GROUPS * 8
                acc = jnp.zeros((span, 128), f32)
                for k in range(CONV_K):
                    acc = acc + cw_ref[pl.ds(k, 1), lanes] * ed[c, pl.ds((v0 + CONV_K - 1 - k) * 8, span), :]
                ss[0, v0 * 8:v0 * 8 + span, :] = acc
                acc = ep[c, pl.ds(v0 * 8, span), :]
                for d in range(1, w_pool):
                    acc = acc + ep[c, pl.ds((v0 + d) * 8, span), :]
                ss[1, v0 * 8:v0 * 8 + span, :] = acc
                d1 = ed[c, pl.ds(v0 * 8, span), :]
                for k in range(CONV_K):
                    prod = d1 * eu[c, pl.ds((SHIFT_PAD - (CONV_K - 1) + v0 + k) * 8, span), :]
                    fold = prod[0:8]
                    for r in range(8, span, 8):
                        fold = fold + prod[r:r + 8]
                    dcw_acc[k, :, lanes] += fold
            for j in range(8):
                rows = slice(j * nv, (j + 1) * nv)
                du0 = ss[0, pl.ds(j, nv, stride=8), :]
                av, sv = z_ref[rows, lanes], _sigmoid(z_ref[rows, b_lanes])
                dz_ref[rows, lanes] = (du0 * sv).astype(bf16)
                dz_ref[rows, b_lanes] = (du0 * av * sv * (1.0 - sv)).astype(bf16)
                dz_ref[rows, v_lanes] = (ss[1, pl.ds(j, nv, stride=8), :] - dp_ref[rows, lanes]).astype(bf16)
        h, xh, r = _rms_fwd(x_ref[...], g_ref[...])
        dz = dz_ref[...]
        acc_ref[...] += _dot_tn(h.astype(bf16), dz)
        dh = _dot_nt(dz[:, 0:W_IN_CHUNK], w_ref[0])
        for j in range(1, N_CHIPS):
            dh = dh + _dot_nt(dz[:, j * W_IN_CHUNK:(j + 1) * W_IN_CHUNK], w_ref[j])
        dxb, dg = _rms_bwd(dh, xh, r, g_ref[...])
        gx_ref[...] = dx1_ref[...] + dxb
        small_ref[0:1, :] += dg

        @pl.when(i == n - 1)
        def _():
            for j in range(N_CHIPS):
                dw_ref[j] = acc_ref[:, j * W_IN_CHUNK:(j + 1) * W_IN_CHUNK].astype(bf16)
            dcw_ref[...] = jnp.sum(dcw_acc[...], axis=1)

        pl.when(i == n - 1)(gather.finish)
        pl.when(i == n - 1)(cross.finish)

    nxt = lambda i: (jnp.minimum((i + 1) * hb, last), 0)
    res = pl.pallas_call(
        body, name="in_bwd", grid=(n,),
        in_specs=[_row(0, (tm, C_CONV)), pl.BlockSpec((HALO, C_CONV), nxt),
                  _row(0, (tm, C_POOL)), pl.BlockSpec((HALO, C_POOL), nxt),
                  _row(0, (tm, W_IN_COLS)),
                  pl.BlockSpec((HALO, W_IN_COLS), lambda i: (jnp.maximum(i * hb - 1, 0), 0)),
                  _row(0, (tm, D_MODEL)), _row(0, (tm, D_MODEL)), _full((HALO, C_CONV)), _full((1, D_MODEL)),
                  _full((N_CHIPS, D_MODEL, W_IN_CHUNK))] + cs["in_specs"] + [ANY] * ns,
        out_specs=[_row(0, (tm, D_MODEL)), _full((N_CHIPS, D_MODEL, W_IN_CHUNK)), _full((HALO, C_CONV)),
                   _full((8, D_MODEL))] + cs["out_specs"] + [ANY] * ns,
        out_shape=[jax.ShapeDtypeStruct((s, D_MODEL), f32), jax.ShapeDtypeStruct((N_CHIPS, D_MODEL, W_IN_CHUNK), bf16),
                   jax.ShapeDtypeStruct((HALO, C_CONV), f32), jax.ShapeDtypeStruct((8, D_MODEL), f32)] + cs["out_shape"]
        + _gathered_shapes(small_parts),
        scratch_shapes=[pltpu.VMEM((4, (SHIFT_PAD + nv) * 8, 128), f32), pltpu.VMEM((4, (nv + SHIFT_PAD) * 8, 128), f32),
                        pltpu.VMEM((4, (nv + SHIFT_PAD) * 8, 128), f32), pltpu.VMEM((2, tm, 128), f32),
                        pltpu.VMEM((tm, C_CONV), f32), pltpu.VMEM((tm, W_IN_COLS), bf16),
                        pltpu.VMEM((D_MODEL, W_IN_COLS), f32), pltpu.VMEM((HALO, 8, C_CONV), f32)] + cs["scratch"]
        + _SmallGather.scratch(ns),
        compiler_params=_cp("arbitrary"),
    )(du1, du1, dpo, dpo, z, z, x, dx1, conv_w, g_mix, w_in_g, *parts, *small_parts)
    return res[:4], res[4:4 + nc], res[4 + nc:]


def _pair_reduce(name, grads):
    nk = len(grads)
    halves = [g.shape[1] // 2 for g in grads]

    def body(*refs):
        ins, outs, got = refs[:nk], refs[nk:2 * nk], refs[2 * nk:3 * nk]
        send_sems, recv_sems = refs[3 * nk:]
        x, y, c, _, _ = _place()

        def half(k, core):
            return pl.ds(pl.multiple_of(core * halves[k], 16), halves[k])

        cps = [_remote(ins[k].at[:, half(k, 1 - c), :], got[k], send_sems.at[k], recv_sems.at[k], (x, y, 1 - c))
               for k in range(nk)]
        for cp in cps:
            cp.start()
        for k, cp in enumerate(cps):
            cp.wait_recv()
            outs[k][...] = (ins[k][:, half(k, c), :].astype(f32) + got[k][...].astype(f32)).astype(bf16)
        for cp in cps:
            cp.wait_send()

    shapes = [(N_CHIPS, h, g.shape[2]) for g, h in zip(grads, halves)]
    return pl.pallas_call(
        body, name=f"pair_reduce_{name}", in_specs=[VMEM] * nk, out_specs=[VMEM] * nk,
        out_shape=[jax.ShapeDtypeStruct(s, bf16) for s in shapes],
        scratch_shapes=[pltpu.VMEM(s, bf16) for s in shapes]
        + [pltpu.SemaphoreType.DMA((nk,)), pltpu.SemaphoreType.DMA((nk,))],
        compiler_params=pltpu.CompilerParams(vmem_limit_bytes=VMEM_LIMIT),
    )(*grads)


class _Cross:
    def __init__(self, parts, landed, send_sems, recv_sems):
        self.parts, self.landed, self.send_sems, self.recv_sems = parts, landed, send_sems, recv_sems
        _, _, self.c, self.me, self.others = _place()

    def _copy(self, k, j, src_chunk, dst_slot):
        ox, oy = self.others[j]
        return _remote(self.parts[k].at[src_chunk], self.landed[k].at[dst_slot], self.send_sems.at[3 * k + j],
                       self.recv_sems.at[3 * k + j], (ox, oy, self.c))

    def _each(self):
        return [(k, j, 2 * self.others[j][0] + self.others[j][1]) for k in range(len(self.parts)) for j in range(3)]

    def start(self):
        for k, j, chip in self._each():
            self._copy(k, j, chip, self.me).start()

    def finish(self):
        for k, j, chip in self._each():
            self._copy(k, j, chip, chip).wait_recv()
        for k, j, chip in self._each():
            self._copy(k, j, chip, self.me).wait_send()

    @staticmethod
    def scratch(n):
        return [pltpu.SemaphoreType.DMA((3 * n,)), pltpu.SemaphoreType.DMA((3 * n,))]


class _CrossAll:
    def __init__(self, grads, landed, send_sems, recv_sems):
        self.grads, self.landed, self.send_sems, self.recv_sems = grads, landed, send_sems, recv_sems
        self.x, self.y, self.c, self.me, self.others = _place()
        self.dev = 2 * self.me + self.c

    def _piece(self, k, chip, half):
        rows = self.grads[k].shape[1] // 2
        return self.grads[k].at[chip, pl.ds(half * rows, rows), :]

    def _to_sibling(self, k):
        return _remote(self._piece(k, self.me, 1 - self.c), self.landed[k].at[self.dev], self.send_sems.at[7 * k],
                       self.recv_sems.at[7 * k], (self.x, self.y, 1 - self.c))

    def _to_chip(self, k, j, half):
        ox, oy = self.others[j]
        return _remote(self._piece(k, 2 * ox + oy, half), self.landed[k].at[self.dev],
                       self.send_sems.at[7 * k + 1 + 2 * j + half], self.recv_sems.at[7 * k + 1 + 2 * j + self.c],
                       (ox, oy, half))

    def _from(self, k, sem, slot):
        return _remote(self.landed[k].at[slot], self.landed[k].at[slot], self.send_sems.at[7 * k + sem],
                       self.recv_sems.at[7 * k + sem], (self.x, self.y, 1 - self.c))

    def start(self):
        for k in range(len(self.grads)):
            self._to_sibling(k).start()
            for j in range(3):
                for half in range(2):
                    self._to_chip(k, j, half).start()

    def finish(self):
        for k in range(len(self.grads)):
            self._from(k, 0, 2 * self.me + 1 - self.c).wait_recv()
            for j, (ox, oy) in enumerate(self.others):
                for core in range(2):
                    self._from(k, 1 + 2 * j + core, 4 * ox + 2 * oy + core).wait_recv()
        for k in range(len(self.grads)):
            self._to_sibling(k).wait_send()
            for j in range(3):
                for half in range(2):
                    self._to_chip(k, j, half).wait_send()

    @staticmethod
    def scratch(n):
        return [pltpu.SemaphoreType.DMA((7 * n,)), pltpu.SemaphoreType.DMA((7 * n,))]


def _rs_sum_chips(name, place, landed, part):
    _, half, cols = landed.shape

    def body(place_ref, l_ref, p_ref, o_ref):
        me = place_ref[0]
        own = p_ref[0].astype(f32)
        acc = jnp.where(me == 0, own, l_ref[0].astype(f32))
        for j in range(1, N_CHIPS):
            acc = acc + jnp.where(me == j, own, l_ref[j].astype(f32))
        o_ref[...] = acc

    return pl.pallas_call(
        body, name=f"rs_sum_chips_{name}",
        grid_spec=pltpu.PrefetchScalarGridSpec(
            num_scalar_prefetch=1, grid=(1,),
            in_specs=[pl.BlockSpec((N_CHIPS, half, cols), lambda t, place_ref: (0, 0, 0)),
                      pl.BlockSpec((1, half, cols), lambda t, place_ref: (place_ref[0], 0, 0))],
            out_specs=pl.BlockSpec((half, cols), lambda t, place_ref: (place_ref[1], 0))),
        out_shape=jax.ShapeDtypeStruct((2 * half, cols), f32),
        compiler_params=_cp("arbitrary"),
    )(place, landed, part)


def _grad_pair(shards):
    nk = len(shards)

    def body(*refs):
        outs = refs[nk:2 * nk]
        send_sems, recv_sems = refs[2 * nk:]
        x, y, c, _, _ = _place()

        def half(k, core):
            h = outs[k].shape[0] // 2
            return outs[k].at[pl.ds(core * h, h), :]

        cps = [_remote(half(k, c), half(k, c), send_sems.at[k], recv_sems.at[k], (x, y, 1 - c)) for k in range(nk)]
        for cp in cps:
            cp.start()
        for k in range(nk):
            _remote(half(k, 1 - c), half(k, 1 - c), send_sems.at[k], recv_sems.at[k], (x, y, 1 - c)).wait_recv()
        for cp in cps:
            cp.wait_send()

    return pl.pallas_call(
        body, name="grad_pair", in_specs=[ANY] * nk, out_specs=[ANY] * nk,
        out_shape=[jax.ShapeDtypeStruct(a.shape, f32) for a in shards],
        input_output_aliases={k: k for k in range(nk)},
        scratch_shapes=[pltpu.SemaphoreType.DMA((nk,)), pltpu.SemaphoreType.DMA((nk,))],
    )(*shards)


def _adam_math(w, g, m, v):
    m = ADAM_B1 * m + (1.0 - ADAM_B1) * g
    v = ADAM_B2 * v + (1.0 - ADAM_B2) * (g * g)
    m_hat = m / (1.0 - ADAM_B1 ** ADAM_STEP)
    v_hat = v / (1.0 - ADAM_B2 ** ADAM_STEP)
    delta = -ADAM_LR * (m_hat / (jnp.sqrt(v_hat) + ADAM_EPS) + ADAM_WD * w)
    return delta, m, v


def _adam(ws, gs, ms, vs):
    n = len(ws)
    steps = 2 * SHARD_STEPS

    def body(*refs):
        for k in range(n):
            w_ref, g_ref, m_ref, v_ref = (refs[j * n + k] for j in range(4))
            go_ref, d_ref, nm_ref, nv_ref = (refs[(4 + j) * n + k] for j in range(4))
            g = g_ref[...]
            go_ref[...] = g
            d_ref[...], nm_ref[...], nv_ref[...] = _adam_math(w_ref[...], g, m_ref[...], v_ref[...])

    specs = [_row(0, (w.shape[0] // steps, w.shape[1])) for w in ws]
    res = pl.pallas_call(
        body, name="adam_shards", grid=(steps,), in_specs=specs * 4, out_specs=specs * 4,
        out_shape=[jax.ShapeDtypeStruct(w.shape, f32) for w in ws] * 4,
        compiler_params=_cp("parallel"),
    )(*ws, *gs, *ms, *vs)
    return [tuple(res[j * n + k] for j in range(4)) for k in range(n)]


class _SmallGather:
    def __init__(self, parts, bufs, send_sems, recv_sems):
        self.parts, self.bufs, self.send_sems, self.recv_sems = parts, bufs, send_sems, recv_sems
        self.x, self.y, self.c, _, self.others = _place()
        self.sibling = (self.x, self.y, 1 - self.c)

    def _copy(self, a, k, block, to, src=None):
        slot = self.bufs[a].at[4 * block[0] + 2 * block[1] + block[2]]
        return _remote(slot if src is None else src, slot, self.send_sems.at[7 * a + k], self.recv_sems.at[7 * a + k],
                       to)

    def _first(self, a):
        me = (self.x, self.y, self.c)
        return [self._copy(a, 0, me, self.sibling, src=self.parts[a])] + [
            self._copy(a, 1 + j, me, (*chip, self.c), src=self.parts[a]) for j, chip in enumerate(self.others)]

    def _passed(self, a):
        return [self._copy(a, 4 + j, (*chip, self.c), self.sibling) for j, chip in enumerate(self.others)]

    @staticmethod
    def scratch(n):
        return [pltpu.SemaphoreType.DMA((7 * n,)), pltpu.SemaphoreType.DMA((7 * n,))]

    def start(self):
        for a in range(len(self.parts)):
            for cp in self._first(a):
                cp.start()

    def finish(self):
        sent = []
        for a in range(len(self.parts)):
            passed = self._passed(a)
            for j, chip in enumerate(self.others):
                self._copy(a, 1 + j, (*chip, self.c), self.sibling).wait_recv()
                passed[j].start()
            sent += self._first(a) + passed
        for a in range(len(self.parts)):
            self._copy(a, 0, self.sibling, self.sibling).wait_recv()
            for j, chip in enumerate(self.others):
                self._copy(a, 4 + j, (*chip, 1 - self.c), self.sibling).wait_recv()
        for cp in sent:
            cp.wait_send()


def _gathered_shapes(parts):
    return [jax.ShapeDtypeStruct((N_DEV,) + p.shape, p.dtype) for p in parts]


def _rs_sum_group(name, place, landed, parts, landed_all, grads_all, cross_parts, small_parts):
    nk, na, nx, ns = len(landed), len(landed_all), len(cross_parts), len(small_parts)
    dims = [a.shape[1:] for a in landed]
    dims_all = [a.shape[1:] for a in landed_all]

    def body(place_ref, *refs):
        take = iter(refs)
        l_refs, p_refs, la_refs, ga_refs, x_refs, sp_refs, o_refs, oa_refs, xl_refs, sbufs = (
            [next(take) for _ in range(cnt)] for cnt in (nk, nk, na, na, nx, ns, nk, na, nx, ns))
        sems = list(take)
        cross = _Cross(x_refs, xl_refs, sems[0], sems[1])
        small = _SmallGather(sp_refs, sbufs, sems[2], sems[3])
        t = pl.program_id(0)

        @pl.when(t == 0)
        def _():
            cross.start()
            small.start()

        me = place_ref[0]
        for l_ref, p_ref, o_ref in zip(l_refs, p_refs, o_refs):
            own = p_ref[0].astype(f32)
            acc = jnp.where(me == 0, own, l_ref[0].astype(f32))
            for j in range(1, N_CHIPS):
                acc = acc + jnp.where(me == j, own, l_ref[j].astype(f32))
            o_ref[...] = acc
        dev = 2 * me + place_ref[1]
        for l_ref, g_ref, o_ref in zip(la_refs, ga_refs, oa_refs):
            own = g_ref[0].astype(f32)
            acc = jnp.where(dev == 0, own, l_ref[0].astype(f32))
            for d in range(1, N_DEV):
                acc = acc + jnp.where(dev == d, own, l_ref[d].astype(f32))
            o_ref[...] = acc

        @pl.when(t == 1)
        def _():
            small.finish()
            cross.finish()

    def halves(h, c, lead, index):
        return pl.BlockSpec((lead, h // 2, c) if lead else (h // 2, c), index)

    in_specs = [halves(h, c, N_CHIPS, lambda t, pr: (0, t, 0)) for h, c in dims]
    in_specs += [halves(h, c, 1, lambda t, pr: (pr[0], t, 0)) for h, c in dims]
    in_specs += [halves(h, c, N_DEV, lambda t, pr: (0, t, 0)) for h, c in dims_all]
    in_specs += [halves(h, c, 1, lambda t, pr: (pr[0], 2 * pr[1] + t, 0)) for h, c in dims_all]
    in_specs += [ANY] * (nx + ns)
    out_specs = [halves(h, c, 0, lambda t, pr: (2 * pr[1] + t, 0)) for h, c in dims + dims_all] + [ANY] * (nx + ns)
    out_shape = [jax.ShapeDtypeStruct((2 * h, c), f32) for h, c in dims + dims_all]
    out_shape += [jax.ShapeDtypeStruct(a.shape, a.dtype) for a in cross_parts] + _gathered_shapes(small_parts)
    res = pl.pallas_call(
        body, name=f"rs_sum_{name}",
        grid_spec=pltpu.PrefetchScalarGridSpec(
            num_scalar_prefetch=1, grid=(2,), in_specs=in_specs, out_specs=out_specs,
            scratch_shapes=_Cross.scratch(nx) + _SmallGather.scratch(ns)),
        out_shape=out_shape, compiler_params=_cp("arbitrary"),
    )(place, *landed, *parts, *landed_all, *grads_all, *cross_parts, *small_parts)
    return res[:nk], res[nk:nk + na], res[nk + na:nk + na + nx], res[nk + na + nx:]


SMALL_PARAMS = ("g_ple_gate", "g_ple_post", "g_final", "g_ffn", "ln_g", "ln_b", "conv_b", "pool_scale", "pool_w",
                "conv_w", "g_mix")
SMALL_ROWS = {"g_ple_gate": (0, 0), "g_ple_post": (0, 1), "g_final": (0, 2), "g_ffn": (1, 0), "ln_g": (2, 0),
              "ln_b": (2, 1), "conv_b": (2, 2), "pool_scale": (2, 3), "g_mix": (5, 0)}
LOSS_ROW = (0, 3)


def _small_adam(place, gathered, parts, params):
    nb, names = len(parts), SMALL_PARAMS
    flat = [a for nm in names for a in params[nm]]

    def body(place_ref, *refs):
        b_refs, p_refs = refs[:nb], refs[nb:2 * nb]
        w_refs = refs[2 * nb:2 * nb + 3 * len(names)]
        outs = refs[2 * nb + 3 * len(names):]
        loss_ref, o_refs, cw_sum = outs[0], outs[1:1 + 4 * len(names)], outs[1 + 4 * len(names)]
        chip = place_ref[0]
        me = 2 * chip + place_ref[1]

        def total(blk, idx):
            own = p_refs[blk][idx]
            g = jnp.where(me == 0, own, b_refs[blk][(0,) + idx])
            for d in range(1, N_DEV):
                g = g + jnp.where(me == d, own, b_refs[blk][(d,) + idx])
            return g

        everything = (slice(None), slice(None))
        loss_ref[...] = total(LOSS_ROW[0], (pl.ds(LOSS_ROW[1], 1), pl.ds(0, 128)))
        d_cw = total(4, everything)
        mine = jnp.where(chip == 0, d_cw[:, 0:128], 0.0)
        for j in range(1, N_CHIPS):
            mine = mine + jnp.where(chip == j, d_cw[:, j * 128:(j + 1) * 128], 0.0)
        cw_sum[...] = mine
        for k, nm in enumerate(names):
            w_ref, m_ref, v_ref = w_refs[3 * k:3 * k + 3]
            g_ref, d_ref, nm_ref, nv_ref = o_refs[4 * k:4 * k + 4]
            if nm == "pool_w":
                g = total(3, everything + (slice(None),))
            elif nm == "conv_w":
                g = cw_sum[pl.ds(0, CONV_K), :]
            else:
                blk, row = SMALL_ROWS[nm]
                g = total(blk, (pl.ds(row, 1), slice(None)))
            g_ref[...] = g
            d_ref[...], nm_ref[...], nv_ref[...] = _adam_math(w_ref[...], g, m_ref[...], v_ref[...])

    whole = lambda a: pl.BlockSpec(a.shape, lambda t, pr: (0,) * a.ndim)
    out_shape = [jax.ShapeDtypeStruct((1, 128), f32)]
    out_shape += [jax.ShapeDtypeStruct(params[nm][0].shape, f32) for nm in names for _ in range(4)]
    res = pl.pallas_call(
        body, name="small_adam",
        grid_spec=pltpu.PrefetchScalarGridSpec(
            num_scalar_prefetch=1, grid=(1,),
            in_specs=[whole(a) for a in list(gathered) + list(parts) + flat],
            out_specs=[whole(s) for s in out_shape], scratch_shapes=[pltpu.VMEM((HALO, 128), f32)]),
        out_shape=out_shape, compiler_params=_cp("arbitrary"),
    )(place, *gathered, *parts, *flat)
    return res[0], {nm: res[1 + 4 * k:5 + 4 * k] for k, nm in enumerate(names)}


def _pad_rows(a, rows):
    return jnp.concatenate([a, jnp.zeros((rows - a.shape[0],) + a.shape[1:], a.dtype)], axis=0)


def kernel(x, p, g_mix, w_in, conv_w, conv_b, ln_g, ln_b, pool_w, pool_scale, w_out, g_ffn, w_gate_up, w_down, g_ple_gate, w_ple_gate, w_ple_up, g_ple_post, g_final, loss_target, m_g_mix, m_w_in, m_conv_w, m_conv_b, m_ln_g, m_ln_b, m_pool_w, m_pool_scale, m_w_out, m_g_ffn, m_w_gate_up, m_w_down, m_g_ple_gate, m_w_ple_gate, m_w_ple_up, m_g_ple_post, m_g_final, v_g_mix, v_w_in, v_conv_w, v_conv_b, v_ln_g, v_ln_b, v_pool_w, v_pool_scale, v_w_out, v_g_ffn, v_w_gate_up, v_w_down, v_g_ple_gate, v_w_ple_gate, v_w_ple_up, v_g_ple_post, v_g_final):
    seq = x.shape[1]
    me = 2 * lax.axis_index("x") + lax.axis_index("y")
    chip = me.astype(jnp.int32).reshape(1)
    core = lax.axis_index("c").astype(jnp.int32).reshape(1)
    place = jnp.concatenate([chip, core])
    xs, ps, ts = x.reshape(seq, D_MODEL), p.reshape(seq, D_PLE), loss_target.reshape(seq, D_MODEL)

    big = [w_in[0], w_gate_up[0], w_out[0], w_down[0], w_ple_gate[0], w_ple_up[0]]
    big_m = [m_w_in[0], m_w_gate_up[0], m_w_out[0], m_w_down[0], m_w_ple_gate[0], m_w_ple_up[0]]
    big_v = [v_w_in[0], v_w_gate_up[0], v_w_out[0], v_w_down[0], v_w_ple_gate[0], v_w_ple_up[0]]
    b_in, b_gu, b_out, b_down, b_pg, b_pu, b_cw = _cast_into_slots(
        chip, big + [_pad_rows(conv_w[0], HALO)], [bf16] * len(big) + [f32])
    xi, yi = lax.axis_index("x"), lax.axis_index("y")
    order = jnp.stack([me, 2 * (1 - xi) + yi, 2 * xi + 1 - yi, 2 * (1 - xi) + 1 - yi]).astype(jnp.int32)

    z, (w_in_g, cw_g, w_out_g) = _mix_in(xs, g_mix, order, [b_in, b_cw, b_out])
    conv_w_f = cw_g.transpose(1, 0, 2).reshape(HALO, C_CONV)
    w_out_f = w_out_g.reshape(D_MODEL, D_MODEL)
    (x1, mix, u1, pooled, h2), (w_gu_g,) = _conv_pool_out(z, xs, conv_w_f, conv_b, ln_g, ln_b, pool_w[0], pool_scale,
                                                          w_out_f, g_ffn, [b_gu])
    (gu, ffn_f), (w_down_g, w_pg_g, w_pu_g) = _ffn_up(h2, w_gu_g, [b_down, b_pg, b_pu])
    w_down_f = w_down_g.reshape(D_FF, D_MODEL)
    x2 = _ffn_down(x1, ffn_f, w_down_f)
    w_pg_f = w_pg_g.reshape(D_MODEL, D_MODEL)
    dx2, d_w_pg, d_w_pu, small_ple = _ple_loss(x2, ps, ts, g_ple_gate, g_ple_post, g_final.reshape(1, D_MODEL),
                                               w_pg_f, w_pu_g)
    d_w_down = _ffn_bwd_dw_down(ffn_f, dx2)
    grads_a = [d_w_down.reshape(N_CHIPS, -1, D_MODEL), d_w_pg.reshape(N_CHIPS, -1, D_MODEL), d_w_pu]
    (dx1, dgu, small_ffn), landed_a = _ffn_bwd_dx(dx2, x1, gu, g_ffn, w_gu_g, w_down_f, grads_a)
    d_w_gu = _ffn_bwd_dw_gu(h2, dgu)
    du1, dpo, d_w_out, d_pool_w, small_mix = _mix_bwd_local(dx1, mix, u1, pooled, w_out_f, ln_g, ln_b, pool_w[0],
                                                             pool_scale)
    parts_b = _pair_reduce("b", [d_w_gu, d_w_out.reshape(N_CHIPS, -1, D_MODEL)])
    small_0 = [small_ple, small_ffn, small_mix, d_pool_w]
    (grad_x, d_w_in, d_conv_w, small_in), landed_b, small_all_0 = _in_bwd(du1, dpo, z, xs, dx1, conv_w_f, g_mix,
                                                                            w_in_g, parts_b, small_0)
    parts_c = _pair_reduce("c", [d_w_in])
    small_1 = [d_conv_w, small_in]
    (h_gu, h_out), (h_down, h_pg, h_pu), landed_c, small_all_1 = _rs_sum_group(
        "ab", place, landed_b, parts_b, landed_a, grads_a, parts_c, small_1)
    h_in = _rs_sum_chips("w_in", place, landed_c[0], parts_c[0])
    big_g = _grad_pair([h_in, h_gu, h_out, h_down, h_pg, h_pu])
    big_upd = _adam(big, big_g, big_m, big_v)

    p3 = lambda w, m, v: (w, m, v)
    row = lambda a: a.reshape(1, D_MODEL)
    params = dict(
        g_ple_gate=p3(g_ple_gate, m_g_ple_gate, v_g_ple_gate), g_ple_post=p3(g_ple_post, m_g_ple_post, v_g_ple_post),
        g_final=p3(row(g_final), row(m_g_final), row(v_g_final)), g_ffn=p3(g_ffn, m_g_ffn, v_g_ffn),
        ln_g=p3(ln_g, m_ln_g, v_ln_g), ln_b=p3(ln_b, m_ln_b, v_ln_b), conv_b=p3(conv_b, m_conv_b, v_conv_b),
        pool_scale=p3(pool_scale, m_pool_scale, v_pool_scale), pool_w=p3(pool_w[0], m_pool_w[0], v_pool_w[0]),
        conv_w=p3(conv_w[0], m_conv_w[0], v_conv_w[0]), g_mix=p3(g_mix, m_g_mix, v_g_mix))
    loss, small = _small_adam(place, list(small_all_0) + list(small_all_1), small_0 + small_1, params)
    back = dict(g_final=lambda a: a.reshape(D_MODEL), pool_w=lambda a: a[None], conv_w=lambda a: a[None])

    names = ["g_mix", "w_in", "conv_w", "conv_b", "ln_g", "ln_b", "pool_w", "pool_scale", "w_out", "g_ffn",
             "w_gate_up", "w_down", "g_ple_gate", "w_ple_gate", "w_ple_up", "g_ple_post", "g_final"]
    big_at = {"w_in": 0, "w_gate_up": 1, "w_out": 2, "w_down": 3, "w_ple_gate": 4, "w_ple_up": 5}
    out = [loss[0, 0], grad_x.reshape(1, seq, D_MODEL)]
    for kind in range(4):
        for nm in names:
            if nm in big_at:
                out.append(big_upd[big_at[nm]][kind][None])
            else:
                out.append(back.get(nm, lambda a: a)(small[nm][kind]))
    return tuple(out)
```

```python
import functools

import jax
import jax.numpy as jnp
from jax import lax
from jax.experimental import pallas as pl
from jax.experimental.pallas import tpu as pltpu

f32, bf16 = jnp.float32, jnp.bfloat16

EPS = 1e-6
D_MODEL = 1024
C_CONV = 512
C_POOL = 512
POOL_WINDOWS = (2, 4, 8, 16)
POOL_GROUP = 128
CONV_K = 31
D_FF = 2816
D_PLE = 256
N_CHIPS = 4
N_DEV = 8
W_IN_COLS = 2 * C_CONV + C_POOL
W_IN_CHUNK = W_IN_COLS // N_CHIPS
FF_CHUNK = 2 * D_FF // N_CHIPS
PLE_CHUNK = D_MODEL // N_CHIPS
HALO = 32
ROW_TILE = 512
SHIFT_PAD = 32
SHIFT_GROUPS = 16
FF_SUB = (0, 512, 1024, FF_CHUNK)
VMEM_LIMIT = 56 * 1024 * 1024

ADAM_LR = 0.001
ADAM_B1 = 0.9
ADAM_B2 = 0.999
ADAM_EPS = 1e-08
ADAM_WD = 0.01
ADAM_STEP = 10

MESH = pl.DeviceIdType.MESH
ANY = pl.BlockSpec(memory_space=pl.ANY)
VMEM = pl.BlockSpec(memory_space=pltpu.VMEM)


def _cp(*sem):
    return pltpu.CompilerParams(dimension_semantics=sem, vmem_limit_bytes=VMEM_LIMIT)


def _dot(a, b):
    return jnp.dot(a, b, preferred_element_type=f32)


def _dot_nt(a, b):
    return lax.dot_general(a, b, (((1,), (1,)), ((), ())), preferred_element_type=f32)


def _dot_tn(a, b):
    return lax.dot_general(a, b, (((0,), (0,)), ((), ())), preferred_element_type=f32)


def _sigmoid(v):
    return jax.nn.sigmoid(v)


def _rms_fwd(v, g):
    r = lax.rsqrt(jnp.mean(v * v, axis=-1, keepdims=True) + EPS)
    vh = v * r
    return vh * g, vh, r


def _rms_bwd(dy, vh, r, g):
    dvh = dy * g
    dv = r * (dvh - vh * jnp.mean(dvh * vh, axis=-1, keepdims=True))
    return dv, jnp.sum(dy * vh, axis=0, keepdims=True)


def _silu_grad(v, s):
    return s * (1.0 + v * (1.0 - s))


def _row(i, n):
    return pl.BlockSpec((n[0], n[1]), lambda *a: (a[i], 0))


def _full(shape):
    nd = len(shape)
    return pl.BlockSpec(shape, lambda *a: (0,) * nd)


def _place():
    x, y, c = lax.axis_index("x"), lax.axis_index("y"), lax.axis_index("c")
    others = [(1 - x, y), (x, 1 - y), (1 - x, 1 - y)]
    return x, y, c, 2 * x + y, others


def _remote(src, dst, send_sem, recv_sem, dev):
    return pltpu.make_async_remote_copy(src_ref=src, dst_ref=dst, send_sem=send_sem, recv_sem=recv_sem,
                                        device_id=dev, device_id_type=MESH)


SHARD_STEPS = 4


def _cast_into_slots(me, ws, dtypes):
    n = len(ws)

    def body(me_ref, *refs):
        for w_ref, o_ref, dtype in zip(refs[:n], refs[n:], dtypes):
            o_ref[0] = w_ref[...].astype(dtype)

    return pl.pallas_call(
        body, name="cast_shards",
        grid_spec=pltpu.PrefetchScalarGridSpec(
            num_scalar_prefetch=1, grid=(SHARD_STEPS,),
            in_specs=[pl.BlockSpec((w.shape[0] // SHARD_STEPS, w.shape[1]), lambda r, me_ref: (r, 0)) for w in ws],
            out_specs=[pl.BlockSpec((1, w.shape[0] // SHARD_STEPS, w.shape[1]), lambda r, me_ref: (me_ref[0], r, 0))
                       for w in ws]),
        out_shape=[jax.ShapeDtypeStruct((N_CHIPS,) + w.shape, dt) for w, dt in zip(ws, dtypes)],
        compiler_params=_cp("parallel"),
    )(me, *ws)


class _Gather:
    def __init__(self, bufs, send_sems, recv_sems):
        self.bufs, self.send_sems, self.recv_sems = bufs, send_sems, recv_sems
        self.x, self.y, self.c, self.me, self.others = _place()
        self.halves = [b.shape[1] // 2 for b in bufs]

    def _piece(self, k, chip, half):
        return self.bufs[k].at[chip, pl.ds(half * self.halves[k], self.halves[k]), :]

    def _ici(self, k, j, chip):
        ox, oy = self.others[j]
        piece = self._piece(k, chip, self.c)
        return _remote(piece, piece, self.send_sems.at[6 * k + j], self.recv_sems.at[6 * k + j], (ox, oy, self.c))

    def _relay(self, k):
        first = self.c == 0
        piece = self._piece(k, jnp.where(first, self.chip(0), self.chip(1)), self.c)
        to = (jnp.where(first, self.others[1][0], self.others[0][0]),
              jnp.where(first, self.others[1][1], self.others[0][1]), self.c)
        return _remote(piece, piece, self.send_sems.at[6 * k + 2], self.recv_sems.at[6 * k + 2], to)

    def _pair(self, k, j, half):
        ox, oy = self.others[j]
        piece = self._piece(k, 2 * ox + oy, half)
        return _remote(piece, piece, self.send_sems.at[6 * k + 3 + j], self.recv_sems.at[6 * k + 3 + j],
                       (self.x, self.y, 1 - self.c))

    def _each(self, ks=None):
        return [(k, j) for k in (range(len(self.bufs)) if ks is None else ks) for j in range(3)]

    def chip(self, j):
        ox, oy = self.others[j]
        return 2 * ox + oy

    def start(self):
        for k in range(len(self.bufs)):
            for j in range(2):
                self._ici(k, j, self.me).start()

    def forward(self, pairs=None):
        for k, j in self._each() if pairs is None else pairs:
            self._ici(k, j, self.chip(j)).wait_recv()
            self._pair(k, j, self.c).start()
            if j < 2:
                pl.when(self.c == j)(self._relay(k).start)

    def landed(self, pairs):
        for k, j in pairs:
            self._pair(k, j, 1 - self.c).wait_recv()

    def finish(self, ks=None):
        self.landed(self._each(ks))
        for k in range(len(self.bufs)):
            for j in range(2):
                self._ici(k, j, self.me).wait_send()
            self._relay(k).wait_send()
            for j in range(3):
                self._pair(k, j, self.c).wait_send()

    @staticmethod
    def scratch(n):
        return [pltpu.SemaphoreType.DMA((6 * n,)), pltpu.SemaphoreType.DMA((6 * n,))]


def _carried(bufs):
    n = len(bufs)
    return dict(in_specs=[ANY] * n, out_specs=[ANY] * n,
                out_shape=[jax.ShapeDtypeStruct(b.shape, b.dtype) for b in bufs], scratch=_Gather.scratch(n))


def _mix_in(x, g_mix, order, carry):
    s = x.shape[0]
    tm = min(2 * ROW_TILE, s)
    n = s // tm
    nc = len(carry)
    cs = _carried(carry)

    def body(order_ref, x_ref, g_ref, *refs):
        z_ref = refs[nc]
        bufs = refs[nc + 1:2 * nc + 1]
        h_ref, w_ref, w_sem = refs[2 * nc + 1:2 * nc + 4]
        gather = _Gather(bufs, *refs[2 * nc + 4:])
        q, i = pl.program_id(0), pl.program_id(1)
        first = i == 0
        pl.when(jnp.logical_and(q == 0, first))(gather.start)
        for j in range(3):

            @pl.when(jnp.logical_and(q == j + 1, first))
            def _():
                gather.forward([(0, j)])
                gather.landed([(0, j)])
                if j == 1:
                    gather.forward([(k, jj) for k in range(1, nc) for jj in range(2)])

        @pl.when(first)
        def _():
            load = pltpu.make_async_copy(bufs[0].at[order_ref[q]], w_ref, w_sem)
            load.start()
            load.wait()

        @pl.when(q == 0)
        def _():
            h, _, _ = _rms_fwd(x_ref[...], g_ref[...])
            h_ref[i] = h.astype(bf16)

        z_ref[...] = _dot(h_ref[i], w_ref[...])

        @pl.when(jnp.logical_and(q == N_CHIPS - 1, i == n - 1))
        def _():
            gather.forward([(k, 2) for k in range(1, nc)])
            gather.finish(range(1, nc))

    res = pl.pallas_call(
        body, name="mix_in",
        grid_spec=pltpu.PrefetchScalarGridSpec(
            num_scalar_prefetch=1, grid=(N_CHIPS, n),
            in_specs=[pl.BlockSpec((tm, D_MODEL), lambda q, i, order_ref: (jnp.where(q == 0, i, 0), 0)),
                      pl.BlockSpec((1, D_MODEL), lambda q, i, order_ref: (0, 0))] + cs["in_specs"],
            out_specs=[pl.BlockSpec((tm, W_IN_CHUNK), lambda q, i, order_ref: (i, order_ref[q]))] + cs["out_specs"],
            scratch_shapes=[pltpu.VMEM((n, tm, D_MODEL), bf16), pltpu.VMEM((D_MODEL, W_IN_CHUNK), bf16),
                            pltpu.SemaphoreType.DMA(())] + cs["scratch"]),
        out_shape=[jax.ShapeDtypeStruct((s, W_IN_COLS), f32)] + cs["out_shape"],
        input_output_aliases={3 + k: 1 + k for k in range(nc)},
        compiler_params=_cp("arbitrary", "arbitrary"),
    )(order, x, g_mix, *carry)
    return res[0], res[1:]


def _pool_counts(tm, w, first_row):
    t1 = (lax.broadcasted_iota(jnp.int32, (tm, 1), 0) + first_row + 1).astype(f32)
    return jnp.minimum(t1, float(w))


def _conv_pool_out(z, x, conv_w, conv_b, ln_g, ln_b, pool_w, pool_scale, w_out, g_ffn, carry):
    s = x.shape[0]
    tm = min(ROW_TILE, s)
    n = s // tm
    hb = tm // HALO
    nv = tm // 8
    assert nv >= SHIFT_PAD and nv % SHIFT_GROUPS == 0
    nc = len(carry)
    cs = _carried(carry)

    def body(z_ref, zp_ref, x_ref, cw_ref, cb_ref, lg_ref, lb_ref, pw_ref, ps_ref, wo_ref, gf_ref, *refs):
        x1_ref, mix_ref, u1_ref, pooled_ref, h2_ref = refs[nc:nc + 5]
        eu, ev, ss, u0_ref = refs[2 * nc + 5:2 * nc + 9]
        gather = _Gather(refs[nc + 5:2 * nc + 5], *refs[2 * nc + 9:])
        i = pl.program_id(0)
        pl.when(i == 0)(gather.start)
        for k in range(nc):
            pl.when(i == min(3 * n // 4 + k, n - 1))(functools.partial(gather.forward, [(k, 0), (k, 1)]))
        keep = (i > 0).astype(f32)
        zp = zp_ref[...] * keep
        u0_prev = zp[:, :C_CONV] * _sigmoid(zp[:, C_CONV:2 * C_CONV])
        u0_ref[...] = z_ref[:, :C_CONV] * _sigmoid(z_ref[:, C_CONV:2 * C_CONV])
        for c, w_pool in enumerate(POOL_WINDOWS):
            lanes = slice(c * 128, (c + 1) * 128)
            v_lanes = slice(2 * C_CONV + c * 128, 2 * C_CONV + (c + 1) * 128)
            eu[c, pl.ds(0, SHIFT_PAD, stride=8), :] = u0_prev[:, lanes]
            ev[c, pl.ds(0, SHIFT_PAD, stride=8), :] = zp[:, v_lanes]
            for j in range(8):
                rows = slice(j * nv, (j + 1) * nv)
                eu[c, pl.ds(SHIFT_PAD * 8 + j, nv, stride=8), :] = u0_ref[rows, lanes]
                ev[c, pl.ds(SHIFT_PAD * 8 + j, nv, stride=8), :] = z_ref[rows, v_lanes]
                if j >= 1:
                    edge = slice(j * nv - SHIFT_PAD, j * nv)
                    eu[c, pl.ds(j, SHIFT_PAD, stride=8), :] = u0_ref[edge, lanes]
                    ev[c, pl.ds(j, SHIFT_PAD, stride=8), :] = z_ref[edge, v_lanes]
            for v0 in range(0, nv, SHIFT_GROUPS):
                span = SHIFT_GROUPS * 8
                acc = jnp.zeros((span, 128), f32) + cb_ref[:, lanes]
                for k in range(CONV_K):
                    acc = acc + cw_ref[pl.ds(k, 1), lanes] * eu[c, pl.ds((SHIFT_PAD - (CONV_K - 1) + v0 + k) * 8, span), :]
                ss[0, v0 * 8:v0 * 8 + span, :] = acc
                acc = ev[c, pl.ds((SHIFT_PAD + v0) * 8, span), :]
                for d in range(1, w_pool):
                    acc = acc + ev[c, pl.ds((SHIFT_PAD + v0 - d) * 8, span), :]
                ss[1, v0 * 8:v0 * 8 + span, :] = acc
            for j in range(8):
                rows = slice(j * nv, (j + 1) * nv)
                u1_ref[rows, lanes] = ss[0, pl.ds(j, nv, stride=8), :]
                mean = ss[1, pl.ds(j, nv, stride=8), :] / _pool_counts(nv, w_pool, i * tm + j * nv)
                pooled_ref[rows, lanes] = (mean - z_ref[rows, v_lanes]).astype(bf16)
        u1 = u1_ref[...]
        mu = jnp.mean(u1, axis=-1, keepdims=True)
        uc = u1 - mu
        rstd = lax.rsqrt(jnp.mean(uc * uc, axis=-1, keepdims=True) + EPS)
        u2 = uc * rstd * lg_ref[...] + lb_ref[...]
        mix_ref[:, :C_CONV] = (u2 * _sigmoid(u2)).astype(bf16)
        for g in range(len(POOL_WINDOWS)):
            cols = slice(g * POOL_GROUP, (g + 1) * POOL_GROUP)
            mixed = _dot(pooled_ref[:, cols], pw_ref[g].astype(bf16))
            mix_ref[:, C_CONV + g * POOL_GROUP:C_CONV + (g + 1) * POOL_GROUP] = (mixed * ps_ref[:, cols]).astype(bf16)
        x1 = x_ref[...] + _dot(mix_ref[...], wo_ref[...])
        x1_ref[...] = x1
        h2_ref[...] = _rms_fwd(x1, gf_ref[...])[0].astype(bf16)
        @pl.when(i == n - 1)
        def _():
            gather.forward([(k, 2) for k in range(nc)])
            gather.finish()

    res = pl.pallas_call(
        body, name="conv_pool_out", grid=(n,),
        in_specs=[_row(0, (tm, W_IN_COLS)),
                  pl.BlockSpec((HALO, W_IN_COLS), lambda i: (jnp.maximum(i * hb - 1, 0), 0)),
                  _row(0, (tm, D_MODEL)), _full((HALO, C_CONV)), _full((1, C_CONV)), _full((1, C_CONV)),
                  _full((1, C_CONV)), _full((4, POOL_GROUP, POOL_GROUP)), _full((1, C_POOL)),
                  _full((D_MODEL, D_MODEL)), _full((1, D_MODEL))] + cs["in_specs"],
        out_specs=[_row(0, (tm, D_MODEL)), _row(0, (tm, D_MODEL)), _row(0, (tm, C_CONV)), _row(0, (tm, C_POOL)),
                   _row(0, (tm, D_MODEL))] + cs["out_specs"],
        out_shape=[jax.ShapeDtypeStruct((s, D_MODEL), f32), jax.ShapeDtypeStruct((s, D_MODEL), bf16),
                   jax.ShapeDtypeStruct((s, C_CONV), f32), jax.ShapeDtypeStruct((s, C_POOL), bf16),
                   jax.ShapeDtypeStruct((s, D_MODEL), bf16)] + cs["out_shape"],
        input_output_aliases={11 + k: 5 + k for k in range(nc)},
        scratch_shapes=[pltpu.VMEM((4, (SHIFT_PAD + nv) * 8, 128), f32), pltpu.VMEM((4, (SHIFT_PAD + nv) * 8, 128), f32),
                        pltpu.VMEM((2, tm, 128), f32), pltpu.VMEM((tm, C_CONV), f32)] + cs["scratch"],
        compiler_params=_cp("arbitrary"),
    )(z, z, x, conv_w, conv_b, ln_g, ln_b, pool_w, pool_scale, w_out, g_ffn, *carry)
    return res[:5], res[5:]


def _ffn_up(h2, w_gu_g, carry):
    s = h2.shape[0]
    tm = min(2 * ROW_TILE, s)
    n = s // tm
    nc = len(carry)
    cs = _carried(carry)

    def body(h2_ref, wg_ref, wu_ref, *refs):
        gu_ref, f_ref = refs[nc:nc + 2]
        gather = _Gather(refs[nc + 2:2 * nc + 2], *refs[2 * nc + 2:])
        c, i = pl.program_id(0), pl.program_id(1)
        pl.when(jnp.logical_and(i == 0, c == 0))(gather.start)
        direct = [(k, j) for k in range(nc) for j in range(2)]
        pl.when(jnp.logical_and(i == n - 1, c == 0))(functools.partial(gather.forward, direct))
        pl.when(jnp.logical_and(i == n // 2, c == 1))(functools.partial(gather.forward, [(k, 2) for k in range(nc)]))
        h = h2_ref[...]
        for lo, hi in zip(FF_SUB[:-1], FF_SUB[1:]):
            gate = _dot(h, wg_ref[0, :, lo:hi])
            up = _dot(h, wu_ref[0, :, lo:hi])
            gu_ref[0, :, lo:hi] = gate.astype(bf16)
            gu_ref[1, :, lo:hi] = up.astype(bf16)
            f_ref[:, lo:hi] = (gate * _sigmoid(gate) * up).astype(bf16)
        pl.when(jnp.logical_and(i == n - 1, c == 1))(gather.finish)

    res = pl.pallas_call(
        body, name="ffn_up", grid=(2, n),
        in_specs=[_row(1, (tm, D_MODEL)),
                  pl.BlockSpec((1, D_MODEL, FF_CHUNK), lambda c, i: (c, 0, 0)),
                  pl.BlockSpec((1, D_MODEL, FF_CHUNK), lambda c, i: (2 + c, 0, 0))] + cs["in_specs"],
        out_specs=[pl.BlockSpec((2, tm, FF_CHUNK), lambda c, i: (0, i, c)),
                   pl.BlockSpec((tm, FF_CHUNK), lambda c, i: (i, c))] + cs["out_specs"],
        out_shape=[jax.ShapeDtypeStruct((2, s, D_FF), bf16), jax.ShapeDtypeStruct((s, D_FF), bf16)] + cs["out_shape"],
        input_output_aliases={3 + k: 2 + k for k in range(nc)},
        scratch_shapes=cs["scratch"],
        compiler_params=_cp("arbitrary", "arbitrary"),
    )(h2, w_gu_g, w_gu_g, *carry)
    return res[:2], res[2:]


def _ffn_down(x1, f, w_down):
    s = x1.shape[0]
    tm = min(2 * ROW_TILE, s)

    def body(x1_ref, f_ref, wd_ref, x2_ref):
        x2_ref[...] = x1_ref[...] + _dot(f_ref[...], wd_ref[...])

    return pl.pallas_call(
        body, name="ffn_down", grid=(s // tm,),
        in_specs=[_row(0, (tm, D_MODEL)), _row(0, (tm, D_FF)), _full((D_FF, D_MODEL))],
        out_specs=_row(0, (tm, D_MODEL)), out_shape=jax.ShapeDtypeStruct((s, D_MODEL), f32),
        compiler_params=_cp("parallel"),
    )(x1, f, w_down)


def _ple_loss(x2, p, target, g_pg, g_post, g_final, w_pg, w_pu_g):
    s = x2.shape[0]
    tm = min(ROW_TILE, s)
    n = s // tm

    def body(x2_ref, p_ref, t_ref, gpg_ref, gpo_ref, gf_ref, wpg_ref, wpu_ref,
             dx2_ref, dwpg_ref, dwpu_ref, small_ref, apg_ref, apu_ref):
        i = pl.program_id(0)

        @pl.when(i == 0)
        def _():
            apg_ref[...] = jnp.zeros_like(apg_ref)
            apu_ref[...] = jnp.zeros_like(apu_ref)
            small_ref[...] = jnp.zeros_like(small_ref)

        x2 = x2_ref[...]
        h3, x2h, r2 = _rms_fwd(x2, gpg_ref[...])
        h3b = h3.astype(bf16)
        gate = _sigmoid(_dot(h3b, wpg_ref[...]))
        pb = p_ref[...].astype(bf16)
        pe = jnp.concatenate([_dot(pb, wpu_ref[j]) for j in range(N_CHIPS)], axis=-1)
        e, peh, rp = _rms_fwd(pe, gpo_ref[...])
        x3 = x2 + gate * e
        y, x3h, r3 = _rms_fwd(x3, gf_ref[...])
        d = y - t_ref[...]
        loss = 0.5 * jnp.sum(jnp.sum(d * d, axis=-1, keepdims=True) * (1.0 / D_MODEL), axis=0, keepdims=True)
        dx3, dgf = _rms_bwd(d * (1.0 / D_MODEL), x3h, r3, gf_ref[...])
        dpe, dgpo = _rms_bwd(dx3 * gate, peh, rp, gpo_ref[...])
        dgl = (dx3 * e * gate * (1.0 - gate)).astype(bf16)
        apg_ref[...] += _dot_tn(h3b, dgl)
        apu_ref[...] += _dot_tn(pb, dpe.astype(bf16))
        dh3 = _dot_nt(dgl, wpg_ref[...])
        dx2b, dgpg = _rms_bwd(dh3, x2h, r2, gpg_ref[...])
        dx2_ref[...] = dx3 + dx2b
        small_ref[0:1, :] += dgpg
        small_ref[1:2, :] += dgpo
        small_ref[2:3, :] += dgf
        small_ref[3:4, :] += jnp.broadcast_to(loss, (1, D_MODEL))

        @pl.when(i == n - 1)
        def _():
            dwpg_ref[...] = apg_ref[...].astype(bf16)
            for j in range(N_CHIPS):
                dwpu_ref[j] = apu_ref[:, j * PLE_CHUNK:(j + 1) * PLE_CHUNK].astype(bf16)

    return pl.pallas_call(
        body, name="ple_loss", grid=(n,),
        in_specs=[_row(0, (tm, D_MODEL)), _row(0, (tm, D_PLE)), _row(0, (tm, D_MODEL)),
                  _full((1, D_MODEL)), _full((1, D_MODEL)), _full((1, D_MODEL)),
                  _full((D_MODEL, D_MODEL)), _full((N_CHIPS, D_PLE, PLE_CHUNK))],
        out_specs=[_row(0, (tm, D_MODEL)), _full((D_MODEL, D_MODEL)), _full((N_CHIPS, D_PLE, PLE_CHUNK)),
                   _full((8, D_MODEL))],
        out_shape=[jax.ShapeDtypeStruct((s, D_MODEL), f32), jax.ShapeDtypeStruct((D_MODEL, D_MODEL), bf16),
                   jax.ShapeDtypeStruct((N_CHIPS, D_PLE, PLE_CHUNK), bf16), jax.ShapeDtypeStruct((8, D_MODEL), f32)],
        scratch_shapes=[pltpu.VMEM((D_MODEL, D_MODEL), f32), pltpu.VMEM((D_PLE, D_MODEL), f32)],
        compiler_params=_cp("arbitrary"),
    )(x2, p, target, g_pg, g_post, g_final, w_pg, w_pu_g)


def _crossed(parts):
    n = len(parts)
    return dict(in_specs=[ANY] * n, out_specs=[ANY] * n,
                out_shape=[jax.ShapeDtypeStruct(a.shape, a.dtype) for a in parts], scratch=_Cross.scratch(n))


def _ffn_bwd_dx(dx2, x1, gu, g_ffn, w_gu_g, w_down, grads):
    s = x1.shape[0]
    tm = min(ROW_TILE, s)
    n = s // tm
    nc = len(grads)
    cs = dict(in_specs=[ANY] * nc, out_specs=[ANY] * nc, scratch=_CrossAll.scratch(nc),
              out_shape=[jax.ShapeDtypeStruct((N_DEV, g.shape[1] // 2, g.shape[2]), g.dtype) for g in grads])

    def body(dx2_ref, x1_ref, gu_ref, g_ref, wg_ref, wu_ref, wd_ref, *refs):
        dx1_ref, dgu_ref, small_ref = refs[nc:nc + 3]
        acc_ref = refs[2 * nc + 3]
        cross = _CrossAll(refs[:nc], refs[nc + 3:2 * nc + 3], *refs[2 * nc + 4:])
        i, c = pl.program_id(0), pl.program_id(1)
        pl.when(jnp.logical_and(i == 0, c == 0))(cross.start)

        @pl.when(jnp.logical_and(i == 0, c == 0))
        def _():
            small_ref[...] = jnp.zeros_like(small_ref)

        @pl.when(c == 0)
        def _():
            acc_ref[...] = jnp.zeros_like(acc_ref)

        dyb = dx2_ref[...].astype(bf16)
        for lo, hi in zip(FF_SUB[:-1], FF_SUB[1:]):
            df = _dot_nt(dyb, wd_ref[lo:hi, :])
            gate = gu_ref[0, :, lo:hi].astype(f32)
            up = gu_ref[1, :, lo:hi].astype(f32)
            sg = _sigmoid(gate)
            dgate = (df * up * _silu_grad(gate, sg)).astype(bf16)
            dup = (df * gate * sg).astype(bf16)
            dgu_ref[0, :, lo:hi] = dgate
            dgu_ref[1, :, lo:hi] = dup
            acc_ref[...] += _dot_nt(dgate, wg_ref[0, :, lo:hi]) + _dot_nt(dup, wu_ref[0, :, lo:hi])

        @pl.when(c == 1)
        def _():
            _, x1h, r1 = _rms_fwd(x1_ref[...], g_ref[...])
            dx1b, dg = _rms_bwd(acc_ref[...], x1h, r1, g_ref[...])
            dx1_ref[...] = dx2_ref[...] + dx1b
            small_ref[0:1, :] += dg

        pl.when(jnp.logical_and(i == n - 1, c == 1))(cross.finish)

    res = pl.pallas_call(
        body, name="ffn_bwd_dx", grid=(n, 2),
        in_specs=[_row(0, (tm, D_MODEL)), _row(0, (tm, D_MODEL)),
                  pl.BlockSpec((2, tm, FF_CHUNK), lambda i, c: (0, i, c)), _full((1, D_MODEL)),
                  pl.BlockSpec((1, D_MODEL, FF_CHUNK), lambda i, c: (c, 0, 0)),
                  pl.BlockSpec((1, D_MODEL, FF_CHUNK), lambda i, c: (2 + c, 0, 0)),
                  pl.BlockSpec((FF_CHUNK, D_MODEL), lambda i, c: (c, 0))] + cs["in_specs"],
        out_specs=[_row(0, (tm, D_MODEL)), pl.BlockSpec((2, tm, FF_CHUNK), lambda i, c: (0, i, c)),
                   _full((8, D_MODEL))] + cs["out_specs"],
        out_shape=[jax.ShapeDtypeStruct((s, D_MODEL), f32), jax.ShapeDtypeStruct((2, s, D_FF), bf16),
                   jax.ShapeDtypeStruct((8, D_MODEL), f32)] + cs["out_shape"],
        scratch_shapes=[pltpu.VMEM((tm, D_MODEL), f32)] + cs["scratch"],
        compiler_params=_cp("arbitrary", "arbitrary"),
    )(dx2, x1, gu, g_ffn, w_gu_g, w_gu_g, w_down, *grads)
    return res[:3], res[3:]


def _ffn_bwd_dw_gu(h2, dgu):
    s = h2.shape[0]
    ts = min(2 * ROW_TILE, s)
    n = s // ts

    def body(h_ref, d_ref, o_ref, acc_ref):
        t = pl.program_id(1)

        @pl.when(t == 0)
        def _():
            acc_ref[...] = jnp.zeros_like(acc_ref)

        acc_ref[...] += _dot_tn(h_ref[...], d_ref[0])

        @pl.when(t == n - 1)
        def _():
            o_ref[0] = acc_ref[...].astype(bf16)

    return pl.pallas_call(
        body, name="ffn_bwd_dw_gu", grid=(N_CHIPS, n),
        in_specs=[pl.BlockSpec((ts, D_MODEL), lambda j, t: (t, 0)),
                  pl.BlockSpec((1, ts, FF_CHUNK), lambda j, t: (j // 2, t, j % 2))],
        out_specs=pl.BlockSpec((1, D_MODEL, FF_CHUNK), lambda j, t: (j, 0, 0)),
        out_shape=jax.ShapeDtypeStruct((N_CHIPS, D_MODEL, FF_CHUNK), bf16),
        scratch_shapes=[pltpu.VMEM((D_MODEL, FF_CHUNK), f32)],
        compiler_params=_cp("parallel", "arbitrary"),
    )(h2, dgu)


def _ffn_bwd_dw_down(f, dx2):
    s = dx2.shape[0]
    ts = min(2 * ROW_TILE, s)
    n = s // ts

    def body(f_ref, d_ref, o_ref, acc_ref):
        t = pl.program_id(1)

        @pl.when(t == 0)
        def _():
            acc_ref[...] = jnp.zeros_like(acc_ref)

        acc_ref[...] += _dot_tn(f_ref[...], d_ref[...].astype(bf16))

        @pl.when(t == n - 1)
        def _():
            o_ref[...] = acc_ref[...].astype(bf16)

    return pl.pallas_call(
        body, name="ffn_bwd_dw_down", grid=(2, n),
        in_specs=[pl.BlockSpec((ts, FF_CHUNK), lambda c, t: (t, c)),
                  pl.BlockSpec((ts, D_MODEL), lambda c, t: (t, 0))],
        out_specs=pl.BlockSpec((FF_CHUNK, D_MODEL), lambda c, t: (c, 0)),
        out_shape=jax.ShapeDtypeStruct((D_FF, D_MODEL), bf16),
        scratch_shapes=[pltpu.VMEM((FF_CHUNK, D_MODEL), f32)],
        compiler_params=_cp("parallel", "arbitrary"),
    )(f, dx2)


def _mix_bwd_local(dx1, mix, u1, pooled, w_out, ln_g, ln_b, pool_w, pool_scale):
    s = dx1.shape[0]
    tm = min(ROW_TILE, s)
    n = s // tm

    def body(dx1_ref, mix_ref, u1_ref, po_ref, wo_ref, lg_ref, lb_ref, pw_ref, ps_ref,
             du1_ref, dpo_ref, dwo_ref, dpw_ref, small_ref, awo_ref):
        i = pl.program_id(0)

        @pl.when(i == 0)
        def _():
            awo_ref[...] = jnp.zeros_like(awo_ref)
            dpw_ref[...] = jnp.zeros_like(dpw_ref)
            small_ref[...] = jnp.zeros_like(small_ref)

        dyb = dx1_ref[...].astype(bf16)
        dmix = _dot_nt(dyb, wo_ref[...])
        awo_ref[...] += _dot_tn(mix_ref[...], dyb)
        u1 = u1_ref[...]
        mu = jnp.mean(u1, axis=-1, keepdims=True)
        uc = u1 - mu
        rstd = lax.rsqrt(jnp.mean(uc * uc, axis=-1, keepdims=True) + EPS)
        uh = uc * rstd
        u2 = uh * lg_ref[...] + lb_ref[...]
        du2 = dmix[:, :C_CONV] * _silu_grad(u2, _sigmoid(u2))
        duh = du2 * lg_ref[...]
        du1 = rstd * (duh - jnp.mean(duh, axis=-1, keepdims=True) - uh * jnp.mean(duh * uh, axis=-1, keepdims=True))
        du1_ref[...] = du1
        small_ref[0:1, :] += jnp.sum(du2 * uh, axis=0, keepdims=True)
        small_ref[1:2, :] += jnp.sum(du2, axis=0, keepdims=True)
        small_ref[2:3, :] += jnp.sum(du1, axis=0, keepdims=True)
        for g in range(len(POOL_WINDOWS)):
            cols = slice(g * POOL_GROUP, (g + 1) * POOL_GROUP)
            dq = dmix[:, C_CONV + g * POOL_GROUP:C_CONV + (g + 1) * POOL_GROUP]
            pwb = pw_ref[g].astype(bf16)
            pg = po_ref[:, cols]
            mixed = _dot(pg, pwb)
            small_ref[3:4, cols] += jnp.sum(dq * mixed, axis=0, keepdims=True)
            dmixed = (dq * ps_ref[:, cols]).astype(bf16)
            dpw_ref[g] += _dot_tn(pg, dmixed)
            dpo_ref[:, cols] = _dot_nt(dmixed, pwb)

        @pl.when(i == n - 1)
        def _():
            dwo_ref[...] = awo_ref[...].astype(bf16)

    return pl.pallas_call(
        body, name="mix_bwd_local", grid=(n,),
        in_specs=[_row(0, (tm, D_MODEL)), _row(0, (tm, D_MODEL)), _row(0, (tm, C_CONV)), _row(0, (tm, C_POOL)),
                  _full((D_MODEL, D_MODEL)), _full((1, C_CONV)), _full((1, C_CONV)),
                  _full((4, POOL_GROUP, POOL_GROUP)), _full((1, C_POOL))],
        out_specs=[_row(0, (tm, C_CONV)), _row(0, (tm, C_POOL)), _full((D_MODEL, D_MODEL)),
                   _full((4, POOL_GROUP, POOL_GROUP)), _full((8, C_CONV))],
        out_shape=[jax.ShapeDtypeStruct((s, C_CONV), f32), jax.ShapeDtypeStruct((s, C_POOL), f32),
                   jax.ShapeDtypeStruct((D_MODEL, D_MODEL), bf16),
                   jax.ShapeDtypeStruct((4, POOL_GROUP, POOL_GROUP), f32), jax.ShapeDtypeStruct((8, C_CONV), f32)],
        scratch_shapes=[pltpu.VMEM((D_MODEL, D_MODEL), f32)],
        compiler_params=_cp("arbitrary"),
    )(dx1, mix, u1, pooled, w_out, ln_g, ln_b, pool_w, pool_scale)


def _in_bwd(du1, dpo, z, x, dx1, conv_w, g_mix, w_in_g, parts, small_parts):
    s = x.shape[0]
    tm = min(ROW_TILE, s)
    n = s // tm
    hb = tm // HALO
    last = s // HALO - 1
    nv = tm // 8
    assert nv >= SHIFT_PAD and nv % SHIFT_GROUPS == 0
    nc, ns = len(parts), len(small_parts)
    cs = _crossed(parts)

    def body(du_ref, dun_ref, dp_ref, dpn_ref, z_ref, zp_ref, x_ref, dx1_ref, cw_ref, g_ref, w_ref, *refs):
        outs = refs[nc + ns:]
        gx_ref, dw_ref, dcw_ref, small_ref = outs[:4]
        eu, ed, ep, ss, u0_ref, dz_ref, acc_ref, dcw_acc = outs[4 + nc + ns:12 + nc + ns]
        sems = outs[12 + nc + ns:]
        cross = _Cross(refs[:nc], outs[4:4 + nc], sems[0], sems[1])
        gather = _SmallGather(refs[nc:nc + ns], outs[4 + nc:4 + nc + ns], sems[2], sems[3])
        i = pl.program_id(0)
        pl.when(i == 0)(cross.start)
        pl.when(i == 0)(gather.start)

        @pl.when(i == 0)
        def _():
            acc_ref[...] = jnp.zeros_like(acc_ref)
            dcw_acc[...] = jnp.zeros_like(dcw_acc)
            small_ref[...] = jnp.zeros_like(small_ref)

        keep_prev = (i > 0).astype(f32)
        keep_next = (i < n - 1).astype(f32)
        zp = zp_ref[...] * keep_prev
        u0_prev = zp[:, :C_CONV] * _sigmoid(zp[:, C_CONV:2 * C_CONV])
        u0_ref[...] = z_ref[:, :C_CONV] * _sigmoid(z_ref[:, C_CONV:2 * C_CONV])
        du_next = dun_ref[...] * keep_next
        for c, w_pool in enumerate(POOL_WINDOWS):
            lanes = slice(c * 128, (c + 1) * 128)
            eu[c, pl.ds(0, SHIFT_PAD, stride=8), :] = u0_prev[:, lanes]
            ed[c, pl.ds(nv * 8 + 7, SHIFT_PAD, stride=8), :] = du_next[:, lanes]
            ep[c, pl.ds(nv * 8 + 7, SHIFT_PAD, stride=8), :] = (
                dpn_ref[:, lanes] * keep_next / _pool_counts(HALO, w_pool, (i + 1) * tm))
            for j in range(8):
                rows = slice(j * nv, (j + 1) * nv)
                eu[c, pl.ds(SHIFT_PAD * 8 + j, nv, stride=8), :] = u0_ref[rows, lanes]
                ed[c, pl.ds(j, nv, stride=8), :] = du_ref[rows, lanes]
                ep[c, pl.ds(j, nv, stride=8), :] = dp_ref[rows, lanes] / _pool_counts(nv, w_pool, i * tm + j * nv)
                if j >= 1:
                    eu[c, pl.ds(j, SHIFT_PAD, stride=8), :] = u0_ref[j * nv - SHIFT_PAD:j * nv, lanes]
                if j <= 6:
                    edge = slice((j + 1) * nv, (j + 1) * nv + SHIFT_PAD)
                    ed[c, pl.ds(nv * 8 + j, SHIFT_PAD, stride=8), :] = du_ref[edge, lanes]
                    ep[c, pl.ds(nv * 8 + j, SHIFT_PAD, stride=8), :] = (
                        dp_ref[edge, lanes] / _pool_counts(SHIFT_PAD, w_pool, i * tm + (j + 1) * nv))
        for c, w_pool in enumerate(POOL_WINDOWS):
            lanes = slice(c * 128, (c + 1) * 128)
            b_lanes = slice(C_CONV + c * 128, C_CONV + (c + 1) * 128)
            v_lanes = slice(2 * C_CONV + c * 128, 2 * C_CONV + (c + 1) * 128)
            for v0 in range(0, nv, SHIFT_GROUPS):
                span = SHIFT_GROUPS * 8
                acc = jnp.zeros((span, 128), f32)
                for k in range(CONV_K):
                    acc = acc + cw_ref[pl.ds(k, 1), lanes] * ed[c, pl.ds((v0 + CONV_K - 1 - k) * 8, span), :]
                ss[0, v0 * 8:v0 * 8 + span, :] = acc
                acc = ep[c, pl.ds(v0 * 8, span), :]
                for d in range(1, w_pool):
                    acc = acc + ep[c, pl.ds((v0 + d) * 8, span), :]
                ss[1, v0 * 8:v0 * 8 + span, :] = acc
                d1 = ed[c, pl.ds(v0 * 8, span), :]
                for k in range(CONV_K):
                    prod = d1 * eu[c, pl.ds((SHIFT_PAD - (CONV_K - 1) + v0 + k) * 8, span), :]
                    fold = prod[0:8]
                    for r in range(8, span, 8):
                        fold = fold + prod[r:r + 8]
                    dcw_acc[k, :, lanes] += fold
            for j in range(8):
                rows = slice(j * nv, (j + 1) * nv)
                du0 = ss[0, pl.ds(j, nv, stride=8), :]
                av, sv = z_ref[rows, lanes], _sigmoid(z_ref[rows, b_lanes])
                dz_ref[rows, lanes] = (du0 * sv).astype(bf16)
                dz_ref[rows, b_lanes] = (du0 * av * sv * (1.0 - sv)).astype(bf16)
                dz_ref[rows, v_lanes] = (ss[1, pl.ds(j, nv, stride=8), :] - dp_ref[rows, lanes]).astype(bf16)
        h, xh, r = _rms_fwd(x_ref[...], g_ref[...])
        dz = dz_ref[...]
        acc_ref[...] += _dot_tn(h.astype(bf16), dz)
        dh = _dot_nt(dz[:, 0:W_IN_CHUNK], w_ref[0])
        for j in range(1, N_CHIPS):
            dh = dh + _dot_nt(dz[:, j * W_IN_CHUNK:(j + 1) * W_IN_CHUNK], w_ref[j])
        dxb, dg = _rms_bwd(dh, xh, r, g_ref[...])
        gx_ref[...] = dx1_ref[...] + dxb
        small_ref[0:1, :] += dg

        @pl.when(i == n - 1)
        def _():
            for j in range(N_CHIPS):
                dw_ref[j] = acc_ref[:, j * W_IN_CHUNK:(j + 1) * W_IN_CHUNK].astype(bf16)
            dcw_ref[...] = jnp.sum(dcw_acc[...], axis=1)

        pl.when(i == n - 1)(gather.finish)
        pl.when(i == n - 1)(cross.finish)

    nxt = lambda i: (jnp.minimum((i + 1) * hb, last), 0)
    res = pl.pallas_call(
        body, name="in_bwd", grid=(n,),
        in_specs=[_row(0, (tm, C_CONV)), pl.BlockSpec((HALO, C_CONV), nxt),
                  _row(0, (tm, C_POOL)), pl.BlockSpec((HALO, C_POOL), nxt),
                  _row(0, (tm, W_IN_COLS)),
                  pl.BlockSpec((HALO, W_IN_COLS), lambda i: (jnp.maximum(i * hb - 1, 0), 0)),
                  _row(0, (tm, D_MODEL)), _row(0, (tm, D_MODEL)), _full((HALO, C_CONV)), _full((1, D_MODEL)),
                  _full((N_CHIPS, D_MODEL, W_IN_CHUNK))] + cs["in_specs"] + [ANY] * ns,
        out_specs=[_row(0, (tm, D_MODEL)), _full((N_CHIPS, D_MODEL, W_IN_CHUNK)), _full((HALO, C_CONV)),
                   _full((8, D_MODEL))] + cs["out_specs"] + [ANY] * ns,
        out_shape=[jax.ShapeDtypeStruct((s, D_MODEL), f32), jax.ShapeDtypeStruct((N_CHIPS, D_MODEL, W_IN_CHUNK), bf16),
                   jax.ShapeDtypeStruct((HALO, C_CONV), f32), jax.ShapeDtypeStruct((8, D_MODEL), f32)] + cs["out_shape"]
        + _gathered_shapes(small_parts),
        scratch_shapes=[pltpu.VMEM((4, (SHIFT_PAD + nv) * 8, 128), f32), pltpu.VMEM((4, (nv + SHIFT_PAD) * 8, 128), f32),
                        pltpu.VMEM((4, (nv + SHIFT_PAD) * 8, 128), f32), pltpu.VMEM((2, tm, 128), f32),
                        pltpu.VMEM((tm, C_CONV), f32), pltpu.VMEM((tm, W_IN_COLS), bf16),
                        pltpu.VMEM((D_MODEL, W_IN_COLS), f32), pltpu.VMEM((HALO, 8, C_CONV), f32)] + cs["scratch"]
        + _SmallGather.scratch(ns),
        compiler_params=_cp("arbitrary"),
    )(du1, du1, dpo, dpo, z, z, x, dx1, conv_w, g_mix, w_in_g, *parts, *small_parts)
    return res[:4], res[4:4 + nc], res[4 + nc:]


def _pair_reduce(name, grads):
    nk = len(grads)
    halves = [g.shape[1] // 2 for g in grads]

    def body(*refs):
        ins, outs, got = refs[:nk], refs[nk:2 * nk], refs[2 * nk:3 * nk]
        send_sems, recv_sems = refs[3 * nk:]
        x, y, c, _, _ = _place()

        def half(k, core):
            return pl.ds(pl.multiple_of(core * halves[k], 16), halves[k])

        cps = [_remote(ins[k].at[:, half(k, 1 - c), :], got[k], send_sems.at[k], recv_sems.at[k], (x, y, 1 - c))
               for k in range(nk)]
        for cp in cps:
            cp.start()
        for k, cp in enumerate(cps):
            cp.wait_recv()
            outs[k][...] = (ins[k][:, half(k, c), :].astype(f32) + got[k][...].astype(f32)).astype(bf16)
        for cp in cps:
            cp.wait_send()

    shapes = [(N_CHIPS, h, g.shape[2]) for g, h in zip(grads, halves)]
    return pl.pallas_call(
        body, name=f"pair_reduce_{name}", in_specs=[VMEM] * nk, out_specs=[VMEM] * nk,
        out_shape=[jax.ShapeDtypeStruct(s, bf16) for s in shapes],
        scratch_shapes=[pltpu.VMEM(s, bf16) for s in shapes]
        + [pltpu.SemaphoreType.DMA((nk,)), pltpu.SemaphoreType.DMA((nk,))],
        compiler_params=pltpu.CompilerParams(vmem_limit_bytes=VMEM_LIMIT),
    )(*grads)


class _Cross:
    def __init__(self, parts, landed, send_sems, recv_sems):
        self.parts, self.landed, self.send_sems, self.recv_sems = parts, landed, send_sems, recv_sems
        _, _, self.c, self.me, self.others = _place()

    def _copy(self, k, j, src_chunk, dst_slot):
        ox, oy = self.others[j]
        return _remote(self.parts[k].at[src_chunk], self.landed[k].at[dst_slot], self.send_sems.at[3 * k + j],
                       self.recv_sems.at[3 * k + j], (ox, oy, self.c))

    def _each(self):
        return [(k, j, 2 * self.others[j][0] + self.others[j][1]) for k in range(len(self.parts)) for j in range(3)]

    def start(self):
        for k, j, chip in self._each():
            self._copy(k, j, chip, self.me).start()

    def finish(self):
        for k, j, chip in self._each():
            self._copy(k, j, chip, chip).wait_recv()
        for k, j, chip in self._each():
            self._copy(k, j, chip, self.me).wait_send()

    @staticmethod
    def scratch(n):
        return [pltpu.SemaphoreType.DMA((3 * n,)), pltpu.SemaphoreType.DMA((3 * n,))]


class _CrossAll:
    def __init__(self, grads, landed, send_sems, recv_sems):
        self.grads, self.landed, self.send_sems, self.recv_sems = grads, landed, send_sems, recv_sems
        self.x, self.y, self.c, self.me, self.others = _place()
        self.dev = 2 * self.me + self.c

    def _piece(self, k, chip, half):
        rows = self.grads[k].shape[1] // 2
        return self.grads[k].at[chip, pl.ds(half * rows, rows), :]

    def _to_sibling(self, k):
        return _remote(self._piece(k, self.me, 1 - self.c), self.landed[k].at[self.dev], self.send_sems.at[7 * k],
                       self.recv_sems.at[7 * k], (self.x, self.y, 1 - self.c))

    def _to_chip(self, k, j, half):
        ox, oy = self.others[j]
        return _remote(self._piece(k, 2 * ox + oy, half), self.landed[k].at[self.dev],
                       self.send_sems.at[7 * k + 1 + 2 * j + half], self.recv_sems.at[7 * k + 1 + 2 * j + self.c],
                       (ox, oy, half))

    def _from(self, k, sem, slot):
        return _remote(self.landed[k].at[slot], self.landed[k].at[slot], self.send_sems.at[7 * k + sem],
                       self.recv_sems.at[7 * k + sem], (self.x, self.y, 1 - self.c))

    def start(self):
        for k in range(len(self.grads)):
            self._to_sibling(k).start()
            for j in range(3):
                for half in range(2):
                    self._to_chip(k, j, half).start()

    def finish(self):
        for k in range(len(self.grads)):
            self._from(k, 0, 2 * self.me + 1 - self.c).wait_recv()
            for j, (ox, oy) in enumerate(self.others):
                for core in range(2):
                    self._from(k, 1 + 2 * j + core, 4 * ox + 2 * oy + core).wait_recv()
        for k in range(len(self.grads)):
            self._to_sibling(k).wait_send()
            for j in range(3):
                for half in range(2):
                    self._to_chip(k, j, half).wait_send()

    @staticmethod
    def scratch(n):
        return [pltpu.SemaphoreType.DMA((7 * n,)), pltpu.SemaphoreType.DMA((7 * n,))]


def _rs_sum_chips(name, place, landed, part):
    _, half, cols = landed.shape

    def body(place_ref, l_ref, p_ref, o_ref):
        me = place_ref[0]
        own = p_ref[0].astype(f32)
        acc = jnp.where(me == 0, own, l_ref[0].astype(f32))
        for j in range(1, N_CHIPS):
            acc = acc + jnp.where(me == j, own, l_ref[j].astype(f32))
        o_ref[...] = acc

    return pl.pallas_call(
        body, name=f"rs_sum_chips_{name}",
        grid_spec=pltpu.PrefetchScalarGridSpec(
            num_scalar_prefetch=1, grid=(1,),
            in_specs=[pl.BlockSpec((N_CHIPS, half, cols), lambda t, place_ref: (0, 0, 0)),
                      pl.BlockSpec((1, half, cols), lambda t, place_ref: (place_ref[0], 0, 0))],
            out_specs=pl.BlockSpec((half, cols), lambda t, place_ref: (place_ref[1], 0))),
        out_shape=jax.ShapeDtypeStruct((2 * half, cols), f32),
        compiler_params=_cp("arbitrary"),
    )(place, landed, part)


def _grad_pair(shards):
    nk = len(shards)

    def body(*refs):
        outs = refs[nk:2 * nk]
        send_sems, recv_sems = refs[2 * nk:]
        x, y, c, _, _ = _place()

        def half(k, core):
            h = outs[k].shape[0] // 2
            return outs[k].at[pl.ds(core * h, h), :]

        cps = [_remote(half(k, c), half(k, c), send_sems.at[k], recv_sems.at[k], (x, y, 1 - c)) for k in range(nk)]
        for cp in cps:
            cp.start()
        for k in range(nk):
            _remote(half(k, 1 - c), half(k, 1 - c), send_sems.at[k], recv_sems.at[k], (x, y, 1 - c)).wait_recv()
        for cp in cps:
            cp.wait_send()

    return pl.pallas_call(
        body, name="grad_pair", in_specs=[ANY] * nk, out_specs=[ANY] * nk,
        out_shape=[jax.ShapeDtypeStruct(a.shape, f32) for a in shards],
        input_output_aliases={k: k for k in range(nk)},
        scratch_shapes=[pltpu.SemaphoreType.DMA((nk,)), pltpu.SemaphoreType.DMA((nk,))],
    )(*shards)


def _adam_math(w, g, m, v):
    m = ADAM_B1 * m + (1.0 - ADAM_B1) * g
    v = ADAM_B2 * v + (1.0 - ADAM_B2) * (g * g)
    m_hat = m / (1.0 - ADAM_B1 ** ADAM_STEP)
    v_hat = v / (1.0 - ADAM_B2 ** ADAM_STEP)
    delta = -ADAM_LR * (m_hat / (jnp.sqrt(v_hat) + ADAM_EPS) + ADAM_WD * w)
    return delta, m, v


def _adam(ws, gs, ms, vs):
    n = len(ws)
    steps = 2 * SHARD_STEPS

    def body(*refs):
        for k in range(n):
            w_ref, g_ref, m_ref, v_ref = (refs[j * n + k] for j in range(4))
            go_ref, d_ref, nm_ref, nv_ref = (refs[(4 + j) * n + k] for j in range(4))
            g = g_ref[...]
            go_ref[...] = g
            d_ref[...], nm_ref[...], nv_ref[...] = _adam_math(w_ref[...], g, m_ref[...], v_ref[...])

    specs = [_row(0, (w.shape[0] // steps, w.shape[1])) for w in ws]
    res = pl.pallas_call(
        body, name="adam_shards", grid=(steps,), in_specs=specs * 4, out_specs=specs * 4,
        out_shape=[jax.ShapeDtypeStruct(w.shape, f32) for w in ws] * 4,
        compiler_params=_cp("parallel"),
    )(*ws, *gs, *ms, *vs)
    return [tuple(res[j * n + k] for j in range(4)) for k in range(n)]


class _SmallGather:
    def __init__(self, parts, bufs, send_sems, recv_sems):
        self.parts, self.bufs, self.send_sems, self.recv_sems = parts, bufs, send_sems, recv_sems
        self.x, self.y, self.c, _, self.others = _place()
        self.sibling = (self.x, self.y, 1 - self.c)

    def _copy(self, a, k, block, to, src=None):
        slot = self.bufs[a].at[4 * block[0] + 2 * block[1] + block[2]]
        return _remote(slot if src is None else src, slot, self.send_sems.at[7 * a + k], self.recv_sems.at[7 * a + k],
                       to)

    def _first(self, a):
        me = (self.x, self.y, self.c)
        return [self._copy(a, 0, me, self.sibling, src=self.parts[a])] + [
            self._copy(a, 1 + j, me, (*chip, self.c), src=self.parts[a]) for j, chip in enumerate(self.others)]

    def _passed(self, a):
        return [self._copy(a, 4 + j, (*chip, self.c), self.sibling) for j, chip in enumerate(self.others)]

    @staticmethod
    def scratch(n):
        return [pltpu.SemaphoreType.DMA((7 * n,)), pltpu.SemaphoreType.DMA((7 * n,))]

    def start(self):
        for a in range(len(self.parts)):
            for cp in self._first(a):
                cp.start()

    def finish(self):
        sent = []
        for a in range(len(self.parts)):
            passed = self._passed(a)
            for j, chip in enumerate(self.others):
                self._copy(a, 1 + j, (*chip, self.c), self.sibling).wait_recv()
                passed[j].start()
            sent += self._first(a) + passed
        for a in range(len(self.parts)):
            self._copy(a, 0, self.sibling, self.sibling).wait_recv()
            for j, chip in enumerate(self.others):
                self._copy(a, 4 + j, (*chip, 1 - self.c), self.sibling).wait_recv()
        for cp in sent:
            cp.wait_send()


def _gathered_shapes(parts):
    return [jax.ShapeDtypeStruct((N_DEV,) + p.shape, p.dtype) for p in parts]


def _rs_sum_group(name, place, landed, parts, landed_all, grads_all, cross_parts, small_parts):
    nk, na, nx, ns = len(landed), len(landed_all), len(cross_parts), len(small_parts)
    dims = [a.shape[1:] for a in landed]
    dims_all = [a.shape[1:] for a in landed_all]

    def body(place_ref, *refs):
        take = iter(refs)
        l_refs, p_refs, la_refs, ga_refs, x_refs, sp_refs, o_refs, oa_refs, xl_refs, sbufs = (
            [next(take) for _ in range(cnt)] for cnt in (nk, nk, na, na, nx, ns, nk, na, nx, ns))
        sems = list(take)
        cross = _Cross(x_refs, xl_refs, sems[0], sems[1])
        small = _SmallGather(sp_refs, sbufs, sems[2], sems[3])
        t = pl.program_id(0)

        @pl.when(t == 0)
        def _():
            cross.start()
            small.start()

        me = place_ref[0]
        for l_ref, p_ref, o_ref in zip(l_refs, p_refs, o_refs):
            own = p_ref[0].astype(f32)
            acc = jnp.where(me == 0, own, l_ref[0].astype(f32))
            for j in range(1, N_CHIPS):
                acc = acc + jnp.where(me == j, own, l_ref[j].astype(f32))
            o_ref[...] = acc
        dev = 2 * me + place_ref[1]
        for l_ref, g_ref, o_ref in zip(la_refs, ga_refs, oa_refs):
            own = g_ref[0].astype(f32)
            acc = jnp.where(dev == 0, own, l_ref[0].astype(f32))
            for d in range(1, N_DEV):
                acc = acc + jnp.where(dev == d, own, l_ref[d].astype(f32))
            o_ref[...] = acc

        @pl.when(t == 1)
        def _():
            small.finish()
            cross.finish()

    def halves(h, c, lead, index):
        return pl.BlockSpec((lead, h // 2, c) if lead else (h // 2, c), index)

    in_specs = [halves(h, c, N_CHIPS, lambda t, pr: (0, t, 0)) for h, c in dims]
    in_specs += [halves(h, c, 1, lambda t, pr: (pr[0], t, 0)) for h, c in dims]
    in_specs += [halves(h, c, N_DEV, lambda t, pr: (0, t, 0)) for h, c in dims_all]
    in_specs += [halves(h, c, 1, lambda t, pr: (pr[0], 2 * pr[1] + t, 0)) for h, c in dims_all]
    in_specs += [ANY] * (nx + ns)
    out_specs = [halves(h, c, 0, lambda t, pr: (2 * pr[1] + t, 0)) for h, c in dims + dims_all] + [ANY] * (nx + ns)
    out_shape = [jax.ShapeDtypeStruct((2 * h, c), f32) for h, c in dims + dims_all]
    out_shape += [jax.ShapeDtypeStruct(a.shape, a.dtype) for a in cross_parts] + _gathered_shapes(small_parts)
    res = pl.pallas_call(
        body, name=f"rs_sum_{name}",
        grid_spec=pltpu.PrefetchScalarGridSpec(
            num_scalar_prefetch=1, grid=(2,), in_specs=in_specs, out_specs=out_specs,
            scratch_shapes=_Cross.scratch(nx) + _SmallGather.scratch(ns)),
        out_shape=out_shape, compiler_params=_cp("arbitrary"),
    )(place, *landed, *parts, *landed_all, *grads_all, *cross_parts, *small_parts)
    return res[:nk], res[nk:nk + na], res[nk + na:nk + na + nx], res[nk + na + nx:]


SMALL_PARAMS = ("g_ple_gate", "g_ple_post", "g_final", "g_ffn", "ln_g", "ln_b", "conv_b", "pool_scale", "pool_w",
                "conv_w", "g_mix")
SMALL_ROWS = {"g_ple_gate": (0, 0), "g_ple_post": (0, 1), "g_final": (0, 2), "g_ffn": (1, 0), "ln_g": (2, 0),
              "ln_b": (2, 1), "conv_b": (2, 2), "pool_scale": (2, 3), "g_mix": (5, 0)}
LOSS_ROW = (0, 3)


def _small_adam(place, gathered, parts, params):
    nb, names = len(parts), SMALL_PARAMS
    flat = [a for nm in names for a in params[nm]]

    def body(place_ref, *refs):
        b_refs, p_refs = refs[:nb], refs[nb:2 * nb]
        w_refs = refs[2 * nb:2 * nb + 3 * len(names)]
        outs = refs[2 * nb + 3 * len(names):]
        loss_ref, o_refs, cw_sum = outs[0], outs[1:1 + 4 * len(names)], outs[1 + 4 * len(names)]
        chip = place_ref[0]
        me = 2 * chip + place_ref[1]

        def total(blk, idx):
            own = p_refs[blk][idx]
            g = jnp.where(me == 0, own, b_refs[blk][(0,) + idx])
            for d in range(1, N_DEV):
                g = g + jnp.where(me == d, own, b_refs[blk][(d,) + idx])
            return g

        everything = (slice(None), slice(None))
        loss_ref[...] = total(LOSS_ROW[0], (pl.ds(LOSS_ROW[1], 1), pl.ds(0, 128)))
        d_cw = total(4, everything)
        mine = jnp.where(chip == 0, d_cw[:, 0:128], 0.0)
        for j in range(1, N_CHIPS):
            mine = mine + jnp.where(chip == j, d_cw[:, j * 128:(j + 1) * 128], 0.0)
        cw_sum[...] = mine
        for k, nm in enumerate(names):
            w_ref, m_ref, v_ref = w_refs[3 * k:3 * k + 3]
            g_ref, d_ref, nm_ref, nv_ref = o_refs[4 * k:4 * k + 4]
            if nm == "pool_w":
                g = total(3, everything + (slice(None),))
            elif nm == "conv_w":
                g = cw_sum[pl.ds(0, CONV_K), :]
            else:
                blk, row = SMALL_ROWS[nm]
                g = total(blk, (pl.ds(row, 1), slice(None)))
            g_ref[...] = g
            d_ref[...], nm_ref[...], nv_ref[...] = _adam_math(w_ref[...], g, m_ref[...], v_ref[...])

    whole = lambda a: pl.BlockSpec(a.shape, lambda t, pr: (0,) * a.ndim)
    out_shape = [jax.ShapeDtypeStruct((1, 128), f32)]
    out_shape += [jax.ShapeDtypeStruct(params[nm][0].shape, f32) for nm in names for _ in range(4)]
    res = pl.pallas_call(
        body, name="small_adam",
        grid_spec=pltpu.PrefetchScalarGridSpec(
            num_scalar_prefetch=1, grid=(1,),
            in_specs=[whole(a) for a in list(gathered) + list(parts) + flat],
            out_specs=[whole(s) for s in out_shape], scratch_shapes=[pltpu.VMEM((HALO, 128), f32)]),
        out_shape=out_shape, compiler_params=_cp("arbitrary"),
    )(place, *gathered, *parts, *flat)
    return res[0], {nm: res[1 + 4 * k:5 + 4 * k] for k, nm in enumerate(names)}


def _pad_rows(a, rows):
    return jnp.concatenate([a, jnp.zeros((rows - a.shape[0],) + a.shape[1:], a.dtype)], axis=0)


def kernel(x, p, g_mix, w_in, conv_w, conv_b, ln_g, ln_b, pool_w, pool_scale, w_out, g_ffn, w_gate_up, w_down, g_ple_gate, w_ple_gate, w_ple_up, g_ple_post, g_final, loss_target, m_g_mix, m_w_in, m_conv_w, m_conv_b, m_ln_g, m_ln_b, m_pool_w, m_pool_scale, m_w_out, m_g_ffn, m_w_gate_up, m_w_down, m_g_ple_gate, m_w_ple_gate, m_w_ple_up, m_g_ple_post, m_g_final, v_g_mix, v_w_in, v_conv_w, v_conv_b, v_ln_g, v_ln_b, v_pool_w, v_pool_scale, v_w_out, v_g_ffn, v_w_gate_up, v_w_down, v_g_ple_gate, v_w_ple_gate, v_w_ple_up, v_g_ple_post, v_g_final):
    seq = x.shape[1]
    me = 2 * lax.axis_index("x") + lax.axis_index("y")
    chip = me.astype(jnp.int32).reshape(1)
    core = lax.axis_index("c").astype(jnp.int32).reshape(1)
    place = jnp.concatenate([chip, core])
    xs, ps, ts = x.reshape(seq, D_MODEL), p.reshape(seq, D_PLE), loss_target.reshape(seq, D_MODEL)

    big = [w_in[0], w_gate_up[0], w_out[0], w_down[0], w_ple_gate[0], w_ple_up[0]]
    big_m = [m_w_in[0], m_w_gate_up[0], m_w_out[0], m_w_down[0], m_w_ple_gate[0], m_w_ple_up[0]]
    big_v = [v_w_in[0], v_w_gate_up[0], v_w_out[0], v_w_down[0], v_w_ple_gate[0], v_w_ple_up[0]]
    b_in, b_gu, b_out, b_down, b_pg, b_pu, b_cw = _cast_into_slots(
        chip, big + [_pad_rows(conv_w[0], HALO)], [bf16] * len(big) + [f32])
    xi, yi = lax.axis_index("x"), lax.axis_index("y")
    order = jnp.stack([me, 2 * (1 - xi) + yi, 2 * xi + 1 - yi, 2 * (1 - xi) + 1 - yi]).astype(jnp.int32)

    z, (w_in_g, cw_g, w_out_g) = _mix_in(xs, g_mix, order, [b_in, b_cw, b_out])
    conv_w_f = cw_g.transpose(1, 0, 2).reshape(HALO, C_CONV)
    w_out_f = w_out_g.reshape(D_MODEL, D_MODEL)
    (x1, mix, u1, pooled, h2), (w_gu_g,) = _conv_pool_out(z, xs, conv_w_f, conv_b, ln_g, ln_b, pool_w[0], pool_scale,
                                                          w_out_f, g_ffn, [b_gu])
    (gu, ffn_f), (w_down_g, w_pg_g, w_pu_g) = _ffn_up(h2, w_gu_g, [b_down, b_pg, b_pu])
    w_down_f = w_down_g.reshape(D_FF, D_MODEL)
    x2 = _ffn_down(x1, ffn_f, w_down_f)
    w_pg_f = w_pg_g.reshape(D_MODEL, D_MODEL)
    dx2, d_w_pg, d_w_pu, small_ple = _ple_loss(x2, ps, ts, g_ple_gate, g_ple_post, g_final.reshape(1, D_MODEL),
                                               w_pg_f, w_pu_g)
    d_w_down = _ffn_bwd_dw_down(ffn_f, dx2)
    grads_a = [d_w_down.reshape(N_CHIPS, -1, D_MODEL), d_w_pg.reshape(N_CHIPS, -1, D_MODEL), d_w_pu]
    (dx1, dgu, small_ffn), landed_a = _ffn_bwd_dx(dx2, x1, gu, g_ffn, w_gu_g, w_down_f, grads_a)
    d_w_gu = _ffn_bwd_dw_gu(h2, dgu)
    du1, dpo, d_w_out, d_pool_w, small_mix = _mix_bwd_local(dx1, mix, u1, pooled, w_out_f, ln_g, ln_b, pool_w[0],
                                                             pool_scale)
    parts_b = _pair_reduce("b", [d_w_gu, d_w_out.reshape(N_CHIPS, -1, D_MODEL)])
    small_0 = [small_ple, small_ffn, small_mix, d_pool_w]
    (grad_x, d_w_in, d_conv_w, small_in), landed_b, small_all_0 = _in_bwd(du1, dpo, z, xs, dx1, conv_w_f, g_mix,
                                                                            w_in_g, parts_b, small_0)
    parts_c = _pair_reduce("c", [d_w_in])
    small_1 = [d_conv_w, small_in]
    (h_gu, h_out), (h_down, h_pg, h_pu), landed_c, small_all_1 = _rs_sum_group(
        "ab", place, landed_b, parts_b, landed_a, grads_a, parts_c, small_1)
    h_in = _rs_sum_chips("w_in", place, landed_c[0], parts_c[0])
    big_g = _grad_pair([h_in, h_gu, h_out, h_down, h_pg, h_pu])
    big_upd = _adam(big, big_g, big_m, big_v)

    p3 = lambda w, m, v: (w, m, v)
    row = lambda a: a.reshape(1, D_MODEL)
    params = dict(
        g_ple_gate=p3(g_ple_gate, m_g_ple_gate, v_g_ple_gate), g_ple_post=p3(g_ple_post, m_g_ple_post, v_g_ple_post),
        g_final=p3(row(g_final), row(m_g_final), row(v_g_final)), g_ffn=p3(g_ffn, m_g_ffn, v_g_ffn),
        ln_g=p3(ln_g, m_ln_g, v_ln_g), ln_b=p3(ln_b, m_ln_b, v_ln_b), conv_b=p3(conv_b, m_conv_b, v_conv_b),
        pool_scale=p3(pool_scale, m_pool_scale, v_pool_scale), pool_w=p3(pool_w[0], m_pool_w[0], v_pool_w[0]),
        conv_w=p3(conv_w[0], m_conv_w[0], v_conv_w[0]), g_mix=p3(g_mix, m_g_mix, v_g_mix))
    loss, small = _small_adam(place, list(small_all_0) + list(small_all_1), small_0 + small_1, params)
    back = dict(g_final=lambda a: a.reshape(D_MODEL), pool_w=lambda a: a[None], conv_w=lambda a: a[None])

    names = ["g_mix", "w_in", "conv_w", "conv_b", "ln_g", "ln_b", "pool_w", "pool_scale", "w_out", "g_ffn",
             "w_gate_up", "w_down", "g_ple_gate", "w_ple_gate", "w_ple_up", "g_ple_post", "g_final"]
    big_at = {"w_in": 0, "w_gate_up": 1, "w_out": 2, "w_down": 3, "w_ple_gate": 4, "w_ple_up": 5}
    out = [loss[0, 0], grad_x.reshape(1, seq, D_MODEL)]
    for kind in range(4):
        for nm in names:
            if nm in big_at:
                out.append(big_upd[big_at[nm]][kind][None])
            else:
                out.append(back.get(nm, lambda a: a)(small[nm][kind]))
    return tuple(out)
```

```python
import functools

import jax
import jax.numpy as jnp
from jax import lax
from jax.experimental import pallas as pl
from jax.experimental.pallas import tpu as pltpu

f32, bf16 = jnp.float32, jnp.bfloat16

EPS = 1e-6
D_MODEL = 1024
C_CONV = 512
C_POOL = 512
POOL_WINDOWS = (2, 4, 8, 16)
POOL_GROUP = 128
CONV_K = 31
D_FF = 2816
D_PLE = 256
N_CHIPS = 4
N_DEV = 8
W_IN_COLS = 2 * C_CONV + C_POOL
W_IN_CHUNK = W_IN_COLS // N_CHIPS
FF_CHUNK = 2 * D_FF // N_CHIPS
PLE_CHUNK = D_MODEL // N_CHIPS
HALO = 32
ROW_TILE = 512
SHIFT_PAD = 32
SHIFT_GROUPS = 16
FF_SUB = (0, 512, 1024, FF_CHUNK)
VMEM_LIMIT = 56 * 1024 * 1024

ADAM_LR = 0.001
ADAM_B1 = 0.9
ADAM_B2 = 0.999
ADAM_EPS = 1e-08
ADAM_WD = 0.01
ADAM_STEP = 10

MESH = pl.DeviceIdType.MESH
ANY = pl.BlockSpec(memory_space=pl.ANY)
VMEM = pl.BlockSpec(memory_space=pltpu.VMEM)


def _cp(*sem):
    return pltpu.CompilerParams(dimension_semantics=sem, vmem_limit_bytes=VMEM_LIMIT)


def _dot(a, b):
    return jnp.dot(a, b, preferred_element_type=f32)


def _dot_nt(a, b):
    return lax.dot_general(a, b, (((1,), (1,)), ((), ())), preferred_element_type=f32)


def _dot_tn(a, b):
    return lax.dot_general(a, b, (((0,), (0,)), ((), ())), preferred_element_type=f32)


def _sigmoid(v):
    return jax.nn.sigmoid(v)


def _rms_fwd(v, g):
    r = lax.rsqrt(jnp.mean(v * v, axis=-1, keepdims=True) + EPS)
    vh = v * r
    return vh * g, vh, r


def _rms_bwd(dy, vh, r, g):
    dvh = dy * g
    dv = r * (dvh - vh * jnp.mean(dvh * vh, axis=-1, keepdims=True))
    return dv, jnp.sum(dy * vh, axis=0, keepdims=True)


def _silu_grad(v, s):
    return s * (1.0 + v * (1.0 - s))


def _row(i, n):
    return pl.BlockSpec((n[0], n[1]), lambda *a: (a[i], 0))


def _full(shape):
    nd = len(shape)
    return pl.BlockSpec(shape, lambda *a: (0,) * nd)


def _place():
    x, y, c = lax.axis_index("x"), lax.axis_index("y"), lax.axis_index("c")
    others = [(1 - x, y), (x, 1 - y), (1 - x, 1 - y)]
    return x, y, c, 2 * x + y, others


def _remote(src, dst, send_sem, recv_sem, dev):
    return pltpu.make_async_remote_copy(src_ref=src, dst_ref=dst, send_sem=send_sem, recv_sem=recv_sem,
                                        device_id=dev, device_id_type=MESH)


SHARD_STEPS = 4


def _cast_into_slots(me, ws, dtypes):
    n = len(ws)

    def body(me_ref, *refs):
        for w_ref, o_ref, dtype in zip(refs[:n], refs[n:], dtypes):
            o_ref[0] = w_ref[...].astype(dtype)

    return pl.pallas_call(
        body, name="cast_shards",
        grid_spec=pltpu.PrefetchScalarGridSpec(
            num_scalar_prefetch=1, grid=(SHARD_STEPS,),
            in_specs=[pl.BlockSpec((w.shape[0] // SHARD_STEPS, w.shape[1]), lambda r, me_ref: (r, 0)) for w in ws],
            out_specs=[pl.BlockSpec((1, w.shape[0] // SHARD_STEPS, w.shape[1]), lambda r, me_ref: (me_ref[0], r, 0))
                       for w in ws]),
        out_shape=[jax.ShapeDtypeStruct((N_CHIPS,) + w.shape, dt) for w, dt in zip(ws, dtypes)],
        compiler_params=_cp("parallel"),
    )(me, *ws)


class _Gather:
    def __init__(self, bufs, send_sems, recv_sems):
        self.bufs, self.send_sems, self.recv_sems = bufs, send_sems, recv_sems
        self.x, self.y, self.c, self.me, self.others = _place()
        self.halves = [b.shape[1] // 2 for b in bufs]

    def _piece(self, k, chip, half):
        return self.bufs[k].at[chip, pl.ds(half * self.halves[k], self.halves[k]), :]

    def _ici(self, k, j, chip):
        ox, oy = self.others[j]
        piece = self._piece(k, chip, self.c)
        return _remote(piece, piece, self.send_sems.at[6 * k + j], self.recv_sems.at[6 * k + j], (ox, oy, self.c))

    def _relay(self, k):
        first = self.c == 0
        piece = self._piece(k, jnp.where(first, self.chip(0), self.chip(1)), self.c)
        to = (jnp.where(first, self.others[1][0], self.others[0][0]),
              jnp.where(first, self.others[1][1], self.others[0][1]), self.c)
        return _remote(piece, piece, self.send_sems.at[6 * k + 2], self.recv_sems.at[6 * k + 2], to)

    def _pair(self, k, j, half):
        ox, oy = self.others[j]
        piece = self._piece(k, 2 * ox + oy, half)
        return _remote(piece, piece, self.send_sems.at[6 * k + 3 + j], self.recv_sems.at[6 * k + 3 + j],
                       (self.x, self.y, 1 - self.c))

    def _each(self, ks=None):
        return [(k, j) for k in (range(len(self.bufs)) if ks is None else ks) for j in range(3)]

    def chip(self, j):
        ox, oy = self.others[j]
        return 2 * ox + oy

    def start(self):
        for k in range(len(self.bufs)):
            for j in range(2):
                self._ici(k, j, self.me).start()

    def forward(self, pairs=None):
        for k, j in self._each() if pairs is None else pairs:
            self._ici(k, j, self.chip(j)).wait_recv()
            self._pair(k, j, self.c).start()
            if j < 2:
                pl.when(self.c == j)(self._relay(k).start)

    def landed(self, pairs):
        for k, j in pairs:
            self._pair(k, j, 1 - self.c).wait_recv()

    def finish(self, ks=None):
        self.landed(self._each(ks))
        for k in range(len(self.bufs)):
            for j in range(2):
                self._ici(k, j, self.me).wait_send()
            self._relay(k).wait_send()
            for j in range(3):
                self._pair(k, j, self.c).wait_send()

    @staticmethod
    def scratch(n):
        return [pltpu.SemaphoreType.DMA((6 * n,)), pltpu.SemaphoreType.DMA((6 * n,))]


def _carried(bufs):
    n = len(bufs)
    return dict(in_specs=[ANY] * n, out_specs=[ANY] * n,
                out_shape=[jax.ShapeDtypeStruct(b.shape, b.dtype) for b in bufs], scratch=_Gather.scratch(n))


def _mix_in(x, g_mix, order, carry):
    s = x.shape[0]
    tm = min(2 * ROW_TILE, s)
    n = s // tm
    nc = len(carry)
    cs = _carried(carry)

    def body(order_ref, x_ref, g_ref, *refs):
        z_ref = refs[nc]
        bufs = refs[nc + 1:2 * nc + 1]
        h_ref, w_ref, w_sem = refs[2 * nc + 1:2 * nc + 4]
        gather = _Gather(bufs, *refs[2 * nc + 4:])
        q, i = pl.program_id(0), pl.program_id(1)
        first = i == 0
        pl.when(jnp.logical_and(q == 0, first))(gather.start)
        for j in range(3):

            @pl.when(jnp.logical_and(q == j + 1, first))
            def _():
                gather.forward([(0, j)])
                gather.landed([(0, j)])
                if j == 1:
                    gather.forward([(k, jj) for k in range(1, nc) for jj in range(2)])

        @pl.when(first)
        def _():
            load = pltpu.make_async_copy(bufs[0].at[order_ref[q]], w_ref, w_sem)
            load.start()
            load.wait()

        @pl.when(q == 0)
        def _():
            h, _, _ = _rms_fwd(x_ref[...], g_ref[...])
            h_ref[i] = h.astype(bf16)

        z_ref[...] = _dot(h_ref[i], w_ref[...])

        @pl.when(jnp.logical_and(q == N_CHIPS - 1, i == n - 1))
        def _():
            gather.forward([(k, 2) for k in range(1, nc)])
            gather.finish(range(1, nc))

    res = pl.pallas_call(
        body, name="mix_in",
        grid_spec=pltpu.PrefetchScalarGridSpec(
            num_scalar_prefetch=1, grid=(N_CHIPS, n),
            in_specs=[pl.BlockSpec((tm, D_MODEL), lambda q, i, order_ref: (jnp.where(q == 0, i, 0), 0)),
                      pl.BlockSpec((1, D_MODEL), lambda q, i, order_ref: (0, 0))] + cs["in_specs"],
            out_specs=[pl.BlockSpec((tm, W_IN_CHUNK), lambda q, i, order_ref: (i, order_ref[q]))] + cs["out_specs"],
            scratch_shapes=[pltpu.VMEM((n, tm, D_MODEL), bf16), pltpu.VMEM((D_MODEL, W_IN_CHUNK), bf16),
                            pltpu.SemaphoreType.DMA(())] + cs["scratch"]),
        out_shape=[jax.ShapeDtypeStruct((s, W_IN_COLS), f32)] + cs["out_shape"],
        input_output_aliases={3 + k: 1 + k for k in range(nc)},
        compiler_params=_cp("arbitrary", "arbitrary"),
    )(order, x, g_mix, *carry)
    return res[0], res[1:]


def _pool_counts(tm, w, first_row):
    t1 = (lax.broadcasted_iota(jnp.int32, (tm, 1), 0) + first_row + 1).astype(f32)
    return jnp.minimum(t1, float(w))


def _conv_pool_out(z, x, conv_w, conv_b, ln_g, ln_b, pool_w, pool_scale, w_out, g_ffn, carry):
    s = x.shape[0]
    tm = min(ROW_TILE, s)
    n = s // tm
    hb = tm // HALO
    nv = tm // 8
    assert nv >= SHIFT_PAD and nv % SHIFT_GROUPS == 0
    nc = len(carry)
    cs = _carried(carry)

    def body(z_ref, zp_ref, x_ref, cw_ref, cb_ref, lg_ref, lb_ref, pw_ref, ps_ref, wo_ref, gf_ref, *refs):
        x1_ref, mix_ref, u1_ref, pooled_ref, h2_ref = refs[nc:nc + 5]
        eu, ev, ss, u0_ref = refs[2 * nc + 5:2 * nc + 9]
        gather = _Gather(refs[nc + 5:2 * nc + 5], *refs[2 * nc + 9:])
        i = pl.program_id(0)
        pl.when(i == 0)(gather.start)
        for k in range(nc):
            pl.when(i == min(3 * n // 4 + k, n - 1))(functools.partial(gather.forward, [(k, 0), (k, 1)]))
        keep = (i > 0).astype(f32)
        zp = zp_ref[...] * keep
        u0_prev = zp[:, :C_CONV] * _sigmoid(zp[:, C_CONV:2 * C_CONV])
        u0_ref[...] = z_ref[:, :C_CONV] * _sigmoid(z_ref[:, C_CONV:2 * C_CONV])
        for c, w_pool in enumerate(POOL_WINDOWS):
            lanes = slice(c * 128, (c + 1) * 128)
            v_lanes = slice(2 * C_CONV + c * 128, 2 * C_CONV + (c + 1) * 128)
            eu[c, pl.ds(0, SHIFT_PAD, stride=8), :] = u0_prev[:, lanes]
            ev[c, pl.ds(0, SHIFT_PAD, stride=8), :] = zp[:, v_lanes]
            for j in range(8):
                rows = slice(j * nv, (j + 1) * nv)
                eu[c, pl.ds(SHIFT_PAD * 8 + j, nv, stride=8), :] = u0_ref[rows, lanes]
                ev[c, pl.ds(SHIFT_PAD * 8 + j, nv, stride=8), :] = z_ref[rows, v_lanes]
                if j >= 1:
                    edge = slice(j * nv - SHIFT_PAD, j * nv)
                    eu[c, pl.ds(j, SHIFT_PAD, stride=8), :] = u0_ref[edge, lanes]
                    ev[c, pl.ds(j, SHIFT_PAD, stride=8), :] = z_ref[edge, v_lanes]
            for v0 in range(0, nv, SHIFT_GROUPS):
                span = SHIFT_GROUPS * 8
                acc = jnp.zeros((span, 128), f32) + cb_ref[:, lanes]
                for k in range(CONV_K):
                    acc = acc + cw_ref[pl.ds(k, 1), lanes] * eu[c, pl.ds((SHIFT_PAD - (CONV_K - 1) + v0 + k) * 8, span), :]
                ss[0, v0 * 8:v0 * 8 + span, :] = acc
                acc = ev[c, pl.ds((SHIFT_PAD + v0) * 8, span), :]
                for d in range(1, w_pool):
                    acc = acc + ev[c, pl.ds((SHIFT_PAD + v0 - d) * 8, span), :]
                ss[1, v0 * 8:v0 * 8 + span, :] = acc
            for j in range(8):
                rows = slice(j * nv, (j + 1) * nv)
                u1_ref[rows, lanes] = ss[0, pl.ds(j, nv, stride=8), :]
                mean = ss[1, pl.ds(j, nv, stride=8), :] / _pool_counts(nv, w_pool, i * tm + j * nv)
                pooled_ref[rows, lanes] = (mean - z_ref[rows, v_lanes]).astype(bf16)
        u1 = u1_ref[...]
        mu = jnp.mean(u1, axis=-1, keepdims=True)
        uc = u1 - mu
        rstd = lax.rsqrt(jnp.mean(uc * uc, axis=-1, keepdims=True) + EPS)
        u2 = uc * rstd * lg_ref[...] + lb_ref[...]
        mix_ref[:, :C_CONV] = (u2 * _sigmoid(u2)).astype(bf16)
        for g in range(len(POOL_WINDOWS)):
            cols = slice(g * POOL_GROUP, (g + 1) * POOL_GROUP)
            mixed = _dot(pooled_ref[:, cols], pw_ref[g].astype(bf16))
            mix_ref[:, C_CONV + g * POOL_GROUP:C_CONV + (g + 1) * POOL_GROUP] = (mixed * ps_ref[:, cols]).astype(bf16)
        x1 = x_ref[...] + _dot(mix_ref[...], wo_ref[...])
        x1_ref[...] = x1
        h2_ref[...] = _rms_fwd(x1, gf_ref[...])[0].astype(bf16)
        @pl.when(i == n - 1)
        def _():
            gather.forward([(k, 2) for k in range(nc)])
            gather.finish()

    res = pl.pallas_call(
        body, name="conv_pool_out", grid=(n,),
        in_specs=[_row(0, (tm, W_IN_COLS)),
                  pl.BlockSpec((HALO, W_IN_COLS), lambda i: (jnp.maximum(i * hb - 1, 0), 0)),
                  _row(0, (tm, D_MODEL)), _full((HALO, C_CONV)), _full((1, C_CONV)), _full((1, C_CONV)),
                  _full((1, C_CONV)), _full((4, POOL_GROUP, POOL_GROUP)), _full((1, C_POOL)),
                  _full((D_MODEL, D_MODEL)), _full((1, D_MODEL))] + cs["in_specs"],
        out_specs=[_row(0, (tm, D_MODEL)), _row(0, (tm, D_MODEL)), _row(0, (tm, C_CONV)), _row(0, (tm, C_POOL)),
                   _row(0, (tm, D_MODEL))] + cs["out_specs"],
        out_shape=[jax.ShapeDtypeStruct((s, D_MODEL), f32), jax.ShapeDtypeStruct((s, D_MODEL), bf16),
                   jax.ShapeDtypeStruct((s, C_CONV), f32), jax.ShapeDtypeStruct((s, C_POOL), bf16),
                   jax.ShapeDtypeStruct((s, D_MODEL), bf16)] + cs["out_shape"],
        input_output_aliases={11 + k: 5 + k for k in range(nc)},
        scratch_shapes=[pltpu.VMEM((4, (SHIFT_PAD + nv) * 8, 128), f32), pltpu.VMEM((4, (SHIFT_PAD + nv) * 8, 128), f32),
                        pltpu.VMEM((2, tm, 128), f32), pltpu.VMEM((tm, C_CONV), f32)] + cs["scratch"],
        compiler_params=_cp("arbitrary"),
    )(z, z, x, conv_w, conv_b, ln_g, ln_b, pool_w, pool_scale, w_out, g_ffn, *carry)
    return res[:5], res[5:]


def _ffn_up(h2, w_gu_g, carry):
    s = h2.shape[0]
    tm = min(2 * ROW_TILE, s)
    n = s // tm
    nc = len(carry)
    cs = _carried(carry)

    def body(h2_ref, wg_ref, wu_ref, *refs):
        gu_ref, f_ref = refs[nc:nc + 2]
        gather = _Gather(refs[nc + 2:2 * nc + 2], *refs[2 * nc + 2:])
        c, i = pl.program_id(0), pl.program_id(1)
        pl.when(jnp.logical_and(i == 0, c == 0))(gather.start)
        direct = [(k, j) for k in range(nc) for j in range(2)]
        pl.when(jnp.logical_and(i == n - 1, c == 0))(functools.partial(gather.forward, direct))
        pl.when(jnp.logical_and(i == n // 2, c == 1))(functools.partial(gather.forward, [(k, 2) for k in range(nc)]))
        h = h2_ref[...]
        for lo, hi in zip(FF_SUB[:-1], FF_SUB[1:]):
            gate = _dot(h, wg_ref[0, :, lo:hi])
            up = _dot(h, wu_ref[0, :, lo:hi])
            gu_ref[0, :, lo:hi] = gate.astype(bf16)
            gu_ref[1, :, lo:hi] = up.astype(bf16)
            f_ref[:, lo:hi] = (gate * _sigmoid(gate) * up).astype(bf16)
        pl.when(jnp.logical_and(i == n - 1, c == 1))(gather.finish)

    res = pl.pallas_call(
        body, name="ffn_up", grid=(2, n),
        in_specs=[_row(1, (tm, D_MODEL)),
                  pl.BlockSpec((1, D_MODEL, FF_CHUNK), lambda c, i: (c, 0, 0)),
                  pl.BlockSpec((1, D_MODEL, FF_CHUNK), lambda c, i: (2 + c, 0, 0))] + cs["in_specs"],
        out_specs=[pl.BlockSpec((2, tm, FF_CHUNK), lambda c, i: (0, i, c)),
                   pl.BlockSpec((tm, FF_CHUNK), lambda c, i: (i, c))] + cs["out_specs"],
        out_shape=[jax.ShapeDtypeStruct((2, s, D_FF), bf16), jax.ShapeDtypeStruct((s, D_FF), bf16)] + cs["out_shape"],
        input_output_aliases={3 + k: 2 + k for k in range(nc)},
        scratch_shapes=cs["scratch"],
        compiler_params=_cp("arbitrary", "arbitrary"),
    )(h2, w_gu_g, w_gu_g, *carry)
    return res[:2], res[2:]


def _ffn_down(x1, f, w_down):
    s = x1.shape[0]
    tm = min(2 * ROW_TILE, s)

    def body(x1_ref, f_ref, wd_ref, x2_ref):
        x2_ref[...] = x1_ref[...] + _dot(f_ref[...], wd_ref[...])

    return pl.pallas_call(
        body, name="ffn_down", grid=(s // tm,),
        in_specs=[_row(0, (tm, D_MODEL)), _row(0, (tm, D_FF)), _full((D_FF, D_MODEL))],
        out_specs=_row(0, (tm, D_MODEL)), out_shape=jax.ShapeDtypeStruct((s, D_MODEL), f32),
        compiler_params=_cp("parallel"),
    )(x1, f, w_down)


def _ple_loss(x2, p, target, g_pg, g_post, g_final, w_pg, w_pu_g):
    s = x2.shape[0]
    tm = min(ROW_TILE, s)
    n = s // tm

    def body(x2_ref, p_ref, t_ref, gpg_ref, gpo_ref, gf_ref, wpg_ref, wpu_ref,
             dx2_ref, dwpg_ref, dwpu_ref, small_ref, apg_ref, apu_ref):
        i = pl.program_id(0)

        @pl.when(i == 0)
        def _():
            apg_ref[...] = jnp.zeros_like(apg_ref)
            apu_ref[...] = jnp.zeros_like(apu_ref)
            small_ref[...] = jnp.zeros_like(small_ref)

        x2 = x2_ref[...]
        h3, x2h, r2 = _rms_fwd(x2, gpg_ref[...])
        h3b = h3.astype(bf16)
        gate = _sigmoid(_dot(h3b, wpg_ref[...]))
        pb = p_ref[...].astype(bf16)
        pe = jnp.concatenate([_dot(pb, wpu_ref[j]) for j in range(N_CHIPS)], axis=-1)
        e, peh, rp = _rms_fwd(pe, gpo_ref[...])
        x3 = x2 + gate * e
        y, x3h, r3 = _rms_fwd(x3, gf_ref[...])
        d = y - t_ref[...]
        loss = 0.5 * jnp.sum(jnp.sum(d * d, axis=-1, keepdims=True) * (1.0 / D_MODEL), axis=0, keepdims=True)
        dx3, dgf = _rms_bwd(d * (1.0 / D_MODEL), x3h, r3, gf_ref[...])
        dpe, dgpo = _rms_bwd(dx3 * gate, peh, rp, gpo_ref[...])
        dgl = (dx3 * e * gate * (1.0 - gate)).astype(bf16)
        apg_ref[...] += _dot_tn(h3b, dgl)
        apu_ref[...] += _dot_tn(pb, dpe.astype(bf16))
        dh3 = _dot_nt(dgl, wpg_ref[...])
        dx2b, dgpg = _rms_bwd(dh3, x2h, r2, gpg_ref[...])
        dx2_ref[...] = dx3 + dx2b
        small_ref[0:1, :] += dgpg
        small_ref[1:2, :] += dgpo
        small_ref[2:3, :] += dgf
        small_ref[3:4, :] += jnp.broadcast_to(loss, (1, D_MODEL))

        @pl.when(i == n - 1)
        def _():
            dwpg_ref[...] = apg_ref[...].astype(bf16)
            for j in range(N_CHIPS):
                dwpu_ref[j] = apu_ref[:, j * PLE_CHUNK:(j + 1) * PLE_CHUNK].astype(bf16)

    return pl.pallas_call(
        body, name="ple_loss", grid=(n,),
        in_specs=[_row(0, (tm, D_MODEL)), _row(0, (tm, D_PLE)), _row(0, (tm, D_MODEL)),
                  _full((1, D_MODEL)), _full((1, D_MODEL)), _full((1, D_MODEL)),
                  _full((D_MODEL, D_MODEL)), _full((N_CHIPS, D_PLE, PLE_CHUNK))],
        out_specs=[_row(0, (tm, D_MODEL)), _full((D_MODEL, D_MODEL)), _full((N_CHIPS, D_PLE, PLE_CHUNK)),
                   _full((8, D_MODEL))],
        out_shape=[jax.ShapeDtypeStruct((s, D_MODEL), f32), jax.ShapeDtypeStruct((D_MODEL, D_MODEL), bf16),
                   jax.ShapeDtypeStruct((N_CHIPS, D_PLE, PLE_CHUNK), bf16), jax.ShapeDtypeStruct((8, D_MODEL), f32)],
        scratch_shapes=[pltpu.VMEM((D_MODEL, D_MODEL), f32), pltpu.VMEM((D_PLE, D_MODEL), f32)],
        compiler_params=_cp("arbitrary"),
    )(x2, p, target, g_pg, g_post, g_final, w_pg, w_pu_g)


def _crossed(parts):
    n = len(parts)
    return dict(in_specs=[ANY] * n, out_specs=[ANY] * n,
                out_shape=[jax.ShapeDtypeStruct(a.shape, a.dtype) for a in parts], scratch=_Cross.scratch(n))


def _ffn_bwd_dx(dx2, x1, gu, g_ffn, w_gu_g, w_down, grads):
    s = x1.shape[0]
    tm = min(ROW_TILE, s)
    n = s // tm
    nc = len(grads)
    cs = dict(in_specs=[ANY] * nc, out_specs=[ANY] * nc, scratch=_CrossAll.scratch(nc),
              out_shape=[jax.ShapeDtypeStruct((N_DEV, g.shape[1] // 2, g.shape[2]), g.dtype) for g in grads])

    def body(dx2_ref, x1_ref, gu_ref, g_ref, wg_ref, wu_ref, wd_ref, *refs):
        dx1_ref, dgu_ref, small_ref = refs[nc:nc + 3]
        acc_ref = refs[2 * nc + 3]
        cross = _CrossAll(refs[:nc], refs[nc + 3:2 * nc + 3], *refs[2 * nc + 4:])
        i, c = pl.program_id(0), pl.program_id(1)
        pl.when(jnp.logical_and(i == 0, c == 0))(cross.start)

        @pl.when(jnp.logical_and(i == 0, c == 0))
        def _():
            small_ref[...] = jnp.zeros_like(small_ref)

        @pl.when(c == 0)
        def _():
            acc_ref[...] = jnp.zeros_like(acc_ref)

        dyb = dx2_ref[...].astype(bf16)
        for lo, hi in zip(FF_SUB[:-1], FF_SUB[1:]):
            df = _dot_nt(dyb, wd_ref[lo:hi, :])
            gate = gu_ref[0, :, lo:hi].astype(f32)
            up = gu_ref[1, :, lo:hi].astype(f32)
            sg = _sigmoid(gate)
            dgate = (df * up * _silu_grad(gate, sg)).astype(bf16)
            dup = (df * gate * sg).astype(bf16)
            dgu_ref[0, :, lo:hi] = dgate
            dgu_ref[1, :, lo:hi] = dup
            acc_ref[...] += _dot_nt(dgate, wg_ref[0, :, lo:hi]) + _dot_nt(dup, wu_ref[0, :, lo:hi])

        @pl.when(c == 1)
        def _():
            _, x1h, r1 = _rms_fwd(x1_ref[...], g_ref[...])
            dx1b, dg = _rms_bwd(acc_ref[...], x1h, r1, g_ref[...])
            dx1_ref[...] = dx2_ref[...] + dx1b
            small_ref[0:1, :] += dg

        pl.when(jnp.logical_and(i == n - 1, c == 1))(cross.finish)

    res = pl.pallas_call(
        body, name="ffn_bwd_dx", grid=(n, 2),
        in_specs=[_row(0, (tm, D_MODEL)), _row(0, (tm, D_MODEL)),
                  pl.BlockSpec((2, tm, FF_CHUNK), lambda i, c: (0, i, c)), _full((1, D_MODEL)),
                  pl.BlockSpec((1, D_MODEL, FF_CHUNK), lambda i, c: (c, 0, 0)),
                  pl.BlockSpec((1, D_MODEL, FF_CHUNK), lambda i, c: (2 + c, 0, 0)),
                  pl.BlockSpec((FF_CHUNK, D_MODEL), lambda i, c: (c, 0))] + cs["in_specs"],
        out_specs=[_row(0, (tm, D_MODEL)), pl.BlockSpec((2, tm, FF_CHUNK), lambda i, c: (0, i, c)),
                   _full((8, D_MODEL))] + cs["out_specs"],
        out_shape=[jax.ShapeDtypeStruct((s, D_MODEL), f32), jax.ShapeDtypeStruct((2, s, D_FF), bf16),
                   jax.ShapeDtypeStruct((8, D_MODEL), f32)] + cs["out_shape"],
        scratch_shapes=[pltpu.VMEM((tm, D_MODEL), f32)] + cs["scratch"],
        compiler_params=_cp("arbitrary", "arbitrary"),
    )(dx2, x1, gu, g_ffn, w_gu_g, w_gu_g, w_down, *grads)
    return res[:3], res[3:]


def _ffn_bwd_dw_gu(h2, dgu):
    s = h2.shape[0]
    ts = min(2 * ROW_TILE, s)
    n = s // ts

    def body(h_ref, d_ref, o_ref, acc_ref):
        t = pl.program_id(1)

        @pl.when(t == 0)
        def _():
            acc_ref[...] = jnp.zeros_like(acc_ref)

        acc_ref[...] += _dot_tn(h_ref[...], d_ref[0])

        @pl.when(t == n - 1)
        def _():
            o_ref[0] = acc_ref[...].astype(bf16)

    return pl.pallas_call(
        body, name="ffn_bwd_dw_gu", grid=(N_CHIPS, n),
        in_specs=[pl.BlockSpec((ts, D_MODEL), lambda j, t: (t, 0)),
                  pl.BlockSpec((1, ts, FF_CHUNK), lambda j, t: (j // 2, t, j % 2))],
        out_specs=pl.BlockSpec((1, D_MODEL, FF_CHUNK), lambda j, t: (j, 0, 0)),
        out_shape=jax.ShapeDtypeStruct((N_CHIPS, D_MODEL, FF_CHUNK), bf16),
        scratch_shapes=[pltpu.VMEM((D_MODEL, FF_CHUNK), f32)],
        compiler_params=_cp("parallel", "arbitrary"),
    )(h2, dgu)


def _ffn_bwd_dw_down(f, dx2):
    s = dx2.shape[0]
    ts = min(2 * ROW_TILE, s)
    n = s // ts

    def body(f_ref, d_ref, o_ref, acc_ref):
        t = pl.program_id(1)

        @pl.when(t == 0)
        def _():
            acc_ref[...] = jnp.zeros_like(acc_ref)

        acc_ref[...] += _dot_tn(f_ref[...], d_ref[...].astype(bf16))

        @pl.when(t == n - 1)
        def _():
            o_ref[...] = acc_ref[...].astype(bf16)

    return pl.pallas_call(
        body, name="ffn_bwd_dw_down", grid=(2, n),
        in_specs=[pl.BlockSpec((ts, FF_CHUNK), lambda c, t: (t, c)),
                  pl.BlockSpec((ts, D_MODEL), lambda c, t: (t, 0))],
        out_specs=pl.BlockSpec((FF_CHUNK, D_MODEL), lambda c, t: (c, 0)),
        out_shape=jax.ShapeDtypeStruct((D_FF, D_MODEL), bf16),
        scratch_shapes=[pltpu.VMEM((FF_CHUNK, D_MODEL), f32)],
        compiler_params=_cp("parallel", "arbitrary"),
    )(f, dx2)


def _mix_bwd_local(dx1, mix, u1, pooled, w_out, ln_g, ln_b, pool_w, pool_scale):
    s = dx1.shape[0]
    tm = min(ROW_TILE, s)
    n = s // tm

    def body(dx1_ref, mix_ref, u1_ref, po_ref, wo_ref, lg_ref, lb_ref, pw_ref, ps_ref,
             du1_ref, dpo_ref, dwo_ref, dpw_ref, small_ref, awo_ref):
        i = pl.program_id(0)

        @pl.when(i == 0)
        def _():
            awo_ref[...] = jnp.zeros_like(awo_ref)
            dpw_ref[...] = jnp.zeros_like(dpw_ref)
            small_ref[...] = jnp.zeros_like(small_ref)

        dyb = dx1_ref[...].astype(bf16)
        dmix = _dot_nt(dyb, wo_ref[...])
        awo_ref[...] += _dot_tn(mix_ref[...], dyb)
        u1 = u1_ref[...]
        mu = jnp.mean(u1, axis=-1, keepdims=True)
        uc = u1 - mu
        rstd = lax.rsqrt(jnp.mean(uc * uc, axis=-1, keepdims=True) + EPS)
        uh = uc * rstd
        u2 = uh * lg_ref[...] + lb_ref[...]
        du2 = dmix[:, :C_CONV] * _silu_grad(u2, _sigmoid(u2))
        duh = du2 * lg_ref[...]
        du1 = rstd * (duh - jnp.mean(duh, axis=-1, keepdims=True) - uh * jnp.mean(duh * uh, axis=-1, keepdims=True))
        du1_ref[...] = du1
        small_ref[0:1, :] += jnp.sum(du2 * uh, axis=0, keepdims=True)
        small_ref[1:2, :] += jnp.sum(du2, axis=0, keepdims=True)
        small_ref[2:3, :] += jnp.sum(du1, axis=0, keepdims=True)
        for g in range(len(POOL_WINDOWS)):
            cols = slice(g * POOL_GROUP, (g + 1) * POOL_GROUP)
            dq = dmix[:, C_CONV + g * POOL_GROUP:C_CONV + (g + 1) * POOL_GROUP]
            pwb = pw_ref[g].astype(bf16)
            pg = po_ref[:, cols]
            mixed = _dot(pg, pwb)
            small_ref[3:4, cols] += jnp.sum(dq * mixed, axis=0, keepdims=True)
            dmixed = (dq * ps_ref[:, cols]).astype(bf16)
            dpw_ref[g] += _dot_tn(pg, dmixed)
            dpo_ref[:, cols] = _dot_nt(dmixed, pwb)

        @pl.when(i == n - 1)
        def _():
            dwo_ref[...] = awo_ref[...].astype(bf16)

    return pl.pallas_call(
        body, name="mix_bwd_local", grid=(n,),
        in_specs=[_row(0, (tm, D_MODEL)), _row(0, (tm, D_MODEL)), _row(0, (tm, C_CONV)), _row(0, (tm, C_POOL)),
                  _full((D_MODEL, D_MODEL)), _full((1, C_CONV)), _full((1, C_CONV)),
                  _full((4, POOL_GROUP, POOL_GROUP)), _full((1, C_POOL))],
        out_specs=[_row(0, (tm, C_CONV)), _row(0, (tm, C_POOL)), _full((D_MODEL, D_MODEL)),
                   _full((4, POOL_GROUP, POOL_GROUP)), _full((8, C_CONV))],
        out_shape=[jax.ShapeDtypeStruct((s, C_CONV), f32), jax.ShapeDtypeStruct((s, C_POOL), f32),
                   jax.ShapeDtypeStruct((D_MODEL, D_MODEL), bf16),
                   jax.ShapeDtypeStruct((4, POOL_GROUP, POOL_GROUP), f32), jax.ShapeDtypeStruct((8, C_CONV), f32)],
        scratch_shapes=[pltpu.VMEM((D_MODEL, D_MODEL), f32)],
        compiler_params=_cp("arbitrary"),
    )(dx1, mix, u1, pooled, w_out, ln_g, ln_b, pool_w, pool_scale)


def _in_bwd(du1, dpo, z, x, dx1, conv_w, g_mix, w_in_g, parts, small_parts):
    s = x.shape[0]
    tm = min(ROW_TILE, s)
    n = s // tm
    hb = tm // HALO
    last = s // HALO - 1
    nv = tm // 8
    assert nv >= SHIFT_PAD and nv % SHIFT_GROUPS == 0
    nc, ns = len(parts), len(small_parts)
    cs = _crossed(parts)

    def body(du_ref, dun_ref, dp_ref, dpn_ref, z_ref, zp_ref, x_ref, dx1_ref, cw_ref, g_ref, w_ref, *refs):
        outs = refs[nc + ns:]
        gx_ref, dw_ref, dcw_ref, small_ref = outs[:4]
        eu, ed, ep, ss, u0_ref, dz_ref, acc_ref, dcw_acc = outs[4 + nc + ns:12 + nc + ns]
        sems = outs[12 + nc + ns:]
        cross = _Cross(refs[:nc], outs[4:4 + nc], sems[0], sems[1])
        gather = _SmallGather(refs[nc:nc + ns], outs[4 + nc:4 + nc + ns], sems[2], sems[3])
        i = pl.program_id(0)
        pl.when(i == 0)(cross.start)
        pl.when(i == 0)(gather.start)

        @pl.when(i == 0)
        def _():
            acc_ref[...] = jnp.zeros_like(acc_ref)
            dcw_acc[...] = jnp.zeros_like(dcw_acc)
            small_ref[...] = jnp.zeros_like(small_ref)

        keep_prev = (i > 0).astype(f32)
        keep_next = (i < n - 1).astype(f32)
        zp = zp_ref[...] * keep_prev
        u0_prev = zp[:, :C_CONV] * _sigmoid(zp[:, C_CONV:2 * C_CONV])
        u0_ref[...] = z_ref[:, :C_CONV] * _sigmoid(z_ref[:, C_CONV:2 * C_CONV])
        du_next = dun_ref[...] * keep_next
        for c, w_pool in enumerate(POOL_WINDOWS):
            lanes = slice(c * 128, (c + 1) * 128)
            eu[c, pl.ds(0, SHIFT_PAD, stride=8), :] = u0_prev[:, lanes]
            ed[c, pl.ds(nv * 8 + 7, SHIFT_PAD, stride=8), :] = du_next[:, lanes]
            ep[c, pl.ds(nv * 8 + 7, SHIFT_PAD, stride=8), :] = (
                dpn_ref[:, lanes] * keep_next / _pool_counts(HALO, w_pool, (i + 1) * tm))
            for j in range(8):
                rows = slice(j * nv, (j + 1) * nv)
                eu[c, pl.ds(SHIFT_PAD * 8 + j, nv, stride=8), :] = u0_ref[rows, lanes]
                ed[c, pl.ds(j, nv, stride=8), :] = du_ref[rows, lanes]
                ep[c, pl.ds(j, nv, stride=8), :] = dp_ref[rows, lanes] / _pool_counts(nv, w_pool, i * tm + j * nv)
                if j >= 1:
                    eu[c, pl.ds(j, SHIFT_PAD, stride=8), :] = u0_ref[j * nv - SHIFT_PAD:j * nv, lanes]
                if j <= 6:
                    edge = slice((j + 1) * nv, (j + 1) * nv + SHIFT_PAD)
                    ed[c, pl.ds(nv * 8 + j, SHIFT_PAD, stride=8), :] = du_ref[edge, lanes]
                    ep[c, pl.ds(nv * 8 + j, SHIFT_PAD, stride=8), :] = (
                        dp_ref[edge, lanes] / _pool_counts(SHIFT_PAD, w_pool, i * tm + (j + 1) * nv))
        for c, w_pool in enumerate(POOL_WINDOWS):
            lanes = slice(c * 128, (c + 1) * 128)
            b_lanes = slice(C_CONV + c * 128, C_CONV + (c + 1) * 128)
            v_lanes = slice(2 * C_CONV + c * 128, 2 * C_CONV + (c + 1) * 128)
            for v0 in range(0, nv, SHIFT_GROUPS):
                span = SHIFT_GROUPS * 8
                acc = jnp.zeros((span, 128), f32)
                for k in range(CONV_K):
                    acc = acc + cw_ref[pl.ds(k, 1), lanes] * ed[c, pl.ds((v0 + CONV_K - 1 - k) * 8, span), :]
                ss[0, v0 * 8:v0 * 8 + span, :] = acc
                acc = ep[c, pl.ds(v0 * 8, span), :]
                for d in range(1, w_pool):
                    acc = acc + ep[c, pl.ds((v0 + d) * 8, span), :]
                ss[1, v0 * 8:v0 * 8 + span, :] = acc
                d1 = ed[c, pl.ds(v0 * 8, span), :]
                for k in range(CONV_K):
                    prod = d1 * eu[c, pl.ds((SHIFT_PAD - (CONV_K - 1) + v0 + k) * 8, span), :]
                    fold = prod[0:8]
                    for r in range(8, span, 8):
                        fold = fold + prod[r:r + 8]
                    dcw_acc[k, :, lanes] += fold
            for j in range(8):
                rows = slice(j * nv, (j + 1) * nv)
                du0 = ss[0, pl.ds(j, nv, stride=8), :]
                av, sv = z_ref[rows, lanes], _sigmoid(z_ref[rows, b_lanes])
                dz_ref[rows, lanes] = (du0 * sv).astype(bf16)
                dz_ref[rows, b_lanes] = (du0 * av * sv * (1.0 - sv)).astype(bf16)
                dz_ref[rows, v_lanes] = (ss[1, pl.ds(j, nv, stride=8), :] - dp_ref[rows, lanes]).astype(bf16)
        h, xh, r = _rms_fwd(x_ref[...], g_ref[...])
        dz = dz_ref[...]
        acc_ref[...] += _dot_tn(h.astype(bf16), dz)
        dh = _dot_nt(dz[:, 0:W_IN_CHUNK], w_ref[0])
        for j in range(1, N_CHIPS):
            dh = dh + _dot_nt(dz[:, j * W_IN_CHUNK:(j + 1) * W_IN_CHUNK], w_ref[j])
        dxb, dg = _rms_bwd(dh, xh, r, g_ref[...])
        gx_ref[...] = dx1_ref[...] + dxb
        small_ref[0:1, :] += dg

        @pl.when(i == n - 1)
        def _():
            for j in range(N_CHIPS):
                dw_ref[j] = acc_ref[:, j * W_IN_CHUNK:(j + 1) * W_IN_CHUNK].astype(bf16)
            dcw_ref[...] = jnp.sum(dcw_acc[...], axis=1)

        pl.when(i == n - 1)(gather.finish)
        pl.when(i == n - 1)(cross.finish)

    nxt = lambda i: (jnp.minimum((i + 1) * hb, last), 0)
    res = pl.pallas_call(
        body, name="in_bwd", grid=(n,),
        in_specs=[_row(0, (tm, C_CONV)), pl.BlockSpec((HALO, C_CONV), nxt),
                  _row(0, (tm, C_POOL)), pl.BlockSpec((HALO, C_POOL), nxt),
                  _row(0, (tm, W_IN_COLS)),
                  pl.BlockSpec((HALO, W_IN_COLS), lambda i: (jnp.maximum(i * hb - 1, 0), 0)),
                  _row(0, (tm, D_MODEL)), _row(0, (tm, D_MODEL)), _full((HALO, C_CONV)), _full((1, D_MODEL)),
                  _full((N_CHIPS, D_MODEL, W_IN_CHUNK))] + cs["in_specs"] + [ANY] * ns,
        out_specs=[_row(0, (tm, D_MODEL)), _full((N_CHIPS, D_MODEL, W_IN_CHUNK)), _full((HALO, C_CONV)),
                   _full((8, D_MODEL))] + cs["out_specs"] + [ANY] * ns,
        out_shape=[jax.ShapeDtypeStruct((s, D_MODEL), f32), jax.ShapeDtypeStruct((N_CHIPS, D_MODEL, W_IN_CHUNK), bf16),
                   jax.ShapeDtypeStruct((HALO, C_CONV), f32), jax.ShapeDtypeStruct((8, D_MODEL), f32)] + cs["out_shape"]
        + _gathered_shapes(small_parts),
        scratch_shapes=[pltpu.VMEM((4, (SHIFT_PAD + nv) * 8, 128), f32), pltpu.VMEM((4, (nv + SHIFT_PAD) * 8, 128), f32),
                        pltpu.VMEM((4, (nv + SHIFT_PAD) * 8, 128), f32), pltpu.VMEM((2, tm, 128), f32),
                        pltpu.VMEM((tm, C_CONV), f32), pltpu.VMEM((tm, W_IN_COLS), bf16),
                        pltpu.VMEM((D_MODEL, W_IN_COLS), f32), pltpu.VMEM((HALO, 8, C_CONV), f32)] + cs["scratch"]
        + _SmallGather.scratch(ns),
        compiler_params=_cp("arbitrary"),
    )(du1, du1, dpo, dpo, z, z, x, dx1, conv_w, g_mix, w_in_g, *parts, *small_parts)
    return res[:4], res[4:4 + nc], res[4 + nc:]


def _pair_reduce(name, grads):
    nk = len(grads)
    halves = [g.shape[1] // 2 for g in grads]

    def body(*refs):
        ins, outs, got, own = refs[:nk], refs[nk:2 * nk], refs[2 * nk:3 * nk], refs[3 * nk:4 * nk]
        send_sems, recv_sems, load_sems = refs[4 * nk:]
        x, y, c, _, _ = _place()

        def half(k, core):
            return ins[k].at[:, pl.ds(core * halves[k], halves[k]), :]

        cps = [_remote(half(k, 1 - c), got[k], send_sems.at[k], recv_sems.at[k], (x, y, 1 - c)) for k in range(nk)]
        loads = [pltpu.make_async_copy(half(k, c), own[k], load_sems.at[k]) for k in range(nk)]
        for cp in cps + loads:
            cp.start()
        for k in range(nk):
            loads[k].wait()
            cps[k].wait_recv()
            outs[k][...] = (own[k][...].astype(f32) + got[k][...].astype(f32)).astype(bf16)
        for cp in cps:
            cp.wait_send()

    shapes = [(N_CHIPS, h, g.shape[2]) for g, h in zip(grads, halves)]
    return pl.pallas_call(
        body, name=f"pair_reduce_{name}", in_specs=[ANY] * nk, out_specs=[VMEM] * nk,
        out_shape=[jax.ShapeDtypeStruct(s, bf16) for s in shapes],
        scratch_shapes=[pltpu.VMEM(s, bf16) for s in shapes] * 2
        + [pltpu.SemaphoreType.DMA((nk,)), pltpu.SemaphoreType.DMA((nk,)), pltpu.SemaphoreType.DMA((nk,))],
        compiler_params=pltpu.CompilerParams(vmem_limit_bytes=VMEM_LIMIT),
    )(*grads)


class _Cross:
    def __init__(self, parts, landed, send_sems, recv_sems):
        self.parts, self.landed, self.send_sems, self.recv_sems = parts, landed, send_sems, recv_sems
        _, _, self.c, self.me, self.others = _place()

    def _copy(self, k, j, src_chunk, dst_slot):
        ox, oy = self.others[j]
        return _remote(self.parts[k].at[src_chunk], self.landed[k].at[dst_slot], self.send_sems.at[3 * k + j],
                       self.recv_sems.at[3 * k + j], (ox, oy, self.c))

    def _each(self):
        return [(k, j, 2 * self.others[j][0] + self.others[j][1]) for k in range(len(self.parts)) for j in range(3)]

    def start(self):
        for k, j, chip in self._each():
            self._copy(k, j, chip, self.me).start()

    def finish(self):
        for k, j, chip in self._each():
            self._copy(k, j, chip, chip).wait_recv()
        for k, j, chip in self._each():
            self._copy(k, j, chip, self.me).wait_send()

    @staticmethod
    def scratch(n):
        return [pltpu.SemaphoreType.DMA((3 * n,)), pltpu.SemaphoreType.DMA((3 * n,))]


class _CrossAll:
    def __init__(self, grads, landed, send_sems, recv_sems):
        self.grads, self.landed, self.send_sems, self.recv_sems = grads, landed, send_sems, recv_sems
        self.x, self.y, self.c, self.me, self.others = _place()
        self.dev = 2 * self.me + self.c

    def _piece(self, k, chip, half):
        rows = self.grads[k].shape[1] // 2
        return self.grads[k].at[chip, pl.ds(half * rows, rows), :]

    def _to_sibling(self, k):
        return _remote(self._piece(k, self.me, 1 - self.c), self.landed[k].at[self.dev], self.send_sems.at[7 * k],
                       self.recv_sems.at[7 * k], (self.x, self.y, 1 - self.c))

    def _to_chip(self, k, j, half):
        ox, oy = self.others[j]
        return _remote(self._piece(k, 2 * ox + oy, half), self.landed[k].at[self.dev],
                       self.send_sems.at[7 * k + 1 + 2 * j + half], self.recv_sems.at[7 * k + 1 + 2 * j + self.c],
                       (ox, oy, half))

    def _from(self, k, sem, slot):
        return _remote(self.landed[k].at[slot], self.landed[k].at[slot], self.send_sems.at[7 * k + sem],
                       self.recv_sems.at[7 * k + sem], (self.x, self.y, 1 - self.c))

    def start(self):
        for k in range(len(self.grads)):
            self._to_sibling(k).start()
            for j in range(3):
                for half in range(2):
                    self._to_chip(k, j, half).start()

    def finish(self):
        for k in range(len(self.grads)):
            self._from(k, 0, 2 * self.me + 1 - self.c).wait_recv()
            for j, (ox, oy) in enumerate(self.others):
                for core in range(2):
                    self._from(k, 1 + 2 * j + core, 4 * ox + 2 * oy + core).wait_recv()
        for k in range(len(self.grads)):
            self._to_sibling(k).wait_send()
            for j in range(3):
                for half in range(2):
                    self._to_chip(k, j, half).wait_send()

    @staticmethod
    def scratch(n):
        return [pltpu.SemaphoreType.DMA((7 * n,)), pltpu.SemaphoreType.DMA((7 * n,))]


def _rs_sum_chips(name, place, landed, part):
    _, half, cols = landed.shape

    def body(place_ref, l_ref, p_ref, o_ref):
        me = place_ref[0]
        own = p_ref[0].astype(f32)
        acc = jnp.where(me == 0, own, l_ref[0].astype(f32))
        for j in range(1, N_CHIPS):
            acc = acc + jnp.where(me == j, own, l_ref[j].astype(f32))
        o_ref[...] = acc

    return pl.pallas_call(
        body, name=f"rs_sum_chips_{name}",
        grid_spec=pltpu.PrefetchScalarGridSpec(
            num_scalar_prefetch=1, grid=(1,),
            in_specs=[pl.BlockSpec((N_CHIPS, half, cols), lambda t, place_ref: (0, 0, 0)),
                      pl.BlockSpec((1, half, cols), lambda t, place_ref: (place_ref[0], 0, 0))],
            out_specs=pl.BlockSpec((half, cols), lambda t, place_ref: (place_ref[1], 0))),
        out_shape=jax.ShapeDtypeStruct((2 * half, cols), f32),
        compiler_params=_cp("arbitrary"),
    )(place, landed, part)


def _grad_pair(shards):
    nk = len(shards)

    def body(*refs):
        outs = refs[nk:2 * nk]
        send_sems, recv_sems = refs[2 * nk:]
        x, y, c, _, _ = _place()

        def half(k, core):
            h = outs[k].shape[0] // 2
            return outs[k].at[pl.ds(core * h, h), :]

        cps = [_remote(half(k, c), half(k, c), send_sems.at[k], recv_sems.at[k], (x, y, 1 - c)) for k in range(nk)]
        for cp in cps:
            cp.start()
        for k in range(nk):
            _remote(half(k, 1 - c), half(k, 1 - c), send_sems.at[k], recv_sems.at[k], (x, y, 1 - c)).wait_recv()
        for cp in cps:
            cp.wait_send()

    return pl.pallas_call(
        body, name="grad_pair", in_specs=[ANY] * nk, out_specs=[ANY] * nk,
        out_shape=[jax.ShapeDtypeStruct(a.shape, f32) for a in shards],
        input_output_aliases={k: k for k in range(nk)},
        scratch_shapes=[pltpu.SemaphoreType.DMA((nk,)), pltpu.SemaphoreType.DMA((nk,))],
    )(*shards)


def _adam_math(w, g, m, v):
    m = ADAM_B1 * m + (1.0 - ADAM_B1) * g
    v = ADAM_B2 * v + (1.0 - ADAM_B2) * (g * g)
    m_hat = m / (1.0 - ADAM_B1 ** ADAM_STEP)
    v_hat = v / (1.0 - ADAM_B2 ** ADAM_STEP)
    delta = -ADAM_LR * (m_hat / (jnp.sqrt(v_hat) + ADAM_EPS) + ADAM_WD * w)
    return delta, m, v


def _adam(ws, gs, ms, vs):
    n = len(ws)
    steps = 2 * SHARD_STEPS

    def body(*refs):
        for k in range(n):
            w_ref, g_ref, m_ref, v_ref = (refs[j * n + k] for j in range(4))
            go_ref, d_ref, nm_ref, nv_ref = (refs[(4 + j) * n + k] for j in range(4))
            g = g_ref[...]
            go_ref[...] = g
            d_ref[...], nm_ref[...], nv_ref[...] = _adam_math(w_ref[...], g, m_ref[...], v_ref[...])

    specs = [_row(0, (w.shape[0] // steps, w.shape[1])) for w in ws]
    res = pl.pallas_call(
        body, name="adam_shards", grid=(steps,), in_specs=specs * 4, out_specs=specs * 4,
        out_shape=[jax.ShapeDtypeStruct(w.shape, f32) for w in ws] * 4,
        compiler_params=_cp("parallel"),
    )(*ws, *gs, *ms, *vs)
    return [tuple(res[j * n + k] for j in range(4)) for k in range(n)]


class _SmallGather:
    def __init__(self, parts, bufs, send_sems, recv_sems):
        self.parts, self.bufs, self.send_sems, self.recv_sems = parts, bufs, send_sems, recv_sems
        self.x, self.y, self.c, _, self.others = _place()
        self.sibling = (self.x, self.y, 1 - self.c)

    def _copy(self, a, k, block, to, src=None):
        slot = self.bufs[a].at[4 * block[0] + 2 * block[1] + block[2]]
        return _remote(slot if src is None else src, slot, self.send_sems.at[7 * a + k], self.recv_sems.at[7 * a + k],
                       to)

    def _first(self, a):
        me = (self.x, self.y, self.c)
        return [self._copy(a, 0, me, self.sibling, src=self.parts[a])] + [
            self._copy(a, 1 + j, me, (*chip, self.c), src=self.parts[a]) for j, chip in enumerate(self.others)]

    def _passed(self, a):
        return [self._copy(a, 4 + j, (*chip, self.c), self.sibling) for j, chip in enumerate(self.others)]

    @staticmethod
    def scratch(n):
        return [pltpu.SemaphoreType.DMA((7 * n,)), pltpu.SemaphoreType.DMA((7 * n,))]

    def start(self):
        for a in range(len(self.parts)):
            for cp in self._first(a):
                cp.start()

    def finish(self):
        sent = []
        for a in range(len(self.parts)):
            passed = self._passed(a)
            for j, chip in enumerate(self.others):
                self._copy(a, 1 + j, (*chip, self.c), self.sibling).wait_recv()
                passed[j].start()
            sent += self._first(a) + passed
        for a in range(len(self.parts)):
            self._copy(a, 0, self.sibling, self.sibling).wait_recv()
            for j, chip in enumerate(self.others):
                self._copy(a, 4 + j, (*chip, 1 - self.c), self.sibling).wait_recv()
        for cp in sent:
            cp.wait_send()


def _gathered_shapes(parts):
    return [jax.ShapeDtypeStruct((N_DEV,) + p.shape, p.dtype) for p in parts]


def _rs_sum_group(name, place, landed, parts, landed_all, grads_all, cross_parts, small_parts):
    nk, na, nx, ns = len(landed), len(landed_all), len(cross_parts), len(small_parts)
    dims = [a.shape[1:] for a in landed]
    dims_all = [a.shape[1:] for a in landed_all]

    def body(place_ref, *refs):
        take = iter(refs)
        l_refs, p_refs, la_refs, ga_refs, x_refs, sp_refs, o_refs, oa_refs, xl_refs, sbufs = (
            [next(take) for _ in range(cnt)] for cnt in (nk, nk, na, na, nx, ns, nk, na, nx, ns))
        sems = list(take)
        cross = _Cross(x_refs, xl_refs, sems[0], sems[1])
        small = _SmallGather(sp_refs, sbufs, sems[2], sems[3])
        t = pl.program_id(0)

        @pl.when(t == 0)
        def _():
            cross.start()
            small.start()

        me = place_ref[0]
        for l_ref, p_ref, o_ref in zip(l_refs, p_refs, o_refs):
            own = p_ref[0].astype(f32)
            acc = jnp.where(me == 0, own, l_ref[0].astype(f32))
            for j in range(1, N_CHIPS):
                acc = acc + jnp.where(me == j, own, l_ref[j].astype(f32))
            o_ref[...] = acc
        dev = 2 * me + place_ref[1]
        for l_ref, g_ref, o_ref in zip(la_refs, ga_refs, oa_refs):
            own = g_ref[0].astype(f32)
            acc = jnp.where(dev == 0, own, l_ref[0].astype(f32))
            for d in range(1, N_DEV):
                acc = acc + jnp.where(dev == d, own, l_ref[d].astype(f32))
            o_ref[...] = acc

        @pl.when(t == 1)
        def _():
            small.finish()
            cross.finish()

    def halves(h, c, lead, index):
        return pl.BlockSpec((lead, h // 2, c) if lead else (h // 2, c), index)

    in_specs = [halves(h, c, N_CHIPS, lambda t, pr: (0, t, 0)) for h, c in dims]
    in_specs += [halves(h, c, 1, lambda t, pr: (pr[0], t, 0)) for h, c in dims]
    in_specs += [halves(h, c, N_DEV, lambda t, pr: (0, t, 0)) for h, c in dims_all]
    in_specs += [halves(h, c, 1, lambda t, pr: (pr[0], 2 * pr[1] + t, 0)) for h, c in dims_all]
    in_specs += [ANY] * (nx + ns)
    out_specs = [halves(h, c, 0, lambda t, pr: (2 * pr[1] + t, 0)) for h, c in dims + dims_all] + [ANY] * (nx + ns)
    out_shape = [jax.ShapeDtypeStruct((2 * h, c), f32) for h, c in dims + dims_all]
    out_shape += [jax.ShapeDtypeStruct(a.shape, a.dtype) for a in cross_parts] + _gathered_shapes(small_parts)
    res = pl.pallas_call(
        body, name=f"rs_sum_{name}",
        grid_spec=pltpu.PrefetchScalarGridSpec(
            num_scalar_prefetch=1, grid=(2,), in_specs=in_specs, out_specs=out_specs,
            scratch_shapes=_Cross.scratch(nx) + _SmallGather.scratch(ns)),
        out_shape=out_shape, compiler_params=_cp("arbitrary"),
    )(place, *landed, *parts, *landed_all, *grads_all, *cross_parts, *small_parts)
    return res[:nk], res[nk:nk + na], res[nk + na:nk + na + nx], res[nk + na + nx:]


SMALL_PARAMS = ("g_ple_gate", "g_ple_post", "g_final", "g_ffn", "ln_g", "ln_b", "conv_b", "pool_scale", "pool_w",
                "conv_w", "g_mix")
SMALL_ROWS = {"g_ple_gate": (0, 0), "g_ple_post": (0, 1), "g_final": (0, 2), "g_ffn": (1, 0), "ln_g": (2, 0),
              "ln_b": (2, 1), "conv_b": (2, 2), "pool_scale": (2, 3), "g_mix": (5, 0)}
LOSS_ROW = (0, 3)


def _small_adam(place, gathered, parts, params):
    nb, names = len(parts), SMALL_PARAMS
    flat = [a for nm in names for a in params[nm]]

    def body(place_ref, *refs):
        b_refs, p_refs = refs[:nb], refs[nb:2 * nb]
        w_refs = refs[2 * nb:2 * nb + 3 * len(names)]
        outs = refs[2 * nb + 3 * len(names):]
        loss_ref, o_refs, cw_sum = outs[0], outs[1:1 + 4 * len(names)], outs[1 + 4 * len(names)]
        chip = place_ref[0]
        me = 2 * chip + place_ref[1]

        def total(blk, idx):
            own = p_refs[blk][idx]
            g = jnp.where(me == 0, own, b_refs[blk][(0,) + idx])
            for d in range(1, N_DEV):
                g = g + jnp.where(me == d, own, b_refs[blk][(d,) + idx])
            return g

        everything = (slice(None), slice(None))
        loss_ref[...] = total(LOSS_ROW[0], (pl.ds(LOSS_ROW[1], 1), pl.ds(0, 128)))
        d_cw = total(4, everything)
        mine = jnp.where(chip == 0, d_cw[:, 0:128], 0.0)
        for j in range(1, N_CHIPS):
            mine = mine + jnp.where(chip == j, d_cw[:, j * 128:(j + 1) * 128], 0.0)
        cw_sum[...] = mine
        for k, nm in enumerate(names):
            w_ref, m_ref, v_ref = w_refs[3 * k:3 * k + 3]
            g_ref, d_ref, nm_ref, nv_ref = o_refs[4 * k:4 * k + 4]
            if nm == "pool_w":
                g = total(3, everything + (slice(None),))
            elif nm == "conv_w":
                g = cw_sum[pl.ds(0, CONV_K), :]
            else:
                blk, row = SMALL_ROWS[nm]
                g = total(blk, (pl.ds(row, 1), slice(None)))
            g_ref[...] = g
            d_ref[...], nm_ref[...], nv_ref[...] = _adam_math(w_ref[...], g, m_ref[...], v_ref[...])

    whole = lambda a: pl.BlockSpec(a.shape, lambda t, pr: (0,) * a.ndim)
    out_shape = [jax.ShapeDtypeStruct((1, 128), f32)]
    out_shape += [jax.ShapeDtypeStruct(params[nm][0].shape, f32) for nm in names for _ in range(4)]
    res = pl.pallas_call(
        body, name="small_adam",
        grid_spec=pltpu.PrefetchScalarGridSpec(
            num_scalar_prefetch=1, grid=(1,),
            in_specs=[whole(a) for a in list(gathered) + list(parts) + flat],
            out_specs=[whole(s) for s in out_shape], scratch_shapes=[pltpu.VMEM((HALO, 128), f32)]),
        out_shape=out_shape, compiler_params=_cp("arbitrary"),
    )(place, *gathered, *parts, *flat)
    return res[0], {nm: res[1 + 4 * k:5 + 4 * k] for k, nm in enumerate(names)}


def _pad_rows(a, rows):
    return jnp.concatenate([a, jnp.zeros((rows - a.shape[0],) + a.shape[1:], a.dtype)], axis=0)


def kernel(x, p, g_mix, w_in, conv_w, conv_b, ln_g, ln_b, pool_w, pool_scale, w_out, g_ffn, w_gate_up, w_down, g_ple_gate, w_ple_gate, w_ple_up, g_ple_post, g_final, loss_target, m_g_mix, m_w_in, m_conv_w, m_conv_b, m_ln_g, m_ln_b, m_pool_w, m_pool_scale, m_w_out, m_g_ffn, m_w_gate_up, m_w_down, m_g_ple_gate, m_w_ple_gate, m_w_ple_up, m_g_ple_post, m_g_final, v_g_mix, v_w_in, v_conv_w, v_conv_b, v_ln_g, v_ln_b, v_pool_w, v_pool_scale, v_w_out, v_g_ffn, v_w_gate_up, v_w_down, v_g_ple_gate, v_w_ple_gate, v_w_ple_up, v_g_ple_post, v_g_final):
    seq = x.shape[1]
    me = 2 * lax.axis_index("x") + lax.axis_index("y")
    chip = me.astype(jnp.int32).reshape(1)
    core = lax.axis_index("c").astype(jnp.int32).reshape(1)
    place = jnp.concatenate([chip, core])
    xs, ps, ts = x.reshape(seq, D_MODEL), p.reshape(seq, D_PLE), loss_target.reshape(seq, D_MODEL)

    big = [w_in[0], w_gate_up[0], w_out[0], w_down[0], w_ple_gate[0], w_ple_up[0]]
    big_m = [m_w_in[0], m_w_gate_up[0], m_w_out[0], m_w_down[0], m_w_ple_gate[0], m_w_ple_up[0]]
    big_v = [v_w_in[0], v_w_gate_up[0], v_w_out[0], v_w_down[0], v_w_ple_gate[0], v_w_ple_up[0]]
    b_in, b_gu, b_out, b_down, b_pg, b_pu, b_cw = _cast_into_slots(
        chip, big + [_pad_rows(conv_w[0], HALO)], [bf16] * len(big) + [f32])
    xi, yi = lax.axis_index("x"), lax.axis_index("y")
    order = jnp.stack([me, 2 * (1 - xi) + yi, 2 * xi + 1 - yi, 2 * (1 - xi) + 1 - yi]).astype(jnp.int32)

    z, (w_in_g, cw_g, w_out_g) = _mix_in(xs, g_mix, order, [b_in, b_cw, b_out])
    conv_w_f = cw_g.transpose(1, 0, 2).reshape(HALO, C_CONV)
    w_out_f = w_out_g.reshape(D_MODEL, D_MODEL)
    (x1, mix, u1, pooled, h2), (w_gu_g,) = _conv_pool_out(z, xs, conv_w_f, conv_b, ln_g, ln_b, pool_w[0], pool_scale,
                                                          w_out_f, g_ffn, [b_gu])
    (gu, ffn_f), (w_down_g, w_pg_g, w_pu_g) = _ffn_up(h2, w_gu_g, [b_down, b_pg, b_pu])
    w_down_f = w_down_g.reshape(D_FF, D_MODEL)
    x2 = _ffn_down(x1, ffn_f, w_down_f)
    w_pg_f = w_pg_g.reshape(D_MODEL, D_MODEL)
    dx2, d_w_pg, d_w_pu, small_ple = _ple_loss(x2, ps, ts, g_ple_gate, g_ple_post, g_final.reshape(1, D_MODEL),
                                               w_pg_f, w_pu_g)
    d_w_down = _ffn_bwd_dw_down(ffn_f, dx2)
    grads_a = [d_w_down.reshape(N_CHIPS, -1, D_MODEL), d_w_pg.reshape(N_CHIPS, -1, D_MODEL), d_w_pu]
    (dx1, dgu, small_ffn), landed_a = _ffn_bwd_dx(dx2, x1, gu, g_ffn, w_gu_g, w_down_f, grads_a)
    d_w_gu = _ffn_bwd_dw_gu(h2, dgu)
    du1, dpo, d_w_out, d_pool_w, small_mix = _mix_bwd_local(dx1, mix, u1, pooled, w_out_f, ln_g, ln_b, pool_w[0],
                                                             pool_scale)
    parts_b = _pair_reduce("b", [d_w_gu, d_w_out.reshape(N_CHIPS, -1, D_MODEL)])
    small_0 = [small_ple, small_ffn, small_mix, d_pool_w]
    (grad_x, d_w_in, d_conv_w, small_in), landed_b, small_all_0 = _in_bwd(du1, dpo, z, xs, dx1, conv_w_f, g_mix,
                                                                            w_in_g, parts_b, small_0)
    parts_c = _pair_reduce("c", [d_w_in])
    small_1 = [d_conv_w, small_in]
    (h_gu, h_out), (h_down, h_pg, h_pu), landed_c, small_all_1 = _rs_sum_group(
        "ab", place, landed_b, parts_b, landed_a, grads_a, parts_c, small_1)
    h_in = _rs_sum_chips("w_in", place, landed_c[0], parts_c[0])
    big_g = _grad_pair([h_in, h_gu, h_out, h_down, h_pg, h_pu])
    big_upd = _adam(big, big_g, big_m, big_v)

    p3 = lambda w, m, v: (w, m, v)
    row = lambda a: a.reshape(1, D_MODEL)
    params = dict(
        g_ple_gate=p3(g_ple_gate, m_g_ple_gate, v_g_ple_gate), g_ple_post=p3(g_ple_post, m_g_ple_post, v_g_ple_post),
        g_final=p3(row(g_final), row(m_g_final), row(v_g_final)), g_ffn=p3(g_ffn, m_g_ffn, v_g_ffn),
        ln_g=p3(ln_g, m_ln_g, v_ln_g), ln_b=p3(ln_b, m_ln_b, v_ln_b), conv_b=p3(conv_b, m_conv_b, v_conv_b),
        pool_scale=p3(pool_scale, m_pool_scale, v_pool_scale), pool_w=p3(pool_w[0], m_pool_w[0], v_pool_w[0]),
        conv_w=p3(conv_w[0], m_conv_w[0], v_conv_w[0]), g_mix=p3(g_mix, m_g_mix, v_g_mix))
    loss, small = _small_adam(place, list(small_all_0) + list(small_all_1), small_0 + small_1, params)
    back = dict(g_final=lambda a: a.reshape(D_MODEL), pool_w=lambda a: a[None], conv_w=lambda a: a[None])

    names = ["g_mix", "w_in", "conv_w", "conv_b", "ln_g", "ln_b", "pool_w", "pool_scale", "w_out", "g_ffn",
             "w_gate_up", "w_down", "g_ple_gate", "w_ple_gate", "w_ple_up", "g_ple_post", "g_final"]
    big_at = {"w_in": 0, "w_gate_up": 1, "w_out": 2, "w_down": 3, "w_ple_gate": 4, "w_ple_up": 5}
    out = [loss[0, 0], grad_x.reshape(1, seq, D_MODEL)]
    for kind in range(4):
        for nm in names:
            if nm in big_at:
                out.append(big_upd[big_at[nm]][kind][None])
            else:
                out.append(back.get(nm, lambda a: a)(small[nm][kind]))
    return tuple(out)
```

```python
import functools

import jax
import jax.numpy as jnp
from jax import lax
from jax.experimental import pallas as pl
from jax.experimental.pallas import tpu as pltpu

f32, bf16 = jnp.float32, jnp.bfloat16

EPS = 1e-6
D_MODEL = 1024
C_CONV = 512
C_POOL = 512
POOL_WINDOWS = (2, 4, 8, 16)
POOL_GROUP = 128
CONV_K = 31
D_FF = 2816
D_PLE = 256
N_CHIPS = 4
N_DEV = 8
W_IN_COLS = 2 * C_CONV + C_POOL
W_IN_CHUNK = W_IN_COLS // N_CHIPS
FF_CHUNK = 2 * D_FF // N_CHIPS
PLE_CHUNK = D_MODEL // N_CHIPS
HALO = 32
ROW_TILE = 512
SHIFT_PAD = 32
SHIFT_GROUPS = 16
FF_SUB = (0, 512, 1024, FF_CHUNK)
VMEM_LIMIT = 56 * 1024 * 1024

ADAM_LR = 0.001
ADAM_B1 = 0.9
ADAM_B2 = 0.999
ADAM_EPS = 1e-08
ADAM_WD = 0.01
ADAM_STEP = 10

MESH = pl.DeviceIdType.MESH
ANY = pl.BlockSpec(memory_space=pl.ANY)
VMEM = pl.BlockSpec(memory_space=pltpu.VMEM)


def _cp(*sem):
    return pltpu.CompilerParams(dimension_semantics=sem, vmem_limit_bytes=VMEM_LIMIT)


def _dot(a, b):
    return jnp.dot(a, b, preferred_element_type=f32)


def _dot_nt(a, b):
    return lax.dot_general(a, b, (((1,), (1,)), ((), ())), preferred_element_type=f32)


def _dot_tn(a, b):
    return lax.dot_general(a, b, (((0,), (0,)), ((), ())), preferred_element_type=f32)


def _sigmoid(v):
    return jax.nn.sigmoid(v)


def _rms_fwd(v, g):
    r = lax.rsqrt(jnp.mean(v * v, axis=-1, keepdims=True) + EPS)
    vh = v * r
    return vh * g, vh, r


def _rms_bwd(dy, vh, r, g):
    dvh = dy * g
    dv = r * (dvh - vh * jnp.mean(dvh * vh, axis=-1, keepdims=True))
    return dv, jnp.sum(dy * vh, axis=0, keepdims=True)


def _silu_grad(v, s):
    return s * (1.0 + v * (1.0 - s))


def _row(i, n):
    return pl.BlockSpec((n[0], n[1]), lambda *a: (a[i], 0))


def _full(shape):
    nd = len(shape)
    return pl.BlockSpec(shape, lambda *a: (0,) * nd)


def _place():
    x, y, c = lax.axis_index("x"), lax.axis_index("y"), lax.axis_index("c")
    others = [(1 - x, y), (x, 1 - y), (1 - x, 1 - y)]
    return x, y, c, 2 * x + y, others


def _remote(src, dst, send_sem, recv_sem, dev):
    return pltpu.make_async_remote_copy(src_ref=src, dst_ref=dst, send_sem=send_sem, recv_sem=recv_sem,
                                        device_id=dev, device_id_type=MESH)


SHARD_STEPS = 4


def _cast_into_slots(me, ws, dtypes):
    n = len(ws)

    def body(me_ref, *refs):
        for w_ref, o_ref, dtype in zip(refs[:n], refs[n:], dtypes):
            o_ref[0] = w_ref[...].astype(dtype)

    return pl.pallas_call(
        body, name="cast_shards",
        grid_spec=pltpu.PrefetchScalarGridSpec(
            num_scalar_prefetch=1, grid=(SHARD_STEPS,),
            in_specs=[pl.BlockSpec((w.shape[0] // SHARD_STEPS, w.shape[1]), lambda r, me_ref: (r, 0)) for w in ws],
            out_specs=[pl.BlockSpec((1, w.shape[0] // SHARD_STEPS, w.shape[1]), lambda r, me_ref: (me_ref[0], r, 0))
                       for w in ws]),
        out_shape=[jax.ShapeDtypeStruct((N_CHIPS,) + w.shape, dt) for w, dt in zip(ws, dtypes)],
        compiler_params=_cp("parallel"),
    )(me, *ws)


class _Gather:
    def __init__(self, bufs, send_sems, recv_sems):
        self.bufs, self.send_sems, self.recv_sems = bufs, send_sems, recv_sems
        self.x, self.y, self.c, self.me, self.others = _place()
        self.halves = [b.shape[1] // 2 for b in bufs]

    def _piece(self, k, chip, half):
        return self.bufs[k].at[chip, pl.ds(half * self.halves[k], self.halves[k]), :]

    def _ici(self, k, j, chip):
        ox, oy = self.others[j]
        piece = self._piece(k, chip, self.c)
        return _remote(piece, piece, self.send_sems.at[6 * k + j], self.recv_sems.at[6 * k + j], (ox, oy, self.c))

    def _relay(self, k):
        first = self.c == 0
        piece = self._piece(k, jnp.where(first, self.chip(0), self.chip(1)), self.c)
        to = (jnp.where(first, self.others[1][0], self.others[0][0]),
              jnp.where(first, self.others[1][1], self.others[0][1]), self.c)
        return _remote(piece, piece, self.send_sems.at[6 * k + 2], self.recv_sems.at[6 * k + 2], to)

    def _pair(self, k, j, half):
        ox, oy = self.others[j]
        piece = self._piece(k, 2 * ox + oy, half)
        return _remote(piece, piece, self.send_sems.at[6 * k + 3 + j], self.recv_sems.at[6 * k + 3 + j],
                       (self.x, self.y, 1 - self.c))

    def _each(self, ks=None):
        return [(k, j) for k in (range(len(self.bufs)) if ks is None else ks) for j in range(3)]

    def chip(self, j):
        ox, oy = self.others[j]
        return 2 * ox + oy

    def start(self):
        for k in range(len(self.bufs)):
            for j in range(2):
                self._ici(k, j, self.me).start()

    def forward(self, pairs=None):
        for k, j in self._each() if pairs is None else pairs:
            self._ici(k, j, self.chip(j)).wait_recv()
            self._pair(k, j, self.c).start()
            if j < 2:
                pl.when(self.c == j)(self._relay(k).start)

    def landed(self, pairs):
        for k, j in pairs:
            self._pair(k, j, 1 - self.c).wait_recv()

    def finish(self, ks=None):
        self.landed(self._each(ks))
        for k in range(len(self.bufs)):
            for j in range(2):
                self._ici(k, j, self.me).wait_send()
            self._relay(k).wait_send()
            for j in range(3):
                self._pair(k, j, self.c).wait_send()

    @staticmethod
    def scratch(n):
        return [pltpu.SemaphoreType.DMA((6 * n,)), pltpu.SemaphoreType.DMA((6 * n,))]


def _carried(bufs):
    n = len(bufs)
    return dict(in_specs=[ANY] * n, out_specs=[ANY] * n,
                out_shape=[jax.ShapeDtypeStruct(b.shape, b.dtype) for b in bufs], scratch=_Gather.scratch(n))


def _mix_in(x, g_mix, order, carry):
    s = x.shape[0]
    tm = min(2 * ROW_TILE, s)
    n = s // tm
    nc = len(carry)
    cs = _carried(carry)

    def body(order_ref, x_ref, g_ref, *refs):
        z_ref = refs[nc]
        bufs = refs[nc + 1:2 * nc + 1]
        h_ref, w_ref, w_sem = refs[2 * nc + 1:2 * nc + 4]
        gather = _Gather(bufs, *refs[2 * nc + 4:])
        q, i = pl.program_id(0), pl.program_id(1)
        first = i == 0
        pl.when(jnp.logical_and(q == 0, first))(gather.start)
        for j in range(3):

            @pl.when(jnp.logical_and(q == j + 1, first))
            def _():
                gather.forward([(0, j)])
                gather.landed([(0, j)])
                if j == 1:
                    gather.forward([(k, jj) for k in range(1, nc) for jj in range(2)])

        @pl.when(first)
        def _():
            load = pltpu.make_async_copy(bufs[0].at[order_ref[q]], w_ref, w_sem)
            load.start()
            load.wait()

        @pl.when(q == 0)
        def _():
            h, _, _ = _rms_fwd(x_ref[...], g_ref[...])
            h_ref[i] = h.astype(bf16)

        z_ref[...] = _dot(h_ref[i], w_ref[...])

        @pl.when(jnp.logical_and(q == N_CHIPS - 1, i == n - 1))
        def _():
            gather.forward([(k, 2) for k in range(1, nc)])
            gather.finish(range(1, nc))

    res = pl.pallas_call(
        body, name="mix_in",
        grid_spec=pltpu.PrefetchScalarGridSpec(
            num_scalar_prefetch=1, grid=(N_CHIPS, n),
            in_specs=[pl.BlockSpec((tm, D_MODEL), lambda q, i, order_ref: (jnp.where(q == 0, i, 0), 0)),
                      pl.BlockSpec((1, D_MODEL), lambda q, i, order_ref: (0, 0))] + cs["in_specs"],
            out_specs=[pl.BlockSpec((tm, W_IN_CHUNK), lambda q, i, order_ref: (i, order_ref[q]))] + cs["out_specs"],
            scratch_shapes=[pltpu.VMEM((n, tm, D_MODEL), bf16), pltpu.VMEM((D_MODEL, W_IN_CHUNK), bf16),
                            pltpu.SemaphoreType.DMA(())] + cs["scratch"]),
        out_shape=[jax.ShapeDtypeStruct((s, W_IN_COLS), f32)] + cs["out_shape"],
        input_output_aliases={3 + k: 1 + k for k in range(nc)},
        compiler_params=_cp("arbitrary", "arbitrary"),
    )(order, x, g_mix, *carry)
    return res[0], res[1:]


def _pool_counts(tm, w, first_row):
    t1 = (lax.broadcasted_iota(jnp.int32, (tm, 1), 0) + first_row + 1).astype(f32)
    return jnp.minimum(t1, float(w))


def _conv_pool_out(z, x, conv_w, conv_b, ln_g, ln_b, pool_w, pool_scale, w_out, g_ffn, carry):
    s = x.shape[0]
    tm = min(ROW_TILE, s)
    n = s // tm
    hb = tm // HALO
    nv = tm // 8
    assert nv >= SHIFT_PAD and nv % SHIFT_GROUPS == 0
    nc = len(carry)
    cs = _carried(carry)

    def body(z_ref, zp_ref, x_ref, cw_ref, cb_ref, lg_ref, lb_ref, pw_ref, ps_ref, wo_ref, gf_ref, *refs):
        x1_ref, mix_ref, u1_ref, pooled_ref, h2_ref = refs[nc:nc + 5]
        eu, ev, ss, u0_ref = refs[2 * nc + 5:2 * nc + 9]
        gather = _Gather(refs[nc + 5:2 * nc + 5], *refs[2 * nc + 9:])
        i = pl.program_id(0)
        pl.when(i == 0)(gather.start)
        for k in range(nc):
            pl.when(i == min(3 * n // 4 + k, n - 1))(functools.partial(gather.forward, [(k, 0), (k, 1)]))
        keep = (i > 0).astype(f32)
        zp = zp_ref[...] * keep
        u0_prev = zp[:, :C_CONV] * _sigmoid(zp[:, C_CONV:2 * C_CONV])
        u0_ref[...] = z_ref[:, :C_CONV] * _sigmoid(z_ref[:, C_CONV:2 * C_CONV])
        for c, w_pool in enumerate(POOL_WINDOWS):
            lanes = slice(c * 128, (c + 1) * 128)
            v_lanes = slice(2 * C_CONV + c * 128, 2 * C_CONV + (c + 1) * 128)
            eu[c, pl.ds(0, SHIFT_PAD, stride=8), :] = u0_prev[:, lanes]
            ev[c, pl.ds(0, SHIFT_PAD, stride=8), :] = zp[:, v_lanes]
            for j in range(8):
                rows = slice(j * nv, (j + 1) * nv)
                eu[c, pl.ds(SHIFT_PAD * 8 + j, nv, stride=8), :] = u0_ref[rows, lanes]
                ev[c, pl.ds(SHIFT_PAD * 8 + j, nv, stride=8), :] = z_ref[rows, v_lanes]
                if j >= 1:
                    edge = slice(j * nv - SHIFT_PAD, j * nv)
                    eu[c, pl.ds(j, SHIFT_PAD, stride=8), :] = u0_ref[edge, lanes]
                    ev[c, pl.ds(j, SHIFT_PAD, stride=8), :] = z_ref[edge, v_lanes]
            for v0 in range(0, nv, SHIFT_GROUPS):
                span = SHIFT_GROUPS * 8
                acc = jnp.zeros((span, 128), f32) + cb_ref[:, lanes]
                for k in range(CONV_K):
                    acc = acc + cw_ref[pl.ds(k, 1), lanes] * eu[c, pl.ds((SHIFT_PAD - (CONV_K - 1) + v0 + k) * 8, span), :]
                ss[0, v0 * 8:v0 * 8 + span, :] = acc
                acc = ev[c, pl.ds((SHIFT_PAD + v0) * 8, span), :]
                for d in range(1, w_pool):
                    acc = acc + ev[c, pl.ds((SHIFT_PAD + v0 - d) * 8, span), :]
                ss[1, v0 * 8:v0 * 8 + span, :] = acc
            for j in range(8):
                rows = slice(j * nv, (j + 1) * nv)
                u1_ref[rows, lanes] = ss[0, pl.ds(j, nv, stride=8), :]
                mean = ss[1, pl.ds(j, nv, stride=8), :] / _pool_counts(nv, w_pool, i * tm + j * nv)
                pooled_ref[rows, lanes] = (mean - z_ref[rows, v_lanes]).astype(bf16)
        u1 = u1_ref[...]
        mu = jnp.mean(u1, axis=-1, keepdims=True)
        uc = u1 - mu
        rstd = lax.rsqrt(jnp.mean(uc * uc, axis=-1, keepdims=True) + EPS)
        u2 = uc * rstd * lg_ref[...] + lb_ref[...]
        mix_ref[:, :C_CONV] = (u2 * _sigmoid(u2)).astype(bf16)
        for g in range(len(POOL_WINDOWS)):
            cols = slice(g * POOL_GROUP, (g + 1) * POOL_GROUP)
            mixed = _dot(pooled_ref[:, cols], pw_ref[g].astype(bf16))
            mix_ref[:, C_CONV + g * POOL_GROUP:C_CONV + (g + 1) * POOL_GROUP] = (mixed * ps_ref[:, cols]).astype(bf16)
        x1 = x_ref[...] + _dot(mix_ref[...], wo_ref[...])
        x1_ref[...] = x1
        h2_ref[...] = _rms_fwd(x1, gf_ref[...])[0].astype(bf16)
        @pl.when(i == n - 1)
        def _():
            gather.forward([(k, 2) for k in range(nc)])
            gather.finish()

    res = pl.pallas_call(
        body, name="conv_pool_out", grid=(n,),
        in_specs=[_row(0, (tm, W_IN_COLS)),
                  pl.BlockSpec((HALO, W_IN_COLS), lambda i: (jnp.maximum(i * hb - 1, 0), 0)),
                  _row(0, (tm, D_MODEL)), _full((HALO, C_CONV)), _full((1, C_CONV)), _full((1, C_CONV)),
                  _full((1, C_CONV)), _full((4, POOL_GROUP, POOL_GROUP)), _full((1, C_POOL)),
                  _full((D_MODEL, D_MODEL)), _full((1, D_MODEL))] + cs["in_specs"],
        out_specs=[_row(0, (tm, D_MODEL)), _row(0, (tm, D_MODEL)), _row(0, (tm, C_CONV)), _row(0, (tm, C_POOL)),
                   _row(0, (tm, D_MODEL))] + cs["out_specs"],
        out_shape=[jax.ShapeDtypeStruct((s, D_MODEL), f32), jax.ShapeDtypeStruct((s, D_MODEL), bf16),
                   jax.ShapeDtypeStruct((s, C_CONV), f32), jax.ShapeDtypeStruct((s, C_POOL), bf16),
                   jax.ShapeDtypeStruct((s, D_MODEL), bf16)] + cs["out_shape"],
        input_output_aliases={11 + k: 5 + k for k in range(nc)},
        scratch_shapes=[pltpu.VMEM((4, (SHIFT_PAD + nv) * 8, 128), f32), pltpu.VMEM((4, (SHIFT_PAD + nv) * 8, 128), f32),
                        pltpu.VMEM((2, tm, 128), f32), pltpu.VMEM((tm, C_CONV), f32)] + cs["scratch"],
        compiler_params=_cp("arbitrary"),
    )(z, z, x, conv_w, conv_b, ln_g, ln_b, pool_w, pool_scale, w_out, g_ffn, *carry)
    return res[:5], res[5:]


def _ffn_up(h2, w_gu_g, carry):
    s = h2.shape[0]
    tm = min(2 * ROW_TILE, s)
    n = s // tm
    nc = len(carry)
    cs = _carried(carry)

    def body(h2_ref, wg_ref, wu_ref, *refs):
        gu_ref, f_ref = refs[nc:nc + 2]
        gather = _Gather(refs[nc + 2:2 * nc + 2], *refs[2 * nc + 2:])
        c, i = pl.program_id(0), pl.program_id(1)
        pl.when(jnp.logical_and(i == 0, c == 0))(gather.start)
        direct = [(k, j) for k in range(nc) for j in range(2)]
        pl.when(jnp.logical_and(i == n - 1, c == 0))(functools.partial(gather.forward, direct))
        pl.when(jnp.logical_and(i == n // 2, c == 1))(functools.partial(gather.forward, [(k, 2) for k in range(nc)]))
        h = h2_ref[...]
        for lo, hi in zip(FF_SUB[:-1], FF_SUB[1:]):
            gate = _dot(h, wg_ref[0, :, lo:hi])
            up = _dot(h, wu_ref[0, :, lo:hi])
            gu_ref[0, :, lo:hi] = gate.astype(bf16)
            gu_ref[1, :, lo:hi] = up.astype(bf16)
            f_ref[:, lo:hi] = (gate * _sigmoid(gate) * up).astype(bf16)
        pl.when(jnp.logical_and(i == n - 1, c == 1))(gather.finish)

    res = pl.pallas_call(
        body, name="ffn_up", grid=(2, n),
        in_specs=[_row(1, (tm, D_MODEL)),
                  pl.BlockSpec((1, D_MODEL, FF_CHUNK), lambda c, i: (c, 0, 0)),
                  pl.BlockSpec((1, D_MODEL, FF_CHUNK), lambda c, i: (2 + c, 0, 0))] + cs["in_specs"],
        out_specs=[pl.BlockSpec((2, tm, FF_CHUNK), lambda c, i: (0, i, c)),
                   pl.BlockSpec((tm, FF_CHUNK), lambda c, i: (i, c))] + cs["out_specs"],
        out_shape=[jax.ShapeDtypeStruct((2, s, D_FF), bf16), jax.ShapeDtypeStruct((s, D_FF), bf16)] + cs["out_shape"],
        input_output_aliases={3 + k: 2 + k for k in range(nc)},
        scratch_shapes=cs["scratch"],
        compiler_params=_cp("arbitrary", "arbitrary"),
    )(h2, w_gu_g, w_gu_g, *carry)
    return res[:2], res[2:]


def _ffn_down(x1, f, w_down):
    s = x1.shape[0]
    tm = min(2 * ROW_TILE, s)

    def body(x1_ref, f_ref, wd_ref, x2_ref):
        x2_ref[...] = x1_ref[...] + _dot(f_ref[...], wd_ref[...])

    return pl.pallas_call(
        body, name="ffn_down", grid=(s // tm,),
        in_specs=[_row(0, (tm, D_MODEL)), _row(0, (tm, D_FF)), _full((D_FF, D_MODEL))],
        out_specs=_row(0, (tm, D_MODEL)), out_shape=jax.ShapeDtypeStruct((s, D_MODEL), f32),
        compiler_params=_cp("parallel"),
    )(x1, f, w_down)


def _ple_loss(x2, p, target, g_pg, g_post, g_final, w_pg, w_pu_g):
    s = x2.shape[0]
    tm = min(ROW_TILE, s)
    n = s // tm

    def body(x2_ref, p_ref, t_ref, gpg_ref, gpo_ref, gf_ref, wpg_ref, wpu_ref,
             dx2_ref, dwpg_ref, dwpu_ref, small_ref, apg_ref, apu_ref):
        i = pl.program_id(0)

        @pl.when(i == 0)
        def _():
            apg_ref[...] = jnp.zeros_like(apg_ref)
            apu_ref[...] = jnp.zeros_like(apu_ref)
            small_ref[...] = jnp.zeros_like(small_ref)

        x2 = x2_ref[...]
        h3, x2h, r2 = _rms_fwd(x2, gpg_ref[...])
        h3b = h3.astype(bf16)
        gate = _sigmoid(_dot(h3b, wpg_ref[...]))
        pb = p_ref[...].astype(bf16)
        pe = jnp.concatenate([_dot(pb, wpu_ref[j]) for j in range(N_CHIPS)], axis=-1)
        e, peh, rp = _rms_fwd(pe, gpo_ref[...])
        x3 = x2 + gate * e
        y, x3h, r3 = _rms_fwd(x3, gf_ref[...])
        d = y - t_ref[...]
        loss = 0.5 * jnp.sum(jnp.sum(d * d, axis=-1, keepdims=True) * (1.0 / D_MODEL), axis=0, keepdims=True)
        dx3, dgf = _rms_bwd(d * (1.0 / D_MODEL), x3h, r3, gf_ref[...])
        dpe, dgpo = _rms_bwd(dx3 * gate, peh, rp, gpo_ref[...])
        dgl = (dx3 * e * gate * (1.0 - gate)).astype(bf16)
        apg_ref[...] += _dot_tn(h3b, dgl)
        apu_ref[...] += _dot_tn(pb, dpe.astype(bf16))
        dh3 = _dot_nt(dgl, wpg_ref[...])
        dx2b, dgpg = _rms_bwd(dh3, x2h, r2, gpg_ref[...])
        dx2_ref[...] = dx3 + dx2b
        small_ref[0:1, :] += dgpg
        small_ref[1:2, :] += dgpo
        small_ref[2:3, :] += dgf
        small_ref[3:4, :] += jnp.broadcast_to(loss, (1, D_MODEL))

        @pl.when(i == n - 1)
        def _():
            dwpg_ref[...] = apg_ref[...].astype(bf16)
            for j in range(N_CHIPS):
                dwpu_ref[j] = apu_ref[:, j * PLE_CHUNK:(j + 1) * PLE_CHUNK].astype(bf16)

    return pl.pallas_call(
        body, name="ple_loss", grid=(n,),
        in_specs=[_row(0, (tm, D_MODEL)), _row(0, (tm, D_PLE)), _row(0, (tm, D_MODEL)),
                  _full((1, D_MODEL)), _full((1, D_MODEL)), _full((1, D_MODEL)),
                  _full((D_MODEL, D_MODEL)), _full((N_CHIPS, D_PLE, PLE_CHUNK))],
        out_specs=[_row(0, (tm, D_MODEL)), _full((D_MODEL, D_MODEL)), _full((N_CHIPS, D_PLE, PLE_CHUNK)),
                   _full((8, D_MODEL))],
        out_shape=[jax.ShapeDtypeStruct((s, D_MODEL), f32), jax.ShapeDtypeStruct((D_MODEL, D_MODEL), bf16),
                   jax.ShapeDtypeStruct((N_CHIPS, D_PLE, PLE_CHUNK), bf16), jax.ShapeDtypeStruct((8, D_MODEL), f32)],
        scratch_shapes=[pltpu.VMEM((D_MODEL, D_MODEL), f32), pltpu.VMEM((D_PLE, D_MODEL), f32)],
        compiler_params=_cp("arbitrary"),
    )(x2, p, target, g_pg, g_post, g_final, w_pg, w_pu_g)


def _crossed(parts):
    n = len(parts)
    return dict(in_specs=[ANY] * n, out_specs=[ANY] * n,
                out_shape=[jax.ShapeDtypeStruct(a.shape, a.dtype) for a in parts], scratch=_Cross.scratch(n))


def _ffn_bwd_dx(dx2, x1, gu, g_ffn, w_gu_g, w_down, grads):
    s = x1.shape[0]
    tm = min(ROW_TILE, s)
    n = s // tm
    nc = len(grads)
    cs = dict(in_specs=[ANY] * nc, out_specs=[ANY] * nc, scratch=_CrossAll.scratch(nc),
              out_shape=[jax.ShapeDtypeStruct((N_DEV, g.shape[1] // 2, g.shape[2]), g.dtype) for g in grads])

    def body(dx2_ref, x1_ref, gu_ref, g_ref, wg_ref, wu_ref, wd_ref, *refs):
        dx1_ref, dgu_ref, small_ref = refs[nc:nc + 3]
        acc_ref = refs[2 * nc + 3]
        cross = _CrossAll(refs[:nc], refs[nc + 3:2 * nc + 3], *refs[2 * nc + 4:])
        i, c = pl.program_id(0), pl.program_id(1)
        pl.when(jnp.logical_and(i == 0, c == 0))(cross.start)

        @pl.when(jnp.logical_and(i == 0, c == 0))
        def _():
            small_ref[...] = jnp.zeros_like(small_ref)

        @pl.when(c == 0)
        def _():
            acc_ref[...] = jnp.zeros_like(acc_ref)

        dyb = dx2_ref[...].astype(bf16)
        for lo, hi in zip(FF_SUB[:-1], FF_SUB[1:]):
            df = _dot_nt(dyb, wd_ref[lo:hi, :])
            gate = gu_ref[0, :, lo:hi].astype(f32)
            up = gu_ref[1, :, lo:hi].astype(f32)
            sg = _sigmoid(gate)
            dgate = (df * up * _silu_grad(gate, sg)).astype(bf16)
            dup = (df * gate * sg).astype(bf16)
            dgu_ref[0, :, lo:hi] = dgate
            dgu_ref[1, :, lo:hi] = dup
            acc_ref[...] += _dot_nt(dgate, wg_ref[0, :, lo:hi]) + _dot_nt(dup, wu_ref[0, :, lo:hi])

        @pl.when(c == 1)
        def _():
            _, x1h, r1 = _rms_fwd(x1_ref[...], g_ref[...])
            dx1b, dg = _rms_bwd(acc_ref[...], x1h, r1, g_ref[...])
            dx1_ref[...] = dx2_ref[...] + dx1b
            small_ref[0:1, :] += dg

        pl.when(jnp.logical_and(i == n - 1, c == 1))(cross.finish)

    res = pl.pallas_call(
        body, name="ffn_bwd_dx", grid=(n, 2),
        in_specs=[_row(0, (tm, D_MODEL)), _row(0, (tm, D_MODEL)),
                  pl.BlockSpec((2, tm, FF_CHUNK), lambda i, c: (0, i, c)), _full((1, D_MODEL)),
                  pl.BlockSpec((1, D_MODEL, FF_CHUNK), lambda i, c: (c, 0, 0)),
                  pl.BlockSpec((1, D_MODEL, FF_CHUNK), lambda i, c: (2 + c, 0, 0)),
                  pl.BlockSpec((FF_CHUNK, D_MODEL), lambda i, c: (c, 0))] + cs["in_specs"],
        out_specs=[_row(0, (tm, D_MODEL)), pl.BlockSpec((2, tm, FF_CHUNK), lambda i, c: (0, i, c)),
                   _full((8, D_MODEL))] + cs["out_specs"],
        out_shape=[jax.ShapeDtypeStruct((s, D_MODEL), f32), jax.ShapeDtypeStruct((2, s, D_FF), bf16),
                   jax.ShapeDtypeStruct((8, D_MODEL), f32)] + cs["out_shape"],
        scratch_shapes=[pltpu.VMEM((tm, D_MODEL), f32)] + cs["scratch"],
        compiler_params=_cp("arbitrary", "arbitrary"),
    )(dx2, x1, gu, g_ffn, w_gu_g, w_gu_g, w_down, *grads)
    return res[:3], res[3:]


def _ffn_bwd_dw_gu(h2, dgu):
    s = h2.shape[0]
    ts = min(2 * ROW_TILE, s)
    n = s // ts

    def body(h_ref, d_ref, o_ref, acc_ref):
        t = pl.program_id(1)

        @pl.when(t == 0)
        def _():
            acc_ref[...] = jnp.zeros_like(acc_ref)

        acc_ref[...] += _dot_tn(h_ref[...], d_ref[0])

        @pl.when(t == n - 1)
        def _():
            o_ref[0] = acc_ref[...].astype(bf16)

    return pl.pallas_call(
        body, name="ffn_bwd_dw_gu", grid=(N_CHIPS, n),
        in_specs=[pl.BlockSpec((ts, D_MODEL), lambda j, t: (t, 0)),
                  pl.BlockSpec((1, ts, FF_CHUNK), lambda j, t: (j // 2, t, j % 2))],
        out_specs=pl.BlockSpec((1, D_MODEL, FF_CHUNK), lambda j, t: (j, 0, 0)),
        out_shape=jax.ShapeDtypeStruct((N_CHIPS, D_MODEL, FF_CHUNK), bf16),
        scratch_shapes=[pltpu.VMEM((D_MODEL, FF_CHUNK), f32)],
        compiler_params=_cp("parallel", "arbitrary"),
    )(h2, dgu)


def _ffn_bwd_dw_down(f, dx2):
    s = dx2.shape[0]
    ts = min(2 * ROW_TILE, s)
    n = s // ts

    def body(f_ref, d_ref, o_ref, acc_ref):
        t = pl.program_id(1)

        @pl.when(t == 0)
        def _():
            acc_ref[...] = jnp.zeros_like(acc_ref)

        acc_ref[...] += _dot_tn(f_ref[...], d_ref[...].astype(bf16))

        @pl.when(t == n - 1)
        def _():
            o_ref[...] = acc_ref[...].astype(bf16)

    return pl.pallas_call(
        body, name="ffn_bwd_dw_down", grid=(2, n),
        in_specs=[pl.BlockSpec((ts, FF_CHUNK), lambda c, t: (t, c)),
                  pl.BlockSpec((ts, D_MODEL), lambda c, t: (t, 0))],
        out_specs=pl.BlockSpec((FF_CHUNK, D_MODEL), lambda c, t: (c, 0)),
        out_shape=jax.ShapeDtypeStruct((D_FF, D_MODEL), bf16),
        scratch_shapes=[pltpu.VMEM((FF_CHUNK, D_MODEL), f32)],
        compiler_params=_cp("parallel", "arbitrary"),
    )(f, dx2)


def _mix_bwd_local(dx1, mix, u1, pooled, w_out, ln_g, ln_b, pool_w, pool_scale):
    s = dx1.shape[0]
    tm = min(ROW_TILE, s)
    n = s // tm

    def body(dx1_ref, mix_ref, u1_ref, po_ref, wo_ref, lg_ref, lb_ref, pw_ref, ps_ref,
             du1_ref, dpo_ref, dwo_ref, dpw_ref, small_ref, awo_ref):
        i = pl.program_id(0)

        @pl.when(i == 0)
        def _():
            awo_ref[...] = jnp.zeros_like(awo_ref)
            dpw_ref[...] = jnp.zeros_like(dpw_ref)
            small_ref[...] = jnp.zeros_like(small_ref)

        dyb = dx1_ref[...].astype(bf16)
        dmix = _dot_nt(dyb, wo_ref[...])
        awo_ref[...] += _dot_tn(mix_ref[...], dyb)
        u1 = u1_ref[...]
        mu = jnp.mean(u1, axis=-1, keepdims=True)
        uc = u1 - mu
        rstd = lax.rsqrt(jnp.mean(uc * uc, axis=-1, keepdims=True) + EPS)
        uh = uc * rstd
        u2 = uh * lg_ref[...] + lb_ref[...]
        du2 = dmix[:, :C_CONV] * _silu_grad(u2, _sigmoid(u2))
        duh = du2 * lg_ref[...]
        du1 = rstd * (duh - jnp.mean(duh, axis=-1, keepdims=True) - uh * jnp.mean(duh * uh, axis=-1, keepdims=True))
        du1_ref[...] = du1
        small_ref[0:1, :] += jnp.sum(du2 * uh, axis=0, keepdims=True)
        small_ref[1:2, :] += jnp.sum(du2, axis=0, keepdims=True)
        small_ref[2:3, :] += jnp.sum(du1, axis=0, keepdims=True)
        for g in range(len(POOL_WINDOWS)):
            cols = slice(g * POOL_GROUP, (g + 1) * POOL_GROUP)
            dq = dmix[:, C_CONV + g * POOL_GROUP:C_CONV + (g + 1) * POOL_GROUP]
            pwb = pw_ref[g].astype(bf16)
            pg = po_ref[:, cols]
            mixed = _dot(pg, pwb)
            small_ref[3:4, cols] += jnp.sum(dq * mixed, axis=0, keepdims=True)
            dmixed = (dq * ps_ref[:, cols]).astype(bf16)
            dpw_ref[g] += _dot_tn(pg, dmixed)
            dpo_ref[:, cols] = _dot_nt(dmixed, pwb)

        @pl.when(i == n - 1)
        def _():
            dwo_ref[...] = awo_ref[...].astype(bf16)

    return pl.pallas_call(
        body, name="mix_bwd_local", grid=(n,),
        in_specs=[_row(0, (tm, D_MODEL)), _row(0, (tm, D_MODEL)), _row(0, (tm, C_CONV)), _row(0, (tm, C_POOL)),
                  _full((D_MODEL, D_MODEL)), _full((1, C_CONV)), _full((1, C_CONV)),
                  _full((4, POOL_GROUP, POOL_GROUP)), _full((1, C_POOL))],
        out_specs=[_row(0, (tm, C_CONV)), _row(0, (tm, C_POOL)), _full((D_MODEL, D_MODEL)),
                   _full((4, POOL_GROUP, POOL_GROUP)), _full((8, C_CONV))],
        out_shape=[jax.ShapeDtypeStruct((s, C_CONV), f32), jax.ShapeDtypeStruct((s, C_POOL), f32),
                   jax.ShapeDtypeStruct((D_MODEL, D_MODEL), bf16),
                   jax.ShapeDtypeStruct((4, POOL_GROUP, POOL_GROUP), f32), jax.ShapeDtypeStruct((8, C_CONV), f32)],
        scratch_shapes=[pltpu.VMEM((D_MODEL, D_MODEL), f32)],
        compiler_params=_cp("arbitrary"),
    )(dx1, mix, u1, pooled, w_out, ln_g, ln_b, pool_w, pool_scale)


def _in_bwd(du1, dpo, z, x, dx1, conv_w, g_mix, w_in_g, parts, small_parts):
    s = x.shape[0]
    tm = min(ROW_TILE, s)
    n = s // tm
    hb = tm // HALO
    last = s // HALO - 1
    nv = tm // 8
    assert nv >= SHIFT_PAD and nv % SHIFT_GROUPS == 0
    nc, ns = len(parts), len(small_parts)
    cs = _crossed(parts)

    def body(du_ref, dun_ref, dp_ref, dpn_ref, z_ref, zp_ref, x_ref, dx1_ref, cw_ref, g_ref, w_ref, *refs):
        outs = refs[nc + ns:]
        gx_ref, dw_ref, dcw_ref, small_ref = outs[:4]
        eu, ed, ep, ss, u0_ref, dz_ref, acc_ref, dcw_acc = outs[4 + nc + ns:12 + nc + ns]
        sems = outs[12 + nc + ns:]
        cross = _Cross(refs[:nc], outs[4:4 + nc], sems[0], sems[1])
        gather = _SmallGather(refs[nc:nc + ns], outs[4 + nc:4 + nc + ns], sems[2], sems[3])
        i = pl.program_id(0)
        pl.when(i == 0)(cross.start)
        pl.when(i == 0)(gather.start)

        @pl.when(i == 0)
        def _():
            acc_ref[...] = jnp.zeros_like(acc_ref)
            dcw_acc[...] = jnp.zeros_like(dcw_acc)
            small_ref[...] = jnp.zeros_like(small_ref)

        keep_prev = (i > 0).astype(f32)
        keep_next = (i < n - 1).astype(f32)
        zp = zp_ref[...] * keep_prev
        u0_prev = zp[:, :C_CONV] * _sigmoid(zp[:, C_CONV:2 * C_CONV])
        u0_ref[...] = z_ref[:, :C_CONV] * _sigmoid(z_ref[:, C_CONV:2 * C_CONV])
        du_next = dun_ref[...] * keep_next
        for c, w_pool in enumerate(POOL_WINDOWS):
            lanes = slice(c * 128, (c + 1) * 128)
            eu[c, pl.ds(0, SHIFT_PAD, stride=8), :] = u0_prev[:, lanes]
            ed[c, pl.ds(nv * 8 + 7, SHIFT_PAD, stride=8), :] = du_next[:, lanes]
            ep[c, pl.ds(nv * 8 + 7, SHIFT_PAD, stride=8), :] = (
                dpn_ref[:, lanes] * keep_next / _pool_counts(HALO, w_pool, (i + 1) * tm))
            for j in range(8):
                rows = slice(j * nv, (j + 1) * nv)
                eu[c, pl.ds(SHIFT_PAD * 8 + j, nv, stride=8), :] = u0_ref[rows, lanes]
                ed[c, pl.ds(j, nv, stride=8), :] = du_ref[rows, lanes]
                ep[c, pl.ds(j, nv, stride=8), :] = dp_ref[rows, lanes] / _pool_counts(nv, w_pool, i * tm + j * nv)
                if j >= 1:
                    eu[c, pl.ds(j, SHIFT_PAD, stride=8), :] = u0_ref[j * nv - SHIFT_PAD:j * nv, lanes]
                if j <= 6:
                    edge = slice((j + 1) * nv, (j + 1) * nv + SHIFT_PAD)
                    ed[c, pl.ds(nv * 8 + j, SHIFT_PAD, stride=8), :] = du_ref[edge, lanes]
                    ep[c, pl.ds(nv * 8 + j, SHIFT_PAD, stride=8), :] = (
                        dp_ref[edge, lanes] / _pool_counts(SHIFT_PAD, w_pool, i * tm + (j + 1) * nv))
        for c, w_pool in enumerate(POOL_WINDOWS):
            lanes = slice(c * 128, (c + 1) * 128)
            b_lanes = slice(C_CONV + c * 128, C_CONV + (c + 1) * 128)
            v_lanes = slice(2 * C_CONV + c * 128, 2 * C_CONV + (c + 1) * 128)
            for v0 in range(0, nv, SHIFT_GROUPS):
                span = SHIFT_GROUPS * 8
                acc = jnp.zeros((span, 128), f32)
                for k in range(CONV_K):
                    acc = acc + cw_ref[pl.ds(k, 1), lanes] * ed[c, pl.ds((v0 + CONV_K - 1 - k) * 8, span), :]
                ss[0, v0 * 8:v0 * 8 + span, :] = acc
                acc = ep[c, pl.ds(v0 * 8, span), :]
                for d in range(1, w_pool):
                    acc = acc + ep[c, pl.ds((v0 + d) * 8, span), :]
                ss[1, v0 * 8:v0 * 8 + span, :] = acc
                d1 = ed[c, pl.ds(v0 * 8, span), :]
                for k in range(CONV_K):
                    prod = d1 * eu[c, pl.ds((SHIFT_PAD - (CONV_K - 1) + v0 + k) * 8, span), :]
                    fold = prod[0:8]
                    for r in range(8, span, 8):
                        fold = fold + prod[r:r + 8]
                    dcw_acc[k, :, lanes] += fold
            for j in range(8):
                rows = slice(j * nv, (j + 1) * nv)
                du0 = ss[0, pl.ds(j, nv, stride=8), :]
                av, sv = z_ref[rows, lanes], _sigmoid(z_ref[rows, b_lanes])
                dz_ref[rows, lanes] = (du0 * sv).astype(bf16)
                dz_ref[rows, b_lanes] = (du0 * av * sv * (1.0 - sv)).astype(bf16)
                dz_ref[rows, v_lanes] = (ss[1, pl.ds(j, nv, stride=8), :] - dp_ref[rows, lanes]).astype(bf16)
        h, xh, r = _rms_fwd(x_ref[...], g_ref[...])
        dz = dz_ref[...]
        acc_ref[...] += _dot_tn(h.astype(bf16), dz)
        dh = _dot_nt(dz[:, 0:W_IN_CHUNK], w_ref[0])
        for j in range(1, N_CHIPS):
            dh = dh + _dot_nt(dz[:, j * W_IN_CHUNK:(j + 1) * W_IN_CHUNK], w_ref[j])
        dxb, dg = _rms_bwd(dh, xh, r, g_ref[...])
        gx_ref[...] = dx1_ref[...] + dxb
        small_ref[0:1, :] += dg

        @pl.when(i == n - 1)
        def _():
            for j in range(N_CHIPS):
                dw_ref[j] = acc_ref[:, j * W_IN_CHUNK:(j + 1) * W_IN_CHUNK].astype(bf16)
            dcw_ref[...] = jnp.sum(dcw_acc[...], axis=1)

        pl.when(i == n - 1)(gather.finish)
        pl.when(i == n - 1)(cross.finish)

    nxt = lambda i: (jnp.minimum((i + 1) * hb, last), 0)
    res = pl.pallas_call(
        body, name="in_bwd", grid=(n,),
        in_specs=[_row(0, (tm, C_CONV)), pl.BlockSpec((HALO, C_CONV), nxt),
                  _row(0, (tm, C_POOL)), pl.BlockSpec((HALO, C_POOL), nxt),
                  _row(0, (tm, W_IN_COLS)),
                  pl.BlockSpec((HALO, W_IN_COLS), lambda i: (jnp.maximum(i * hb - 1, 0), 0)),
                  _row(0, (tm, D_MODEL)), _row(0, (tm, D_MODEL)), _full((HALO, C_CONV)), _full((1, D_MODEL)),
                  _full((N_CHIPS, D_MODEL, W_IN_CHUNK))] + cs["in_specs"] + [ANY] * ns,
        out_specs=[_row(0, (tm, D_MODEL)), _full((N_CHIPS, D_MODEL, W_IN_CHUNK)), _full((HALO, C_CONV)),
                   _full((8, D_MODEL))] + cs["out_specs"] + [ANY] * ns,
        out_shape=[jax.ShapeDtypeStruct((s, D_MODEL), f32), jax.ShapeDtypeStruct((N_CHIPS, D_MODEL, W_IN_CHUNK), bf16),
                   jax.ShapeDtypeStruct((HALO, C_CONV), f32), jax.ShapeDtypeStruct((8, D_MODEL), f32)] + cs["out_shape"]
        + _gathered_shapes(small_parts),
        scratch_shapes=[pltpu.VMEM((4, (SHIFT_PAD + nv) * 8, 128), f32), pltpu.VMEM((4, (nv + SHIFT_PAD) * 8, 128), f32),
                        pltpu.VMEM((4, (nv + SHIFT_PAD) * 8, 128), f32), pltpu.VMEM((2, tm, 128), f32),
                        pltpu.VMEM((tm, C_CONV), f32), pltpu.VMEM((tm, W_IN_COLS), bf16),
                        pltpu.VMEM((D_MODEL, W_IN_COLS), f32), pltpu.VMEM((HALO, 8, C_CONV), f32)] + cs["scratch"]
        + _SmallGather.scratch(ns),
        compiler_params=_cp("arbitrary"),
    )(du1, du1, dpo, dpo, z, z, x, dx1, conv_w, g_mix, w_in_g, *parts, *small_parts)
    return res[:4], res[4:4 + nc], res[4 + nc:]


def _sibling_handshake(x, y, c):
    barrier = pltpu.get_barrier_semaphore()
    pl.semaphore_signal(barrier, inc=1, device_id=(x, y, 1 - c), device_id_type=MESH)
    pl.semaphore_wait(barrier, 1)


def _pair_reduce(name, collective_id, grads):
    nk = len(grads)
    halves = [g.shape[1] // 2 for g in grads]

    def body(*refs):
        ins, outs, got, own = refs[:nk], refs[nk:2 * nk], refs[2 * nk:3 * nk], refs[3 * nk:4 * nk]
        send_sems, recv_sems, load_sems = refs[4 * nk:]
        x, y, c, _, _ = _place()
        _sibling_handshake(x, y, c)

        def half(k, core):
            return ins[k].at[:, pl.ds(core * halves[k], halves[k]), :]

        cps = [_remote(half(k, 1 - c), got[k], send_sems.at[k], recv_sems.at[k], (x, y, 1 - c)) for k in range(nk)]
        loads = [pltpu.make_async_copy(half(k, c), own[k], load_sems.at[k]) for k in range(nk)]
        for cp in cps + loads:
            cp.start()
        for k in range(nk):
            loads[k].wait()
            cps[k].wait_recv()
            outs[k][...] = (own[k][...].astype(f32) + got[k][...].astype(f32)).astype(bf16)
        for cp in cps:
            cp.wait_send()

    shapes = [(N_CHIPS, h, g.shape[2]) for g, h in zip(grads, halves)]
    return pl.pallas_call(
        body, name=f"pair_reduce_{name}", in_specs=[ANY] * nk, out_specs=[VMEM] * nk,
        out_shape=[jax.ShapeDtypeStruct(s, bf16) for s in shapes],
        scratch_shapes=[pltpu.VMEM(s, bf16) for s in shapes] * 2
        + [pltpu.SemaphoreType.DMA((nk,)), pltpu.SemaphoreType.DMA((nk,)), pltpu.SemaphoreType.DMA((nk,))],
        compiler_params=pltpu.CompilerParams(vmem_limit_bytes=VMEM_LIMIT, collective_id=collective_id),
    )(*grads)


class _Cross:
    def __init__(self, parts, landed, send_sems, recv_sems):
        self.parts, self.landed, self.send_sems, self.recv_sems = parts, landed, send_sems, recv_sems
        _, _, self.c, self.me, self.others = _place()

    def _copy(self, k, j, src_chunk, dst_slot):
        ox, oy = self.others[j]
        return _remote(self.parts[k].at[src_chunk], self.landed[k].at[dst_slot], self.send_sems.at[3 * k + j],
                       self.recv_sems.at[3 * k + j], (ox, oy, self.c))

    def _each(self):
        return [(k, j, 2 * self.others[j][0] + self.others[j][1]) for k in range(len(self.parts)) for j in range(3)]

    def start(self):
        for k, j, chip in self._each():
            self._copy(k, j, chip, self.me).start()

    def finish(self):
        for k, j, chip in self._each():
            self._copy(k, j, chip, chip).wait_recv()
        for k, j, chip in self._each():
            self._copy(k, j, chip, self.me).wait_send()

    @staticmethod
    def scratch(n):
        return [pltpu.SemaphoreType.DMA((3 * n,)), pltpu.SemaphoreType.DMA((3 * n,))]


class _CrossAll:
    def __init__(self, grads, landed, send_sems, recv_sems):
        self.grads, self.landed, self.send_sems, self.recv_sems = grads, landed, send_sems, recv_sems
        self.x, self.y, self.c, self.me, self.others = _place()
        self.dev = 2 * self.me + self.c

    def _piece(self, k, chip, half):
        rows = self.grads[k].shape[1] // 2
        return self.grads[k].at[chip, pl.ds(half * rows, rows), :]

    def _to_sibling(self, k):
        return _remote(self._piece(k, self.me, 1 - self.c), self.landed[k].at[self.dev], self.send_sems.at[7 * k],
                       self.recv_sems.at[7 * k], (self.x, self.y, 1 - self.c))

    def _to_chip(self, k, j, half):
        ox, oy = self.others[j]
        return _remote(self._piece(k, 2 * ox + oy, half), self.landed[k].at[self.dev],
                       self.send_sems.at[7 * k + 1 + 2 * j + half], self.recv_sems.at[7 * k + 1 + 2 * j + self.c],
                       (ox, oy, half))

    def _from(self, k, sem, slot):
        return _remote(self.landed[k].at[slot], self.landed[k].at[slot], self.send_sems.at[7 * k + sem],
                       self.recv_sems.at[7 * k + sem], (self.x, self.y, 1 - self.c))

    def start(self):
        for k in range(len(self.grads)):
            self._to_sibling(k).start()
            for j in range(3):
                for half in range(2):
                    self._to_chip(k, j, half).start()

    def finish(self):
        for k in range(len(self.grads)):
            self._from(k, 0, 2 * self.me + 1 - self.c).wait_recv()
            for j, (ox, oy) in enumerate(self.others):
                for core in range(2):
                    self._from(k, 1 + 2 * j + core, 4 * ox + 2 * oy + core).wait_recv()
        for k in range(len(self.grads)):
            self._to_sibling(k).wait_send()
            for j in range(3):
                for half in range(2):
                    self._to_chip(k, j, half).wait_send()

    @staticmethod
    def scratch(n):
        return [pltpu.SemaphoreType.DMA((7 * n,)), pltpu.SemaphoreType.DMA((7 * n,))]


def _rs_sum_chips(name, place, landed, part):
    _, half, cols = landed.shape

    def body(place_ref, l_ref, p_ref, o_ref):
        me = place_ref[0]
        own = p_ref[0].astype(f32)
        acc = jnp.where(me == 0, own, l_ref[0].astype(f32))
        for j in range(1, N_CHIPS):
            acc = acc + jnp.where(me == j, own, l_ref[j].astype(f32))
        o_ref[...] = acc

    return pl.pallas_call(
        body, name=f"rs_sum_chips_{name}",
        grid_spec=pltpu.PrefetchScalarGridSpec(
            num_scalar_prefetch=1, grid=(1,),
            in_specs=[pl.BlockSpec((N_CHIPS, half, cols), lambda t, place_ref: (0, 0, 0)),
                      pl.BlockSpec((1, half, cols), lambda t, place_ref: (place_ref[0], 0, 0))],
            out_specs=pl.BlockSpec((half, cols), lambda t, place_ref: (place_ref[1], 0))),
        out_shape=jax.ShapeDtypeStruct((2 * half, cols), f32),
        compiler_params=_cp("arbitrary"),
    )(place, landed, part)


def _grad_pair(shards):
    nk = len(shards)

    def body(*refs):
        outs = refs[nk:2 * nk]
        send_sems, recv_sems = refs[2 * nk:]
        x, y, c, _, _ = _place()
        _sibling_handshake(x, y, c)

        def half(k, core):
            h = outs[k].shape[0] // 2
            return outs[k].at[pl.ds(core * h, h), :]

        cps = [_remote(half(k, c), half(k, c), send_sems.at[k], recv_sems.at[k], (x, y, 1 - c)) for k in range(nk)]
        for cp in cps:
            cp.start()
        for k in range(nk):
            _remote(half(k, 1 - c), half(k, 1 - c), send_sems.at[k], recv_sems.at[k], (x, y, 1 - c)).wait_recv()
        for cp in cps:
            cp.wait_send()

    return pl.pallas_call(
        body, name="grad_pair", in_specs=[ANY] * nk, out_specs=[ANY] * nk,
        out_shape=[jax.ShapeDtypeStruct(a.shape, f32) for a in shards],
        input_output_aliases={k: k for k in range(nk)},
        scratch_shapes=[pltpu.SemaphoreType.DMA((nk,)), pltpu.SemaphoreType.DMA((nk,))],
        compiler_params=pltpu.CompilerParams(collective_id=2),
    )(*shards)


def _adam_math(w, g, m, v):
    m = ADAM_B1 * m + (1.0 - ADAM_B1) * g
    v = ADAM_B2 * v + (1.0 - ADAM_B2) * (g * g)
    m_hat = m / (1.0 - ADAM_B1 ** ADAM_STEP)
    v_hat = v / (1.0 - ADAM_B2 ** ADAM_STEP)
    delta = -ADAM_LR * (m_hat / (jnp.sqrt(v_hat) + ADAM_EPS) + ADAM_WD * w)
    return delta, m, v


def _adam(ws, gs, ms, vs):
    n = len(ws)
    steps = 2 * SHARD_STEPS

    def body(*refs):
        for k in range(n):
            w_ref, g_ref, m_ref, v_ref = (refs[j * n + k] for j in range(4))
            go_ref, d_ref, nm_ref, nv_ref = (refs[(4 + j) * n + k] for j in range(4))
            g = g_ref[...]
            go_ref[...] = g
            d_ref[...], nm_ref[...], nv_ref[...] = _adam_math(w_ref[...], g, m_ref[...], v_ref[...])

    specs = [_row(0, (w.shape[0] // steps, w.shape[1])) for w in ws]
    res = pl.pallas_call(
        body, name="adam_shards", grid=(steps,), in_specs=specs * 4, out_specs=specs * 4,
        out_shape=[jax.ShapeDtypeStruct(w.shape, f32) for w in ws] * 4,
        compiler_params=_cp("parallel"),
    )(*ws, *gs, *ms, *vs)
    return [tuple(res[j * n + k] for j in range(4)) for k in range(n)]


class _SmallGather:
    def __init__(self, parts, bufs, send_sems, recv_sems):
        self.parts, self.bufs, self.send_sems, self.recv_sems = parts, bufs, send_sems, recv_sems
        self.x, self.y, self.c, _, self.others = _place()
        self.sibling = (self.x, self.y, 1 - self.c)

    def _copy(self, a, k, block, to, src=None):
        slot = self.bufs[a].at[4 * block[0] + 2 * block[1] + block[2]]
        return _remote(slot if src is None else src, slot, self.send_sems.at[7 * a + k], self.recv_sems.at[7 * a + k],
                       to)

    def _first(self, a):
        me = (self.x, self.y, self.c)
        return [self._copy(a, 0, me, self.sibling, src=self.parts[a])] + [
            self._copy(a, 1 + j, me, (*chip, self.c), src=self.parts[a]) for j, chip in enumerate(self.others)]

    def _passed(self, a):
        return [self._copy(a, 4 + j, (*chip, self.c), self.sibling) for j, chip in enumerate(self.others)]

    @staticmethod
    def scratch(n):
        return [pltpu.SemaphoreType.DMA((7 * n,)), pltpu.SemaphoreType.DMA((7 * n,))]

    def start(self):
        for a in range(len(self.parts)):
            for cp in self._first(a):
                cp.start()

    def finish(self):
        sent = []
        for a in range(len(self.parts)):
            passed = self._passed(a)
            for j, chip in enumerate(self.others):
                self._copy(a, 1 + j, (*chip, self.c), self.sibling).wait_recv()
                passed[j].start()
            sent += self._first(a) + passed
        for a in range(len(self.parts)):
            self._copy(a, 0, self.sibling, self.sibling).wait_recv()
            for j, chip in enumerate(self.others):
                self._copy(a, 4 + j, (*chip, 1 - self.c), self.sibling).wait_recv()
        for cp in sent:
            cp.wait_send()


def _gathered_shapes(parts):
    return [jax.ShapeDtypeStruct((N_DEV,) + p.shape, p.dtype) for p in parts]


def _rs_sum_group(name, place, landed, parts, landed_all, grads_all, cross_parts, small_parts):
    nk, na, nx, ns = len(landed), len(landed_all), len(cross_parts), len(small_parts)
    dims = [a.shape[1:] for a in landed]
    dims_all = [a.shape[1:] for a in landed_all]

    def body(place_ref, *refs):
        take = iter(refs)
        l_refs, p_refs, la_refs, ga_refs, x_refs, sp_refs, o_refs, oa_refs, xl_refs, sbufs = (
            [next(take) for _ in range(cnt)] for cnt in (nk, nk, na, na, nx, ns, nk, na, nx, ns))
        sems = list(take)
        cross = _Cross(x_refs, xl_refs, sems[0], sems[1])
        small = _SmallGather(sp_refs, sbufs, sems[2], sems[3])
        t = pl.program_id(0)

        @pl.when(t == 0)
        def _():
            cross.start()
            small.start()

        me = place_ref[0]
        for l_ref, p_ref, o_ref in zip(l_refs, p_refs, o_refs):
            own = p_ref[0].astype(f32)
            acc = jnp.where(me == 0, own, l_ref[0].astype(f32))
            for j in range(1, N_CHIPS):
                acc = acc + jnp.where(me == j, own, l_ref[j].astype(f32))
            o_ref[...] = acc
        dev = 2 * me + place_ref[1]
        for l_ref, g_ref, o_ref in zip(la_refs, ga_refs, oa_refs):
            own = g_ref[0].astype(f32)
            acc = jnp.where(dev == 0, own, l_ref[0].astype(f32))
            for d in range(1, N_DEV):
                acc = acc + jnp.where(dev == d, own, l_ref[d].astype(f32))
            o_ref[...] = acc

        @pl.when(t == 1)
        def _():
            small.finish()
            cross.finish()

    def halves(h, c, lead, index):
        return pl.BlockSpec((lead, h // 2, c) if lead else (h // 2, c), index)

    in_specs = [halves(h, c, N_CHIPS, lambda t, pr: (0, t, 0)) for h, c in dims]
    in_specs += [halves(h, c, 1, lambda t, pr: (pr[0], t, 0)) for h, c in dims]
    in_specs += [halves(h, c, N_DEV, lambda t, pr: (0, t, 0)) for h, c in dims_all]
    in_specs += [halves(h, c, 1, lambda t, pr: (pr[0], 2 * pr[1] + t, 0)) for h, c in dims_all]
    in_specs += [ANY] * (nx + ns)
    out_specs = [halves(h, c, 0, lambda t, pr: (2 * pr[1] + t, 0)) for h, c in dims + dims_all] + [ANY] * (nx + ns)
    out_shape = [jax.ShapeDtypeStruct((2 * h, c), f32) for h, c in dims + dims_all]
    out_shape += [jax.ShapeDtypeStruct(a.shape, a.dtype) for a in cross_parts] + _gathered_shapes(small_parts)
    res = pl.pallas_call(
        body, name=f"rs_sum_{name}",
        grid_spec=pltpu.PrefetchScalarGridSpec(
            num_scalar_prefetch=1, grid=(2,), in_specs=in_specs, out_specs=out_specs,
            scratch_shapes=_Cross.scratch(nx) + _SmallGather.scratch(ns)),
        out_shape=out_shape, compiler_params=_cp("arbitrary"),
    )(place, *landed, *parts, *landed_all, *grads_all, *cross_parts, *small_parts)
    return res[:nk], res[nk:nk + na], res[nk + na:nk + na + nx], res[nk + na + nx:]


SMALL_PARAMS = ("g_ple_gate", "g_ple_post", "g_final", "g_ffn", "ln_g", "ln_b", "conv_b", "pool_scale", "pool_w",
                "conv_w", "g_mix")
SMALL_ROWS = {"g_ple_gate": (0, 0), "g_ple_post": (0, 1), "g_final": (0, 2), "g_ffn": (1, 0), "ln_g": (2, 0),
              "ln_b": (2, 1), "conv_b": (2, 2), "pool_scale": (2, 3), "g_mix": (5, 0)}
LOSS_ROW = (0, 3)


def _small_adam(place, gathered, parts, params):
    nb, names = len(parts), SMALL_PARAMS
    flat = [a for nm in names for a in params[nm]]

    def body(place_ref, *refs):
        b_refs, p_refs = refs[:nb], refs[nb:2 * nb]
        w_refs = refs[2 * nb:2 * nb + 3 * len(names)]
        outs = refs[2 * nb + 3 * len(names):]
        loss_ref, o_refs, cw_sum = outs[0], outs[1:1 + 4 * len(names)], outs[1 + 4 * len(names)]
        chip = place_ref[0]
        me = 2 * chip + place_ref[1]

        def total(blk, idx):
            own = p_refs[blk][idx]
            g = jnp.where(me == 0, own, b_refs[blk][(0,) + idx])
            for d in range(1, N_DEV):
                g = g + jnp.where(me == d, own, b_refs[blk][(d,) + idx])
            return g

        everything = (slice(None), slice(None))
        loss_ref[...] = total(LOSS_ROW[0], (pl.ds(LOSS_ROW[1], 1), pl.ds(0, 128)))
        d_cw = total(4, everything)
        mine = jnp.where(chip == 0, d_cw[:, 0:128], 0.0)
        for j in range(1, N_CHIPS):
            mine = mine + jnp.where(chip == j, d_cw[:, j * 128:(j + 1) * 128], 0.0)
        cw_sum[...] = mine
        for k, nm in enumerate(names):
            w_ref, m_ref, v_ref = w_refs[3 * k:3 * k + 3]
            g_ref, d_ref, nm_ref, nv_ref = o_refs[4 * k:4 * k + 4]
            if nm == "pool_w":
                g = total(3, everything + (slice(None),))
            elif nm == "conv_w":
                g = cw_sum[pl.ds(0, CONV_K), :]
            else:
                blk, row = SMALL_ROWS[nm]
                g = total(blk, (pl.ds(row, 1), slice(None)))
            g_ref[...] = g
            d_ref[...], nm_ref[...], nv_ref[...] = _adam_math(w_ref[...], g, m_ref[...], v_ref[...])

    whole = lambda a: pl.BlockSpec(a.shape, lambda t, pr: (0,) * a.ndim)
    out_shape = [jax.ShapeDtypeStruct((1, 128), f32)]
    out_shape += [jax.ShapeDtypeStruct(params[nm][0].shape, f32) for nm in names for _ in range(4)]
    res = pl.pallas_call(
        body, name="small_adam",
        grid_spec=pltpu.PrefetchScalarGridSpec(
            num_scalar_prefetch=1, grid=(1,),
            in_specs=[whole(a) for a in list(gathered) + list(parts) + flat],
            out_specs=[whole(s) for s in out_shape], scratch_shapes=[pltpu.VMEM((HALO, 128), f32)]),
        out_shape=out_shape, compiler_params=_cp("arbitrary"),
    )(place, *gathered, *parts, *flat)
    return res[0], {nm: res[1 + 4 * k:5 + 4 * k] for k, nm in enumerate(names)}


def _pad_rows(a, rows):
    return jnp.concatenate([a, jnp.zeros((rows - a.shape[0],) + a.shape[1:], a.dtype)], axis=0)


def kernel(x, p, g_mix, w_in, conv_w, conv_b, ln_g, ln_b, pool_w, pool_scale, w_out, g_ffn, w_gate_up, w_down, g_ple_gate, w_ple_gate, w_ple_up, g_ple_post, g_final, loss_target, m_g_mix, m_w_in, m_conv_w, m_conv_b, m_ln_g, m_ln_b, m_pool_w, m_pool_scale, m_w_out, m_g_ffn, m_w_gate_up, m_w_down, m_g_ple_gate, m_w_ple_gate, m_w_ple_up, m_g_ple_post, m_g_final, v_g_mix, v_w_in, v_conv_w, v_conv_b, v_ln_g, v_ln_b, v_pool_w, v_pool_scale, v_w_out, v_g_ffn, v_w_gate_up, v_w_down, v_g_ple_gate, v_w_ple_gate, v_w_ple_up, v_g_ple_post, v_g_final):
    seq = x.shape[1]
    me = 2 * lax.axis_index("x") + lax.axis_index("y")
    chip = me.astype(jnp.int32).reshape(1)
    core = lax.axis_index("c").astype(jnp.int32).reshape(1)
    place = jnp.concatenate([chip, core])
    xs, ps, ts = x.reshape(seq, D_MODEL), p.reshape(seq, D_PLE), loss_target.reshape(seq, D_MODEL)

    big = [w_in[0], w_gate_up[0], w_out[0], w_down[0], w_ple_gate[0], w_ple_up[0]]
    big_m = [m_w_in[0], m_w_gate_up[0], m_w_out[0], m_w_down[0], m_w_ple_gate[0], m_w_ple_up[0]]
    big_v = [v_w_in[0], v_w_gate_up[0], v_w_out[0], v_w_down[0], v_w_ple_gate[0], v_w_ple_up[0]]
    b_in, b_gu, b_out, b_down, b_pg, b_pu, b_cw = _cast_into_slots(
        chip, big + [_pad_rows(conv_w[0], HALO)], [bf16] * len(big) + [f32])
    xi, yi = lax.axis_index("x"), lax.axis_index("y")
    order = jnp.stack([me, 2 * (1 - xi) + yi, 2 * xi + 1 - yi, 2 * (1 - xi) + 1 - yi]).astype(jnp.int32)

    z, (w_in_g, cw_g, w_out_g) = _mix_in(xs, g_mix, order, [b_in, b_cw, b_out])
    conv_w_f = cw_g.transpose(1, 0, 2).reshape(HALO, C_CONV)
    w_out_f = w_out_g.reshape(D_MODEL, D_MODEL)
    (x1, mix, u1, pooled, h2), (w_gu_g,) = _conv_pool_out(z, xs, conv_w_f, conv_b, ln_g, ln_b, pool_w[0], pool_scale,
                                                          w_out_f, g_ffn, [b_gu])
    (gu, ffn_f), (w_down_g, w_pg_g, w_pu_g) = _ffn_up(h2, w_gu_g, [b_down, b_pg, b_pu])
    w_down_f = w_down_g.reshape(D_FF, D_MODEL)
    x2 = _ffn_down(x1, ffn_f, w_down_f)
    w_pg_f = w_pg_g.reshape(D_MODEL, D_MODEL)
    dx2, d_w_pg, d_w_pu, small_ple = _ple_loss(x2, ps, ts, g_ple_gate, g_ple_post, g_final.reshape(1, D_MODEL),
                                               w_pg_f, w_pu_g)
    d_w_down = _ffn_bwd_dw_down(ffn_f, dx2)
    grads_a = [d_w_down.reshape(N_CHIPS, -1, D_MODEL), d_w_pg.reshape(N_CHIPS, -1, D_MODEL), d_w_pu]
    (dx1, dgu, small_ffn), landed_a = _ffn_bwd_dx(dx2, x1, gu, g_ffn, w_gu_g, w_down_f, grads_a)
    d_w_gu = _ffn_bwd_dw_gu(h2, dgu)
    du1, dpo, d_w_out, d_pool_w, small_mix = _mix_bwd_local(dx1, mix, u1, pooled, w_out_f, ln_g, ln_b, pool_w[0],
                                                             pool_scale)
    parts_b = _pair_reduce("b", 0, [d_w_gu, d_w_out.reshape(N_CHIPS, -1, D_MODEL)])
    small_0 = [small_ple, small_ffn, small_mix, d_pool_w]
    (grad_x, d_w_in, d_conv_w, small_in), landed_b, small_all_0 = _in_bwd(du1, dpo, z, xs, dx1, conv_w_f, g_mix,
                                                                            w_in_g, parts_b, small_0)
    parts_c = _pair_reduce("c", 1, [d_w_in])
    small_1 = [d_conv_w, small_in]
    (h_gu, h_out), (h_down, h_pg, h_pu), landed_c, small_all_1 = _rs_sum_group(
        "ab", place, landed_b, parts_b, landed_a, grads_a, parts_c, small_1)
    h_in = _rs_sum_chips("w_in", place, landed_c[0], parts_c[0])
    big_g = _grad_pair([h_in, h_gu, h_out, h_down, h_pg, h_pu])
    big_upd = _adam(big, big_g, big_m, big_v)

    p3 = lambda w, m, v: (w, m, v)
    row = lambda a: a.reshape(1, D_MODEL)
    params = dict(
        g_ple_gate=p3(g_ple_gate, m_g_ple_gate, v_g_ple_gate), g_ple_post=p3(g_ple_post, m_g_ple_post, v_g_ple_post),
        g_final=p3(row(g_final), row(m_g_final), row(v_g_final)), g_ffn=p3(g_ffn, m_g_ffn, v_g_ffn),
        ln_g=p3(ln_g, m_ln_g, v_ln_g), ln_b=p3(ln_b, m_ln_b, v_ln_b), conv_b=p3(conv_b, m_conv_b, v_conv_b),
        pool_scale=p3(pool_scale, m_pool_scale, v_pool_scale), pool_w=p3(pool_w[0], m_pool_w[0], v_pool_w[0]),
        conv_w=p3(conv_w[0], m_conv_w[0], v_conv_w[0]), g_mix=p3(g_mix, m_g_mix, v_g_mix))
    loss, small = _small_adam(place, list(small_all_0) + list(small_all_1), small_0 + small_1, params)
    back = dict(g_final=lambda a: a.reshape(D_MODEL), pool_w=lambda a: a[None], conv_w=lambda a: a[None])

    names = ["g_mix", "w_in", "conv_w", "conv_b", "ln_g", "ln_b", "pool_w", "pool_scale", "w_out", "g_ffn",
             "w_gate_up", "w_down", "g_ple_gate", "w_ple_gate", "w_ple_up", "g_ple_post", "g_final"]
    big_at = {"w_in": 0, "w_gate_up": 1, "w_out": 2, "w_down": 3, "w_ple_gate": 4, "w_ple_up": 5}
    out = [loss[0, 0], grad_x.reshape(1, seq, D_MODEL)]
    for kind in range(4):
        for nm in names:
            if nm in big_at:
                out.append(big_upd[big_at[nm]][kind][None])
            else:
                out.append(back.get(nm, lambda a: a)(small[nm][kind]))
    return tuple(out)
```

```python
import functools

import jax
import jax.numpy as jnp
from jax import lax
from jax.experimental import pallas as pl
from jax.experimental.pallas import tpu as pltpu

f32, bf16 = jnp.float32, jnp.bfloat16

EPS = 1e-6
D_MODEL = 1024
C_CONV = 512
C_POOL = 512
POOL_WINDOWS = (2, 4, 8, 16)
POOL_GROUP = 128
CONV_K = 31
D_FF = 2816
D_PLE = 256
N_CHIPS = 4
N_DEV = 8
W_IN_COLS = 2 * C_CONV + C_POOL
W_IN_CHUNK = W_IN_COLS // N_CHIPS
FF_CHUNK = 2 * D_FF // N_CHIPS
PLE_CHUNK = D_MODEL // N_CHIPS
HALO = 32
ROW_TILE = 512
SHIFT_PAD = 32
SHIFT_GROUPS = 16
FF_SUB = (0, 512, 1024, FF_CHUNK)
VMEM_LIMIT = 56 * 1024 * 1024

ADAM_LR = 0.001
ADAM_B1 = 0.9
ADAM_B2 = 0.999
ADAM_EPS = 1e-08
ADAM_WD = 0.01
ADAM_STEP = 10

MESH = pl.DeviceIdType.MESH
ANY = pl.BlockSpec(memory_space=pl.ANY)
VMEM = pl.BlockSpec(memory_space=pltpu.VMEM)


def _cp(*sem, collective_id=None):
    return pltpu.CompilerParams(dimension_semantics=sem, vmem_limit_bytes=VMEM_LIMIT, collective_id=collective_id)


def _dot(a, b):
    return jnp.dot(a, b, preferred_element_type=f32)


def _dot_nt(a, b):
    return lax.dot_general(a, b, (((1,), (1,)), ((), ())), preferred_element_type=f32)


def _dot_tn(a, b):
    return lax.dot_general(a, b, (((0,), (0,)), ((), ())), preferred_element_type=f32)


def _sigmoid(v):
    return jax.nn.sigmoid(v)


def _rms_fwd(v, g):
    r = lax.rsqrt(jnp.mean(v * v, axis=-1, keepdims=True) + EPS)
    vh = v * r
    return vh * g, vh, r


def _rms_bwd(dy, vh, r, g):
    dvh = dy * g
    dv = r * (dvh - vh * jnp.mean(dvh * vh, axis=-1, keepdims=True))
    return dv, jnp.sum(dy * vh, axis=0, keepdims=True)


def _silu_grad(v, s):
    return s * (1.0 + v * (1.0 - s))


def _row(i, n):
    return pl.BlockSpec((n[0], n[1]), lambda *a: (a[i], 0))


def _full(shape):
    nd = len(shape)
    return pl.BlockSpec(shape, lambda *a: (0,) * nd)


def _place():
    x, y, c = lax.axis_index("x"), lax.axis_index("y"), lax.axis_index("c")
    others = [(1 - x, y), (x, 1 - y), (1 - x, 1 - y)]
    return x, y, c, 2 * x + y, others


def _handshake(peers):
    barrier = pltpu.get_barrier_semaphore()
    for peer in peers:
        pl.semaphore_signal(barrier, inc=1, device_id=peer, device_id_type=MESH)
    pl.semaphore_wait(barrier, len(peers))


def _remote(src, dst, send_sem, recv_sem, dev):
    return pltpu.make_async_remote_copy(src_ref=src, dst_ref=dst, send_sem=send_sem, recv_sem=recv_sem,
                                        device_id=dev, device_id_type=MESH)


SHARD_STEPS = 4


def _cast_into_slots(me, ws, dtypes):
    n = len(ws)

    def body(me_ref, *refs):
        for w_ref, o_ref, dtype in zip(refs[:n], refs[n:], dtypes):
            o_ref[0] = w_ref[...].astype(dtype)

    return pl.pallas_call(
        body, name="cast_shards",
        grid_spec=pltpu.PrefetchScalarGridSpec(
            num_scalar_prefetch=1, grid=(SHARD_STEPS,),
            in_specs=[pl.BlockSpec((w.shape[0] // SHARD_STEPS, w.shape[1]), lambda r, me_ref: (r, 0)) for w in ws],
            out_specs=[pl.BlockSpec((1, w.shape[0] // SHARD_STEPS, w.shape[1]), lambda r, me_ref: (me_ref[0], r, 0))
                       for w in ws]),
        out_shape=[jax.ShapeDtypeStruct((N_CHIPS,) + w.shape, dt) for w, dt in zip(ws, dtypes)],
        compiler_params=_cp("parallel"),
    )(me, *ws)


class _Gather:
    def __init__(self, bufs, send_sems, recv_sems):
        self.bufs, self.send_sems, self.recv_sems = bufs, send_sems, recv_sems
        self.x, self.y, self.c, self.me, self.others = _place()
        self.halves = [b.shape[1] // 2 for b in bufs]

    def _piece(self, k, chip, half):
        return self.bufs[k].at[chip, pl.ds(half * self.halves[k], self.halves[k]), :]

    def _ici(self, k, j, chip):
        ox, oy = self.others[j]
        piece = self._piece(k, chip, self.c)
        return _remote(piece, piece, self.send_sems.at[6 * k + j], self.recv_sems.at[6 * k + j], (ox, oy, self.c))

    def _relay(self, k):
        first = self.c == 0
        piece = self._piece(k, jnp.where(first, self.chip(0), self.chip(1)), self.c)
        to = (jnp.where(first, self.others[1][0], self.others[0][0]),
              jnp.where(first, self.others[1][1], self.others[0][1]), self.c)
        return _remote(piece, piece, self.send_sems.at[6 * k + 2], self.recv_sems.at[6 * k + 2], to)

    def _pair(self, k, j, half):
        ox, oy = self.others[j]
        piece = self._piece(k, 2 * ox + oy, half)
        return _remote(piece, piece, self.send_sems.at[6 * k + 3 + j], self.recv_sems.at[6 * k + 3 + j],
                       (self.x, self.y, 1 - self.c))

    def _each(self, ks=None):
        return [(k, j) for k in (range(len(self.bufs)) if ks is None else ks) for j in range(3)]

    def chip(self, j):
        ox, oy = self.others[j]
        return 2 * ox + oy

    def start(self):
        _handshake([(self.x, self.y, 1 - self.c)] + [(*self.others[j], self.c) for j in range(2)])
        for k in range(len(self.bufs)):
            for j in range(2):
                self._ici(k, j, self.me).start()

    def forward(self, pairs=None):
        for k, j in self._each() if pairs is None else pairs:
            self._ici(k, j, self.chip(j)).wait_recv()
            self._pair(k, j, self.c).start()
            if j < 2:
                pl.when(self.c == j)(self._relay(k).start)

    def landed(self, pairs):
        for k, j in pairs:
            self._pair(k, j, 1 - self.c).wait_recv()

    def finish(self, ks=None):
        self.landed(self._each(ks))
        for k in range(len(self.bufs)):
            for j in range(2):
                self._ici(k, j, self.me).wait_send()
            self._relay(k).wait_send()
            for j in range(3):
                self._pair(k, j, self.c).wait_send()

    @staticmethod
    def scratch(n):
        return [pltpu.SemaphoreType.DMA((6 * n,)), pltpu.SemaphoreType.DMA((6 * n,))]


def _carried(bufs):
    n = len(bufs)
    return dict(in_specs=[ANY] * n, out_specs=[ANY] * n,
                out_shape=[jax.ShapeDtypeStruct(b.shape, b.dtype) for b in bufs], scratch=_Gather.scratch(n))


def _mix_in(x, g_mix, order, carry):
    s = x.shape[0]
    tm = min(2 * ROW_TILE, s)
    n = s // tm
    nc = len(carry)
    cs = _carried(carry)

    def body(order_ref, x_ref, g_ref, *refs):
        z_ref = refs[nc]
        bufs = refs[nc + 1:2 * nc + 1]
        h_ref, w_ref, w_sem = refs[2 * nc + 1:2 * nc + 4]
        gather = _Gather(bufs, *refs[2 * nc + 4:])
        q, i = pl.program_id(0), pl.program_id(1)
        first = i == 0
        pl.when(jnp.logical_and(q == 0, first))(gather.start)
        for j in range(3):

            @pl.when(jnp.logical_and(q == j + 1, first))
            def _():
                gather.forward([(0, j)])
                gather.landed([(0, j)])
                if j == 1:
                    gather.forward([(k, jj) for k in range(1, nc) for jj in range(2)])

        @pl.when(first)
        def _():
            load = pltpu.make_async_copy(bufs[0].at[order_ref[q]], w_ref, w_sem)
            load.start()
            load.wait()

        @pl.when(q == 0)
        def _():
            h, _, _ = _rms_fwd(x_ref[...], g_ref[...])
            h_ref[i] = h.astype(bf16)

        z_ref[...] = _dot(h_ref[i], w_ref[...])

        @pl.when(jnp.logical_and(q == N_CHIPS - 1, i == n - 1))
        def _():
            gather.forward([(k, 2) for k in range(1, nc)])
            gather.finish(range(1, nc))

    res = pl.pallas_call(
        body, name="mix_in",
        grid_spec=pltpu.PrefetchScalarGridSpec(
            num_scalar_prefetch=1, grid=(N_CHIPS, n),
            in_specs=[pl.BlockSpec((tm, D_MODEL), lambda q, i, order_ref: (jnp.where(q == 0, i, 0), 0)),
                      pl.BlockSpec((1, D_MODEL), lambda q, i, order_ref: (0, 0))] + cs["in_specs"],
            out_specs=[pl.BlockSpec((tm, W_IN_CHUNK), lambda q, i, order_ref: (i, order_ref[q]))] + cs["out_specs"],
            scratch_shapes=[pltpu.VMEM((n, tm, D_MODEL), bf16), pltpu.VMEM((D_MODEL, W_IN_CHUNK), bf16),
                            pltpu.SemaphoreType.DMA(())] + cs["scratch"]),
        out_shape=[jax.ShapeDtypeStruct((s, W_IN_COLS), f32)] + cs["out_shape"],
        input_output_aliases={3 + k: 1 + k for k in range(nc)},
        compiler_params=_cp("arbitrary", "arbitrary", collective_id=3),
    )(order, x, g_mix, *carry)
    return res[0], res[1:]


def _pool_counts(tm, w, first_row):
    t1 = (lax.broadcasted_iota(jnp.int32, (tm, 1), 0) + first_row + 1).astype(f32)
    return jnp.minimum(t1, float(w))


def _conv_pool_out(z, x, conv_w, conv_b, ln_g, ln_b, pool_w, pool_scale, w_out, g_ffn, carry):
    s = x.shape[0]
    tm = min(ROW_TILE, s)
    n = s // tm
    hb = tm // HALO
    nv = tm // 8
    assert nv >= SHIFT_PAD and nv % SHIFT_GROUPS == 0
    nc = len(carry)
    cs = _carried(carry)

    def body(z_ref, zp_ref, x_ref, cw_ref, cb_ref, lg_ref, lb_ref, pw_ref, ps_ref, wo_ref, gf_ref, *refs):
        x1_ref, mix_ref, u1_ref, pooled_ref, h2_ref = refs[nc:nc + 5]
        eu, ev, ss, u0_ref = refs[2 * nc + 5:2 * nc + 9]
        gather = _Gather(refs[nc + 5:2 * nc + 5], *refs[2 * nc + 9:])
        i = pl.program_id(0)
        pl.when(i == 0)(gather.start)
        for k in range(nc):
            pl.when(i == min(3 * n // 4 + k, n - 1))(functools.partial(gather.forward, [(k, 0), (k, 1)]))
        keep = (i > 0).astype(f32)
        zp = zp_ref[...] * keep
        u0_prev = zp[:, :C_CONV] * _sigmoid(zp[:, C_CONV:2 * C_CONV])
        u0_ref[...] = z_ref[:, :C_CONV] * _sigmoid(z_ref[:, C_CONV:2 * C_CONV])
        for c, w_pool in enumerate(POOL_WINDOWS):
            lanes = slice(c * 128, (c + 1) * 128)
            v_lanes = slice(2 * C_CONV + c * 128, 2 * C_CONV + (c + 1) * 128)
            eu[c, pl.ds(0, SHIFT_PAD, stride=8), :] = u0_prev[:, lanes]
            ev[c, pl.ds(0, SHIFT_PAD, stride=8), :] = zp[:, v_lanes]
            for j in range(8):
                rows = slice(j * nv, (j + 1) * nv)
                eu[c, pl.ds(SHIFT_PAD * 8 + j, nv, stride=8), :] = u0_ref[rows, lanes]
                ev[c, pl.ds(SHIFT_PAD * 8 + j, nv, stride=8), :] = z_ref[rows, v_lanes]
                if j >= 1:
                    edge = slice(j * nv - SHIFT_PAD, j * nv)
                    eu[c, pl.ds(j, SHIFT_PAD, stride=8), :] = u0_ref[edge, lanes]
                    ev[c, pl.ds(j, SHIFT_PAD, stride=8), :] = z_ref[edge, v_lanes]
            for v0 in range(0, nv, SHIFT_GROUPS):
                span = SHIFT_GROUPS * 8
                acc = jnp.zeros((span, 128), f32) + cb_ref[:, lanes]
                for k in range(CONV_K):
                    acc = acc + cw_ref[pl.ds(k, 1), lanes] * eu[c, pl.ds((SHIFT_PAD - (CONV_K - 1) + v0 + k) * 8, span), :]
                ss[0, v0 * 8:v0 * 8 + span, :] = acc
                acc = ev[c, pl.ds((SHIFT_PAD + v0) * 8, span), :]
                for d in range(1, w_pool):
                    acc = acc + ev[c, pl.ds((SHIFT_PAD + v0 - d) * 8, span), :]
                ss[1, v0 * 8:v0 * 8 + span, :] = acc
            for j in range(8):
                rows = slice(j * nv, (j + 1) * nv)
                u1_ref[rows, lanes] = ss[0, pl.ds(j, nv, stride=8), :]
                mean = ss[1, pl.ds(j, nv, stride=8), :] / _pool_counts(nv, w_pool, i * tm + j * nv)
                pooled_ref[rows, lanes] = (mean - z_ref[rows, v_lanes]).astype(bf16)
        u1 = u1_ref[...]
        mu = jnp.mean(u1, axis=-1, keepdims=True)
        uc = u1 - mu
        rstd = lax.rsqrt(jnp.mean(uc * uc, axis=-1, keepdims=True) + EPS)
        u2 = uc * rstd * lg_ref[...] + lb_ref[...]
        mix_ref[:, :C_CONV] = (u2 * _sigmoid(u2)).astype(bf16)
        for g in range(len(POOL_WINDOWS)):
            cols = slice(g * POOL_GROUP, (g + 1) * POOL_GROUP)
            mixed = _dot(pooled_ref[:, cols], pw_ref[g].astype(bf16))
            mix_ref[:, C_CONV + g * POOL_GROUP:C_CONV + (g + 1) * POOL_GROUP] = (mixed * ps_ref[:, cols]).astype(bf16)
        x1 = x_ref[...] + _dot(mix_ref[...], wo_ref[...])
        x1_ref[...] = x1
        h2_ref[...] = _rms_fwd(x1, gf_ref[...])[0].astype(bf16)
        @pl.when(i == n - 1)
        def _():
            gather.forward([(k, 2) for k in range(nc)])
            gather.finish()

    res = pl.pallas_call(
        body, name="conv_pool_out", grid=(n,),
        in_specs=[_row(0, (tm, W_IN_COLS)),
                  pl.BlockSpec((HALO, W_IN_COLS), lambda i: (jnp.maximum(i * hb - 1, 0), 0)),
                  _row(0, (tm, D_MODEL)), _full((HALO, C_CONV)), _full((1, C_CONV)), _full((1, C_CONV)),
                  _full((1, C_CONV)), _full((4, POOL_GROUP, POOL_GROUP)), _full((1, C_POOL)),
                  _full((D_MODEL, D_MODEL)), _full((1, D_MODEL))] + cs["in_specs"],
        out_specs=[_row(0, (tm, D_MODEL)), _row(0, (tm, D_MODEL)), _row(0, (tm, C_CONV)), _row(0, (tm, C_POOL)),
                   _row(0, (tm, D_MODEL))] + cs["out_specs"],
        out_shape=[jax.ShapeDtypeStruct((s, D_MODEL), f32), jax.ShapeDtypeStruct((s, D_MODEL), bf16),
                   jax.ShapeDtypeStruct((s, C_CONV), f32), jax.ShapeDtypeStruct((s, C_POOL), bf16),
                   jax.ShapeDtypeStruct((s, D_MODEL), bf16)] + cs["out_shape"],
        input_output_aliases={11 + k: 5 + k for k in range(nc)},
        scratch_shapes=[pltpu.VMEM((4, (SHIFT_PAD + nv) * 8, 128), f32), pltpu.VMEM((4, (SHIFT_PAD + nv) * 8, 128), f32),
                        pltpu.VMEM((2, tm, 128), f32), pltpu.VMEM((tm, C_CONV), f32)] + cs["scratch"],
        compiler_params=_cp("arbitrary", collective_id=4),
    )(z, z, x, conv_w, conv_b, ln_g, ln_b, pool_w, pool_scale, w_out, g_ffn, *carry)
    return res[:5], res[5:]


def _ffn_up(h2, w_gu_g, carry):
    s = h2.shape[0]
    tm = min(2 * ROW_TILE, s)
    n = s // tm
    nc = len(carry)
    cs = _carried(carry)

    def body(h2_ref, wg_ref, wu_ref, *refs):
        gu_ref, f_ref = refs[nc:nc + 2]
        gather = _Gather(refs[nc + 2:2 * nc + 2], *refs[2 * nc + 2:])
        c, i = pl.program_id(0), pl.program_id(1)
        pl.when(jnp.logical_and(i == 0, c == 0))(gather.start)
        direct = [(k, j) for k in range(nc) for j in range(2)]
        pl.when(jnp.logical_and(i == n - 1, c == 0))(functools.partial(gather.forward, direct))
        pl.when(jnp.logical_and(i == n // 2, c == 1))(functools.partial(gather.forward, [(k, 2) for k in range(nc)]))
        h = h2_ref[...]
        for lo, hi in zip(FF_SUB[:-1], FF_SUB[1:]):
            gate = _dot(h, wg_ref[0, :, lo:hi])
            up = _dot(h, wu_ref[0, :, lo:hi])
            gu_ref[0, :, lo:hi] = gate.astype(bf16)
            gu_ref[1, :, lo:hi] = up.astype(bf16)
            f_ref[:, lo:hi] = (gate * _sigmoid(gate) * up).astype(bf16)
        pl.when(jnp.logical_and(i == n - 1, c == 1))(gather.finish)

    res = pl.pallas_call(
        body, name="ffn_up", grid=(2, n),
        in_specs=[_row(1, (tm, D_MODEL)),
                  pl.BlockSpec((1, D_MODEL, FF_CHUNK), lambda c, i: (c, 0, 0)),
                  pl.BlockSpec((1, D_MODEL, FF_CHUNK), lambda c, i: (2 + c, 0, 0))] + cs["in_specs"],
        out_specs=[pl.BlockSpec((2, tm, FF_CHUNK), lambda c, i: (0, i, c)),
                   pl.BlockSpec((tm, FF_CHUNK), lambda c, i: (i, c))] + cs["out_specs"],
        out_shape=[jax.ShapeDtypeStruct((2, s, D_FF), bf16), jax.ShapeDtypeStruct((s, D_FF), bf16)] + cs["out_shape"],
        input_output_aliases={3 + k: 2 + k for k in range(nc)},
        scratch_shapes=cs["scratch"],
        compiler_params=_cp("arbitrary", "arbitrary", collective_id=5),
    )(h2, w_gu_g, w_gu_g, *carry)
    return res[:2], res[2:]


def _ffn_down(x1, f, w_down):
    s = x1.shape[0]
    tm = min(2 * ROW_TILE, s)

    def body(x1_ref, f_ref, wd_ref, x2_ref):
        x2_ref[...] = x1_ref[...] + _dot(f_ref[...], wd_ref[...])

    return pl.pallas_call(
        body, name="ffn_down", grid=(s // tm,),
        in_specs=[_row(0, (tm, D_MODEL)), _row(0, (tm, D_FF)), _full((D_FF, D_MODEL))],
        out_specs=_row(0, (tm, D_MODEL)), out_shape=jax.ShapeDtypeStruct((s, D_MODEL), f32),
        compiler_params=_cp("parallel"),
    )(x1, f, w_down)


def _ple_loss(x2, p, target, g_pg, g_post, g_final, w_pg, w_pu_g):
    s = x2.shape[0]
    tm = min(ROW_TILE, s)
    n = s // tm

    def body(x2_ref, p_ref, t_ref, gpg_ref, gpo_ref, gf_ref, wpg_ref, wpu_ref,
             dx2_ref, dwpg_ref, dwpu_ref, small_ref, apg_ref, apu_ref):
        i = pl.program_id(0)

        @pl.when(i == 0)
        def _():
            apg_ref[...] = jnp.zeros_like(apg_ref)
            apu_ref[...] = jnp.zeros_like(apu_ref)
            small_ref[...] = jnp.zeros_like(small_ref)

        x2 = x2_ref[...]
        h3, x2h, r2 = _rms_fwd(x2, gpg_ref[...])
        h3b = h3.astype(bf16)
        gate = _sigmoid(_dot(h3b, wpg_ref[...]))
        pb = p_ref[...].astype(bf16)
        pe = jnp.concatenate([_dot(pb, wpu_ref[j]) for j in range(N_CHIPS)], axis=-1)
        e, peh, rp = _rms_fwd(pe, gpo_ref[...])
        x3 = x2 + gate * e
        y, x3h, r3 = _rms_fwd(x3, gf_ref[...])
        d = y - t_ref[...]
        loss = 0.5 * jnp.sum(jnp.sum(d * d, axis=-1, keepdims=True) * (1.0 / D_MODEL), axis=0, keepdims=True)
        dx3, dgf = _rms_bwd(d * (1.0 / D_MODEL), x3h, r3, gf_ref[...])
        dpe, dgpo = _rms_bwd(dx3 * gate, peh, rp, gpo_ref[...])
        dgl = (dx3 * e * gate * (1.0 - gate)).astype(bf16)
        apg_ref[...] += _dot_tn(h3b, dgl)
        apu_ref[...] += _dot_tn(pb, dpe.astype(bf16))
        dh3 = _dot_nt(dgl, wpg_ref[...])
        dx2b, dgpg = _rms_bwd(dh3, x2h, r2, gpg_ref[...])
        dx2_ref[...] = dx3 + dx2b
        small_ref[0:1, :] += dgpg
        small_ref[1:2, :] += dgpo
        small_ref[2:3, :] += dgf
        small_ref[3:4, :] += jnp.broadcast_to(loss, (1, D_MODEL))

        @pl.when(i == n - 1)
        def _():
            dwpg_ref[...] = apg_ref[...].astype(bf16)
            for j in range(N_CHIPS):
                dwpu_ref[j] = apu_ref[:, j * PLE_CHUNK:(j + 1) * PLE_CHUNK].astype(bf16)

    return pl.pallas_call(
        body, name="ple_loss", grid=(n,),
        in_specs=[_row(0, (tm, D_MODEL)), _row(0, (tm, D_PLE)), _row(0, (tm, D_MODEL)),
                  _full((1, D_MODEL)), _full((1, D_MODEL)), _full((1, D_MODEL)),
                  _full((D_MODEL, D_MODEL)), _full((N_CHIPS, D_PLE, PLE_CHUNK))],
        out_specs=[_row(0, (tm, D_MODEL)), _full((D_MODEL, D_MODEL)), _full((N_CHIPS, D_PLE, PLE_CHUNK)),
                   _full((8, D_MODEL))],
        out_shape=[jax.ShapeDtypeStruct((s, D_MODEL), f32), jax.ShapeDtypeStruct((D_MODEL, D_MODEL), bf16),
                   jax.ShapeDtypeStruct((N_CHIPS, D_PLE, PLE_CHUNK), bf16), jax.ShapeDtypeStruct((8, D_MODEL), f32)],
        scratch_shapes=[pltpu.VMEM((D_MODEL, D_MODEL), f32), pltpu.VMEM((D_PLE, D_MODEL), f32)],
        compiler_params=_cp("arbitrary"),
    )(x2, p, target, g_pg, g_post, g_final, w_pg, w_pu_g)


def _crossed(parts):
    n = len(parts)
    return dict(in_specs=[ANY] * n, out_specs=[ANY] * n,
                out_shape=[jax.ShapeDtypeStruct(a.shape, a.dtype) for a in parts], scratch=_Cross.scratch(n))


def _ffn_bwd_dx(dx2, x1, gu, g_ffn, w_gu_g, w_down, grads):
    s = x1.shape[0]
    tm = min(ROW_TILE, s)
    n = s // tm
    nc = len(grads)
    cs = dict(in_specs=[ANY] * nc, out_specs=[ANY] * nc, scratch=_CrossAll.scratch(nc),
              out_shape=[jax.ShapeDtypeStruct((N_DEV, g.shape[1] // 2, g.shape[2]), g.dtype) for g in grads])

    def body(dx2_ref, x1_ref, gu_ref, g_ref, wg_ref, wu_ref, wd_ref, *refs):
        dx1_ref, dgu_ref, small_ref = refs[nc:nc + 3]
        acc_ref = refs[2 * nc + 3]
        cross = _CrossAll(refs[:nc], refs[nc + 3:2 * nc + 3], *refs[2 * nc + 4:])
        i, c = pl.program_id(0), pl.program_id(1)
        pl.when(jnp.logical_and(i == 0, c == 0))(cross.start)

        @pl.when(jnp.logical_and(i == 0, c == 0))
        def _():
            small_ref[...] = jnp.zeros_like(small_ref)

        @pl.when(c == 0)
        def _():
            acc_ref[...] = jnp.zeros_like(acc_ref)

        dyb = dx2_ref[...].astype(bf16)
        for lo, hi in zip(FF_SUB[:-1], FF_SUB[1:]):
            df = _dot_nt(dyb, wd_ref[lo:hi, :])
            gate = gu_ref[0, :, lo:hi].astype(f32)
            up = gu_ref[1, :, lo:hi].astype(f32)
            sg = _sigmoid(gate)
            dgate = (df * up * _silu_grad(gate, sg)).astype(bf16)
            dup = (df * gate * sg).astype(bf16)
            dgu_ref[0, :, lo:hi] = dgate
            dgu_ref[1, :, lo:hi] = dup
            acc_ref[...] += _dot_nt(dgate, wg_ref[0, :, lo:hi]) + _dot_nt(dup, wu_ref[0, :, lo:hi])

        @pl.when(c == 1)
        def _():
            _, x1h, r1 = _rms_fwd(x1_ref[...], g_ref[...])
            dx1b, dg = _rms_bwd(acc_ref[...], x1h, r1, g_ref[...])
            dx1_ref[...] = dx2_ref[...] + dx1b
            small_ref[0:1, :] += dg

        pl.when(jnp.logical_and(i == n - 1, c == 1))(cross.finish)

    res = pl.pallas_call(
        body, name="ffn_bwd_dx", grid=(n, 2),
        in_specs=[_row(0, (tm, D_MODEL)), _row(0, (tm, D_MODEL)),
                  pl.BlockSpec((2, tm, FF_CHUNK), lambda i, c: (0, i, c)), _full((1, D_MODEL)),
                  pl.BlockSpec((1, D_MODEL, FF_CHUNK), lambda i, c: (c, 0, 0)),
                  pl.BlockSpec((1, D_MODEL, FF_CHUNK), lambda i, c: (2 + c, 0, 0)),
                  pl.BlockSpec((FF_CHUNK, D_MODEL), lambda i, c: (c, 0))] + cs["in_specs"],
        out_specs=[_row(0, (tm, D_MODEL)), pl.BlockSpec((2, tm, FF_CHUNK), lambda i, c: (0, i, c)),
                   _full((8, D_MODEL))] + cs["out_specs"],
        out_shape=[jax.ShapeDtypeStruct((s, D_MODEL), f32), jax.ShapeDtypeStruct((2, s, D_FF), bf16),
                   jax.ShapeDtypeStruct((8, D_MODEL), f32)] + cs["out_shape"],
        scratch_shapes=[pltpu.VMEM((tm, D_MODEL), f32)] + cs["scratch"],
        compiler_params=_cp("arbitrary", "arbitrary"),
    )(dx2, x1, gu, g_ffn, w_gu_g, w_gu_g, w_down, *grads)
    return res[:3], res[3:]


def _ffn_bwd_dw_gu(h2, dgu):
    s = h2.shape[0]
    ts = min(2 * ROW_TILE, s)
    n = s // ts

    def body(h_ref, d_ref, o_ref, acc_ref):
        t = pl.program_id(1)

        @pl.when(t == 0)
        def _():
            acc_ref[...] = jnp.zeros_like(acc_ref)

        acc_ref[...] += _dot_tn(h_ref[...], d_ref[0])

        @pl.when(t == n - 1)
        def _():
            o_ref[0] = acc_ref[...].astype(bf16)

    return pl.pallas_call(
        body, name="ffn_bwd_dw_gu", grid=(N_CHIPS, n),
        in_specs=[pl.BlockSpec((ts, D_MODEL), lambda j, t: (t, 0)),
                  pl.BlockSpec((1, ts, FF_CHUNK), lambda j, t: (j // 2, t, j % 2))],
        out_specs=pl.BlockSpec((1, D_MODEL, FF_CHUNK), lambda j, t: (j, 0, 0)),
        out_shape=jax.ShapeDtypeStruct((N_CHIPS, D_MODEL, FF_CHUNK), bf16),
        scratch_shapes=[pltpu.VMEM((D_MODEL, FF_CHUNK), f32)],
        compiler_params=_cp("parallel", "arbitrary"),
    )(h2, dgu)


def _ffn_bwd_dw_down(f, dx2):
    s = dx2.shape[0]
    ts = min(2 * ROW_TILE, s)
    n = s // ts

    def body(f_ref, d_ref, o_ref, acc_ref):
        t = pl.program_id(1)

        @pl.when(t == 0)
        def _():
            acc_ref[...] = jnp.zeros_like(acc_ref)

        acc_ref[...] += _dot_tn(f_ref[...], d_ref[...].astype(bf16))

        @pl.when(t == n - 1)
        def _():
            o_ref[...] = acc_ref[...].astype(bf16)

    return pl.pallas_call(
        body, name="ffn_bwd_dw_down", grid=(2, n),
        in_specs=[pl.BlockSpec((ts, FF_CHUNK), lambda c, t: (t, c)),
                  pl.BlockSpec((ts, D_MODEL), lambda c, t: (t, 0))],
        out_specs=pl.BlockSpec((FF_CHUNK, D_MODEL), lambda c, t: (c, 0)),
        out_shape=jax.ShapeDtypeStruct((D_FF, D_MODEL), bf16),
        scratch_shapes=[pltpu.VMEM((FF_CHUNK, D_MODEL), f32)],
        compiler_params=_cp("parallel", "arbitrary"),
    )(f, dx2)


def _mix_bwd_local(dx1, mix, u1, pooled, w_out, ln_g, ln_b, pool_w, pool_scale):
    s = dx1.shape[0]
    tm = min(ROW_TILE, s)
    n = s // tm

    def body(dx1_ref, mix_ref, u1_ref, po_ref, wo_ref, lg_ref, lb_ref, pw_ref, ps_ref,
             du1_ref, dpo_ref, dwo_ref, dpw_ref, small_ref, awo_ref):
        i = pl.program_id(0)

        @pl.when(i == 0)
        def _():
            awo_ref[...] = jnp.zeros_like(awo_ref)
            dpw_ref[...] = jnp.zeros_like(dpw_ref)
            small_ref[...] = jnp.zeros_like(small_ref)

        dyb = dx1_ref[...].astype(bf16)
        dmix = _dot_nt(dyb, wo_ref[...])
        awo_ref[...] += _dot_tn(mix_ref[...], dyb)
        u1 = u1_ref[...]
        mu = jnp.mean(u1, axis=-1, keepdims=True)
        uc = u1 - mu
        rstd = lax.rsqrt(jnp.mean(uc * uc, axis=-1, keepdims=True) + EPS)
        uh = uc * rstd
        u2 = uh * lg_ref[...] + lb_ref[...]
        du2 = dmix[:, :C_CONV] * _silu_grad(u2, _sigmoid(u2))
        duh = du2 * lg_ref[...]
        du1 = rstd * (duh - jnp.mean(duh, axis=-1, keepdims=True) - uh * jnp.mean(duh * uh, axis=-1, keepdims=True))
        du1_ref[...] = du1
        small_ref[0:1, :] += jnp.sum(du2 * uh, axis=0, keepdims=True)
        small_ref[1:2, :] += jnp.sum(du2, axis=0, keepdims=True)
        small_ref[2:3, :] += jnp.sum(du1, axis=0, keepdims=True)
        for g in range(len(POOL_WINDOWS)):
            cols = slice(g * POOL_GROUP, (g + 1) * POOL_GROUP)
            dq = dmix[:, C_CONV + g * POOL_GROUP:C_CONV + (g + 1) * POOL_GROUP]
            pwb = pw_ref[g].astype(bf16)
            pg = po_ref[:, cols]
            mixed = _dot(pg, pwb)
            small_ref[3:4, cols] += jnp.sum(dq * mixed, axis=0, keepdims=True)
            dmixed = (dq * ps_ref[:, cols]).astype(bf16)
            dpw_ref[g] += _dot_tn(pg, dmixed)
            dpo_ref[:, cols] = _dot_nt(dmixed, pwb)

        @pl.when(i == n - 1)
        def _():
            dwo_ref[...] = awo_ref[...].astype(bf16)

    return pl.pallas_call(
        body, name="mix_bwd_local", grid=(n,),
        in_specs=[_row(0, (tm, D_MODEL)), _row(0, (tm, D_MODEL)), _row(0, (tm, C_CONV)), _row(0, (tm, C_POOL)),
                  _full((D_MODEL, D_MODEL)), _full((1, C_CONV)), _full((1, C_CONV)),
                  _full((4, POOL_GROUP, POOL_GROUP)), _full((1, C_POOL))],
        out_specs=[_row(0, (tm, C_CONV)), _row(0, (tm, C_POOL)), _full((D_MODEL, D_MODEL)),
                   _full((4, POOL_GROUP, POOL_GROUP)), _full((8, C_CONV))],
        out_shape=[jax.ShapeDtypeStruct((s, C_CONV), f32), jax.ShapeDtypeStruct((s, C_POOL), f32),
                   jax.ShapeDtypeStruct((D_MODEL, D_MODEL), bf16),
                   jax.ShapeDtypeStruct((4, POOL_GROUP, POOL_GROUP), f32), jax.ShapeDtypeStruct((8, C_CONV), f32)],
        scratch_shapes=[pltpu.VMEM((D_MODEL, D_MODEL), f32)],
        compiler_params=_cp("arbitrary"),
    )(dx1, mix, u1, pooled, w_out, ln_g, ln_b, pool_w, pool_scale)


def _in_bwd(du1, dpo, z, x, dx1, conv_w, g_mix, w_in_g, parts, small_parts):
    s = x.shape[0]
    tm = min(ROW_TILE, s)
    n = s // tm
    hb = tm // HALO
    last = s // HALO - 1
    nv = tm // 8
    assert nv >= SHIFT_PAD and nv % SHIFT_GROUPS == 0
    nc, ns = len(parts), len(small_parts)
    cs = _crossed(parts)

    def body(du_ref, dun_ref, dp_ref, dpn_ref, z_ref, zp_ref, x_ref, dx1_ref, cw_ref, g_ref, w_ref, *refs):
        outs = refs[nc + ns:]
        gx_ref, dw_ref, dcw_ref, small_ref = outs[:4]
        eu, ed, ep, ss, u0_ref, dz_ref, acc_ref, dcw_acc = outs[4 + nc + ns:12 + nc + ns]
        sems = outs[12 + nc + ns:]
        cross = _Cross(refs[:nc], outs[4:4 + nc], sems[0], sems[1])
        gather = _SmallGather(refs[nc:nc + ns], outs[4 + nc:4 + nc + ns], sems[2], sems[3])
        i = pl.program_id(0)
        pl.when(i == 0)(functools.partial(_handshake, cross.peers()))
        pl.when(i == 0)(cross.start)
        pl.when(i == 0)(gather.start)

        @pl.when(i == 0)
        def _():
            acc_ref[...] = jnp.zeros_like(acc_ref)
            dcw_acc[...] = jnp.zeros_like(dcw_acc)
            small_ref[...] = jnp.zeros_like(small_ref)

        keep_prev = (i > 0).astype(f32)
        keep_next = (i < n - 1).astype(f32)
        zp = zp_ref[...] * keep_prev
        u0_prev = zp[:, :C_CONV] * _sigmoid(zp[:, C_CONV:2 * C_CONV])
        u0_ref[...] = z_ref[:, :C_CONV] * _sigmoid(z_ref[:, C_CONV:2 * C_CONV])
        du_next = dun_ref[...] * keep_next
        for c, w_pool in enumerate(POOL_WINDOWS):
            lanes = slice(c * 128, (c + 1) * 128)
            eu[c, pl.ds(0, SHIFT_PAD, stride=8), :] = u0_prev[:, lanes]
            ed[c, pl.ds(nv * 8 + 7, SHIFT_PAD, stride=8), :] = du_next[:, lanes]
            ep[c, pl.ds(nv * 8 + 7, SHIFT_PAD, stride=8), :] = (
                dpn_ref[:, lanes] * keep_next / _pool_counts(HALO, w_pool, (i + 1) * tm))
            for j in range(8):
                rows = slice(j * nv, (j + 1) * nv)
                eu[c, pl.ds(SHIFT_PAD * 8 + j, nv, stride=8), :] = u0_ref[rows, lanes]
                ed[c, pl.ds(j, nv, stride=8), :] = du_ref[rows, lanes]
                ep[c, pl.ds(j, nv, stride=8), :] = dp_ref[rows, lanes] / _pool_counts(nv, w_pool, i * tm + j * nv)
                if j >= 1:
                    eu[c, pl.ds(j, SHIFT_PAD, stride=8), :] = u0_ref[j * nv - SHIFT_PAD:j * nv, lanes]
                if j <= 6:
                    edge = slice((j + 1) * nv, (j + 1) * nv + SHIFT_PAD)
                    ed[c, pl.ds(nv * 8 + j, SHIFT_PAD, stride=8), :] = du_ref[edge, lanes]
                    ep[c, pl.ds(nv * 8 + j, SHIFT_PAD, stride=8), :] = (
                        dp_ref[edge, lanes] / _pool_counts(SHIFT_PAD, w_pool, i * tm + (j + 1) * nv))
        for c, w_pool in enumerate(POOL_WINDOWS):
            lanes = slice(c * 128, (c + 1) * 128)
            b_lanes = slice(C_CONV + c * 128, C_CONV + (c + 1) * 128)
            v_lanes = slice(2 * C_CONV + c * 128, 2 * C_CONV + (c + 1) * 128)
            for v0 in range(0, nv, SHIFT_GROUPS):
                span = SHIFT_GROUPS * 8
                acc = jnp.zeros((span, 128), f32)
                for k in range(CONV_K):
                    acc = acc + cw_ref[pl.ds(k, 1), lanes] * ed[c, pl.ds((v0 + CONV_K - 1 - k) * 8, span), :]
                ss[0, v0 * 8:v0 * 8 + span, :] = acc
                acc = ep[c, pl.ds(v0 * 8, span), :]
                for d in range(1, w_pool):
                    acc = acc + ep[c, pl.ds((v0 + d) * 8, span), :]
                ss[1, v0 * 8:v0 * 8 + span, :] = acc
                d1 = ed[c, pl.ds(v0 * 8, span), :]
                for k in range(CONV_K):
                    prod = d1 * eu[c, pl.ds((SHIFT_PAD - (CONV_K - 1) + v0 + k) * 8, span), :]
                    fold = prod[0:8]
                    for r in range(8, span, 8):
                        fold = fold + prod[r:r + 8]
                    dcw_acc[k, :, lanes] += fold
            for j in range(8):
                rows = slice(j * nv, (j + 1) * nv)
                du0 = ss[0, pl.ds(j, nv, stride=8), :]
                av, sv = z_ref[rows, lanes], _sigmoid(z_ref[rows, b_lanes])
                dz_ref[rows, lanes] = (du0 * sv).astype(bf16)
                dz_ref[rows, b_lanes] = (du0 * av * sv * (1.0 - sv)).astype(bf16)
                dz_ref[rows, v_lanes] = (ss[1, pl.ds(j, nv, stride=8), :] - dp_ref[rows, lanes]).astype(bf16)
        h, xh, r = _rms_fwd(x_ref[...], g_ref[...])
        dz = dz_ref[...]
        acc_ref[...] += _dot_tn(h.astype(bf16), dz)
        dh = _dot_nt(dz[:, 0:W_IN_CHUNK], w_ref[0])
        for j in range(1, N_CHIPS):
            dh = dh + _dot_nt(dz[:, j * W_IN_CHUNK:(j + 1) * W_IN_CHUNK], w_ref[j])
        dxb, dg = _rms_bwd(dh, xh, r, g_ref[...])
        gx_ref[...] = dx1_ref[...] + dxb
        small_ref[0:1, :] += dg

        @pl.when(i == n - 1)
        def _():
            for j in range(N_CHIPS):
                dw_ref[j] = acc_ref[:, j * W_IN_CHUNK:(j + 1) * W_IN_CHUNK].astype(bf16)
            dcw_ref[...] = jnp.sum(dcw_acc[...], axis=1)

        pl.when(i == n - 1)(gather.finish)
        pl.when(i == n - 1)(cross.finish)

    nxt = lambda i: (jnp.minimum((i + 1) * hb, last), 0)
    res = pl.pallas_call(
        body, name="in_bwd", grid=(n,),
        in_specs=[_row(0, (tm, C_CONV)), pl.BlockSpec((HALO, C_CONV), nxt),
                  _row(0, (tm, C_POOL)), pl.BlockSpec((HALO, C_POOL), nxt),
                  _row(0, (tm, W_IN_COLS)),
                  pl.BlockSpec((HALO, W_IN_COLS), lambda i: (jnp.maximum(i * hb - 1, 0), 0)),
                  _row(0, (tm, D_MODEL)), _row(0, (tm, D_MODEL)), _full((HALO, C_CONV)), _full((1, D_MODEL)),
                  _full((N_CHIPS, D_MODEL, W_IN_CHUNK))] + cs["in_specs"] + [ANY] * ns,
        out_specs=[_row(0, (tm, D_MODEL)), _full((N_CHIPS, D_MODEL, W_IN_CHUNK)), _full((HALO, C_CONV)),
                   _full((8, D_MODEL))] + cs["out_specs"] + [ANY] * ns,
        out_shape=[jax.ShapeDtypeStruct((s, D_MODEL), f32), jax.ShapeDtypeStruct((N_CHIPS, D_MODEL, W_IN_CHUNK), bf16),
                   jax.ShapeDtypeStruct((HALO, C_CONV), f32), jax.ShapeDtypeStruct((8, D_MODEL), f32)] + cs["out_shape"]
        + _gathered_shapes(small_parts),
        scratch_shapes=[pltpu.VMEM((4, (SHIFT_PAD + nv) * 8, 128), f32), pltpu.VMEM((4, (nv + SHIFT_PAD) * 8, 128), f32),
                        pltpu.VMEM((4, (nv + SHIFT_PAD) * 8, 128), f32), pltpu.VMEM((2, tm, 128), f32),
                        pltpu.VMEM((tm, C_CONV), f32), pltpu.VMEM((tm, W_IN_COLS), bf16),
                        pltpu.VMEM((D_MODEL, W_IN_COLS), f32), pltpu.VMEM((HALO, 8, C_CONV), f32)] + cs["scratch"]
        + _SmallGather.scratch(ns),
        compiler_params=_cp("arbitrary", collective_id=6),
    )(du1, du1, dpo, dpo, z, z, x, dx1, conv_w, g_mix, w_in_g, *parts, *small_parts)
    return res[:4], res[4:4 + nc], res[4 + nc:]


def _sibling_handshake(x, y, c):
    barrier = pltpu.get_barrier_semaphore()
    pl.semaphore_signal(barrier, inc=1, device_id=(x, y, 1 - c), device_id_type=MESH)
    pl.semaphore_wait(barrier, 1)


def _pair_reduce(name, collective_id, grads):
    nk = len(grads)
    halves = [g.shape[1] // 2 for g in grads]

    def body(*refs):
        ins, outs, got, own = refs[:nk], refs[nk:2 * nk], refs[2 * nk:3 * nk], refs[3 * nk:4 * nk]
        send_sems, recv_sems, load_sems = refs[4 * nk:]
        x, y, c, _, _ = _place()
        _sibling_handshake(x, y, c)

        def half(k, core):
            return ins[k].at[:, pl.ds(core * halves[k], halves[k]), :]

        cps = [_remote(half(k, 1 - c), got[k], send_sems.at[k], recv_sems.at[k], (x, y, 1 - c)) for k in range(nk)]
        loads = [pltpu.make_async_copy(half(k, c), own[k], load_sems.at[k]) for k in range(nk)]
        for cp in cps + loads:
            cp.start()
        for k in range(nk):
            loads[k].wait()
            cps[k].wait_recv()
            outs[k][...] = (own[k][...].astype(f32) + got[k][...].astype(f32)).astype(bf16)
        for cp in cps:
            cp.wait_send()

    shapes = [(N_CHIPS, h, g.shape[2]) for g, h in zip(grads, halves)]
    return pl.pallas_call(
        body, name=f"pair_reduce_{name}", in_specs=[ANY] * nk, out_specs=[VMEM] * nk,
        out_shape=[jax.ShapeDtypeStruct(s, bf16) for s in shapes],
        scratch_shapes=[pltpu.VMEM(s, bf16) for s in shapes] * 2
        + [pltpu.SemaphoreType.DMA((nk,)), pltpu.SemaphoreType.DMA((nk,)), pltpu.SemaphoreType.DMA((nk,))],
        compiler_params=pltpu.CompilerParams(vmem_limit_bytes=VMEM_LIMIT, collective_id=collective_id),
    )(*grads)


class _Cross:
    def __init__(self, parts, landed, send_sems, recv_sems):
        self.parts, self.landed, self.send_sems, self.recv_sems = parts, landed, send_sems, recv_sems
        self.x, self.y, self.c, self.me, self.others = _place()

    def peers(self):
        return [(self.x, self.y, 1 - self.c)] + [(ox, oy, self.c) for ox, oy in self.others]

    def _copy(self, k, j, src_chunk, dst_slot):
        ox, oy = self.others[j]
        return _remote(self.parts[k].at[src_chunk], self.landed[k].at[dst_slot], self.send_sems.at[3 * k + j],
                       self.recv_sems.at[3 * k + j], (ox, oy, self.c))

    def _each(self):
        return [(k, j, 2 * self.others[j][0] + self.others[j][1]) for k in range(len(self.parts)) for j in range(3)]

    def start(self):
        for k, j, chip in self._each():
            self._copy(k, j, chip, self.me).start()

    def finish(self):
        for k, j, chip in self._each():
            self._copy(k, j, chip, chip).wait_recv()
        for k, j, chip in self._each():
            self._copy(k, j, chip, self.me).wait_send()

    @staticmethod
    def scratch(n):
        return [pltpu.SemaphoreType.DMA((3 * n,)), pltpu.SemaphoreType.DMA((3 * n,))]


class _CrossAll:
    def __init__(self, grads, landed, send_sems, recv_sems):
        self.grads, self.landed, self.send_sems, self.recv_sems = grads, landed, send_sems, recv_sems
        self.x, self.y, self.c, self.me, self.others = _place()
        self.dev = 2 * self.me + self.c

    def _piece(self, k, chip, half):
        rows = self.grads[k].shape[1] // 2
        return self.grads[k].at[chip, pl.ds(half * rows, rows), :]

    def _to_sibling(self, k):
        return _remote(self._piece(k, self.me, 1 - self.c), self.landed[k].at[self.dev], self.send_sems.at[7 * k],
                       self.recv_sems.at[7 * k], (self.x, self.y, 1 - self.c))

    def _to_chip(self, k, j, half):
        ox, oy = self.others[j]
        return _remote(self._piece(k, 2 * ox + oy, half), self.landed[k].at[self.dev],
                       self.send_sems.at[7 * k + 1 + 2 * j + half], self.recv_sems.at[7 * k + 1 + 2 * j + self.c],
                       (ox, oy, half))

    def _from(self, k, sem, slot):
        return _remote(self.landed[k].at[slot], self.landed[k].at[slot], self.send_sems.at[7 * k + sem],
                       self.recv_sems.at[7 * k + sem], (self.x, self.y, 1 - self.c))

    def start(self):
        for k in range(len(self.grads)):
            self._to_sibling(k).start()
            for j in range(3):
                for half in range(2):
                    self._to_chip(k, j, half).start()

    def finish(self):
        for k in range(len(self.grads)):
            self._from(k, 0, 2 * self.me + 1 - self.c).wait_recv()
            for j, (ox, oy) in enumerate(self.others):
                for core in range(2):
                    self._from(k, 1 + 2 * j + core, 4 * ox + 2 * oy + core).wait_recv()
        for k in range(len(self.grads)):
            self._to_sibling(k).wait_send()
            for j in range(3):
                for half in range(2):
                    self._to_chip(k, j, half).wait_send()

    @staticmethod
    def scratch(n):
        return [pltpu.SemaphoreType.DMA((7 * n,)), pltpu.SemaphoreType.DMA((7 * n,))]


def _rs_sum_chips(name, place, landed, part):
    _, half, cols = landed.shape

    def body(place_ref, l_ref, p_ref, o_ref):
        me = place_ref[0]
        own = p_ref[0].astype(f32)
        acc = jnp.where(me == 0, own, l_ref[0].astype(f32))
        for j in range(1, N_CHIPS):
            acc = acc + jnp.where(me == j, own, l_ref[j].astype(f32))
        o_ref[...] = acc

    return pl.pallas_call(
        body, name=f"rs_sum_chips_{name}",
        grid_spec=pltpu.PrefetchScalarGridSpec(
            num_scalar_prefetch=1, grid=(1,),
            in_specs=[pl.BlockSpec((N_CHIPS, half, cols), lambda t, place_ref: (0, 0, 0)),
                      pl.BlockSpec((1, half, cols), lambda t, place_ref: (place_ref[0], 0, 0))],
            out_specs=pl.BlockSpec((half, cols), lambda t, place_ref: (place_ref[1], 0))),
        out_shape=jax.ShapeDtypeStruct((2 * half, cols), f32),
        compiler_params=_cp("arbitrary"),
    )(place, landed, part)


def _grad_pair(shards):
    nk = len(shards)

    def body(*refs):
        outs = refs[nk:2 * nk]
        send_sems, recv_sems = refs[2 * nk:]
        x, y, c, _, _ = _place()
        _sibling_handshake(x, y, c)

        def half(k, core):
            h = outs[k].shape[0] // 2
            return outs[k].at[pl.ds(core * h, h), :]

        cps = [_remote(half(k, c), half(k, c), send_sems.at[k], recv_sems.at[k], (x, y, 1 - c)) for k in range(nk)]
        for cp in cps:
            cp.start()
        for k in range(nk):
            _remote(half(k, 1 - c), half(k, 1 - c), send_sems.at[k], recv_sems.at[k], (x, y, 1 - c)).wait_recv()
        for cp in cps:
            cp.wait_send()

    return pl.pallas_call(
        body, name="grad_pair", in_specs=[ANY] * nk, out_specs=[ANY] * nk,
        out_shape=[jax.ShapeDtypeStruct(a.shape, f32) for a in shards],
        input_output_aliases={k: k for k in range(nk)},
        scratch_shapes=[pltpu.SemaphoreType.DMA((nk,)), pltpu.SemaphoreType.DMA((nk,))],
        compiler_params=pltpu.CompilerParams(collective_id=2),
    )(*shards)


def _adam_math(w, g, m, v):
    m = ADAM_B1 * m + (1.0 - ADAM_B1) * g
    v = ADAM_B2 * v + (1.0 - ADAM_B2) * (g * g)
    m_hat = m / (1.0 - ADAM_B1 ** ADAM_STEP)
    v_hat = v / (1.0 - ADAM_B2 ** ADAM_STEP)
    delta = -ADAM_LR * (m_hat / (jnp.sqrt(v_hat) + ADAM_EPS) + ADAM_WD * w)
    return delta, m, v


def _adam(ws, gs, ms, vs):
    n = len(ws)
    steps = 2 * SHARD_STEPS

    def body(*refs):
        for k in range(n):
            w_ref, g_ref, m_ref, v_ref = (refs[j * n + k] for j in range(4))
            go_ref, d_ref, nm_ref, nv_ref = (refs[(4 + j) * n + k] for j in range(4))
            g = g_ref[...]
            go_ref[...] = g
            d_ref[...], nm_ref[...], nv_ref[...] = _adam_math(w_ref[...], g, m_ref[...], v_ref[...])

    specs = [_row(0, (w.shape[0] // steps, w.shape[1])) for w in ws]
    res = pl.pallas_call(
        body, name="adam_shards", grid=(steps,), in_specs=specs * 4, out_specs=specs * 4,
        out_shape=[jax.ShapeDtypeStruct(w.shape, f32) for w in ws] * 4,
        compiler_params=_cp("parallel"),
    )(*ws, *gs, *ms, *vs)
    return [tuple(res[j * n + k] for j in range(4)) for k in range(n)]


class _SmallGather:
    def __init__(self, parts, bufs, send_sems, recv_sems):
        self.parts, self.bufs, self.send_sems, self.recv_sems = parts, bufs, send_sems, recv_sems
        self.x, self.y, self.c, _, self.others = _place()
        self.sibling = (self.x, self.y, 1 - self.c)

    def _copy(self, a, k, block, to, src=None):
        slot = self.bufs[a].at[4 * block[0] + 2 * block[1] + block[2]]
        return _remote(slot if src is None else src, slot, self.send_sems.at[7 * a + k], self.recv_sems.at[7 * a + k],
                       to)

    def _first(self, a):
        me = (self.x, self.y, self.c)
        return [self._copy(a, 0, me, self.sibling, src=self.parts[a])] + [
            self._copy(a, 1 + j, me, (*chip, self.c), src=self.parts[a]) for j, chip in enumerate(self.others)]

    def _passed(self, a):
        return [self._copy(a, 4 + j, (*chip, self.c), self.sibling) for j, chip in enumerate(self.others)]

    @staticmethod
    def scratch(n):
        return [pltpu.SemaphoreType.DMA((7 * n,)), pltpu.SemaphoreType.DMA((7 * n,))]

    def start(self):
        for a in range(len(self.parts)):
            for cp in self._first(a):
                cp.start()

    def finish(self):
        sent = []
        for a in range(len(self.parts)):
            passed = self._passed(a)
            for j, chip in enumerate(self.others):
                self._copy(a, 1 + j, (*chip, self.c), self.sibling).wait_recv()
                passed[j].start()
            sent += self._first(a) + passed
        for a in range(len(self.parts)):
            self._copy(a, 0, self.sibling, self.sibling).wait_recv()
            for j, chip in enumerate(self.others):
                self._copy(a, 4 + j, (*chip, 1 - self.c), self.sibling).wait_recv()
        for cp in sent:
            cp.wait_send()


def _gathered_shapes(parts):
    return [jax.ShapeDtypeStruct((N_DEV,) + p.shape, p.dtype) for p in parts]


def _rs_sum_group(name, place, landed, parts, landed_all, grads_all, cross_parts, small_parts):
    nk, na, nx, ns = len(landed), len(landed_all), len(cross_parts), len(small_parts)
    dims = [a.shape[1:] for a in landed]
    dims_all = [a.shape[1:] for a in landed_all]

    def body(place_ref, *refs):
        take = iter(refs)
        l_refs, p_refs, la_refs, ga_refs, x_refs, sp_refs, o_refs, oa_refs, xl_refs, sbufs = (
            [next(take) for _ in range(cnt)] for cnt in (nk, nk, na, na, nx, ns, nk, na, nx, ns))
        sems = list(take)
        cross = _Cross(x_refs, xl_refs, sems[0], sems[1])
        small = _SmallGather(sp_refs, sbufs, sems[2], sems[3])
        t = pl.program_id(0)

        @pl.when(t == 0)
        def _():
            _handshake(cross.peers())
            cross.start()
            small.start()

        me = place_ref[0]
        for l_ref, p_ref, o_ref in zip(l_refs, p_refs, o_refs):
            own = p_ref[0].astype(f32)
            acc = jnp.where(me == 0, own, l_ref[0].astype(f32))
            for j in range(1, N_CHIPS):
                acc = acc + jnp.where(me == j, own, l_ref[j].astype(f32))
            o_ref[...] = acc
        dev = 2 * me + place_ref[1]
        for l_ref, g_ref, o_ref in zip(la_refs, ga_refs, oa_refs):
            own = g_ref[0].astype(f32)
            acc = jnp.where(dev == 0, own, l_ref[0].astype(f32))
            for d in range(1, N_DEV):
                acc = acc + jnp.where(dev == d, own, l_ref[d].astype(f32))
            o_ref[...] = acc

        @pl.when(t == 1)
        def _():
            small.finish()
            cross.finish()

    def halves(h, c, lead, index):
        return pl.BlockSpec((lead, h // 2, c) if lead else (h // 2, c), index)

    in_specs = [halves(h, c, N_CHIPS, lambda t, pr: (0, t, 0)) for h, c in dims]
    in_specs += [halves(h, c, 1, lambda t, pr: (pr[0], t, 0)) for h, c in dims]
    in_specs += [halves(h, c, N_DEV, lambda t, pr: (0, t, 0)) for h, c in dims_all]
    in_specs += [halves(h, c, 1, lambda t, pr: (pr[0], 2 * pr[1] + t, 0)) for h, c in dims_all]
    in_specs += [ANY] * (nx + ns)
    out_specs = [halves(h, c, 0, lambda t, pr: (2 * pr[1] + t, 0)) for h, c in dims + dims_all] + [ANY] * (nx + ns)
    out_shape = [jax.ShapeDtypeStruct((2 * h, c), f32) for h, c in dims + dims_all]
    out_shape += [jax.ShapeDtypeStruct(a.shape, a.dtype) for a in cross_parts] + _gathered_shapes(small_parts)
    res = pl.pallas_call(
        body, name=f"rs_sum_{name}",
        grid_spec=pltpu.PrefetchScalarGridSpec(
            num_scalar_prefetch=1, grid=(2,), in_specs=in_specs, out_specs=out_specs,
            scratch_shapes=_Cross.scratch(nx) + _SmallGather.scratch(ns)),
        out_shape=out_shape, compiler_params=_cp("arbitrary", collective_id=7),
    )(place, *landed, *parts, *landed_all, *grads_all, *cross_parts, *small_parts)
    return res[:nk], res[nk:nk + na], res[nk + na:nk + na + nx], res[nk + na + nx:]


SMALL_PARAMS = ("g_ple_gate", "g_ple_post", "g_final", "g_ffn", "ln_g", "ln_b", "conv_b", "pool_scale", "pool_w",
                "conv_w", "g_mix")
SMALL_ROWS = {"g_ple_gate": (0, 0), "g_ple_post": (0, 1), "g_final": (0, 2), "g_ffn": (1, 0), "ln_g": (2, 0),
              "ln_b": (2, 1), "conv_b": (2, 2), "pool_scale": (2, 3), "g_mix": (5, 0)}
LOSS_ROW = (0, 3)


def _small_adam(place, gathered, parts, params):
    nb, names = len(parts), SMALL_PARAMS
    flat = [a for nm in names for a in params[nm]]

    def body(place_ref, *refs):
        b_refs, p_refs = refs[:nb], refs[nb:2 * nb]
        w_refs = refs[2 * nb:2 * nb + 3 * len(names)]
        outs = refs[2 * nb + 3 * len(names):]
        loss_ref, o_refs, cw_sum = outs[0], outs[1:1 + 4 * len(names)], outs[1 + 4 * len(names)]
        chip = place_ref[0]
        me = 2 * chip + place_ref[1]

        def total(blk, idx):
            own = p_refs[blk][idx]
            g = jnp.where(me == 0, own, b_refs[blk][(0,) + idx])
            for d in range(1, N_DEV):
                g = g + jnp.where(me == d, own, b_refs[blk][(d,) + idx])
            return g

        everything = (slice(None), slice(None))
        loss_ref[...] = total(LOSS_ROW[0], (pl.ds(LOSS_ROW[1], 1), pl.ds(0, 128)))
        d_cw = total(4, everything)
        mine = jnp.where(chip == 0, d_cw[:, 0:128], 0.0)
        for j in range(1, N_CHIPS):
            mine = mine + jnp.where(chip == j, d_cw[:, j * 128:(j + 1) * 128], 0.0)
        cw_sum[...] = mine
        for k, nm in enumerate(names):
            w_ref, m_ref, v_ref = w_refs[3 * k:3 * k + 3]
            g_ref, d_ref, nm_ref, nv_ref = o_refs[4 * k:4 * k + 4]
            if nm == "pool_w":
                g = total(3, everything + (slice(None),))
            elif nm == "conv_w":
                g = cw_sum[pl.ds(0, CONV_K), :]
            else:
                blk, row = SMALL_ROWS[nm]
                g = total(blk, (pl.ds(row, 1), slice(None)))
            g_ref[...] = g
            d_ref[...], nm_ref[...], nv_ref[...] = _adam_math(w_ref[...], g, m_ref[...], v_ref[...])

    whole = lambda a: pl.BlockSpec(a.shape, lambda t, pr: (0,) * a.ndim)
    out_shape = [jax.ShapeDtypeStruct((1, 128), f32)]
    out_shape += [jax.ShapeDtypeStruct(params[nm][0].shape, f32) for nm in names for _ in range(4)]
    res = pl.pallas_call(
        body, name="small_adam",
        grid_spec=pltpu.PrefetchScalarGridSpec(
            num_scalar_prefetch=1, grid=(1,),
            in_specs=[whole(a) for a in list(gathered) + list(parts) + flat],
            out_specs=[whole(s) for s in out_shape], scratch_shapes=[pltpu.VMEM((HALO, 128), f32)]),
        out_shape=out_shape, compiler_params=_cp("arbitrary"),
    )(place, *gathered, *parts, *flat)
    return res[0], {nm: res[1 + 4 * k:5 + 4 * k] for k, nm in enumerate(names)}


def _pad_rows(a, rows):
    return jnp.concatenate([a, jnp.zeros((rows - a.shape[0],) + a.shape[1:], a.dtype)], axis=0)


def kernel(x, p, g_mix, w_in, conv_w, conv_b, ln_g, ln_b, pool_w, pool_scale, w_out, g_ffn, w_gate_up, w_down, g_ple_gate, w_ple_gate, w_ple_up, g_ple_post, g_final, loss_target, m_g_mix, m_w_in, m_conv_w, m_conv_b, m_ln_g, m_ln_b, m_pool_w, m_pool_scale, m_w_out, m_g_ffn, m_w_gate_up, m_w_down, m_g_ple_gate, m_w_ple_gate, m_w_ple_up, m_g_ple_post, m_g_final, v_g_mix, v_w_in, v_conv_w, v_conv_b, v_ln_g, v_ln_b, v_pool_w, v_pool_scale, v_w_out, v_g_ffn, v_w_gate_up, v_w_down, v_g_ple_gate, v_w_ple_gate, v_w_ple_up, v_g_ple_post, v_g_final):
    seq = x.shape[1]
    me = 2 * lax.axis_index("x") + lax.axis_index("y")
    chip = me.astype(jnp.int32).reshape(1)
    core = lax.axis_index("c").astype(jnp.int32).reshape(1)
    place = jnp.concatenate([chip, core])
    xs, ps, ts = x.reshape(seq, D_MODEL), p.reshape(seq, D_PLE), loss_target.reshape(seq, D_MODEL)

    big = [w_in[0], w_gate_up[0], w_out[0], w_down[0], w_ple_gate[0], w_ple_up[0]]
    big_m = [m_w_in[0], m_w_gate_up[0], m_w_out[0], m_w_down[0], m_w_ple_gate[0], m_w_ple_up[0]]
    big_v = [v_w_in[0], v_w_gate_up[0], v_w_out[0], v_w_down[0], v_w_ple_gate[0], v_w_ple_up[0]]
    b_in, b_gu, b_out, b_down, b_pg, b_pu, b_cw = _cast_into_slots(
        chip, big + [_pad_rows(conv_w[0], HALO)], [bf16] * len(big) + [f32])
    xi, yi = lax.axis_index("x"), lax.axis_index("y")
    order = jnp.stack([me, 2 * (1 - xi) + yi, 2 * xi + 1 - yi, 2 * (1 - xi) + 1 - yi]).astype(jnp.int32)

    z, (w_in_g, cw_g, w_out_g) = _mix_in(xs, g_mix, order, [b_in, b_cw, b_out])
    conv_w_f = cw_g.transpose(1, 0, 2).reshape(HALO, C_CONV)
    w_out_f = w_out_g.reshape(D_MODEL, D_MODEL)
    (x1, mix, u1, pooled, h2), (w_gu_g,) = _conv_pool_out(z, xs, conv_w_f, conv_b, ln_g, ln_b, pool_w[0], pool_scale,
                                                          w_out_f, g_ffn, [b_gu])
    (gu, ffn_f), (w_down_g, w_pg_g, w_pu_g) = _ffn_up(h2, w_gu_g, [b_down, b_pg, b_pu])
    w_down_f = w_down_g.reshape(D_FF, D_MODEL)
    x2 = _ffn_down(x1, ffn_f, w_down_f)
    w_pg_f = w_pg_g.reshape(D_MODEL, D_MODEL)
    dx2, d_w_pg, d_w_pu, small_ple = _ple_loss(x2, ps, ts, g_ple_gate, g_ple_post, g_final.reshape(1, D_MODEL),
                                               w_pg_f, w_pu_g)
    d_w_down = _ffn_bwd_dw_down(ffn_f, dx2)
    grads_a = [d_w_down.reshape(N_CHIPS, -1, D_MODEL), d_w_pg.reshape(N_CHIPS, -1, D_MODEL), d_w_pu]
    (dx1, dgu, small_ffn), landed_a = _ffn_bwd_dx(dx2, x1, gu, g_ffn, w_gu_g, w_down_f, grads_a)
    d_w_gu = _ffn_bwd_dw_gu(h2, dgu)
    du1, dpo, d_w_out, d_pool_w, small_mix = _mix_bwd_local(dx1, mix, u1, pooled, w_out_f, ln_g, ln_b, pool_w[0],
                                                             pool_scale)
    parts_b = _pair_reduce("b", 0, [d_w_gu, d_w_out.reshape(N_CHIPS, -1, D_MODEL)])
    small_0 = [small_ple, small_ffn, small_mix, d_pool_w]
    (grad_x, d_w_in, d_conv_w, small_in), landed_b, small_all_0 = _in_bwd(du1, dpo, z, xs, dx1, conv_w_f, g_mix,
                                                                            w_in_g, parts_b, small_0)
    parts_c = _pair_reduce("c", 1, [d_w_in])
    small_1 = [d_conv_w, small_in]
    (h_gu, h_out), (h_down, h_pg, h_pu), landed_c, small_all_1 = _rs_sum_group(
        "ab", place, landed_b, parts_b, landed_a, grads_a, parts_c, small_1)
    h_in = _rs_sum_chips("w_in", place, landed_c[0], parts_c[0])
    big_g = _grad_pair([h_in, h_gu, h_out, h_down, h_pg, h_pu])
    big_upd = _adam(big, big_g, big_m, big_v)

    p3 = lambda w, m, v: (w, m, v)
    row = lambda a: a.reshape(1, D_MODEL)
    params = dict(
        g_ple_gate=p3(g_ple_gate, m_g_ple_gate, v_g_ple_gate), g_ple_post=p3(g_ple_post, m_g_ple_post, v_g_ple_post),
        g_final=p3(row(g_final), row(m_g_final), row(v_g_final)), g_ffn=p3(g_ffn, m_g_ffn, v_g_ffn),
        ln_g=p3(ln_g, m_ln_g, v_ln_g), ln_b=p3(ln_b, m_ln_b, v_ln_b), conv_b=p3(conv_b, m_conv_b, v_conv_b),
        pool_scale=p3(pool_scale, m_pool_scale, v_pool_scale), pool_w=p3(pool_w[0], m_pool_w[0], v_pool_w[0]),
        conv_w=p3(conv_w[0], m_conv_w[0], v_conv_w[0]), g_mix=p3(g_mix, m_g_mix, v_g_mix))
    loss, small = _small_adam(place, list(small_all_0) + list(small_all_1), small_0 + small_1, params)
    back = dict(g_final=lambda a: a.reshape(D_MODEL), pool_w=lambda a: a[None], conv_w=lambda a: a[None])

    names = ["g_mix", "w_in", "conv_w", "conv_b", "ln_g", "ln_b", "pool_w", "pool_scale", "w_out", "g_ffn",
             "w_gate_up", "w_down", "g_ple_gate", "w_ple_gate", "w_ple_up", "g_ple_post", "g_final"]
    big_at = {"w_in": 0, "w_gate_up": 1, "w_out": 2, "w_down": 3, "w_ple_gate": 4, "w_ple_up": 5}
    out = [loss[0, 0], grad_x.reshape(1, seq, D_MODEL)]
    for kind in range(4):
        for nm in names:
            if nm in big_at:
                out.append(big_upd[big_at[nm]][kind][None])
            else:
                out.append(back.get(nm, lambda a: a)(small[nm][kind]))
    return tuple(out)
```

```python
import functools

import jax
import jax.numpy as jnp
from jax import lax
from jax.experimental import pallas as pl
from jax.experimental.pallas import tpu as pltpu

f32, bf16 = jnp.float32, jnp.bfloat16

EPS = 1e-6
D_MODEL = 1024
C_CONV = 512
C_POOL = 512
POOL_WINDOWS = (2, 4, 8, 16)
POOL_GROUP = 128
CONV_K = 31
D_FF = 2816
D_PLE = 256
N_CHIPS = 4
N_DEV = 8
W_IN_COLS = 2 * C_CONV + C_POOL
W_IN_CHUNK = W_IN_COLS // N_CHIPS
FF_CHUNK = 2 * D_FF // N_CHIPS
PLE_CHUNK = D_MODEL // N_CHIPS
HALO = 32
ROW_TILE = 512
SHIFT_PAD = 32
SHIFT_GROUPS = 16
FF_SUB = (0, 512, 1024, FF_CHUNK)
VMEM_LIMIT = 56 * 1024 * 1024

ADAM_LR = 0.001
ADAM_B1 = 0.9
ADAM_B2 = 0.999
ADAM_EPS = 1e-08
ADAM_WD = 0.01
ADAM_STEP = 10

MESH = pl.DeviceIdType.MESH
ANY = pl.BlockSpec(memory_space=pl.ANY)
VMEM = pl.BlockSpec(memory_space=pltpu.VMEM)


def _cp(*sem, collective_id=None):
    return pltpu.CompilerParams(dimension_semantics=sem, vmem_limit_bytes=VMEM_LIMIT, collective_id=collective_id)


def _dot(a, b):
    return jnp.dot(a, b, preferred_element_type=f32)


def _dot_nt(a, b):
    return lax.dot_general(a, b, (((1,), (1,)), ((), ())), preferred_element_type=f32)


def _dot_tn(a, b):
    return lax.dot_general(a, b, (((0,), (0,)), ((), ())), preferred_element_type=f32)


def _sigmoid(v):
    return jax.nn.sigmoid(v)


def _rms_fwd(v, g):
    r = lax.rsqrt(jnp.mean(v * v, axis=-1, keepdims=True) + EPS)
    vh = v * r
    return vh * g, vh, r


def _rms_bwd(dy, vh, r, g):
    dvh = dy * g
    dv = r * (dvh - vh * jnp.mean(dvh * vh, axis=-1, keepdims=True))
    return dv, jnp.sum(dy * vh, axis=0, keepdims=True)


def _silu_grad(v, s):
    return s * (1.0 + v * (1.0 - s))


def _row(i, n):
    return pl.BlockSpec((n[0], n[1]), lambda *a: (a[i], 0))


def _full(shape):
    nd = len(shape)
    return pl.BlockSpec(shape, lambda *a: (0,) * nd)


def _place():
    x, y, c = lax.axis_index("x"), lax.axis_index("y"), lax.axis_index("c")
    others = [(1 - x, y), (x, 1 - y), (1 - x, 1 - y)]
    return x, y, c, 2 * x + y, others


def _handshake(peers):
    barrier = pltpu.get_barrier_semaphore()
    for peer in peers:
        pl.semaphore_signal(barrier, inc=1, device_id=peer, device_id_type=MESH)
    pl.semaphore_wait(barrier, len(peers))


def _remote(src, dst, send_sem, recv_sem, dev):
    return pltpu.make_async_remote_copy(src_ref=src, dst_ref=dst, send_sem=send_sem, recv_sem=recv_sem,
                                        device_id=dev, device_id_type=MESH)


SHARD_STEPS = 4


def _cast_into_slots(me, ws, dtypes):
    n = len(ws)

    def body(me_ref, *refs):
        for w_ref, o_ref, dtype in zip(refs[:n], refs[n:], dtypes):
            o_ref[0] = w_ref[...].astype(dtype)

    return pl.pallas_call(
        body, name="cast_shards",
        grid_spec=pltpu.PrefetchScalarGridSpec(
            num_scalar_prefetch=1, grid=(SHARD_STEPS,),
            in_specs=[pl.BlockSpec((w.shape[0] // SHARD_STEPS, w.shape[1]), lambda r, me_ref: (r, 0)) for w in ws],
            out_specs=[pl.BlockSpec((1, w.shape[0] // SHARD_STEPS, w.shape[1]), lambda r, me_ref: (me_ref[0], r, 0))
                       for w in ws]),
        out_shape=[jax.ShapeDtypeStruct((N_CHIPS,) + w.shape, dt) for w, dt in zip(ws, dtypes)],
        compiler_params=_cp("parallel"),
    )(me, *ws)


class _Gather:
    def __init__(self, bufs, send_sems, recv_sems):
        self.bufs, self.send_sems, self.recv_sems = bufs, send_sems, recv_sems
        self.x, self.y, self.c, self.me, self.others = _place()
        self.halves = [b.shape[1] // 2 for b in bufs]

    def _piece(self, k, chip, half):
        return self.bufs[k].at[chip, pl.ds(half * self.halves[k], self.halves[k]), :]

    def _ici(self, k, j, chip):
        ox, oy = self.others[j]
        piece = self._piece(k, chip, self.c)
        return _remote(piece, piece, self.send_sems.at[6 * k + j], self.recv_sems.at[6 * k + j], (ox, oy, self.c))

    def _relay(self, k):
        first = self.c == 0
        piece = self._piece(k, jnp.where(first, self.chip(0), self.chip(1)), self.c)
        to = (jnp.where(first, self.others[1][0], self.others[0][0]),
              jnp.where(first, self.others[1][1], self.others[0][1]), self.c)
        return _remote(piece, piece, self.send_sems.at[6 * k + 2], self.recv_sems.at[6 * k + 2], to)

    def _pair(self, k, j, half):
        ox, oy = self.others[j]
        piece = self._piece(k, 2 * ox + oy, half)
        return _remote(piece, piece, self.send_sems.at[6 * k + 3 + j], self.recv_sems.at[6 * k + 3 + j],
                       (self.x, self.y, 1 - self.c))

    def _each(self, ks=None):
        return [(k, j) for k in (range(len(self.bufs)) if ks is None else ks) for j in range(3)]

    def chip(self, j):
        ox, oy = self.others[j]
        return 2 * ox + oy

    def start(self):
        _handshake([(self.x, self.y, 1 - self.c)] + [(*self.others[j], self.c) for j in range(2)])
        for k in range(len(self.bufs)):
            for j in range(2):
                self._ici(k, j, self.me).start()

    def forward(self, pairs=None):
        for k, j in self._each() if pairs is None else pairs:
            self._ici(k, j, self.chip(j)).wait_recv()
            self._pair(k, j, self.c).start()
            if j < 2:
                pl.when(self.c == j)(self._relay(k).start)

    def landed(self, pairs):
        for k, j in pairs:
            self._pair(k, j, 1 - self.c).wait_recv()

    def finish(self, ks=None):
        self.landed(self._each(ks))
        for k in range(len(self.bufs)):
            for j in range(2):
                self._ici(k, j, self.me).wait_send()
            self._relay(k).wait_send()
            for j in range(3):
                self._pair(k, j, self.c).wait_send()

    @staticmethod
    def scratch(n):
        return [pltpu.SemaphoreType.DMA((6 * n,)), pltpu.SemaphoreType.DMA((6 * n,))]


def _carried(bufs):
    n = len(bufs)
    return dict(in_specs=[ANY] * n, out_specs=[ANY] * n,
                out_shape=[jax.ShapeDtypeStruct(b.shape, b.dtype) for b in bufs], scratch=_Gather.scratch(n))


def _mix_in(x, g_mix, order, carry):
    s = x.shape[0]
    tm = min(2 * ROW_TILE, s)
    n = s // tm
    nc = len(carry)
    cs = _carried(carry)

    def body(order_ref, x_ref, g_ref, *refs):
        z_ref = refs[nc]
        bufs = refs[nc + 1:2 * nc + 1]
        h_ref, w_ref, w_sem = refs[2 * nc + 1:2 * nc + 4]
        gather = _Gather(bufs, *refs[2 * nc + 4:])
        q, i = pl.program_id(0), pl.program_id(1)
        first = i == 0
        pl.when(jnp.logical_and(q == 0, first))(gather.start)
        for j in range(3):

            @pl.when(jnp.logical_and(q == j + 1, first))
            def _():
                gather.forward([(0, j)])
                gather.landed([(0, j)])
                if j == 1:
                    gather.forward([(k, jj) for k in range(1, nc) for jj in range(2)])

        @pl.when(first)
        def _():
            load = pltpu.make_async_copy(bufs[0].at[order_ref[q]], w_ref, w_sem)
            load.start()
            load.wait()

        @pl.when(q == 0)
        def _():
            h, _, _ = _rms_fwd(x_ref[...], g_ref[...])
            h_ref[i] = h.astype(bf16)

        z_ref[...] = _dot(h_ref[i], w_ref[...])

        @pl.when(jnp.logical_and(q == N_CHIPS - 1, i == n - 1))
        def _():
            gather.forward([(k, 2) for k in range(1, nc)])
            gather.finish(range(1, nc))

    res = pl.pallas_call(
        body, name="mix_in",
        grid_spec=pltpu.PrefetchScalarGridSpec(
            num_scalar_prefetch=1, grid=(N_CHIPS, n),
            in_specs=[pl.BlockSpec((tm, D_MODEL), lambda q, i, order_ref: (jnp.where(q == 0, i, 0), 0)),
                      pl.BlockSpec((1, D_MODEL), lambda q, i, order_ref: (0, 0))] + cs["in_specs"],
            out_specs=[pl.BlockSpec((tm, W_IN_CHUNK), lambda q, i, order_ref: (i, order_ref[q]))] + cs["out_specs"],
            scratch_shapes=[pltpu.VMEM((n, tm, D_MODEL), bf16), pltpu.VMEM((D_MODEL, W_IN_CHUNK), bf16),
                            pltpu.SemaphoreType.DMA(())] + cs["scratch"]),
        out_shape=[jax.ShapeDtypeStruct((s, W_IN_COLS), f32)] + cs["out_shape"],
        input_output_aliases={3 + k: 1 + k for k in range(nc)},
        compiler_params=_cp("arbitrary", "arbitrary", collective_id=3),
    )(order, x, g_mix, *carry)
    return res[0], res[1:]


def _pool_counts(tm, w, first_row):
    t1 = (lax.broadcasted_iota(jnp.int32, (tm, 1), 0) + first_row + 1).astype(f32)
    return jnp.minimum(t1, float(w))


def _conv_pool_out(z, x, conv_w, conv_b, ln_g, ln_b, pool_w, pool_scale, w_out, g_ffn, carry):
    s = x.shape[0]
    tm = min(ROW_TILE, s)
    n = s // tm
    hb = tm // HALO
    nv = tm // 8
    assert nv >= SHIFT_PAD and nv % SHIFT_GROUPS == 0
    nc = len(carry)
    cs = _carried(carry)

    def body(z_ref, zp_ref, x_ref, cw_ref, cb_ref, lg_ref, lb_ref, pw_ref, ps_ref, wo_ref, gf_ref, *refs):
        x1_ref, mix_ref, u1_ref, pooled_ref, h2_ref = refs[nc:nc + 5]
        eu, ev, ss, u0_ref = refs[2 * nc + 5:2 * nc + 9]
        gather = _Gather(refs[nc + 5:2 * nc + 5], *refs[2 * nc + 9:])
        i = pl.program_id(0)
        pl.when(i == 0)(gather.start)
        for k in range(nc):
            pl.when(i == min(3 * n // 4 + k, n - 1))(functools.partial(gather.forward, [(k, 0), (k, 1)]))
        keep = (i > 0).astype(f32)
        zp = zp_ref[...] * keep
        u0_prev = zp[:, :C_CONV] * _sigmoid(zp[:, C_CONV:2 * C_CONV])
        u0_ref[...] = z_ref[:, :C_CONV] * _sigmoid(z_ref[:, C_CONV:2 * C_CONV])
        for c, w_pool in enumerate(POOL_WINDOWS):
            lanes = slice(c * 128, (c + 1) * 128)
            v_lanes = slice(2 * C_CONV + c * 128, 2 * C_CONV + (c + 1) * 128)
            eu[c, pl.ds(0, SHIFT_PAD, stride=8), :] = u0_prev[:, lanes]
            ev[c, pl.ds(0, SHIFT_PAD, stride=8), :] = zp[:, v_lanes]
            for j in range(8):
                rows = slice(j * nv, (j + 1) * nv)
                eu[c, pl.ds(SHIFT_PAD * 8 + j, nv, stride=8), :] = u0_ref[rows, lanes]
                ev[c, pl.ds(SHIFT_PAD * 8 + j, nv, stride=8), :] = z_ref[rows, v_lanes]
                if j >= 1:
                    edge = slice(j * nv - SHIFT_PAD, j * nv)
                    eu[c, pl.ds(j, SHIFT_PAD, stride=8), :] = u0_ref[edge, lanes]
                    ev[c, pl.ds(j, SHIFT_PAD, stride=8), :] = z_ref[edge, v_lanes]
            for v0 in range(0, nv, SHIFT_GROUPS):
                span = SHIFT_GROUPS * 8
                acc = jnp.zeros((span, 128), f32) + cb_ref[:, lanes]
                for k in range(CONV_K):
                    acc = acc + cw_ref[pl.ds(k, 1), lanes] * eu[c, pl.ds((SHIFT_PAD - (CONV_K - 1) + v0 + k) * 8, span), :]
                ss[0, v0 * 8:v0 * 8 + span, :] = acc
                acc = ev[c, pl.ds((SHIFT_PAD + v0) * 8, span), :]
                for d in range(1, w_pool):
                    acc = acc + ev[c, pl.ds((SHIFT_PAD + v0 - d) * 8, span), :]
                ss[1, v0 * 8:v0 * 8 + span, :] = acc
            for j in range(8):
                rows = slice(j * nv, (j + 1) * nv)
                u1_ref[rows, lanes] = ss[0, pl.ds(j, nv, stride=8), :]
                mean = ss[1, pl.ds(j, nv, stride=8), :] / _pool_counts(nv, w_pool, i * tm + j * nv)
                pooled_ref[rows, lanes] = (mean - z_ref[rows, v_lanes]).astype(bf16)
        u1 = u1_ref[...]
        mu = jnp.mean(u1, axis=-1, keepdims=True)
        uc = u1 - mu
        rstd = lax.rsqrt(jnp.mean(uc * uc, axis=-1, keepdims=True) + EPS)
        u2 = uc * rstd * lg_ref[...] + lb_ref[...]
        mix_ref[:, :C_CONV] = (u2 * _sigmoid(u2)).astype(bf16)
        for g in range(len(POOL_WINDOWS)):
            cols = slice(g * POOL_GROUP, (g + 1) * POOL_GROUP)
            mixed = _dot(pooled_ref[:, cols], pw_ref[g].astype(bf16))
            mix_ref[:, C_CONV + g * POOL_GROUP:C_CONV + (g + 1) * POOL_GROUP] = (mixed * ps_ref[:, cols]).astype(bf16)
        x1 = x_ref[...] + _dot(mix_ref[...], wo_ref[...])
        x1_ref[...] = x1
        h2_ref[...] = _rms_fwd(x1, gf_ref[...])[0].astype(bf16)
        @pl.when(i == n - 1)
        def _():
            gather.forward([(k, 2) for k in range(nc)])
            gather.finish()

    res = pl.pallas_call(
        body, name="conv_pool_out", grid=(n,),
        in_specs=[_row(0, (tm, W_IN_COLS)),
                  pl.BlockSpec((HALO, W_IN_COLS), lambda i: (jnp.maximum(i * hb - 1, 0), 0)),
                  _row(0, (tm, D_MODEL)), _full((HALO, C_CONV)), _full((1, C_CONV)), _full((1, C_CONV)),
                  _full((1, C_CONV)), _full((4, POOL_GROUP, POOL_GROUP)), _full((1, C_POOL)),
                  _full((D_MODEL, D_MODEL)), _full((1, D_MODEL))] + cs["in_specs"],
        out_specs=[_row(0, (tm, D_MODEL)), _row(0, (tm, D_MODEL)), _row(0, (tm, C_CONV)), _row(0, (tm, C_POOL)),
                   _row(0, (tm, D_MODEL))] + cs["out_specs"],
        out_shape=[jax.ShapeDtypeStruct((s, D_MODEL), f32), jax.ShapeDtypeStruct((s, D_MODEL), bf16),
                   jax.ShapeDtypeStruct((s, C_CONV), f32), jax.ShapeDtypeStruct((s, C_POOL), bf16),
                   jax.ShapeDtypeStruct((s, D_MODEL), bf16)] + cs["out_shape"],
        input_output_aliases={11 + k: 5 + k for k in range(nc)},
        scratch_shapes=[pltpu.VMEM((4, (SHIFT_PAD + nv) * 8, 128), f32), pltpu.VMEM((4, (SHIFT_PAD + nv) * 8, 128), f32),
                        pltpu.VMEM((2, tm, 128), f32), pltpu.VMEM((tm, C_CONV), f32)] + cs["scratch"],
        compiler_params=_cp("arbitrary", collective_id=4),
    )(z, z, x, conv_w, conv_b, ln_g, ln_b, pool_w, pool_scale, w_out, g_ffn, *carry)
    return res[:5], res[5:]


def _ffn_up(h2, w_gu_g, carry):
    s = h2.shape[0]
    tm = min(2 * ROW_TILE, s)
    n = s // tm
    nc = len(carry)
    cs = _carried(carry)

    def body(h2_ref, wg_ref, wu_ref, *refs):
        gu_ref, f_ref = refs[nc:nc + 2]
        gather = _Gather(refs[nc + 2:2 * nc + 2], *refs[2 * nc + 2:])
        c, i = pl.program_id(0), pl.program_id(1)
        pl.when(jnp.logical_and(i == 0, c == 0))(gather.start)
        direct = [(k, j) for k in range(nc) for j in range(2)]
        pl.when(jnp.logical_and(i == n - 1, c == 0))(functools.partial(gather.forward, direct))
        pl.when(jnp.logical_and(i == n // 2, c == 1))(functools.partial(gather.forward, [(k, 2) for k in range(nc)]))
        h = h2_ref[...]
        for lo, hi in zip(FF_SUB[:-1], FF_SUB[1:]):
            gate = _dot(h, wg_ref[0, :, lo:hi])
            up = _dot(h, wu_ref[0, :, lo:hi])
            gu_ref[0, :, lo:hi] = gate.astype(bf16)
            gu_ref[1, :, lo:hi] = up.astype(bf16)
            f_ref[:, lo:hi] = (gate * _sigmoid(gate) * up).astype(bf16)
        pl.when(jnp.logical_and(i == n - 1, c == 1))(gather.finish)

    res = pl.pallas_call(
        body, name="ffn_up", grid=(2, n),
        in_specs=[_row(1, (tm, D_MODEL)),
                  pl.BlockSpec((1, D_MODEL, FF_CHUNK), lambda c, i: (c, 0, 0)),
                  pl.BlockSpec((1, D_MODEL, FF_CHUNK), lambda c, i: (2 + c, 0, 0))] + cs["in_specs"],
        out_specs=[pl.BlockSpec((2, tm, FF_CHUNK), lambda c, i: (0, i, c)),
                   pl.BlockSpec((tm, FF_CHUNK), lambda c, i: (i, c))] + cs["out_specs"],
        out_shape=[jax.ShapeDtypeStruct((2, s, D_FF), bf16), jax.ShapeDtypeStruct((s, D_FF), bf16)] + cs["out_shape"],
        input_output_aliases={3 + k: 2 + k for k in range(nc)},
        scratch_shapes=cs["scratch"],
        compiler_params=_cp("arbitrary", "arbitrary", collective_id=5),
    )(h2, w_gu_g, w_gu_g, *carry)
    return res[:2], res[2:]


def _ffn_down(x1, f, w_down):
    s = x1.shape[0]
    tm = min(2 * ROW_TILE, s)

    def body(x1_ref, f_ref, wd_ref, x2_ref):
        x2_ref[...] = x1_ref[...] + _dot(f_ref[...], wd_ref[...])

    return pl.pallas_call(
        body, name="ffn_down", grid=(s // tm,),
        in_specs=[_row(0, (tm, D_MODEL)), _row(0, (tm, D_FF)), _full((D_FF, D_MODEL))],
        out_specs=_row(0, (tm, D_MODEL)), out_shape=jax.ShapeDtypeStruct((s, D_MODEL), f32),
        compiler_params=_cp("parallel"),
    )(x1, f, w_down)


def _ple_loss(x2, p, target, g_pg, g_post, g_final, w_pg, w_pu_g):
    s = x2.shape[0]
    tm = min(ROW_TILE, s)
    n = s // tm

    def body(x2_ref, p_ref, t_ref, gpg_ref, gpo_ref, gf_ref, wpg_ref, wpu_ref,
             dx2_ref, dwpg_ref, dwpu_ref, small_ref, apg_ref, apu_ref):
        i = pl.program_id(0)

        @pl.when(i == 0)
        def _():
            apg_ref[...] = jnp.zeros_like(apg_ref)
            apu_ref[...] = jnp.zeros_like(apu_ref)
            small_ref[...] = jnp.zeros_like(small_ref)

        x2 = x2_ref[...]
        h3, x2h, r2 = _rms_fwd(x2, gpg_ref[...])
        h3b = h3.astype(bf16)
        gate = _sigmoid(_dot(h3b, wpg_ref[...]))
        pb = p_ref[...].astype(bf16)
        pe = jnp.concatenate([_dot(pb, wpu_ref[j]) for j in range(N_CHIPS)], axis=-1)
        e, peh, rp = _rms_fwd(pe, gpo_ref[...])
        x3 = x2 + gate * e
        y, x3h, r3 = _rms_fwd(x3, gf_ref[...])
        d = y - t_ref[...]
        loss = 0.5 * jnp.sum(jnp.sum(d * d, axis=-1, keepdims=True) * (1.0 / D_MODEL), axis=0, keepdims=True)
        dx3, dgf = _rms_bwd(d * (1.0 / D_MODEL), x3h, r3, gf_ref[...])
        dpe, dgpo = _rms_bwd(dx3 * gate, peh, rp, gpo_ref[...])
        dgl = (dx3 * e * gate * (1.0 - gate)).astype(bf16)
        apg_ref[...] += _dot_tn(h3b, dgl)
        apu_ref[...] += _dot_tn(pb, dpe.astype(bf16))
        dh3 = _dot_nt(dgl, wpg_ref[...])
        dx2b, dgpg = _rms_bwd(dh3, x2h, r2, gpg_ref[...])
        dx2_ref[...] = dx3 + dx2b
        small_ref[0:1, :] += dgpg
        small_ref[1:2, :] += dgpo
        small_ref[2:3, :] += dgf
        small_ref[3:4, :] += jnp.broadcast_to(loss, (1, D_MODEL))

        @pl.when(i == n - 1)
        def _():
            dwpg_ref[...] = apg_ref[...].astype(bf16)
            for j in range(N_CHIPS):
                dwpu_ref[j] = apu_ref[:, j * PLE_CHUNK:(j + 1) * PLE_CHUNK].astype(bf16)

    return pl.pallas_call(
        body, name="ple_loss", grid=(n,),
        in_specs=[_row(0, (tm, D_MODEL)), _row(0, (tm, D_PLE)), _row(0, (tm, D_MODEL)),
                  _full((1, D_MODEL)), _full((1, D_MODEL)), _full((1, D_MODEL)),
                  _full((D_MODEL, D_MODEL)), _full((N_CHIPS, D_PLE, PLE_CHUNK))],
        out_specs=[_row(0, (tm, D_MODEL)), _full((D_MODEL, D_MODEL)), _full((N_CHIPS, D_PLE, PLE_CHUNK)),
                   _full((8, D_MODEL))],
        out_shape=[jax.ShapeDtypeStruct((s, D_MODEL), f32), jax.ShapeDtypeStruct((D_MODEL, D_MODEL), bf16),
                   jax.ShapeDtypeStruct((N_CHIPS, D_PLE, PLE_CHUNK), bf16), jax.ShapeDtypeStruct((8, D_MODEL), f32)],
        scratch_shapes=[pltpu.VMEM((D_MODEL, D_MODEL), f32), pltpu.VMEM((D_PLE, D_MODEL), f32)],
        compiler_params=_cp("arbitrary"),
    )(x2, p, target, g_pg, g_post, g_final, w_pg, w_pu_g)


def _crossed(parts):
    n = len(parts)
    return dict(in_specs=[ANY] * n, out_specs=[ANY] * n,
                out_shape=[jax.ShapeDtypeStruct(a.shape, a.dtype) for a in parts], scratch=_Cross.scratch(n))


def _ffn_bwd_dx(dx2, x1, gu, g_ffn, w_gu_g, w_down, grads):
    s = x1.shape[0]
    tm = min(ROW_TILE, s)
    n = s // tm
    nc = len(grads)
    cs = dict(in_specs=[ANY] * nc, out_specs=[ANY] * nc, scratch=_CrossAll.scratch(nc),
              out_shape=[jax.ShapeDtypeStruct((N_DEV, g.shape[1] // 2, g.shape[2]), g.dtype) for g in grads])

    def body(dx2_ref, x1_ref, gu_ref, g_ref, wg_ref, wu_ref, wd_ref, *refs):
        dx1_ref, dgu_ref, small_ref = refs[nc:nc + 3]
        acc_ref = refs[2 * nc + 3]
        cross = _CrossAll(refs[:nc], refs[nc + 3:2 * nc + 3], *refs[2 * nc + 4:])
        i, c = pl.program_id(0), pl.program_id(1)
        everyone = [(cross.x, cross.y, 1 - cross.c)] + [(ox, oy, h) for ox, oy in cross.others for h in range(2)]
        pl.when(jnp.logical_and(i == 0, c == 0))(functools.partial(_handshake, everyone))
        pl.when(jnp.logical_and(i == 0, c == 0))(cross.start)

        @pl.when(jnp.logical_and(i == 0, c == 0))
        def _():
            small_ref[...] = jnp.zeros_like(small_ref)

        @pl.when(c == 0)
        def _():
            acc_ref[...] = jnp.zeros_like(acc_ref)

        dyb = dx2_ref[...].astype(bf16)
        for lo, hi in zip(FF_SUB[:-1], FF_SUB[1:]):
            df = _dot_nt(dyb, wd_ref[lo:hi, :])
            gate = gu_ref[0, :, lo:hi].astype(f32)
            up = gu_ref[1, :, lo:hi].astype(f32)
            sg = _sigmoid(gate)
            dgate = (df * up * _silu_grad(gate, sg)).astype(bf16)
            dup = (df * gate * sg).astype(bf16)
            dgu_ref[0, :, lo:hi] = dgate
            dgu_ref[1, :, lo:hi] = dup
            acc_ref[...] += _dot_nt(dgate, wg_ref[0, :, lo:hi]) + _dot_nt(dup, wu_ref[0, :, lo:hi])

        @pl.when(c == 1)
        def _():
            _, x1h, r1 = _rms_fwd(x1_ref[...], g_ref[...])
            dx1b, dg = _rms_bwd(acc_ref[...], x1h, r1, g_ref[...])
            dx1_ref[...] = dx2_ref[...] + dx1b
            small_ref[0:1, :] += dg

        pl.when(jnp.logical_and(i == n - 1, c == 1))(cross.finish)

    res = pl.pallas_call(
        body, name="ffn_bwd_dx", grid=(n, 2),
        in_specs=[_row(0, (tm, D_MODEL)), _row(0, (tm, D_MODEL)),
                  pl.BlockSpec((2, tm, FF_CHUNK), lambda i, c: (0, i, c)), _full((1, D_MODEL)),
                  pl.BlockSpec((1, D_MODEL, FF_CHUNK), lambda i, c: (c, 0, 0)),
                  pl.BlockSpec((1, D_MODEL, FF_CHUNK), lambda i, c: (2 + c, 0, 0)),
                  pl.BlockSpec((FF_CHUNK, D_MODEL), lambda i, c: (c, 0))] + cs["in_specs"],
        out_specs=[_row(0, (tm, D_MODEL)), pl.BlockSpec((2, tm, FF_CHUNK), lambda i, c: (0, i, c)),
                   _full((8, D_MODEL))] + cs["out_specs"],
        out_shape=[jax.ShapeDtypeStruct((s, D_MODEL), f32), jax.ShapeDtypeStruct((2, s, D_FF), bf16),
                   jax.ShapeDtypeStruct((8, D_MODEL), f32)] + cs["out_shape"],
        scratch_shapes=[pltpu.VMEM((tm, D_MODEL), f32)] + cs["scratch"],
        compiler_params=_cp("arbitrary", "arbitrary", collective_id=8),
    )(dx2, x1, gu, g_ffn, w_gu_g, w_gu_g, w_down, *grads)
    return res[:3], res[3:]


def _ffn_bwd_dw_gu(h2, dgu):
    s = h2.shape[0]
    ts = min(2 * ROW_TILE, s)
    n = s // ts

    def body(h_ref, d_ref, o_ref, acc_ref):
        t = pl.program_id(1)

        @pl.when(t == 0)
        def _():
            acc_ref[...] = jnp.zeros_like(acc_ref)

        acc_ref[...] += _dot_tn(h_ref[...], d_ref[0])

        @pl.when(t == n - 1)
        def _():
            o_ref[0] = acc_ref[...].astype(bf16)

    return pl.pallas_call(
        body, name="ffn_bwd_dw_gu", grid=(N_CHIPS, n),
        in_specs=[pl.BlockSpec((ts, D_MODEL), lambda j, t: (t, 0)),
                  pl.BlockSpec((1, ts, FF_CHUNK), lambda j, t: (j // 2, t, j % 2))],
        out_specs=pl.BlockSpec((1, D_MODEL, FF_CHUNK), lambda j, t: (j, 0, 0)),
        out_shape=jax.ShapeDtypeStruct((N_CHIPS, D_MODEL, FF_CHUNK), bf16),
        scratch_shapes=[pltpu.VMEM((D_MODEL, FF_CHUNK), f32)],
        compiler_params=_cp("parallel", "arbitrary"),
    )(h2, dgu)


def _ffn_bwd_dw_down(f, dx2):
    s = dx2.shape[0]
    ts = min(2 * ROW_TILE, s)
    n = s // ts

    def body(f_ref, d_ref, o_ref, acc_ref):
        t = pl.program_id(1)

        @pl.when(t == 0)
        def _():
            acc_ref[...] = jnp.zeros_like(acc_ref)

        acc_ref[...] += _dot_tn(f_ref[...], d_ref[...].astype(bf16))

        @pl.when(t == n - 1)
        def _():
            o_ref[...] = acc_ref[...].astype(bf16)

    return pl.pallas_call(
        body, name="ffn_bwd_dw_down", grid=(2, n),
        in_specs=[pl.BlockSpec((ts, FF_CHUNK), lambda c, t: (t, c)),
                  pl.BlockSpec((ts, D_MODEL), lambda c, t: (t, 0))],
        out_specs=pl.BlockSpec((FF_CHUNK, D_MODEL), lambda c, t: (c, 0)),
        out_shape=jax.ShapeDtypeStruct((D_FF, D_MODEL), bf16),
        scratch_shapes=[pltpu.VMEM((FF_CHUNK, D_MODEL), f32)],
        compiler_params=_cp("parallel", "arbitrary"),
    )(f, dx2)


def _mix_bwd_local(dx1, mix, u1, pooled, w_out, ln_g, ln_b, pool_w, pool_scale):
    s = dx1.shape[0]
    tm = min(ROW_TILE, s)
    n = s // tm

    def body(dx1_ref, mix_ref, u1_ref, po_ref, wo_ref, lg_ref, lb_ref, pw_ref, ps_ref,
             du1_ref, dpo_ref, dwo_ref, dpw_ref, small_ref, awo_ref):
        i = pl.program_id(0)

        @pl.when(i == 0)
        def _():
            awo_ref[...] = jnp.zeros_like(awo_ref)
            dpw_ref[...] = jnp.zeros_like(dpw_ref)
            small_ref[...] = jnp.zeros_like(small_ref)

        dyb = dx1_ref[...].astype(bf16)
        dmix = _dot_nt(dyb, wo_ref[...])
        awo_ref[...] += _dot_tn(mix_ref[...], dyb)
        u1 = u1_ref[...]
        mu = jnp.mean(u1, axis=-1, keepdims=True)
        uc = u1 - mu
        rstd = lax.rsqrt(jnp.mean(uc * uc, axis=-1, keepdims=True) + EPS)
        uh = uc * rstd
        u2 = uh * lg_ref[...] + lb_ref[...]
        du2 = dmix[:, :C_CONV] * _silu_grad(u2, _sigmoid(u2))
        duh = du2 * lg_ref[...]
        du1 = rstd * (duh - jnp.mean(duh, axis=-1, keepdims=True) - uh * jnp.mean(duh * uh, axis=-1, keepdims=True))
        du1_ref[...] = du1
        small_ref[0:1, :] += jnp.sum(du2 * uh, axis=0, keepdims=True)
        small_ref[1:2, :] += jnp.sum(du2, axis=0, keepdims=True)
        small_ref[2:3, :] += jnp.sum(du1, axis=0, keepdims=True)
        for g in range(len(POOL_WINDOWS)):
            cols = slice(g * POOL_GROUP, (g + 1) * POOL_GROUP)
            dq = dmix[:, C_CONV + g * POOL_GROUP:C_CONV + (g + 1) * POOL_GROUP]
            pwb = pw_ref[g].astype(bf16)
            pg = po_ref[:, cols]
            mixed = _dot(pg, pwb)
            small_ref[3:4, cols] += jnp.sum(dq * mixed, axis=0, keepdims=True)
            dmixed = (dq * ps_ref[:, cols]).astype(bf16)
            dpw_ref[g] += _dot_tn(pg, dmixed)
            dpo_ref[:, cols] = _dot_nt(dmixed, pwb)

        @pl.when(i == n - 1)
        def _():
            dwo_ref[...] = awo_ref[...].astype(bf16)

    return pl.pallas_call(
        body, name="mix_bwd_local", grid=(n,),
        in_specs=[_row(0, (tm, D_MODEL)), _row(0, (tm, D_MODEL)), _row(0, (tm, C_CONV)), _row(0, (tm, C_POOL)),
                  _full((D_MODEL, D_MODEL)), _full((1, C_CONV)), _full((1, C_CONV)),
                  _full((4, POOL_GROUP, POOL_GROUP)), _full((1, C_POOL))],
        out_specs=[_row(0, (tm, C_CONV)), _row(0, (tm, C_POOL)), _full((D_MODEL, D_MODEL)),
                   _full((4, POOL_GROUP, POOL_GROUP)), _full((8, C_CONV))],
        out_shape=[jax.ShapeDtypeStruct((s, C_CONV), f32), jax.ShapeDtypeStruct((s, C_POOL), f32),
                   jax.ShapeDtypeStruct((D_MODEL, D_MODEL), bf16),
                   jax.ShapeDtypeStruct((4, POOL_GROUP, POOL_GROUP), f32), jax.ShapeDtypeStruct((8, C_CONV), f32)],
        scratch_shapes=[pltpu.VMEM((D_MODEL, D_MODEL), f32)],
        compiler_params=_cp("arbitrary"),
    )(dx1, mix, u1, pooled, w_out, ln_g, ln_b, pool_w, pool_scale)


def _in_bwd(du1, dpo, z, x, dx1, conv_w, g_mix, w_in_g, parts, small_parts):
    s = x.shape[0]
    tm = min(ROW_TILE, s)
    n = s // tm
    hb = tm // HALO
    last = s // HALO - 1
    nv = tm // 8
    assert nv >= SHIFT_PAD and nv % SHIFT_GROUPS == 0
    nc, ns = len(parts), len(small_parts)
    cs = _crossed(parts)

    def body(du_ref, dun_ref, dp_ref, dpn_ref, z_ref, zp_ref, x_ref, dx1_ref, cw_ref, g_ref, w_ref, *refs):
        outs = refs[nc + ns:]
        gx_ref, dw_ref, dcw_ref, small_ref = outs[:4]
        eu, ed, ep, ss, u0_ref, dz_ref, acc_ref, dcw_acc = outs[4 + nc + ns:12 + nc + ns]
        sems = outs[12 + nc + ns:]
        cross = _Cross(refs[:nc], outs[4:4 + nc], sems[0], sems[1])
        gather = _SmallGather(refs[nc:nc + ns], outs[4 + nc:4 + nc + ns], sems[2], sems[3])
        i = pl.program_id(0)
        pl.when(i == 0)(functools.partial(_handshake, cross.peers()))
        pl.when(i == 0)(cross.start)
        pl.when(i == 0)(gather.start)

        @pl.when(i == 0)
        def _():
            acc_ref[...] = jnp.zeros_like(acc_ref)
            dcw_acc[...] = jnp.zeros_like(dcw_acc)
            small_ref[...] = jnp.zeros_like(small_ref)

        keep_prev = (i > 0).astype(f32)
        keep_next = (i < n - 1).astype(f32)
        zp = zp_ref[...] * keep_prev
        u0_prev = zp[:, :C_CONV] * _sigmoid(zp[:, C_CONV:2 * C_CONV])
        u0_ref[...] = z_ref[:, :C_CONV] * _sigmoid(z_ref[:, C_CONV:2 * C_CONV])
        du_next = dun_ref[...] * keep_next
        for c, w_pool in enumerate(POOL_WINDOWS):
            lanes = slice(c * 128, (c + 1) * 128)
            eu[c, pl.ds(0, SHIFT_PAD, stride=8), :] = u0_prev[:, lanes]
            ed[c, pl.ds(nv * 8 + 7, SHIFT_PAD, stride=8), :] = du_next[:, lanes]
            ep[c, pl.ds(nv * 8 + 7, SHIFT_PAD, stride=8), :] = (
                dpn_ref[:, lanes] * keep_next / _pool_counts(HALO, w_pool, (i + 1) * tm))
            for j in range(8):
                rows = slice(j * nv, (j + 1) * nv)
                eu[c, pl.ds(SHIFT_PAD * 8 + j, nv, stride=8), :] = u0_ref[rows, lanes]
                ed[c, pl.ds(j, nv, stride=8), :] = du_ref[rows, lanes]
                ep[c, pl.ds(j, nv, stride=8), :] = dp_ref[rows, lanes] / _pool_counts(nv, w_pool, i * tm + j * nv)
                if j >= 1:
                    eu[c, pl.ds(j, SHIFT_PAD, stride=8), :] = u0_ref[j * nv - SHIFT_PAD:j * nv, lanes]
                if j <= 6:
                    edge = slice((j + 1) * nv, (j + 1) * nv + SHIFT_PAD)
                    ed[c, pl.ds(nv * 8 + j, SHIFT_PAD, stride=8), :] = du_ref[edge, lanes]
                    ep[c, pl.ds(nv * 8 + j, SHIFT_PAD, stride=8), :] = (
                        dp_ref[edge, lanes] / _pool_counts(SHIFT_PAD, w_pool, i * tm + (j + 1) * nv))
        for c, w_pool in enumerate(POOL_WINDOWS):
            lanes = slice(c * 128, (c + 1) * 128)
            b_lanes = slice(C_CONV + c * 128, C_CONV + (c + 1) * 128)
            v_lanes = slice(2 * C_CONV + c * 128, 2 * C_CONV + (c + 1) * 128)
            for v0 in range(0, nv, SHIFT_GROUPS):
                span = SHIFT_GROUPS * 8
                acc = jnp.zeros((span, 128), f32)
                for k in range(CONV_K):
                    acc = acc + cw_ref[pl.ds(k, 1), lanes] * ed[c, pl.ds((v0 + CONV_K - 1 - k) * 8, span), :]
                ss[0, v0 * 8:v0 * 8 + span, :] = acc
                acc = ep[c, pl.ds(v0 * 8, span), :]
                for d in range(1, w_pool):
                    acc = acc + ep[c, pl.ds((v0 + d) * 8, span), :]
                ss[1, v0 * 8:v0 * 8 + span, :] = acc
                d1 = ed[c, pl.ds(v0 * 8, span), :]
                for k in range(CONV_K):
                    prod = d1 * eu[c, pl.ds((SHIFT_PAD - (CONV_K - 1) + v0 + k) * 8, span), :]
                    fold = prod[0:8]
                    for r in range(8, span, 8):
                        fold = fold + prod[r:r + 8]
                    dcw_acc[k, :, lanes] += fold
            for j in range(8):
                rows = slice(j * nv, (j + 1) * nv)
                du0 = ss[0, pl.ds(j, nv, stride=8), :]
                av, sv = z_ref[rows, lanes], _sigmoid(z_ref[rows, b_lanes])
                dz_ref[rows, lanes] = (du0 * sv).astype(bf16)
                dz_ref[rows, b_lanes] = (du0 * av * sv * (1.0 - sv)).astype(bf16)
                dz_ref[rows, v_lanes] = (ss[1, pl.ds(j, nv, stride=8), :] - dp_ref[rows, lanes]).astype(bf16)
        h, xh, r = _rms_fwd(x_ref[...], g_ref[...])
        dz = dz_ref[...]
        acc_ref[...] += _dot_tn(h.astype(bf16), dz)
        dh = _dot_nt(dz[:, 0:W_IN_CHUNK], w_ref[0])
        for j in range(1, N_CHIPS):
            dh = dh + _dot_nt(dz[:, j * W_IN_CHUNK:(j + 1) * W_IN_CHUNK], w_ref[j])
        dxb, dg = _rms_bwd(dh, xh, r, g_ref[...])
        gx_ref[...] = dx1_ref[...] + dxb
        small_ref[0:1, :] += dg

        @pl.when(i == n - 1)
        def _():
            for j in range(N_CHIPS):
                dw_ref[j] = acc_ref[:, j * W_IN_CHUNK:(j + 1) * W_IN_CHUNK].astype(bf16)
            dcw_ref[...] = jnp.sum(dcw_acc[...], axis=1)

        pl.when(i == n - 1)(gather.finish)
        pl.when(i == n - 1)(cross.finish)

    nxt = lambda i: (jnp.minimum((i + 1) * hb, last), 0)
    res = pl.pallas_call(
        body, name="in_bwd", grid=(n,),
        in_specs=[_row(0, (tm, C_CONV)), pl.BlockSpec((HALO, C_CONV), nxt),
                  _row(0, (tm, C_POOL)), pl.BlockSpec((HALO, C_POOL), nxt),
                  _row(0, (tm, W_IN_COLS)),
                  pl.BlockSpec((HALO, W_IN_COLS), lambda i: (jnp.maximum(i * hb - 1, 0), 0)),
                  _row(0, (tm, D_MODEL)), _row(0, (tm, D_MODEL)), _full((HALO, C_CONV)), _full((1, D_MODEL)),
                  _full((N_CHIPS, D_MODEL, W_IN_CHUNK))] + cs["in_specs"] + [ANY] * ns,
        out_specs=[_row(0, (tm, D_MODEL)), _full((N_CHIPS, D_MODEL, W_IN_CHUNK)), _full((HALO, C_CONV)),
                   _full((8, D_MODEL))] + cs["out_specs"] + [ANY] * ns,
        out_shape=[jax.ShapeDtypeStruct((s, D_MODEL), f32), jax.ShapeDtypeStruct((N_CHIPS, D_MODEL, W_IN_CHUNK), bf16),
                   jax.ShapeDtypeStruct((HALO, C_CONV), f32), jax.ShapeDtypeStruct((8, D_MODEL), f32)] + cs["out_shape"]
        + _gathered_shapes(small_parts),
        scratch_shapes=[pltpu.VMEM((4, (SHIFT_PAD + nv) * 8, 128), f32), pltpu.VMEM((4, (nv + SHIFT_PAD) * 8, 128), f32),
                        pltpu.VMEM((4, (nv + SHIFT_PAD) * 8, 128), f32), pltpu.VMEM((2, tm, 128), f32),
                        pltpu.VMEM((tm, C_CONV), f32), pltpu.VMEM((tm, W_IN_COLS), bf16),
                        pltpu.VMEM((D_MODEL, W_IN_COLS), f32), pltpu.VMEM((HALO, 8, C_CONV), f32)] + cs["scratch"]
        + _SmallGather.scratch(ns),
        compiler_params=_cp("arbitrary", collective_id=6),
    )(du1, du1, dpo, dpo, z, z, x, dx1, conv_w, g_mix, w_in_g, *parts, *small_parts)
    return res[:4], res[4:4 + nc], res[4 + nc:]


def _sibling_handshake(x, y, c):
    barrier = pltpu.get_barrier_semaphore()
    pl.semaphore_signal(barrier, inc=1, device_id=(x, y, 1 - c), device_id_type=MESH)
    pl.semaphore_wait(barrier, 1)


def _pair_reduce(name, collective_id, grads):
    nk = len(grads)
    halves = [g.shape[1] // 2 for g in grads]

    def body(*refs):
        ins, outs, got, own = refs[:nk], refs[nk:2 * nk], refs[2 * nk:3 * nk], refs[3 * nk:4 * nk]
        send_sems, recv_sems, load_sems = refs[4 * nk:]
        x, y, c, _, _ = _place()
        _sibling_handshake(x, y, c)

        def half(k, core):
            return ins[k].at[:, pl.ds(core * halves[k], halves[k]), :]

        cps = [_remote(half(k, 1 - c), got[k], send_sems.at[k], recv_sems.at[k], (x, y, 1 - c)) for k in range(nk)]
        loads = [pltpu.make_async_copy(half(k, c), own[k], load_sems.at[k]) for k in range(nk)]
        for cp in cps + loads:
            cp.start()
        for k in range(nk):
            loads[k].wait()
            cps[k].wait_recv()
            outs[k][...] = (own[k][...].astype(f32) + got[k][...].astype(f32)).astype(bf16)
        for cp in cps:
            cp.wait_send()

    shapes = [(N_CHIPS, h, g.shape[2]) for g, h in zip(grads, halves)]
    return pl.pallas_call(
        body, name=f"pair_reduce_{name}", in_specs=[ANY] * nk, out_specs=[VMEM] * nk,
        out_shape=[jax.ShapeDtypeStruct(s, bf16) for s in shapes],
        scratch_shapes=[pltpu.VMEM(s, bf16) for s in shapes] * 2
        + [pltpu.SemaphoreType.DMA((nk,)), pltpu.SemaphoreType.DMA((nk,)), pltpu.SemaphoreType.DMA((nk,))],
        compiler_params=pltpu.CompilerParams(vmem_limit_bytes=VMEM_LIMIT, collective_id=collective_id),
    )(*grads)


class _Cross:
    def __init__(self, parts, landed, send_sems, recv_sems):
        self.parts, self.landed, self.send_sems, self.recv_sems = parts, landed, send_sems, recv_sems
        self.x, self.y, self.c, self.me, self.others = _place()

    def peers(self):
        return [(self.x, self.y, 1 - self.c)] + [(ox, oy, self.c) for ox, oy in self.others]

    def _copy(self, k, j, src_chunk, dst_slot):
        ox, oy = self.others[j]
        return _remote(self.parts[k].at[src_chunk], self.landed[k].at[dst_slot], self.send_sems.at[3 * k + j],
                       self.recv_sems.at[3 * k + j], (ox, oy, self.c))

    def _each(self):
        return [(k, j, 2 * self.others[j][0] + self.others[j][1]) for k in range(len(self.parts)) for j in range(3)]

    def start(self):
        for k, j, chip in self._each():
            self._copy(k, j, chip, self.me).start()

    def finish(self):
        for k, j, chip in self._each():
            self._copy(k, j, chip, chip).wait_recv()
        for k, j, chip in self._each():
            self._copy(k, j, chip, self.me).wait_send()

    @staticmethod
    def scratch(n):
        return [pltpu.SemaphoreType.DMA((3 * n,)), pltpu.SemaphoreType.DMA((3 * n,))]


class _CrossAll:
    def __init__(self, grads, landed, send_sems, recv_sems):
        self.grads, self.landed, self.send_sems, self.recv_sems = grads, landed, send_sems, recv_sems
        self.x, self.y, self.c, self.me, self.others = _place()
        self.dev = 2 * self.me + self.c

    def _piece(self, k, chip, half):
        rows = self.grads[k].shape[1] // 2
        return self.grads[k].at[chip, pl.ds(half * rows, rows), :]

    def _to_sibling(self, k):
        return _remote(self._piece(k, self.me, 1 - self.c), self.landed[k].at[self.dev], self.send_sems.at[7 * k],
                       self.recv_sems.at[7 * k], (self.x, self.y, 1 - self.c))

    def _to_chip(self, k, j, half):
        ox, oy = self.others[j]
        return _remote(self._piece(k, 2 * ox + oy, half), self.landed[k].at[self.dev],
                       self.send_sems.at[7 * k + 1 + 2 * j + half], self.recv_sems.at[7 * k + 1 + 2 * j + self.c],
                       (ox, oy, half))

    def _from(self, k, sem, slot):
        return _remote(self.landed[k].at[slot], self.landed[k].at[slot], self.send_sems.at[7 * k + sem],
                       self.recv_sems.at[7 * k + sem], (self.x, self.y, 1 - self.c))

    def start(self):
        for k in range(len(self.grads)):
            self._to_sibling(k).start()
            for j in range(3):
                for half in range(2):
                    self._to_chip(k, j, half).start()

    def finish(self):
        for k in range(len(self.grads)):
            self._from(k, 0, 2 * self.me + 1 - self.c).wait_recv()
            for j, (ox, oy) in enumerate(self.others):
                for core in range(2):
                    self._from(k, 1 + 2 * j + core, 4 * ox + 2 * oy + core).wait_recv()
        for k in range(len(self.grads)):
            self._to_sibling(k).wait_send()
            for j in range(3):
                for half in range(2):
                    self._to_chip(k, j, half).wait_send()

    @staticmethod
    def scratch(n):
        return [pltpu.SemaphoreType.DMA((7 * n,)), pltpu.SemaphoreType.DMA((7 * n,))]


def _rs_sum_chips(name, place, landed, part):
    _, half, cols = landed.shape

    def body(place_ref, l_ref, p_ref, o_ref):
        me = place_ref[0]
        own = p_ref[0].astype(f32)
        acc = jnp.where(me == 0, own, l_ref[0].astype(f32))
        for j in range(1, N_CHIPS):
            acc = acc + jnp.where(me == j, own, l_ref[j].astype(f32))
        o_ref[...] = acc

    return pl.pallas_call(
        body, name=f"rs_sum_chips_{name}",
        grid_spec=pltpu.PrefetchScalarGridSpec(
            num_scalar_prefetch=1, grid=(1,),
            in_specs=[pl.BlockSpec((N_CHIPS, half, cols), lambda t, place_ref: (0, 0, 0)),
                      pl.BlockSpec((1, half, cols), lambda t, place_ref: (place_ref[0], 0, 0))],
            out_specs=pl.BlockSpec((half, cols), lambda t, place_ref: (place_ref[1], 0))),
        out_shape=jax.ShapeDtypeStruct((2 * half, cols), f32),
        compiler_params=_cp("arbitrary"),
    )(place, landed, part)


def _grad_pair(shards):
    nk = len(shards)

    def body(*refs):
        outs = refs[nk:2 * nk]
        send_sems, recv_sems = refs[2 * nk:]
        x, y, c, _, _ = _place()
        _sibling_handshake(x, y, c)

        def half(k, core):
            h = outs[k].shape[0] // 2
            return outs[k].at[pl.ds(core * h, h), :]

        cps = [_remote(half(k, c), half(k, c), send_sems.at[k], recv_sems.at[k], (x, y, 1 - c)) for k in range(nk)]
        for cp in cps:
            cp.start()
        for k in range(nk):
            _remote(half(k, 1 - c), half(k, 1 - c), send_sems.at[k], recv_sems.at[k], (x, y, 1 - c)).wait_recv()
        for cp in cps:
            cp.wait_send()

    return pl.pallas_call(
        body, name="grad_pair", in_specs=[ANY] * nk, out_specs=[ANY] * nk,
        out_shape=[jax.ShapeDtypeStruct(a.shape, f32) for a in shards],
        input_output_aliases={k: k for k in range(nk)},
        scratch_shapes=[pltpu.SemaphoreType.DMA((nk,)), pltpu.SemaphoreType.DMA((nk,))],
        compiler_params=pltpu.CompilerParams(collective_id=2),
    )(*shards)


def _adam_math(w, g, m, v):
    m = ADAM_B1 * m + (1.0 - ADAM_B1) * g
    v = ADAM_B2 * v + (1.0 - ADAM_B2) * (g * g)
    m_hat = m / (1.0 - ADAM_B1 ** ADAM_STEP)
    v_hat = v / (1.0 - ADAM_B2 ** ADAM_STEP)
    delta = -ADAM_LR * (m_hat / (jnp.sqrt(v_hat) + ADAM_EPS) + ADAM_WD * w)
    return delta, m, v


def _adam(ws, gs, ms, vs):
    n = len(ws)
    steps = 2 * SHARD_STEPS

    def body(*refs):
        for k in range(n):
            w_ref, g_ref, m_ref, v_ref = (refs[j * n + k] for j in range(4))
            go_ref, d_ref, nm_ref, nv_ref = (refs[(4 + j) * n + k] for j in range(4))
            g = g_ref[...]
            go_ref[...] = g
            d_ref[...], nm_ref[...], nv_ref[...] = _adam_math(w_ref[...], g, m_ref[...], v_ref[...])

    specs = [_row(0, (w.shape[0] // steps, w.shape[1])) for w in ws]
    res = pl.pallas_call(
        body, name="adam_shards", grid=(steps,), in_specs=specs * 4, out_specs=specs * 4,
        out_shape=[jax.ShapeDtypeStruct(w.shape, f32) for w in ws] * 4,
        compiler_params=_cp("parallel"),
    )(*ws, *gs, *ms, *vs)
    return [tuple(res[j * n + k] for j in range(4)) for k in range(n)]


class _SmallGather:
    def __init__(self, parts, bufs, send_sems, recv_sems):
        self.parts, self.bufs, self.send_sems, self.recv_sems = parts, bufs, send_sems, recv_sems
        self.x, self.y, self.c, _, self.others = _place()
        self.sibling = (self.x, self.y, 1 - self.c)

    def _copy(self, a, k, block, to, src=None):
        slot = self.bufs[a].at[4 * block[0] + 2 * block[1] + block[2]]
        return _remote(slot if src is None else src, slot, self.send_sems.at[7 * a + k], self.recv_sems.at[7 * a + k],
                       to)

    def _first(self, a):
        me = (self.x, self.y, self.c)
        return [self._copy(a, 0, me, self.sibling, src=self.parts[a])] + [
            self._copy(a, 1 + j, me, (*chip, self.c), src=self.parts[a]) for j, chip in enumerate(self.others)]

    def _passed(self, a):
        return [self._copy(a, 4 + j, (*chip, self.c), self.sibling) for j, chip in enumerate(self.others)]

    @staticmethod
    def scratch(n):
        return [pltpu.SemaphoreType.DMA((7 * n,)), pltpu.SemaphoreType.DMA((7 * n,))]

    def start(self):
        for a in range(len(self.parts)):
            for cp in self._first(a):
                cp.start()

    def finish(self):
        sent = []
        for a in range(len(self.parts)):
            passed = self._passed(a)
            for j, chip in enumerate(self.others):
                self._copy(a, 1 + j, (*chip, self.c), self.sibling).wait_recv()
                passed[j].start()
            sent += self._first(a) + passed
        for a in range(len(self.parts)):
            self._copy(a, 0, self.sibling, self.sibling).wait_recv()
            for j, chip in enumerate(self.others):
                self._copy(a, 4 + j, (*chip, 1 - self.c), self.sibling).wait_recv()
        for cp in sent:
            cp.wait_send()


def _gathered_shapes(parts):
    return [jax.ShapeDtypeStruct((N_DEV,) + p.shape, p.dtype) for p in parts]


def _rs_sum_group(name, place, landed, parts, landed_all, grads_all, cross_parts, small_parts):
    nk, na, nx, ns = len(landed), len(landed_all), len(cross_parts), len(small_parts)
    dims = [a.shape[1:] for a in landed]
    dims_all = [a.shape[1:] for a in landed_all]

    def body(place_ref, *refs):
        take = iter(refs)
        l_refs, p_refs, la_refs, ga_refs, x_refs, sp_refs, o_refs, oa_refs, xl_refs, sbufs = (
            [next(take) for _ in range(cnt)] for cnt in (nk, nk, na, na, nx, ns, nk, na, nx, ns))
        sems = list(take)
        cross = _Cross(x_refs, xl_refs, sems[0], sems[1])
        small = _SmallGather(sp_refs, sbufs, sems[2], sems[3])
        t = pl.program_id(0)

        @pl.when(t == 0)
        def _():
            _handshake(cross.peers())
            cross.start()
            small.start()

        me = place_ref[0]
        for l_ref, p_ref, o_ref in zip(l_refs, p_refs, o_refs):
            own = p_ref[0].astype(f32)
            acc = jnp.where(me == 0, own, l_ref[0].astype(f32))
            for j in range(1, N_CHIPS):
                acc = acc + jnp.where(me == j, own, l_ref[j].astype(f32))
            o_ref[...] = acc
        dev = 2 * me + place_ref[1]
        for l_ref, g_ref, o_ref in zip(la_refs, ga_refs, oa_refs):
            own = g_ref[0].astype(f32)
            acc = jnp.where(dev == 0, own, l_ref[0].astype(f32))
            for d in range(1, N_DEV):
                acc = acc + jnp.where(dev == d, own, l_ref[d].astype(f32))
            o_ref[...] = acc

        @pl.when(t == 1)
        def _():
            small.finish()
            cross.finish()

    def halves(h, c, lead, index):
        return pl.BlockSpec((lead, h // 2, c) if lead else (h // 2, c), index)

    in_specs = [halves(h, c, N_CHIPS, lambda t, pr: (0, t, 0)) for h, c in dims]
    in_specs += [halves(h, c, 1, lambda t, pr: (pr[0], t, 0)) for h, c in dims]
    in_specs += [halves(h, c, N_DEV, lambda t, pr: (0, t, 0)) for h, c in dims_all]
    in_specs += [halves(h, c, 1, lambda t, pr: (pr[0], 2 * pr[1] + t, 0)) for h, c in dims_all]
    in_specs += [ANY] * (nx + ns)
    out_specs = [halves(h, c, 0, lambda t, pr: (2 * pr[1] + t, 0)) for h, c in dims + dims_all] + [ANY] * (nx + ns)
    out_shape = [jax.ShapeDtypeStruct((2 * h, c), f32) for h, c in dims + dims_all]
    out_shape += [jax.ShapeDtypeStruct(a.shape, a.dtype) for a in cross_parts] + _gathered_shapes(small_parts)
    res = pl.pallas_call(
        body, name=f"rs_sum_{name}",
        grid_spec=pltpu.PrefetchScalarGridSpec(
            num_scalar_prefetch=1, grid=(2,), in_specs=in_specs, out_specs=out_specs,
            scratch_shapes=_Cross.scratch(nx) + _SmallGather.scratch(ns)),
        out_shape=out_shape, compiler_params=_cp("arbitrary", collective_id=7),
    )(place, *landed, *parts, *landed_all, *grads_all, *cross_parts, *small_parts)
    return res[:nk], res[nk:nk + na], res[nk + na:nk + na + nx], res[nk + na + nx:]


SMALL_PARAMS = ("g_ple_gate", "g_ple_post", "g_final", "g_ffn", "ln_g", "ln_b", "conv_b", "pool_scale", "pool_w",
                "conv_w", "g_mix")
SMALL_ROWS = {"g_ple_gate": (0, 0), "g_ple_post": (0, 1), "g_final": (0, 2), "g_ffn": (1, 0), "ln_g": (2, 0),
              "ln_b": (2, 1), "conv_b": (2, 2), "pool_scale": (2, 3), "g_mix": (5, 0)}
LOSS_ROW = (0, 3)


def _small_adam(place, gathered, parts, params):
    nb, names = len(parts), SMALL_PARAMS
    flat = [a for nm in names for a in params[nm]]

    def body(place_ref, *refs):
        b_refs, p_refs = refs[:nb], refs[nb:2 * nb]
        w_refs = refs[2 * nb:2 * nb + 3 * len(names)]
        outs = refs[2 * nb + 3 * len(names):]
        loss_ref, o_refs, cw_sum = outs[0], outs[1:1 + 4 * len(names)], outs[1 + 4 * len(names)]
        chip = place_ref[0]
        me = 2 * chip + place_ref[1]

        def total(blk, idx):
            own = p_refs[blk][idx]
            g = jnp.where(me == 0, own, b_refs[blk][(0,) + idx])
            for d in range(1, N_DEV):
                g = g + jnp.where(me == d, own, b_refs[blk][(d,) + idx])
            return g

        everything = (slice(None), slice(None))
        loss_ref[...] = total(LOSS_ROW[0], (pl.ds(LOSS_ROW[1], 1), pl.ds(0, 128)))
        d_cw = total(4, everything)
        mine = jnp.where(chip == 0, d_cw[:, 0:128], 0.0)
        for j in range(1, N_CHIPS):
            mine = mine + jnp.where(chip == j, d_cw[:, j * 128:(j + 1) * 128], 0.0)
        cw_sum[...] = mine
        for k, nm in enumerate(names):
            w_ref, m_ref, v_ref = w_refs[3 * k:3 * k + 3]
            g_ref, d_ref, nm_ref, nv_ref = o_refs[4 * k:4 * k + 4]
            if nm == "pool_w":
                g = total(3, everything + (slice(None),))
            elif nm == "conv_w":
                g = cw_sum[pl.ds(0, CONV_K), :]
            else:
                blk, row = SMALL_ROWS[nm]
                g = total(blk, (pl.ds(row, 1), slice(None)))
            g_ref[...] = g
            d_ref[...], nm_ref[...], nv_ref[...] = _adam_math(w_ref[...], g, m_ref[...], v_ref[...])

    whole = lambda a: pl.BlockSpec(a.shape, lambda t, pr: (0,) * a.ndim)
    out_shape = [jax.ShapeDtypeStruct((1, 128), f32)]
    out_shape += [jax.ShapeDtypeStruct(params[nm][0].shape, f32) for nm in names for _ in range(4)]
    res = pl.pallas_call(
        body, name="small_adam",
        grid_spec=pltpu.PrefetchScalarGridSpec(
            num_scalar_prefetch=1, grid=(1,),
            in_specs=[whole(a) for a in list(gathered) + list(parts) + flat],
            out_specs=[whole(s) for s in out_shape], scratch_shapes=[pltpu.VMEM((HALO, 128), f32)]),
        out_shape=out_shape, compiler_params=_cp("arbitrary"),
    )(place, *gathered, *parts, *flat)
    return res[0], {nm: res[1 + 4 * k:5 + 4 * k] for k, nm in enumerate(names)}


def _pad_rows(a, rows):
    return jnp.concatenate([a, jnp.zeros((rows - a.shape[0],) + a.shape[1:], a.dtype)], axis=0)


def kernel(x, p, g_mix, w_in, conv_w, conv_b, ln_g, ln_b, pool_w, pool_scale, w_out, g_ffn, w_gate_up, w_down, g_ple_gate, w_ple_gate, w_ple_up, g_ple_post, g_final, loss_target, m_g_mix, m_w_in, m_conv_w, m_conv_b, m_ln_g, m_ln_b, m_pool_w, m_pool_scale, m_w_out, m_g_ffn, m_w_gate_up, m_w_down, m_g_ple_gate, m_w_ple_gate, m_w_ple_up, m_g_ple_post, m_g_final, v_g_mix, v_w_in, v_conv_w, v_conv_b, v_ln_g, v_ln_b, v_pool_w, v_pool_scale, v_w_out, v_g_ffn, v_w_gate_up, v_w_down, v_g_ple_gate, v_w_ple_gate, v_w_ple_up, v_g_ple_post, v_g_final):
    seq = x.shape[1]
    me = 2 * lax.axis_index("x") + lax.axis_index("y")
    chip = me.astype(jnp.int32).reshape(1)
    core = lax.axis_index("c").astype(jnp.int32).reshape(1)
    place = jnp.concatenate([chip, core])
    xs, ps, ts = x.reshape(seq, D_MODEL), p.reshape(seq, D_PLE), loss_target.reshape(seq, D_MODEL)

    big = [w_in[0], w_gate_up[0], w_out[0], w_down[0], w_ple_gate[0], w_ple_up[0]]
    big_m = [m_w_in[0], m_w_gate_up[0], m_w_out[0], m_w_down[0], m_w_ple_gate[0], m_w_ple_up[0]]
    big_v = [v_w_in[0], v_w_gate_up[0], v_w_out[0], v_w_down[0], v_w_ple_gate[0], v_w_ple_up[0]]
    b_in, b_gu, b_out, b_down, b_pg, b_pu, b_cw = _cast_into_slots(
        chip, big + [_pad_rows(conv_w[0], HALO)], [bf16] * len(big) + [f32])
    xi, yi = lax.axis_index("x"), lax.axis_index("y")
    order = jnp.stack([me, 2 * (1 - xi) + yi, 2 * xi + 1 - yi, 2 * (1 - xi) + 1 - yi]).astype(jnp.int32)

    z, (w_in_g, cw_g, w_out_g) = _mix_in(xs, g_mix, order, [b_in, b_cw, b_out])
    conv_w_f = cw_g.transpose(1, 0, 2).reshape(HALO, C_CONV)
    w_out_f = w_out_g.reshape(D_MODEL, D_MODEL)
    (x1, mix, u1, pooled, h2), (w_gu_g,) = _conv_pool_out(z, xs, conv_w_f, conv_b, ln_g, ln_b, pool_w[0], pool_scale,
                                                          w_out_f, g_ffn, [b_gu])
    (gu, ffn_f), (w_down_g, w_pg_g, w_pu_g) = _ffn_up(h2, w_gu_g, [b_down, b_pg, b_pu])
    w_down_f = w_down_g.reshape(D_FF, D_MODEL)
    x2 = _ffn_down(x1, ffn_f, w_down_f)
    w_pg_f = w_pg_g.reshape(D_MODEL, D_MODEL)
    dx2, d_w_pg, d_w_pu, small_ple = _ple_loss(x2, ps, ts, g_ple_gate, g_ple_post, g_final.reshape(1, D_MODEL),
                                               w_pg_f, w_pu_g)
    d_w_down = _ffn_bwd_dw_down(ffn_f, dx2)
    grads_a = [d_w_down.reshape(N_CHIPS, -1, D_MODEL), d_w_pg.reshape(N_CHIPS, -1, D_MODEL), d_w_pu]
    (dx1, dgu, small_ffn), landed_a = _ffn_bwd_dx(dx2, x1, gu, g_ffn, w_gu_g, w_down_f, grads_a)
    d_w_gu = _ffn_bwd_dw_gu(h2, dgu)
    du1, dpo, d_w_out, d_pool_w, small_mix = _mix_bwd_local(dx1, mix, u1, pooled, w_out_f, ln_g, ln_b, pool_w[0],
                                                             pool_scale)
    parts_b = _pair_reduce("b", 0, [d_w_gu, d_w_out.reshape(N_CHIPS, -1, D_MODEL)])
    small_0 = [small_ple, small_ffn, small_mix, d_pool_w]
    (grad_x, d_w_in, d_conv_w, small_in), landed_b, small_all_0 = _in_bwd(du1, dpo, z, xs, dx1, conv_w_f, g_mix,
                                                                            w_in_g, parts_b, small_0)
    parts_c = _pair_reduce("c", 1, [d_w_in])
    small_1 = [d_conv_w, small_in]
    (h_gu, h_out), (h_down, h_pg, h_pu), landed_c, small_all_1 = _rs_sum_group(
        "ab", place, landed_b, parts_b, landed_a, grads_a, parts_c, small_1)
    h_in = _rs_sum_chips("w_in", place, landed_c[0], parts_c[0])
    big_g = _grad_pair([h_in, h_gu, h_out, h_down, h_pg, h_pu])
    big_upd = _adam(big, big_g, big_m, big_v)

    p3 = lambda w, m, v: (w, m, v)
    row = lambda a: a.reshape(1, D_MODEL)
    params = dict(
        g_ple_gate=p3(g_ple_gate, m_g_ple_gate, v_g_ple_gate), g_ple_post=p3(g_ple_post, m_g_ple_post, v_g_ple_post),
        g_final=p3(row(g_final), row(m_g_final), row(v_g_final)), g_ffn=p3(g_ffn, m_g_ffn, v_g_ffn),
        ln_g=p3(ln_g, m_ln_g, v_ln_g), ln_b=p3(ln_b, m_ln_b, v_ln_b), conv_b=p3(conv_b, m_conv_b, v_conv_b),
        pool_scale=p3(pool_scale, m_pool_scale, v_pool_scale), pool_w=p3(pool_w[0], m_pool_w[0], v_pool_w[0]),
        conv_w=p3(conv_w[0], m_conv_w[0], v_conv_w[0]), g_mix=p3(g_mix, m_g_mix, v_g_mix))
    loss, small = _small_adam(place, list(small_all_0) + list(small_all_1), small_0 + small_1, params)
    back = dict(g_final=lambda a: a.reshape(D_MODEL), pool_w=lambda a: a[None], conv_w=lambda a: a[None])

    names = ["g_mix", "w_in", "conv_w", "conv_b", "ln_g", "ln_b", "pool_w", "pool_scale", "w_out", "g_ffn",
             "w_gate_up", "w_down", "g_ple_gate", "w_ple_gate", "w_ple_up", "g_ple_post", "g_final"]
    big_at = {"w_in": 0, "w_gate_up": 1, "w_out": 2, "w_down": 3, "w_ple_gate": 4, "w_ple_up": 5}
    out = [loss[0, 0], grad_x.reshape(1, seq, D_MODEL)]
    for kind in range(4):
        for nm in names:
            if nm in big_at:
                out.append(big_upd[big_at[nm]][kind][None])
            else:
                out.append(back.get(nm, lambda a: a)(small[nm][kind]))
    return tuple(out)
```

```python
import functools

import jax
import jax.numpy as jnp
from jax import lax
from jax.experimental import pallas as pl
from jax.experimental.pallas import tpu as pltpu

f32, bf16 = jnp.float32, jnp.bfloat16

EPS = 1e-6
D_MODEL = 1024
C_CONV = 512
C_POOL = 512
POOL_WINDOWS = (2, 4, 8, 16)
POOL_GROUP = 128
CONV_K = 31
D_FF = 2816
D_PLE = 256
N_CHIPS = 4
N_DEV = 8
W_IN_COLS = 2 * C_CONV + C_POOL
W_IN_CHUNK = W_IN_COLS // N_CHIPS
FF_CHUNK = 2 * D_FF // N_CHIPS
PLE_CHUNK = D_MODEL // N_CHIPS
HALO = 32
ROW_TILE = 512
SHIFT_PAD = 32
SHIFT_GROUPS = 16
FF_SUB = (0, 512, 1024, FF_CHUNK)
VMEM_LIMIT = 56 * 1024 * 1024

ADAM_LR = 0.001
ADAM_B1 = 0.9
ADAM_B2 = 0.999
ADAM_EPS = 1e-08
ADAM_WD = 0.01
ADAM_STEP = 10

MESH = pl.DeviceIdType.MESH
ANY = pl.BlockSpec(memory_space=pl.ANY)
VMEM = pl.BlockSpec(memory_space=pltpu.VMEM)


def _cp(*sem, collective_id=None):
    return pltpu.CompilerParams(dimension_semantics=sem, vmem_limit_bytes=VMEM_LIMIT, collective_id=collective_id)


def _dot(a, b):
    return jnp.dot(a, b, preferred_element_type=f32)


def _dot_nt(a, b):
    return lax.dot_general(a, b, (((1,), (1,)), ((), ())), preferred_element_type=f32)


def _dot_tn(a, b):
    return lax.dot_general(a, b, (((0,), (0,)), ((), ())), preferred_element_type=f32)


def _sigmoid(v):
    return jax.nn.sigmoid(v)


def _rms_fwd(v, g):
    r = lax.rsqrt(jnp.mean(v * v, axis=-1, keepdims=True) + EPS)
    vh = v * r
    return vh * g, vh, r


def _rms_bwd(dy, vh, r, g):
    dvh = dy * g
    dv = r * (dvh - vh * jnp.mean(dvh * vh, axis=-1, keepdims=True))
    return dv, jnp.sum(dy * vh, axis=0, keepdims=True)


def _silu_grad(v, s):
    return s * (1.0 + v * (1.0 - s))


def _row(i, n):
    return pl.BlockSpec((n[0], n[1]), lambda *a: (a[i], 0))


def _full(shape):
    nd = len(shape)
    return pl.BlockSpec(shape, lambda *a: (0,) * nd)


def _place():
    x, y, c = lax.axis_index("x"), lax.axis_index("y"), lax.axis_index("c")
    others = [(1 - x, y), (x, 1 - y), (1 - x, 1 - y)]
    return x, y, c, 2 * x + y, others


def _handshake(peers):
    barrier = pltpu.get_barrier_semaphore()
    for peer in peers:
        pl.semaphore_signal(barrier, inc=1, device_id=peer, device_id_type=MESH)
    pl.semaphore_wait(barrier, len(peers))


def _remote(src, dst, send_sem, recv_sem, dev):
    return pltpu.make_async_remote_copy(src_ref=src, dst_ref=dst, send_sem=send_sem, recv_sem=recv_sem,
                                        device_id=dev, device_id_type=MESH)


SHARD_STEPS = 4


def _cast_into_slots(me, ws, dtypes):
    n = len(ws)

    def body(me_ref, *refs):
        for w_ref, o_ref, dtype in zip(refs[:n], refs[n:], dtypes):
            o_ref[0] = w_ref[...].astype(dtype)

    return pl.pallas_call(
        body, name="cast_shards",
        grid_spec=pltpu.PrefetchScalarGridSpec(
            num_scalar_prefetch=1, grid=(SHARD_STEPS,),
            in_specs=[pl.BlockSpec((w.shape[0] // SHARD_STEPS, w.shape[1]), lambda r, me_ref: (r, 0)) for w in ws],
            out_specs=[pl.BlockSpec((1, w.shape[0] // SHARD_STEPS, w.shape[1]), lambda r, me_ref: (me_ref[0], r, 0))
                       for w in ws]),
        out_shape=[jax.ShapeDtypeStruct((N_CHIPS,) + w.shape, dt) for w, dt in zip(ws, dtypes)],
        compiler_params=_cp("parallel"),
    )(me, *ws)


class _Gather:
    def __init__(self, bufs, send_sems, recv_sems):
        self.bufs, self.send_sems, self.recv_sems = bufs, send_sems, recv_sems
        self.x, self.y, self.c, self.me, self.others = _place()
        self.halves = [b.shape[1] // 2 for b in bufs]

    def _piece(self, k, chip, half):
        return self.bufs[k].at[chip, pl.ds(half * self.halves[k], self.halves[k]), :]

    def _ici(self, k, j, chip):
        ox, oy = self.others[j]
        piece = self._piece(k, chip, self.c)
        return _remote(piece, piece, self.send_sems.at[6 * k + j], self.recv_sems.at[6 * k + j], (ox, oy, self.c))

    def _relay(self, k):
        first = self.c == 0
        piece = self._piece(k, jnp.where(first, self.chip(0), self.chip(1)), self.c)
        to = (jnp.where(first, self.others[1][0], self.others[0][0]),
              jnp.where(first, self.others[1][1], self.others[0][1]), self.c)
        return _remote(piece, piece, self.send_sems.at[6 * k + 2], self.recv_sems.at[6 * k + 2], to)

    def _pair(self, k, j, half):
        ox, oy = self.others[j]
        piece = self._piece(k, 2 * ox + oy, half)
        return _remote(piece, piece, self.send_sems.at[6 * k + 3 + j], self.recv_sems.at[6 * k + 3 + j],
                       (self.x, self.y, 1 - self.c))

    def _each(self, ks=None):
        return [(k, j) for k in (range(len(self.bufs)) if ks is None else ks) for j in range(3)]

    def chip(self, j):
        ox, oy = self.others[j]
        return 2 * ox + oy

    def start(self):
        _handshake([(self.x, self.y, 1 - self.c)] + [(*self.others[j], self.c) for j in range(2)])
        for k in range(len(self.bufs)):
            for j in range(2):
                self._ici(k, j, self.me).start()

    def forward(self, pairs=None):
        for k, j in self._each() if pairs is None else pairs:
            self._ici(k, j, self.chip(j)).wait_recv()
            self._pair(k, j, self.c).start()
            if j < 2:
                pl.when(self.c == j)(self._relay(k).start)

    def landed(self, pairs):
        for k, j in pairs:
            self._pair(k, j, 1 - self.c).wait_recv()

    def finish(self, ks=None):
        self.landed(self._each(ks))
        for k in range(len(self.bufs)):
            for j in range(2):
                self._ici(k, j, self.me).wait_send()
            self._relay(k).wait_send()
            for j in range(3):
                self._pair(k, j, self.c).wait_send()

    @staticmethod
    def scratch(n):
        return [pltpu.SemaphoreType.DMA((6 * n,)), pltpu.SemaphoreType.DMA((6 * n,))]


def _carried(bufs):
    n = len(bufs)
    return dict(in_specs=[ANY] * n, out_specs=[ANY] * n,
                out_shape=[jax.ShapeDtypeStruct(b.shape, b.dtype) for b in bufs], scratch=_Gather.scratch(n))


def _mix_in(x, g_mix, order, carry):
    s = x.shape[0]
    tm = min(2 * ROW_TILE, s)
    n = s // tm
    nc = len(carry)
    cs = _carried(carry)

    def body(order_ref, x_ref, g_ref, *refs):
        z_ref = refs[nc]
        bufs = refs[nc + 1:2 * nc + 1]
        h_ref, w_ref, w_sem = refs[2 * nc + 1:2 * nc + 4]
        gather = _Gather(bufs, *refs[2 * nc + 4:])
        q, i = pl.program_id(0), pl.program_id(1)
        first = i == 0
        pl.when(jnp.logical_and(q == 0, first))(gather.start)
        for j in range(3):

            @pl.when(jnp.logical_and(q == j + 1, first))
            def _():
                gather.forward([(0, j)])
                gather.landed([(0, j)])
                if j == 1:
                    gather.forward([(k, jj) for k in range(1, nc) for jj in range(2)])

        @pl.when(first)
        def _():
            load = pltpu.make_async_copy(bufs[0].at[order_ref[q]], w_ref, w_sem)
            load.start()
            load.wait()

        @pl.when(q == 0)
        def _():
            h, _, _ = _rms_fwd(x_ref[...], g_ref[...])
            h_ref[i] = h.astype(bf16)

        z_ref[...] = _dot(h_ref[i], w_ref[...])

        @pl.when(jnp.logical_and(q == N_CHIPS - 1, i == n - 1))
        def _():
            gather.forward([(k, 2) for k in range(1, nc)])
            gather.finish(range(1, nc))

    res = pl.pallas_call(
        body, name="mix_in",
        grid_spec=pltpu.PrefetchScalarGridSpec(
            num_scalar_prefetch=1, grid=(N_CHIPS, n),
            in_specs=[pl.BlockSpec((tm, D_MODEL), lambda q, i, order_ref: (jnp.where(q == 0, i, 0), 0)),
                      pl.BlockSpec((1, D_MODEL), lambda q, i, order_ref: (0, 0))] + cs["in_specs"],
            out_specs=[pl.BlockSpec((tm, W_IN_CHUNK), lambda q, i, order_ref: (i, order_ref[q]))] + cs["out_specs"],
            scratch_shapes=[pltpu.VMEM((n, tm, D_MODEL), bf16), pltpu.VMEM((D_MODEL, W_IN_CHUNK), bf16),
                            pltpu.SemaphoreType.DMA(())] + cs["scratch"]),
        out_shape=[jax.ShapeDtypeStruct((s, W_IN_COLS), f32)] + cs["out_shape"],
        input_output_aliases={3 + k: 1 + k for k in range(nc)},
        compiler_params=_cp("arbitrary", "arbitrary", collective_id=3),
    )(order, x, g_mix, *carry)
    return res[0], res[1:]


def _pool_counts(tm, w, first_row):
    t1 = (lax.broadcasted_iota(jnp.int32, (tm, 1), 0) + first_row + 1).astype(f32)
    return jnp.minimum(t1, float(w))


def _conv_pool_out(z, x, conv_w, conv_b, ln_g, ln_b, pool_w, pool_scale, w_out, g_ffn, carry):
    s = x.shape[0]
    tm = min(ROW_TILE, s)
    n = s // tm
    hb = tm // HALO
    nv = tm // 8
    assert nv >= SHIFT_PAD and nv % SHIFT_GROUPS == 0
    nc = len(carry)
    cs = _carried(carry)

    def body(z_ref, zp_ref, x_ref, cw_ref, cb_ref, lg_ref, lb_ref, pw_ref, ps_ref, wo_ref, gf_ref, *refs):
        x1_ref, mix_ref, u1_ref, pooled_ref, h2_ref = refs[nc:nc + 5]
        eu, ev, ss, u0_ref = refs[2 * nc + 5:2 * nc + 9]
        gather = _Gather(refs[nc + 5:2 * nc + 5], *refs[2 * nc + 9:])
        i = pl.program_id(0)
        pl.when(i == 0)(gather.start)
        for k in range(nc):
            pl.when(i == min(3 * n // 4 + k, n - 1))(functools.partial(gather.forward, [(k, 0), (k, 1)]))
        keep = (i > 0).astype(f32)
        zp = zp_ref[...] * keep
        u0_prev = zp[:, :C_CONV] * _sigmoid(zp[:, C_CONV:2 * C_CONV])
        u0_ref[...] = z_ref[:, :C_CONV] * _sigmoid(z_ref[:, C_CONV:2 * C_CONV])
        for c, w_pool in enumerate(POOL_WINDOWS):
            lanes = slice(c * 128, (c + 1) * 128)
            v_lanes = slice(2 * C_CONV + c * 128, 2 * C_CONV + (c + 1) * 128)
            eu[c, pl.ds(0, SHIFT_PAD, stride=8), :] = u0_prev[:, lanes]
            ev[c, pl.ds(0, SHIFT_PAD, stride=8), :] = zp[:, v_lanes]
            for j in range(8):
                rows = slice(j * nv, (j + 1) * nv)
                eu[c, pl.ds(SHIFT_PAD * 8 + j, nv, stride=8), :] = u0_ref[rows, lanes]
                ev[c, pl.ds(SHIFT_PAD * 8 + j, nv, stride=8), :] = z_ref[rows, v_lanes]
                if j >= 1:
                    edge = slice(j * nv - SHIFT_PAD, j * nv)
                    eu[c, pl.ds(j, SHIFT_PAD, stride=8), :] = u0_ref[edge, lanes]
                    ev[c, pl.ds(j, SHIFT_PAD, stride=8), :] = z_ref[edge, v_lanes]
            for v0 in range(0, nv, SHIFT_GROUPS):
                span = SHIFT_GROUPS * 8
                acc = jnp.zeros((span, 128), f32) + cb_ref[:, lanes]
                for k in range(CONV_K):
                    acc = acc + cw_ref[pl.ds(k, 1), lanes] * eu[c, pl.ds((SHIFT_PAD - (CONV_K - 1) + v0 + k) * 8, span), :]
                ss[0, v0 * 8:v0 * 8 + span, :] = acc
                acc = ev[c, pl.ds((SHIFT_PAD + v0) * 8, span), :]
                for d in range(1, w_pool):
                    acc = acc + ev[c, pl.ds((SHIFT_PAD + v0 - d) * 8, span), :]
                ss[1, v0 * 8:v0 * 8 + span, :] = acc
            for j in range(8):
                rows = slice(j * nv, (j + 1) * nv)
                u1_ref[rows, lanes] = ss[0, pl.ds(j, nv, stride=8), :]
                mean = ss[1, pl.ds(j, nv, stride=8), :] / _pool_counts(nv, w_pool, i * tm + j * nv)
                pooled_ref[rows, lanes] = (mean - z_ref[rows, v_lanes]).astype(bf16)
        u1 = u1_ref[...]
        mu = jnp.mean(u1, axis=-1, keepdims=True)
        uc = u1 - mu
        rstd = lax.rsqrt(jnp.mean(uc * uc, axis=-1, keepdims=True) + EPS)
        u2 = uc * rstd * lg_ref[...] + lb_ref[...]
        mix_ref[:, :C_CONV] = (u2 * _sigmoid(u2)).astype(bf16)
        for g in range(len(POOL_WINDOWS)):
            cols = slice(g * POOL_GROUP, (g + 1) * POOL_GROUP)
            mixed = _dot(pooled_ref[:, cols], pw_ref[g].astype(bf16))
            mix_ref[:, C_CONV + g * POOL_GROUP:C_CONV + (g + 1) * POOL_GROUP] = (mixed * ps_ref[:, cols]).astype(bf16)
        x1 = x_ref[...] + _dot(mix_ref[...], wo_ref[...])
        x1_ref[...] = x1
        h2_ref[...] = _rms_fwd(x1, gf_ref[...])[0].astype(bf16)
        @pl.when(i == n - 1)
        def _():
            gather.forward([(k, 2) for k in range(nc)])
            gather.finish()

    res = pl.pallas_call(
        body, name="conv_pool_out", grid=(n,),
        in_specs=[_row(0, (tm, W_IN_COLS)),
                  pl.BlockSpec((HALO, W_IN_COLS), lambda i: (jnp.maximum(i * hb - 1, 0), 0)),
                  _row(0, (tm, D_MODEL)), _full((HALO, C_CONV)), _full((1, C_CONV)), _full((1, C_CONV)),
                  _full((1, C_CONV)), _full((4, POOL_GROUP, POOL_GROUP)), _full((1, C_POOL)),
                  _full((D_MODEL, D_MODEL)), _full((1, D_MODEL))] + cs["in_specs"],
        out_specs=[_row(0, (tm, D_MODEL)), _row(0, (tm, D_MODEL)), _row(0, (tm, C_CONV)), _row(0, (tm, C_POOL)),
                   _row(0, (tm, D_MODEL))] + cs["out_specs"],
        out_shape=[jax.ShapeDtypeStruct((s, D_MODEL), f32), jax.ShapeDtypeStruct((s, D_MODEL), bf16),
                   jax.ShapeDtypeStruct((s, C_CONV), f32), jax.ShapeDtypeStruct((s, C_POOL), bf16),
                   jax.ShapeDtypeStruct((s, D_MODEL), bf16)] + cs["out_shape"],
        input_output_aliases={11 + k: 5 + k for k in range(nc)},
        scratch_shapes=[pltpu.VMEM((4, (SHIFT_PAD + nv) * 8, 128), f32), pltpu.VMEM((4, (SHIFT_PAD + nv) * 8, 128), f32),
                        pltpu.VMEM((2, tm, 128), f32), pltpu.VMEM((tm, C_CONV), f32)] + cs["scratch"],
        compiler_params=_cp("arbitrary", collective_id=4),
    )(z, z, x, conv_w, conv_b, ln_g, ln_b, pool_w, pool_scale, w_out, g_ffn, *carry)
    return res[:5], res[5:]


def _ffn_up(h2, w_gu_g, carry):
    s = h2.shape[0]
    tm = min(2 * ROW_TILE, s)
    n = s // tm
    nc = len(carry)
    cs = _carried(carry)

    def body(h2_ref, wg_ref, wu_ref, *refs):
        gu_ref, f_ref = refs[nc:nc + 2]
        gather = _Gather(refs[nc + 2:2 * nc + 2], *refs[2 * nc + 2:])
        c, i = pl.program_id(0), pl.program_id(1)
        pl.when(jnp.logical_and(i == 0, c == 0))(gather.start)
        direct = [(k, j) for k in range(nc) for j in range(2)]
        pl.when(jnp.logical_and(i == n - 1, c == 0))(functools.partial(gather.forward, direct))
        pl.when(jnp.logical_and(i == n // 2, c == 1))(functools.partial(gather.forward, [(k, 2) for k in range(nc)]))
        h = h2_ref[...]
        for lo, hi in zip(FF_SUB[:-1], FF_SUB[1:]):
            gate = _dot(h, wg_ref[0, :, lo:hi])
            up = _dot(h, wu_ref[0, :, lo:hi])
            gu_ref[0, :, lo:hi] = gate.astype(bf16)
            gu_ref[1, :, lo:hi] = up.astype(bf16)
            f_ref[:, lo:hi] = (gate * _sigmoid(gate) * up).astype(bf16)
        pl.when(jnp.logical_and(i == n - 1, c == 1))(gather.finish)

    res = pl.pallas_call(
        body, name="ffn_up", grid=(2, n),
        in_specs=[_row(1, (tm, D_MODEL)),
                  pl.BlockSpec((1, D_MODEL, FF_CHUNK), lambda c, i: (c, 0, 0)),
                  pl.BlockSpec((1, D_MODEL, FF_CHUNK), lambda c, i: (2 + c, 0, 0))] + cs["in_specs"],
        out_specs=[pl.BlockSpec((2, tm, FF_CHUNK), lambda c, i: (0, i, c)),
                   pl.BlockSpec((tm, FF_CHUNK), lambda c, i: (i, c))] + cs["out_specs"],
        out_shape=[jax.ShapeDtypeStruct((2, s, D_FF), bf16), jax.ShapeDtypeStruct((s, D_FF), bf16)] + cs["out_shape"],
        input_output_aliases={3 + k: 2 + k for k in range(nc)},
        scratch_shapes=cs["scratch"],
        compiler_params=_cp("arbitrary", "arbitrary", collective_id=5),
    )(h2, w_gu_g, w_gu_g, *carry)
    return res[:2], res[2:]


def _ffn_down(x1, f, w_down):
    s = x1.shape[0]
    tm = min(2 * ROW_TILE, s)

    def body(x1_ref, f_ref, wd_ref, x2_ref):
        x2_ref[...] = x1_ref[...] + _dot(f_ref[...], wd_ref[...])

    return pl.pallas_call(
        body, name="ffn_down", grid=(s // tm,),
        in_specs=[_row(0, (tm, D_MODEL)), _row(0, (tm, D_FF)), _full((D_FF, D_MODEL))],
        out_specs=_row(0, (tm, D_MODEL)), out_shape=jax.ShapeDtypeStruct((s, D_MODEL), f32),
        compiler_params=_cp("parallel"),
    )(x1, f, w_down)


def _ple_loss(x2, p, target, g_pg, g_post, g_final, w_pg, w_pu_g):
    s = x2.shape[0]
    tm = min(ROW_TILE, s)
    n = s // tm

    def body(x2_ref, p_ref, t_ref, gpg_ref, gpo_ref, gf_ref, wpg_ref, wpu_ref,
             dx2_ref, dwpg_ref, dwpu_ref, small_ref, apg_ref, apu_ref):
        i = pl.program_id(0)

        @pl.when(i == 0)
        def _():
            apg_ref[...] = jnp.zeros_like(apg_ref)
            apu_ref[...] = jnp.zeros_like(apu_ref)
            small_ref[...] = jnp.zeros_like(small_ref)

        x2 = x2_ref[...]
        h3, x2h, r2 = _rms_fwd(x2, gpg_ref[...])
        h3b = h3.astype(bf16)
        gate = _sigmoid(_dot(h3b, wpg_ref[...]))
        pb = p_ref[...].astype(bf16)
        pe = jnp.concatenate([_dot(pb, wpu_ref[j]) for j in range(N_CHIPS)], axis=-1)
        e, peh, rp = _rms_fwd(pe, gpo_ref[...])
        x3 = x2 + gate * e
        y, x3h, r3 = _rms_fwd(x3, gf_ref[...])
        d = y - t_ref[...]
        loss = 0.5 * jnp.sum(jnp.sum(d * d, axis=-1, keepdims=True) * (1.0 / D_MODEL), axis=0, keepdims=True)
        dx3, dgf = _rms_bwd(d * (1.0 / D_MODEL), x3h, r3, gf_ref[...])
        dpe, dgpo = _rms_bwd(dx3 * gate, peh, rp, gpo_ref[...])
        dgl = (dx3 * e * gate * (1.0 - gate)).astype(bf16)
        apg_ref[...] += _dot_tn(h3b, dgl)
        apu_ref[...] += _dot_tn(pb, dpe.astype(bf16))
        dh3 = _dot_nt(dgl, wpg_ref[...])
        dx2b, dgpg = _rms_bwd(dh3, x2h, r2, gpg_ref[...])
        dx2_ref[...] = dx3 + dx2b
        small_ref[0:1, :] += dgpg
        small_ref[1:2, :] += dgpo
        small_ref[2:3, :] += dgf
        small_ref[3:4, :] += jnp.broadcast_to(loss, (1, D_MODEL))

        @pl.when(i == n - 1)
        def _():
            dwpg_ref[...] = apg_ref[...].astype(bf16)
            for j in range(N_CHIPS):
                dwpu_ref[j] = apu_ref[:, j * PLE_CHUNK:(j + 1) * PLE_CHUNK].astype(bf16)

    return pl.pallas_call(
        body, name="ple_loss", grid=(n,),
        in_specs=[_row(0, (tm, D_MODEL)), _row(0, (tm, D_PLE)), _row(0, (tm, D_MODEL)),
                  _full((1, D_MODEL)), _full((1, D_MODEL)), _full((1, D_MODEL)),
                  _full((D_MODEL, D_MODEL)), _full((N_CHIPS, D_PLE, PLE_CHUNK))],
        out_specs=[_row(0, (tm, D_MODEL)), _full((D_MODEL, D_MODEL)), _full((N_CHIPS, D_PLE, PLE_CHUNK)),
                   _full((8, D_MODEL))],
        out_shape=[jax.ShapeDtypeStruct((s, D_MODEL), f32), jax.ShapeDtypeStruct((D_MODEL, D_MODEL), bf16),
                   jax.ShapeDtypeStruct((N_CHIPS, D_PLE, PLE_CHUNK), bf16), jax.ShapeDtypeStruct((8, D_MODEL), f32)],
        scratch_shapes=[pltpu.VMEM((D_MODEL, D_MODEL), f32), pltpu.VMEM((D_PLE, D_MODEL), f32)],
        compiler_params=_cp("arbitrary"),
    )(x2, p, target, g_pg, g_post, g_final, w_pg, w_pu_g)


def _crossed(parts):
    n = len(parts)
    return dict(in_specs=[ANY] * n, out_specs=[ANY] * n,
                out_shape=[jax.ShapeDtypeStruct(a.shape, a.dtype) for a in parts], scratch=_Cross.scratch(n))


def _ffn_bwd_dx(dx2, x1, gu, g_ffn, w_gu_g, w_down, grads):
    s = x1.shape[0]
    tm = min(ROW_TILE, s)
    n = s // tm
    nc = len(grads)
    cs = dict(in_specs=[ANY] * nc, out_specs=[ANY] * nc, scratch=_CrossAll.scratch(nc),
              out_shape=[jax.ShapeDtypeStruct((N_DEV, g.shape[1] // 2, g.shape[2]), g.dtype) for g in grads])

    def body(dx2_ref, x1_ref, gu_ref, g_ref, wg_ref, wu_ref, wd_ref, *refs):
        dx1_ref, dgu_ref, small_ref = refs[nc:nc + 3]
        acc_ref = refs[2 * nc + 3]
        cross = _CrossAll(refs[:nc], refs[nc + 3:2 * nc + 3], *refs[2 * nc + 4:])
        i, c = pl.program_id(0), pl.program_id(1)
        everyone = [(cross.x, cross.y, 1 - cross.c)] + [(ox, oy, h) for ox, oy in cross.others for h in range(2)]
        pl.when(jnp.logical_and(i == 0, c == 0))(functools.partial(_handshake, everyone))
        pl.when(jnp.logical_and(i == 0, c == 0))(cross.start)

        @pl.when(jnp.logical_and(i == 0, c == 0))
        def _():
            small_ref[...] = jnp.zeros_like(small_ref)

        @pl.when(c == 0)
        def _():
            acc_ref[...] = jnp.zeros_like(acc_ref)

        dyb = dx2_ref[...].astype(bf16)
        for lo, hi in zip(FF_SUB[:-1], FF_SUB[1:]):
            df = _dot_nt(dyb, wd_ref[lo:hi, :])
            gate = gu_ref[0, :, lo:hi].astype(f32)
            up = gu_ref[1, :, lo:hi].astype(f32)
            sg = _sigmoid(gate)
            dgate = (df * up * _silu_grad(gate, sg)).astype(bf16)
            dup = (df * gate * sg).astype(bf16)
            dgu_ref[0, :, lo:hi] = dgate
            dgu_ref[1, :, lo:hi] = dup
            acc_ref[...] += _dot_nt(dgate, wg_ref[0, :, lo:hi]) + _dot_nt(dup, wu_ref[0, :, lo:hi])

        @pl.when(c == 1)
        def _():
            _, x1h, r1 = _rms_fwd(x1_ref[...], g_ref[...])
            dx1b, dg = _rms_bwd(acc_ref[...], x1h, r1, g_ref[...])
            dx1_ref[...] = dx2_ref[...] + dx1b
            small_ref[0:1, :] += dg

        pl.when(jnp.logical_and(i == n - 1, c == 1))(cross.finish)

    res = pl.pallas_call(
        body, name="ffn_bwd_dx", grid=(n, 2),
        in_specs=[_row(0, (tm, D_MODEL)), _row(0, (tm, D_MODEL)),
                  pl.BlockSpec((2, tm, FF_CHUNK), lambda i, c: (0, i, c)), _full((1, D_MODEL)),
                  pl.BlockSpec((1, D_MODEL, FF_CHUNK), lambda i, c: (c, 0, 0)),
                  pl.BlockSpec((1, D_MODEL, FF_CHUNK), lambda i, c: (2 + c, 0, 0)),
                  pl.BlockSpec((FF_CHUNK, D_MODEL), lambda i, c: (c, 0))] + cs["in_specs"],
        out_specs=[_row(0, (tm, D_MODEL)), pl.BlockSpec((2, tm, FF_CHUNK), lambda i, c: (0, i, c)),
                   _full((8, D_MODEL))] + cs["out_specs"],
        out_shape=[jax.ShapeDtypeStruct((s, D_MODEL), f32), jax.ShapeDtypeStruct((2, s, D_FF), bf16),
                   jax.ShapeDtypeStruct((8, D_MODEL), f32)] + cs["out_shape"],
        scratch_shapes=[pltpu.VMEM((tm, D_MODEL), f32)] + cs["scratch"],
        compiler_params=_cp("arbitrary", "arbitrary", collective_id=8),
    )(dx2, x1, gu, g_ffn, w_gu_g, w_gu_g, w_down, *grads)
    return res[:3], res[3:]


def _ffn_bwd_dw_gu(h2, dgu):
    s = h2.shape[0]
    ts = min(2 * ROW_TILE, s)
    n = s // ts

    def body(h_ref, d_ref, o_ref, acc_ref):
        t = pl.program_id(1)

        @pl.when(t == 0)
        def _():
            acc_ref[...] = jnp.zeros_like(acc_ref)

        acc_ref[...] += _dot_tn(h_ref[...], d_ref[0])

        @pl.when(t == n - 1)
        def _():
            o_ref[0] = acc_ref[...].astype(bf16)

    return pl.pallas_call(
        body, name="ffn_bwd_dw_gu", grid=(N_CHIPS, n),
        in_specs=[pl.BlockSpec((ts, D_MODEL), lambda j, t: (t, 0)),
                  pl.BlockSpec((1, ts, FF_CHUNK), lambda j, t: (j // 2, t, j % 2))],
        out_specs=pl.BlockSpec((1, D_MODEL, FF_CHUNK), lambda j, t: (j, 0, 0)),
        out_shape=jax.ShapeDtypeStruct((N_CHIPS, D_MODEL, FF_CHUNK), bf16),
        scratch_shapes=[pltpu.VMEM((D_MODEL, FF_CHUNK), f32)],
        compiler_params=_cp("parallel", "arbitrary"),
    )(h2, dgu)


def _ffn_bwd_dw_down(f, dx2):
    s = dx2.shape[0]
    ts = min(2 * ROW_TILE, s)
    n = s // ts

    def body(f_ref, d_ref, o_ref, acc_ref):
        t = pl.program_id(1)

        @pl.when(t == 0)
        def _():
            acc_ref[...] = jnp.zeros_like(acc_ref)

        acc_ref[...] += _dot_tn(f_ref[...], d_ref[...].astype(bf16))

        @pl.when(t == n - 1)
        def _():
            o_ref[...] = acc_ref[...].astype(bf16)

    return pl.pallas_call(
        body, name="ffn_bwd_dw_down", grid=(2, n),
        in_specs=[pl.BlockSpec((ts, FF_CHUNK), lambda c, t: (t, c)),
                  pl.BlockSpec((ts, D_MODEL), lambda c, t: (t, 0))],
        out_specs=pl.BlockSpec((FF_CHUNK, D_MODEL), lambda c, t: (c, 0)),
        out_shape=jax.ShapeDtypeStruct((D_FF, D_MODEL), bf16),
        scratch_shapes=[pltpu.VMEM((FF_CHUNK, D_MODEL), f32)],
        compiler_params=_cp("parallel", "arbitrary"),
    )(f, dx2)


def _mix_bwd_local(dx1, mix, u1, pooled, w_out, ln_g, ln_b, pool_w, pool_scale):
    s = dx1.shape[0]
    tm = min(ROW_TILE, s)
    n = s // tm

    def body(dx1_ref, mix_ref, u1_ref, po_ref, wo_ref, lg_ref, lb_ref, pw_ref, ps_ref,
             du1_ref, dpo_ref, dwo_ref, dpw_ref, small_ref, awo_ref):
        i = pl.program_id(0)

        @pl.when(i == 0)
        def _():
            awo_ref[...] = jnp.zeros_like(awo_ref)
            dpw_ref[...] = jnp.zeros_like(dpw_ref)
            small_ref[...] = jnp.zeros_like(small_ref)

        dyb = dx1_ref[...].astype(bf16)
        dmix = _dot_nt(dyb, wo_ref[...])
        awo_ref[...] += _dot_tn(mix_ref[...], dyb)
        u1 = u1_ref[...]
        mu = jnp.mean(u1, axis=-1, keepdims=True)
        uc = u1 - mu
        rstd = lax.rsqrt(jnp.mean(uc * uc, axis=-1, keepdims=True) + EPS)
        uh = uc * rstd
        u2 = uh * lg_ref[...] + lb_ref[...]
        du2 = dmix[:, :C_CONV] * _silu_grad(u2, _sigmoid(u2))
        duh = du2 * lg_ref[...]
        du1 = rstd * (duh - jnp.mean(duh, axis=-1, keepdims=True) - uh * jnp.mean(duh * uh, axis=-1, keepdims=True))
        du1_ref[...] = du1
        small_ref[0:1, :] += jnp.sum(du2 * uh, axis=0, keepdims=True)
        small_ref[1:2, :] += jnp.sum(du2, axis=0, keepdims=True)
        small_ref[2:3, :] += jnp.sum(du1, axis=0, keepdims=True)
        for g in range(len(POOL_WINDOWS)):
            cols = slice(g * POOL_GROUP, (g + 1) * POOL_GROUP)
            dq = dmix[:, C_CONV + g * POOL_GROUP:C_CONV + (g + 1) * POOL_GROUP]
            pwb = pw_ref[g].astype(bf16)
            pg = po_ref[:, cols]
            mixed = _dot(pg, pwb)
            small_ref[3:4, cols] += jnp.sum(dq * mixed, axis=0, keepdims=True)
            dmixed = (dq * ps_ref[:, cols]).astype(bf16)
            dpw_ref[g] += _dot_tn(pg, dmixed)
            dpo_ref[:, cols] = _dot_nt(dmixed, pwb)

        @pl.when(i == n - 1)
        def _():
            dwo_ref[...] = awo_ref[...].astype(bf16)

    return pl.pallas_call(
        body, name="mix_bwd_local", grid=(n,),
        in_specs=[_row(0, (tm, D_MODEL)), _row(0, (tm, D_MODEL)), _row(0, (tm, C_CONV)), _row(0, (tm, C_POOL)),
                  _full((D_MODEL, D_MODEL)), _full((1, C_CONV)), _full((1, C_CONV)),
                  _full((4, POOL_GROUP, POOL_GROUP)), _full((1, C_POOL))],
        out_specs=[_row(0, (tm, C_CONV)), _row(0, (tm, C_POOL)), _full((D_MODEL, D_MODEL)),
                   _full((4, POOL_GROUP, POOL_GROUP)), _full((8, C_CONV))],
        out_shape=[jax.ShapeDtypeStruct((s, C_CONV), f32), jax.ShapeDtypeStruct((s, C_POOL), f32),
                   jax.ShapeDtypeStruct((D_MODEL, D_MODEL), bf16),
                   jax.ShapeDtypeStruct((4, POOL_GROUP, POOL_GROUP), f32), jax.ShapeDtypeStruct((8, C_CONV), f32)],
        scratch_shapes=[pltpu.VMEM((D_MODEL, D_MODEL), f32)],
        compiler_params=_cp("arbitrary"),
    )(dx1, mix, u1, pooled, w_out, ln_g, ln_b, pool_w, pool_scale)


def _in_bwd(du1, dpo, z, x, dx1, conv_w, g_mix, w_in_g, parts, small_parts):
    s = x.shape[0]
    tm = min(ROW_TILE, s)
    n = s // tm
    hb = tm // HALO
    last = s // HALO - 1
    nv = tm // 8
    assert nv >= SHIFT_PAD and nv % SHIFT_GROUPS == 0
    nc, ns = len(parts), len(small_parts)
    cs = _crossed(parts)

    def body(du_ref, dun_ref, dp_ref, dpn_ref, z_ref, zp_ref, x_ref, dx1_ref, cw_ref, g_ref, w_ref, *refs):
        outs = refs[nc + ns:]
        gx_ref, dw_ref, dcw_ref, small_ref = outs[:4]
        eu, ed, ep, ss, u0_ref, dz_ref, acc_ref, dcw_acc = outs[4 + nc + ns:12 + nc + ns]
        sems = outs[12 + nc + ns:]
        cross = _Cross(refs[:nc], outs[4:4 + nc], sems[0], sems[1])
        gather = _SmallGather(refs[nc:nc + ns], outs[4 + nc:4 + nc + ns], sems[2], sems[3])
        i = pl.program_id(0)
        pl.when(i == 0)(functools.partial(_handshake, cross.peers()))
        pl.when(i == 0)(cross.start)
        pl.when(i == 0)(gather.start)

        @pl.when(i == 0)
        def _():
            acc_ref[...] = jnp.zeros_like(acc_ref)
            dcw_acc[...] = jnp.zeros_like(dcw_acc)
            small_ref[...] = jnp.zeros_like(small_ref)

        keep_prev = (i > 0).astype(f32)
        keep_next = (i < n - 1).astype(f32)
        zp = zp_ref[...] * keep_prev
        u0_prev = zp[:, :C_CONV] * _sigmoid(zp[:, C_CONV:2 * C_CONV])
        u0_ref[...] = z_ref[:, :C_CONV] * _sigmoid(z_ref[:, C_CONV:2 * C_CONV])
        du_next = dun_ref[...] * keep_next
        for c, w_pool in enumerate(POOL_WINDOWS):
            lanes = slice(c * 128, (c + 1) * 128)
            eu[c, pl.ds(0, SHIFT_PAD, stride=8), :] = u0_prev[:, lanes]
            ed[c, pl.ds(nv * 8 + 7, SHIFT_PAD, stride=8), :] = du_next[:, lanes]
            ep[c, pl.ds(nv * 8 + 7, SHIFT_PAD, stride=8), :] = (
                dpn_ref[:, lanes] * keep_next / _pool_counts(HALO, w_pool, (i + 1) * tm))
            for j in range(8):
                rows = slice(j * nv, (j + 1) * nv)
                eu[c, pl.ds(SHIFT_PAD * 8 + j, nv, stride=8), :] = u0_ref[rows, lanes]
                ed[c, pl.ds(j, nv, stride=8), :] = du_ref[rows, lanes]
                ep[c, pl.ds(j, nv, stride=8), :] = dp_ref[rows, lanes] / _pool_counts(nv, w_pool, i * tm + j * nv)
                if j >= 1:
                    eu[c, pl.ds(j, SHIFT_PAD, stride=8), :] = u0_ref[j * nv - SHIFT_PAD:j * nv, lanes]
                if j <= 6:
                    edge = slice((j + 1) * nv, (j + 1) * nv + SHIFT_PAD)
                    ed[c, pl.ds(nv * 8 + j, SHIFT_PAD, stride=8), :] = du_ref[edge, lanes]
                    ep[c, pl.ds(nv * 8 + j, SHIFT_PAD, stride=8), :] = (
                        dp_ref[edge, lanes] / _pool_counts(SHIFT_PAD, w_pool, i * tm + (j + 1) * nv))
        for c, w_pool in enumerate(POOL_WINDOWS):
            lanes = slice(c * 128, (c + 1) * 128)
            b_lanes = slice(C_CONV + c * 128, C_CONV + (c + 1) * 128)
            v_lanes = slice(2 * C_CONV + c * 128, 2 * C_CONV + (c + 1) * 128)
            for v0 in range(0, nv, SHIFT_GROUPS):
                span = SHIFT_GROUPS * 8
                acc = jnp.zeros((span, 128), f32)
                for k in range(CONV_K):
                    acc = acc + cw_ref[pl.ds(k, 1), lanes] * ed[c, pl.ds((v0 + CONV_K - 1 - k) * 8, span), :]
                ss[0, v0 * 8:v0 * 8 + span, :] = acc
                acc = ep[c, pl.ds(v0 * 8, span), :]
                for d in range(1, w_pool):
                    acc = acc + ep[c, pl.ds((v0 + d) * 8, span), :]
                ss[1, v0 * 8:v0 * 8 + span, :] = acc
                d1 = ed[c, pl.ds(v0 * 8, span), :]
                for k in range(CONV_K):
                    prod = d1 * eu[c, pl.ds((SHIFT_PAD - (CONV_K - 1) + v0 + k) * 8, span), :]
                    fold = prod[0:8]
                    for r in range(8, span, 8):
                        fold = fold + prod[r:r + 8]
                    dcw_acc[k, :, lanes] += fold
            for j in range(8):
                rows = slice(j * nv, (j + 1) * nv)
                du0 = ss[0, pl.ds(j, nv, stride=8), :]
                av, sv = z_ref[rows, lanes], _sigmoid(z_ref[rows, b_lanes])
                dz_ref[rows, lanes] = (du0 * sv).astype(bf16)
                dz_ref[rows, b_lanes] = (du0 * av * sv * (1.0 - sv)).astype(bf16)
                dz_ref[rows, v_lanes] = (ss[1, pl.ds(j, nv, stride=8), :] - dp_ref[rows, lanes]).astype(bf16)
        h, xh, r = _rms_fwd(x_ref[...], g_ref[...])
        dz = dz_ref[...]
        acc_ref[...] += _dot_tn(h.astype(bf16), dz)
        dh = _dot_nt(dz[:, 0:W_IN_CHUNK], w_ref[0])
        for j in range(1, N_CHIPS):
            dh = dh + _dot_nt(dz[:, j * W_IN_CHUNK:(j + 1) * W_IN_CHUNK], w_ref[j])
        dxb, dg = _rms_bwd(dh, xh, r, g_ref[...])
        gx_ref[...] = dx1_ref[...] + dxb
        small_ref[0:1, :] += dg

        @pl.when(i == n - 1)
        def _():
            for j in range(N_CHIPS):
                dw_ref[j] = acc_ref[:, j * W_IN_CHUNK:(j + 1) * W_IN_CHUNK].astype(bf16)
            dcw_ref[...] = jnp.sum(dcw_acc[...], axis=1)

        pl.when(i == n - 1)(gather.finish)
        pl.when(i == n - 1)(cross.finish)

    nxt = lambda i: (jnp.minimum((i + 1) * hb, last), 0)
    res = pl.pallas_call(
        body, name="in_bwd", grid=(n,),
        in_specs=[_row(0, (tm, C_CONV)), pl.BlockSpec((HALO, C_CONV), nxt),
                  _row(0, (tm, C_POOL)), pl.BlockSpec((HALO, C_POOL), nxt),
                  _row(0, (tm, W_IN_COLS)),
                  pl.BlockSpec((HALO, W_IN_COLS), lambda i: (jnp.maximum(i * hb - 1, 0), 0)),
                  _row(0, (tm, D_MODEL)), _row(0, (tm, D_MODEL)), _full((HALO, C_CONV)), _full((1, D_MODEL)),
                  _full((N_CHIPS, D_MODEL, W_IN_CHUNK))] + cs["in_specs"] + [ANY] * ns,
        out_specs=[pl.BlockSpec((None, tm, D_MODEL), lambda i: (0, i, 0)), _full((N_CHIPS, D_MODEL, W_IN_CHUNK)),
                   _full((HALO, C_CONV)),
                   _full((8, D_MODEL))] + cs["out_specs"] + [ANY] * ns,
        out_shape=[jax.ShapeDtypeStruct((1, s, D_MODEL), f32), jax.ShapeDtypeStruct((N_CHIPS, D_MODEL, W_IN_CHUNK), bf16),
                   jax.ShapeDtypeStruct((HALO, C_CONV), f32), jax.ShapeDtypeStruct((8, D_MODEL), f32)] + cs["out_shape"]
        + _gathered_shapes(small_parts),
        scratch_shapes=[pltpu.VMEM((4, (SHIFT_PAD + nv) * 8, 128), f32), pltpu.VMEM((4, (nv + SHIFT_PAD) * 8, 128), f32),
                        pltpu.VMEM((4, (nv + SHIFT_PAD) * 8, 128), f32), pltpu.VMEM((2, tm, 128), f32),
                        pltpu.VMEM((tm, C_CONV), f32), pltpu.VMEM((tm, W_IN_COLS), bf16),
                        pltpu.VMEM((D_MODEL, W_IN_COLS), f32), pltpu.VMEM((HALO, 8, C_CONV), f32)] + cs["scratch"]
        + _SmallGather.scratch(ns),
        compiler_params=_cp("arbitrary", collective_id=6),
    )(du1, du1, dpo, dpo, z, z, x, dx1, conv_w, g_mix, w_in_g, *parts, *small_parts)
    return res[:4], res[4:4 + nc], res[4 + nc:]


def _sibling_handshake(x, y, c):
    barrier = pltpu.get_barrier_semaphore()
    pl.semaphore_signal(barrier, inc=1, device_id=(x, y, 1 - c), device_id_type=MESH)
    pl.semaphore_wait(barrier, 1)


def _pair_reduce(name, collective_id, grads):
    nk = len(grads)
    halves = [g.shape[1] // 2 for g in grads]

    def body(*refs):
        ins, outs, got, own = refs[:nk], refs[nk:2 * nk], refs[2 * nk:3 * nk], refs[3 * nk:4 * nk]
        send_sems, recv_sems, load_sems = refs[4 * nk:]
        x, y, c, _, _ = _place()
        _sibling_handshake(x, y, c)

        def half(k, core):
            return ins[k].at[:, pl.ds(core * halves[k], halves[k]), :]

        cps = [_remote(half(k, 1 - c), got[k], send_sems.at[k], recv_sems.at[k], (x, y, 1 - c)) for k in range(nk)]
        loads = [pltpu.make_async_copy(half(k, c), own[k], load_sems.at[k]) for k in range(nk)]
        for cp in cps + loads:
            cp.start()
        for k in range(nk):
            loads[k].wait()
            cps[k].wait_recv()
            outs[k][...] = (own[k][...].astype(f32) + got[k][...].astype(f32)).astype(bf16)
        for cp in cps:
            cp.wait_send()

    shapes = [(N_CHIPS, h, g.shape[2]) for g, h in zip(grads, halves)]
    return pl.pallas_call(
        body, name=f"pair_reduce_{name}", in_specs=[ANY] * nk, out_specs=[VMEM] * nk,
        out_shape=[jax.ShapeDtypeStruct(s, bf16) for s in shapes],
        scratch_shapes=[pltpu.VMEM(s, bf16) for s in shapes] * 2
        + [pltpu.SemaphoreType.DMA((nk,)), pltpu.SemaphoreType.DMA((nk,)), pltpu.SemaphoreType.DMA((nk,))],
        compiler_params=pltpu.CompilerParams(vmem_limit_bytes=VMEM_LIMIT, collective_id=collective_id),
    )(*grads)


class _Cross:
    def __init__(self, parts, landed, send_sems, recv_sems):
        self.parts, self.landed, self.send_sems, self.recv_sems = parts, landed, send_sems, recv_sems
        self.x, self.y, self.c, self.me, self.others = _place()

    def peers(self):
        return [(self.x, self.y, 1 - self.c)] + [(ox, oy, self.c) for ox, oy in self.others]

    def _copy(self, k, j, src_chunk, dst_slot):
        ox, oy = self.others[j]
        return _remote(self.parts[k].at[src_chunk], self.landed[k].at[dst_slot], self.send_sems.at[3 * k + j],
                       self.recv_sems.at[3 * k + j], (ox, oy, self.c))

    def _each(self):
        return [(k, j, 2 * self.others[j][0] + self.others[j][1]) for k in range(len(self.parts)) for j in range(3)]

    def start(self):
        for k, j, chip in self._each():
            self._copy(k, j, chip, self.me).start()

    def finish(self):
        for k, j, chip in self._each():
            self._copy(k, j, chip, chip).wait_recv()
        for k, j, chip in self._each():
            self._copy(k, j, chip, self.me).wait_send()

    @staticmethod
    def scratch(n):
        return [pltpu.SemaphoreType.DMA((3 * n,)), pltpu.SemaphoreType.DMA((3 * n,))]


class _CrossAll:
    def __init__(self, grads, landed, send_sems, recv_sems):
        self.grads, self.landed, self.send_sems, self.recv_sems = grads, landed, send_sems, recv_sems
        self.x, self.y, self.c, self.me, self.others = _place()
        self.dev = 2 * self.me + self.c

    def _piece(self, k, chip, half):
        rows = self.grads[k].shape[1] // 2
        return self.grads[k].at[chip, pl.ds(half * rows, rows), :]

    def _to_sibling(self, k):
        return _remote(self._piece(k, self.me, 1 - self.c), self.landed[k].at[self.dev], self.send_sems.at[7 * k],
                       self.recv_sems.at[7 * k], (self.x, self.y, 1 - self.c))

    def _to_chip(self, k, j, half):
        ox, oy = self.others[j]
        return _remote(self._piece(k, 2 * ox + oy, half), self.landed[k].at[self.dev],
                       self.send_sems.at[7 * k + 1 + 2 * j + half], self.recv_sems.at[7 * k + 1 + 2 * j + self.c],
                       (ox, oy, half))

    def _from(self, k, sem, slot):
        return _remote(self.landed[k].at[slot], self.landed[k].at[slot], self.send_sems.at[7 * k + sem],
                       self.recv_sems.at[7 * k + sem], (self.x, self.y, 1 - self.c))

    def start(self):
        for k in range(len(self.grads)):
            self._to_sibling(k).start()
            for j in range(3):
                for half in range(2):
                    self._to_chip(k, j, half).start()

    def finish(self):
        for k in range(len(self.grads)):
            self._from(k, 0, 2 * self.me + 1 - self.c).wait_recv()
            for j, (ox, oy) in enumerate(self.others):
                for core in range(2):
                    self._from(k, 1 + 2 * j + core, 4 * ox + 2 * oy + core).wait_recv()
        for k in range(len(self.grads)):
            self._to_sibling(k).wait_send()
            for j in range(3):
                for half in range(2):
                    self._to_chip(k, j, half).wait_send()

    @staticmethod
    def scratch(n):
        return [pltpu.SemaphoreType.DMA((7 * n,)), pltpu.SemaphoreType.DMA((7 * n,))]


def _rs_sum_chips(name, place, landed, part):
    _, half, cols = landed.shape

    def body(place_ref, l_ref, p_ref, o_ref):
        me = place_ref[0]
        own = p_ref[0].astype(f32)
        acc = jnp.where(me == 0, own, l_ref[0].astype(f32))
        for j in range(1, N_CHIPS):
            acc = acc + jnp.where(me == j, own, l_ref[j].astype(f32))
        o_ref[...] = acc

    return pl.pallas_call(
        body, name=f"rs_sum_chips_{name}",
        grid_spec=pltpu.PrefetchScalarGridSpec(
            num_scalar_prefetch=1, grid=(1,),
            in_specs=[pl.BlockSpec((N_CHIPS, half, cols), lambda t, place_ref: (0, 0, 0)),
                      pl.BlockSpec((1, half, cols), lambda t, place_ref: (place_ref[0], 0, 0))],
            out_specs=pl.BlockSpec((half, cols), lambda t, place_ref: (place_ref[1], 0))),
        out_shape=jax.ShapeDtypeStruct((2 * half, cols), f32),
        compiler_params=_cp("arbitrary"),
    )(place, landed, part)


def _grad_pair(shards):
    nk = len(shards)

    def body(*refs):
        outs = refs[nk:2 * nk]
        send_sems, recv_sems = refs[2 * nk:]
        x, y, c, _, _ = _place()
        _sibling_handshake(x, y, c)

        def half(k, core):
            h = outs[k].shape[0] // 2
            return outs[k].at[pl.ds(core * h, h), :]

        cps = [_remote(half(k, c), half(k, c), send_sems.at[k], recv_sems.at[k], (x, y, 1 - c)) for k in range(nk)]
        for cp in cps:
            cp.start()
        for k in range(nk):
            _remote(half(k, 1 - c), half(k, 1 - c), send_sems.at[k], recv_sems.at[k], (x, y, 1 - c)).wait_recv()
        for cp in cps:
            cp.wait_send()

    return pl.pallas_call(
        body, name="grad_pair", in_specs=[ANY] * nk, out_specs=[ANY] * nk,
        out_shape=[jax.ShapeDtypeStruct(a.shape, f32) for a in shards],
        input_output_aliases={k: k for k in range(nk)},
        scratch_shapes=[pltpu.SemaphoreType.DMA((nk,)), pltpu.SemaphoreType.DMA((nk,))],
        compiler_params=pltpu.CompilerParams(collective_id=2),
    )(*shards)


def _adam_math(w, g, m, v):
    m = ADAM_B1 * m + (1.0 - ADAM_B1) * g
    v = ADAM_B2 * v + (1.0 - ADAM_B2) * (g * g)
    m_hat = m / (1.0 - ADAM_B1 ** ADAM_STEP)
    v_hat = v / (1.0 - ADAM_B2 ** ADAM_STEP)
    delta = -ADAM_LR * (m_hat / (jnp.sqrt(v_hat) + ADAM_EPS) + ADAM_WD * w)
    return delta, m, v


def _adam(ws, gs, ms, vs):
    n = len(ws)
    steps = 2 * SHARD_STEPS

    def body(*refs):
        for k in range(n):
            w_ref, g_ref, m_ref, v_ref = (refs[j * n + k] for j in range(4))
            go_ref, d_ref, nm_ref, nv_ref = (refs[(4 + j) * n + k] for j in range(4))
            g = g_ref[...]
            go_ref[...] = g
            d_ref[...], nm_ref[...], nv_ref[...] = _adam_math(w_ref[...], g, m_ref[...], v_ref[...])

    specs = [_row(0, (w.shape[0] // steps, w.shape[1])) for w in ws]
    res = pl.pallas_call(
        body, name="adam_shards", grid=(steps,), in_specs=specs * 4, out_specs=specs * 4,
        out_shape=[jax.ShapeDtypeStruct(w.shape, f32) for w in ws] * 4,
        compiler_params=_cp("parallel"),
    )(*ws, *gs, *ms, *vs)
    return [tuple(res[j * n + k] for j in range(4)) for k in range(n)]


class _SmallGather:
    def __init__(self, parts, bufs, send_sems, recv_sems):
        self.parts, self.bufs, self.send_sems, self.recv_sems = parts, bufs, send_sems, recv_sems
        self.x, self.y, self.c, _, self.others = _place()
        self.sibling = (self.x, self.y, 1 - self.c)

    def _copy(self, a, k, block, to, src=None):
        slot = self.bufs[a].at[4 * block[0] + 2 * block[1] + block[2]]
        return _remote(slot if src is None else src, slot, self.send_sems.at[7 * a + k], self.recv_sems.at[7 * a + k],
                       to)

    def _first(self, a):
        me = (self.x, self.y, self.c)
        return [self._copy(a, 0, me, self.sibling, src=self.parts[a])] + [
            self._copy(a, 1 + j, me, (*chip, self.c), src=self.parts[a]) for j, chip in enumerate(self.others)]

    def _passed(self, a):
        return [self._copy(a, 4 + j, (*chip, self.c), self.sibling) for j, chip in enumerate(self.others)]

    @staticmethod
    def scratch(n):
        return [pltpu.SemaphoreType.DMA((7 * n,)), pltpu.SemaphoreType.DMA((7 * n,))]

    def start(self):
        for a in range(len(self.parts)):
            for cp in self._first(a):
                cp.start()

    def finish(self):
        sent = []
        for a in range(len(self.parts)):
            passed = self._passed(a)
            for j, chip in enumerate(self.others):
                self._copy(a, 1 + j, (*chip, self.c), self.sibling).wait_recv()
                passed[j].start()
            sent += self._first(a) + passed
        for a in range(len(self.parts)):
            self._copy(a, 0, self.sibling, self.sibling).wait_recv()
            for j, chip in enumerate(self.others):
                self._copy(a, 4 + j, (*chip, 1 - self.c), self.sibling).wait_recv()
        for cp in sent:
            cp.wait_send()


def _gathered_shapes(parts):
    return [jax.ShapeDtypeStruct((N_DEV,) + p.shape, p.dtype) for p in parts]


def _rs_sum_group(name, place, landed, parts, landed_all, grads_all, cross_parts, small_parts):
    nk, na, nx, ns = len(landed), len(landed_all), len(cross_parts), len(small_parts)
    dims = [a.shape[1:] for a in landed]
    dims_all = [a.shape[1:] for a in landed_all]

    def body(place_ref, *refs):
        take = iter(refs)
        l_refs, p_refs, la_refs, ga_refs, x_refs, sp_refs, o_refs, oa_refs, xl_refs, sbufs = (
            [next(take) for _ in range(cnt)] for cnt in (nk, nk, na, na, nx, ns, nk, na, nx, ns))
        sems = list(take)
        cross = _Cross(x_refs, xl_refs, sems[0], sems[1])
        small = _SmallGather(sp_refs, sbufs, sems[2], sems[3])
        t = pl.program_id(0)

        @pl.when(t == 0)
        def _():
            _handshake(cross.peers())
            cross.start()
            small.start()

        me = place_ref[0]
        for l_ref, p_ref, o_ref in zip(l_refs, p_refs, o_refs):
            own = p_ref[0].astype(f32)
            acc = jnp.where(me == 0, own, l_ref[0].astype(f32))
            for j in range(1, N_CHIPS):
                acc = acc + jnp.where(me == j, own, l_ref[j].astype(f32))
            o_ref[...] = acc
        dev = 2 * me + place_ref[1]
        for l_ref, g_ref, o_ref in zip(la_refs, ga_refs, oa_refs):
            own = g_ref[0].astype(f32)
            acc = jnp.where(dev == 0, own, l_ref[0].astype(f32))
            for d in range(1, N_DEV):
                acc = acc + jnp.where(dev == d, own, l_ref[d].astype(f32))
            o_ref[...] = acc

        @pl.when(t == 1)
        def _():
            small.finish()
            cross.finish()

    def halves(h, c, lead, index):
        return pl.BlockSpec((lead, h // 2, c) if lead else (h // 2, c), index)

    in_specs = [halves(h, c, N_CHIPS, lambda t, pr: (0, t, 0)) for h, c in dims]
    in_specs += [halves(h, c, 1, lambda t, pr: (pr[0], t, 0)) for h, c in dims]
    in_specs += [halves(h, c, N_DEV, lambda t, pr: (0, t, 0)) for h, c in dims_all]
    in_specs += [halves(h, c, 1, lambda t, pr: (pr[0], 2 * pr[1] + t, 0)) for h, c in dims_all]
    in_specs += [ANY] * (nx + ns)
    out_specs = [halves(h, c, 0, lambda t, pr: (2 * pr[1] + t, 0)) for h, c in dims + dims_all] + [ANY] * (nx + ns)
    out_shape = [jax.ShapeDtypeStruct((2 * h, c), f32) for h, c in dims + dims_all]
    out_shape += [jax.ShapeDtypeStruct(a.shape, a.dtype) for a in cross_parts] + _gathered_shapes(small_parts)
    res = pl.pallas_call(
        body, name=f"rs_sum_{name}",
        grid_spec=pltpu.PrefetchScalarGridSpec(
            num_scalar_prefetch=1, grid=(2,), in_specs=in_specs, out_specs=out_specs,
            scratch_shapes=_Cross.scratch(nx) + _SmallGather.scratch(ns)),
        out_shape=out_shape, compiler_params=_cp("arbitrary", collective_id=7),
    )(place, *landed, *parts, *landed_all, *grads_all, *cross_parts, *small_parts)
    return res[:nk], res[nk:nk + na], res[nk + na:nk + na + nx], res[nk + na + nx:]


SMALL_PARAMS = ("g_ple_gate", "g_ple_post", "g_final", "g_ffn", "ln_g", "ln_b", "conv_b", "pool_scale", "pool_w",
                "conv_w", "g_mix")
SMALL_ROWS = {"g_ple_gate": (0, 0), "g_ple_post": (0, 1), "g_final": (0, 2), "g_ffn": (1, 0), "ln_g": (2, 0),
              "ln_b": (2, 1), "conv_b": (2, 2), "pool_scale": (2, 3), "g_mix": (5, 0)}
LOSS_ROW = (0, 3)


def _small_adam(place, gathered, parts, params):
    nb, names = len(parts), SMALL_PARAMS
    flat = [a for nm in names for a in params[nm]]

    def body(place_ref, *refs):
        b_refs, p_refs = refs[:nb], refs[nb:2 * nb]
        w_refs = refs[2 * nb:2 * nb + 3 * len(names)]
        outs = refs[2 * nb + 3 * len(names):]
        loss_ref, o_refs, cw_sum = outs[0], outs[1:1 + 4 * len(names)], outs[1 + 4 * len(names)]
        chip = place_ref[0]
        me = 2 * chip + place_ref[1]

        def total(blk, idx):
            own = p_refs[blk][idx]
            g = jnp.where(me == 0, own, b_refs[blk][(0,) + idx])
            for d in range(1, N_DEV):
                g = g + jnp.where(me == d, own, b_refs[blk][(d,) + idx])
            return g

        everything = (slice(None), slice(None))
        loss_ref[...] = total(LOSS_ROW[0], (pl.ds(LOSS_ROW[1], 1), pl.ds(0, 128)))
        d_cw = total(4, everything)
        mine = jnp.where(chip == 0, d_cw[:, 0:128], 0.0)
        for j in range(1, N_CHIPS):
            mine = mine + jnp.where(chip == j, d_cw[:, j * 128:(j + 1) * 128], 0.0)
        cw_sum[...] = mine
        for k, nm in enumerate(names):
            w_ref, m_ref, v_ref = w_refs[3 * k:3 * k + 3]
            g_ref, d_ref, nm_ref, nv_ref = o_refs[4 * k:4 * k + 4]
            if nm == "pool_w":
                g = total(3, everything + (slice(None),))
            elif nm == "conv_w":
                g = cw_sum[pl.ds(0, CONV_K), :]
            else:
                blk, row = SMALL_ROWS[nm]
                g = total(blk, (pl.ds(row, 1), slice(None)))
            g_ref[...] = g
            d_ref[...], nm_ref[...], nv_ref[...] = _adam_math(w_ref[...], g, m_ref[...], v_ref[...])

    whole = lambda a: pl.BlockSpec(a.shape, lambda t, pr: (0,) * a.ndim)
    out_shape = [jax.ShapeDtypeStruct((1, 128), f32)]
    out_shape += [jax.ShapeDtypeStruct(params[nm][0].shape, f32) for nm in names for _ in range(4)]
    res = pl.pallas_call(
        body, name="small_adam",
        grid_spec=pltpu.PrefetchScalarGridSpec(
            num_scalar_prefetch=1, grid=(1,),
            in_specs=[whole(a) for a in list(gathered) + list(parts) + flat],
            out_specs=[whole(s) for s in out_shape], scratch_shapes=[pltpu.VMEM((HALO, 128), f32)]),
        out_shape=out_shape, compiler_params=_cp("arbitrary"),
    )(place, *gathered, *parts, *flat)
    return res[0], {nm: res[1 + 4 * k:5 + 4 * k] for k, nm in enumerate(names)}


def _pad_rows(a, rows):
    return jnp.concatenate([a, jnp.zeros((rows - a.shape[0],) + a.shape[1:], a.dtype)], axis=0)


def kernel(x, p, g_mix, w_in, conv_w, conv_b, ln_g, ln_b, pool_w, pool_scale, w_out, g_ffn, w_gate_up, w_down, g_ple_gate, w_ple_gate, w_ple_up, g_ple_post, g_final, loss_target, m_g_mix, m_w_in, m_conv_w, m_conv_b, m_ln_g, m_ln_b, m_pool_w, m_pool_scale, m_w_out, m_g_ffn, m_w_gate_up, m_w_down, m_g_ple_gate, m_w_ple_gate, m_w_ple_up, m_g_ple_post, m_g_final, v_g_mix, v_w_in, v_conv_w, v_conv_b, v_ln_g, v_ln_b, v_pool_w, v_pool_scale, v_w_out, v_g_ffn, v_w_gate_up, v_w_down, v_g_ple_gate, v_w_ple_gate, v_w_ple_up, v_g_ple_post, v_g_final):
    seq = x.shape[1]
    me = 2 * lax.axis_index("x") + lax.axis_index("y")
    chip = me.astype(jnp.int32).reshape(1)
    core = lax.axis_index("c").astype(jnp.int32).reshape(1)
    place = jnp.concatenate([chip, core])
    xs, ps, ts = x.reshape(seq, D_MODEL), p.reshape(seq, D_PLE), loss_target.reshape(seq, D_MODEL)

    big = [w_in[0], w_gate_up[0], w_out[0], w_down[0], w_ple_gate[0], w_ple_up[0]]
    big_m = [m_w_in[0], m_w_gate_up[0], m_w_out[0], m_w_down[0], m_w_ple_gate[0], m_w_ple_up[0]]
    big_v = [v_w_in[0], v_w_gate_up[0], v_w_out[0], v_w_down[0], v_w_ple_gate[0], v_w_ple_up[0]]
    b_in, b_gu, b_out, b_down, b_pg, b_pu, b_cw = _cast_into_slots(
        chip, big + [_pad_rows(conv_w[0], HALO)], [bf16] * len(big) + [f32])
    xi, yi = lax.axis_index("x"), lax.axis_index("y")
    order = jnp.stack([me, 2 * (1 - xi) + yi, 2 * xi + 1 - yi, 2 * (1 - xi) + 1 - yi]).astype(jnp.int32)

    z, (w_in_g, cw_g, w_out_g) = _mix_in(xs, g_mix, order, [b_in, b_cw, b_out])
    conv_w_f = cw_g.transpose(1, 0, 2).reshape(HALO, C_CONV)
    w_out_f = w_out_g.reshape(D_MODEL, D_MODEL)
    (x1, mix, u1, pooled, h2), (w_gu_g,) = _conv_pool_out(z, xs, conv_w_f, conv_b, ln_g, ln_b, pool_w[0], pool_scale,
                                                          w_out_f, g_ffn, [b_gu])
    (gu, ffn_f), (w_down_g, w_pg_g, w_pu_g) = _ffn_up(h2, w_gu_g, [b_down, b_pg, b_pu])
    w_down_f = w_down_g.reshape(D_FF, D_MODEL)
    x2 = _ffn_down(x1, ffn_f, w_down_f)
    w_pg_f = w_pg_g.reshape(D_MODEL, D_MODEL)
    dx2, d_w_pg, d_w_pu, small_ple = _ple_loss(x2, ps, ts, g_ple_gate, g_ple_post, g_final.reshape(1, D_MODEL),
                                               w_pg_f, w_pu_g)
    d_w_down = _ffn_bwd_dw_down(ffn_f, dx2)
    grads_a = [d_w_down.reshape(N_CHIPS, -1, D_MODEL), d_w_pg.reshape(N_CHIPS, -1, D_MODEL), d_w_pu]
    (dx1, dgu, small_ffn), landed_a = _ffn_bwd_dx(dx2, x1, gu, g_ffn, w_gu_g, w_down_f, grads_a)
    d_w_gu = _ffn_bwd_dw_gu(h2, dgu)
    du1, dpo, d_w_out, d_pool_w, small_mix = _mix_bwd_local(dx1, mix, u1, pooled, w_out_f, ln_g, ln_b, pool_w[0],
                                                             pool_scale)
    parts_b = _pair_reduce("b", 0, [d_w_gu, d_w_out.reshape(N_CHIPS, -1, D_MODEL)])
    small_0 = [small_ple, small_ffn, small_mix, d_pool_w]
    (grad_x, d_w_in, d_conv_w, small_in), landed_b, small_all_0 = _in_bwd(du1, dpo, z, xs, dx1, conv_w_f, g_mix,
                                                                            w_in_g, parts_b, small_0)
    parts_c = _pair_reduce("c", 1, [d_w_in])
    small_1 = [d_conv_w, small_in]
    (h_gu, h_out), (h_down, h_pg, h_pu), landed_c, small_all_1 = _rs_sum_group(
        "ab", place, landed_b, parts_b, landed_a, grads_a, parts_c, small_1)
    h_in = _rs_sum_chips("w_in", place, landed_c[0], parts_c[0])
    big_g = _grad_pair([h_in, h_gu, h_out, h_down, h_pg, h_pu])
    big_upd = _adam(big, big_g, big_m, big_v)

    p3 = lambda w, m, v: (w, m, v)
    row = lambda a: a.reshape(1, D_MODEL)
    params = dict(
        g_ple_gate=p3(g_ple_gate, m_g_ple_gate, v_g_ple_gate), g_ple_post=p3(g_ple_post, m_g_ple_post, v_g_ple_post),
        g_final=p3(row(g_final), row(m_g_final), row(v_g_final)), g_ffn=p3(g_ffn, m_g_ffn, v_g_ffn),
        ln_g=p3(ln_g, m_ln_g, v_ln_g), ln_b=p3(ln_b, m_ln_b, v_ln_b), conv_b=p3(conv_b, m_conv_b, v_conv_b),
        pool_scale=p3(pool_scale, m_pool_scale, v_pool_scale), pool_w=p3(pool_w[0], m_pool_w[0], v_pool_w[0]),
        conv_w=p3(conv_w[0], m_conv_w[0], v_conv_w[0]), g_mix=p3(g_mix, m_g_mix, v_g_mix))
    loss, small = _small_adam(place, list(small_all_0) + list(small_all_1), small_0 + small_1, params)
    back = dict(g_final=lambda a: a.reshape(D_MODEL), pool_w=lambda a: a[None], conv_w=lambda a: a[None])

    names = ["g_mix", "w_in", "conv_w", "conv_b", "ln_g", "ln_b", "pool_w", "pool_scale", "w_out", "g_ffn",
             "w_gate_up", "w_down", "g_ple_gate", "w_ple_gate", "w_ple_up", "g_ple_post", "g_final"]
    big_at = {"w_in": 0, "w_gate_up": 1, "w_out": 2, "w_down": 3, "w_ple_gate": 4, "w_ple_up": 5}
    out = [loss[0, 0], grad_x]
    for kind in range(4):
        for nm in names:
            if nm in big_at:
                out.append(big_upd[big_at[nm]][kind][None])
            else:
                out.append(back.get(nm, lambda a: a)(small[nm][kind]))
    return tuple(out)
```
